```python
import numpy as np
import jax
import jax.numpy as jnp
from jax import lax

D_MODEL = 1024
BATCH = 8
SEQ = 4096
DEPTH = 4

CHUNK = 64
MEM_LEN = 256
EPS = 1e-6
RET_HEADS = 4
RET_QK_DIM = 128
RET_V_DIM = 256
RET_QK = RET_HEADS * RET_QK_DIM
RET_V = RET_HEADS * RET_V_DIM
ROPE_THETA = 10000.0
SSM_INNER = 2 * D_MODEL
SSM_HEAD_DIM = 64
SSM_HEADS = SSM_INNER // SSM_HEAD_DIM
SSM_GROUPS = 8
SSM_HEADS_PER_GROUP = SSM_HEADS // SSM_GROUPS
SSM_STATE = 128
SSM_CONV = 4
SSM_BC = SSM_GROUPS * SSM_STATE
SSM_CONV_DIM = SSM_INNER + 2 * SSM_BC
N_BRANCH = 2
IN_SIZES = (RET_QK, RET_QK, RET_V, RET_V, SSM_INNER, SSM_CONV_DIM, SSM_HEADS, N_BRANCH * D_MODEL)
IN_DIM = RET_QK + RET_QK + RET_V + RET_V + SSM_INNER + SSM_CONV_DIM + SSM_HEADS + N_BRANCH * D_MODEL
XA_HEADS = 4
XA_HEAD_DIM = D_MODEL // XA_HEADS
D_FF = 4 * D_MODEL

kernel_name = 'hybrid_retention_ssd_xattn_trunk'


def _rms(x):
    xf = x.astype(jnp.float32)
    return (xf * lax.rsqrt(jnp.mean(xf * xf, axis=-1, keepdims=True) + EPS)).astype(x.dtype)


def _rmsnorm(x, w):
    return _rms(x) * w


def _rope(t, cos, sin):
    t1, t2 = jnp.split(t, 2, axis=-1)
    return jnp.concatenate([t1 * cos - t2 * sin, t1 * sin + t2 * cos], axis=-1)


def _causal_conv(x, w, bias):
    c = x.shape[-1]
    out = lax.conv_general_dilated(
        x, w[:, None, :].astype(x.dtype), window_strides=(1,), padding=[(SSM_CONV - 1, 0)],
        dimension_numbers=('NWC', 'WIO', 'NWC'), feature_group_count=c)
    return out + bias


def _chunk_scan(ret_q, ret_k, ret_v, ssm_x, ssm_dt, ssm_b, ssm_c, ssm_a):
    b, s = ret_q.shape[:2]
    n_chunks = s // CHUNK

    def to_chunks(t):
        t = t.astype(jnp.float32).reshape((b, n_chunks, CHUNK) + t.shape[2:])
        return jnp.moveaxis(t, 1, 0)

    idx = jnp.arange(CHUNK, dtype=jnp.float32)
    log_gamma = jnp.log1p(-(2.0 ** (-5.0 - jnp.arange(RET_HEADS, dtype=jnp.float32))))
    rel = jnp.abs(idx[:, None] - idx[None, :])
    ret_intra = jnp.exp(log_gamma[:, None, None] * rel)
    ret_q_decay = jnp.exp(log_gamma[None, :] * (idx[:, None] + 1.0))[..., None]
    ret_k_decay = jnp.exp(log_gamma[None, :] * (CHUNK - 1.0 - idx[:, None]))[..., None]
    ret_chunk_decay = jnp.exp(log_gamma * CHUNK)
    a = ssm_a.astype(jnp.float32)

    def step(carry, inp):
        s_ret, h_ssm = carry
        q, k, v, xs, dt, bm, cm = inp
        sc = jnp.einsum('blhd,bmhd->bhlm', q, k) * ret_intra
        y_ret = (jnp.einsum('bhlm,bmhe->blhe', sc, v)
                 + jnp.einsum('blhd,bhde->blhe', q * ret_q_decay, s_ret))
        s_ret = (s_ret * ret_chunk_decay[None, :, None, None]
                 + jnp.einsum('blhd,blhe->bhde', k * ret_k_decay, v))
        cum = jnp.cumsum(dt * a, axis=1)
        cum_h = jnp.moveaxis(cum, 1, -1)
        seg = jnp.exp(-jnp.abs(cum_h[..., :, None] - cum_h[..., None, :]))
        cb = jnp.einsum('blgn,bmgn->bglm', cm, bm)
        xdt = xs * dt[..., None]
        y_ssm = (jnp.einsum('bghlm,bmghp->blghp', cb[:, :, None] * seg, xdt)
                 + jnp.einsum('blgn,bghpn->blghp', cm, h_ssm) * jnp.exp(cum)[..., None])
        cum_last = cum[:, -1]
        h_ssm = (h_ssm * jnp.exp(cum_last)[..., None, None]
                 + jnp.einsum('bmgn,bmghp->bghpn', bm,
                              xdt * jnp.exp(cum_last[:, None] - cum)[..., None]))
        return (s_ret, h_ssm), (y_ret, y_ssm)

    init = (jnp.zeros((b, RET_HEADS, RET_QK_DIM, RET_V_DIM), jnp.float32),
            jnp.zeros((b, SSM_GROUPS, SSM_HEADS_PER_GROUP, SSM_HEAD_DIM, SSM_STATE), jnp.float32))
    xs_in = tuple(to_chunks(t) for t in (ret_q, ret_k, ret_v, ssm_x, ssm_dt, ssm_b, ssm_c))
    _, (y_ret, y_ssm) = lax.scan(step, init, xs_in)

    def from_chunks(t):
        t = jnp.moveaxis(t, 0, 1)
        return t.reshape((b, s) + t.shape[3:])

    return from_chunks(y_ret), from_chunks(y_ssm)


def _hybrid_mixer(u, cos, sin, w_in, b_gate, conv_w, conv_b, dt_bias, a_log, d_skip,
                  ssm_norm, w_br_ret, w_br_ssm, w_out):
    b, s, _ = u.shape
    proj = u @ w_in
    offsets = [int(o) for o in np.cumsum(IN_SIZES)[:-1]]
    q, k, v, g, z, xbc, dt, gates = jnp.split(proj, offsets, axis=-1)
    q = _rope(q.reshape(b, s, RET_HEADS, RET_QK_DIM), cos, sin)
    k = _rope(k.reshape(b, s, RET_HEADS, RET_QK_DIM), cos, sin) * (RET_QK_DIM ** -0.5)
    v = v.reshape(b, s, RET_HEADS, RET_V_DIM)
    xbc = jax.nn.silu(_causal_conv(xbc, conv_w, conv_b))
    xs, bm, cm = jnp.split(xbc, [SSM_INNER, SSM_INNER + SSM_BC], axis=-1)
    xs = xs.reshape(b, s, SSM_GROUPS, SSM_HEADS_PER_GROUP, SSM_HEAD_DIM)
    bm = bm.reshape(b, s, SSM_GROUPS, SSM_STATE)
    cm = cm.reshape(b, s, SSM_GROUPS, SSM_STATE)
    dt = jax.nn.softplus(dt.astype(jnp.float32) + dt_bias.astype(jnp.float32))
    dt = dt.reshape(b, s, SSM_GROUPS, SSM_HEADS_PER_GROUP)
    ssm_a = -jnp.exp(a_log.astype(jnp.float32)).reshape(SSM_GROUPS, SSM_HEADS_PER_GROUP)
    y_ret, y_ssm = _chunk_scan(q, k, v, xs, dt, bm, cm, ssm_a)
    y_ret = _rms(y_ret.astype(u.dtype)).reshape(b, s, RET_V) * jax.nn.silu(g)
    y_ssm = y_ssm.astype(u.dtype) + xs * d_skip.reshape(SSM_GROUPS, SSM_HEADS_PER_GROUP, 1)
    y_ssm = y_ssm.reshape(b, s, SSM_INNER) * jax.nn.silu(z)
    y_ssm = _rms(y_ssm.reshape(b, s, SSM_GROUPS, SSM_INNER // SSM_GROUPS)).reshape(b, s, SSM_INNER) * ssm_norm
    gate_ret, gate_ssm = jnp.split(jax.nn.sigmoid(gates + b_gate), N_BRANCH, axis=-1)
    merged = gate_ret * (y_ret @ w_br_ret) + gate_ssm * (y_ssm @ w_br_ssm)
    return merged @ w_out


def _cross_attention(h, mem_n, wq, wkv, wo):
    b, s, _ = h.shape
    m = mem_n.shape[1]
    q = (h @ wq).reshape(b, s, XA_HEADS, XA_HEAD_DIM)
    k, v = jnp.split(mem_n @ wkv, 2, axis=-1)
    k = k.reshape(b, m, XA_HEADS, XA_HEAD_DIM)
    v = v.reshape(b, m, XA_HEADS, XA_HEAD_DIM)
    scores = jnp.einsum('bshd,bmhd->bhsm', q, k).astype(jnp.float32) * (XA_HEAD_DIM ** -0.5)
    p = jax.nn.softmax(scores, axis=-1).astype(v.dtype)
    o = jnp.einsum('bhsm,bmhd->bshd', p, v).reshape(b, s, D_MODEL)
    return o @ wo


def _sq_relu_mlp(h, w1, w2):
    return jnp.square(jax.nn.relu(h @ w1)) @ w2


def _fwd_setup_inputs(seed: int = 0) -> dict:
    key = jax.random.key(seed)
    ks = jax.random.split(key, 32)

    def nrm(k, shape, scale):
        return jax.random.normal(k, shape, jnp.float32) * scale

    def gain(k, shape):
        return 1.0 + nrm(k, shape, 0.02)

    L = DEPTH
    x = nrm(ks[0], (BATCH, SEQ, D_MODEL), 1.0)
    mem = nrm(ks[1], (BATCH, MEM_LEN, D_MODEL), 1.0)
    offset = jax.random.randint(ks[2], (BATCH, 1), 0, 8192, dtype=jnp.int32)
    positions = offset + jnp.arange(SEQ, dtype=jnp.int32)[None, :]
    dt0 = jnp.exp(jax.random.uniform(ks[3], (L, SSM_HEADS), jnp.float32)
                  * (jnp.log(0.1) - jnp.log(0.001)) + jnp.log(0.001))
    dt_bias = dt0 + jnp.log(-jnp.expm1(-dt0))
    a_log = jnp.log(jax.random.uniform(ks[4], (L, SSM_HEADS), jnp.float32, 1.0, 16.0))
    return {
        'x': x,
        'mem': mem,
        'positions': positions,
        'norm_mix': gain(ks[5], (L, D_MODEL)),
        'w_in': nrm(ks[6], (L, D_MODEL, IN_DIM), D_MODEL ** -0.5),
        'b_gate': nrm(ks[7], (L, N_BRANCH * D_MODEL), 0.01),
        'conv_w': nrm(ks[8], (L, SSM_CONV, SSM_CONV_DIM), SSM_CONV ** -0.5),
        'conv_b': nrm(ks[9], (L, SSM_CONV_DIM), 0.01),
        'dt_bias': dt_bias,
        'a_log': a_log,
        'd_skip': gain(ks[10], (L, SSM_HEADS)),
        'ssm_norm': gain(ks[11], (L, SSM_INNER)),
        'w_br_ret': nrm(ks[12], (L, RET_V, D_MODEL), RET_V ** -0.5),
        'w_br_ssm': nrm(ks[13], (L, SSM_INNER, D_MODEL), SSM_INNER ** -0.5),
        'w_out': nrm(ks[14], (L, D_MODEL, D_MODEL), D_MODEL ** -0.5),
        'norm_xa': gain(ks[15], (L, D_MODEL)),
        'norm_mem': gain(ks[16], (L, D_MODEL)),
        'xa_wq': nrm(ks[17], (L, D_MODEL, D_MODEL), D_MODEL ** -0.5),
        'xa_wkv': nrm(ks[18], (L, D_MODEL, 2 * D_MODEL), D_MODEL ** -0.5),
        'xa_wo': nrm(ks[19], (L, D_MODEL, D_MODEL), D_MODEL ** -0.5),
        'norm_mlp': gain(ks[20], (L, D_MODEL)),
        'mlp_w1': nrm(ks[21], (L, D_MODEL, D_FF), D_MODEL ** -0.5),
        'mlp_w2': nrm(ks[22], (L, D_FF, D_MODEL), D_FF ** -0.5),
        'norm_final': gain(ks[23], (D_MODEL,)),
    }


def _fwd_reference(x, mem, positions, norm_mix, w_in, b_gate, conv_w, conv_b, dt_bias, a_log,
              d_skip, ssm_norm, w_br_ret, w_br_ssm, w_out, norm_xa, norm_mem, xa_wq, xa_wkv,
              xa_wo, norm_mlp, mlp_w1, mlp_w2, norm_final):
    inv_freq = ROPE_THETA ** (-jnp.arange(0, RET_QK_DIM, 2, dtype=jnp.float32) / RET_QK_DIM)
    ang = positions.astype(jnp.float32)[..., None] * inv_freq
    cos = jnp.cos(ang)[:, :, None, :].astype(x.dtype)
    sin = jnp.sin(ang)[:, :, None, :].astype(x.dtype)
    for l in range(DEPTH):
        x = x + _hybrid_mixer(_rmsnorm(x, norm_mix[l]), cos, sin, w_in[l], b_gate[l],
                              conv_w[l], conv_b[l], dt_bias[l], a_log[l], d_skip[l],
                              ssm_norm[l], w_br_ret[l], w_br_ssm[l], w_out[l])
        x = x + _cross_attention(_rmsnorm(x, norm_xa[l]), _rmsnorm(mem, norm_mem[l]),
                                 xa_wq[l], xa_wkv[l], xa_wo[l])
        x = x + _sq_relu_mlp(_rmsnorm(x, norm_mlp[l]), mlp_w1[l], mlp_w2[l])
    return _rmsnorm(x, norm_final)


import jax as _jax
import jax.numpy as _jnp

TWIN_FORMAT = 'train_step'
FWD_PARAMS = ['x', 'mem', 'positions', 'norm_mix', 'w_in', 'b_gate', 'conv_w', 'conv_b', 'dt_bias', 'a_log', 'd_skip', 'ssm_norm', 'w_br_ret', 'w_br_ssm', 'w_out', 'norm_xa', 'norm_mem', 'xa_wq', 'xa_wkv', 'xa_wo', 'norm_mlp', 'mlp_w1', 'mlp_w2', 'norm_final']
TWIN_WEIGHTS = ['norm_mix', 'w_in', 'b_gate', 'conv_w', 'conv_b', 'dt_bias', 'a_log', 'd_skip', 'ssm_norm', 'w_br_ret', 'w_br_ssm', 'w_out', 'norm_xa', 'norm_mem', 'xa_wq', 'xa_wkv', 'xa_wo', 'norm_mlp', 'mlp_w1', 'mlp_w2', 'norm_final']
TWIN_DIFF_INPUT = 'x'
TWIN_INPUTS = ['x', 'mem', 'positions', 'norm_mix', 'w_in', 'b_gate', 'conv_w', 'conv_b', 'dt_bias', 'a_log', 'd_skip', 'ssm_norm', 'w_br_ret', 'w_br_ssm', 'w_out', 'norm_xa', 'norm_mem', 'xa_wq', 'xa_wkv', 'xa_wo', 'norm_mlp', 'mlp_w1', 'mlp_w2', 'norm_final', 'loss_target', 'm_norm_mix', 'm_w_in', 'm_b_gate', 'm_conv_w', 'm_conv_b', 'm_dt_bias', 'm_a_log', 'm_d_skip', 'm_ssm_norm', 'm_w_br_ret', 'm_w_br_ssm', 'm_w_out', 'm_norm_xa', 'm_norm_mem', 'm_xa_wq', 'm_xa_wkv', 'm_xa_wo', 'm_norm_mlp', 'm_mlp_w1', 'm_mlp_w2', 'm_norm_final', 'v_norm_mix', 'v_w_in', 'v_b_gate', 'v_conv_w', 'v_conv_b', 'v_dt_bias', 'v_a_log', 'v_d_skip', 'v_ssm_norm', 'v_w_br_ret', 'v_w_br_ssm', 'v_w_out', 'v_norm_xa', 'v_norm_mem', 'v_xa_wq', 'v_xa_wkv', 'v_xa_wo', 'v_norm_mlp', 'v_mlp_w1', 'v_mlp_w2', 'v_norm_final']
TWIN_OUTPUTS = ['loss', 'grad_x', 'grad_norm_mix', 'grad_w_in', 'grad_b_gate', 'grad_conv_w', 'grad_conv_b', 'grad_dt_bias', 'grad_a_log', 'grad_d_skip', 'grad_ssm_norm', 'grad_w_br_ret', 'grad_w_br_ssm', 'grad_w_out', 'grad_norm_xa', 'grad_norm_mem', 'grad_xa_wq', 'grad_xa_wkv', 'grad_xa_wo', 'grad_norm_mlp', 'grad_mlp_w1', 'grad_mlp_w2', 'grad_norm_final', 'delta_norm_mix', 'delta_w_in', 'delta_b_gate', 'delta_conv_w', 'delta_conv_b', 'delta_dt_bias', 'delta_a_log', 'delta_d_skip', 'delta_ssm_norm', 'delta_w_br_ret', 'delta_w_br_ssm', 'delta_w_out', 'delta_norm_xa', 'delta_norm_mem', 'delta_xa_wq', 'delta_xa_wkv', 'delta_xa_wo', 'delta_norm_mlp', 'delta_mlp_w1', 'delta_mlp_w2', 'delta_norm_final', 'new_m_norm_mix', 'new_m_w_in', 'new_m_b_gate', 'new_m_conv_w', 'new_m_conv_b', 'new_m_dt_bias', 'new_m_a_log', 'new_m_d_skip', 'new_m_ssm_norm', 'new_m_w_br_ret', 'new_m_w_br_ssm', 'new_m_w_out', 'new_m_norm_xa', 'new_m_norm_mem', 'new_m_xa_wq', 'new_m_xa_wkv', 'new_m_xa_wo', 'new_m_norm_mlp', 'new_m_mlp_w1', 'new_m_mlp_w2', 'new_m_norm_final', 'new_v_norm_mix', 'new_v_w_in', 'new_v_b_gate', 'new_v_conv_w', 'new_v_conv_b', 'new_v_dt_bias', 'new_v_a_log', 'new_v_d_skip', 'new_v_ssm_norm', 'new_v_w_br_ret', 'new_v_w_br_ssm', 'new_v_w_out', 'new_v_norm_xa', 'new_v_norm_mem', 'new_v_xa_wq', 'new_v_xa_wkv', 'new_v_xa_wo', 'new_v_norm_mlp', 'new_v_mlp_w1', 'new_v_mlp_w2', 'new_v_norm_final']
TWIN_LEAF_KINDS = {'loss': 'loss', 'grad_x': 'grad_x', 'grad_norm_mix': 'grad_w', 'grad_w_in': 'grad_w', 'grad_b_gate': 'grad_w', 'grad_conv_w': 'grad_w', 'grad_conv_b': 'grad_w', 'grad_dt_bias': 'grad_w', 'grad_a_log': 'grad_w', 'grad_d_skip': 'grad_w', 'grad_ssm_norm': 'grad_w', 'grad_w_br_ret': 'grad_w', 'grad_w_br_ssm': 'grad_w', 'grad_w_out': 'grad_w', 'grad_norm_xa': 'grad_w', 'grad_norm_mem': 'grad_w', 'grad_xa_wq': 'grad_w', 'grad_xa_wkv': 'grad_w', 'grad_xa_wo': 'grad_w', 'grad_norm_mlp': 'grad_w', 'grad_mlp_w1': 'grad_w', 'grad_mlp_w2': 'grad_w', 'grad_norm_final': 'grad_w', 'delta_norm_mix': 'delta_w', 'delta_w_in': 'delta_w', 'delta_b_gate': 'delta_w', 'delta_conv_w': 'delta_w', 'delta_conv_b': 'delta_w', 'delta_dt_bias': 'delta_w', 'delta_a_log': 'delta_w', 'delta_d_skip': 'delta_w', 'delta_ssm_norm': 'delta_w', 'delta_w_br_ret': 'delta_w', 'delta_w_br_ssm': 'delta_w', 'delta_w_out': 'delta_w', 'delta_norm_xa': 'delta_w', 'delta_norm_mem': 'delta_w', 'delta_xa_wq': 'delta_w', 'delta_xa_wkv': 'delta_w', 'delta_xa_wo': 'delta_w', 'delta_norm_mlp': 'delta_w', 'delta_mlp_w1': 'delta_w', 'delta_mlp_w2': 'delta_w', 'delta_norm_final': 'delta_w', 'new_m_norm_mix': 'new_m', 'new_m_w_in': 'new_m', 'new_m_b_gate': 'new_m', 'new_m_conv_w': 'new_m', 'new_m_conv_b': 'new_m', 'new_m_dt_bias': 'new_m', 'new_m_a_log': 'new_m', 'new_m_d_skip': 'new_m', 'new_m_ssm_norm': 'new_m', 'new_m_w_br_ret': 'new_m', 'new_m_w_br_ssm': 'new_m', 'new_m_w_out': 'new_m', 'new_m_norm_xa': 'new_m', 'new_m_norm_mem': 'new_m', 'new_m_xa_wq': 'new_m', 'new_m_xa_wkv': 'new_m', 'new_m_xa_wo': 'new_m', 'new_m_norm_mlp': 'new_m', 'new_m_mlp_w1': 'new_m', 'new_m_mlp_w2': 'new_m', 'new_m_norm_final': 'new_m', 'new_v_norm_mix': 'new_v', 'new_v_w_in': 'new_v', 'new_v_b_gate': 'new_v', 'new_v_conv_w': 'new_v', 'new_v_conv_b': 'new_v', 'new_v_dt_bias': 'new_v', 'new_v_a_log': 'new_v', 'new_v_d_skip': 'new_v', 'new_v_ssm_norm': 'new_v', 'new_v_w_br_ret': 'new_v', 'new_v_w_br_ssm': 'new_v', 'new_v_w_out': 'new_v', 'new_v_norm_xa': 'new_v', 'new_v_norm_mem': 'new_v', 'new_v_xa_wq': 'new_v', 'new_v_xa_wkv': 'new_v', 'new_v_xa_wo': 'new_v', 'new_v_norm_mlp': 'new_v', 'new_v_mlp_w1': 'new_v', 'new_v_mlp_w2': 'new_v', 'new_v_norm_final': 'new_v'}


def _forward(args):
    return _fwd_reference(*[args[k] for k in FWD_PARAMS])


def _output_shape():
    out = _jax.eval_shape(lambda: _forward(_fwd_setup_inputs(0)))
    return out.shape, out.dtype

N_MICROBATCH = 1
ADAM_LR = 0.001
ADAM_B1 = 0.9
ADAM_B2 = 0.999
ADAM_EPS = 1e-08
ADAM_WD = 0.01
ADAM_STEP = 10
PER_EXAMPLE_BATCH_AXIS = {'x': 0, 'mem': 0, 'positions': 0, 'loss_target': 0}
SHARED_INPUTS = []
_WEIGHT_DTYPES = {'norm_mix': _jnp.float32, 'w_in': _jnp.float32, 'b_gate': _jnp.float32, 'conv_w': _jnp.float32, 'conv_b': _jnp.float32, 'dt_bias': _jnp.float32, 'a_log': _jnp.float32, 'd_skip': _jnp.float32, 'ssm_norm': _jnp.float32, 'w_br_ret': _jnp.float32, 'w_br_ssm': _jnp.float32, 'w_out': _jnp.float32, 'norm_xa': _jnp.float32, 'norm_mem': _jnp.float32, 'xa_wq': _jnp.float32, 'xa_wkv': _jnp.float32, 'xa_wo': _jnp.float32, 'norm_mlp': _jnp.float32, 'mlp_w1': _jnp.float32, 'mlp_w2': _jnp.float32, 'norm_final': _jnp.float32}
MOMENT_SCALE = {'norm_mix': 1.476647e-01, 'w_in': 4.418009e-02, 'b_gate': 2.513955e-02, 'conv_w': 4.005171e-02, 'conv_b': 5.995613e-02, 'dt_bias': 9.303327e-02, 'a_log': 1.666836e-01, 'd_skip': 3.859154e-01, 'ssm_norm': 5.429911e-02, 'w_br_ret': 4.402864e-02, 'w_br_ssm': 7.593883e-02, 'w_out': 8.767145e-02, 'norm_xa': 1.487564e-02, 'norm_mem': 2.169301e-02, 'xa_wq': 1.429354e-02, 'xa_wkv': 1.497800e-02, 'xa_wo': 1.561274e-02, 'norm_mlp': 1.427951e-01, 'mlp_w1': 6.975697e-02, 'mlp_w2': 1.346259e-01, 'norm_final': 3.272091e+01}


def _to_microbatches(a, axis):
    t = _jnp.moveaxis(a, axis, 0)
    t = t.reshape((N_MICROBATCH, t.shape[0] // N_MICROBATCH) + t.shape[1:])
    return _jnp.moveaxis(t, 1, axis + 1)


def setup_inputs(seed: int = 0) -> dict:
    inp = _fwd_setup_inputs(seed)
    key = _jax.random.fold_in(_jax.random.key(seed), 7919)
    shape, _ = _output_shape()
    out = dict(inp)
    out["loss_target"] = _jax.random.normal(_jax.random.fold_in(key, 0), shape, _jnp.float32)
    for i, name in enumerate(TWIN_WEIGHTS):
        w = inp[name].astype(_jnp.float32)
        if MOMENT_SCALE is None:
            s = _jnp.sqrt(_jnp.mean(_jnp.square(w)) + 1e-30)
        else:
            s = MOMENT_SCALE[name]
        km, kv = _jax.random.split(_jax.random.fold_in(key, i + 1))
        out[name] = w
        out["m_" + name] = s * _jax.random.normal(km, w.shape, _jnp.float32)
        out["v_" + name] = (s * s) * _jax.random.uniform(kv, w.shape, _jnp.float32, 0.5, 1.5)
    if N_MICROBATCH > 1:
        for name, axis in PER_EXAMPLE_BATCH_AXIS.items():
            out[name] = _to_microbatches(out[name], axis)
    return {'x': out['x'], 'mem': out['mem'], 'positions': out['positions'], 'norm_mix': out['norm_mix'], 'w_in': out['w_in'], 'b_gate': out['b_gate'], 'conv_w': out['conv_w'], 'conv_b': out['conv_b'], 'dt_bias': out['dt_bias'], 'a_log': out['a_log'], 'd_skip': out['d_skip'], 'ssm_norm': out['ssm_norm'], 'w_br_ret': out['w_br_ret'], 'w_br_ssm': out['w_br_ssm'], 'w_out': out['w_out'], 'norm_xa': out['norm_xa'], 'norm_mem': out['norm_mem'], 'xa_wq': out['xa_wq'], 'xa_wkv': out['xa_wkv'], 'xa_wo': out['xa_wo'], 'norm_mlp': out['norm_mlp'], 'mlp_w1': out['mlp_w1'], 'mlp_w2': out['mlp_w2'], 'norm_final': out['norm_final'], 'loss_target': out['loss_target'], 'm_norm_mix': out['m_norm_mix'], 'm_w_in': out['m_w_in'], 'm_b_gate': out['m_b_gate'], 'm_conv_w': out['m_conv_w'], 'm_conv_b': out['m_conv_b'], 'm_dt_bias': out['m_dt_bias'], 'm_a_log': out['m_a_log'], 'm_d_skip': out['m_d_skip'], 'm_ssm_norm': out['m_ssm_norm'], 'm_w_br_ret': out['m_w_br_ret'], 'm_w_br_ssm': out['m_w_br_ssm'], 'm_w_out': out['m_w_out'], 'm_norm_xa': out['m_norm_xa'], 'm_norm_mem': out['m_norm_mem'], 'm_xa_wq': out['m_xa_wq'], 'm_xa_wkv': out['m_xa_wkv'], 'm_xa_wo': out['m_xa_wo'], 'm_norm_mlp': out['m_norm_mlp'], 'm_mlp_w1': out['m_mlp_w1'], 'm_mlp_w2': out['m_mlp_w2'], 'm_norm_final': out['m_norm_final'], 'v_norm_mix': out['v_norm_mix'], 'v_w_in': out['v_w_in'], 'v_b_gate': out['v_b_gate'], 'v_conv_w': out['v_conv_w'], 'v_conv_b': out['v_conv_b'], 'v_dt_bias': out['v_dt_bias'], 'v_a_log': out['v_a_log'], 'v_d_skip': out['v_d_skip'], 'v_ssm_norm': out['v_ssm_norm'], 'v_w_br_ret': out['v_w_br_ret'], 'v_w_br_ssm': out['v_w_br_ssm'], 'v_w_out': out['v_w_out'], 'v_norm_xa': out['v_norm_xa'], 'v_norm_mem': out['v_norm_mem'], 'v_xa_wq': out['v_xa_wq'], 'v_xa_wkv': out['v_xa_wkv'], 'v_xa_wo': out['v_xa_wo'], 'v_norm_mlp': out['v_norm_mlp'], 'v_mlp_w1': out['v_mlp_w1'], 'v_mlp_w2': out['v_mlp_w2'], 'v_norm_final': out['v_norm_final']}


def _loss(weights, diff, rest, loss_target):
    with _jax.named_scope("forward"):
        args = {**rest, TWIN_DIFF_INPUT: diff, **{k: w.astype(_WEIGHT_DTYPES[k]) for k, w in weights.items()}}
        y = _forward(args)
    with _jax.named_scope("loss_head"):
        err = _jnp.square(y.astype(_jnp.float32) - loss_target)
        return 0.5 * _jnp.sum(_jnp.mean(err, axis=-1)) if err.ndim else 0.5 * err


def _adamw(w, g, m, v):
    m = ADAM_B1 * m + (1.0 - ADAM_B1) * g
    v = ADAM_B2 * v + (1.0 - ADAM_B2) * _jnp.square(g)
    m_hat = m / (1.0 - ADAM_B1 ** ADAM_STEP)
    v_hat = v / (1.0 - ADAM_B2 ** ADAM_STEP)
    delta = -ADAM_LR * (m_hat / (_jnp.sqrt(v_hat) + ADAM_EPS) + ADAM_WD * w)
    return delta, m, v


def reference(x, mem, positions, norm_mix, w_in, b_gate, conv_w, conv_b, dt_bias, a_log, d_skip, ssm_norm, w_br_ret, w_br_ssm, w_out, norm_xa, norm_mem, xa_wq, xa_wkv, xa_wo, norm_mlp, mlp_w1, mlp_w2, norm_final, loss_target, m_norm_mix, m_w_in, m_b_gate, m_conv_w, m_conv_b, m_dt_bias, m_a_log, m_d_skip, m_ssm_norm, m_w_br_ret, m_w_br_ssm, m_w_out, m_norm_xa, m_norm_mem, m_xa_wq, m_xa_wkv, m_xa_wo, m_norm_mlp, m_mlp_w1, m_mlp_w2, m_norm_final, v_norm_mix, v_w_in, v_b_gate, v_conv_w, v_conv_b, v_dt_bias, v_a_log, v_d_skip, v_ssm_norm, v_w_br_ret, v_w_br_ssm, v_w_out, v_norm_xa, v_norm_mem, v_xa_wq, v_xa_wkv, v_xa_wo, v_norm_mlp, v_mlp_w1, v_mlp_w2, v_norm_final):
    given = dict(x=x, mem=mem, positions=positions, norm_mix=norm_mix, w_in=w_in, b_gate=b_gate, conv_w=conv_w, conv_b=conv_b, dt_bias=dt_bias, a_log=a_log, d_skip=d_skip, ssm_norm=ssm_norm, w_br_ret=w_br_ret, w_br_ssm=w_br_ssm, w_out=w_out, norm_xa=norm_xa, norm_mem=norm_mem, xa_wq=xa_wq, xa_wkv=xa_wkv, xa_wo=xa_wo, norm_mlp=norm_mlp, mlp_w1=mlp_w1, mlp_w2=mlp_w2, norm_final=norm_final, loss_target=loss_target, m_norm_mix=m_norm_mix, m_w_in=m_w_in, m_b_gate=m_b_gate, m_conv_w=m_conv_w, m_conv_b=m_conv_b, m_dt_bias=m_dt_bias, m_a_log=m_a_log, m_d_skip=m_d_skip, m_ssm_norm=m_ssm_norm, m_w_br_ret=m_w_br_ret, m_w_br_ssm=m_w_br_ssm, m_w_out=m_w_out, m_norm_xa=m_norm_xa, m_norm_mem=m_norm_mem, m_xa_wq=m_xa_wq, m_xa_wkv=m_xa_wkv, m_xa_wo=m_xa_wo, m_norm_mlp=m_norm_mlp, m_mlp_w1=m_mlp_w1, m_mlp_w2=m_mlp_w2, m_norm_final=m_norm_final, v_norm_mix=v_norm_mix, v_w_in=v_w_in, v_b_gate=v_b_gate, v_conv_w=v_conv_w, v_conv_b=v_conv_b, v_dt_bias=v_dt_bias, v_a_log=v_a_log, v_d_skip=v_d_skip, v_ssm_norm=v_ssm_norm, v_w_br_ret=v_w_br_ret, v_w_br_ssm=v_w_br_ssm, v_w_out=v_w_out, v_norm_xa=v_norm_xa, v_norm_mem=v_norm_mem, v_xa_wq=v_xa_wq, v_xa_wkv=v_xa_wkv, v_xa_wo=v_xa_wo, v_norm_mlp=v_norm_mlp, v_mlp_w1=v_mlp_w1, v_mlp_w2=v_mlp_w2, v_norm_final=v_norm_final)
    weights = {n: given[n] for n in TWIN_WEIGHTS}
    shared = {n: given[n] for n in SHARED_INPUTS}
    per_example = {n: given[n] for n in ['x', 'mem', 'positions']}
    grad_fn = _jax.value_and_grad(_loss, argnums=(0, 1))

    def one_microbatch(ex, loss_target):
        ex = dict(ex)
        diff = ex.pop(TWIN_DIFF_INPUT)
        return grad_fn(weights, diff, {**shared, **ex}, loss_target)

    if N_MICROBATCH == 1:
        loss, (grad_w, grad_x) = one_microbatch(per_example, given["loss_target"])
    else:
        def body(carry, xs):
            loss_sum, grad_sum = carry
            l_k, (gw_k, gx_k) = one_microbatch(xs[0], xs[1])
            with _jax.named_scope("update"):
                return (loss_sum + l_k, _jax.tree.map(_jnp.add, grad_sum, gw_k)), gx_k

        init = (_jnp.zeros((), _jnp.float32), _jax.tree.map(_jnp.zeros_like, weights))
        (loss, grad_w), grad_x = _jax.lax.scan(body, init, (per_example, given["loss_target"]))
    with _jax.named_scope("update"):
        delta_w, new_m, new_v = {}, {}, {}
        for n in TWIN_WEIGHTS:
            delta_w[n], new_m[n], new_v[n] = _adamw(weights[n], grad_w[n], given["m_" + n], given["v_" + n])
    return (loss, grad_x, *[grad_w[n] for n in TWIN_WEIGHTS], *[delta_w[n] for n in TWIN_WEIGHTS],
            *[new_m[n] for n in TWIN_WEIGHTS], *[new_v[n] for n in TWIN_WEIGHTS])
```

```python
import numpy as np
import jax
import jax.numpy as jnp
from jax import lax
from jax.experimental import pallas as pl
from jax.experimental.pallas import tpu as pltpu

F32 = jnp.float32
BF16 = jnp.bfloat16
MESH = pl.DeviceIdType.MESH

D_MODEL = 1024
CHUNK = 64
EPS = 1e-6
RET_HEADS = 4
RET_QK_DIM = 128
RET_V_DIM = 256
RET_QK = 512
RET_V = 1024
ROPE_THETA = 10000.0
SSM_INNER = 2048
SSM_HEADS = 32
SSM_GROUPS = 8
SSM_STATE = 128
SSM_CONV = 4
SSM_BC = 1024
SSM_CONV_DIM = 4096
XA_HEADS = 4
XA_HEAD_DIM = 256
D_FF = 4096
GROUP_W = 256

DT_PAD = 512
IN_DIM = 11296
NP = 11776
C_XBC, C_Z, C_GATES, C_Q, C_K, C_V, C_G, C_DT = 0, 4096, 6144, 8192, 8704, 9216, 10240, 11264
O_Q, O_K, O_V, O_G, O_Z, O_XBC, O_DT, O_GATES = 0, 512, 1024, 2048, 3072, 5120, 9216, 9248

ADAM_LR = 0.001
ADAM_B1 = 0.9
ADAM_B2 = 0.999
ADAM_EPS = 1e-08
ADAM_WD = 0.01
ADAM_STEP = 10

VMEM_LIMIT = 56 * 1024 * 1024


def _params(*sem):
    return pltpu.CompilerParams(dimension_semantics=sem, vmem_limit_bytes=VMEM_LIMIT)


def _pcall(body, **kw):
    return pl.pallas_call(body, **kw)


def _bf(a):
    return a.astype(BF16)


def _dot(a, b):
    return jnp.dot(_bf(a), _bf(b), preferred_element_type=F32)


def _dot_nt(a, b):
    return lax.dot_general(_bf(a), _bf(b), (((1,), (1,)), ((), ())), preferred_element_type=F32)


def _dot_tn(a, b):
    return lax.dot_general(_bf(a), _bf(b), (((0,), (0,)), ((), ())), preferred_element_type=F32)


def _colsum(a):
    return jnp.sum(a, axis=0, keepdims=True)


def _rstd(x):
    return lax.rsqrt(jnp.mean(x * x, axis=-1, keepdims=True) + EPS)


def _rms_bwd(dy, x, rstd):
    xh = x * rstd
    return rstd * (dy - xh * jnp.mean(dy * xh, axis=-1, keepdims=True))


def _sigmoid(x):
    return 1.0 / (1.0 + jnp.exp(-x))


def _silu_and_grad(x):
    s = _sigmoid(x)
    return x * s, s + x * s * (1.0 - s)


def _softplus(x):
    u = jnp.exp(-jnp.abs(x))
    l1p = jnp.where(u < 1e-4, u * (1.0 - 0.5 * u), jnp.log(1.0 + u))
    return jnp.maximum(x, 0.0) + l1p


def _split3_dot(a, e):
    hi = a.astype(BF16)
    r1 = a - hi.astype(F32)
    mid = r1.astype(BF16)
    lo = (r1 - mid.astype(F32)).astype(BF16)
    return (jnp.dot(hi, e, preferred_element_type=F32) + jnp.dot(mid, e, preferred_element_type=F32)
            + jnp.dot(lo, e, preferred_element_type=F32))


def _cumsum_rows(a):
    rows = lax.broadcasted_iota(jnp.int32, a.shape, 0)
    s = 1
    while s < a.shape[0]:
        a = a + jnp.where(rows >= s, pltpu.roll(a, s, 0), 0.0)
        s *= 2
    return a


def _revcumsum_rows(a):
    n = a.shape[0]
    rows = lax.broadcasted_iota(jnp.int32, a.shape, 0)
    s = 1
    while s < n:
        a = a + jnp.where(rows < n - s, pltpu.roll(a, n - s, 0), 0.0)
        s *= 2
    return a


def _rms_groups(y, width):
    out = []
    for h in range(y.shape[1] // width):
        slab = y[:, h * width:(h + 1) * width]
        out.append((slab, _rstd(slab)))
    return out


def _ret_constants():
    idx = np.arange(CHUNK, dtype=np.float32)
    lg = np.log1p(-(np.float32(2.0) ** (np.float32(-5.0) - np.arange(RET_HEADS, dtype=np.float32)))).astype(np.float32)
    rel = np.abs(idx[:, None] - idx[None, :])
    dm = np.exp(lg[:, None, None] * rel).astype(np.float32)
    qd = np.exp(lg[None, :] * (idx[:, None] + 1.0)).astype(np.float32)
    kd = np.exp(lg[None, :] * (CHUNK - 1.0 - idx[:, None])).astype(np.float32)
    cd = np.exp(lg * CHUNK).astype(np.float32)
    qd = np.repeat(qd, RET_QK_DIM, axis=1)
    kd = np.repeat(kd, RET_QK_DIM, axis=1)
    cd = np.repeat(cd, RET_QK_DIM)[:, None] * np.ones((1, RET_V_DIM), np.float32)
    return dm, qd, kd, cd.astype(np.float32)


def _ssd_constants():
    eye = np.tile(np.eye(CHUNK, dtype=np.float32), (1, GROUP_W // CHUNK))
    blk = np.kron(np.eye(GROUP_W // CHUNK, dtype=np.float32), np.ones((CHUNK, CHUNK), np.float32))
    return eye, blk


def _head_expand():
    e = np.zeros((128, SSM_INNER), np.float32)
    for h in range(SSM_HEADS):
        e[h, h * 64:(h + 1) * 64] = 1.0
    return e


def _ret_chunk_fwd(qh, kh, vh, sh, dmh, qdh, kdh, cdh):
    a = _dot_nt(qh, kh) * dmh
    y = _dot(a, vh) + _dot(qh * qdh, sh)
    s_new = sh * cdh + _dot_tn(kh * kdh, vh)
    return y, s_new


def _ret_chunk_bwd(qh, kh, vh, sh, dmh, qdh, kdh, cdh, dy, ds_new):
    a = _dot_nt(qh, kh) * dmh
    dp = _dot_nt(dy, vh) * dmh
    dq = _dot(dp, kh) + _dot_nt(dy, sh) * qdh
    dk = _dot_tn(dp, qh) + _dot_nt(vh, ds_new) * kdh
    dv = _dot_tn(a, dy) + _dot(kh * kdh, ds_new)
    ds = cdh * ds_new + _dot_tn(qh * qdh, dy)
    return dq, dk, dv, ds


def _ssd_common(xs, dtx, ax, eye):
    cum = _cumsum_rows(dtx * ax)
    last = cum[CHUNK - 1:CHUNK, :]
    r = _colsum(jnp.where(eye > 0.5, cum, 0.0))
    return cum, last, r, xs * dtx


def _tile4(a):
    return jnp.concatenate([a, a, a, a], axis=0)


def _ssd_chunk_fwd(xs, dtx, b, c, ax, hg, eye, blk):
    cum, last, r, x = _ssd_common(xs, dtx, ax, eye)
    lam = jnp.exp(-jnp.abs(cum - r))
    wc = _dot_nt(c, _tile4(b)) * lam
    bd = _tile4(x) * blk
    y = _dot(wc, bd) + _dot(c, hg) * jnp.exp(cum)
    h_new = hg * jnp.exp(last) + _dot_tn(b, x * jnp.exp(last - cum))
    return y, h_new


def _ssd_chunk_bwd(xs, dtx, b, c, ax, hg, eye, blk, dy, dh_new):
    cum, last, r, x = _ssd_common(xs, dtx, ax, eye)
    delta = cum - r
    lam = jnp.exp(-jnp.abs(delta))
    b4 = _tile4(b)
    cb4 = _dot_nt(c, b4)
    wc = cb4 * lam
    bd = _tile4(x) * blk
    ecx = jnp.exp(cum)
    wl = jnp.exp(last - cum)
    ecl = jnp.exp(last)
    z = _dot(c, hg)
    dwc = _dot_nt(dy, bd)
    dbd = _dot_tn(wc, dy) * blk
    dx = dbd[0:64] + dbd[64:128] + dbd[128:192] + dbd[192:256]
    dt_ = _dot(b, dh_new)
    dx = dx + dt_ * wl
    dcb4 = dwc * lam
    dz = dy * ecx
    dc = _dot(dcb4, b4) + _dot_nt(dz, hg)
    db4 = _dot_tn(dcb4, c)
    db = db4[0:64] + db4[64:128] + db4[128:192] + db4[192:256] + _dot_nt(x * wl, dh_new)
    g = dwc * cb4 * lam * (-jnp.sign(delta))
    dr = -_colsum(g)
    dwl = dt_ * x * wl
    u = g + eye * dr + dy * z * ecx - dwl
    lastrow = _colsum(dwl) + _colsum(dh_new * hg) * ecl
    rows = lax.broadcasted_iota(jnp.int32, u.shape, 0)
    u = u + jnp.where(rows == CHUNK - 1, lastrow, 0.0)
    dh = _dot_tn(c, dz) + dh_new * ecl
    rc = _revcumsum_rows(u)
    dxs = dx * dtx
    g_dtx = dx * xs + rc * ax
    da = _colsum(rc * dtx)
    return dxs, g_dtx, db, dc, da, dh


def _row_tile(s, want):
    t = min(s, want)
    assert s % t == 0
    return t


def _nmm(x, gain, w, name, tn=512, save_u=False):
    s, d = x.shape
    n = w.shape[1]
    tm = _row_tile(s, 512)
    assert n % tn == 0

    def body(x_ref, g_ref, w_ref, *rest):
        o_ref, u_sc = rest[0], rest[-1]

        @pl.when(pl.program_id(1) == 0)
        def _():
            xx = x_ref[...]
            u = _bf((xx * _rstd(xx)) * g_ref[...])
            u_sc[...] = u
            if save_u:
                rest[1][...] = u

        o_ref[...] = jnp.dot(u_sc[...], w_ref[...], preferred_element_type=F32)

    out_shape = [jax.ShapeDtypeStruct((s, n), F32)]
    out_specs = [pl.BlockSpec((tm, tn), lambda i, j: (i, j))]
    if save_u:
        out_shape.append(jax.ShapeDtypeStruct((s, d), BF16))
        out_specs.append(pl.BlockSpec((tm, d), lambda i, j: (i, 0)))
    res = _pcall(
        body, grid=(s // tm, n // tn),
        in_specs=[pl.BlockSpec((tm, d), lambda i, j: (i, 0)), pl.BlockSpec((1, d), lambda i, j: (0, 0)),
                  pl.BlockSpec((d, tn), lambda i, j: (0, j))],
        out_specs=out_specs, out_shape=out_shape, scratch_shapes=[pltpu.VMEM((tm, d), BF16)],
        compiler_params=_params("parallel", "arbitrary"), name=name)(x, gain, w)
    return res if save_u else res[0]


def _mm_tn(a, b, name, tm=512, tn=512):
    k, m = a.shape
    n = b.shape[1]
    tk = _row_tile(k, 512)
    tm, tn = min(tm, m), min(tn, n)
    assert m % tm == 0 and n % tn == 0
    nk = k // tk

    def body(a_ref, b_ref, o_ref, acc):
        kk = pl.program_id(2)

        @pl.when(kk == 0)
        def _():
            acc[...] = jnp.zeros_like(acc)

        acc[...] += _dot_tn(a_ref[...], b_ref[...])

        @pl.when(kk == nk - 1)
        def _():
            o_ref[...] = acc[...]

    return _pcall(
        body, grid=(m // tm, n // tn, nk),
        in_specs=[pl.BlockSpec((tk, tm), lambda i, j, kk: (kk, i)), pl.BlockSpec((tk, tn), lambda i, j, kk: (kk, j))],
        out_specs=pl.BlockSpec((tm, tn), lambda i, j, kk: (i, j)), out_shape=jax.ShapeDtypeStruct((m, n), F32),
        scratch_shapes=[pltpu.VMEM((tm, tn), F32)],
        compiler_params=_params("parallel", "parallel", "arbitrary"), name=name)(a, b)


def _in_bwd(dproj, wcat, x, gain, dres, name):
    s, n = dproj.shape
    d = wcat.shape[0]
    tm = _row_tile(s, 512)
    tk = 512
    nk = n // tk
    ns = s // tm

    def body(dp_ref, w_ref, x_ref, g_ref, dr_ref, dx_ref, dg_ref, acc):
        i, kk = pl.program_id(0), pl.program_id(1)

        @pl.when(kk == 0)
        def _():
            acc[...] = jnp.zeros_like(acc)

        @pl.when((kk == 0) & (i == 0))
        def _():
            dg_ref[...] = jnp.zeros_like(dg_ref)

        acc[...] += _dot_nt(dp_ref[...], w_ref[...])

        @pl.when(kk == nk - 1)
        def _():
            xx = x_ref[...]
            r = _rstd(xx)
            du = acc[...]
            dg_ref[...] += _colsum(du * (xx * r))
            dx_ref[...] = dr_ref[...] + _rms_bwd(du * g_ref[...], xx, r)

    return _pcall(
        body, grid=(ns, nk),
        in_specs=[pl.BlockSpec((tm, tk), lambda i, kk: (i, kk)), pl.BlockSpec((d, tk), lambda i, kk: (0, kk)),
                  pl.BlockSpec((tm, d), lambda i, kk: (i, 0)), pl.BlockSpec((1, d), lambda i, kk: (0, 0)),
                  pl.BlockSpec((tm, d), lambda i, kk: (i, 0))],
        out_specs=[pl.BlockSpec((tm, d), lambda i, kk: (i, 0)), pl.BlockSpec((8, d), lambda i, kk: (0, 0))],
        out_shape=[jax.ShapeDtypeStruct((s, d), F32), jax.ShapeDtypeStruct((8, d), F32)],
        scratch_shapes=[pltpu.VMEM((tm, d), F32)],
        compiler_params=_params("arbitrary", "arbitrary"), name=name)(dproj, wcat, x, gain, dres)


def _conv_pre(xcat, cw_ref, cb_ref, ts):
    pre = cb_ref[...] + cw_ref[3:4, :] * xcat[8:8 + ts]
    for j in range(3):
        pre = pre + cw_ref[j:j + 1, :] * pltpu.roll(xcat, 3 - j, 0)[8:8 + ts]
    return pre


def _prev_rows_spec(ts, width):
    return pl.BlockSpec((8, width), lambda i: (jnp.maximum(i * (ts // 8) - 1, 0), 0))


def _prescan(proj, cosf, sinf, cw, cb, dtb, eexp, name):
    s = proj.shape[0]
    ts = _row_tile(s, 256)

    def body(xbc_ref, prev_ref, q_ref, k_ref, dt_ref, cos_ref, sin_ref, cw_ref, cb_ref, dtb_ref, e_ref,
             qo_ref, ko_ref, xc_ref, dtx_ref):
        i = pl.program_id(0)
        prev = jnp.where(i > 0, prev_ref[...], 0.0)
        xcat = jnp.concatenate([prev, xbc_ref[...]], axis=0)
        pre = _conv_pre(xcat, cw_ref, cb_ref, ts)
        xc_ref[...] = pre * _sigmoid(pre)
        cs, sn = cos_ref[...], sin_ref[...]
        for h in range(RET_HEADS):
            sl = slice(h * 128, (h + 1) * 128)
            qh, kh = q_ref[:, sl], k_ref[:, sl]
            qo_ref[:, sl] = qh * cs + pltpu.roll(qh, 64, 1) * sn
            ko_ref[:, sl] = (kh * cs + pltpu.roll(kh, 64, 1) * sn) * (RET_QK_DIM ** -0.5)
        dtv = _softplus(dt_ref[:, 0:128] + dtb_ref[...])
        dtx_ref[...] = _split3_dot(dtv, e_ref[...])

    row = lambda w, c: pl.BlockSpec((ts, w), lambda i: (i, c))
    full = lambda a: pl.BlockSpec(a.shape, lambda i: (0,) * a.ndim)
    return _pcall(
        body, grid=(s // ts,),
        in_specs=[row(4096, 0), _prev_rows_spec(ts, 4096), row(512, C_Q // 512), row(512, C_K // 512),
                  row(DT_PAD, C_DT // DT_PAD), row(128, 0), row(128, 0), full(cw), full(cb), full(dtb), full(eexp)],
        out_specs=[row(512, 0), row(512, 0), row(4096, 0), row(2048, 0)],
        out_shape=[jax.ShapeDtypeStruct((s, 512), F32), jax.ShapeDtypeStruct((s, 512), F32),
                   jax.ShapeDtypeStruct((s, 4096), F32), jax.ShapeDtypeStruct((s, 2048), F32)],
        compiler_params=_params("parallel"), name=name)(proj, proj, proj, proj, proj, cosf, sinf, cw, cb, dtb, eexp)


def _scan_fwd(qr, kr, proj, xc, dtx, ax, consts, name):
    s = qr.shape[0]
    nc = s // CHUNK
    dm, qd, kd, cd, eye, blk = consts

    def body(q_ref, k_ref, v_ref, xc_ref, dtx_ref, ax_ref, dm_ref, qd_ref, kd_ref, cd_ref, eye_ref, blk_ref,
             yr_ref, ys_ref, sst_ref, hst_ref, s_sc, h_sc):
        @pl.when(pl.program_id(0) == 0)
        def _():
            s_sc[...] = jnp.zeros_like(s_sc)
            h_sc[...] = jnp.zeros_like(h_sc)

        sst_ref[0] = s_sc[...]
        hst_ref[0] = h_sc[...]
        for h in range(RET_HEADS):
            ql, vl = slice(h * 128, (h + 1) * 128), slice(h * 256, (h + 1) * 256)
            y, s_new = _ret_chunk_fwd(q_ref[:, ql], k_ref[:, ql], v_ref[:, vl], s_sc[ql, :], dm_ref[h],
                                      qd_ref[:, ql], kd_ref[:, ql], cd_ref[ql, :])
            yr_ref[:, vl] = y
            s_sc[ql, :] = s_new
        eye_v, blk_v = eye_ref[...], blk_ref[...]
        for g in range(SSM_GROUPS):
            sl = slice(g * GROUP_W, (g + 1) * GROUP_W)
            bl = slice(SSM_INNER + g * 128, SSM_INNER + (g + 1) * 128)
            cl = slice(SSM_INNER + SSM_BC + g * 128, SSM_INNER + SSM_BC + (g + 1) * 128)
            y, h_new = _ssd_chunk_fwd(xc_ref[:, sl], dtx_ref[:, sl], xc_ref[:, bl], xc_ref[:, cl], ax_ref[:, sl],
                                      h_sc[:, sl], eye_v, blk_v)
            ys_ref[:, sl] = y
            h_sc[:, sl] = h_new

    row = lambda w, c=0: pl.BlockSpec((CHUNK, w), lambda i: (i, c))
    full = lambda a: pl.BlockSpec(a.shape, lambda i: (0,) * a.ndim)
    return _pcall(
        body, grid=(nc,),
        in_specs=[row(512), row(512), row(1024, C_V // 1024), row(4096), row(2048), full(ax), full(dm), full(qd),
                  full(kd), full(cd), full(eye), full(blk)],
        out_specs=[row(1024), row(2048), pl.BlockSpec((1, 512, 256), lambda i: (i, 0, 0)),
                   pl.BlockSpec((1, 128, 2048), lambda i: (i, 0, 0))],
        out_shape=[jax.ShapeDtypeStruct((s, 1024), F32), jax.ShapeDtypeStruct((s, 2048), F32),
                   jax.ShapeDtypeStruct((nc, 512, 256), F32), jax.ShapeDtypeStruct((nc, 128, 2048), F32)],
        scratch_shapes=[pltpu.VMEM((512, 256), F32), pltpu.VMEM((128, 2048), F32)],
        compiler_params=_params("arbitrary"), name=name)(qr, kr, proj, xc, dtx, ax, dm, qd, kd, cd, eye, blk)


def _scan_bwd(qr, kr, proj, xc, dtx, ax, consts, sst, hst, dyr, dys, name):
    s = qr.shape[0]
    nc = s // CHUNK
    dm, qd, kd, cd, eye, blk = consts

    def body(q_ref, k_ref, v_ref, xc_ref, dtx_ref, ax_ref, dm_ref, qd_ref, kd_ref, cd_ref, eye_ref, blk_ref,
             sst_ref, hst_ref, dyr_ref, dys_ref, dq_ref, dk_ref, dv_ref, dxc_ref, gdt_ref, da_ref, ds_sc, dh_sc):
        @pl.when(pl.program_id(0) == 0)
        def _():
            ds_sc[...] = jnp.zeros_like(ds_sc)
            dh_sc[...] = jnp.zeros_like(dh_sc)
            da_ref[...] = jnp.zeros_like(da_ref)

        for h in range(RET_HEADS):
            ql, vl = slice(h * 128, (h + 1) * 128), slice(h * 256, (h + 1) * 256)
            dq, dk, dv, ds = _ret_chunk_bwd(q_ref[:, ql], k_ref[:, ql], v_ref[:, vl], sst_ref[0, ql, :], dm_ref[h],
                                            qd_ref[:, ql], kd_ref[:, ql], cd_ref[ql, :], dyr_ref[:, vl], ds_sc[ql, :])
            dq_ref[:, ql] = dq
            dk_ref[:, ql] = dk
            dv_ref[:, vl] = _bf(dv)
            ds_sc[ql, :] = ds
        eye_v, blk_v = eye_ref[...], blk_ref[...]
        for g in range(SSM_GROUPS):
            sl = slice(g * GROUP_W, (g + 1) * GROUP_W)
            bl = slice(SSM_INNER + g * 128, SSM_INNER + (g + 1) * 128)
            cl = slice(SSM_INNER + SSM_BC + g * 128, SSM_INNER + SSM_BC + (g + 1) * 128)
            dxs, g_dtx, db, dc, da, dh = _ssd_chunk_bwd(
                xc_ref[:, sl], dtx_ref[:, sl], xc_ref[:, bl], xc_ref[:, cl], ax_ref[:, sl], hst_ref[0, :, sl],
                eye_v, blk_v, dys_ref[:, sl], dh_sc[:, sl])
            dxc_ref[:, sl] = dxs
            dxc_ref[:, bl] = db
            dxc_ref[:, cl] = dc
            gdt_ref[:, sl] = g_dtx
            da_ref[:, sl] += da
            dh_sc[:, sl] = dh

    row = lambda w, c=0: pl.BlockSpec((CHUNK, w), lambda i: (nc - 1 - i, c))
    full = lambda a: pl.BlockSpec(a.shape, lambda i: (0,) * a.ndim)
    return _pcall(
        body, grid=(nc,),
        in_specs=[row(512), row(512), row(1024, C_V // 1024), row(4096), row(2048), full(ax), full(dm), full(qd),
                  full(kd), full(cd), full(eye), full(blk),
                  pl.BlockSpec((1, 512, 256), lambda i: (nc - 1 - i, 0, 0)),
                  pl.BlockSpec((1, 128, 2048), lambda i: (nc - 1 - i, 0, 0)), row(1024), row(2048)],
        out_specs=[row(512), row(512), row(1024), row(4096), row(2048), pl.BlockSpec((1, 2048), lambda i: (0, 0))],
        out_shape=[jax.ShapeDtypeStruct((s, 512), F32), jax.ShapeDtypeStruct((s, 512), F32),
                   jax.ShapeDtypeStruct((s, 1024), BF16), jax.ShapeDtypeStruct((s, 4096), F32),
                   jax.ShapeDtypeStruct((s, 2048), F32), jax.ShapeDtypeStruct((1, 2048), F32)],
        scratch_shapes=[pltpu.VMEM((512, 256), F32), pltpu.VMEM((128, 2048), F32)],
        compiler_params=_params("arbitrary"), name=name)(qr, kr, proj, xc, dtx, ax, dm, qd, kd, cd, eye, blk, sst, hst,
                                                          dyr, dys)


def _mix_values(yr, g, ys, xs, z, gates, bg, dsk, sn):
    sg, dsg = _silu_and_grad(g)
    ret = _rms_groups(yr, RET_V_DIM)
    yrn = jnp.concatenate([slab * r for slab, r in ret], axis=1) * sg
    sz, dsz = _silu_and_grad(z)
    ys0 = ys + xs * dsk
    ys1 = ys0 * sz
    grp = _rms_groups(ys1, GROUP_W)
    ysh = jnp.concatenate([slab * r for slab, r in grp], axis=1)
    ysn = ysh * sn
    gg = _sigmoid(gates + bg)
    return dict(sg=sg, dsg=dsg, ret=ret, yrn=yrn, sz=sz, dsz=dsz, ys0=ys0, ys1=ys1, grp=grp, ysh=ysh, ysn=ysn,
                gr=gg[:, :D_MODEL], gs=gg[:, D_MODEL:])


def _postscan_fwd(x, yr, ys, xc, proj, bg, dsk, sn, wr, ws, wo, name):
    s = x.shape[0]
    ts = _row_tile(s, 256)

    def body(x_ref, yr_ref, ys_ref, xs_ref, g_ref, z_ref, gt_ref, bg_ref, dsk_ref, sn_ref, wr_ref, ws_ref, wo_ref,
             o_ref):
        m = _mix_values(yr_ref[...], g_ref[...], ys_ref[...], xs_ref[...], z_ref[...], gt_ref[...], bg_ref[...],
                        dsk_ref[...], sn_ref[...])
        merged = m["gr"] * _dot(m["yrn"], wr_ref[...]) + m["gs"] * _dot(m["ysn"], ws_ref[...])
        o_ref[...] = x_ref[...] + _dot(merged, wo_ref[...])

    row = lambda w, c=0: pl.BlockSpec((ts, w), lambda i: (i, c))
    full = lambda a: pl.BlockSpec(a.shape, lambda i: (0,) * a.ndim)
    return _pcall(
        body, grid=(s // ts,),
        in_specs=[row(1024), row(1024), row(2048), row(2048), row(1024, C_G // 1024), row(2048, C_Z // 2048),
                  row(2048, C_GATES // 2048), full(bg), full(dsk), full(sn), full(wr), full(ws), full(wo)],
        out_specs=row(1024), out_shape=jax.ShapeDtypeStruct((s, D_MODEL), F32),
        compiler_params=_params("parallel"), name=name)(x, yr, ys, xc, proj, proj, proj, bg, dsk, sn, wr, ws, wo)


def _postscan_bwd(dout, yr, ys, xc, proj, bg, dsk, sn, wr, ws, wo, name):
    s = dout.shape[0]
    ts = _row_tile(s, 128)

    def body(do_ref, yr_ref, ys_ref, xs_ref, g_ref, z_ref, gt_ref, bg_ref, dsk_ref, sn_ref, wr_ref, ws_ref, wo_ref,
             dyr_ref, dys_ref, dxs_ref, dg_ref, dz_ref, dgt_ref, yrn_ref, ysn_ref, mg_ref, dbr_ref, dbs_ref,
             dbg_ref, ddsk_ref, dsn_ref):
        @pl.when(pl.program_id(0) == 0)
        def _():
            dbg_ref[...] = jnp.zeros_like(dbg_ref)
            ddsk_ref[...] = jnp.zeros_like(ddsk_ref)
            dsn_ref[...] = jnp.zeros_like(dsn_ref)

        xs = xs_ref[...]
        m = _mix_values(yr_ref[...], g_ref[...], ys_ref[...], xs, z_ref[...], gt_ref[...], bg_ref[...],
                        dsk_ref[...], sn_ref[...])
        gr, gs = m["gr"], m["gs"]
        br, bs = _dot(m["yrn"], wr_ref[...]), _dot(m["ysn"], ws_ref[...])
        dmerged = _dot_nt(do_ref[...], wo_ref[...])
        dgt = jnp.concatenate([dmerged * br * gr * (1.0 - gr), dmerged * bs * gs * (1.0 - gs)], axis=1)
        dgt_ref[...] = _bf(dgt)
        dbg_ref[...] += _colsum(dgt)
        dbr, dbs = dmerged * gr, dmerged * gs
        yrn_ref[...] = _bf(m["yrn"])
        ysn_ref[...] = _bf(m["ysn"])
        mg_ref[...] = _bf(gr * br + gs * bs)
        dbr_ref[...] = _bf(dbr)
        dbs_ref[...] = _bf(dbs)
        dyrn = _dot_nt(dbr, wr_ref[...])
        dysn = _dot_nt(dbs, ws_ref[...])
        rn = jnp.concatenate([slab * r for slab, r in m["ret"]], axis=1)
        dg_ref[...] = _bf(dyrn * rn * m["dsg"])
        drn = dyrn * m["sg"]
        dyr_ref[...] = jnp.concatenate(
            [_rms_bwd(drn[:, h * RET_V_DIM:(h + 1) * RET_V_DIM], slab, r) for h, (slab, r) in enumerate(m["ret"])], axis=1)
        dsn_ref[...] += _colsum(dysn * m["ysh"])
        dysh = dysn * sn_ref[...]
        dys1 = jnp.concatenate(
            [_rms_bwd(dysh[:, h * GROUP_W:(h + 1) * GROUP_W], slab, r) for h, (slab, r) in enumerate(m["grp"])], axis=1)
        dz_ref[...] = _bf(dys1 * m["ys0"] * m["dsz"])
        dys0 = dys1 * m["sz"]
        dys_ref[...] = dys0
        dxs_ref[...] = dys0 * dsk_ref[...]
        ddsk_ref[...] += _colsum(dys0 * xs)

    row = lambda w, c=0: pl.BlockSpec((ts, w), lambda i: (i, c))
    full = lambda a: pl.BlockSpec(a.shape, lambda i: (0,) * a.ndim)
    acc = lambda w: pl.BlockSpec((8, w), lambda i: (0, 0))
    sds = jax.ShapeDtypeStruct
    return _pcall(
        body, grid=(s // ts,),
        in_specs=[row(1024), row(1024), row(2048), row(2048), row(1024, C_G // 1024), row(2048, C_Z // 2048),
                  row(2048, C_GATES // 2048), full(bg), full(dsk), full(sn), full(wr), full(ws), full(wo)],
        out_specs=[row(1024), row(2048), row(2048), row(1024), row(2048), row(2048), row(1024), row(2048), row(1024),
                   row(1024), row(1024), acc(2048), acc(2048), acc(2048)],
        out_shape=[sds((s, 1024), F32), sds((s, 2048), F32), sds((s, 2048), F32), sds((s, 1024), BF16),
                   sds((s, 2048), BF16), sds((s, 2048), BF16), sds((s, 1024), BF16), sds((s, 2048), BF16),
                   sds((s, 1024), BF16), sds((s, 1024), BF16), sds((s, 1024), BF16), sds((8, 2048), F32),
                   sds((8, 2048), F32), sds((8, 2048), F32)],
        compiler_params=_params("arbitrary"), name=name)(dout, yr, ys, xc, proj, proj, proj, bg, dsk, sn, wr, ws, wo)


def _prescan_bwd(proj, dxc, dxs_skip, gdtx, dqr, dkr, cosf, sinf, cw, cb, dtb, eexp_t, name):
    s = proj.shape[0]
    ts = _row_tile(s, 256)

    def body(xbc_ref, prev_ref, dt_ref, dxc_ref, dsk_ref, gdt_ref, dq_ref, dk_ref, cos_ref, sin_ref, cw_ref, cb_ref,
             dtb_ref, et_ref, dpre_ref, dqk_ref, ddt_ref, ddtb_ref):
        i = pl.program_id(0)

        @pl.when(i == 0)
        def _():
            ddtb_ref[...] = jnp.zeros_like(ddtb_ref)

        prev = jnp.where(i > 0, prev_ref[...], 0.0)
        xcat = jnp.concatenate([prev, xbc_ref[...]], axis=0)
        pre = _conv_pre(xcat, cw_ref, cb_ref, ts)
        _, dsilu = _silu_and_grad(pre)
        dpre_ref[:, 0:SSM_INNER] = (dxc_ref[:, 0:SSM_INNER] + dsk_ref[...]) * dsilu[:, 0:SSM_INNER]
        dpre_ref[:, SSM_INNER:] = dxc_ref[:, SSM_INNER:] * dsilu[:, SSM_INNER:]
        cs, sn = cos_ref[...], sin_ref[...]
        for h in range(RET_HEADS):
            sl = slice(h * 128, (h + 1) * 128)
            dq = dq_ref[:, sl]
            dk = dk_ref[:, sl] * (RET_QK_DIM ** -0.5)
            dqk_ref[:, sl] = _bf(dq * cs + pltpu.roll(dq * sn, 64, 1))
            dqk_ref[:, 512 + h * 128:512 + (h + 1) * 128] = _bf(dk * cs + pltpu.roll(dk * sn, 64, 1))
        ddt = _split3_dot(gdt_ref[...], et_ref[...])
        ddt = ddt * _sigmoid(dt_ref[:, 0:128] + dtb_ref[...])
        ddtb_ref[...] += _colsum(ddt)
        ddt_ref[:, 0:128] = _bf(ddt)
        ddt_ref[:, 128:] = jnp.zeros((ts, DT_PAD - 128), BF16)

    row = lambda w, c=0: pl.BlockSpec((ts, w), lambda i: (i, c))
    full = lambda a: pl.BlockSpec(a.shape, lambda i: (0,) * a.ndim)
    sds = jax.ShapeDtypeStruct
    return _pcall(
        body, grid=(s // ts,),
        in_specs=[row(4096), _prev_rows_spec(ts, 4096), row(DT_PAD, C_DT // DT_PAD), row(4096), row(2048), row(2048),
                  row(512), row(512), row(128), row(128), full(cw), full(cb), full(dtb), full(eexp_t)],
        out_specs=[row(4096), row(1024), row(DT_PAD), pl.BlockSpec((8, 128), lambda i: (0, 0))],
        out_shape=[sds((s, 4096), F32), sds((s, 1024), BF16), sds((s, DT_PAD), BF16), sds((8, 128), F32)],
        compiler_params=_params("arbitrary"), name=name)(proj, proj, proj, dxc, dxs_skip, gdtx, dqr, dkr, cosf, sinf,
                                                          cw, cb, dtb, eexp_t)


def _conv_bwd(proj, dpre, cw, name):
    s = proj.shape[0]
    ts = _row_tile(s, 256)
    nt = s // ts
    n = ts + 8

    def body(xbc_ref, prev_ref, dp_ref, nxt_ref, cw_ref, dx_ref, dcw_ref, dcb_ref):
        i = pl.program_id(0)

        @pl.when(i == 0)
        def _():
            dcw_ref[...] = jnp.zeros_like(dcw_ref)
            dcb_ref[...] = jnp.zeros_like(dcb_ref)

        prev = jnp.where(i > 0, prev_ref[...], 0.0)
        xcat = jnp.concatenate([prev, xbc_ref[...]], axis=0)
        dp = dp_ref[...]
        nxt = jnp.where(i < nt - 1, nxt_ref[...], 0.0)
        dcat = jnp.concatenate([dp, nxt], axis=0)
        dx = cw_ref[3:4, :] * dp
        dcw_ref[24:32, :] += _colsum(dp * xcat[8:8 + ts])
        for j in range(3):
            sh = 3 - j
            dx = dx + cw_ref[j:j + 1, :] * pltpu.roll(dcat, n - sh, 0)[0:ts]
            dcw_ref[8 * j:8 * j + 8, :] += _colsum(dp * pltpu.roll(xcat, sh, 0)[8:8 + ts])
        dx_ref[...] = _bf(dx)
        dcb_ref[...] += _colsum(dp)

    row = lambda w, c=0: pl.BlockSpec((ts, w), lambda i: (i, c))
    nxt_spec = pl.BlockSpec((8, 4096), lambda i: (jnp.minimum((i + 1) * (ts // 8), s // 8 - 1), 0))
    sds = jax.ShapeDtypeStruct
    return _pcall(
        body, grid=(nt,),
        in_specs=[row(4096), _prev_rows_spec(ts, 4096), row(4096), nxt_spec, pl.BlockSpec(cw.shape, lambda i: (0, 0))],
        out_specs=[row(4096), pl.BlockSpec((32, 4096), lambda i: (0, 0)), pl.BlockSpec((8, 4096), lambda i: (0, 0))],
        out_shape=[sds((s, 4096), BF16), sds((32, 4096), F32), sds((8, 4096), F32)],
        compiler_params=_params("arbitrary"), name=name)(proj, proj, dpre, dpre, cw)


def _xattn_values(x, gain, wq, kv):
    r = _rstd(x)
    h = (x * r) * gain
    q = _dot(h, wq)
    ps, os_ = [], []
    for hd in range(XA_HEADS):
        sl = slice(hd * XA_HEAD_DIM, (hd + 1) * XA_HEAD_DIM)
        sc = _dot_nt(q[:, sl], kv[:, sl]) * (XA_HEAD_DIM ** -0.5)
        e = jnp.exp(sc - jnp.max(sc, axis=-1, keepdims=True))
        p = e / jnp.sum(e, axis=-1, keepdims=True)
        ps.append(p)
        os_.append(_dot(p, kv[:, D_MODEL + hd * XA_HEAD_DIM:D_MODEL + (hd + 1) * XA_HEAD_DIM]))
    return r, h, q, ps, jnp.concatenate(os_, axis=1)


def _xattn_fwd(x, gain, wq, kv, wo, name):
    s = x.shape[0]
    ts = _row_tile(s, 256)

    def body(x_ref, g_ref, wq_ref, kv_ref, wo_ref, o_ref):
        x_ = x_ref[...]
        _, _, _, _, o = _xattn_values(x_, g_ref[...], wq_ref[...], kv_ref[...])
        o_ref[...] = x_ + _dot(o, wo_ref[...])

    row = pl.BlockSpec((ts, D_MODEL), lambda i: (i, 0))
    full = lambda a: pl.BlockSpec(a.shape, lambda i: (0,) * a.ndim)
    return _pcall(
        body, grid=(s // ts,), in_specs=[row, full(gain), full(wq), full(kv), full(wo)], out_specs=row,
        out_shape=jax.ShapeDtypeStruct((s, D_MODEL), F32), compiler_params=_params("parallel"), name=name)(
            x, gain, wq, kv, wo)


def _xattn_bwd(x, dout, gain, wq, kv, wo, name):
    s = x.shape[0]
    m = kv.shape[0]
    ts = _row_tile(s, 256)

    def body(x_ref, do_ref, g_ref, wq_ref, kv_ref, wo_ref, dx_ref, h_ref, dq_ref, o_ref, dkv_ref, dg_ref):
        @pl.when(pl.program_id(0) == 0)
        def _():
            dkv_ref[...] = jnp.zeros_like(dkv_ref)
            dg_ref[...] = jnp.zeros_like(dg_ref)

        x_, do, kvv = x_ref[...], do_ref[...], kv_ref[...]
        r, h, q, ps, o = _xattn_values(x_, g_ref[...], wq_ref[...], kvv)
        dov = _dot_nt(do, wo_ref[...])
        dqs = []
        for hd in range(XA_HEADS):
            sl = slice(hd * XA_HEAD_DIM, (hd + 1) * XA_HEAD_DIM)
            vl = slice(D_MODEL + hd * XA_HEAD_DIM, D_MODEL + (hd + 1) * XA_HEAD_DIM)
            p, doh = ps[hd], dov[:, sl]
            dp = _dot_nt(doh, kvv[:, vl])
            dsc = p * (dp - jnp.sum(dp * p, axis=-1, keepdims=True)) * (XA_HEAD_DIM ** -0.5)
            dqs.append(_dot(dsc, kvv[:, sl]))
            dkv_ref[:, sl] += _dot_tn(dsc, q[:, sl])
            dkv_ref[:, vl] += _dot_tn(p, doh)
        dq = jnp.concatenate(dqs, axis=1)
        dh = _dot_nt(dq, wq_ref[...])
        dg_ref[...] += _colsum(dh * (x_ * r))
        dx_ref[...] = do + _rms_bwd(dh * g_ref[...], x_, r)
        h_ref[...] = _bf(h)
        dq_ref[...] = _bf(dq)
        o_ref[...] = _bf(o)

    row = pl.BlockSpec((ts, D_MODEL), lambda i: (i, 0))
    full = lambda a: pl.BlockSpec(a.shape, lambda i: (0,) * a.ndim)
    sds = jax.ShapeDtypeStruct
    return _pcall(
        body, grid=(s // ts,), in_specs=[row, row, full(gain), full(wq), full(kv), full(wo)],
        out_specs=[row, row, row, row, pl.BlockSpec((m, 2 * D_MODEL), lambda i: (0, 0)),
                   pl.BlockSpec((8, D_MODEL), lambda i: (0, 0))],
        out_shape=[sds((s, D_MODEL), F32), sds((s, D_MODEL), BF16), sds((s, D_MODEL), BF16), sds((s, D_MODEL), BF16),
                   sds((m, 2 * D_MODEL), F32), sds((8, D_MODEL), F32)],
        compiler_params=_params("arbitrary"), name=name)(x, dout, gain, wq, kv, wo)


def _mem_bwd(mem, gain, dkv, wkv, name):
    m = mem.shape[0]

    def body(mem_ref, g_ref, dkv_ref, w_ref, mn_ref, dg_ref):
        mm = mem_ref[...]
        r = _rstd(mm)
        xh = mm * r
        mn_ref[...] = _bf(xh * g_ref[...])
        dmn = _dot_nt(dkv_ref[...], w_ref[...])
        dg_ref[...] = jnp.zeros_like(dg_ref) + _colsum(dmn * xh)

    full = lambda a: pl.BlockSpec(a.shape, lambda: (0,) * a.ndim)
    return _pcall(
        body, in_specs=[full(mem), full(gain), full(dkv), full(wkv)],
        out_specs=[pl.BlockSpec((m, D_MODEL), lambda: (0, 0)), pl.BlockSpec((8, D_MODEL), lambda: (0, 0))],
        out_shape=[jax.ShapeDtypeStruct((m, D_MODEL), BF16), jax.ShapeDtypeStruct((8, D_MODEL), F32)],
        compiler_params=pltpu.CompilerParams(vmem_limit_bytes=VMEM_LIMIT), name=name)(mem, gain, dkv, wkv)


def _mlp_fwd(x, gain, w1, w2, name):
    s = x.shape[0]
    ts = _row_tile(s, 512)
    tf = 1024
    nf = D_FF // tf

    def body(x_ref, g_ref, w1_ref, w2_ref, o_ref, h_sc, acc):
        j = pl.program_id(1)

        @pl.when(j == 0)
        def _():
            xx = x_ref[...]
            h_sc[...] = _bf((xx * _rstd(xx)) * g_ref[...])
            acc[...] = jnp.zeros_like(acc)

        a = jnp.dot(h_sc[...], w1_ref[...], preferred_element_type=F32)
        r = jnp.square(jnp.maximum(a, 0.0))
        acc[...] += _dot(r, w2_ref[...])

        @pl.when(j == nf - 1)
        def _():
            o_ref[...] = x_ref[...] + acc[...]

    row = pl.BlockSpec((ts, D_MODEL), lambda i, j: (i, 0))
    return _pcall(
        body, grid=(s // ts, nf),
        in_specs=[row, pl.BlockSpec((1, D_MODEL), lambda i, j: (0, 0)), pl.BlockSpec((D_MODEL, tf), lambda i, j: (0, j)),
                  pl.BlockSpec((tf, D_MODEL), lambda i, j: (j, 0))],
        out_specs=row, out_shape=jax.ShapeDtypeStruct((s, D_MODEL), F32),
        scratch_shapes=[pltpu.VMEM((ts, D_MODEL), BF16), pltpu.VMEM((ts, D_MODEL), F32)],
        compiler_params=_params("parallel", "arbitrary"), name=name)(x, gain, w1, w2)


def _mlp_bwd(x, dout, gain, w1, w2, name):
    s = x.shape[0]
    ts = _row_tile(s, 512)
    tf = 1024
    nf = D_FF // tf

    def body(x_ref, do_ref, g_ref, w1_ref, w2_ref, dx_ref, h_ref, r_ref, da_ref, dg_ref, h_sc, do_sc, acc):
        i, j = pl.program_id(0), pl.program_id(1)

        @pl.when(j == 0)
        def _():
            xx = x_ref[...]
            h_sc[...] = _bf((xx * _rstd(xx)) * g_ref[...])
            do_sc[...] = _bf(do_ref[...])
            acc[...] = jnp.zeros_like(acc)
            h_ref[...] = h_sc[...]

        @pl.when((j == 0) & (i == 0))
        def _():
            dg_ref[...] = jnp.zeros_like(dg_ref)

        a = jnp.dot(h_sc[...], w1_ref[...], preferred_element_type=F32)
        ra = jnp.maximum(a, 0.0)
        r_ref[...] = _bf(ra * ra)
        dr = lax.dot_general(do_sc[...], w2_ref[...], (((1,), (1,)), ((), ())), preferred_element_type=F32)
        da = _bf(dr * 2.0 * ra)
        da_ref[...] = da
        acc[...] += lax.dot_general(da, w1_ref[...], (((1,), (1,)), ((), ())), preferred_element_type=F32)

        @pl.when(j == nf - 1)
        def _():
            xx = x_ref[...]
            r = _rstd(xx)
            dh = acc[...]
            dg_ref[...] += _colsum(dh * (xx * r))
            dx_ref[...] = do_ref[...] + _rms_bwd(dh * g_ref[...], xx, r)

    row = pl.BlockSpec((ts, D_MODEL), lambda i, j: (i, 0))
    ff = pl.BlockSpec((ts, tf), lambda i, j: (i, j))
    sds = jax.ShapeDtypeStruct
    return _pcall(
        body, grid=(s // ts, nf),
        in_specs=[row, row, pl.BlockSpec((1, D_MODEL), lambda i, j: (0, 0)),
                  pl.BlockSpec((D_MODEL, tf), lambda i, j: (0, j)), pl.BlockSpec((tf, D_MODEL), lambda i, j: (j, 0))],
        out_specs=[row, row, ff, ff, pl.BlockSpec((8, D_MODEL), lambda i, j: (0, 0))],
        out_shape=[sds((s, D_MODEL), F32), sds((s, D_MODEL), BF16), sds((s, D_FF), BF16), sds((s, D_FF), BF16),
                   sds((8, D_MODEL), F32)],
        scratch_shapes=[pltpu.VMEM((ts, D_MODEL), BF16), pltpu.VMEM((ts, D_MODEL), BF16), pltpu.VMEM((ts, D_MODEL), F32)],
        compiler_params=_params("arbitrary", "arbitrary"), name=name)(x, dout, gain, w1, w2)


def _final(x, gain, tgt, name):
    s = x.shape[0]
    ts = _row_tile(s, 512)

    def body(x_ref, g_ref, t_ref, dx_ref, loss_ref, dg_ref):
        @pl.when(pl.program_id(0) == 0)
        def _():
            loss_ref[...] = jnp.zeros_like(loss_ref)
            dg_ref[...] = jnp.zeros_like(dg_ref)

        xx = x_ref[...]
        r = _rstd(xx)
        xh = xx * r
        err = xh * g_ref[...] - t_ref[...]
        loss_ref[...] += 0.5 * jnp.sum(jnp.sum(err * err, axis=1, keepdims=True), axis=0, keepdims=True) / D_MODEL
        dy = err * (1.0 / D_MODEL)
        dg_ref[...] += _colsum(dy * xh)
        dx_ref[...] = _rms_bwd(dy * g_ref[...], xx, r)

    row = pl.BlockSpec((ts, D_MODEL), lambda i: (i, 0))
    return _pcall(
        body, grid=(s // ts,), in_specs=[row, pl.BlockSpec((1, D_MODEL), lambda i: (0, 0)), row],
        out_specs=[row, pl.BlockSpec((8, 128), lambda i: (0, 0)), pl.BlockSpec((8, D_MODEL), lambda i: (0, 0))],
        out_shape=[jax.ShapeDtypeStruct((s, D_MODEL), F32), jax.ShapeDtypeStruct((8, 128), F32),
                   jax.ShapeDtypeStruct((8, D_MODEL), F32)],
        compiler_params=_params("arbitrary"), name=name)(x, gain, tgt)


def _as3d(a):
    return a.reshape((-1,) + a.shape[-2:])


def _sum_cast(terms, out_dtype, name, row_want=256):
    shape = terms[0].shape
    t3 = [_as3d(t) for t in terms]
    b, r, c = t3[0].shape
    tr = _row_tile(r, row_want)

    def body(*refs):
        acc = refs[0][...].astype(F32)
        for t in refs[1:-1]:
            acc = acc + t[...].astype(F32)
        refs[-1][...] = acc.astype(out_dtype)

    spec = pl.BlockSpec((1, tr, c), lambda i, j: (i, j, 0))
    out = _pcall(body, grid=(b, r // tr), in_specs=[spec] * len(t3), out_specs=spec,
                 out_shape=jax.ShapeDtypeStruct((b, r, c), out_dtype), compiler_params=_params("parallel", "parallel"),
                 name=name)(*t3)
    return out.reshape(shape)


def _pair_sum(a, b, sel, loff, out_dtype, name):
    half, _, r, c = b.shape
    k = sel.shape[0]
    tr = _row_tile(r, 256)

    def body(sel_ref, loff_ref, a_ref, b_ref, o_ref):
        o_ref[...] = (a_ref[...] + b_ref[...]).astype(out_dtype)

    blkshape = (1, 1, tr, c)
    grid_spec = pltpu.PrefetchScalarGridSpec(
        num_scalar_prefetch=2, grid=(k, half, r // tr),
        in_specs=[pl.BlockSpec(blkshape, lambda q, l, j, sel_ref, lo_ref: (lo_ref[0] + l, sel_ref[q], j, 0)),
                  pl.BlockSpec(blkshape, lambda q, l, j, sel_ref, lo_ref: (l, sel_ref[q], j, 0))],
        out_specs=pl.BlockSpec(blkshape, lambda q, l, j, sel_ref, lo_ref: (q, l, j, 0)))
    return _pcall(body, grid_spec=grid_spec, out_shape=jax.ShapeDtypeStruct((k, half, r, c), out_dtype),
                  compiler_params=_params("parallel", "parallel", "parallel"), name=name)(sel, loff, a, b)


def _adamw(w, g, m, v, name):
    shape = w.shape
    w3, g3, m3, v3 = _as3d(w), _as3d(g), _as3d(m), _as3d(v)
    b, r, c = w3.shape
    tr = _row_tile(r, 256)

    def body(w_ref, g_ref, m_ref, v_ref, d_ref, mo_ref, vo_ref):
        gg = g_ref[...]
        mn = ADAM_B1 * m_ref[...] + (1.0 - ADAM_B1) * gg
        vn = ADAM_B2 * v_ref[...] + (1.0 - ADAM_B2) * jnp.square(gg)
        m_hat = mn / (1.0 - ADAM_B1 ** ADAM_STEP)
        v_hat = vn / (1.0 - ADAM_B2 ** ADAM_STEP)
        d_ref[...] = -ADAM_LR * (m_hat / (jnp.sqrt(v_hat) + ADAM_EPS) + ADAM_WD * w_ref[...])
        mo_ref[...] = mn
        vo_ref[...] = vn

    spec = pl.BlockSpec((1, tr, c), lambda i, j: (i, j, 0))
    sd = jax.ShapeDtypeStruct((b, r, c), F32)
    d, mo, vo = _pcall(body, grid=(b, r // tr), in_specs=[spec] * 4, out_specs=[spec] * 3, out_shape=[sd] * 3,
                       compiler_params=_params("parallel", "parallel"), name=name)(w3, g3, m3, v3)
    return d.reshape(shape), mo.reshape(shape), vo.reshape(shape)


ANY = pl.BlockSpec(memory_space=pl.ANY)


def _place():
    return lax.axis_index("x"), lax.axis_index("y"), lax.axis_index("c")


def _flip(x, y, r):
    return (1 - x if r & 2 else x), (1 - y if r & 1 else y)


def _chip_gather(arrs, name):
    n = len(arrs)
    nl = arrs[0].shape[0]
    half = nl // 2
    assert half * 2 == nl

    def body(*refs):
        ins, outs = refs[:n], refs[n:2 * n]
        send, recv, fsend, frecv, lsem = refs[2 * n:]
        x, y, c = _place()
        blk = 2 * x + y
        mine = pl.ds(c * half, half)
        theirs = pl.ds((1 - c) * half, half)
        local = [pltpu.make_async_copy(ins[a], outs[a].at[blk], lsem.at[a]) for a in range(n)]
        for cp in local:
            cp.start()

        def ici(r, a):
            cx, cy = _flip(x, y, r)
            return pltpu.make_async_remote_copy(
                src_ref=ins[a].at[mine], dst_ref=outs[a].at[blk, mine], send_sem=send.at[(r - 1) * n + a],
                recv_sem=recv.at[(r - 1) * n + a], device_id=(cx, cy, c), device_id_type=MESH)

        def ici_in(r, a):
            cx, cy = _flip(x, y, r)
            return pltpu.make_async_remote_copy(
                src_ref=ins[a].at[mine], dst_ref=outs[a].at[2 * cx + cy, mine], send_sem=send.at[(r - 1) * n + a],
                recv_sem=recv.at[(r - 1) * n + a], device_id=(cx, cy, c), device_id_type=MESH)

        def d2d(r, a, rows):
            cx, cy = _flip(x, y, r)
            ref = outs[a].at[2 * cx + cy, rows]
            return pltpu.make_async_remote_copy(
                src_ref=ref, dst_ref=ref, send_sem=fsend.at[(r - 1) * n + a], recv_sem=frecv.at[(r - 1) * n + a],
                device_id=(x, y, 1 - c), device_id_type=MESH)

        for r in (1, 2, 3):
            for a in range(n):
                ici(r, a).start()
        for r in (1, 2, 3):
            for a in range(n):
                ici_in(r, a).wait_recv()
                d2d(r, a, mine).start()
        for r in (1, 2, 3):
            for a in range(n):
                d2d(r, a, theirs).wait_recv()
        for r in (1, 2, 3):
            for a in range(n):
                ici(r, a).wait_send()
                d2d(r, a, mine).wait_send()
        for cp in local:
            cp.wait()

    return _pcall(
        body, in_specs=[ANY] * n, out_specs=[ANY] * n,
        out_shape=[jax.ShapeDtypeStruct((4,) + a.shape, a.dtype) for a in arrs],
        scratch_shapes=[pltpu.SemaphoreType.DMA((3 * n,)), pltpu.SemaphoreType.DMA((3 * n,)),
                        pltpu.SemaphoreType.DMA((3 * n,)), pltpu.SemaphoreType.DMA((3 * n,)),
                        pltpu.SemaphoreType.DMA((n,))],
        name=name)(*arrs)


def _sibling_swap(arrs, name):
    n = len(arrs)
    half = arrs[0].shape[0] // 2

    def body(*refs):
        ins, outs = refs[:n], refs[n:2 * n]
        send, recv = refs[2 * n:]
        x, y, c = _place()
        cps = [pltpu.make_async_remote_copy(
            src_ref=ins[a].at[pl.ds((1 - c) * half, half)], dst_ref=outs[a], send_sem=send.at[a], recv_sem=recv.at[a],
            device_id=(x, y, 1 - c), device_id_type=MESH) for a in range(n)]
        for cp in cps:
            cp.start()
        for cp in cps:
            cp.wait()

    return _pcall(
        body, in_specs=[ANY] * n, out_specs=[ANY] * n,
        out_shape=[jax.ShapeDtypeStruct((half,) + a.shape[1:], a.dtype) for a in arrs],
        scratch_shapes=[pltpu.SemaphoreType.DMA((n,)), pltpu.SemaphoreType.DMA((n,))], name=name)(*arrs)


def _chip_exchange(arrs, name):
    n = len(arrs)

    def body(*refs):
        ins, outs = refs[:n], refs[n:2 * n]
        send, recv = refs[2 * n:]
        x, y, c = _place()
        cps = []
        for r in (1, 2, 3):
            cx, cy = _flip(x, y, r)
            for a in range(n):
                cps.append(pltpu.make_async_remote_copy(
                    src_ref=ins[a].at[r - 1], dst_ref=outs[a].at[r - 1], send_sem=send.at[(r - 1) * n + a],
                    recv_sem=recv.at[(r - 1) * n + a], device_id=(cx, cy, c), device_id_type=MESH))
        for cp in cps:
            cp.start()
        for cp in cps:
            cp.wait()

    return _pcall(
        body, in_specs=[ANY] * n, out_specs=[ANY] * n,
        out_shape=[jax.ShapeDtypeStruct(a.shape, a.dtype) for a in arrs],
        scratch_shapes=[pltpu.SemaphoreType.DMA((3 * n,)), pltpu.SemaphoreType.DMA((3 * n,))], name=name)(*arrs)


def _sibling_join(arrs, name):
    n = len(arrs)
    half = arrs[0].shape[0]

    def body(*refs):
        ins, outs = refs[:n], refs[n:2 * n]
        send, recv, lsem = refs[2 * n:]
        x, y, c = _place()
        mine = pl.ds(c * half, half)
        local = [pltpu.make_async_copy(ins[a], outs[a].at[mine], lsem.at[a]) for a in range(n)]
        cps = [pltpu.make_async_remote_copy(
            src_ref=ins[a], dst_ref=outs[a].at[mine], send_sem=send.at[a], recv_sem=recv.at[a],
            device_id=(x, y, 1 - c), device_id_type=MESH) for a in range(n)]
        for cp in local + cps:
            cp.start()
        for a in range(n):
            pltpu.make_async_remote_copy(
                src_ref=ins[a], dst_ref=outs[a].at[pl.ds((1 - c) * half, half)], send_sem=send.at[a],
                recv_sem=recv.at[a], device_id=(x, y, 1 - c), device_id_type=MESH).wait_recv()
        for cp in cps:
            cp.wait_send()
        for cp in local:
            cp.wait()

    return _pcall(
        body, in_specs=[ANY] * n, out_specs=[ANY] * n,
        out_shape=[jax.ShapeDtypeStruct((2 * half,) + a.shape[1:], a.dtype) for a in arrs],
        scratch_shapes=[pltpu.SemaphoreType.DMA((n,)), pltpu.SemaphoreType.DMA((n,)), pltpu.SemaphoreType.DMA((n,))],
        name=name)(*arrs)


def _gather8(v, reduce, name):
    rows, w = v.shape

    def body(v_ref, out_ref, buf, send_sems, recv_sems):
        x, y, c = _place()
        me, sibling = (x, y, c), (x, y, 1 - c)
        chips = [_flip(x, y, r) for r in (1, 2, 3)]
        dst = out_ref if not reduce else buf

        def slot(px, py, pc):
            return dst.at[4 * px + 2 * py + pc]

        def copy(k, block, to, src=None):
            return pltpu.make_async_remote_copy(
                src_ref=slot(*block) if src is None else src, dst_ref=slot(*block), send_sem=send_sems.at[k],
                recv_sem=recv_sems.at[k], device_id=to, device_id_type=MESH)

        dst[4 * x + 2 * y + c] = v_ref[...]
        first = [copy(0, me, sibling, src=v_ref)]
        first += [copy(1 + j, me, (*chip, c), src=v_ref) for j, chip in enumerate(chips)]
        for cp in first:
            cp.start()
        passed = [copy(4 + j, (*chip, c), sibling) for j, chip in enumerate(chips)]
        for j, chip in enumerate(chips):
            copy(1 + j, (*chip, c), me).wait_recv()
            passed[j].start()
        copy(0, sibling, me).wait_recv()
        for j, chip in enumerate(chips):
            copy(4 + j, (*chip, 1 - c), me).wait_recv()
        for cp in first + passed:
            cp.wait_send()
        if reduce:
            acc = buf[0]
            for d in range(1, 8):
                acc = acc + buf[d]
            out_ref[...] = acc

    vm = pl.BlockSpec(memory_space=pltpu.VMEM)
    scratch = [pltpu.VMEM((8, rows, w) if reduce else (8, 8, 128), F32), pltpu.SemaphoreType.DMA((7,)),
               pltpu.SemaphoreType.DMA((7,))]
    out_shape = jax.ShapeDtypeStruct((rows, w) if reduce else (8, rows, w), F32)
    return _pcall(body, in_specs=[vm], out_specs=vm, out_shape=out_shape, scratch_shapes=scratch,
                  compiler_params=pltpu.CompilerParams(vmem_limit_bytes=VMEM_LIMIT), name=name)(v)


SMALL = [("norm_mix", 1024), ("b_gate", 2048), ("conv_b", 4096), ("dt_bias", 32), ("a_log", 32), ("d_skip", 32),
         ("ssm_norm", 2048), ("norm_xa", 1024), ("norm_mem", 1024), ("norm_mlp", 1024)]


def _rows_of(width):
    return max(1, width // 1024)


def _pack_rows(pieces):
    out = []
    for p in pieces:
        p = p.astype(F32)
        if p.shape[-1] < 1024:
            p = jnp.pad(p, ((0, 0), (0, 1024 - p.shape[-1])))
        out.append(p.reshape(-1, 1024))
    cat = jnp.concatenate(out, axis=0)
    pad = (-cat.shape[0]) % 8
    return jnp.pad(cat, ((0, pad), (0, 0))) if pad else cat


def _unpack_rows(packed, widths_rows):
    out, at = [], 0
    for r, w in widths_rows:
        k = r * _rows_of(w)
        p = packed[at:at + k]
        at += k
        out.append(p[:, :w] if w < 1024 else p.reshape(r, w))
    return out


def _to_cat(w):
    pieces = [w[:, O_XBC:O_XBC + 4096], w[:, O_Z:O_Z + 2048], w[:, O_GATES:O_GATES + 2048], w[:, O_Q:O_Q + 512],
              w[:, O_K:O_K + 512], w[:, O_V:O_V + 1024], w[:, O_G:O_G + 1024], w[:, O_DT:O_DT + 32],
              jnp.zeros((w.shape[0], NP - IN_DIM), w.dtype)]
    return jnp.concatenate(pieces, axis=1)


def _from_cat(g):
    pieces = [g[:, C_Q:C_Q + 512], g[:, C_K:C_K + 512], g[:, C_V:C_V + 1024], g[:, C_G:C_G + 1024],
              g[:, C_Z:C_Z + 2048], g[:, C_XBC:C_XBC + 4096], g[:, C_DT:C_DT + 32], g[:, C_GATES:C_GATES + 2048]]
    return jnp.concatenate(pieces, axis=1)


PACK_ROWS = [("w_br_ret", 256), ("w_br_ssm", 512), ("w_out", 256), ("xa_wq", 256), ("xa_wo", 256), ("mlp_w1", 1024),
             ("mlp_w2", 1024)]
PACK_N = sum(r for _, r in PACK_ROWS)


def _blocks_rows(g, n):
    return g.reshape(4, n, g.shape[-1])


def _blocks_cols(g, n):
    return g.reshape(g.shape[0], 4, n).transpose(1, 0, 2)


def kernel(x, mem, positions, norm_mix, w_in, b_gate, conv_w, conv_b, dt_bias, a_log, d_skip, ssm_norm, w_br_ret, w_br_ssm, w_out, norm_xa, norm_mem, xa_wq, xa_wkv, xa_wo, norm_mlp, mlp_w1, mlp_w2, norm_final, loss_target, m_norm_mix, m_w_in, m_b_gate, m_conv_w, m_conv_b, m_dt_bias, m_a_log, m_d_skip, m_ssm_norm, m_w_br_ret, m_w_br_ssm, m_w_out, m_norm_xa, m_norm_mem, m_xa_wq, m_xa_wkv, m_xa_wo, m_norm_mlp, m_mlp_w1, m_mlp_w2, m_norm_final, v_norm_mix, v_w_in, v_b_gate, v_conv_w, v_conv_b, v_dt_bias, v_a_log, v_d_skip, v_ssm_norm, v_w_br_ret, v_w_br_ssm, v_w_out, v_norm_xa, v_norm_mem, v_xa_wq, v_xa_wkv, v_xa_wo, v_norm_mlp, v_mlp_w1, v_mlp_w2, v_norm_final):
    W = dict(norm_mix=norm_mix, w_in=w_in, b_gate=b_gate, conv_w=conv_w, conv_b=conv_b, dt_bias=dt_bias, a_log=a_log,
             d_skip=d_skip, ssm_norm=ssm_norm, w_br_ret=w_br_ret, w_br_ssm=w_br_ssm, w_out=w_out, norm_xa=norm_xa,
             norm_mem=norm_mem, xa_wq=xa_wq, xa_wkv=xa_wkv, xa_wo=xa_wo, norm_mlp=norm_mlp, mlp_w1=mlp_w1,
             mlp_w2=mlp_w2, norm_final=norm_final)
    M = dict(norm_mix=m_norm_mix, w_in=m_w_in, b_gate=m_b_gate, conv_w=m_conv_w, conv_b=m_conv_b, dt_bias=m_dt_bias,
             a_log=m_a_log, d_skip=m_d_skip, ssm_norm=m_ssm_norm, w_br_ret=m_w_br_ret, w_br_ssm=m_w_br_ssm,
             w_out=m_w_out, norm_xa=m_norm_xa, norm_mem=m_norm_mem, xa_wq=m_xa_wq, xa_wkv=m_xa_wkv, xa_wo=m_xa_wo,
             norm_mlp=m_norm_mlp, mlp_w1=m_mlp_w1, mlp_w2=m_mlp_w2, norm_final=m_norm_final)
    V = dict(norm_mix=v_norm_mix, w_in=v_w_in, b_gate=v_b_gate, conv_w=v_conv_w, conv_b=v_conv_b, dt_bias=v_dt_bias,
             a_log=v_a_log, d_skip=v_d_skip, ssm_norm=v_ssm_norm, w_br_ret=v_w_br_ret, w_br_ssm=v_w_br_ssm,
             w_out=v_w_out, norm_xa=v_norm_xa, norm_mem=v_norm_mem, xa_wq=v_xa_wq, xa_wkv=v_xa_wkv, xa_wo=v_xa_wo,
             norm_mlp=v_norm_mlp, mlp_w1=v_mlp_w1, mlp_w2=v_mlp_w2, norm_final=v_norm_final)
    nl = w_in.shape[0]
    s = x.shape[1]
    x0 = x[0]
    mem2 = mem[0]
    tgt = loss_target[0]
    blk = 2 * lax.axis_index("x") + lax.axis_index("y")

    a1 = jnp.concatenate([W[k] for k, _ in PACK_ROWS], axis=1).astype(BF16)
    g1, gin, gkv = _chip_gather([a1, w_in.astype(BF16), xa_wkv.astype(BF16)], "gather_weights")
    cw_all = _gather8(conv_w.reshape(nl * SSM_CONV, 1024), False, "gather_conv_w")
    cw_full = cw_all.reshape(4, 2, nl, SSM_CONV, 1024)[:, 0].transpose(1, 2, 0, 3).reshape(nl, SSM_CONV, SSM_CONV_DIM)

    offs = {}
    at = 0
    for k, r in PACK_ROWS:
        offs[k] = (at, r)
        at += r

    def rows_weight(k, l):
        o, r = offs[k]
        return g1[:, l, o:o + r].reshape(4 * r, 1024)

    inv_freq = ROPE_THETA ** (-jnp.arange(0, RET_QK_DIM, 2, dtype=F32) / RET_QK_DIM)
    ang = positions.astype(F32)[0][:, None] * inv_freq
    cos, sin = jnp.cos(ang), jnp.sin(ang)
    cosf = jnp.concatenate([cos, cos], axis=1)
    sinf = jnp.concatenate([-sin, sin], axis=1)
    dm, qd, kd, cd = (jnp.asarray(c) for c in _ret_constants())
    eye, blkm = (jnp.asarray(c) for c in _ssd_constants())
    consts = (dm, qd, kd, cd, eye, blkm)
    e_np = _head_expand()
    eexp = jnp.asarray(e_np, BF16)
    eexp_t = jnp.asarray(e_np.T.copy(), BF16)

    saved = []
    xcur = x0
    for l in range(nl):
        wcat = _to_cat(jnp.concatenate([gin[j, l] for j in range(4)], axis=1))
        wr, ws, wo = rows_weight("w_br_ret", l), rows_weight("w_br_ssm", l), rows_weight("w_out", l)
        wq, wxo, w2 = rows_weight("xa_wq", l), rows_weight("xa_wo", l), rows_weight("mlp_w2", l)
        o1, r1 = offs["mlp_w1"]
        w1 = g1[:, l, o1:o1 + r1].transpose(1, 0, 2).reshape(D_MODEL, D_FF)
        wkv = gkv[:, l].transpose(1, 0, 2).reshape(D_MODEL, 2 * D_MODEL)
        cw, cb = cw_full[l], conv_b[l][None]
        dtb = jnp.pad(dt_bias[l], (0, 128 - SSM_HEADS))[None]
        ax = jnp.repeat(-jnp.exp(a_log[l]), 64)[None]
        dsk = jnp.repeat(d_skip[l], 64)[None]
        bg, sn = b_gate[l][None], ssm_norm[l][None]
        proj, u = _nmm(xcur, norm_mix[l][None], wcat, "in_proj", save_u=True)
        qr, kr, xc, dtx = _prescan(proj, cosf, sinf, cw, cb, dtb, eexp, "prescan")
        yr, ys, sst, hst = _scan_fwd(qr, kr, proj, xc, dtx, ax, consts, "scan_fwd")
        x1 = _postscan_fwd(xcur, yr, ys, xc, proj, bg, dsk, sn, wr, ws, wo, "postscan")
        kv = _bf(_nmm(mem2, norm_mem[l][None], wkv, "mem_kv"))
        x2 = _xattn_fwd(x1, norm_xa[l][None], wq, kv, wxo, "xattn")
        x3 = _mlp_fwd(x2, norm_mlp[l][None], w1, w2, "mlp")
        saved.append(dict(x0=xcur, x1=x1, x2=x2, proj=proj, u=u, qr=qr, kr=kr, xc=xc, dtx=dtx, yr=yr, ys=ys, sst=sst,
                          hst=hst, kv=kv, wcat=wcat, wr=wr, ws=ws, wo=wo, wq=wq, wxo=wxo, w1=w1, w2=w2, wkv=wkv, cw=cw,
                          cb=cb, dtb=dtb, ax=ax, dsk=dsk, bg=bg, sn=sn))
        xcur = x3

    dx, loss_acc, dnf = _final(xcur, norm_final[None], tgt, "final")
    loss = lax.psum(loss_acc[0, 0], ("x", "y", "c"))

    small_g = [None] * nl
    p1, pin, pkv = [None] * nl, [None] * nl, [None] * nl
    for l in reversed(range(nl)):
        sv = saved[l]
        dx2, hm, rm, dam, dg_mlp = _mlp_bwd(sv["x2"], dx, norm_mlp[l][None], sv["w1"], sv["w2"], "mlp_bwd")
        dw1 = _mm_tn(hm, dam, "dw_mlp1")
        dw2 = _mm_tn(rm, dx, "dw_mlp2")
        dx1, hx, dqx, ox, dkv, dg_xa = _xattn_bwd(sv["x1"], dx2, norm_xa[l][None], sv["wq"], sv["kv"], sv["wxo"],
                                                  "xattn_bwd")
        dwq = _mm_tn(hx, dqx, "dw_xq")
        dwxo = _mm_tn(ox, dx2, "dw_xo")
        memn, dg_mem = _mem_bwd(mem2, norm_mem[l][None], dkv, sv["wkv"], "mem_bwd")
        dwkv = _mm_tn(memn, dkv, "dw_xkv")
        (dyr, dys, dxs_skip, dgp, dzp, dgtp, yrn, ysn, mg, dbr, dbs, dbg, ddsk, dsn) = _postscan_bwd(
            dx1, sv["yr"], sv["ys"], sv["xc"], sv["proj"], sv["bg"], sv["dsk"], sv["sn"], sv["wr"], sv["ws"], sv["wo"],
            "postscan_bwd")
        dwo = _mm_tn(mg, dx1, "dw_out")
        dwr = _mm_tn(yrn, dbr, "dw_br_ret")
        dws = _mm_tn(ysn, dbs, "dw_br_ssm")
        dqr, dkr, dvp, dxc, gdtx, da_cols = _scan_bwd(sv["qr"], sv["kr"], sv["proj"], sv["xc"], sv["dtx"], sv["ax"],
                                                      consts, sv["sst"], sv["hst"], dyr, dys, "scan_bwd")
        dpre, dqk, ddtp, ddtb = _prescan_bwd(sv["proj"], dxc, dxs_skip, gdtx, dqr, dkr, cosf, sinf, sv["cw"], sv["cb"],
                                             sv["dtb"], eexp_t, "prescan_bwd")
        dxbc, dcw, dcb = _conv_bwd(sv["proj"], dpre, sv["cw"], "conv_bwd")
        dproj = jnp.concatenate([dxbc, dzp, dgtp, dqk, dvp, dgp, ddtp], axis=1)
        dwcat = _mm_tn(sv["u"], dproj, "dw_in")
        dx, dg_mix = _in_bwd(dproj, sv["wcat"], sv["x0"], norm_mix[l][None], dx1, "in_bwd")

        da_log = (da_cols.reshape(SSM_HEADS, 64).sum(axis=1)) * (-jnp.exp(a_log[l]))
        dd_skip = ddsk[0].reshape(SSM_HEADS, 64).sum(axis=1)
        small_g[l] = [dg_mix[0:1], dbg[0:1], dcb[0:1], ddtb[0:1, :SSM_HEADS], da_log[None], dd_skip[None], dsn[0:1],
                      dg_xa[0:1], dg_mem[0:1], dg_mlp[0:1], dcw[0::8]]
        p1[l] = jnp.concatenate([_blocks_rows(dwr, 256), _blocks_rows(dws, 512), _blocks_rows(dwo, 256),
                                 _blocks_rows(dwq, 256), _blocks_rows(dwxo, 256), _blocks_cols(dw1, 1024),
                                 _blocks_rows(dw2, 1024)], axis=1)
        pin[l] = _blocks_cols(_from_cat(dwcat), IN_DIM // 4)
        pkv[l] = _blocks_cols(dwkv, 512)

    grad_x = dx[None]

    pieces = []
    for l in range(nl):
        pieces += small_g[l]
    pieces.append(dnf[0:1])
    small_sum = _gather8(_pack_rows(pieces), True, "reduce_small")
    layout = []
    for l in range(nl):
        layout += [(1, w) for _, w in SMALL] + [(SSM_CONV, SSM_CONV_DIM)]
    layout.append((1, 1024))
    red = _unpack_rows(small_sum, layout)
    per = len(SMALL) + 1
    g_small = {k: jnp.concatenate([red[l * per + i] for l in range(nl)], axis=0) for i, (k, _) in enumerate(SMALL)}
    g_convw_full = jnp.stack([red[l * per + len(SMALL)] for l in range(nl)])
    g_small["conv_w"] = lax.dynamic_slice_in_dim(g_convw_full, blk * 1024, 1024, axis=2)
    g_small["norm_final"] = red[-1][0]

    P = [jnp.stack(p1), jnp.stack(pin), jnp.stack(pkv)]
    half = nl // 2
    c = lax.axis_index("c")
    got = _sibling_swap(P, "grads_core_swap")
    loff = (c * half).astype(jnp.int32)[None]
    blk = blk.astype(jnp.int32)
    sel_own = blk[None]
    sel_rem = jnp.stack([blk ^ 1, blk ^ 2, blk ^ 3])
    own = [_pair_sum(p, g_, sel_own, loff, F32, "chip_sum_own")[0] for p, g_ in zip(P, got)]
    out_b = [_pair_sum(p, g_, sel_rem, loff, BF16, "chip_sum_send") for p, g_ in zip(P, got)]
    inc = _chip_exchange(out_b, "grads_chip_exchange")
    red_half = [_sum_cast([o, i_[0], i_[1], i_[2]], F32, "grads_total") for o, i_ in zip(own, inc)]
    g_pack, g_in, g_kv = _sibling_join(red_half, "grads_core_join")

    grads = dict(g_small)
    for k, r in PACK_ROWS:
        o, _ = offs[k]
        grads[k] = g_pack[:, o:o + r]
    grads["w_in"] = g_in
    grads["xa_wkv"] = g_kv

    delta, new_m, new_v = {}, {}, {}
    for k in ["w_in", "xa_wkv"] + [k for k, _ in PACK_ROWS]:
        delta[k], new_m[k], new_v[k] = _adamw(W[k], grads[k], M[k], V[k], "adamw_" + k)
    small_names = [k for k, _ in SMALL] + ["conv_w", "norm_final"]

    def pack_small(src):
        ps = []
        for k in small_names:
            a = src[k]
            ps.append(a.reshape(-1, a.shape[-1]) if a.ndim > 1 else a[None])
        return _pack_rows(ps)

    ds_, ms_, vs_ = _adamw(pack_small(W), pack_small(grads), pack_small(M), pack_small(V), "adamw_small")
    lay2 = []
    for k in small_names:
        a = W[k]
        lay2.append((int(np.prod(a.shape[:-1])) if a.ndim > 1 else 1, a.shape[-1]))
    for src, dst in ((ds_, delta), (ms_, new_m), (vs_, new_v)):
        for k, piece in zip(small_names, _unpack_rows(src, lay2)):
            dst[k] = piece.reshape(W[k].shape)

    names = ["norm_mix", "w_in", "b_gate", "conv_w", "conv_b", "dt_bias", "a_log", "d_skip", "ssm_norm", "w_br_ret",
             "w_br_ssm", "w_out", "norm_xa", "norm_mem", "xa_wq", "xa_wkv", "xa_wo", "norm_mlp", "mlp_w1", "mlp_w2",
             "norm_final"]
    return (loss, grad_x, *[grads[n] for n in names], *[delta[n] for n in names], *[new_m[n] for n in names],
            *[new_v[n] for n in names])
```

```python
import numpy as np
import jax
import jax.numpy as jnp
from jax import lax
from jax.experimental import pallas as pl
from jax.experimental.pallas import tpu as pltpu

F32 = jnp.float32
BF16 = jnp.bfloat16
MESH = pl.DeviceIdType.MESH

D_MODEL = 1024
CHUNK = 64
EPS = 1e-6
RET_HEADS = 4
RET_QK_DIM = 128
RET_V_DIM = 256
RET_QK = 512
RET_V = 1024
ROPE_THETA = 10000.0
SSM_INNER = 2048
SSM_HEADS = 32
SSM_GROUPS = 8
SSM_STATE = 128
SSM_CONV = 4
SSM_BC = 1024
SSM_CONV_DIM = 4096
XA_HEADS = 4
XA_HEAD_DIM = 256
D_FF = 4096
GROUP_W = 256

DT_PAD = 1024
IN_DIM = 11296
NP = 12288
C_XBC, C_Q, C_K, C_DT, C_Z, C_GATES, C_V, C_G = 0, 4096, 4608, 5120, 6144, 8192, 10240, 11264
O_Q, O_K, O_V, O_G, O_Z, O_XBC, O_DT, O_GATES = 0, 512, 1024, 2048, 3072, 5120, 9216, 9248

ADAM_LR = 0.001
ADAM_B1 = 0.9
ADAM_B2 = 0.999
ADAM_EPS = 1e-08
ADAM_WD = 0.01
ADAM_STEP = 10

VMEM_LIMIT = 56 * 1024 * 1024


def _params(*sem):
    return pltpu.CompilerParams(dimension_semantics=sem, vmem_limit_bytes=VMEM_LIMIT)


def _pcall(body, **kw):
    return pl.pallas_call(body, **kw)


def _bf(a):
    return a.astype(BF16)


def _dot(a, b):
    return jnp.dot(_bf(a), _bf(b), preferred_element_type=F32)


def _dot_nt(a, b):
    return lax.dot_general(_bf(a), _bf(b), (((1,), (1,)), ((), ())), preferred_element_type=F32)


def _dot_tn(a, b):
    return lax.dot_general(_bf(a), _bf(b), (((0,), (0,)), ((), ())), preferred_element_type=F32)


def _colsum(a):
    return jnp.sum(a, axis=0, keepdims=True)


def _rstd(x):
    return lax.rsqrt(jnp.mean(x * x, axis=-1, keepdims=True) + EPS)


def _rms_bwd(dy, x, rstd):
    xh = x * rstd
    return rstd * (dy - xh * jnp.mean(dy * xh, axis=-1, keepdims=True))


def _sigmoid(x):
    return 1.0 / (1.0 + jnp.exp(-x))


def _silu_and_grad(x):
    s = _sigmoid(x)
    return x * s, s + x * s * (1.0 - s)


def _softplus(x):
    u = jnp.exp(-jnp.abs(x))
    l1p = jnp.where(u < 1e-4, u * (1.0 - 0.5 * u), jnp.log(1.0 + u))
    return jnp.maximum(x, 0.0) + l1p


def _split3_dot(a, e):
    hi = a.astype(BF16)
    r1 = a - hi.astype(F32)
    mid = r1.astype(BF16)
    lo = (r1 - mid.astype(F32)).astype(BF16)
    return (jnp.dot(hi, e, preferred_element_type=F32) + jnp.dot(mid, e, preferred_element_type=F32)
            + jnp.dot(lo, e, preferred_element_type=F32))


def _cumsum_rows(a):
    rows = lax.broadcasted_iota(jnp.int32, a.shape, 0)
    s = 1
    while s < a.shape[0]:
        a = a + jnp.where(rows >= s, pltpu.roll(a, s, 0), 0.0)
        s *= 2
    return a


def _revcumsum_rows(a):
    n = a.shape[0]
    rows = lax.broadcasted_iota(jnp.int32, a.shape, 0)
    s = 1
    while s < n:
        a = a + jnp.where(rows < n - s, pltpu.roll(a, n - s, 0), 0.0)
        s *= 2
    return a


def _rms_groups(y, width):
    out = []
    for h in range(y.shape[1] // width):
        slab = y[:, h * width:(h + 1) * width]
        out.append((slab, _rstd(slab)))
    return out


def _ret_constants():
    idx = np.arange(CHUNK, dtype=np.float32)
    lg = np.log1p(-(np.float32(2.0) ** (np.float32(-5.0) - np.arange(RET_HEADS, dtype=np.float32)))).astype(np.float32)
    rel = np.abs(idx[:, None] - idx[None, :])
    dm = np.exp(lg[:, None, None] * rel).astype(np.float32)
    qd = np.exp(lg[None, :] * (idx[:, None] + 1.0)).astype(np.float32)
    kd = np.exp(lg[None, :] * (CHUNK - 1.0 - idx[:, None])).astype(np.float32)
    cd = np.exp(lg * CHUNK).astype(np.float32)
    qd = np.repeat(qd, RET_QK_DIM, axis=1)
    kd = np.repeat(kd, RET_QK_DIM, axis=1)
    cd = np.repeat(cd, RET_QK_DIM)[:, None] * np.ones((1, RET_V_DIM), np.float32)
    return dm, qd, kd, cd.astype(np.float32)


def _ssd_constants():
    eye = np.tile(np.eye(CHUNK, dtype=np.float32), (1, GROUP_W // CHUNK))
    blk = np.kron(np.eye(GROUP_W // CHUNK, dtype=np.float32), np.ones((CHUNK, CHUNK), np.float32))
    return eye, blk


def _head_expand():
    e = np.zeros((128, SSM_INNER), np.float32)
    for h in range(SSM_HEADS):
        e[h, h * 64:(h + 1) * 64] = 1.0
    return e


def _ret_chunk_fwd(qh, kh, vh, sh, dmh, qdh, kdh, cdh):
    a = _dot_nt(qh, kh) * dmh
    y = _dot(a, vh) + _dot(qh * qdh, sh)
    s_new = sh * cdh + _dot_tn(kh * kdh, vh)
    return y, s_new


def _ret_chunk_bwd(qh, kh, vh, sh, dmh, qdh, kdh, cdh, dy, ds_new):
    a = _dot_nt(qh, kh) * dmh
    dp = _dot_nt(dy, vh) * dmh
    dq = _dot(dp, kh) + _dot_nt(dy, sh) * qdh
    dk = _dot_tn(dp, qh) + _dot_nt(vh, ds_new) * kdh
    dv = _dot_tn(a, dy) + _dot(kh * kdh, ds_new)
    ds = cdh * ds_new + _dot_tn(qh * qdh, dy)
    return dq, dk, dv, ds


def _ssd_common(xs, dtx, ax, eye):
    cum = _cumsum_rows(dtx * ax)
    last = cum[CHUNK - 1:CHUNK, :]
    r = _colsum(jnp.where(eye > 0.5, cum, 0.0))
    return cum, last, r, xs * dtx


def _tile4(a):
    return jnp.concatenate([a, a, a, a], axis=0)


def _ssd_chunk_fwd(xs, dtx, b, c, ax, hg, eye, blk):
    cum, last, r, x = _ssd_common(xs, dtx, ax, eye)
    lam = jnp.exp(-jnp.abs(cum - r))
    wc = _dot_nt(c, _tile4(b)) * lam
    bd = _tile4(x) * blk
    y = _dot(wc, bd) + _dot(c, hg) * jnp.exp(cum)
    h_new = hg * jnp.exp(last) + _dot_tn(b, x * jnp.exp(last - cum))
    return y, h_new


def _ssd_chunk_bwd(xs, dtx, b, c, ax, hg, eye, blk, dy, dh_new):
    cum, last, r, x = _ssd_common(xs, dtx, ax, eye)
    delta = cum - r
    lam = jnp.exp(-jnp.abs(delta))
    b4 = _tile4(b)
    cb4 = _dot_nt(c, b4)
    wc = cb4 * lam
    bd = _tile4(x) * blk
    ecx = jnp.exp(cum)
    wl = jnp.exp(last - cum)
    ecl = jnp.exp(last)
    z = _dot(c, hg)
    dwc = _dot_nt(dy, bd)
    dbd = _dot_tn(wc, dy) * blk
    dx = dbd[0:64] + dbd[64:128] + dbd[128:192] + dbd[192:256]
    dt_ = _dot(b, dh_new)
    dx = dx + dt_ * wl
    dcb4 = dwc * lam
    dz = dy * ecx
    dc = _dot(dcb4, b4) + _dot_nt(dz, hg)
    db4 = _dot_tn(dcb4, c)
    db = db4[0:64] + db4[64:128] + db4[128:192] + db4[192:256] + _dot_nt(x * wl, dh_new)
    g = dwc * cb4 * lam * (-jnp.sign(delta))
    dr = -_colsum(g)
    dwl = dt_ * x * wl
    u = g + eye * dr + dy * z * ecx - dwl
    lastrow = _colsum(dwl) + _colsum(dh_new * hg) * ecl
    rows = lax.broadcasted_iota(jnp.int32, u.shape, 0)
    u = u + jnp.where(rows == CHUNK - 1, lastrow, 0.0)
    dh = _dot_tn(c, dz) + dh_new * ecl
    rc = _revcumsum_rows(u)
    dxs = dx * dtx
    g_dtx = dx * xs + rc * ax
    da = _colsum(rc * dtx)
    return dxs, g_dtx, db, dc, da, dh


def _row_tile(s, want):
    t = min(s, want)
    assert s % t == 0
    return t


def _nmm(x, gain, w, name, tn=1024, save_u=False):
    s, d = x.shape
    n = w.shape[1]
    tm = _row_tile(s, 1024)
    assert n % tn == 0

    def body(x_ref, g_ref, w_ref, *rest):
        o_ref, u_sc = rest[0], rest[-1]

        @pl.when(pl.program_id(1) == 0)
        def _():
            xx = x_ref[...]
            u = _bf((xx * _rstd(xx)) * g_ref[...])
            u_sc[...] = u
            if save_u:
                rest[1][...] = u

        o_ref[...] = jnp.dot(u_sc[...], w_ref[...], preferred_element_type=F32)

    out_shape = [jax.ShapeDtypeStruct((s, n), F32)]
    out_specs = [pl.BlockSpec((tm, tn), lambda i, j: (i, j))]
    if save_u:
        out_shape.append(jax.ShapeDtypeStruct((s, d), BF16))
        out_specs.append(pl.BlockSpec((tm, d), lambda i, j: (i, 0)))
    res = _pcall(
        body, grid=(s // tm, n // tn),
        in_specs=[pl.BlockSpec((tm, d), lambda i, j: (i, 0)), pl.BlockSpec((1, d), lambda i, j: (0, 0)),
                  pl.BlockSpec((d, tn), lambda i, j: (0, j))],
        out_specs=out_specs, out_shape=out_shape, scratch_shapes=[pltpu.VMEM((tm, d), BF16)],
        compiler_params=_params("parallel", "arbitrary"), name=name)(x, gain, w)
    return res if save_u else res[0]


def _mm_tn(a, b, name, tm=1024, tn=1024):
    k, m = a.shape
    n = b.shape[1]
    tk = _row_tile(k, 1024)
    tm, tn = min(tm, m), min(tn, n)
    assert m % tm == 0 and n % tn == 0
    nk = k // tk

    def body(a_ref, b_ref, o_ref, acc):
        kk = pl.program_id(2)

        @pl.when(kk == 0)
        def _():
            acc[...] = jnp.zeros_like(acc)

        acc[...] += _dot_tn(a_ref[...], b_ref[...])

        @pl.when(kk == nk - 1)
        def _():
            o_ref[...] = acc[...]

    return _pcall(
        body, grid=(m // tm, n // tn, nk),
        in_specs=[pl.BlockSpec((tk, tm), lambda i, j, kk: (kk, i)), pl.BlockSpec((tk, tn), lambda i, j, kk: (kk, j))],
        out_specs=pl.BlockSpec((tm, tn), lambda i, j, kk: (i, j)), out_shape=jax.ShapeDtypeStruct((m, n), F32),
        scratch_shapes=[pltpu.VMEM((tm, tn), F32)],
        compiler_params=_params("parallel", "parallel", "arbitrary"), name=name)(a, b)


def _in_bwd(dproj, wcat, x, gain, dres, name):
    s, n = dproj.shape
    d = wcat.shape[0]
    tm = _row_tile(s, 1024)
    tk = 1024
    nk = n // tk
    ns = s // tm

    def body(dp_ref, w_ref, x_ref, g_ref, dr_ref, dx_ref, dg_ref, acc):
        i, kk = pl.program_id(0), pl.program_id(1)

        @pl.when(kk == 0)
        def _():
            acc[...] = jnp.zeros_like(acc)

        @pl.when((kk == 0) & (i == 0))
        def _():
            dg_ref[...] = jnp.zeros_like(dg_ref)

        acc[...] += _dot_nt(dp_ref[...], w_ref[...])

        @pl.when(kk == nk - 1)
        def _():
            xx = x_ref[...]
            r = _rstd(xx)
            du = acc[...]
            dg_ref[...] += _colsum(du * (xx * r))
            dx_ref[...] = dr_ref[...] + _rms_bwd(du * g_ref[...], xx, r)

    return _pcall(
        body, grid=(ns, nk),
        in_specs=[pl.BlockSpec((tm, tk), lambda i, kk: (i, kk)), pl.BlockSpec((d, tk), lambda i, kk: (0, kk)),
                  pl.BlockSpec((tm, d), lambda i, kk: (i, 0)), pl.BlockSpec((1, d), lambda i, kk: (0, 0)),
                  pl.BlockSpec((tm, d), lambda i, kk: (i, 0))],
        out_specs=[pl.BlockSpec((tm, d), lambda i, kk: (i, 0)), pl.BlockSpec((8, d), lambda i, kk: (0, 0))],
        out_shape=[jax.ShapeDtypeStruct((s, d), F32), jax.ShapeDtypeStruct((8, d), F32)],
        scratch_shapes=[pltpu.VMEM((tm, d), F32)],
        compiler_params=_params("arbitrary", "arbitrary"), name=name)(dproj, wcat, x, gain, dres)


def _conv_pre(xcat, cw_ref, cb_ref, ts):
    pre = cb_ref[...] + cw_ref[3:4, :] * xcat[8:8 + ts]
    for j in range(3):
        pre = pre + cw_ref[j:j + 1, :] * pltpu.roll(xcat, 3 - j, 0)[8:8 + ts]
    return pre


def _prev_rows_spec(ts, width):
    return pl.BlockSpec((8, width), lambda i: (jnp.maximum(i * (ts // 8) - 1, 0), 0))


def _prescan(proj, cosf, sinf, cw, cb, dtb, eexp, name):
    s = proj.shape[0]
    ts = _row_tile(s, 256)

    def body(xbc_ref, prev_ref, q_ref, k_ref, dt_ref, cos_ref, sin_ref, cw_ref, cb_ref, dtb_ref, e_ref,
             qo_ref, ko_ref, xc_ref, dtx_ref):
        i = pl.program_id(0)
        prev = jnp.where(i > 0, prev_ref[...], 0.0)
        xcat = jnp.concatenate([prev, xbc_ref[...]], axis=0)
        pre = _conv_pre(xcat, cw_ref, cb_ref, ts)
        xc_ref[...] = pre * _sigmoid(pre)
        cs, sn = cos_ref[...], sin_ref[...]
        for h in range(RET_HEADS):
            sl = slice(h * 128, (h + 1) * 128)
            qh, kh = q_ref[:, sl], k_ref[:, sl]
            qo_ref[:, sl] = qh * cs + pltpu.roll(qh, 64, 1) * sn
            ko_ref[:, sl] = (kh * cs + pltpu.roll(kh, 64, 1) * sn) * (RET_QK_DIM ** -0.5)
        dtv = _softplus(dt_ref[:, 0:128] + dtb_ref[...])
        dtx_ref[...] = _split3_dot(dtv, e_ref[...])

    row = lambda w, c: pl.BlockSpec((ts, w), lambda i: (i, c))
    full = lambda a: pl.BlockSpec(a.shape, lambda i: (0,) * a.ndim)
    return _pcall(
        body, grid=(s // ts,),
        in_specs=[row(4096, 0), _prev_rows_spec(ts, 4096), row(512, C_Q // 512), row(512, C_K // 512),
                  row(DT_PAD, C_DT // DT_PAD), row(128, 0), row(128, 0), full(cw), full(cb), full(dtb), full(eexp)],
        out_specs=[row(512, 0), row(512, 0), row(4096, 0), row(2048, 0)],
        out_shape=[jax.ShapeDtypeStruct((s, 512), F32), jax.ShapeDtypeStruct((s, 512), F32),
                   jax.ShapeDtypeStruct((s, 4096), F32), jax.ShapeDtypeStruct((s, 2048), F32)],
        compiler_params=_params("parallel"), name=name)(proj, proj, proj, proj, proj, cosf, sinf, cw, cb, dtb, eexp)


def _scan_fwd(qr, kr, proj, xc, dtx, ax, consts, name):
    s = qr.shape[0]
    nc = s // CHUNK
    dm, qd, kd, cd, eye, blk = consts

    def body(q_ref, k_ref, v_ref, xc_ref, dtx_ref, ax_ref, dm_ref, qd_ref, kd_ref, cd_ref, eye_ref, blk_ref,
             yr_ref, ys_ref, sst_ref, hst_ref, s_sc, h_sc):
        @pl.when(pl.program_id(0) == 0)
        def _():
            s_sc[...] = jnp.zeros_like(s_sc)
            h_sc[...] = jnp.zeros_like(h_sc)

        sst_ref[0] = s_sc[...]
        hst_ref[0] = h_sc[...]
        for h in range(RET_HEADS):
            ql, vl = slice(h * 128, (h + 1) * 128), slice(h * 256, (h + 1) * 256)
            y, s_new = _ret_chunk_fwd(q_ref[:, ql], k_ref[:, ql], v_ref[:, vl], s_sc[ql, :], dm_ref[h],
                                      qd_ref[:, ql], kd_ref[:, ql], cd_ref[ql, :])
            yr_ref[:, vl] = y
            s_sc[ql, :] = s_new
        eye_v, blk_v = eye_ref[...], blk_ref[...]
        for g in range(SSM_GROUPS):
            sl = slice(g * GROUP_W, (g + 1) * GROUP_W)
            bl = slice(SSM_INNER + g * 128, SSM_INNER + (g + 1) * 128)
            cl = slice(SSM_INNER + SSM_BC + g * 128, SSM_INNER + SSM_BC + (g + 1) * 128)
            y, h_new = _ssd_chunk_fwd(xc_ref[:, sl], dtx_ref[:, sl], xc_ref[:, bl], xc_ref[:, cl], ax_ref[:, sl],
                                      h_sc[:, sl], eye_v, blk_v)
            ys_ref[:, sl] = y
            h_sc[:, sl] = h_new

    row = lambda w, c=0: pl.BlockSpec((CHUNK, w), lambda i: (i, c))
    full = lambda a: pl.BlockSpec(a.shape, lambda i: (0,) * a.ndim)
    return _pcall(
        body, grid=(nc,),
        in_specs=[row(512), row(512), row(1024, C_V // 1024), row(4096), row(2048), full(ax), full(dm), full(qd),
                  full(kd), full(cd), full(eye), full(blk)],
        out_specs=[row(1024), row(2048), pl.BlockSpec((1, 512, 256), lambda i: (i, 0, 0)),
                   pl.BlockSpec((1, 128, 2048), lambda i: (i, 0, 0))],
        out_shape=[jax.ShapeDtypeStruct((s, 1024), F32), jax.ShapeDtypeStruct((s, 2048), F32),
                   jax.ShapeDtypeStruct((nc, 512, 256), F32), jax.ShapeDtypeStruct((nc, 128, 2048), F32)],
        scratch_shapes=[pltpu.VMEM((512, 256), F32), pltpu.VMEM((128, 2048), F32)],
        compiler_params=_params("arbitrary"), name=name)(qr, kr, proj, xc, dtx, ax, dm, qd, kd, cd, eye, blk)


def _scan_bwd(qr, kr, proj, xc, dtx, ax, consts, sst, hst, dyr, dys, dproj, name):
    s = qr.shape[0]
    nc = s // CHUNK
    dm, qd, kd, cd, eye, blk = consts

    def body(q_ref, k_ref, v_ref, xc_ref, dtx_ref, ax_ref, dm_ref, qd_ref, kd_ref, cd_ref, eye_ref, blk_ref,
             sst_ref, hst_ref, dyr_ref, dys_ref, dproj_in, dq_ref, dk_ref, dv_ref, dxc_ref, gdt_ref, da_ref, ds_sc,
             dh_sc):
        @pl.when(pl.program_id(0) == 0)
        def _():
            ds_sc[...] = jnp.zeros_like(ds_sc)
            dh_sc[...] = jnp.zeros_like(dh_sc)
            da_ref[...] = jnp.zeros_like(da_ref)

        for h in range(RET_HEADS):
            ql, vl = slice(h * 128, (h + 1) * 128), slice(h * 256, (h + 1) * 256)
            dq, dk, dv, ds = _ret_chunk_bwd(q_ref[:, ql], k_ref[:, ql], v_ref[:, vl], sst_ref[0, ql, :], dm_ref[h],
                                            qd_ref[:, ql], kd_ref[:, ql], cd_ref[ql, :], dyr_ref[:, vl], ds_sc[ql, :])
            dq_ref[:, ql] = dq
            dk_ref[:, ql] = dk
            dv_ref[:, vl] = _bf(dv)
            ds_sc[ql, :] = ds
        eye_v, blk_v = eye_ref[...], blk_ref[...]
        for g in range(SSM_GROUPS):
            sl = slice(g * GROUP_W, (g + 1) * GROUP_W)
            bl = slice(SSM_INNER + g * 128, SSM_INNER + (g + 1) * 128)
            cl = slice(SSM_INNER + SSM_BC + g * 128, SSM_INNER + SSM_BC + (g + 1) * 128)
            dxs, g_dtx, db, dc, da, dh = _ssd_chunk_bwd(
                xc_ref[:, sl], dtx_ref[:, sl], xc_ref[:, bl], xc_ref[:, cl], ax_ref[:, sl], hst_ref[0, :, sl],
                eye_v, blk_v, dys_ref[:, sl], dh_sc[:, sl])
            dxc_ref[:, sl] = dxs
            dxc_ref[:, bl] = db
            dxc_ref[:, cl] = dc
            gdt_ref[:, sl] = g_dtx
            da_ref[:, sl] += da
            dh_sc[:, sl] = dh

    row = lambda w, c=0: pl.BlockSpec((CHUNK, w), lambda i: (nc - 1 - i, c))
    full = lambda a: pl.BlockSpec(a.shape, lambda i: (0,) * a.ndim)
    return _pcall(
        body, grid=(nc,),
        in_specs=[row(512), row(512), row(1024, C_V // 1024), row(4096), row(2048), full(ax), full(dm), full(qd),
                  full(kd), full(cd), full(eye), full(blk),
                  pl.BlockSpec((1, 512, 256), lambda i: (nc - 1 - i, 0, 0)),
                  pl.BlockSpec((1, 128, 2048), lambda i: (nc - 1 - i, 0, 0)), row(1024), row(2048), ANY],
        out_specs=[row(512), row(512), row(1024, C_V // 1024), row(4096), row(2048),
                   pl.BlockSpec((1, 2048), lambda i: (0, 0))],
        out_shape=[jax.ShapeDtypeStruct((s, 512), F32), jax.ShapeDtypeStruct((s, 512), F32),
                   jax.ShapeDtypeStruct(dproj.shape, BF16), jax.ShapeDtypeStruct((s, 4096), F32),
                   jax.ShapeDtypeStruct((s, 2048), F32), jax.ShapeDtypeStruct((1, 2048), F32)],
        scratch_shapes=[pltpu.VMEM((512, 256), F32), pltpu.VMEM((128, 2048), F32)],
        input_output_aliases={16: 2},
        compiler_params=_params("arbitrary"), name=name)(qr, kr, proj, xc, dtx, ax, dm, qd, kd, cd, eye, blk, sst, hst,
                                                          dyr, dys, dproj)


def _mix_values(yr, g, ys, xs, z, gates, bg, dsk, sn):
    sg, dsg = _silu_and_grad(g)
    ret = _rms_groups(yr, RET_V_DIM)
    yrn = jnp.concatenate([slab * r for slab, r in ret], axis=1) * sg
    sz, dsz = _silu_and_grad(z)
    ys0 = ys + xs * dsk
    ys1 = ys0 * sz
    grp = _rms_groups(ys1, GROUP_W)
    ysh = jnp.concatenate([slab * r for slab, r in grp], axis=1)
    ysn = ysh * sn
    gg = _sigmoid(gates + bg)
    return dict(sg=sg, dsg=dsg, ret=ret, yrn=yrn, sz=sz, dsz=dsz, ys0=ys0, ys1=ys1, grp=grp, ysh=ysh, ysn=ysn,
                gr=gg[:, :D_MODEL], gs=gg[:, D_MODEL:])


def _postscan_fwd(x, yr, ys, xc, proj, bg, dsk, sn, wr, ws, wo, name):
    s = x.shape[0]
    ts = _row_tile(s, 256)

    def body(x_ref, yr_ref, ys_ref, xs_ref, g_ref, z_ref, gt_ref, bg_ref, dsk_ref, sn_ref, wr_ref, ws_ref, wo_ref,
             o_ref):
        m = _mix_values(yr_ref[...], g_ref[...], ys_ref[...], xs_ref[...], z_ref[...], gt_ref[...], bg_ref[...],
                        dsk_ref[...], sn_ref[...])
        merged = m["gr"] * _dot(m["yrn"], wr_ref[...]) + m["gs"] * _dot(m["ysn"], ws_ref[...])
        o_ref[...] = x_ref[...] + _dot(merged, wo_ref[...])

    row = lambda w, c=0: pl.BlockSpec((ts, w), lambda i: (i, c))
    full = lambda a: pl.BlockSpec(a.shape, lambda i: (0,) * a.ndim)
    return _pcall(
        body, grid=(s // ts,),
        in_specs=[row(1024), row(1024), row(2048), row(2048), row(1024, C_G // 1024), row(2048, C_Z // 2048),
                  row(2048, C_GATES // 2048), full(bg), full(dsk), full(sn), full(wr), full(ws), full(wo)],
        out_specs=row(1024), out_shape=jax.ShapeDtypeStruct((s, D_MODEL), F32),
        compiler_params=_params("parallel"), name=name)(x, yr, ys, xc, proj, proj, proj, bg, dsk, sn, wr, ws, wo)


def _postscan_bwd(dout, yr, ys, xc, proj, bg, dsk, sn, wr, ws, wo, name):
    s = dout.shape[0]
    ts = _row_tile(s, 128)

    def body(do_ref, yr_ref, ys_ref, xs_ref, g_ref, z_ref, gt_ref, bg_ref, dsk_ref, sn_ref, wr_ref, ws_ref, wo_ref,
             dyr_ref, dys_ref, dxs_ref, dproj_ref, yrn_ref, ysn_ref, mg_ref, dbr_ref, dbs_ref,
             dbg_ref, ddsk_ref, dsn_ref):
        @pl.when(pl.program_id(0) == 0)
        def _():
            dbg_ref[...] = jnp.zeros_like(dbg_ref)
            ddsk_ref[...] = jnp.zeros_like(ddsk_ref)
            dsn_ref[...] = jnp.zeros_like(dsn_ref)

        xs = xs_ref[...]
        m = _mix_values(yr_ref[...], g_ref[...], ys_ref[...], xs, z_ref[...], gt_ref[...], bg_ref[...],
                        dsk_ref[...], sn_ref[...])
        gr, gs = m["gr"], m["gs"]
        br, bs = _dot(m["yrn"], wr_ref[...]), _dot(m["ysn"], ws_ref[...])
        dmerged = _dot_nt(do_ref[...], wo_ref[...])
        dgt = jnp.concatenate([dmerged * br * gr * (1.0 - gr), dmerged * bs * gs * (1.0 - gs)], axis=1)
        dproj_ref[:, C_GATES:C_GATES + 2048] = _bf(dgt)
        dbg_ref[...] += _colsum(dgt)
        dbr, dbs = dmerged * gr, dmerged * gs
        yrn_ref[...] = _bf(m["yrn"])
        ysn_ref[...] = _bf(m["ysn"])
        mg_ref[...] = _bf(gr * br + gs * bs)
        dbr_ref[...] = _bf(dbr)
        dbs_ref[...] = _bf(dbs)
        dyrn = _dot_nt(dbr, wr_ref[...])
        dysn = _dot_nt(dbs, ws_ref[...])
        rn = jnp.concatenate([slab * r for slab, r in m["ret"]], axis=1)
        dproj_ref[:, C_G:C_G + 1024] = _bf(dyrn * rn * m["dsg"])
        drn = dyrn * m["sg"]
        dyr_ref[...] = jnp.concatenate(
            [_rms_bwd(drn[:, h * RET_V_DIM:(h + 1) * RET_V_DIM], slab, r) for h, (slab, r) in enumerate(m["ret"])], axis=1)
        dsn_ref[...] += _colsum(dysn * m["ysh"])
        dysh = dysn * sn_ref[...]
        dys1 = jnp.concatenate(
            [_rms_bwd(dysh[:, h * GROUP_W:(h + 1) * GROUP_W], slab, r) for h, (slab, r) in enumerate(m["grp"])], axis=1)
        dproj_ref[:, C_Z:C_Z + 2048] = _bf(dys1 * m["ys0"] * m["dsz"])
        dys0 = dys1 * m["sz"]
        dys_ref[...] = dys0
        dxs_ref[...] = dys0 * dsk_ref[...]
        ddsk_ref[...] += _colsum(dys0 * xs)

    row = lambda w, c=0: pl.BlockSpec((ts, w), lambda i: (i, c))
    full = lambda a: pl.BlockSpec(a.shape, lambda i: (0,) * a.ndim)
    acc = lambda w: pl.BlockSpec((8, w), lambda i: (0, 0))
    sds = jax.ShapeDtypeStruct
    return _pcall(
        body, grid=(s // ts,),
        in_specs=[row(1024), row(1024), row(2048), row(2048), row(1024, C_G // 1024), row(2048, C_Z // 2048),
                  row(2048, C_GATES // 2048), full(bg), full(dsk), full(sn), full(wr), full(ws), full(wo)],
        out_specs=[row(1024), row(2048), row(2048), row(NP), row(1024), row(2048), row(1024),
                   row(1024), row(1024), acc(2048), acc(2048), acc(2048)],
        out_shape=[sds((s, 1024), F32), sds((s, 2048), F32), sds((s, 2048), F32), sds((s, NP), BF16),
                   sds((s, 1024), BF16), sds((s, 2048), BF16),
                   sds((s, 1024), BF16), sds((s, 1024), BF16), sds((s, 1024), BF16), sds((8, 2048), F32),
                   sds((8, 2048), F32), sds((8, 2048), F32)],
        compiler_params=_params("arbitrary"), name=name)(dout, yr, ys, xc, proj, proj, proj, bg, dsk, sn, wr, ws, wo)


def _prescan_bwd(proj, dxc, dxs_skip, gdtx, dqr, dkr, cosf, sinf, cw, cb, dtb, eexp_t, dproj, name):
    s = proj.shape[0]
    ts = _row_tile(s, 128)
    nt = s // ts
    m = ts + 8
    width = C_DT + DT_PAD

    def body(xbc_ref, prev_ref, nxt_ref, dt_ref, dxc_ref, dxcn_ref, dsk_ref, dskn_ref, gdt_ref, dq_ref, dk_ref,
             cos_ref, sin_ref, cw_ref, cb_ref, dtb_ref, et_ref, dproj_in, dp_ref, dcw_ref, dcb_ref, ddtb_ref):
        i = pl.program_id(0)

        @pl.when(i == 0)
        def _():
            ddtb_ref[...] = jnp.zeros_like(ddtb_ref)
            dcw_ref[...] = jnp.zeros_like(dcw_ref)
            dcb_ref[...] = jnp.zeros_like(dcb_ref)

        prev = jnp.where(i > 0, prev_ref[...], 0.0)
        xcat = jnp.concatenate([prev, xbc_ref[...], nxt_ref[...]], axis=0)
        shifted = [pltpu.roll(xcat, 3 - j, 0) for j in range(3)] + [xcat]
        pre = cb_ref[...]
        for j in range(SSM_CONV):
            pre = pre + cw_ref[j:j + 1, :] * shifted[j][8:]
        _, dsilu = _silu_and_grad(pre)
        dxc = jnp.concatenate([dxc_ref[...], dxcn_ref[...]], axis=0)
        dsk = jnp.concatenate([dsk_ref[...], dskn_ref[...]], axis=0)
        dpre = jnp.concatenate([(dxc[:, :SSM_INNER] + dsk) * dsilu[:, :SSM_INNER],
                                dxc[:, SSM_INNER:] * dsilu[:, SSM_INNER:]], axis=1)
        rows = lax.broadcasted_iota(jnp.int32, dpre.shape, 0)
        dpre = jnp.where((rows < ts) | (i < nt - 1), dpre, 0.0)
        dpt = dpre[0:ts]
        dx = cw_ref[3:4, :] * dpt
        for j in range(3):
            dx = dx + cw_ref[j:j + 1, :] * pltpu.roll(dpre, m - (3 - j), 0)[0:ts]
        for j in range(SSM_CONV):
            dcw_ref[8 * j:8 * j + 8, :] += _colsum(dpt * shifted[j][8:8 + ts])
        dcb_ref[...] += _colsum(dpt)
        dp_ref[:, C_XBC:C_XBC + SSM_CONV_DIM] = _bf(dx)
        cs, sn = cos_ref[...], sin_ref[...]
        for h in range(RET_HEADS):
            sl = slice(h * 128, (h + 1) * 128)
            dq = dq_ref[:, sl]
            dk = dk_ref[:, sl] * (RET_QK_DIM ** -0.5)
            dp_ref[:, C_Q + h * 128:C_Q + (h + 1) * 128] = _bf(dq * cs + pltpu.roll(dq * sn, 64, 1))
            dp_ref[:, C_K + h * 128:C_K + (h + 1) * 128] = _bf(dk * cs + pltpu.roll(dk * sn, 64, 1))
        ddt = _split3_dot(gdt_ref[...], et_ref[...])
        ddt = ddt * _sigmoid(dt_ref[:, 0:128] + dtb_ref[...])
        ddtb_ref[...] += _colsum(ddt)
        dp_ref[:, C_DT:C_DT + 128] = _bf(ddt)
        dp_ref[:, C_DT + 128:C_DT + DT_PAD] = jnp.zeros((ts, DT_PAD - 128), BF16)

    row = lambda w, c=0: pl.BlockSpec((ts, w), lambda i: (i, c))
    nxt = lambda w: pl.BlockSpec((8, w), lambda i: (jnp.minimum((i + 1) * (ts // 8), s // 8 - 1), 0))
    full = lambda a: pl.BlockSpec(a.shape, lambda i: (0,) * a.ndim)
    sds = jax.ShapeDtypeStruct
    return _pcall(
        body, grid=(nt,),
        in_specs=[row(4096), _prev_rows_spec(ts, 4096), nxt(4096), row(DT_PAD, C_DT // DT_PAD), row(4096), nxt(4096),
                  row(2048), nxt(2048), row(2048), row(512), row(512), row(128), row(128), full(cw), full(cb),
                  full(dtb), full(eexp_t), ANY],
        out_specs=[row(width), pl.BlockSpec((32, 4096), lambda i: (0, 0)), pl.BlockSpec((8, 4096), lambda i: (0, 0)),
                   pl.BlockSpec((8, 128), lambda i: (0, 0))],
        out_shape=[sds(dproj.shape, BF16), sds((32, 4096), F32), sds((8, 4096), F32), sds((8, 128), F32)],
        input_output_aliases={17: 0},
        compiler_params=_params("arbitrary"), name=name)(proj, proj, proj, proj, dxc, dxc, dxs_skip, dxs_skip, gdtx,
                                                          dqr, dkr, cosf, sinf, cw, cb, dtb, eexp_t, dproj)


def _xattn_values(x, gain, wq, kv):
    r = _rstd(x)
    h = (x * r) * gain
    q = _dot(h, wq)
    ps, os_ = [], []
    for hd in range(XA_HEADS):
        sl = slice(hd * XA_HEAD_DIM, (hd + 1) * XA_HEAD_DIM)
        sc = _dot_nt(q[:, sl], kv[:, sl]) * (XA_HEAD_DIM ** -0.5)
        e = jnp.exp(sc - jnp.max(sc, axis=-1, keepdims=True))
        p = e / jnp.sum(e, axis=-1, keepdims=True)
        ps.append(p)
        os_.append(_dot(p, kv[:, D_MODEL + hd * XA_HEAD_DIM:D_MODEL + (hd + 1) * XA_HEAD_DIM]))
    return r, h, q, ps, jnp.concatenate(os_, axis=1)


def _xattn_fwd(x, gain, wq, kv, wo, name):
    s = x.shape[0]
    ts = _row_tile(s, 256)

    def body(x_ref, g_ref, wq_ref, kv_ref, wo_ref, o_ref):
        x_ = x_ref[...]
        _, _, _, _, o = _xattn_values(x_, g_ref[...], wq_ref[...], kv_ref[...])
        o_ref[...] = x_ + _dot(o, wo_ref[...])

    row = pl.BlockSpec((ts, D_MODEL), lambda i: (i, 0))
    full = lambda a: pl.BlockSpec(a.shape, lambda i: (0,) * a.ndim)
    return _pcall(
        body, grid=(s // ts,), in_specs=[row, full(gain), full(wq), full(kv), full(wo)], out_specs=row,
        out_shape=jax.ShapeDtypeStruct((s, D_MODEL), F32), compiler_params=_params("parallel"), name=name)(
            x, gain, wq, kv, wo)


def _xattn_bwd(x, dout, gain, wq, kv, wo, name):
    s = x.shape[0]
    m = kv.shape[0]
    ts = _row_tile(s, 256)

    def body(x_ref, do_ref, g_ref, wq_ref, kv_ref, wo_ref, dx_ref, h_ref, dq_ref, o_ref, dkv_ref, dg_ref):
        @pl.when(pl.program_id(0) == 0)
        def _():
            dkv_ref[...] = jnp.zeros_like(dkv_ref)
            dg_ref[...] = jnp.zeros_like(dg_ref)

        x_, do, kvv = x_ref[...], do_ref[...], kv_ref[...]
        r, h, q, ps, o = _xattn_values(x_, g_ref[...], wq_ref[...], kvv)
        dov = _dot_nt(do, wo_ref[...])
        dqs = []
        for hd in range(XA_HEADS):
            sl = slice(hd * XA_HEAD_DIM, (hd + 1) * XA_HEAD_DIM)
            vl = slice(D_MODEL + hd * XA_HEAD_DIM, D_MODEL + (hd + 1) * XA_HEAD_DIM)
            p, doh = ps[hd], dov[:, sl]
            dp = _dot_nt(doh, kvv[:, vl])
            dsc = p * (dp - jnp.sum(dp * p, axis=-1, keepdims=True)) * (XA_HEAD_DIM ** -0.5)
            dqs.append(_dot(dsc, kvv[:, sl]))
            dkv_ref[:, sl] += _dot_tn(dsc, q[:, sl])
            dkv_ref[:, vl] += _dot_tn(p, doh)
        dq = jnp.concatenate(dqs, axis=1)
        dh = _dot_nt(dq, wq_ref[...])
        dg_ref[...] += _colsum(dh * (x_ * r))
        dx_ref[...] = do + _rms_bwd(dh * g_ref[...], x_, r)
        h_ref[...] = _bf(h)
        dq_ref[...] = _bf(dq)
        o_ref[...] = _bf(o)

    row = pl.BlockSpec((ts, D_MODEL), lambda i: (i, 0))
    full = lambda a: pl.BlockSpec(a.shape, lambda i: (0,) * a.ndim)
    sds = jax.ShapeDtypeStruct
    return _pcall(
        body, grid=(s // ts,), in_specs=[row, row, full(gain), full(wq), full(kv), full(wo)],
        out_specs=[row, row, row, row, pl.BlockSpec((m, 2 * D_MODEL), lambda i: (0, 0)),
                   pl.BlockSpec((8, D_MODEL), lambda i: (0, 0))],
        out_shape=[sds((s, D_MODEL), F32), sds((s, D_MODEL), BF16), sds((s, D_MODEL), BF16), sds((s, D_MODEL), BF16),
                   sds((m, 2 * D_MODEL), F32), sds((8, D_MODEL), F32)],
        compiler_params=_params("arbitrary"), name=name)(x, dout, gain, wq, kv, wo)


def _mem_bwd(mem, gain, dkv, wkv, name):
    m = mem.shape[0]

    def body(mem_ref, g_ref, dkv_ref, w_ref, mn_ref, dg_ref):
        mm = mem_ref[...]
        r = _rstd(mm)
        xh = mm * r
        mn_ref[...] = _bf(xh * g_ref[...])
        dmn = _dot_nt(dkv_ref[...], w_ref[...])
        dg_ref[...] = jnp.zeros_like(dg_ref) + _colsum(dmn * xh)

    full = lambda a: pl.BlockSpec(a.shape, lambda: (0,) * a.ndim)
    return _pcall(
        body, in_specs=[full(mem), full(gain), full(dkv), full(wkv)],
        out_specs=[pl.BlockSpec((m, D_MODEL), lambda: (0, 0)), pl.BlockSpec((8, D_MODEL), lambda: (0, 0))],
        out_shape=[jax.ShapeDtypeStruct((m, D_MODEL), BF16), jax.ShapeDtypeStruct((8, D_MODEL), F32)],
        compiler_params=pltpu.CompilerParams(vmem_limit_bytes=VMEM_LIMIT), name=name)(mem, gain, dkv, wkv)


def _mlp_fwd(x, gain, w1, w2, name):
    s = x.shape[0]
    ts = _row_tile(s, 512)
    tf = 1024
    nf = D_FF // tf

    def body(x_ref, g_ref, w1_ref, w2_ref, o_ref, h_sc, acc):
        j = pl.program_id(1)

        @pl.when(j == 0)
        def _():
            xx = x_ref[...]
            h_sc[...] = _bf((xx * _rstd(xx)) * g_ref[...])
            acc[...] = jnp.zeros_like(acc)

        a = jnp.dot(h_sc[...], w1_ref[...], preferred_element_type=F32)
        r = jnp.square(jnp.maximum(a, 0.0))
        acc[...] += _dot(r, w2_ref[...])

        @pl.when(j == nf - 1)
        def _():
            o_ref[...] = x_ref[...] + acc[...]

    row = pl.BlockSpec((ts, D_MODEL), lambda i, j: (i, 0))
    return _pcall(
        body, grid=(s // ts, nf),
        in_specs=[row, pl.BlockSpec((1, D_MODEL), lambda i, j: (0, 0)), pl.BlockSpec((D_MODEL, tf), lambda i, j: (0, j)),
                  pl.BlockSpec((tf, D_MODEL), lambda i, j: (j, 0))],
        out_specs=row, out_shape=jax.ShapeDtypeStruct((s, D_MODEL), F32),
        scratch_shapes=[pltpu.VMEM((ts, D_MODEL), BF16), pltpu.VMEM((ts, D_MODEL), F32)],
        compiler_params=_params("parallel", "arbitrary"), name=name)(x, gain, w1, w2)


def _mlp_bwd(x, dout, gain, w1, w2, name):
    s = x.shape[0]
    ts = _row_tile(s, 512)
    tf = 1024
    nf = D_FF // tf

    def body(x_ref, do_ref, g_ref, w1_ref, w2_ref, dx_ref, h_ref, r_ref, da_ref, dg_ref, h_sc, do_sc, acc):
        i, j = pl.program_id(0), pl.program_id(1)

        @pl.when(j == 0)
        def _():
            xx = x_ref[...]
            h_sc[...] = _bf((xx * _rstd(xx)) * g_ref[...])
            do_sc[...] = _bf(do_ref[...])
            acc[...] = jnp.zeros_like(acc)
            h_ref[...] = h_sc[...]

        @pl.when((j == 0) & (i == 0))
        def _():
            dg_ref[...] = jnp.zeros_like(dg_ref)

        a = jnp.dot(h_sc[...], w1_ref[...], preferred_element_type=F32)
        ra = jnp.maximum(a, 0.0)
        r_ref[...] = _bf(ra * ra)
        dr = lax.dot_general(do_sc[...], w2_ref[...], (((1,), (1,)), ((), ())), preferred_element_type=F32)
        da = _bf(dr * 2.0 * ra)
        da_ref[...] = da
        acc[...] += lax.dot_general(da, w1_ref[...], (((1,), (1,)), ((), ())), preferred_element_type=F32)

        @pl.when(j == nf - 1)
        def _():
            xx = x_ref[...]
            r = _rstd(xx)
            dh = acc[...]
            dg_ref[...] += _colsum(dh * (xx * r))
            dx_ref[...] = do_ref[...] + _rms_bwd(dh * g_ref[...], xx, r)

    row = pl.BlockSpec((ts, D_MODEL), lambda i, j: (i, 0))
    ff = pl.BlockSpec((ts, tf), lambda i, j: (i, j))
    sds = jax.ShapeDtypeStruct
    return _pcall(
        body, grid=(s // ts, nf),
        in_specs=[row, row, pl.BlockSpec((1, D_MODEL), lambda i, j: (0, 0)),
                  pl.BlockSpec((D_MODEL, tf), lambda i, j: (0, j)), pl.BlockSpec((tf, D_MODEL), lambda i, j: (j, 0))],
        out_specs=[row, row, ff, ff, pl.BlockSpec((8, D_MODEL), lambda i, j: (0, 0))],
        out_shape=[sds((s, D_MODEL), F32), sds((s, D_MODEL), BF16), sds((s, D_FF), BF16), sds((s, D_FF), BF16),
                   sds((8, D_MODEL), F32)],
        scratch_shapes=[pltpu.VMEM((ts, D_MODEL), BF16), pltpu.VMEM((ts, D_MODEL), BF16), pltpu.VMEM((ts, D_MODEL), F32)],
        compiler_params=_params("arbitrary", "arbitrary"), name=name)(x, dout, gain, w1, w2)


def _final(x, gain, tgt, name):
    s = x.shape[0]
    ts = _row_tile(s, 512)

    def body(x_ref, g_ref, t_ref, dx_ref, loss_ref, dg_ref):
        @pl.when(pl.program_id(0) == 0)
        def _():
            loss_ref[...] = jnp.zeros_like(loss_ref)
            dg_ref[...] = jnp.zeros_like(dg_ref)

        xx = x_ref[...]
        r = _rstd(xx)
        xh = xx * r
        err = xh * g_ref[...] - t_ref[...]
        loss_ref[...] += 0.5 * jnp.sum(jnp.sum(err * err, axis=1, keepdims=True), axis=0, keepdims=True) / D_MODEL
        dy = err * (1.0 / D_MODEL)
        dg_ref[...] += _colsum(dy * xh)
        dx_ref[...] = _rms_bwd(dy * g_ref[...], xx, r)

    row = pl.BlockSpec((ts, D_MODEL), lambda i: (i, 0))
    return _pcall(
        body, grid=(s // ts,), in_specs=[row, pl.BlockSpec((1, D_MODEL), lambda i: (0, 0)), row],
        out_specs=[row, pl.BlockSpec((8, 128), lambda i: (0, 0)), pl.BlockSpec((8, D_MODEL), lambda i: (0, 0))],
        out_shape=[jax.ShapeDtypeStruct((s, D_MODEL), F32), jax.ShapeDtypeStruct((8, 128), F32),
                   jax.ShapeDtypeStruct((8, D_MODEL), F32)],
        compiler_params=_params("arbitrary"), name=name)(x, gain, tgt)


def _as3d(a):
    return a.reshape((-1,) + a.shape[-2:])


def _sum_cast(terms, out_dtype, name, row_want=256):
    shape = terms[0].shape
    t3 = [_as3d(t) for t in terms]
    b, r, c = t3[0].shape
    tr = _row_tile(r, row_want)

    def body(*refs):
        acc = refs[0][...].astype(F32)
        for t in refs[1:-1]:
            acc = acc + t[...].astype(F32)
        refs[-1][...] = acc.astype(out_dtype)

    spec = pl.BlockSpec((1, tr, c), lambda i, j: (i, j, 0))
    out = _pcall(body, grid=(b, r // tr), in_specs=[spec] * len(t3), out_specs=spec,
                 out_shape=jax.ShapeDtypeStruct((b, r, c), out_dtype), compiler_params=_params("parallel", "parallel"),
                 name=name)(*t3)
    return out.reshape(shape)


def _pair_sum(a, b, sel, loff, out_dtype, name):
    half, _, r, c = b.shape
    k = sel.shape[0]
    tr = _row_tile(r, 256)

    def body(sel_ref, loff_ref, a_ref, b_ref, o_ref):
        o_ref[...] = (a_ref[...] + b_ref[...]).astype(out_dtype)

    blkshape = (1, 1, tr, c)
    grid_spec = pltpu.PrefetchScalarGridSpec(
        num_scalar_prefetch=2, grid=(k, half, r // tr),
        in_specs=[pl.BlockSpec(blkshape, lambda q, l, j, sel_ref, lo_ref: (lo_ref[0] + l, sel_ref[q], j, 0)),
                  pl.BlockSpec(blkshape, lambda q, l, j, sel_ref, lo_ref: (l, sel_ref[q], j, 0))],
        out_specs=pl.BlockSpec(blkshape, lambda q, l, j, sel_ref, lo_ref: (q, l, j, 0)))
    return _pcall(body, grid_spec=grid_spec, out_shape=jax.ShapeDtypeStruct((k, half, r, c), out_dtype),
                  compiler_params=_params("parallel", "parallel", "parallel"), name=name)(sel, loff, a, b)


def _adamw(w, g, m, v, name):
    shape = w.shape
    w3, g3, m3, v3 = _as3d(w), _as3d(g), _as3d(m), _as3d(v)
    b, r, c = w3.shape
    tr = _row_tile(r, 256)

    def body(w_ref, g_ref, m_ref, v_ref, d_ref, mo_ref, vo_ref):
        gg = g_ref[...]
        mn = ADAM_B1 * m_ref[...] + (1.0 - ADAM_B1) * gg
        vn = ADAM_B2 * v_ref[...] + (1.0 - ADAM_B2) * jnp.square(gg)
        m_hat = mn / (1.0 - ADAM_B1 ** ADAM_STEP)
        v_hat = vn / (1.0 - ADAM_B2 ** ADAM_STEP)
        d_ref[...] = -ADAM_LR * (m_hat / (jnp.sqrt(v_hat) + ADAM_EPS) + ADAM_WD * w_ref[...])
        mo_ref[...] = mn
        vo_ref[...] = vn

    spec = pl.BlockSpec((1, tr, c), lambda i, j: (i, j, 0))
    sd = jax.ShapeDtypeStruct((b, r, c), F32)
    d, mo, vo = _pcall(body, grid=(b, r // tr), in_specs=[spec] * 4, out_specs=[spec] * 3, out_shape=[sd] * 3,
                       compiler_params=_params("parallel", "parallel"), name=name)(w3, g3, m3, v3)
    return d.reshape(shape), mo.reshape(shape), vo.reshape(shape)


ANY = pl.BlockSpec(memory_space=pl.ANY)


def _place():
    return lax.axis_index("x"), lax.axis_index("y"), lax.axis_index("c")


def _flip(x, y, r):
    return (1 - x if r & 2 else x), (1 - y if r & 1 else y)


def _chip_gather(arrs, name):
    n = len(arrs)
    nl = arrs[0].shape[0]
    half = nl // 2
    assert half * 2 == nl

    def body(*refs):
        ins, outs = refs[:n], refs[n:2 * n]
        send, recv, fsend, frecv = refs[2 * n:]
        x, y, c = _place()
        blk = 2 * x + y
        mine = pl.ds(c * half, half)
        theirs = pl.ds((1 - c) * half, half)

        def ici(r, a):
            cx, cy = _flip(x, y, r)
            return pltpu.make_async_remote_copy(
                src_ref=ins[a].at[mine], dst_ref=outs[a].at[blk, mine], send_sem=send.at[(r - 1) * n + a],
                recv_sem=recv.at[(r - 1) * n + a], device_id=(cx, cy, c), device_id_type=MESH)

        def ici_in(r, a):
            cx, cy = _flip(x, y, r)
            return pltpu.make_async_remote_copy(
                src_ref=ins[a].at[mine], dst_ref=outs[a].at[2 * cx + cy, mine], send_sem=send.at[(r - 1) * n + a],
                recv_sem=recv.at[(r - 1) * n + a], device_id=(cx, cy, c), device_id_type=MESH)

        def d2d(r, a, rows):
            cx, cy = _flip(x, y, r)
            ref = outs[a].at[2 * cx + cy, rows]
            return pltpu.make_async_remote_copy(
                src_ref=ref, dst_ref=ref, send_sem=fsend.at[(r - 1) * n + a], recv_sem=frecv.at[(r - 1) * n + a],
                device_id=(x, y, 1 - c), device_id_type=MESH)

        for r in (1, 2, 3):
            for a in range(n):
                ici(r, a).start()
        for r in (1, 2, 3):
            for a in range(n):
                ici_in(r, a).wait_recv()
                d2d(r, a, mine).start()
        for r in (1, 2, 3):
            for a in range(n):
                d2d(r, a, theirs).wait_recv()
        for r in (1, 2, 3):
            for a in range(n):
                ici(r, a).wait_send()
                d2d(r, a, mine).wait_send()

    return _pcall(
        body, in_specs=[ANY] * n, out_specs=[ANY] * n,
        out_shape=[jax.ShapeDtypeStruct((4,) + a.shape, a.dtype) for a in arrs],
        scratch_shapes=[pltpu.SemaphoreType.DMA((3 * n,)), pltpu.SemaphoreType.DMA((3 * n,)),
                        pltpu.SemaphoreType.DMA((3 * n,)), pltpu.SemaphoreType.DMA((3 * n,))],
        name=name)(*arrs)


def _sibling_swap(arrs, name):
    n = len(arrs)
    half = arrs[0].shape[0] // 2

    def body(*refs):
        ins, outs = refs[:n], refs[n:2 * n]
        send, recv = refs[2 * n:]
        x, y, c = _place()
        cps = [pltpu.make_async_remote_copy(
            src_ref=ins[a].at[pl.ds((1 - c) * half, half)], dst_ref=outs[a], send_sem=send.at[a], recv_sem=recv.at[a],
            device_id=(x, y, 1 - c), device_id_type=MESH) for a in range(n)]
        for cp in cps:
            cp.start()
        for cp in cps:
            cp.wait()

    return _pcall(
        body, in_specs=[ANY] * n, out_specs=[ANY] * n,
        out_shape=[jax.ShapeDtypeStruct((half,) + a.shape[1:], a.dtype) for a in arrs],
        scratch_shapes=[pltpu.SemaphoreType.DMA((n,)), pltpu.SemaphoreType.DMA((n,))], name=name)(*arrs)


def _chip_exchange(arrs, name):
    n = len(arrs)

    def body(*refs):
        ins, outs = refs[:n], refs[n:2 * n]
        send, recv = refs[2 * n:]
        x, y, c = _place()
        cps = []
        for r in (1, 2, 3):
            cx, cy = _flip(x, y, r)
            for a in range(n):
                cps.append(pltpu.make_async_remote_copy(
                    src_ref=ins[a].at[r - 1], dst_ref=outs[a].at[r - 1], send_sem=send.at[(r - 1) * n + a],
                    recv_sem=recv.at[(r - 1) * n + a], device_id=(cx, cy, c), device_id_type=MESH))
        for cp in cps:
            cp.start()
        for cp in cps:
            cp.wait()

    return _pcall(
        body, in_specs=[ANY] * n, out_specs=[ANY] * n,
        out_shape=[jax.ShapeDtypeStruct(a.shape, a.dtype) for a in arrs],
        scratch_shapes=[pltpu.SemaphoreType.DMA((3 * n,)), pltpu.SemaphoreType.DMA((3 * n,))], name=name)(*arrs)


def _sibling_send(arrs, name):
    n = len(arrs)

    def body(*refs):
        ins, outs = refs[:n], refs[n:2 * n]
        send, recv = refs[2 * n:]
        x, y, c = _place()
        cps = [pltpu.make_async_remote_copy(
            src_ref=ins[a], dst_ref=outs[a], send_sem=send.at[a], recv_sem=recv.at[a],
            device_id=(x, y, 1 - c), device_id_type=MESH) for a in range(n)]
        for cp in cps:
            cp.start()
        for cp in cps:
            cp.wait()

    return _pcall(
        body, in_specs=[ANY] * n, out_specs=[ANY] * n,
        out_shape=[jax.ShapeDtypeStruct(a.shape, a.dtype) for a in arrs],
        scratch_shapes=[pltpu.SemaphoreType.DMA((n,)), pltpu.SemaphoreType.DMA((n,))], name=name)(*arrs)


def _gather8(v, reduce, name):
    rows, w = v.shape

    def body(v_ref, out_ref, buf, send_sems, recv_sems):
        x, y, c = _place()
        me, sibling = (x, y, c), (x, y, 1 - c)
        chips = [_flip(x, y, r) for r in (1, 2, 3)]
        dst = out_ref if not reduce else buf

        def slot(px, py, pc):
            return dst.at[4 * px + 2 * py + pc]

        def copy(k, block, to, src=None):
            return pltpu.make_async_remote_copy(
                src_ref=slot(*block) if src is None else src, dst_ref=slot(*block), send_sem=send_sems.at[k],
                recv_sem=recv_sems.at[k], device_id=to, device_id_type=MESH)

        dst[4 * x + 2 * y + c] = v_ref[...]
        first = [copy(0, me, sibling, src=v_ref)]
        first += [copy(1 + j, me, (*chip, c), src=v_ref) for j, chip in enumerate(chips)]
        for cp in first:
            cp.start()
        passed = [copy(4 + j, (*chip, c), sibling) for j, chip in enumerate(chips)]
        for j, chip in enumerate(chips):
            copy(1 + j, (*chip, c), me).wait_recv()
            passed[j].start()
        copy(0, sibling, me).wait_recv()
        for j, chip in enumerate(chips):
            copy(4 + j, (*chip, 1 - c), me).wait_recv()
        for cp in first + passed:
            cp.wait_send()
        if reduce:
            acc = buf[0]
            for d in range(1, 8):
                acc = acc + buf[d]
            out_ref[...] = acc

    vm = pl.BlockSpec(memory_space=pltpu.VMEM)
    scratch = [pltpu.VMEM((8, rows, w) if reduce else (8, 8, 128), F32), pltpu.SemaphoreType.DMA((7,)),
               pltpu.SemaphoreType.DMA((7,))]
    out_shape = jax.ShapeDtypeStruct((rows, w) if reduce else (8, rows, w), F32)
    return _pcall(body, in_specs=[vm], out_specs=vm, out_shape=out_shape, scratch_shapes=scratch,
                  compiler_params=pltpu.CompilerParams(vmem_limit_bytes=VMEM_LIMIT), name=name)(v)


SMALL = [("norm_mix", 1024), ("b_gate", 2048), ("conv_b", 4096), ("dt_bias", 32), ("a_log", 32), ("d_skip", 32),
         ("ssm_norm", 2048), ("norm_xa", 1024), ("norm_mem", 1024), ("norm_mlp", 1024)]


def _rows_of(width):
    return max(1, width // 1024)


def _pack_rows(pieces):
    out = []
    for p in pieces:
        p = p.astype(F32)
        if p.shape[-1] < 1024:
            p = jnp.pad(p, ((0, 0), (0, 1024 - p.shape[-1])))
        out.append(p.reshape(-1, 1024))
    cat = jnp.concatenate(out, axis=0)
    pad = (-cat.shape[0]) % 8
    return jnp.pad(cat, ((0, pad), (0, 0))) if pad else cat


def _unpack_rows(packed, widths_rows):
    out, at = [], 0
    for r, w in widths_rows:
        k = r * _rows_of(w)
        p = packed[at:at + k]
        at += k
        out.append(p[:, :w] if w < 1024 else p.reshape(r, w))
    return out


def _to_cat(w):
    pieces = [w[:, O_XBC:O_XBC + 4096], w[:, O_Q:O_Q + 512], w[:, O_K:O_K + 512], w[:, O_DT:O_DT + 32],
              jnp.zeros((w.shape[0], DT_PAD - 32), w.dtype), w[:, O_Z:O_Z + 2048], w[:, O_GATES:O_GATES + 2048],
              w[:, O_V:O_V + 1024], w[:, O_G:O_G + 1024]]
    return jnp.concatenate(pieces, axis=1)


def _from_cat(g):
    pieces = [g[:, C_Q:C_Q + 512], g[:, C_K:C_K + 512], g[:, C_V:C_V + 1024], g[:, C_G:C_G + 1024],
              g[:, C_Z:C_Z + 2048], g[:, C_XBC:C_XBC + 4096], g[:, C_DT:C_DT + 32], g[:, C_GATES:C_GATES + 2048]]
    return jnp.concatenate(pieces, axis=1)


PACK_ROWS = [("w_br_ret", 256), ("w_br_ssm", 512), ("w_out", 256), ("xa_wq", 256), ("xa_wo", 256), ("mlp_w1", 1024),
             ("mlp_w2", 1024)]
PACK_N = sum(r for _, r in PACK_ROWS)


def _blocks_rows(g, n):
    return g.reshape(4, n, g.shape[-1])


def _blocks_cols(g, n):
    return g.reshape(g.shape[0], 4, n).transpose(1, 0, 2)


def kernel(x, mem, positions, norm_mix, w_in, b_gate, conv_w, conv_b, dt_bias, a_log, d_skip, ssm_norm, w_br_ret, w_br_ssm, w_out, norm_xa, norm_mem, xa_wq, xa_wkv, xa_wo, norm_mlp, mlp_w1, mlp_w2, norm_final, loss_target, m_norm_mix, m_w_in, m_b_gate, m_conv_w, m_conv_b, m_dt_bias, m_a_log, m_d_skip, m_ssm_norm, m_w_br_ret, m_w_br_ssm, m_w_out, m_norm_xa, m_norm_mem, m_xa_wq, m_xa_wkv, m_xa_wo, m_norm_mlp, m_mlp_w1, m_mlp_w2, m_norm_final, v_norm_mix, v_w_in, v_b_gate, v_conv_w, v_conv_b, v_dt_bias, v_a_log, v_d_skip, v_ssm_norm, v_w_br_ret, v_w_br_ssm, v_w_out, v_norm_xa, v_norm_mem, v_xa_wq, v_xa_wkv, v_xa_wo, v_norm_mlp, v_mlp_w1, v_mlp_w2, v_norm_final):
    W = dict(norm_mix=norm_mix, w_in=w_in, b_gate=b_gate, conv_w=conv_w, conv_b=conv_b, dt_bias=dt_bias, a_log=a_log,
             d_skip=d_skip, ssm_norm=ssm_norm, w_br_ret=w_br_ret, w_br_ssm=w_br_ssm, w_out=w_out, norm_xa=norm_xa,
             norm_mem=norm_mem, xa_wq=xa_wq, xa_wkv=xa_wkv, xa_wo=xa_wo, norm_mlp=norm_mlp, mlp_w1=mlp_w1,
             mlp_w2=mlp_w2, norm_final=norm_final)
    M = dict(norm_mix=m_norm_mix, w_in=m_w_in, b_gate=m_b_gate, conv_w=m_conv_w, conv_b=m_conv_b, dt_bias=m_dt_bias,
             a_log=m_a_log, d_skip=m_d_skip, ssm_norm=m_ssm_norm, w_br_ret=m_w_br_ret, w_br_ssm=m_w_br_ssm,
             w_out=m_w_out, norm_xa=m_norm_xa, norm_mem=m_norm_mem, xa_wq=m_xa_wq, xa_wkv=m_xa_wkv, xa_wo=m_xa_wo,
             norm_mlp=m_norm_mlp, mlp_w1=m_mlp_w1, mlp_w2=m_mlp_w2, norm_final=m_norm_final)
    V = dict(norm_mix=v_norm_mix, w_in=v_w_in, b_gate=v_b_gate, conv_w=v_conv_w, conv_b=v_conv_b, dt_bias=v_dt_bias,
             a_log=v_a_log, d_skip=v_d_skip, ssm_norm=v_ssm_norm, w_br_ret=v_w_br_ret, w_br_ssm=v_w_br_ssm,
             w_out=v_w_out, norm_xa=v_norm_xa, norm_mem=v_norm_mem, xa_wq=v_xa_wq, xa_wkv=v_xa_wkv, xa_wo=v_xa_wo,
             norm_mlp=v_norm_mlp, mlp_w1=v_mlp_w1, mlp_w2=v_mlp_w2, norm_final=v_norm_final)
    nl = w_in.shape[0]
    s = x.shape[1]
    x0 = x[0]
    mem2 = mem[0]
    tgt = loss_target[0]
    blk = 2 * lax.axis_index("x") + lax.axis_index("y")

    a1 = jnp.concatenate([W[k] for k, _ in PACK_ROWS], axis=1).astype(BF16)
    ain, akv = w_in.astype(BF16), xa_wkv.astype(BF16)
    g1, gin, gkv = _chip_gather([a1, ain, akv], "gather_weights")

    def blocks(own, gathered):
        return [jnp.where(blk == j, own, gathered[j]) for j in range(4)]
    cw_all = _gather8(conv_w.reshape(nl * SSM_CONV, 1024), False, "gather_conv_w")
    cw_full = cw_all.reshape(4, 2, nl, SSM_CONV, 1024)[:, 0].transpose(1, 2, 0, 3).reshape(nl, SSM_CONV, SSM_CONV_DIM)

    offs = {}
    at = 0
    for k, r in PACK_ROWS:
        offs[k] = (at, r)
        at += r

    def rows_weight(k, l):
        o, r = offs[k]
        return jnp.concatenate(blocks(a1[l, o:o + r], g1[:, l, o:o + r]), axis=0)

    inv_freq = ROPE_THETA ** (-jnp.arange(0, RET_QK_DIM, 2, dtype=F32) / RET_QK_DIM)
    ang = positions.astype(F32)[0][:, None] * inv_freq
    cos, sin = jnp.cos(ang), jnp.sin(ang)
    cosf = jnp.concatenate([cos, cos], axis=1)
    sinf = jnp.concatenate([-sin, sin], axis=1)
    dm, qd, kd, cd = (jnp.asarray(c) for c in _ret_constants())
    eye, blkm = (jnp.asarray(c) for c in _ssd_constants())
    consts = (dm, qd, kd, cd, eye, blkm)
    e_np = _head_expand()
    eexp = jnp.asarray(e_np, BF16)
    eexp_t = jnp.asarray(e_np.T.copy(), BF16)

    saved = []
    xcur = x0
    for l in range(nl):
        wcat = _to_cat(jnp.concatenate(blocks(ain[l], gin[:, l]), axis=1))
        wr, ws, wo = rows_weight("w_br_ret", l), rows_weight("w_br_ssm", l), rows_weight("w_out", l)
        wq, wxo, w2 = rows_weight("xa_wq", l), rows_weight("xa_wo", l), rows_weight("mlp_w2", l)
        o1, r1 = offs["mlp_w1"]
        w1 = jnp.concatenate(blocks(a1[l, o1:o1 + r1], g1[:, l, o1:o1 + r1]), axis=1)
        wkv = jnp.concatenate(blocks(akv[l], gkv[:, l]), axis=1)
        cw, cb = cw_full[l], conv_b[l][None]
        dtb = jnp.pad(dt_bias[l], (0, 128 - SSM_HEADS))[None]
        ax = jnp.repeat(-jnp.exp(a_log[l]), 64)[None]
        dsk = jnp.repeat(d_skip[l], 64)[None]
        bg, sn = b_gate[l][None], ssm_norm[l][None]
        proj, u = _nmm(xcur, norm_mix[l][None], wcat, "in_proj", save_u=True)
        qr, kr, xc, dtx = _prescan(proj, cosf, sinf, cw, cb, dtb, eexp, "prescan")
        yr, ys, sst, hst = _scan_fwd(qr, kr, proj, xc, dtx, ax, consts, "scan_fwd")
        x1 = _postscan_fwd(xcur, yr, ys, xc, proj, bg, dsk, sn, wr, ws, wo, "postscan")
        kv = _bf(_nmm(mem2, norm_mem[l][None], wkv, "mem_kv"))
        x2 = _xattn_fwd(x1, norm_xa[l][None], wq, kv, wxo, "xattn")
        x3 = _mlp_fwd(x2, norm_mlp[l][None], w1, w2, "mlp")
        saved.append(dict(x0=xcur, x1=x1, x2=x2, proj=proj, u=u, qr=qr, kr=kr, xc=xc, dtx=dtx, yr=yr, ys=ys, sst=sst,
                          hst=hst, kv=kv, wcat=wcat, wr=wr, ws=ws, wo=wo, wq=wq, wxo=wxo, w1=w1, w2=w2, wkv=wkv, cw=cw,
                          cb=cb, dtb=dtb, ax=ax, dsk=dsk, bg=bg, sn=sn))
        xcur = x3

    dx, loss_acc, dnf = _final(xcur, norm_final[None], tgt, "final")
    loss = lax.psum(loss_acc[0, 0], ("x", "y", "c"))

    small_g = [None] * nl
    p1, pin, pkv = [None] * nl, [None] * nl, [None] * nl
    for l in reversed(range(nl)):
        sv = saved[l]
        dx2, hm, rm, dam, dg_mlp = _mlp_bwd(sv["x2"], dx, norm_mlp[l][None], sv["w1"], sv["w2"], "mlp_bwd")
        dw1 = _mm_tn(hm, dam, "dw_mlp1")
        dw2 = _mm_tn(rm, dx, "dw_mlp2")
        dx1, hx, dqx, ox, dkv, dg_xa = _xattn_bwd(sv["x1"], dx2, norm_xa[l][None], sv["wq"], sv["kv"], sv["wxo"],
                                                  "xattn_bwd")
        dwq = _mm_tn(hx, dqx, "dw_xq")
        dwxo = _mm_tn(ox, dx2, "dw_xo")
        memn, dg_mem = _mem_bwd(mem2, norm_mem[l][None], dkv, sv["wkv"], "mem_bwd")
        dwkv = _mm_tn(memn, dkv, "dw_xkv")
        (dyr, dys, dxs_skip, dproj, yrn, ysn, mg, dbr, dbs, dbg, ddsk, dsn) = _postscan_bwd(
            dx1, sv["yr"], sv["ys"], sv["xc"], sv["proj"], sv["bg"], sv["dsk"], sv["sn"], sv["wr"], sv["ws"], sv["wo"],
            "postscan_bwd")
        dwo = _mm_tn(mg, dx1, "dw_out")
        dwr = _mm_tn(yrn, dbr, "dw_br_ret")
        dws = _mm_tn(ysn, dbs, "dw_br_ssm")
        dqr, dkr, dproj, dxc, gdtx, da_cols = _scan_bwd(sv["qr"], sv["kr"], sv["proj"], sv["xc"], sv["dtx"], sv["ax"],
                                                        consts, sv["sst"], sv["hst"], dyr, dys, dproj, "scan_bwd")
        dproj, dcw, dcb, ddtb = _prescan_bwd(sv["proj"], dxc, dxs_skip, gdtx, dqr, dkr, cosf, sinf, sv["cw"], sv["cb"],
                                             sv["dtb"], eexp_t, dproj, "prescan_bwd")
        dwcat = _mm_tn(sv["u"], dproj, "dw_in")
        dx, dg_mix = _in_bwd(dproj, sv["wcat"], sv["x0"], norm_mix[l][None], dx1, "in_bwd")

        da_log = (da_cols.reshape(SSM_HEADS, 64).sum(axis=1)) * (-jnp.exp(a_log[l]))
        dd_skip = ddsk[0].reshape(SSM_HEADS, 64).sum(axis=1)
        small_g[l] = [dg_mix[0:1], dbg[0:1], dcb[0:1], ddtb[0:1, :SSM_HEADS], da_log[None], dd_skip[None], dsn[0:1],
                      dg_xa[0:1], dg_mem[0:1], dg_mlp[0:1], dcw[0::8]]
        p1[l] = jnp.concatenate([_blocks_rows(dwr, 256), _blocks_rows(dws, 512), _blocks_rows(dwo, 256),
                                 _blocks_rows(dwq, 256), _blocks_rows(dwxo, 256), _blocks_cols(dw1, 1024),
                                 _blocks_rows(dw2, 1024)], axis=1)
        pin[l] = _blocks_cols(_from_cat(dwcat), IN_DIM // 4)
        pkv[l] = _blocks_cols(dwkv, 512)

    grad_x = dx[None]

    pieces = []
    for l in range(nl):
        pieces += small_g[l]
    pieces.append(dnf[0:1])
    small_sum = _gather8(_pack_rows(pieces), True, "reduce_small")
    layout = []
    for l in range(nl):
        layout += [(1, w) for _, w in SMALL] + [(SSM_CONV, SSM_CONV_DIM)]
    layout.append((1, 1024))
    red = _unpack_rows(small_sum, layout)
    per = len(SMALL) + 1
    g_small = {k: jnp.concatenate([red[l * per + i] for l in range(nl)], axis=0) for i, (k, _) in enumerate(SMALL)}
    g_convw_full = jnp.stack([red[l * per + len(SMALL)] for l in range(nl)])
    g_small["conv_w"] = lax.dynamic_slice_in_dim(g_convw_full, blk * 1024, 1024, axis=2)
    g_small["norm_final"] = red[-1][0]

    P = [jnp.stack(p1), jnp.stack(pin), jnp.stack(pkv)]
    half = nl // 2
    c = lax.axis_index("c")
    got = _sibling_swap(P, "grads_core_swap")
    loff = (c * half).astype(jnp.int32)[None]
    blk = blk.astype(jnp.int32)
    sel_own = blk[None]
    sel_rem = jnp.stack([blk ^ 1, blk ^ 2, blk ^ 3])
    own = [_pair_sum(p, g_, sel_own, loff, F32, "chip_sum_own")[0] for p, g_ in zip(P, got)]
    out_b = [_pair_sum(p, g_, sel_rem, loff, BF16, "chip_sum_send") for p, g_ in zip(P, got)]
    inc = _chip_exchange(out_b, "grads_chip_exchange")
    red_half = [_sum_cast([o, i_[0], i_[1], i_[2]], F32, "grads_total") for o, i_ in zip(own, inc)]
    sib_half = _sibling_send(red_half, "grads_core_join")

    def both_halves(mine_, theirs_):
        return jnp.concatenate([jnp.where(c == 0, mine_, theirs_), jnp.where(c == 0, theirs_, mine_)], axis=0)

    grads = dict(g_small)
    for k, r in PACK_ROWS:
        o, _ = offs[k]
        grads[k] = both_halves(red_half[0][:, o:o + r], sib_half[0][:, o:o + r])
    grads["w_in"] = both_halves(red_half[1], sib_half[1])
    grads["xa_wkv"] = both_halves(red_half[2], sib_half[2])

    delta, new_m, new_v = {}, {}, {}
    for k in ["w_in", "xa_wkv"] + [k for k, _ in PACK_ROWS]:
        delta[k], new_m[k], new_v[k] = _adamw(W[k], grads[k], M[k], V[k], "adamw_" + k)
    small_names = [k for k, _ in SMALL] + ["conv_w", "norm_final"]

    def pack_small(src):
        ps = []
        for k in small_names:
            a = src[k]
            ps.append(a.reshape(-1, a.shape[-1]) if a.ndim > 1 else a[None])
        return _pack_rows(ps)

    ds_, ms_, vs_ = _adamw(pack_small(W), pack_small(grads), pack_small(M), pack_small(V), "adamw_small")
    lay2 = []
    for k in small_names:
        a = W[k]
        lay2.append((int(np.prod(a.shape[:-1])) if a.ndim > 1 else 1, a.shape[-1]))
    for src, dst in ((ds_, delta), (ms_, new_m), (vs_, new_v)):
        for k, piece in zip(small_names, _unpack_rows(src, lay2)):
            dst[k] = piece.reshape(W[k].shape)

    names = ["norm_mix", "w_in", "b_gate", "conv_w", "conv_b", "dt_bias", "a_log", "d_skip", "ssm_norm", "w_br_ret",
             "w_br_ssm", "w_out", "norm_xa", "norm_mem", "xa_wq", "xa_wkv", "xa_wo", "norm_mlp", "mlp_w1", "mlp_w2",
             "norm_final"]
    return (loss, grad_x, *[grads[n] for n in names], *[delta[n] for n in names], *[new_m[n] for n in names],
            *[new_v[n] for n in names])
```

```python
import numpy as np
import jax
import jax.numpy as jnp
from jax import lax
from jax.experimental import pallas as pl
from jax.experimental.pallas import tpu as pltpu

F32 = jnp.float32
BF16 = jnp.bfloat16
MESH = pl.DeviceIdType.MESH

D_MODEL = 1024
CHUNK = 64
EPS = 1e-6
RET_HEADS = 4
RET_QK_DIM = 128
RET_V_DIM = 256
RET_QK = 512
RET_V = 1024
ROPE_THETA = 10000.0
SSM_INNER = 2048
SSM_HEADS = 32
SSM_GROUPS = 8
SSM_STATE = 128
SSM_CONV = 4
SSM_BC = 1024
SSM_CONV_DIM = 4096
XA_HEADS = 4
XA_HEAD_DIM = 256
D_FF = 4096
GROUP_W = 256

DT_PAD = 1024
IN_DIM = 11296
NP = 12288
C_XBC, C_Q, C_K, C_DT, C_Z, C_GATES, C_V, C_G = 0, 4096, 4608, 5120, 6144, 8192, 10240, 11264
O_Q, O_K, O_V, O_G, O_Z, O_XBC, O_DT, O_GATES = 0, 512, 1024, 2048, 3072, 5120, 9216, 9248

ADAM_LR = 0.001
ADAM_B1 = 0.9
ADAM_B2 = 0.999
ADAM_EPS = 1e-08
ADAM_WD = 0.01
ADAM_STEP = 10

VMEM_LIMIT = 56 * 1024 * 1024


def _params(*sem):
    return pltpu.CompilerParams(dimension_semantics=sem, vmem_limit_bytes=VMEM_LIMIT)


def _pcall(body, **kw):
    return pl.pallas_call(body, **kw)


def _bf(a):
    return a.astype(BF16)


def _dot(a, b):
    return jnp.dot(_bf(a), _bf(b), preferred_element_type=F32)


def _dot_nt(a, b):
    return lax.dot_general(_bf(a), _bf(b), (((1,), (1,)), ((), ())), preferred_element_type=F32)


def _dot_tn(a, b):
    return lax.dot_general(_bf(a), _bf(b), (((0,), (0,)), ((), ())), preferred_element_type=F32)


def _colsum(a):
    return jnp.sum(a, axis=0, keepdims=True)


def _rstd(x):
    return lax.rsqrt(jnp.mean(x * x, axis=-1, keepdims=True) + EPS)


def _rms_bwd(dy, x, rstd):
    xh = x * rstd
    return rstd * (dy - xh * jnp.mean(dy * xh, axis=-1, keepdims=True))


def _sigmoid(x):
    return 1.0 / (1.0 + jnp.exp(-x))


def _silu_and_grad(x):
    s = _sigmoid(x)
    return x * s, s + x * s * (1.0 - s)


def _softplus(x):
    u = jnp.exp(-jnp.abs(x))
    l1p = jnp.where(u < 1e-4, u * (1.0 - 0.5 * u), jnp.log(1.0 + u))
    return jnp.maximum(x, 0.0) + l1p


def _split3_dot(a, e):
    hi = a.astype(BF16)
    r1 = a - hi.astype(F32)
    mid = r1.astype(BF16)
    lo = (r1 - mid.astype(F32)).astype(BF16)
    return (jnp.dot(hi, e, preferred_element_type=F32) + jnp.dot(mid, e, preferred_element_type=F32)
            + jnp.dot(lo, e, preferred_element_type=F32))


def _cumsum_rows(a):
    rows = lax.broadcasted_iota(jnp.int32, a.shape, 0)
    s = 1
    while s < a.shape[0]:
        a = a + jnp.where(rows >= s, pltpu.roll(a, s, 0), 0.0)
        s *= 2
    return a


def _revcumsum_rows(a):
    n = a.shape[0]
    rows = lax.broadcasted_iota(jnp.int32, a.shape, 0)
    s = 1
    while s < n:
        a = a + jnp.where(rows < n - s, pltpu.roll(a, n - s, 0), 0.0)
        s *= 2
    return a


def _rms_groups(y, width):
    out = []
    for h in range(y.shape[1] // width):
        slab = y[:, h * width:(h + 1) * width]
        out.append((slab, _rstd(slab)))
    return out


def _ret_constants():
    idx = np.arange(CHUNK, dtype=np.float32)
    lg = np.log1p(-(np.float32(2.0) ** (np.float32(-5.0) - np.arange(RET_HEADS, dtype=np.float32)))).astype(np.float32)
    rel = np.abs(idx[:, None] - idx[None, :])
    dm = np.exp(lg[:, None, None] * rel).astype(np.float32)
    qd = np.exp(lg[None, :] * (idx[:, None] + 1.0)).astype(np.float32)
    kd = np.exp(lg[None, :] * (CHUNK - 1.0 - idx[:, None])).astype(np.float32)
    cd = np.exp(lg * CHUNK).astype(np.float32)
    qd = np.repeat(qd, RET_QK_DIM, axis=1)
    kd = np.repeat(kd, RET_QK_DIM, axis=1)
    cd = np.repeat(cd, RET_QK_DIM)[:, None] * np.ones((1, RET_V_DIM), np.float32)
    return dm, qd, kd, cd.astype(np.float32)


def _ssd_constants():
    eye = np.tile(np.eye(CHUNK, dtype=np.float32), (1, GROUP_W // CHUNK))
    blk = np.kron(np.eye(GROUP_W // CHUNK, dtype=np.float32), np.ones((CHUNK, CHUNK), np.float32))
    return eye, blk


def _head_expand():
    e = np.zeros((128, SSM_INNER), np.float32)
    for h in range(SSM_HEADS):
        e[h, h * 64:(h + 1) * 64] = 1.0
    return e


def _ret_chunk_fwd(qh, kh, vh, sh, dmh, qdh, kdh, cdh):
    a = _dot_nt(qh, kh) * dmh
    y = _dot(a, vh) + _dot(qh * qdh, sh)
    s_new = sh * cdh + _dot_tn(kh * kdh, vh)
    return y, s_new


def _ret_chunk_bwd(qh, kh, vh, sh, dmh, qdh, kdh, cdh, dy, ds_new):
    a = _dot_nt(qh, kh) * dmh
    dp = _dot_nt(dy, vh) * dmh
    dq = _dot(dp, kh) + _dot_nt(dy, sh) * qdh
    dk = _dot_tn(dp, qh) + _dot_nt(vh, ds_new) * kdh
    dv = _dot_tn(a, dy) + _dot(kh * kdh, ds_new)
    ds = cdh * ds_new + _dot_tn(qh * qdh, dy)
    return dq, dk, dv, ds


def _ssd_common(xs, dtx, ax, eye):
    cum = _cumsum_rows(dtx * ax)
    last = cum[CHUNK - 1:CHUNK, :]
    r = _colsum(jnp.where(eye > 0.5, cum, 0.0))
    return cum, last, r, xs * dtx


def _tile4(a):
    return jnp.concatenate([a, a, a, a], axis=0)


def _ssd_chunk_fwd(xs, dtx, b, c, ax, hg, eye, blk):
    cum, last, r, x = _ssd_common(xs, dtx, ax, eye)
    lam = jnp.exp(-jnp.abs(cum - r))
    wc = _dot_nt(c, _tile4(b)) * lam
    bd = _tile4(x) * blk
    y = _dot(wc, bd) + _dot(c, hg) * jnp.exp(cum)
    h_new = hg * jnp.exp(last) + _dot_tn(b, x * jnp.exp(last - cum))
    return y, h_new


def _ssd_chunk_bwd(xs, dtx, b, c, ax, hg, eye, blk, dy, dh_new):
    cum, last, r, x = _ssd_common(xs, dtx, ax, eye)
    delta = cum - r
    lam = jnp.exp(-jnp.abs(delta))
    b4 = _tile4(b)
    cb4 = _dot_nt(c, b4)
    wc = cb4 * lam
    bd = _tile4(x) * blk
    ecx = jnp.exp(cum)
    wl = jnp.exp(last - cum)
    ecl = jnp.exp(last)
    z = _dot(c, hg)
    dwc = _dot_nt(dy, bd)
    dbd = _dot_tn(wc, dy) * blk
    dx = dbd[0:64] + dbd[64:128] + dbd[128:192] + dbd[192:256]
    dt_ = _dot(b, dh_new)
    dx = dx + dt_ * wl
    dcb4 = dwc * lam
    dz = dy * ecx
    dc = _dot(dcb4, b4) + _dot_nt(dz, hg)
    db4 = _dot_tn(dcb4, c)
    db = db4[0:64] + db4[64:128] + db4[128:192] + db4[192:256] + _dot_nt(x * wl, dh_new)
    g = dwc * cb4 * lam * (-jnp.sign(delta))
    dr = -_colsum(g)
    dwl = dt_ * x * wl
    u = g + eye * dr + dy * z * ecx - dwl
    lastrow = _colsum(dwl) + _colsum(dh_new * hg) * ecl
    rows = lax.broadcasted_iota(jnp.int32, u.shape, 0)
    u = u + jnp.where(rows == CHUNK - 1, lastrow, 0.0)
    dh = _dot_tn(c, dz) + dh_new * ecl
    rc = _revcumsum_rows(u)
    dxs = dx * dtx
    g_dtx = dx * xs + rc * ax
    da = _colsum(rc * dtx)
    return dxs, g_dtx, db, dc, da, dh


def _row_tile(s, want):
    t = min(s, want)
    assert s % t == 0
    return t


def _nmm(x, gain, w, name, tn=1024, save_u=False):
    s, d = x.shape
    blocked = w.ndim == 3
    if blocked:
        tn = w.shape[2]
        n = w.shape[0] * tn
        w_spec = pl.BlockSpec((1, d, tn), lambda i, j: (j, 0, 0))
    else:
        n = w.shape[1]
        w_spec = pl.BlockSpec((d, tn), lambda i, j: (0, j))
    tm = _row_tile(s, 1024)
    assert n % tn == 0

    def body(x_ref, g_ref, w_ref, *rest):
        o_ref, u_sc = rest[0], rest[-1]

        @pl.when(pl.program_id(1) == 0)
        def _():
            xx = x_ref[...]
            u = _bf((xx * _rstd(xx)) * g_ref[...])
            u_sc[...] = u
            if save_u:
                rest[1][...] = u

        o_ref[...] = jnp.dot(u_sc[...], w_ref[0] if blocked else w_ref[...], preferred_element_type=F32)

    out_shape = [jax.ShapeDtypeStruct((s, n), F32)]
    out_specs = [pl.BlockSpec((tm, tn), lambda i, j: (i, j))]
    if save_u:
        out_shape.append(jax.ShapeDtypeStruct((s, d), BF16))
        out_specs.append(pl.BlockSpec((tm, d), lambda i, j: (i, 0)))
    res = _pcall(
        body, grid=(s // tm, n // tn),
        in_specs=[pl.BlockSpec((tm, d), lambda i, j: (i, 0)), pl.BlockSpec((1, d), lambda i, j: (0, 0)), w_spec],
        out_specs=out_specs, out_shape=out_shape, scratch_shapes=[pltpu.VMEM((tm, d), BF16)],
        compiler_params=_params("parallel", "arbitrary"), name=name)(x, gain, w)
    return res if save_u else res[0]


def _mm_tn(a, b, name, tm=1024, tn=1024, col_blocks=None):
    k, m = a.shape
    n = b.shape[1]
    tk = _row_tile(k, 1024)
    tm, tn = min(tm, m), min(tn, n)
    if col_blocks:
        tn = n // col_blocks
    assert m % tm == 0 and n % tn == 0
    nk = k // tk

    def body(a_ref, b_ref, o_ref, acc):
        kk = pl.program_id(2)

        @pl.when(kk == 0)
        def _():
            acc[...] = jnp.zeros_like(acc)

        acc[...] += _dot_tn(a_ref[...], b_ref[...])

        @pl.when(kk == nk - 1)
        def _():
            if col_blocks:
                o_ref[0] = acc[...]
            else:
                o_ref[...] = acc[...]

    if col_blocks:
        out_spec = pl.BlockSpec((1, tm, tn), lambda i, j, kk: (j, i, 0))
        out_shape = jax.ShapeDtypeStruct((col_blocks, m, tn), F32)
    else:
        out_spec = pl.BlockSpec((tm, tn), lambda i, j, kk: (i, j))
        out_shape = jax.ShapeDtypeStruct((m, n), F32)
    return _pcall(
        body, grid=(m // tm, n // tn, nk),
        in_specs=[pl.BlockSpec((tk, tm), lambda i, j, kk: (kk, i)), pl.BlockSpec((tk, tn), lambda i, j, kk: (kk, j))],
        out_specs=out_spec, out_shape=out_shape,
        scratch_shapes=[pltpu.VMEM((tm, tn), F32)],
        compiler_params=_params("parallel", "parallel", "arbitrary"), name=name)(a, b)


def _in_bwd(dproj, wcat, x, gain, dres, name):
    s, n = dproj.shape
    d = wcat.shape[0]
    tm = _row_tile(s, 1024)
    tk = 1024
    nk = n // tk
    ns = s // tm

    def body(dp_ref, w_ref, x_ref, g_ref, dr_ref, dx_ref, dg_ref, acc):
        i, kk = pl.program_id(0), pl.program_id(1)

        @pl.when(kk == 0)
        def _():
            acc[...] = jnp.zeros_like(acc)

        @pl.when((kk == 0) & (i == 0))
        def _():
            dg_ref[...] = jnp.zeros_like(dg_ref)

        acc[...] += _dot_nt(dp_ref[...], w_ref[...])

        @pl.when(kk == nk - 1)
        def _():
            xx = x_ref[...]
            r = _rstd(xx)
            du = acc[...]
            dg_ref[...] += _colsum(du * (xx * r))
            dx_ref[...] = dr_ref[...] + _rms_bwd(du * g_ref[...], xx, r)

    return _pcall(
        body, grid=(ns, nk),
        in_specs=[pl.BlockSpec((tm, tk), lambda i, kk: (i, kk)), pl.BlockSpec((d, tk), lambda i, kk: (0, kk)),
                  pl.BlockSpec((tm, d), lambda i, kk: (i, 0)), pl.BlockSpec((1, d), lambda i, kk: (0, 0)),
                  pl.BlockSpec((tm, d), lambda i, kk: (i, 0))],
        out_specs=[pl.BlockSpec((tm, d), lambda i, kk: (i, 0)), pl.BlockSpec((8, d), lambda i, kk: (0, 0))],
        out_shape=[jax.ShapeDtypeStruct((s, d), F32), jax.ShapeDtypeStruct((8, d), F32)],
        scratch_shapes=[pltpu.VMEM((tm, d), F32)],
        compiler_params=_params("arbitrary", "arbitrary"), name=name)(dproj, wcat, x, gain, dres)


def _prev_rows_spec(ts, width):
    return pl.BlockSpec((8, width), lambda i: (jnp.maximum(i * (ts // 8) - 1, 0), 0))


def _prescan(proj, cosf, sinf, cw, cb, dtb, eexp, name):
    s = proj.shape[0]
    ts = _row_tile(s, 256)

    def body(xbc_ref, prev_ref, q_ref, k_ref, dt_ref, cos_ref, sin_ref, cw_ref, cb_ref, dtb_ref, e_ref,
             qo_ref, ko_ref, xc_ref, dtx_ref):
        i = pl.program_id(0)
        for st in range(SSM_CONV_DIM // 128):
            sl = slice(st * 128, (st + 1) * 128)
            prev = jnp.where(i > 0, prev_ref[:, sl], 0.0)
            xcat = jnp.concatenate([prev, xbc_ref[:, sl]], axis=0)
            pre = cb_ref[:, sl] + cw_ref[3:4, sl] * xcat[8:8 + ts]
            for j in range(3):
                pre = pre + cw_ref[j:j + 1, sl] * pltpu.roll(xcat, 3 - j, 0)[8:8 + ts]
            xc_ref[:, sl] = pre * _sigmoid(pre)
        cs, sn = cos_ref[...], sin_ref[...]
        for h in range(RET_HEADS):
            sl = slice(h * 128, (h + 1) * 128)
            qh, kh = q_ref[:, sl], k_ref[:, sl]
            qo_ref[:, sl] = qh * cs + pltpu.roll(qh, 64, 1) * sn
            ko_ref[:, sl] = (kh * cs + pltpu.roll(kh, 64, 1) * sn) * (RET_QK_DIM ** -0.5)
        dtv = _softplus(dt_ref[:, 0:128] + dtb_ref[...])
        dtx_ref[...] = _split3_dot(dtv, e_ref[...])

    row = lambda w, c: pl.BlockSpec((ts, w), lambda i: (i, c))
    full = lambda a: pl.BlockSpec(a.shape, lambda i: (0,) * a.ndim)
    return _pcall(
        body, grid=(s // ts,),
        in_specs=[row(4096, 0), _prev_rows_spec(ts, 4096), row(512, C_Q // 512), row(512, C_K // 512),
                  row(DT_PAD, C_DT // DT_PAD), row(128, 0), row(128, 0), full(cw), full(cb), full(dtb), full(eexp)],
        out_specs=[row(512, 0), row(512, 0), row(4096, 0), row(2048, 0)],
        out_shape=[jax.ShapeDtypeStruct((s, 512), F32), jax.ShapeDtypeStruct((s, 512), F32),
                   jax.ShapeDtypeStruct((s, 4096), F32), jax.ShapeDtypeStruct((s, 2048), F32)],
        compiler_params=_params("parallel"), name=name)(proj, proj, proj, proj, proj, cosf, sinf, cw, cb, dtb, eexp)


def _scan_fwd(qr, kr, proj, xc, dtx, ax, consts, name):
    s = qr.shape[0]
    nc = s // CHUNK
    dm, qd, kd, cd, eye, blk = consts

    def body(q_ref, k_ref, v_ref, xc_ref, dtx_ref, ax_ref, dm_ref, qd_ref, kd_ref, cd_ref, eye_ref, blk_ref,
             yr_ref, ys_ref, sst_ref, hst_ref, s_sc, h_sc):
        @pl.when(pl.program_id(0) == 0)
        def _():
            s_sc[...] = jnp.zeros_like(s_sc)
            h_sc[...] = jnp.zeros_like(h_sc)

        sst_ref[0] = s_sc[...]
        hst_ref[0] = h_sc[...]
        for h in range(RET_HEADS):
            ql, vl = slice(h * 128, (h + 1) * 128), slice(h * 256, (h + 1) * 256)
            y, s_new = _ret_chunk_fwd(q_ref[:, ql], k_ref[:, ql], v_ref[:, vl], s_sc[ql, :], dm_ref[h],
                                      qd_ref[:, ql], kd_ref[:, ql], cd_ref[ql, :])
            yr_ref[:, vl] = y
            s_sc[ql, :] = s_new
        eye_v, blk_v = eye_ref[...], blk_ref[...]
        for g in range(SSM_GROUPS):
            sl = slice(g * GROUP_W, (g + 1) * GROUP_W)
            bl = slice(SSM_INNER + g * 128, SSM_INNER + (g + 1) * 128)
            cl = slice(SSM_INNER + SSM_BC + g * 128, SSM_INNER + SSM_BC + (g + 1) * 128)
            y, h_new = _ssd_chunk_fwd(xc_ref[:, sl], dtx_ref[:, sl], xc_ref[:, bl], xc_ref[:, cl], ax_ref[:, sl],
                                      h_sc[:, sl], eye_v, blk_v)
            ys_ref[:, sl] = y
            h_sc[:, sl] = h_new

    row = lambda w, c=0: pl.BlockSpec((CHUNK, w), lambda i: (i, c))
    full = lambda a: pl.BlockSpec(a.shape, lambda i: (0,) * a.ndim)
    return _pcall(
        body, grid=(nc,),
        in_specs=[row(512), row(512), row(1024, C_V // 1024), row(4096), row(2048), full(ax), full(dm), full(qd),
                  full(kd), full(cd), full(eye), full(blk)],
        out_specs=[row(1024), row(2048), pl.BlockSpec((1, 512, 256), lambda i: (i, 0, 0)),
                   pl.BlockSpec((1, 128, 2048), lambda i: (i, 0, 0))],
        out_shape=[jax.ShapeDtypeStruct((s, 1024), F32), jax.ShapeDtypeStruct((s, 2048), F32),
                   jax.ShapeDtypeStruct((nc, 512, 256), F32), jax.ShapeDtypeStruct((nc, 128, 2048), F32)],
        scratch_shapes=[pltpu.VMEM((512, 256), F32), pltpu.VMEM((128, 2048), F32)],
        compiler_params=_params("arbitrary"), name=name)(qr, kr, proj, xc, dtx, ax, dm, qd, kd, cd, eye, blk)


def _scan_bwd(qr, kr, proj, xc, dtx, ax, consts, sst, hst, dyr, dys, dproj, name):
    s = qr.shape[0]
    nc = s // CHUNK
    dm, qd, kd, cd, eye, blk = consts

    def body(q_ref, k_ref, v_ref, xc_ref, dtx_ref, ax_ref, dm_ref, qd_ref, kd_ref, cd_ref, eye_ref, blk_ref,
             sst_ref, hst_ref, dyr_ref, dys_ref, dproj_in, dq_ref, dk_ref, dv_ref, dxc_ref, gdt_ref, da_ref, ds_sc,
             dh_sc):
        @pl.when(pl.program_id(0) == 0)
        def _():
            ds_sc[...] = jnp.zeros_like(ds_sc)
            dh_sc[...] = jnp.zeros_like(dh_sc)
            da_ref[...] = jnp.zeros_like(da_ref)

        for h in range(RET_HEADS):
            ql, vl = slice(h * 128, (h + 1) * 128), slice(h * 256, (h + 1) * 256)
            dq, dk, dv, ds = _ret_chunk_bwd(q_ref[:, ql], k_ref[:, ql], v_ref[:, vl], sst_ref[0, ql, :], dm_ref[h],
                                            qd_ref[:, ql], kd_ref[:, ql], cd_ref[ql, :], dyr_ref[:, vl], ds_sc[ql, :])
            dq_ref[:, ql] = dq
            dk_ref[:, ql] = dk
            dv_ref[:, vl] = _bf(dv)
            ds_sc[ql, :] = ds
        eye_v, blk_v = eye_ref[...], blk_ref[...]
        for g in range(SSM_GROUPS):
            sl = slice(g * GROUP_W, (g + 1) * GROUP_W)
            bl = slice(SSM_INNER + g * 128, SSM_INNER + (g + 1) * 128)
            cl = slice(SSM_INNER + SSM_BC + g * 128, SSM_INNER + SSM_BC + (g + 1) * 128)
            dxs, g_dtx, db, dc, da, dh = _ssd_chunk_bwd(
                xc_ref[:, sl], dtx_ref[:, sl], xc_ref[:, bl], xc_ref[:, cl], ax_ref[:, sl], hst_ref[0, :, sl],
                eye_v, blk_v, dys_ref[:, sl], dh_sc[:, sl])
            dxc_ref[:, sl] = dxs
            dxc_ref[:, bl] = db
            dxc_ref[:, cl] = dc
            gdt_ref[:, sl] = g_dtx
            da_ref[:, sl] += da
            dh_sc[:, sl] = dh

    row = lambda w, c=0: pl.BlockSpec((CHUNK, w), lambda i: (nc - 1 - i, c))
    full = lambda a: pl.BlockSpec(a.shape, lambda i: (0,) * a.ndim)
    return _pcall(
        body, grid=(nc,),
        in_specs=[row(512), row(512), row(1024, C_V // 1024), row(4096), row(2048), full(ax), full(dm), full(qd),
                  full(kd), full(cd), full(eye), full(blk),
                  pl.BlockSpec((1, 512, 256), lambda i: (nc - 1 - i, 0, 0)),
                  pl.BlockSpec((1, 128, 2048), lambda i: (nc - 1 - i, 0, 0)), row(1024), row(2048), ANY],
        out_specs=[row(512), row(512), row(1024, C_V // 1024), row(4096), row(2048),
                   pl.BlockSpec((1, 2048), lambda i: (0, 0))],
        out_shape=[jax.ShapeDtypeStruct((s, 512), F32), jax.ShapeDtypeStruct((s, 512), F32),
                   jax.ShapeDtypeStruct(dproj.shape, BF16), jax.ShapeDtypeStruct((s, 4096), F32),
                   jax.ShapeDtypeStruct((s, 2048), F32), jax.ShapeDtypeStruct((1, 2048), F32)],
        scratch_shapes=[pltpu.VMEM((512, 256), F32), pltpu.VMEM((128, 2048), F32)],
        input_output_aliases={16: 2},
        compiler_params=_params("arbitrary"), name=name)(qr, kr, proj, xc, dtx, ax, dm, qd, kd, cd, eye, blk, sst, hst,
                                                          dyr, dys, dproj)


def _mix_values(yr, g, ys, xs, z, gates, bg, dsk, sn):
    sg, dsg = _silu_and_grad(g)
    ret = _rms_groups(yr, RET_V_DIM)
    yrn = jnp.concatenate([slab * r for slab, r in ret], axis=1) * sg
    sz, dsz = _silu_and_grad(z)
    ys0 = ys + xs * dsk
    ys1 = ys0 * sz
    grp = _rms_groups(ys1, GROUP_W)
    ysh = jnp.concatenate([slab * r for slab, r in grp], axis=1)
    ysn = ysh * sn
    gg = _sigmoid(gates + bg)
    return dict(sg=sg, dsg=dsg, ret=ret, yrn=yrn, sz=sz, dsz=dsz, ys0=ys0, ys1=ys1, grp=grp, ysh=ysh, ysn=ysn,
                gr=gg[:, :D_MODEL], gs=gg[:, D_MODEL:])


def _postscan_fwd(x, yr, ys, xc, proj, bg, dsk, sn, wr, ws, wo, name):
    s = x.shape[0]
    ts = _row_tile(s, 256)

    def body(x_ref, yr_ref, ys_ref, xs_ref, g_ref, z_ref, gt_ref, bg_ref, dsk_ref, sn_ref, wr_ref, ws_ref, wo_ref,
             o_ref):
        m = _mix_values(yr_ref[...], g_ref[...], ys_ref[...], xs_ref[...], z_ref[...], gt_ref[...], bg_ref[...],
                        dsk_ref[...], sn_ref[...])
        merged = m["gr"] * _dot(m["yrn"], wr_ref[...]) + m["gs"] * _dot(m["ysn"], ws_ref[...])
        o_ref[...] = x_ref[...] + _dot(merged, wo_ref[...])

    row = lambda w, c=0: pl.BlockSpec((ts, w), lambda i: (i, c))
    full = lambda a: pl.BlockSpec(a.shape, lambda i: (0,) * a.ndim)
    return _pcall(
        body, grid=(s // ts,),
        in_specs=[row(1024), row(1024), row(2048), row(2048), row(1024, C_G // 1024), row(2048, C_Z // 2048),
                  row(2048, C_GATES // 2048), full(bg), full(dsk), full(sn), full(wr), full(ws), full(wo)],
        out_specs=row(1024), out_shape=jax.ShapeDtypeStruct((s, D_MODEL), F32),
        compiler_params=_params("parallel"), name=name)(x, yr, ys, xc, proj, proj, proj, bg, dsk, sn, wr, ws, wo)


def _postscan_bwd(dout, yr, ys, xc, proj, bg, dsk, sn, wr, ws, wo, name):
    s = dout.shape[0]
    ts = _row_tile(s, 128)

    def body(do_ref, yr_ref, ys_ref, xs_ref, g_ref, z_ref, gt_ref, bg_ref, dsk_ref, sn_ref, wr_ref, ws_ref, wo_ref,
             dyr_ref, dys_ref, dxs_ref, dproj_ref, yrn_ref, ysn_ref, mg_ref, dbr_ref, dbs_ref,
             dbg_ref, ddsk_ref, dsn_ref):
        @pl.when(pl.program_id(0) == 0)
        def _():
            dbg_ref[...] = jnp.zeros_like(dbg_ref)
            ddsk_ref[...] = jnp.zeros_like(ddsk_ref)
            dsn_ref[...] = jnp.zeros_like(dsn_ref)

        xs = xs_ref[...]
        m = _mix_values(yr_ref[...], g_ref[...], ys_ref[...], xs, z_ref[...], gt_ref[...], bg_ref[...],
                        dsk_ref[...], sn_ref[...])
        gr, gs = m["gr"], m["gs"]
        br, bs = _dot(m["yrn"], wr_ref[...]), _dot(m["ysn"], ws_ref[...])
        dmerged = _dot_nt(do_ref[...], wo_ref[...])
        dgt = jnp.concatenate([dmerged * br * gr * (1.0 - gr), dmerged * bs * gs * (1.0 - gs)], axis=1)
        dproj_ref[:, C_GATES:C_GATES + 2048] = _bf(dgt)
        dbg_ref[...] += _colsum(dgt)
        dbr, dbs = dmerged * gr, dmerged * gs
        yrn_ref[...] = _bf(m["yrn"])
        ysn_ref[...] = _bf(m["ysn"])
        mg_ref[...] = _bf(gr * br + gs * bs)
        dbr_ref[...] = _bf(dbr)
        dbs_ref[...] = _bf(dbs)
        dyrn = _dot_nt(dbr, wr_ref[...])
        dysn = _dot_nt(dbs, ws_ref[...])
        rn = jnp.concatenate([slab * r for slab, r in m["ret"]], axis=1)
        dproj_ref[:, C_G:C_G + 1024] = _bf(dyrn * rn * m["dsg"])
        drn = dyrn * m["sg"]
        dyr_ref[...] = jnp.concatenate(
            [_rms_bwd(drn[:, h * RET_V_DIM:(h + 1) * RET_V_DIM], slab, r) for h, (slab, r) in enumerate(m["ret"])], axis=1)
        dsn_ref[...] += _colsum(dysn * m["ysh"])
        dysh = dysn * sn_ref[...]
        dys1 = jnp.concatenate(
            [_rms_bwd(dysh[:, h * GROUP_W:(h + 1) * GROUP_W], slab, r) for h, (slab, r) in enumerate(m["grp"])], axis=1)
        dproj_ref[:, C_Z:C_Z + 2048] = _bf(dys1 * m["ys0"] * m["dsz"])
        dys0 = dys1 * m["sz"]
        dys_ref[...] = dys0
        dxs_ref[...] = dys0 * dsk_ref[...]
        ddsk_ref[...] += _colsum(dys0 * xs)

    row = lambda w, c=0: pl.BlockSpec((ts, w), lambda i: (i, c))
    full = lambda a: pl.BlockSpec(a.shape, lambda i: (0,) * a.ndim)
    acc = lambda w: pl.BlockSpec((8, w), lambda i: (0, 0))
    sds = jax.ShapeDtypeStruct
    return _pcall(
        body, grid=(s // ts,),
        in_specs=[row(1024), row(1024), row(2048), row(2048), row(1024, C_G // 1024), row(2048, C_Z // 2048),
                  row(2048, C_GATES // 2048), full(bg), full(dsk), full(sn), full(wr), full(ws), full(wo)],
        out_specs=[row(1024), row(2048), row(2048), row(NP), row(1024), row(2048), row(1024),
                   row(1024), row(1024), acc(2048), acc(2048), acc(2048)],
        out_shape=[sds((s, 1024), F32), sds((s, 2048), F32), sds((s, 2048), F32), sds((s, NP), BF16),
                   sds((s, 1024), BF16), sds((s, 2048), BF16),
                   sds((s, 1024), BF16), sds((s, 1024), BF16), sds((s, 1024), BF16), sds((8, 2048), F32),
                   sds((8, 2048), F32), sds((8, 2048), F32)],
        compiler_params=_params("arbitrary"), name=name)(dout, yr, ys, xc, proj, proj, proj, bg, dsk, sn, wr, ws, wo)


def _prescan_bwd(proj, dxc, dxs_skip, gdtx, dqr, dkr, cosf, sinf, cw, cb, dtb, eexp_t, dproj, name):
    s = proj.shape[0]
    ts = _row_tile(s, 256)
    nt = s // ts
    m = ts + 8
    width = C_DT + DT_PAD

    def body(xbc_ref, prev_ref, nxt_ref, dt_ref, dxc_ref, dxcn_ref, dsk_ref, dskn_ref, gdt_ref, dq_ref, dk_ref,
             cos_ref, sin_ref, cw_ref, cb_ref, dtb_ref, et_ref, dproj_in, dp_ref, dcw_ref, dcb_ref, ddtb_ref):
        i = pl.program_id(0)

        @pl.when(i == 0)
        def _():
            ddtb_ref[...] = jnp.zeros_like(ddtb_ref)
            dcw_ref[...] = jnp.zeros_like(dcw_ref)
            dcb_ref[...] = jnp.zeros_like(dcb_ref)

        rows = lax.broadcasted_iota(jnp.int32, (m, 128), 0)
        live = (rows < ts) | (i < nt - 1)
        for st in range(SSM_CONV_DIM // 128):
            sl = slice(st * 128, (st + 1) * 128)
            prev = jnp.where(i > 0, prev_ref[:, sl], 0.0)
            xcat = jnp.concatenate([prev, xbc_ref[:, sl], nxt_ref[:, sl]], axis=0)
            shifted = [pltpu.roll(xcat, 3 - j, 0) for j in range(3)] + [xcat]
            pre = cb_ref[:, sl]
            for j in range(SSM_CONV):
                pre = pre + cw_ref[j:j + 1, sl] * shifted[j][8:]
            _, dsilu = _silu_and_grad(pre)
            dxc = jnp.concatenate([dxc_ref[:, sl], dxcn_ref[:, sl]], axis=0)
            if st * 128 < SSM_INNER:
                dxc = dxc + jnp.concatenate([dsk_ref[:, sl], dskn_ref[:, sl]], axis=0)
            dpre = jnp.where(live, dxc * dsilu, 0.0)
            dpt = dpre[0:ts]
            dx = cw_ref[3:4, sl] * dpt
            for j in range(3):
                dx = dx + cw_ref[j:j + 1, sl] * pltpu.roll(dpre, m - (3 - j), 0)[0:ts]
            for j in range(SSM_CONV):
                dcw_ref[8 * j:8 * j + 8, sl] += _colsum(dpt * shifted[j][8:8 + ts])
            dcb_ref[:, sl] += _colsum(dpt)
            dp_ref[:, sl] = _bf(dx)
        cs, sn = cos_ref[...], sin_ref[...]
        for h in range(RET_HEADS):
            sl = slice(h * 128, (h + 1) * 128)
            dq = dq_ref[:, sl]
            dk = dk_ref[:, sl] * (RET_QK_DIM ** -0.5)
            dp_ref[:, C_Q + h * 128:C_Q + (h + 1) * 128] = _bf(dq * cs + pltpu.roll(dq * sn, 64, 1))
            dp_ref[:, C_K + h * 128:C_K + (h + 1) * 128] = _bf(dk * cs + pltpu.roll(dk * sn, 64, 1))
        ddt = _split3_dot(gdt_ref[...], et_ref[...])
        ddt = ddt * _sigmoid(dt_ref[:, 0:128] + dtb_ref[...])
        ddtb_ref[...] += _colsum(ddt)
        dp_ref[:, C_DT:C_DT + 128] = _bf(ddt)
        dp_ref[:, C_DT + 128:C_DT + DT_PAD] = jnp.zeros((ts, DT_PAD - 128), BF16)

    row = lambda w, c=0: pl.BlockSpec((ts, w), lambda i: (i, c))
    nxt = lambda w: pl.BlockSpec((8, w), lambda i: (jnp.minimum((i + 1) * (ts // 8), s // 8 - 1), 0))
    full = lambda a: pl.BlockSpec(a.shape, lambda i: (0,) * a.ndim)
    sds = jax.ShapeDtypeStruct
    return _pcall(
        body, grid=(nt,),
        in_specs=[row(4096), _prev_rows_spec(ts, 4096), nxt(4096), row(DT_PAD, C_DT // DT_PAD), row(4096), nxt(4096),
                  row(2048), nxt(2048), row(2048), row(512), row(512), row(128), row(128), full(cw), full(cb),
                  full(dtb), full(eexp_t), ANY],
        out_specs=[row(width), pl.BlockSpec((32, 4096), lambda i: (0, 0)), pl.BlockSpec((8, 4096), lambda i: (0, 0)),
                   pl.BlockSpec((8, 128), lambda i: (0, 0))],
        out_shape=[sds(dproj.shape, BF16), sds((32, 4096), F32), sds((8, 4096), F32), sds((8, 128), F32)],
        input_output_aliases={17: 0},
        compiler_params=_params("arbitrary"), name=name)(proj, proj, proj, proj, dxc, dxc, dxs_skip, dxs_skip, gdtx,
                                                          dqr, dkr, cosf, sinf, cw, cb, dtb, eexp_t, dproj)


def _xattn_values(x, gain, wq, kv):
    r = _rstd(x)
    h = (x * r) * gain
    q = _dot(h, wq)
    ps, os_ = [], []
    for hd in range(XA_HEADS):
        sl = slice(hd * XA_HEAD_DIM, (hd + 1) * XA_HEAD_DIM)
        sc = _dot_nt(q[:, sl], kv[:, sl]) * (XA_HEAD_DIM ** -0.5)
        e = jnp.exp(sc - jnp.max(sc, axis=-1, keepdims=True))
        p = e / jnp.sum(e, axis=-1, keepdims=True)
        ps.append(p)
        os_.append(_dot(p, kv[:, D_MODEL + hd * XA_HEAD_DIM:D_MODEL + (hd + 1) * XA_HEAD_DIM]))
    return r, h, q, ps, jnp.concatenate(os_, axis=1)


def _xattn_fwd(x, gain, wq, kv, wo, name):
    s = x.shape[0]
    ts = _row_tile(s, 256)

    def body(x_ref, g_ref, wq_ref, kv_ref, wo_ref, o_ref):
        x_ = x_ref[...]
        _, _, _, _, o = _xattn_values(x_, g_ref[...], wq_ref[...], kv_ref[...])
        o_ref[...] = x_ + _dot(o, wo_ref[...])

    row = pl.BlockSpec((ts, D_MODEL), lambda i: (i, 0))
    full = lambda a: pl.BlockSpec(a.shape, lambda i: (0,) * a.ndim)
    return _pcall(
        body, grid=(s // ts,), in_specs=[row, full(gain), full(wq), full(kv), full(wo)], out_specs=row,
        out_shape=jax.ShapeDtypeStruct((s, D_MODEL), F32), compiler_params=_params("parallel"), name=name)(
            x, gain, wq, kv, wo)


def _xattn_bwd(x, dout, gain, wq, kv, wo, name):
    s = x.shape[0]
    m = kv.shape[0]
    ts = _row_tile(s, 256)

    def body(x_ref, do_ref, g_ref, wq_ref, kv_ref, wo_ref, dx_ref, h_ref, dq_ref, o_ref, dkv_ref, dg_ref):
        @pl.when(pl.program_id(0) == 0)
        def _():
            dkv_ref[...] = jnp.zeros_like(dkv_ref)
            dg_ref[...] = jnp.zeros_like(dg_ref)

        x_, do, kvv = x_ref[...], do_ref[...], kv_ref[...]
        r, h, q, ps, o = _xattn_values(x_, g_ref[...], wq_ref[...], kvv)
        dov = _dot_nt(do, wo_ref[...])
        dqs = []
        for hd in range(XA_HEADS):
            sl = slice(hd * XA_HEAD_DIM, (hd + 1) * XA_HEAD_DIM)
            vl = slice(D_MODEL + hd * XA_HEAD_DIM, D_MODEL + (hd + 1) * XA_HEAD_DIM)
            p, doh = ps[hd], dov[:, sl]
            dp = _dot_nt(doh, kvv[:, vl])
            dsc = p * (dp - jnp.sum(dp * p, axis=-1, keepdims=True)) * (XA_HEAD_DIM ** -0.5)
            dqs.append(_dot(dsc, kvv[:, sl]))
            dkv_ref[:, sl] += _dot_tn(dsc, q[:, sl])
            dkv_ref[:, vl] += _dot_tn(p, doh)
        dq = jnp.concatenate(dqs, axis=1)
        dh = _dot_nt(dq, wq_ref[...])
        dg_ref[...] += _colsum(dh * (x_ * r))
        dx_ref[...] = do + _rms_bwd(dh * g_ref[...], x_, r)
        h_ref[...] = _bf(h)
        dq_ref[...] = _bf(dq)
        o_ref[...] = _bf(o)

    row = pl.BlockSpec((ts, D_MODEL), lambda i: (i, 0))
    full = lambda a: pl.BlockSpec(a.shape, lambda i: (0,) * a.ndim)
    sds = jax.ShapeDtypeStruct
    return _pcall(
        body, grid=(s // ts,), in_specs=[row, row, full(gain), full(wq), full(kv), full(wo)],
        out_specs=[row, row, row, row, pl.BlockSpec((m, 2 * D_MODEL), lambda i: (0, 0)),
                   pl.BlockSpec((8, D_MODEL), lambda i: (0, 0))],
        out_shape=[sds((s, D_MODEL), F32), sds((s, D_MODEL), BF16), sds((s, D_MODEL), BF16), sds((s, D_MODEL), BF16),
                   sds((m, 2 * D_MODEL), F32), sds((8, D_MODEL), F32)],
        compiler_params=_params("arbitrary"), name=name)(x, dout, gain, wq, kv, wo)


def _mem_bwd(mem, gain, dkv, wkv, name):
    m = mem.shape[0]

    def body(mem_ref, g_ref, dkv_ref, w_ref, mn_ref, dg_ref):
        mm = mem_ref[...]
        r = _rstd(mm)
        xh = mm * r
        mn_ref[...] = _bf(xh * g_ref[...])
        nb, _, wb = w_ref.shape
        dmn = _dot_nt(dkv_ref[:, 0:wb], w_ref[0])
        for j in range(1, nb):
            dmn = dmn + _dot_nt(dkv_ref[:, j * wb:(j + 1) * wb], w_ref[j])
        dg_ref[...] = jnp.zeros_like(dg_ref) + _colsum(dmn * xh)

    full = lambda a: pl.BlockSpec(a.shape, lambda: (0,) * a.ndim)
    return _pcall(
        body, in_specs=[full(mem), full(gain), full(dkv), full(wkv)],
        out_specs=[pl.BlockSpec((m, D_MODEL), lambda: (0, 0)), pl.BlockSpec((8, D_MODEL), lambda: (0, 0))],
        out_shape=[jax.ShapeDtypeStruct((m, D_MODEL), BF16), jax.ShapeDtypeStruct((8, D_MODEL), F32)],
        compiler_params=pltpu.CompilerParams(vmem_limit_bytes=VMEM_LIMIT), name=name)(mem, gain, dkv, wkv)


def _mlp_fwd(x, gain, w1, w2, name):
    s = x.shape[0]
    ts = _row_tile(s, 512)
    tf = 1024
    nf = D_FF // tf

    def body(x_ref, g_ref, w1_ref, w2_ref, o_ref, h_sc, acc):
        j = pl.program_id(1)

        @pl.when(j == 0)
        def _():
            xx = x_ref[...]
            h_sc[...] = _bf((xx * _rstd(xx)) * g_ref[...])
            acc[...] = jnp.zeros_like(acc)

        a = jnp.dot(h_sc[...], w1_ref[0], preferred_element_type=F32)
        r = jnp.square(jnp.maximum(a, 0.0))
        acc[...] += _dot(r, w2_ref[...])

        @pl.when(j == nf - 1)
        def _():
            o_ref[...] = x_ref[...] + acc[...]

    row = pl.BlockSpec((ts, D_MODEL), lambda i, j: (i, 0))
    return _pcall(
        body, grid=(s // ts, nf),
        in_specs=[row, pl.BlockSpec((1, D_MODEL), lambda i, j: (0, 0)),
                  pl.BlockSpec((1, D_MODEL, tf), lambda i, j: (j, 0, 0)), pl.BlockSpec((tf, D_MODEL), lambda i, j: (j, 0))],
        out_specs=row, out_shape=jax.ShapeDtypeStruct((s, D_MODEL), F32),
        scratch_shapes=[pltpu.VMEM((ts, D_MODEL), BF16), pltpu.VMEM((ts, D_MODEL), F32)],
        compiler_params=_params("parallel", "arbitrary"), name=name)(x, gain, w1, w2)


def _mlp_bwd(x, dout, gain, w1, w2, name):
    s = x.shape[0]
    ts = _row_tile(s, 512)
    tf = 1024
    nf = D_FF // tf

    def body(x_ref, do_ref, g_ref, w1_ref, w2_ref, dx_ref, h_ref, r_ref, da_ref, dg_ref, h_sc, do_sc, acc):
        i, j = pl.program_id(0), pl.program_id(1)

        @pl.when(j == 0)
        def _():
            xx = x_ref[...]
            h_sc[...] = _bf((xx * _rstd(xx)) * g_ref[...])
            do_sc[...] = _bf(do_ref[...])
            acc[...] = jnp.zeros_like(acc)
            h_ref[...] = h_sc[...]

        @pl.when((j == 0) & (i == 0))
        def _():
            dg_ref[...] = jnp.zeros_like(dg_ref)

        a = jnp.dot(h_sc[...], w1_ref[0], preferred_element_type=F32)
        ra = jnp.maximum(a, 0.0)
        r_ref[...] = _bf(ra * ra)
        dr = lax.dot_general(do_sc[...], w2_ref[...], (((1,), (1,)), ((), ())), preferred_element_type=F32)
        da = _bf(dr * 2.0 * ra)
        da_ref[...] = da
        acc[...] += lax.dot_general(da, w1_ref[0], (((1,), (1,)), ((), ())), preferred_element_type=F32)

        @pl.when(j == nf - 1)
        def _():
            xx = x_ref[...]
            r = _rstd(xx)
            dh = acc[...]
            dg_ref[...] += _colsum(dh * (xx * r))
            dx_ref[...] = do_ref[...] + _rms_bwd(dh * g_ref[...], xx, r)

    row = pl.BlockSpec((ts, D_MODEL), lambda i, j: (i, 0))
    ff = pl.BlockSpec((ts, tf), lambda i, j: (i, j))
    sds = jax.ShapeDtypeStruct
    return _pcall(
        body, grid=(s // ts, nf),
        in_specs=[row, row, pl.BlockSpec((1, D_MODEL), lambda i, j: (0, 0)),
                  pl.BlockSpec((1, D_MODEL, tf), lambda i, j: (j, 0, 0)), pl.BlockSpec((tf, D_MODEL), lambda i, j: (j, 0))],
        out_specs=[row, row, ff, ff, pl.BlockSpec((8, D_MODEL), lambda i, j: (0, 0))],
        out_shape=[sds((s, D_MODEL), F32), sds((s, D_MODEL), BF16), sds((s, D_FF), BF16), sds((s, D_FF), BF16),
                   sds((8, D_MODEL), F32)],
        scratch_shapes=[pltpu.VMEM((ts, D_MODEL), BF16), pltpu.VMEM((ts, D_MODEL), BF16), pltpu.VMEM((ts, D_MODEL), F32)],
        compiler_params=_params("arbitrary", "arbitrary"), name=name)(x, dout, gain, w1, w2)


def _final(x, gain, tgt, name):
    s = x.shape[0]
    ts = _row_tile(s, 512)

    def body(x_ref, g_ref, t_ref, dx_ref, loss_ref, dg_ref):
        @pl.when(pl.program_id(0) == 0)
        def _():
            loss_ref[...] = jnp.zeros_like(loss_ref)
            dg_ref[...] = jnp.zeros_like(dg_ref)

        xx = x_ref[...]
        r = _rstd(xx)
        xh = xx * r
        err = xh * g_ref[...] - t_ref[...]
        loss_ref[...] += 0.5 * jnp.sum(jnp.sum(err * err, axis=1, keepdims=True), axis=0, keepdims=True) / D_MODEL
        dy = err * (1.0 / D_MODEL)
        dg_ref[...] += _colsum(dy * xh)
        dx_ref[...] = _rms_bwd(dy * g_ref[...], xx, r)

    row = pl.BlockSpec((ts, D_MODEL), lambda i: (i, 0))
    return _pcall(
        body, grid=(s // ts,), in_specs=[row, pl.BlockSpec((1, D_MODEL), lambda i: (0, 0)), row],
        out_specs=[row, pl.BlockSpec((8, 128), lambda i: (0, 0)), pl.BlockSpec((8, D_MODEL), lambda i: (0, 0))],
        out_shape=[jax.ShapeDtypeStruct((s, D_MODEL), F32), jax.ShapeDtypeStruct((8, 128), F32),
                   jax.ShapeDtypeStruct((8, D_MODEL), F32)],
        compiler_params=_params("arbitrary"), name=name)(x, gain, tgt)


def _as3d(a):
    return a.reshape((-1,) + a.shape[-2:])


def _sum_cast(terms, out_dtype, name, row_want=256):
    shape = terms[0].shape
    t3 = [_as3d(t) for t in terms]
    b, r, c = t3[0].shape
    tr = _row_tile(r, row_want)

    def body(*refs):
        acc = refs[0][...].astype(F32)
        for t in refs[1:-1]:
            acc = acc + t[...].astype(F32)
        refs[-1][...] = acc.astype(out_dtype)

    spec = pl.BlockSpec((1, tr, c), lambda i, j: (i, j, 0))
    out = _pcall(body, grid=(b, r // tr), in_specs=[spec] * len(t3), out_specs=spec,
                 out_shape=jax.ShapeDtypeStruct((b, r, c), out_dtype), compiler_params=_params("parallel", "parallel"),
                 name=name)(*t3)
    return out.reshape(shape)


def _pair_sum(a, b, sel, half_id, out_dtype, name):
    _, h, c = b.shape
    k = sel.shape[0]
    tr = _row_tile(h, 256)
    nt = h // tr

    def body(sel_ref, hid_ref, a_ref, b_ref, o_ref):
        o_ref[...] = (a_ref[...] + b_ref[...]).astype(out_dtype)

    blkshape = (1, tr, c)
    grid_spec = pltpu.PrefetchScalarGridSpec(
        num_scalar_prefetch=2, grid=(k, nt),
        in_specs=[pl.BlockSpec(blkshape, lambda q, j, sel_ref, hid_ref: (sel_ref[q], hid_ref[0] * nt + j, 0)),
                  pl.BlockSpec(blkshape, lambda q, j, sel_ref, hid_ref: (sel_ref[q], j, 0))],
        out_specs=pl.BlockSpec(blkshape, lambda q, j, sel_ref, hid_ref: (q, j, 0)))
    return _pcall(body, grid_spec=grid_spec, out_shape=jax.ShapeDtypeStruct((k, h, c), out_dtype),
                  compiler_params=_params("parallel", "parallel"), name=name)(sel, half_id, a, b)


def _adamw(w, g, m, v, name):
    shape = w.shape
    w3, g3, m3, v3 = _as3d(w), _as3d(g), _as3d(m), _as3d(v)
    b, r, c = w3.shape
    tr = _row_tile(r, 256)

    def body(w_ref, g_ref, m_ref, v_ref, d_ref, mo_ref, vo_ref):
        gg = g_ref[...]
        mn = ADAM_B1 * m_ref[...] + (1.0 - ADAM_B1) * gg
        vn = ADAM_B2 * v_ref[...] + (1.0 - ADAM_B2) * jnp.square(gg)
        m_hat = mn / (1.0 - ADAM_B1 ** ADAM_STEP)
        v_hat = vn / (1.0 - ADAM_B2 ** ADAM_STEP)
        d_ref[...] = -ADAM_LR * (m_hat / (jnp.sqrt(v_hat) + ADAM_EPS) + ADAM_WD * w_ref[...])
        mo_ref[...] = mn
        vo_ref[...] = vn

    spec = pl.BlockSpec((1, tr, c), lambda i, j: (i, j, 0))
    sd = jax.ShapeDtypeStruct((b, r, c), F32)
    d, mo, vo = _pcall(body, grid=(b, r // tr), in_specs=[spec] * 4, out_specs=[spec] * 3, out_shape=[sd] * 3,
                       compiler_params=_params("parallel", "parallel"), name=name)(w3, g3, m3, v3)
    return d.reshape(shape), mo.reshape(shape), vo.reshape(shape)


ANY = pl.BlockSpec(memory_space=pl.ANY)


def _place():
    return lax.axis_index("x"), lax.axis_index("y"), lax.axis_index("c")


def _flip(x, y, r):
    return (1 - x if r & 2 else x), (1 - y if r & 1 else y)


def _chip_gather(arrs, name):
    n = len(arrs)
    nl = arrs[0].shape[0]
    half = nl // 2
    assert half * 2 == nl

    def body(*refs):
        ins, outs = refs[:n], refs[n:2 * n]
        send, recv, fsend, frecv = refs[2 * n:]
        x, y, c = _place()
        blk = 2 * x + y
        mine = pl.ds(c * half, half)
        theirs = pl.ds((1 - c) * half, half)

        def ici(r, a):
            cx, cy = _flip(x, y, r)
            return pltpu.make_async_remote_copy(
                src_ref=ins[a].at[mine], dst_ref=outs[a].at[mine, blk], send_sem=send.at[(r - 1) * n + a],
                recv_sem=recv.at[(r - 1) * n + a], device_id=(cx, cy, c), device_id_type=MESH)

        def ici_in(r, a):
            cx, cy = _flip(x, y, r)
            return pltpu.make_async_remote_copy(
                src_ref=ins[a].at[mine], dst_ref=outs[a].at[mine, 2 * cx + cy], send_sem=send.at[(r - 1) * n + a],
                recv_sem=recv.at[(r - 1) * n + a], device_id=(cx, cy, c), device_id_type=MESH)

        def d2d(r, a, rows):
            cx, cy = _flip(x, y, r)
            ref = outs[a].at[rows, 2 * cx + cy]
            return pltpu.make_async_remote_copy(
                src_ref=ref, dst_ref=ref, send_sem=fsend.at[(r - 1) * n + a], recv_sem=frecv.at[(r - 1) * n + a],
                device_id=(x, y, 1 - c), device_id_type=MESH)

        for r in (1, 2, 3):
            for a in range(n):
                ici(r, a).start()
        for r in (1, 2, 3):
            for a in range(n):
                ici_in(r, a).wait_recv()
                d2d(r, a, mine).start()
        for r in (1, 2, 3):
            for a in range(n):
                d2d(r, a, theirs).wait_recv()
        for r in (1, 2, 3):
            for a in range(n):
                ici(r, a).wait_send()
                d2d(r, a, mine).wait_send()

    return _pcall(
        body, in_specs=[ANY] * n, out_specs=[ANY] * n,
        out_shape=[jax.ShapeDtypeStruct((a.shape[0], 4) + a.shape[1:], a.dtype) for a in arrs],
        scratch_shapes=[pltpu.SemaphoreType.DMA((3 * n,)), pltpu.SemaphoreType.DMA((3 * n,)),
                        pltpu.SemaphoreType.DMA((3 * n,)), pltpu.SemaphoreType.DMA((3 * n,))],
        name=name)(*arrs)


def _sibling_swap(arrs, name):
    n = len(arrs)
    halves = [a.shape[1] // 2 for a in arrs]

    def body(*refs):
        ins, outs = refs[:n], refs[n:2 * n]
        send, recv = refs[2 * n:]
        x, y, c = _place()
        cps = [pltpu.make_async_remote_copy(
            src_ref=ins[a].at[:, pl.ds((1 - c) * halves[a], halves[a])], dst_ref=outs[a], send_sem=send.at[a],
            recv_sem=recv.at[a], device_id=(x, y, 1 - c), device_id_type=MESH) for a in range(n)]
        for cp in cps:
            cp.start()
        for cp in cps:
            cp.wait()

    return _pcall(
        body, in_specs=[ANY] * n, out_specs=[ANY] * n,
        out_shape=[jax.ShapeDtypeStruct((4, h, a.shape[2]), a.dtype) for a, h in zip(arrs, halves)],
        scratch_shapes=[pltpu.SemaphoreType.DMA((n,)), pltpu.SemaphoreType.DMA((n,))], name=name)(*arrs)


def _chip_exchange(arrs, name):
    n = len(arrs)

    def body(*refs):
        ins, outs = refs[:n], refs[n:2 * n]
        send, recv = refs[2 * n:]
        x, y, c = _place()
        cps = []
        for r in (1, 2, 3):
            cx, cy = _flip(x, y, r)
            for a in range(n):
                cps.append(pltpu.make_async_remote_copy(
                    src_ref=ins[a].at[r - 1], dst_ref=outs[a].at[r - 1], send_sem=send.at[(r - 1) * n + a],
                    recv_sem=recv.at[(r - 1) * n + a], device_id=(cx, cy, c), device_id_type=MESH))
        for cp in cps:
            cp.start()
        for cp in cps:
            cp.wait()

    return _pcall(
        body, in_specs=[ANY] * n, out_specs=[ANY] * n,
        out_shape=[jax.ShapeDtypeStruct(a.shape, a.dtype) for a in arrs],
        scratch_shapes=[pltpu.SemaphoreType.DMA((3 * n,)), pltpu.SemaphoreType.DMA((3 * n,))], name=name)(*arrs)


def _sibling_send(arrs, name):
    n = len(arrs)

    def body(*refs):
        ins, outs = refs[:n], refs[n:2 * n]
        send, recv = refs[2 * n:]
        x, y, c = _place()
        cps = [pltpu.make_async_remote_copy(
            src_ref=ins[a], dst_ref=outs[a], send_sem=send.at[a], recv_sem=recv.at[a],
            device_id=(x, y, 1 - c), device_id_type=MESH) for a in range(n)]
        for cp in cps:
            cp.start()
        for cp in cps:
            cp.wait()

    return _pcall(
        body, in_specs=[ANY] * n, out_specs=[ANY] * n,
        out_shape=[jax.ShapeDtypeStruct(a.shape, a.dtype) for a in arrs],
        scratch_shapes=[pltpu.SemaphoreType.DMA((n,)), pltpu.SemaphoreType.DMA((n,))], name=name)(*arrs)


def _gather8(v, reduce, name):
    rows, w = v.shape

    def body(v_ref, out_ref, buf, send_sems, recv_sems):
        x, y, c = _place()
        me, sibling = (x, y, c), (x, y, 1 - c)
        chips = [_flip(x, y, r) for r in (1, 2, 3)]
        dst = out_ref if not reduce else buf

        def slot(px, py, pc):
            return dst.at[4 * px + 2 * py + pc]

        def copy(k, block, to, src=None):
            return pltpu.make_async_remote_copy(
                src_ref=slot(*block) if src is None else src, dst_ref=slot(*block), send_sem=send_sems.at[k],
                recv_sem=recv_sems.at[k], device_id=to, device_id_type=MESH)

        dst[4 * x + 2 * y + c] = v_ref[...]
        first = [copy(0, me, sibling, src=v_ref)]
        first += [copy(1 + j, me, (*chip, c), src=v_ref) for j, chip in enumerate(chips)]
        for cp in first:
            cp.start()
        passed = [copy(4 + j, (*chip, c), sibling) for j, chip in enumerate(chips)]
        for j, chip in enumerate(chips):
            copy(1 + j, (*chip, c), me).wait_recv()
            passed[j].start()
        copy(0, sibling, me).wait_recv()
        for j, chip in enumerate(chips):
            copy(4 + j, (*chip, 1 - c), me).wait_recv()
        for cp in first + passed:
            cp.wait_send()
        if reduce:
            acc = buf[0]
            for d in range(1, 8):
                acc = acc + buf[d]
            out_ref[...] = acc

    vm = pl.BlockSpec(memory_space=pltpu.VMEM)
    scratch = [pltpu.VMEM((8, rows, w) if reduce else (8, 8, 128), F32), pltpu.SemaphoreType.DMA((7,)),
               pltpu.SemaphoreType.DMA((7,))]
    out_shape = jax.ShapeDtypeStruct((rows, w) if reduce else (8, rows, w), F32)
    return _pcall(body, in_specs=[vm], out_specs=vm, out_shape=out_shape, scratch_shapes=scratch,
                  compiler_params=pltpu.CompilerParams(vmem_limit_bytes=VMEM_LIMIT), name=name)(v)


SMALL = [("norm_mix", 1024), ("b_gate", 2048), ("conv_b", 4096), ("dt_bias", 32), ("a_log", 32), ("d_skip", 32),
         ("ssm_norm", 2048), ("norm_xa", 1024), ("norm_mem", 1024), ("norm_mlp", 1024)]


def _rows_of(width):
    return max(1, width // 1024)


def _pack_rows(pieces):
    out = []
    for p in pieces:
        p = p.astype(F32)
        if p.shape[-1] < 1024:
            p = jnp.pad(p, ((0, 0), (0, 1024 - p.shape[-1])))
        out.append(p.reshape(-1, 1024))
    cat = jnp.concatenate(out, axis=0)
    pad = (-cat.shape[0]) % 8
    return jnp.pad(cat, ((0, pad), (0, 0))) if pad else cat


def _unpack_rows(packed, widths_rows):
    out, at = [], 0
    for r, w in widths_rows:
        k = r * _rows_of(w)
        p = packed[at:at + k]
        at += k
        out.append(p[:, :w] if w < 1024 else p.reshape(r, w))
    return out


def _to_cat(w):
    pieces = [w[:, O_XBC:O_XBC + 4096], w[:, O_Q:O_Q + 512], w[:, O_K:O_K + 512], w[:, O_DT:O_DT + 32],
              jnp.zeros((w.shape[0], DT_PAD - 32), w.dtype), w[:, O_Z:O_Z + 2048], w[:, O_GATES:O_GATES + 2048],
              w[:, O_V:O_V + 1024], w[:, O_G:O_G + 1024]]
    return jnp.concatenate(pieces, axis=1)


def _from_cat(g):
    pieces = [g[:, C_Q:C_Q + 512], g[:, C_K:C_K + 512], g[:, C_V:C_V + 1024], g[:, C_G:C_G + 1024],
              g[:, C_Z:C_Z + 2048], g[:, C_XBC:C_XBC + 4096], g[:, C_DT:C_DT + 32], g[:, C_GATES:C_GATES + 2048]]
    return jnp.concatenate(pieces, axis=1)


PACK_ROWS = [("w_br_ret", 256), ("w_br_ssm", 512), ("w_out", 256), ("xa_wq", 256), ("xa_wo", 256), ("mlp_w1", 1024),
             ("mlp_w2", 1024)]
PACK_N = sum(r for _, r in PACK_ROWS)


def _blocks_rows(g, n):
    return g.reshape(4, n, g.shape[-1])


def _blocks_cols(g, n):
    return g.reshape(g.shape[0], 4, n).transpose(1, 0, 2)


def kernel(x, mem, positions, norm_mix, w_in, b_gate, conv_w, conv_b, dt_bias, a_log, d_skip, ssm_norm, w_br_ret, w_br_ssm, w_out, norm_xa, norm_mem, xa_wq, xa_wkv, xa_wo, norm_mlp, mlp_w1, mlp_w2, norm_final, loss_target, m_norm_mix, m_w_in, m_b_gate, m_conv_w, m_conv_b, m_dt_bias, m_a_log, m_d_skip, m_ssm_norm, m_w_br_ret, m_w_br_ssm, m_w_out, m_norm_xa, m_norm_mem, m_xa_wq, m_xa_wkv, m_xa_wo, m_norm_mlp, m_mlp_w1, m_mlp_w2, m_norm_final, v_norm_mix, v_w_in, v_b_gate, v_conv_w, v_conv_b, v_dt_bias, v_a_log, v_d_skip, v_ssm_norm, v_w_br_ret, v_w_br_ssm, v_w_out, v_norm_xa, v_norm_mem, v_xa_wq, v_xa_wkv, v_xa_wo, v_norm_mlp, v_mlp_w1, v_mlp_w2, v_norm_final):
    W = dict(norm_mix=norm_mix, w_in=w_in, b_gate=b_gate, conv_w=conv_w, conv_b=conv_b, dt_bias=dt_bias, a_log=a_log,
             d_skip=d_skip, ssm_norm=ssm_norm, w_br_ret=w_br_ret, w_br_ssm=w_br_ssm, w_out=w_out, norm_xa=norm_xa,
             norm_mem=norm_mem, xa_wq=xa_wq, xa_wkv=xa_wkv, xa_wo=xa_wo, norm_mlp=norm_mlp, mlp_w1=mlp_w1,
             mlp_w2=mlp_w2, norm_final=norm_final)
    M = dict(norm_mix=m_norm_mix, w_in=m_w_in, b_gate=m_b_gate, conv_w=m_conv_w, conv_b=m_conv_b, dt_bias=m_dt_bias,
             a_log=m_a_log, d_skip=m_d_skip, ssm_norm=m_ssm_norm, w_br_ret=m_w_br_ret, w_br_ssm=m_w_br_ssm,
             w_out=m_w_out, norm_xa=m_norm_xa, norm_mem=m_norm_mem, xa_wq=m_xa_wq, xa_wkv=m_xa_wkv, xa_wo=m_xa_wo,
             norm_mlp=m_norm_mlp, mlp_w1=m_mlp_w1, mlp_w2=m_mlp_w2, norm_final=m_norm_final)
    V = dict(norm_mix=v_norm_mix, w_in=v_w_in, b_gate=v_b_gate, conv_w=v_conv_w, conv_b=v_conv_b, dt_bias=v_dt_bias,
             a_log=v_a_log, d_skip=v_d_skip, ssm_norm=v_ssm_norm, w_br_ret=v_w_br_ret, w_br_ssm=v_w_br_ssm,
             w_out=v_w_out, norm_xa=v_norm_xa, norm_mem=v_norm_mem, xa_wq=v_xa_wq, xa_wkv=v_xa_wkv, xa_wo=v_xa_wo,
             norm_mlp=v_norm_mlp, mlp_w1=v_mlp_w1, mlp_w2=v_mlp_w2, norm_final=v_norm_final)
    nl = w_in.shape[0]
    s = x.shape[1]
    x0 = x[0]
    mem2 = mem[0]
    tgt = loss_target[0]
    blk = 2 * lax.axis_index("x") + lax.axis_index("y")

    big = [k for k, _ in PACK_ROWS] + ["w_in", "xa_wkv"]
    shards = [W[k].astype(BF16) for k in big]
    gathered = _chip_gather(shards, "gather_weights")
    zero = jnp.zeros((), jnp.int32)
    G = {k: lax.dynamic_update_slice(g, sh[:, None], (zero, blk.astype(jnp.int32), zero, zero))
         for k, g, sh in zip(big, gathered, shards)}
    cw_all = _gather8(conv_w.reshape(nl * SSM_CONV, 1024), False, "gather_conv_w")
    cw_full = cw_all.reshape(4, 2, nl, SSM_CONV, 1024)[:, 0].transpose(1, 2, 0, 3).reshape(nl, SSM_CONV, SSM_CONV_DIM)

    offs = {}
    at = 0
    for k, r in PACK_ROWS:
        offs[k] = (at, r)
        at += r

    def rows_weight(k, l):
        return G[k][l].reshape(-1, D_MODEL)

    inv_freq = ROPE_THETA ** (-jnp.arange(0, RET_QK_DIM, 2, dtype=F32) / RET_QK_DIM)
    ang = positions.astype(F32)[0][:, None] * inv_freq
    cos, sin = jnp.cos(ang), jnp.sin(ang)
    cosf = jnp.concatenate([cos, cos], axis=1)
    sinf = jnp.concatenate([-sin, sin], axis=1)
    dm, qd, kd, cd = (jnp.asarray(c) for c in _ret_constants())
    eye, blkm = (jnp.asarray(c) for c in _ssd_constants())
    consts = (dm, qd, kd, cd, eye, blkm)
    e_np = _head_expand()
    eexp = jnp.asarray(e_np, BF16)
    eexp_t = jnp.asarray(e_np.T.copy(), BF16)

    saved = []
    xcur = x0
    for l in range(nl):
        wcat = _to_cat(jnp.concatenate([G["w_in"][l, j] for j in range(4)], axis=1))
        wr, ws, wo = rows_weight("w_br_ret", l), rows_weight("w_br_ssm", l), rows_weight("w_out", l)
        wq, wxo, w2 = rows_weight("xa_wq", l), rows_weight("xa_wo", l), rows_weight("mlp_w2", l)
        w1, wkv = G["mlp_w1"][l], G["xa_wkv"][l]
        cw, cb = cw_full[l], conv_b[l][None]
        dtb = jnp.pad(dt_bias[l], (0, 128 - SSM_HEADS))[None]
        ax = jnp.repeat(-jnp.exp(a_log[l]), 64)[None]
        dsk = jnp.repeat(d_skip[l], 64)[None]
        bg, sn = b_gate[l][None], ssm_norm[l][None]
        proj, u = _nmm(xcur, norm_mix[l][None], wcat, "in_proj", save_u=True)
        qr, kr, xc, dtx = _prescan(proj, cosf, sinf, cw, cb, dtb, eexp, "prescan")
        yr, ys, sst, hst = _scan_fwd(qr, kr, proj, xc, dtx, ax, consts, "scan_fwd")
        x1 = _postscan_fwd(xcur, yr, ys, xc, proj, bg, dsk, sn, wr, ws, wo, "postscan")
        kv = _bf(_nmm(mem2, norm_mem[l][None], wkv, "mem_kv"))
        x2 = _xattn_fwd(x1, norm_xa[l][None], wq, kv, wxo, "xattn")
        x3 = _mlp_fwd(x2, norm_mlp[l][None], w1, w2, "mlp")
        saved.append(dict(x0=xcur, x1=x1, x2=x2, proj=proj, u=u, qr=qr, kr=kr, xc=xc, dtx=dtx, yr=yr, ys=ys, sst=sst,
                          hst=hst, kv=kv, wcat=wcat, wr=wr, ws=ws, wo=wo, wq=wq, wxo=wxo, w1=w1, w2=w2, wkv=wkv, cw=cw,
                          cb=cb, dtb=dtb, ax=ax, dsk=dsk, bg=bg, sn=sn))
        xcur = x3

    dx, loss_acc, dnf = _final(xcur, norm_final[None], tgt, "final")
    loss = lax.psum(loss_acc[0, 0], ("x", "y", "c"))

    small_g = [None] * nl
    p1, pin, pkv = [None] * nl, [None] * nl, [None] * nl
    for l in reversed(range(nl)):
        sv = saved[l]
        dx2, hm, rm, dam, dg_mlp = _mlp_bwd(sv["x2"], dx, norm_mlp[l][None], sv["w1"], sv["w2"], "mlp_bwd")
        dw1 = _mm_tn(hm, dam, "dw_mlp1", col_blocks=4)
        dw2 = _mm_tn(rm, dx, "dw_mlp2")
        dx1, hx, dqx, ox, dkv, dg_xa = _xattn_bwd(sv["x1"], dx2, norm_xa[l][None], sv["wq"], sv["kv"], sv["wxo"],
                                                  "xattn_bwd")
        dwq = _mm_tn(hx, dqx, "dw_xq")
        dwxo = _mm_tn(ox, dx2, "dw_xo")
        memn, dg_mem = _mem_bwd(mem2, norm_mem[l][None], dkv, sv["wkv"], "mem_bwd")
        dwkv = _mm_tn(memn, dkv, "dw_xkv", col_blocks=4)
        (dyr, dys, dxs_skip, dproj, yrn, ysn, mg, dbr, dbs, dbg, ddsk, dsn) = _postscan_bwd(
            dx1, sv["yr"], sv["ys"], sv["xc"], sv["proj"], sv["bg"], sv["dsk"], sv["sn"], sv["wr"], sv["ws"], sv["wo"],
            "postscan_bwd")
        dwo = _mm_tn(mg, dx1, "dw_out")
        dwr = _mm_tn(yrn, dbr, "dw_br_ret")
        dws = _mm_tn(ysn, dbs, "dw_br_ssm")
        dqr, dkr, dproj, dxc, gdtx, da_cols = _scan_bwd(sv["qr"], sv["kr"], sv["proj"], sv["xc"], sv["dtx"], sv["ax"],
                                                        consts, sv["sst"], sv["hst"], dyr, dys, dproj, "scan_bwd")
        dproj, dcw, dcb, ddtb = _prescan_bwd(sv["proj"], dxc, dxs_skip, gdtx, dqr, dkr, cosf, sinf, sv["cw"], sv["cb"],
                                             sv["dtb"], eexp_t, dproj, "prescan_bwd")
        dwcat = _mm_tn(sv["u"], dproj, "dw_in")
        dx, dg_mix = _in_bwd(dproj, sv["wcat"], sv["x0"], norm_mix[l][None], dx1, "in_bwd")

        da_log = (da_cols.reshape(SSM_HEADS, 64).sum(axis=1)) * (-jnp.exp(a_log[l]))
        dd_skip = ddsk[0].reshape(SSM_HEADS, 64).sum(axis=1)
        small_g[l] = [dg_mix[0:1], dbg[0:1], dcb[0:1], ddtb[0:1, :SSM_HEADS], da_log[None], dd_skip[None], dsn[0:1],
                      dg_xa[0:1], dg_mem[0:1], dg_mlp[0:1], dcw[0::8]]
        p1[l] = jnp.concatenate([_blocks_rows(dwr, 256), _blocks_rows(dws, 512), _blocks_rows(dwo, 256),
                                 _blocks_rows(dwq, 256), _blocks_rows(dwxo, 256), dw1,
                                 _blocks_rows(dw2, 1024)], axis=1)
        pin[l] = _blocks_cols(_from_cat(dwcat), IN_DIM // 4)
        pkv[l] = dwkv

    grad_x = dx[None]

    pieces = []
    for l in range(nl):
        pieces += small_g[l]
    pieces.append(dnf[0:1])
    small_sum = _gather8(_pack_rows(pieces), True, "reduce_small")
    layout = []
    for l in range(nl):
        layout += [(1, w) for _, w in SMALL] + [(SSM_CONV, SSM_CONV_DIM)]
    layout.append((1, 1024))
    red = _unpack_rows(small_sum, layout)
    per = len(SMALL) + 1
    g_small = {k: jnp.concatenate([red[l * per + i] for l in range(nl)], axis=0) for i, (k, _) in enumerate(SMALL)}
    g_convw_full = jnp.stack([red[l * per + len(SMALL)] for l in range(nl)])
    g_small["conv_w"] = lax.dynamic_slice_in_dim(g_convw_full, blk * 1024, 1024, axis=2)
    g_small["norm_final"] = red[-1][0]

    P = p1 + pin + pkv
    c = lax.axis_index("c")
    got = _sibling_swap(P, "grads_core_swap")
    half_id = c.astype(jnp.int32)[None]
    blk = blk.astype(jnp.int32)
    sel_own = blk[None]
    sel_rem = jnp.stack([blk ^ 1, blk ^ 2, blk ^ 3])
    own = [_pair_sum(p, g_, sel_own, half_id, F32, "chip_sum_own")[0] for p, g_ in zip(P, got)]
    out_b = [_pair_sum(p, g_, sel_rem, half_id, BF16, "chip_sum_send") for p, g_ in zip(P, got)]
    inc = _chip_exchange(out_b, "grads_chip_exchange")
    red_half = [_sum_cast([o, i_[0], i_[1], i_[2]], F32, "grads_total") for o, i_ in zip(own, inc)]
    sib_half = _sibling_send(red_half, "grads_core_join")

    def whole(i):
        mine_, theirs_ = red_half[i], sib_half[i]
        return jnp.concatenate([jnp.where(c == 0, mine_, theirs_), jnp.where(c == 0, theirs_, mine_)], axis=0)

    full1 = [whole(l) for l in range(nl)]
    grads = dict(g_small)
    for k, r in PACK_ROWS:
        o, _ = offs[k]
        grads[k] = jnp.stack([f[o:o + r] for f in full1])
    grads["w_in"] = jnp.stack([whole(nl + l) for l in range(nl)])
    grads["xa_wkv"] = jnp.stack([whole(2 * nl + l) for l in range(nl)])

    delta, new_m, new_v = {}, {}, {}
    for k in ["w_in", "xa_wkv"] + [k for k, _ in PACK_ROWS]:
        delta[k], new_m[k], new_v[k] = _adamw(W[k], grads[k], M[k], V[k], "adamw_" + k)
    small_names = [k for k, _ in SMALL] + ["conv_w", "norm_final"]

    def pack_small(src):
        ps = []
        for k in small_names:
            a = src[k]
            ps.append(a.reshape(-1, a.shape[-1]) if a.ndim > 1 else a[None])
        return _pack_rows(ps)

    ds_, ms_, vs_ = _adamw(pack_small(W), pack_small(grads), pack_small(M), pack_small(V), "adamw_small")
    lay2 = []
    for k in small_names:
        a = W[k]
        lay2.append((int(np.prod(a.shape[:-1])) if a.ndim > 1 else 1, a.shape[-1]))
    for src, dst in ((ds_, delta), (ms_, new_m), (vs_, new_v)):
        for k, piece in zip(small_names, _unpack_rows(src, lay2)):
            dst[k] = piece.reshape(W[k].shape)

    names = ["norm_mix", "w_in", "b_gate", "conv_w", "conv_b", "dt_bias", "a_log", "d_skip", "ssm_norm", "w_br_ret",
             "w_br_ssm", "w_out", "norm_xa", "norm_mem", "xa_wq", "xa_wkv", "xa_wo", "norm_mlp", "mlp_w1", "mlp_w2",
             "norm_final"]
    return (loss, grad_x, *[grads[n] for n in names], *[delta[n] for n in names], *[new_m[n] for n in names],
            *[new_v[n] for n in names])
```

```python
import numpy as np
import jax
import jax.numpy as jnp
from jax import lax
from jax.experimental import pallas as pl
from jax.experimental.pallas import tpu as pltpu

F32 = jnp.float32
BF16 = jnp.bfloat16
MESH = pl.DeviceIdType.MESH

D_MODEL = 1024
CHUNK = 64
EPS = 1e-6
RET_HEADS = 4
RET_QK_DIM = 128
RET_V_DIM = 256
RET_QK = 512
RET_V = 1024
ROPE_THETA = 10000.0
SSM_INNER = 2048
SSM_HEADS = 32
SSM_GROUPS = 8
SSM_STATE = 128
SSM_CONV = 4
SSM_BC = 1024
SSM_CONV_DIM = 4096
XA_HEADS = 4
XA_HEAD_DIM = 256
D_FF = 4096
GROUP_W = 256

DT_PAD = 1024
IN_DIM = 11296
NP = 12288
C_XBC, C_Q, C_K, C_DT, C_Z, C_GATES, C_V, C_G = 0, 4096, 4608, 5120, 6144, 8192, 10240, 11264
O_Q, O_K, O_V, O_G, O_Z, O_XBC, O_DT, O_GATES = 0, 512, 1024, 2048, 3072, 5120, 9216, 9248

ADAM_LR = 0.001
ADAM_B1 = 0.9
ADAM_B2 = 0.999
ADAM_EPS = 1e-08
ADAM_WD = 0.01
ADAM_STEP = 10

VMEM_LIMIT = 56 * 1024 * 1024


def _params(*sem):
    return pltpu.CompilerParams(dimension_semantics=sem, vmem_limit_bytes=VMEM_LIMIT)


_CARRY = []


def _carry(comm):
    if comm is not None:
        _CARRY.append(comm)


def _pcall(body, **kw):
    if _CARRY:
        return _hosted(body, _CARRY.pop(), kw)
    return pl.pallas_call(body, **kw)


class _Comm:
    def __init__(self, ins, out_shapes, sems, start, finish, aliases=None):
        self.ins, self.out_shapes, self.sems = list(ins), list(out_shapes), list(sems)
        self.start, self.finish, self.aliases = start, finish, dict(aliases or {})
        self.results = None


def _hosted(body, comm, kw):
    in_specs = list(kw.pop("in_specs"))
    out_specs, out_shape = kw.pop("out_specs"), kw.pop("out_shape")
    single = not isinstance(out_shape, (list, tuple))
    if single:
        out_specs, out_shape = [out_specs], [out_shape]
    out_specs, out_shape = list(out_specs), list(out_shape)
    scratch = list(kw.pop("scratch_shapes", []))
    grid = tuple(kw.get("grid", ()))
    aliases = dict(kw.pop("input_output_aliases", {}))
    n_in, n_out, n_sc = len(in_specs), len(out_shape), len(scratch)
    c_in, c_out = len(comm.ins), len(comm.out_shapes)
    for i, o in comm.aliases.items():
        aliases[n_in + i] = n_out + o
    kw["compiler_params"] = _params(*(["arbitrary"] * len(grid)))

    def wrapped(*refs):
        at = 0
        parts = []
        for cnt in (n_in, c_in, n_out, c_out, n_sc):
            parts.append(refs[at:at + cnt])
            at += cnt
        a, ci, b, co, s = parts
        cs = refs[at:]
        first, last = None, None
        for d, size in enumerate(grid):
            f, l_ = pl.program_id(d) == 0, pl.program_id(d) == size - 1
            first = f if first is None else first & f
            last = l_ if last is None else last & l_

        @pl.when(first)
        def _():
            comm.start(ci, co, cs)

        body(*a, *b, *s)

        @pl.when(last)
        def _():
            comm.finish(ci, co, cs)

    call = _pcall(wrapped, in_specs=in_specs + [ANY] * c_in, out_specs=out_specs + [ANY] * c_out,
                  out_shape=out_shape + comm.out_shapes, scratch_shapes=scratch + comm.sems,
                  input_output_aliases=aliases, **kw)

    def run(*ops):
        res = call(*ops, *comm.ins)
        comm.results = list(res[n_out:])
        return res[0] if single else list(res[:n_out])

    return run


def _run_comm(comm, name):
    def body(*refs):
        c_in, c_out = len(comm.ins), len(comm.out_shapes)
        ci, co, cs = refs[:c_in], refs[c_in:c_in + c_out], refs[c_in + c_out:]
        comm.start(ci, co, cs)
        comm.finish(ci, co, cs)

    aliases = {i: o for i, o in comm.aliases.items()}
    res = _pcall(body, in_specs=[ANY] * len(comm.ins), out_specs=[ANY] * len(comm.out_shapes),
                 out_shape=comm.out_shapes, scratch_shapes=comm.sems, input_output_aliases=aliases, name=name)(*comm.ins)
    comm.results = list(res)
    return comm.results


def _bf(a):
    return a.astype(BF16)


def _dot(a, b):
    return jnp.dot(_bf(a), _bf(b), preferred_element_type=F32)


def _dot_nt(a, b):
    return lax.dot_general(_bf(a), _bf(b), (((1,), (1,)), ((), ())), preferred_element_type=F32)


def _dot_tn(a, b):
    return lax.dot_general(_bf(a), _bf(b), (((0,), (0,)), ((), ())), preferred_element_type=F32)


def _colsum(a):
    return jnp.sum(a, axis=0, keepdims=True)


def _rstd(x):
    return lax.rsqrt(jnp.mean(x * x, axis=-1, keepdims=True) + EPS)


def _rms_bwd(dy, x, rstd):
    xh = x * rstd
    return rstd * (dy - xh * jnp.mean(dy * xh, axis=-1, keepdims=True))


def _sigmoid(x):
    return 1.0 / (1.0 + jnp.exp(-x))


def _silu_and_grad(x):
    s = _sigmoid(x)
    return x * s, s + x * s * (1.0 - s)


def _softplus(x):
    u = jnp.exp(-jnp.abs(x))
    l1p = jnp.where(u < 1e-4, u * (1.0 - 0.5 * u), jnp.log(1.0 + u))
    return jnp.maximum(x, 0.0) + l1p


def _split3_dot(a, e):
    hi = a.astype(BF16)
    r1 = a - hi.astype(F32)
    mid = r1.astype(BF16)
    lo = (r1 - mid.astype(F32)).astype(BF16)
    return (jnp.dot(hi, e, preferred_element_type=F32) + jnp.dot(mid, e, preferred_element_type=F32)
            + jnp.dot(lo, e, preferred_element_type=F32))


def _cumsum_rows(a):
    rows = lax.broadcasted_iota(jnp.int32, a.shape, 0)
    s = 1
    while s < a.shape[0]:
        a = a + jnp.where(rows >= s, pltpu.roll(a, s, 0), 0.0)
        s *= 2
    return a


def _revcumsum_rows(a):
    n = a.shape[0]
    rows = lax.broadcasted_iota(jnp.int32, a.shape, 0)
    s = 1
    while s < n:
        a = a + jnp.where(rows < n - s, pltpu.roll(a, n - s, 0), 0.0)
        s *= 2
    return a


def _rms_groups(y, width):
    out = []
    for h in range(y.shape[1] // width):
        slab = y[:, h * width:(h + 1) * width]
        out.append((slab, _rstd(slab)))
    return out


def _ret_constants():
    idx = np.arange(CHUNK, dtype=np.float32)
    lg = np.log1p(-(np.float32(2.0) ** (np.float32(-5.0) - np.arange(RET_HEADS, dtype=np.float32)))).astype(np.float32)
    rel = np.abs(idx[:, None] - idx[None, :])
    dm = np.exp(lg[:, None, None] * rel).astype(np.float32)
    qd = np.exp(lg[None, :] * (idx[:, None] + 1.0)).astype(np.float32)
    kd = np.exp(lg[None, :] * (CHUNK - 1.0 - idx[:, None])).astype(np.float32)
    cd = np.exp(lg * CHUNK).astype(np.float32)
    qd = np.repeat(qd, RET_QK_DIM, axis=1)
    kd = np.repeat(kd, RET_QK_DIM, axis=1)
    cd = np.repeat(cd, RET_QK_DIM)[:, None] * np.ones((1, RET_V_DIM), np.float32)
    return dm, qd, kd, cd.astype(np.float32)


def _ssd_constants():
    eye = np.tile(np.eye(CHUNK, dtype=np.float32), (1, GROUP_W // CHUNK))
    blk = np.kron(np.eye(GROUP_W // CHUNK, dtype=np.float32), np.ones((CHUNK, CHUNK), np.float32))
    return eye, blk


def _head_expand():
    e = np.zeros((128, SSM_INNER), np.float32)
    for h in range(SSM_HEADS):
        e[h, h * 64:(h + 1) * 64] = 1.0
    return e


def _ret_chunk_fwd(qh, kh, vh, sh, dmh, qdh, kdh, cdh):
    a = _dot_nt(qh, kh) * dmh
    y = _dot(a, vh) + _dot(qh * qdh, sh)
    s_new = sh * cdh + _dot_tn(kh * kdh, vh)
    return y, s_new


def _ret_chunk_bwd(qh, kh, vh, sh, dmh, qdh, kdh, cdh, dy, ds_new):
    a = _dot_nt(qh, kh) * dmh
    dp = _dot_nt(dy, vh) * dmh
    dq = _dot(dp, kh) + _dot_nt(dy, sh) * qdh
    dk = _dot_tn(dp, qh) + _dot_nt(vh, ds_new) * kdh
    dv = _dot_tn(a, dy) + _dot(kh * kdh, ds_new)
    ds = cdh * ds_new + _dot_tn(qh * qdh, dy)
    return dq, dk, dv, ds


def _ssd_common(xs, dtx, ax, eye):
    cum = _cumsum_rows(dtx * ax)
    last = cum[CHUNK - 1:CHUNK, :]
    r = _colsum(jnp.where(eye > 0.5, cum, 0.0))
    return cum, last, r, xs * dtx


def _tile4(a):
    return jnp.concatenate([a, a, a, a], axis=0)


def _ssd_chunk_fwd(xs, dtx, b, c, ax, hg, eye, blk):
    cum, last, r, x = _ssd_common(xs, dtx, ax, eye)
    lam = jnp.exp(-jnp.abs(cum - r))
    wc = _dot_nt(c, _tile4(b)) * lam
    bd = _tile4(x) * blk
    y = _dot(wc, bd) + _dot(c, hg) * jnp.exp(cum)
    h_new = hg * jnp.exp(last) + _dot_tn(b, x * jnp.exp(last - cum))
    return y, h_new


def _ssd_chunk_bwd(xs, dtx, b, c, ax, hg, eye, blk, dy, dh_new):
    cum, last, r, x = _ssd_common(xs, dtx, ax, eye)
    delta = cum - r
    lam = jnp.exp(-jnp.abs(delta))
    b4 = _tile4(b)
    cb4 = _dot_nt(c, b4)
    wc = cb4 * lam
    bd = _tile4(x) * blk
    ecx = jnp.exp(cum)
    wl = jnp.exp(last - cum)
    ecl = jnp.exp(last)
    z = _dot(c, hg)
    dwc = _dot_nt(dy, bd)
    dbd = _dot_tn(wc, dy) * blk
    dx = dbd[0:64] + dbd[64:128] + dbd[128:192] + dbd[192:256]
    dt_ = _dot(b, dh_new)
    dx = dx + dt_ * wl
    dcb4 = dwc * lam
    dz = dy * ecx
    dc = _dot(dcb4, b4) + _dot_nt(dz, hg)
    db4 = _dot_tn(dcb4, c)
    db = db4[0:64] + db4[64:128] + db4[128:192] + db4[192:256] + _dot_nt(x * wl, dh_new)
    g = dwc * cb4 * lam * (-jnp.sign(delta))
    dr = -_colsum(g)
    dwl = dt_ * x * wl
    u = g + eye * dr + dy * z * ecx - dwl
    lastrow = _colsum(dwl) + _colsum(dh_new * hg) * ecl
    rows = lax.broadcasted_iota(jnp.int32, u.shape, 0)
    u = u + jnp.where(rows == CHUNK - 1, lastrow, 0.0)
    dh = _dot_tn(c, dz) + dh_new * ecl
    rc = _revcumsum_rows(u)
    dxs = dx * dtx
    g_dtx = dx * xs + rc * ax
    da = _colsum(rc * dtx)
    return dxs, g_dtx, db, dc, da, dh


def _row_tile(s, want):
    t = min(s, want)
    assert s % t == 0
    return t


def _nmm(x, gain, w, name, tn=1024, save_u=False):
    s, d = x.shape
    blocked = w.ndim == 3
    if blocked:
        tn = w.shape[2]
        n = w.shape[0] * tn
        w_spec = pl.BlockSpec((1, d, tn), lambda i, j: (j, 0, 0))
    else:
        n = w.shape[1]
        w_spec = pl.BlockSpec((d, tn), lambda i, j: (0, j))
    tm = _row_tile(s, 1024)
    assert n % tn == 0

    def body(x_ref, g_ref, w_ref, *rest):
        o_ref, u_sc = rest[0], rest[-1]

        @pl.when(pl.program_id(1) == 0)
        def _():
            xx = x_ref[...]
            u = _bf((xx * _rstd(xx)) * g_ref[...])
            u_sc[...] = u
            if save_u:
                rest[1][...] = u

        o_ref[...] = jnp.dot(u_sc[...], w_ref[0] if blocked else w_ref[...], preferred_element_type=F32)

    out_shape = [jax.ShapeDtypeStruct((s, n), F32)]
    out_specs = [pl.BlockSpec((tm, tn), lambda i, j: (i, j))]
    if save_u:
        out_shape.append(jax.ShapeDtypeStruct((s, d), BF16))
        out_specs.append(pl.BlockSpec((tm, d), lambda i, j: (i, 0)))
    res = _pcall(
        body, grid=(s // tm, n // tn),
        in_specs=[pl.BlockSpec((tm, d), lambda i, j: (i, 0)), pl.BlockSpec((1, d), lambda i, j: (0, 0)), w_spec],
        out_specs=out_specs, out_shape=out_shape, scratch_shapes=[pltpu.VMEM((tm, d), BF16)],
        compiler_params=_params("parallel", "arbitrary"), name=name)(x, gain, w)
    return res if save_u else res[0]


def _mm_tn(a, b, name, tm=1024, tn=1024, col_blocks=None):
    k, m = a.shape
    n = b.shape[1]
    tk = _row_tile(k, 1024)
    tm, tn = min(tm, m), min(tn, n)
    if col_blocks:
        tn = n // col_blocks
    assert m % tm == 0 and n % tn == 0
    nk = k // tk

    def body(a_ref, b_ref, o_ref, acc):
        kk = pl.program_id(2)

        @pl.when(kk == 0)
        def _():
            acc[...] = jnp.zeros_like(acc)

        acc[...] += _dot_tn(a_ref[...], b_ref[...])

        @pl.when(kk == nk - 1)
        def _():
            if col_blocks:
                o_ref[0] = acc[...]
            else:
                o_ref[...] = acc[...]

    if col_blocks:
        out_spec = pl.BlockSpec((1, tm, tn), lambda i, j, kk: (j, i, 0))
        out_shape = jax.ShapeDtypeStruct((col_blocks, m, tn), F32)
    else:
        out_spec = pl.BlockSpec((tm, tn), lambda i, j, kk: (i, j))
        out_shape = jax.ShapeDtypeStruct((m, n), F32)
    return _pcall(
        body, grid=(m // tm, n // tn, nk),
        in_specs=[pl.BlockSpec((tk, tm), lambda i, j, kk: (kk, i)), pl.BlockSpec((tk, tn), lambda i, j, kk: (kk, j))],
        out_specs=out_spec, out_shape=out_shape,
        scratch_shapes=[pltpu.VMEM((tm, tn), F32)],
        compiler_params=_params("parallel", "parallel", "arbitrary"), name=name)(a, b)


def _in_bwd(dproj, wcat, x, gain, dres, name):
    s, n = dproj.shape
    d = wcat.shape[0]
    tm = _row_tile(s, 1024)
    tk = 1024
    nk = n // tk
    ns = s // tm

    def body(dp_ref, w_ref, x_ref, g_ref, dr_ref, dx_ref, dg_ref, acc):
        i, kk = pl.program_id(0), pl.program_id(1)

        @pl.when(kk == 0)
        def _():
            acc[...] = jnp.zeros_like(acc)

        @pl.when((kk == 0) & (i == 0))
        def _():
            dg_ref[...] = jnp.zeros_like(dg_ref)

        acc[...] += _dot_nt(dp_ref[...], w_ref[...])

        @pl.when(kk == nk - 1)
        def _():
            xx = x_ref[...]
            r = _rstd(xx)
            du = acc[...]
            dg_ref[...] += _colsum(du * (xx * r))
            dx_ref[...] = dr_ref[...] + _rms_bwd(du * g_ref[...], xx, r)

    return _pcall(
        body, grid=(ns, nk),
        in_specs=[pl.BlockSpec((tm, tk), lambda i, kk: (i, kk)), pl.BlockSpec((d, tk), lambda i, kk: (0, kk)),
                  pl.BlockSpec((tm, d), lambda i, kk: (i, 0)), pl.BlockSpec((1, d), lambda i, kk: (0, 0)),
                  pl.BlockSpec((tm, d), lambda i, kk: (i, 0))],
        out_specs=[pl.BlockSpec((tm, d), lambda i, kk: (i, 0)), pl.BlockSpec((8, d), lambda i, kk: (0, 0))],
        out_shape=[jax.ShapeDtypeStruct((s, d), F32), jax.ShapeDtypeStruct((8, d), F32)],
        scratch_shapes=[pltpu.VMEM((tm, d), F32)],
        compiler_params=_params("arbitrary", "arbitrary"), name=name)(dproj, wcat, x, gain, dres)


def _prev_rows_spec(ts, width):
    return pl.BlockSpec((8, width), lambda i: (jnp.maximum(i * (ts // 8) - 1, 0), 0))


def _prescan(proj, cosf, sinf, cw, cb, dtb, eexp, name):
    s = proj.shape[0]
    ts = _row_tile(s, 256)

    def body(xbc_ref, prev_ref, q_ref, k_ref, dt_ref, cos_ref, sin_ref, cw_ref, cb_ref, dtb_ref, e_ref,
             qo_ref, ko_ref, xc_ref, dtx_ref):
        i = pl.program_id(0)
        for st in range(SSM_CONV_DIM // 128):
            sl = slice(st * 128, (st + 1) * 128)
            prev = jnp.where(i > 0, prev_ref[:, sl], 0.0)
            xcat = jnp.concatenate([prev, xbc_ref[:, sl]], axis=0)
            pre = cb_ref[:, sl] + cw_ref[3:4, sl] * xcat[8:8 + ts]
            for j in range(3):
                pre = pre + cw_ref[j:j + 1, sl] * pltpu.roll(xcat, 3 - j, 0)[8:8 + ts]
            xc_ref[:, sl] = pre * _sigmoid(pre)
        cs, sn = cos_ref[...], sin_ref[...]
        for h in range(RET_HEADS):
            sl = slice(h * 128, (h + 1) * 128)
            qh, kh = q_ref[:, sl], k_ref[:, sl]
            qo_ref[:, sl] = qh * cs + pltpu.roll(qh, 64, 1) * sn
            ko_ref[:, sl] = (kh * cs + pltpu.roll(kh, 64, 1) * sn) * (RET_QK_DIM ** -0.5)
        dtv = _softplus(dt_ref[:, 0:128] + dtb_ref[...])
        dtx_ref[...] = _split3_dot(dtv, e_ref[...])

    row = lambda w, c: pl.BlockSpec((ts, w), lambda i: (i, c))
    full = lambda a: pl.BlockSpec(a.shape, lambda i: (0,) * a.ndim)
    return _pcall(
        body, grid=(s // ts,),
        in_specs=[row(4096, 0), _prev_rows_spec(ts, 4096), row(512, C_Q // 512), row(512, C_K // 512),
                  row(DT_PAD, C_DT // DT_PAD), row(128, 0), row(128, 0), full(cw), full(cb), full(dtb), full(eexp)],
        out_specs=[row(512, 0), row(512, 0), row(4096, 0), row(2048, 0)],
        out_shape=[jax.ShapeDtypeStruct((s, 512), F32), jax.ShapeDtypeStruct((s, 512), F32),
                   jax.ShapeDtypeStruct((s, 4096), F32), jax.ShapeDtypeStruct((s, 2048), F32)],
        compiler_params=_params("parallel"), name=name)(proj, proj, proj, proj, proj, cosf, sinf, cw, cb, dtb, eexp)


def _scan_fwd(qr, kr, proj, xc, dtx, ax, consts, name):
    s = qr.shape[0]
    nc = s // CHUNK
    dm, qd, kd, cd, eye, blk = consts

    def body(q_ref, k_ref, v_ref, xc_ref, dtx_ref, ax_ref, dm_ref, qd_ref, kd_ref, cd_ref, eye_ref, blk_ref,
             yr_ref, ys_ref, sst_ref, hst_ref, s_sc, h_sc):
        @pl.when(pl.program_id(0) == 0)
        def _():
            s_sc[...] = jnp.zeros_like(s_sc)
            h_sc[...] = jnp.zeros_like(h_sc)

        sst_ref[0] = s_sc[...]
        hst_ref[0] = h_sc[...]
        for h in range(RET_HEADS):
            ql, vl = slice(h * 128, (h + 1) * 128), slice(h * 256, (h + 1) * 256)
            y, s_new = _ret_chunk_fwd(q_ref[:, ql], k_ref[:, ql], v_ref[:, vl], s_sc[ql, :], dm_ref[h],
                                      qd_ref[:, ql], kd_ref[:, ql], cd_ref[ql, :])
            yr_ref[:, vl] = y
            s_sc[ql, :] = s_new
        eye_v, blk_v = eye_ref[...], blk_ref[...]
        for g in range(SSM_GROUPS):
            sl = slice(g * GROUP_W, (g + 1) * GROUP_W)
            bl = slice(SSM_INNER + g * 128, SSM_INNER + (g + 1) * 128)
            cl = slice(SSM_INNER + SSM_BC + g * 128, SSM_INNER + SSM_BC + (g + 1) * 128)
            y, h_new = _ssd_chunk_fwd(xc_ref[:, sl], dtx_ref[:, sl], xc_ref[:, bl], xc_ref[:, cl], ax_ref[:, sl],
                                      h_sc[:, sl], eye_v, blk_v)
            ys_ref[:, sl] = y
            h_sc[:, sl] = h_new

    row = lambda w, c=0: pl.BlockSpec((CHUNK, w), lambda i: (i, c))
    full = lambda a: pl.BlockSpec(a.shape, lambda i: (0,) * a.ndim)
    return _pcall(
        body, grid=(nc,),
        in_specs=[row(512), row(512), row(1024, C_V // 1024), row(4096), row(2048), full(ax), full(dm), full(qd),
                  full(kd), full(cd), full(eye), full(blk)],
        out_specs=[row(1024), row(2048), pl.BlockSpec((1, 512, 256), lambda i: (i, 0, 0)),
                   pl.BlockSpec((1, 128, 2048), lambda i: (i, 0, 0))],
        out_shape=[jax.ShapeDtypeStruct((s, 1024), F32), jax.ShapeDtypeStruct((s, 2048), F32),
                   jax.ShapeDtypeStruct((nc, 512, 256), F32), jax.ShapeDtypeStruct((nc, 128, 2048), F32)],
        scratch_shapes=[pltpu.VMEM((512, 256), F32), pltpu.VMEM((128, 2048), F32)],
        compiler_params=_params("arbitrary"), name=name)(qr, kr, proj, xc, dtx, ax, dm, qd, kd, cd, eye, blk)


def _scan_bwd(qr, kr, proj, xc, dtx, ax, consts, sst, hst, dyr, dys, dproj, name):
    s = qr.shape[0]
    nc = s // CHUNK
    dm, qd, kd, cd, eye, blk = consts

    def body(q_ref, k_ref, v_ref, xc_ref, dtx_ref, ax_ref, dm_ref, qd_ref, kd_ref, cd_ref, eye_ref, blk_ref,
             sst_ref, hst_ref, dyr_ref, dys_ref, dproj_in, dq_ref, dk_ref, dv_ref, dxc_ref, gdt_ref, da_ref, ds_sc,
             dh_sc):
        @pl.when(pl.program_id(0) == 0)
        def _():
            ds_sc[...] = jnp.zeros_like(ds_sc)
            dh_sc[...] = jnp.zeros_like(dh_sc)
            da_ref[...] = jnp.zeros_like(da_ref)

        for h in range(RET_HEADS):
            ql, vl = slice(h * 128, (h + 1) * 128), slice(h * 256, (h + 1) * 256)
            dq, dk, dv, ds = _ret_chunk_bwd(q_ref[:, ql], k_ref[:, ql], v_ref[:, vl], sst_ref[0, ql, :], dm_ref[h],
                                            qd_ref[:, ql], kd_ref[:, ql], cd_ref[ql, :], dyr_ref[:, vl], ds_sc[ql, :])
            dq_ref[:, ql] = dq
            dk_ref[:, ql] = dk
            dv_ref[:, vl] = _bf(dv)
            ds_sc[ql, :] = ds
        eye_v, blk_v = eye_ref[...], blk_ref[...]
        for g in range(SSM_GROUPS):
            sl = slice(g * GROUP_W, (g + 1) * GROUP_W)
            bl = slice(SSM_INNER + g * 128, SSM_INNER + (g + 1) * 128)
            cl = slice(SSM_INNER + SSM_BC + g * 128, SSM_INNER + SSM_BC + (g + 1) * 128)
            dxs, g_dtx, db, dc, da, dh = _ssd_chunk_bwd(
                xc_ref[:, sl], dtx_ref[:, sl], xc_ref[:, bl], xc_ref[:, cl], ax_ref[:, sl], hst_ref[0, :, sl],
                eye_v, blk_v, dys_ref[:, sl], dh_sc[:, sl])
            dxc_ref[:, sl] = dxs
            dxc_ref[:, bl] = db
            dxc_ref[:, cl] = dc
            gdt_ref[:, sl] = g_dtx
            da_ref[:, sl] += da
            dh_sc[:, sl] = dh

    row = lambda w, c=0: pl.BlockSpec((CHUNK, w), lambda i: (nc - 1 - i, c))
    full = lambda a: pl.BlockSpec(a.shape, lambda i: (0,) * a.ndim)
    return _pcall(
        body, grid=(nc,),
        in_specs=[row(512), row(512), row(1024, C_V // 1024), row(4096), row(2048), full(ax), full(dm), full(qd),
                  full(kd), full(cd), full(eye), full(blk),
                  pl.BlockSpec((1, 512, 256), lambda i: (nc - 1 - i, 0, 0)),
                  pl.BlockSpec((1, 128, 2048), lambda i: (nc - 1 - i, 0, 0)), row(1024), row(2048), ANY],
        out_specs=[row(512), row(512), row(1024, C_V // 1024), row(4096), row(2048),
                   pl.BlockSpec((1, 2048), lambda i: (0, 0))],
        out_shape=[jax.ShapeDtypeStruct((s, 512), F32), jax.ShapeDtypeStruct((s, 512), F32),
                   jax.ShapeDtypeStruct(dproj.shape, BF16), jax.ShapeDtypeStruct((s, 4096), F32),
                   jax.ShapeDtypeStruct((s, 2048), F32), jax.ShapeDtypeStruct((1, 2048), F32)],
        scratch_shapes=[pltpu.VMEM((512, 256), F32), pltpu.VMEM((128, 2048), F32)],
        input_output_aliases={16: 2},
        compiler_params=_params("arbitrary"), name=name)(qr, kr, proj, xc, dtx, ax, dm, qd, kd, cd, eye, blk, sst, hst,
                                                          dyr, dys, dproj)


def _mix_values(yr, g, ys, xs, z, gates, bg, dsk, sn):
    sg, dsg = _silu_and_grad(g)
    ret = _rms_groups(yr, RET_V_DIM)
    yrn = jnp.concatenate([slab * r for slab, r in ret], axis=1) * sg
    sz, dsz = _silu_and_grad(z)
    ys0 = ys + xs * dsk
    ys1 = ys0 * sz
    grp = _rms_groups(ys1, GROUP_W)
    ysh = jnp.concatenate([slab * r for slab, r in grp], axis=1)
    ysn = ysh * sn
    gg = _sigmoid(gates + bg)
    return dict(sg=sg, dsg=dsg, ret=ret, yrn=yrn, sz=sz, dsz=dsz, ys0=ys0, ys1=ys1, grp=grp, ysh=ysh, ysn=ysn,
                gr=gg[:, :D_MODEL], gs=gg[:, D_MODEL:])


def _postscan_fwd(x, yr, ys, xc, proj, bg, dsk, sn, wr, ws, wo, name):
    s = x.shape[0]
    ts = _row_tile(s, 256)

    def body(x_ref, yr_ref, ys_ref, xs_ref, g_ref, z_ref, gt_ref, bg_ref, dsk_ref, sn_ref, wr_ref, ws_ref, wo_ref,
             o_ref):
        m = _mix_values(yr_ref[...], g_ref[...], ys_ref[...], xs_ref[...], z_ref[...], gt_ref[...], bg_ref[...],
                        dsk_ref[...], sn_ref[...])
        merged = m["gr"] * _dot(m["yrn"], wr_ref[...]) + m["gs"] * _dot(m["ysn"], ws_ref[...])
        o_ref[...] = x_ref[...] + _dot(merged, wo_ref[...])

    row = lambda w, c=0: pl.BlockSpec((ts, w), lambda i: (i, c))
    full = lambda a: pl.BlockSpec(a.shape, lambda i: (0,) * a.ndim)
    return _pcall(
        body, grid=(s // ts,),
        in_specs=[row(1024), row(1024), row(2048), row(2048), row(1024, C_G // 1024), row(2048, C_Z // 2048),
                  row(2048, C_GATES // 2048), full(bg), full(dsk), full(sn), full(wr), full(ws), full(wo)],
        out_specs=row(1024), out_shape=jax.ShapeDtypeStruct((s, D_MODEL), F32),
        compiler_params=_params("parallel"), name=name)(x, yr, ys, xc, proj, proj, proj, bg, dsk, sn, wr, ws, wo)


def _postscan_bwd(dout, yr, ys, xc, proj, bg, dsk, sn, wr, ws, wo, name):
    s = dout.shape[0]
    ts = _row_tile(s, 128)

    def body(do_ref, yr_ref, ys_ref, xs_ref, g_ref, z_ref, gt_ref, bg_ref, dsk_ref, sn_ref, wr_ref, ws_ref, wo_ref,
             dyr_ref, dys_ref, dxs_ref, dproj_ref, yrn_ref, ysn_ref, mg_ref, dbr_ref, dbs_ref,
             dbg_ref, ddsk_ref, dsn_ref):
        @pl.when(pl.program_id(0) == 0)
        def _():
            dbg_ref[...] = jnp.zeros_like(dbg_ref)
            ddsk_ref[...] = jnp.zeros_like(ddsk_ref)
            dsn_ref[...] = jnp.zeros_like(dsn_ref)

        xs = xs_ref[...]
        m = _mix_values(yr_ref[...], g_ref[...], ys_ref[...], xs, z_ref[...], gt_ref[...], bg_ref[...],
                        dsk_ref[...], sn_ref[...])
        gr, gs = m["gr"], m["gs"]
        br, bs = _dot(m["yrn"], wr_ref[...]), _dot(m["ysn"], ws_ref[...])
        dmerged = _dot_nt(do_ref[...], wo_ref[...])
        dgt = jnp.concatenate([dmerged * br * gr * (1.0 - gr), dmerged * bs * gs * (1.0 - gs)], axis=1)
        dproj_ref[:, C_GATES:C_GATES + 2048] = _bf(dgt)
        dbg_ref[...] += _colsum(dgt)
        dbr, dbs = dmerged * gr, dmerged * gs
        yrn_ref[...] = _bf(m["yrn"])
        ysn_ref[...] = _bf(m["ysn"])
        mg_ref[...] = _bf(gr * br + gs * bs)
        dbr_ref[...] = _bf(dbr)
        dbs_ref[...] = _bf(dbs)
        dyrn = _dot_nt(dbr, wr_ref[...])
        dysn = _dot_nt(dbs, ws_ref[...])
        rn = jnp.concatenate([slab * r for slab, r in m["ret"]], axis=1)
        dproj_ref[:, C_G:C_G + 1024] = _bf(dyrn * rn * m["dsg"])
        drn = dyrn * m["sg"]
        dyr_ref[...] = jnp.concatenate(
            [_rms_bwd(drn[:, h * RET_V_DIM:(h + 1) * RET_V_DIM], slab, r) for h, (slab, r) in enumerate(m["ret"])], axis=1)
        dsn_ref[...] += _colsum(dysn * m["ysh"])
        dysh = dysn * sn_ref[...]
        dys1 = jnp.concatenate(
            [_rms_bwd(dysh[:, h * GROUP_W:(h + 1) * GROUP_W], slab, r) for h, (slab, r) in enumerate(m["grp"])], axis=1)
        dproj_ref[:, C_Z:C_Z + 2048] = _bf(dys1 * m["ys0"] * m["dsz"])
        dys0 = dys1 * m["sz"]
        dys_ref[...] = dys0
        dxs_ref[...] = dys0 * dsk_ref[...]
        ddsk_ref[...] += _colsum(dys0 * xs)

    row = lambda w, c=0: pl.BlockSpec((ts, w), lambda i: (i, c))
    full = lambda a: pl.BlockSpec(a.shape, lambda i: (0,) * a.ndim)
    acc = lambda w: pl.BlockSpec((8, w), lambda i: (0, 0))
    sds = jax.ShapeDtypeStruct
    return _pcall(
        body, grid=(s // ts,),
        in_specs=[row(1024), row(1024), row(2048), row(2048), row(1024, C_G // 1024), row(2048, C_Z // 2048),
                  row(2048, C_GATES // 2048), full(bg), full(dsk), full(sn), full(wr), full(ws), full(wo)],
        out_specs=[row(1024), row(2048), row(2048), row(NP), row(1024), row(2048), row(1024),
                   row(1024), row(1024), acc(2048), acc(2048), acc(2048)],
        out_shape=[sds((s, 1024), F32), sds((s, 2048), F32), sds((s, 2048), F32), sds((s, NP), BF16),
                   sds((s, 1024), BF16), sds((s, 2048), BF16),
                   sds((s, 1024), BF16), sds((s, 1024), BF16), sds((s, 1024), BF16), sds((8, 2048), F32),
                   sds((8, 2048), F32), sds((8, 2048), F32)],
        compiler_params=_params("arbitrary"), name=name)(dout, yr, ys, xc, proj, proj, proj, bg, dsk, sn, wr, ws, wo)


def _prescan_bwd(proj, dxc, dxs_skip, gdtx, dqr, dkr, cosf, sinf, cw, cb, dtb, eexp_t, dproj, name):
    s = proj.shape[0]
    ts = _row_tile(s, 256)
    nt = s // ts
    m = ts + 8
    width = C_DT + DT_PAD

    def body(xbc_ref, prev_ref, nxt_ref, dt_ref, dxc_ref, dxcn_ref, dsk_ref, dskn_ref, gdt_ref, dq_ref, dk_ref,
             cos_ref, sin_ref, cw_ref, cb_ref, dtb_ref, et_ref, dproj_in, dp_ref, dcw_ref, dcb_ref, ddtb_ref):
        i = pl.program_id(0)

        @pl.when(i == 0)
        def _():
            ddtb_ref[...] = jnp.zeros_like(ddtb_ref)
            dcw_ref[...] = jnp.zeros_like(dcw_ref)
            dcb_ref[...] = jnp.zeros_like(dcb_ref)

        rows = lax.broadcasted_iota(jnp.int32, (m, 128), 0)
        live = (rows < ts) | (i < nt - 1)
        for st in range(SSM_CONV_DIM // 128):
            sl = slice(st * 128, (st + 1) * 128)
            prev = jnp.where(i > 0, prev_ref[:, sl], 0.0)
            xcat = jnp.concatenate([prev, xbc_ref[:, sl], nxt_ref[:, sl]], axis=0)
            shifted = [pltpu.roll(xcat, 3 - j, 0) for j in range(3)] + [xcat]
            pre = cb_ref[:, sl]
            for j in range(SSM_CONV):
                pre = pre + cw_ref[j:j + 1, sl] * shifted[j][8:]
            _, dsilu = _silu_and_grad(pre)
            dxc = jnp.concatenate([dxc_ref[:, sl], dxcn_ref[:, sl]], axis=0)
            if st * 128 < SSM_INNER:
                dxc = dxc + jnp.concatenate([dsk_ref[:, sl], dskn_ref[:, sl]], axis=0)
            dpre = jnp.where(live, dxc * dsilu, 0.0)
            dpt = dpre[0:ts]
            dx = cw_ref[3:4, sl] * dpt
            for j in range(3):
                dx = dx + cw_ref[j:j + 1, sl] * pltpu.roll(dpre, m - (3 - j), 0)[0:ts]
            for j in range(SSM_CONV):
                dcw_ref[8 * j:8 * j + 8, sl] += _colsum(dpt * shifted[j][8:8 + ts])
            dcb_ref[:, sl] += _colsum(dpt)
            dp_ref[:, sl] = _bf(dx)
        cs, sn = cos_ref[...], sin_ref[...]
        for h in range(RET_HEADS):
            sl = slice(h * 128, (h + 1) * 128)
            dq = dq_ref[:, sl]
            dk = dk_ref[:, sl] * (RET_QK_DIM ** -0.5)
            dp_ref[:, C_Q + h * 128:C_Q + (h + 1) * 128] = _bf(dq * cs + pltpu.roll(dq * sn, 64, 1))
            dp_ref[:, C_K + h * 128:C_K + (h + 1) * 128] = _bf(dk * cs + pltpu.roll(dk * sn, 64, 1))
        ddt = _split3_dot(gdt_ref[...], et_ref[...])
        ddt = ddt * _sigmoid(dt_ref[:, 0:128] + dtb_ref[...])
        ddtb_ref[...] += _colsum(ddt)
        dp_ref[:, C_DT:C_DT + 128] = _bf(ddt)
        dp_ref[:, C_DT + 128:C_DT + DT_PAD] = jnp.zeros((ts, DT_PAD - 128), BF16)

    row = lambda w, c=0: pl.BlockSpec((ts, w), lambda i: (i, c))
    nxt = lambda w: pl.BlockSpec((8, w), lambda i: (jnp.minimum((i + 1) * (ts // 8), s // 8 - 1), 0))
    full = lambda a: pl.BlockSpec(a.shape, lambda i: (0,) * a.ndim)
    sds = jax.ShapeDtypeStruct
    return _pcall(
        body, grid=(nt,),
        in_specs=[row(4096), _prev_rows_spec(ts, 4096), nxt(4096), row(DT_PAD, C_DT // DT_PAD), row(4096), nxt(4096),
                  row(2048), nxt(2048), row(2048), row(512), row(512), row(128), row(128), full(cw), full(cb),
                  full(dtb), full(eexp_t), ANY],
        out_specs=[row(width), pl.BlockSpec((32, 4096), lambda i: (0, 0)), pl.BlockSpec((8, 4096), lambda i: (0, 0)),
                   pl.BlockSpec((8, 128), lambda i: (0, 0))],
        out_shape=[sds(dproj.shape, BF16), sds((32, 4096), F32), sds((8, 4096), F32), sds((8, 128), F32)],
        input_output_aliases={17: 0},
        compiler_params=_params("arbitrary"), name=name)(proj, proj, proj, proj, dxc, dxc, dxs_skip, dxs_skip, gdtx,
                                                          dqr, dkr, cosf, sinf, cw, cb, dtb, eexp_t, dproj)


def _xattn_values(x, gain, wq, kv):
    r = _rstd(x)
    h = (x * r) * gain
    q = _dot(h, wq)
    ps, os_ = [], []
    for hd in range(XA_HEADS):
        sl = slice(hd * XA_HEAD_DIM, (hd + 1) * XA_HEAD_DIM)
        sc = _dot_nt(q[:, sl], kv[:, sl]) * (XA_HEAD_DIM ** -0.5)
        e = jnp.exp(sc - jnp.max(sc, axis=-1, keepdims=True))
        p = e / jnp.sum(e, axis=-1, keepdims=True)
        ps.append(p)
        os_.append(_dot(p, kv[:, D_MODEL + hd * XA_HEAD_DIM:D_MODEL + (hd + 1) * XA_HEAD_DIM]))
    return r, h, q, ps, jnp.concatenate(os_, axis=1)


def _xattn_fwd(x, gain, wq, kv, wo, name):
    s = x.shape[0]
    ts = _row_tile(s, 256)

    def body(x_ref, g_ref, wq_ref, kv_ref, wo_ref, o_ref):
        x_ = x_ref[...]
        _, _, _, _, o = _xattn_values(x_, g_ref[...], wq_ref[...], kv_ref[...])
        o_ref[...] = x_ + _dot(o, wo_ref[...])

    row = pl.BlockSpec((ts, D_MODEL), lambda i: (i, 0))
    full = lambda a: pl.BlockSpec(a.shape, lambda i: (0,) * a.ndim)
    return _pcall(
        body, grid=(s // ts,), in_specs=[row, full(gain), full(wq), full(kv), full(wo)], out_specs=row,
        out_shape=jax.ShapeDtypeStruct((s, D_MODEL), F32), compiler_params=_params("parallel"), name=name)(
            x, gain, wq, kv, wo)


def _xattn_bwd(x, dout, gain, wq, kv, wo, name):
    s = x.shape[0]
    m = kv.shape[0]
    ts = _row_tile(s, 256)

    def body(x_ref, do_ref, g_ref, wq_ref, kv_ref, wo_ref, dx_ref, h_ref, dq_ref, o_ref, dkv_ref, dg_ref):
        @pl.when(pl.program_id(0) == 0)
        def _():
            dkv_ref[...] = jnp.zeros_like(dkv_ref)
            dg_ref[...] = jnp.zeros_like(dg_ref)

        x_, do, kvv = x_ref[...], do_ref[...], kv_ref[...]
        r, h, q, ps, o = _xattn_values(x_, g_ref[...], wq_ref[...], kvv)
        dov = _dot_nt(do, wo_ref[...])
        dqs = []
        for hd in range(XA_HEADS):
            sl = slice(hd * XA_HEAD_DIM, (hd + 1) * XA_HEAD_DIM)
            vl = slice(D_MODEL + hd * XA_HEAD_DIM, D_MODEL + (hd + 1) * XA_HEAD_DIM)
            p, doh = ps[hd], dov[:, sl]
            dp = _dot_nt(doh, kvv[:, vl])
            dsc = p * (dp - jnp.sum(dp * p, axis=-1, keepdims=True)) * (XA_HEAD_DIM ** -0.5)
            dqs.append(_dot(dsc, kvv[:, sl]))
            dkv_ref[:, sl] += _dot_tn(dsc, q[:, sl])
            dkv_ref[:, vl] += _dot_tn(p, doh)
        dq = jnp.concatenate(dqs, axis=1)
        dh = _dot_nt(dq, wq_ref[...])
        dg_ref[...] += _colsum(dh * (x_ * r))
        dx_ref[...] = do + _rms_bwd(dh * g_ref[...], x_, r)
        h_ref[...] = _bf(h)
        dq_ref[...] = _bf(dq)
        o_ref[...] = _bf(o)

    row = pl.BlockSpec((ts, D_MODEL), lambda i: (i, 0))
    full = lambda a: pl.BlockSpec(a.shape, lambda i: (0,) * a.ndim)
    sds = jax.ShapeDtypeStruct
    return _pcall(
        body, grid=(s // ts,), in_specs=[row, row, full(gain), full(wq), full(kv), full(wo)],
        out_specs=[row, row, row, row, pl.BlockSpec((m, 2 * D_MODEL), lambda i: (0, 0)),
                   pl.BlockSpec((8, D_MODEL), lambda i: (0, 0))],
        out_shape=[sds((s, D_MODEL), F32), sds((s, D_MODEL), BF16), sds((s, D_MODEL), BF16), sds((s, D_MODEL), BF16),
                   sds((m, 2 * D_MODEL), F32), sds((8, D_MODEL), F32)],
        compiler_params=_params("arbitrary"), name=name)(x, dout, gain, wq, kv, wo)


def _mem_bwd(mem, gain, dkv, wkv, name):
    m = mem.shape[0]

    def body(mem_ref, g_ref, dkv_ref, w_ref, mn_ref, dg_ref):
        mm = mem_ref[...]
        r = _rstd(mm)
        xh = mm * r
        mn_ref[...] = _bf(xh * g_ref[...])
        nb, _, wb = w_ref.shape
        dmn = _dot_nt(dkv_ref[:, 0:wb], w_ref[0])
        for j in range(1, nb):
            dmn = dmn + _dot_nt(dkv_ref[:, j * wb:(j + 1) * wb], w_ref[j])
        dg_ref[...] = jnp.zeros_like(dg_ref) + _colsum(dmn * xh)

    full = lambda a: pl.BlockSpec(a.shape, lambda: (0,) * a.ndim)
    return _pcall(
        body, in_specs=[full(mem), full(gain), full(dkv), full(wkv)],
        out_specs=[pl.BlockSpec((m, D_MODEL), lambda: (0, 0)), pl.BlockSpec((8, D_MODEL), lambda: (0, 0))],
        out_shape=[jax.ShapeDtypeStruct((m, D_MODEL), BF16), jax.ShapeDtypeStruct((8, D_MODEL), F32)],
        compiler_params=pltpu.CompilerParams(vmem_limit_bytes=VMEM_LIMIT), name=name)(mem, gain, dkv, wkv)


def _mlp_fwd(x, gain, w1, w2, name):
    s = x.shape[0]
    ts = _row_tile(s, 512)
    tf = 1024
    nf = D_FF // tf

    def body(x_ref, g_ref, w1_ref, w2_ref, o_ref, h_sc, acc):
        j = pl.program_id(1)

        @pl.when(j == 0)
        def _():
            xx = x_ref[...]
            h_sc[...] = _bf((xx * _rstd(xx)) * g_ref[...])
            acc[...] = jnp.zeros_like(acc)

        a = jnp.dot(h_sc[...], w1_ref[0], preferred_element_type=F32)
        r = jnp.square(jnp.maximum(a, 0.0))
        acc[...] += _dot(r, w2_ref[...])

        @pl.when(j == nf - 1)
        def _():
            o_ref[...] = x_ref[...] + acc[...]

    row = pl.BlockSpec((ts, D_MODEL), lambda i, j: (i, 0))
    return _pcall(
        body, grid=(s // ts, nf),
        in_specs=[row, pl.BlockSpec((1, D_MODEL), lambda i, j: (0, 0)),
                  pl.BlockSpec((1, D_MODEL, tf), lambda i, j: (j, 0, 0)), pl.BlockSpec((tf, D_MODEL), lambda i, j: (j, 0))],
        out_specs=row, out_shape=jax.ShapeDtypeStruct((s, D_MODEL), F32),
        scratch_shapes=[pltpu.VMEM((ts, D_MODEL), BF16), pltpu.VMEM((ts, D_MODEL), F32)],
        compiler_params=_params("parallel", "arbitrary"), name=name)(x, gain, w1, w2)


def _mlp_bwd(x, dout, gain, w1, w2, name):
    s = x.shape[0]
    ts = _row_tile(s, 512)
    tf = 1024
    nf = D_FF // tf

    def body(x_ref, do_ref, g_ref, w1_ref, w2_ref, dx_ref, h_ref, r_ref, da_ref, dg_ref, h_sc, do_sc, acc):
        i, j = pl.program_id(0), pl.program_id(1)

        @pl.when(j == 0)
        def _():
            xx = x_ref[...]
            h_sc[...] = _bf((xx * _rstd(xx)) * g_ref[...])
            do_sc[...] = _bf(do_ref[...])
            acc[...] = jnp.zeros_like(acc)
            h_ref[...] = h_sc[...]

        @pl.when((j == 0) & (i == 0))
        def _():
            dg_ref[...] = jnp.zeros_like(dg_ref)

        a = jnp.dot(h_sc[...], w1_ref[0], preferred_element_type=F32)
        ra = jnp.maximum(a, 0.0)
        r_ref[...] = _bf(ra * ra)
        dr = lax.dot_general(do_sc[...], w2_ref[...], (((1,), (1,)), ((), ())), preferred_element_type=F32)
        da = _bf(dr * 2.0 * ra)
        da_ref[...] = da
        acc[...] += lax.dot_general(da, w1_ref[0], (((1,), (1,)), ((), ())), preferred_element_type=F32)

        @pl.when(j == nf - 1)
        def _():
            xx = x_ref[...]
            r = _rstd(xx)
            dh = acc[...]
            dg_ref[...] += _colsum(dh * (xx * r))
            dx_ref[...] = do_ref[...] + _rms_bwd(dh * g_ref[...], xx, r)

    row = pl.BlockSpec((ts, D_MODEL), lambda i, j: (i, 0))
    ff = pl.BlockSpec((ts, tf), lambda i, j: (i, j))
    sds = jax.ShapeDtypeStruct
    return _pcall(
        body, grid=(s // ts, nf),
        in_specs=[row, row, pl.BlockSpec((1, D_MODEL), lambda i, j: (0, 0)),
                  pl.BlockSpec((1, D_MODEL, tf), lambda i, j: (j, 0, 0)), pl.BlockSpec((tf, D_MODEL), lambda i, j: (j, 0))],
        out_specs=[row, row, ff, ff, pl.BlockSpec((8, D_MODEL), lambda i, j: (0, 0))],
        out_shape=[sds((s, D_MODEL), F32), sds((s, D_MODEL), BF16), sds((s, D_FF), BF16), sds((s, D_FF), BF16),
                   sds((8, D_MODEL), F32)],
        scratch_shapes=[pltpu.VMEM((ts, D_MODEL), BF16), pltpu.VMEM((ts, D_MODEL), BF16), pltpu.VMEM((ts, D_MODEL), F32)],
        compiler_params=_params("arbitrary", "arbitrary"), name=name)(x, dout, gain, w1, w2)


def _final(x, gain, tgt, name):
    s = x.shape[0]
    ts = _row_tile(s, 512)

    def body(x_ref, g_ref, t_ref, dx_ref, loss_ref, dg_ref):
        @pl.when(pl.program_id(0) == 0)
        def _():
            loss_ref[...] = jnp.zeros_like(loss_ref)
            dg_ref[...] = jnp.zeros_like(dg_ref)

        xx = x_ref[...]
        r = _rstd(xx)
        xh = xx * r
        err = xh * g_ref[...] - t_ref[...]
        loss_ref[...] += 0.5 * jnp.sum(jnp.sum(err * err, axis=1, keepdims=True), axis=0, keepdims=True) / D_MODEL
        dy = err * (1.0 / D_MODEL)
        dg_ref[...] += _colsum(dy * xh)
        dx_ref[...] = _rms_bwd(dy * g_ref[...], xx, r)

    row = pl.BlockSpec((ts, D_MODEL), lambda i: (i, 0))
    return _pcall(
        body, grid=(s // ts,), in_specs=[row, pl.BlockSpec((1, D_MODEL), lambda i: (0, 0)), row],
        out_specs=[row, pl.BlockSpec((8, 128), lambda i: (0, 0)), pl.BlockSpec((8, D_MODEL), lambda i: (0, 0))],
        out_shape=[jax.ShapeDtypeStruct((s, D_MODEL), F32), jax.ShapeDtypeStruct((8, 128), F32),
                   jax.ShapeDtypeStruct((8, D_MODEL), F32)],
        compiler_params=_params("arbitrary"), name=name)(x, gain, tgt)


def _as3d(a):
    return a.reshape((-1,) + a.shape[-2:])


def _sum_cast(terms, out_dtype, name, row_want=256):
    shape = terms[0].shape
    t3 = [_as3d(t) for t in terms]
    b, r, c = t3[0].shape
    tr = _row_tile(r, row_want)

    def body(*refs):
        acc = refs[0][...].astype(F32)
        for t in refs[1:-1]:
            acc = acc + t[...].astype(F32)
        refs[-1][...] = acc.astype(out_dtype)

    spec = pl.BlockSpec((1, tr, c), lambda i, j: (i, j, 0))
    out = _pcall(body, grid=(b, r // tr), in_specs=[spec] * len(t3), out_specs=spec,
                 out_shape=jax.ShapeDtypeStruct((b, r, c), out_dtype), compiler_params=_params("parallel", "parallel"),
                 name=name)(*t3)
    return out.reshape(shape)


def _pair_sum(a, b, sel, half_id, out_dtype, name):
    _, h, c = b.shape
    k = sel.shape[0]
    tr = _row_tile(h, 256)
    nt = h // tr

    def body(sel_ref, hid_ref, a_ref, b_ref, o_ref):
        o_ref[...] = (a_ref[...] + b_ref[...]).astype(out_dtype)

    blkshape = (1, tr, c)
    grid_spec = pltpu.PrefetchScalarGridSpec(
        num_scalar_prefetch=2, grid=(k, nt),
        in_specs=[pl.BlockSpec(blkshape, lambda q, j, sel_ref, hid_ref: (sel_ref[q], hid_ref[0] * nt + j, 0)),
                  pl.BlockSpec(blkshape, lambda q, j, sel_ref, hid_ref: (sel_ref[q], j, 0))],
        out_specs=pl.BlockSpec(blkshape, lambda q, j, sel_ref, hid_ref: (q, j, 0)))
    return _pcall(body, grid_spec=grid_spec, out_shape=jax.ShapeDtypeStruct((k, h, c), out_dtype),
                  compiler_params=_params("parallel", "parallel"), name=name)(sel, half_id, a, b)


def _adamw(w, g, m, v, name):
    shape = w.shape
    w3, g3, m3, v3 = _as3d(w), _as3d(g), _as3d(m), _as3d(v)
    b, r, c = w3.shape
    tr = _row_tile(r, 256)

    def body(w_ref, g_ref, m_ref, v_ref, d_ref, mo_ref, vo_ref):
        gg = g_ref[...]
        mn = ADAM_B1 * m_ref[...] + (1.0 - ADAM_B1) * gg
        vn = ADAM_B2 * v_ref[...] + (1.0 - ADAM_B2) * jnp.square(gg)
        m_hat = mn / (1.0 - ADAM_B1 ** ADAM_STEP)
        v_hat = vn / (1.0 - ADAM_B2 ** ADAM_STEP)
        d_ref[...] = -ADAM_LR * (m_hat / (jnp.sqrt(v_hat) + ADAM_EPS) + ADAM_WD * w_ref[...])
        mo_ref[...] = mn
        vo_ref[...] = vn

    spec = pl.BlockSpec((1, tr, c), lambda i, j: (i, j, 0))
    sd = jax.ShapeDtypeStruct((b, r, c), F32)
    d, mo, vo = _pcall(body, grid=(b, r // tr), in_specs=[spec] * 4, out_specs=[spec] * 3, out_shape=[sd] * 3,
                       compiler_params=_params("parallel", "parallel"), name=name)(w3, g3, m3, v3)
    return d.reshape(shape), mo.reshape(shape), vo.reshape(shape)


ANY = pl.BlockSpec(memory_space=pl.ANY)


def _place():
    return lax.axis_index("x"), lax.axis_index("y"), lax.axis_index("c")


def _flip(x, y, r):
    return (1 - x if r & 2 else x), (1 - y if r & 1 else y)


def _dma_sems(*counts):
    return [pltpu.SemaphoreType.DMA((k,)) for k in counts]


def _gather_ici(shards):
    n = len(shards)
    hs = [a.shape[0] // 2 for a in shards]

    def copies(ins, outs, sems, incoming):
        send, recv = sems
        x, y, c = _place()
        out = []
        for r in (1, 2, 3):
            cx, cy = _flip(x, y, r)
            for a in range(n):
                rows = pl.ds(c * hs[a], hs[a])
                block = 2 * cx + cy if incoming else 2 * x + y
                out.append(pltpu.make_async_remote_copy(
                    src_ref=ins[a].at[rows], dst_ref=outs[a].at[block, rows], send_sem=send.at[(r - 1) * n + a],
                    recv_sem=recv.at[(r - 1) * n + a], device_id=(cx, cy, c), device_id_type=MESH))
        return out

    def start(ins, outs, sems):
        for cp in copies(ins, outs, sems, False):
            cp.start()

    def finish(ins, outs, sems):
        for cp in copies(ins, outs, sems, True):
            cp.wait_recv()
        for cp in copies(ins, outs, sems, False):
            cp.wait_send()

    return _Comm(shards, [jax.ShapeDtypeStruct((4,) + a.shape, a.dtype) for a in shards], _dma_sems(3 * n, 3 * n),
                 start, finish)


def _gather_d2d(bufs):
    n = len(bufs)
    hs = [a.shape[1] // 2 for a in bufs]

    def copies(outs, sems, incoming):
        send, recv = sems
        x, y, c = _place()
        out = []
        for r in (1, 2, 3):
            cx, cy = _flip(x, y, r)
            for a in range(n):
                ref = outs[a].at[2 * cx + cy, pl.ds(((1 - c) if incoming else c) * hs[a], hs[a])]
                out.append(pltpu.make_async_remote_copy(
                    src_ref=ref, dst_ref=ref, send_sem=send.at[(r - 1) * n + a], recv_sem=recv.at[(r - 1) * n + a],
                    device_id=(x, y, 1 - c), device_id_type=MESH))
        return out

    def start(ins, outs, sems):
        for cp in copies(outs, sems, False):
            cp.start()

    def finish(ins, outs, sems):
        for cp in copies(outs, sems, True):
            cp.wait_recv()
        for cp in copies(outs, sems, False):
            cp.wait_send()

    return _Comm(bufs, [jax.ShapeDtypeStruct(a.shape, a.dtype) for a in bufs], _dma_sems(3 * n, 3 * n), start, finish,
                 aliases={a: a for a in range(n)})


def _swap_rows(packs):
    n = len(packs)
    hs = [a.shape[1] // 2 for a in packs]

    def copies(ins, outs, sems):
        send, recv = sems
        x, y, c = _place()
        return [pltpu.make_async_remote_copy(
            src_ref=ins[a].at[:, pl.ds((1 - c) * hs[a], hs[a])], dst_ref=outs[a], send_sem=send.at[a],
            recv_sem=recv.at[a], device_id=(x, y, 1 - c), device_id_type=MESH) for a in range(n)]

    def start(ins, outs, sems):
        for cp in copies(ins, outs, sems):
            cp.start()

    def finish(ins, outs, sems):
        for cp in copies(ins, outs, sems):
            cp.wait()

    return _Comm(packs, [jax.ShapeDtypeStruct((4, h, a.shape[2]), a.dtype) for a, h in zip(packs, hs)],
                 _dma_sems(n, n), start, finish)


def _exchange(arrs):
    n = len(arrs)

    def copies(ins, outs, sems):
        send, recv = sems
        x, y, c = _place()
        out = []
        for r in (1, 2, 3):
            cx, cy = _flip(x, y, r)
            for a in range(n):
                out.append(pltpu.make_async_remote_copy(
                    src_ref=ins[a].at[r - 1], dst_ref=outs[a].at[r - 1], send_sem=send.at[(r - 1) * n + a],
                    recv_sem=recv.at[(r - 1) * n + a], device_id=(cx, cy, c), device_id_type=MESH))
        return out

    def start(ins, outs, sems):
        for cp in copies(ins, outs, sems):
            cp.start()

    def finish(ins, outs, sems):
        for cp in copies(ins, outs, sems):
            cp.wait()

    return _Comm(arrs, [jax.ShapeDtypeStruct(a.shape, a.dtype) for a in arrs], _dma_sems(3 * n, 3 * n), start, finish)


def _to_sibling(arrs):
    n = len(arrs)

    def copies(ins, outs, sems):
        send, recv = sems
        x, y, c = _place()
        return [pltpu.make_async_remote_copy(
            src_ref=ins[a], dst_ref=outs[a], send_sem=send.at[a], recv_sem=recv.at[a],
            device_id=(x, y, 1 - c), device_id_type=MESH) for a in range(n)]

    def start(ins, outs, sems):
        for cp in copies(ins, outs, sems):
            cp.start()

    def finish(ins, outs, sems):
        for cp in copies(ins, outs, sems):
            cp.wait()

    return _Comm(arrs, [jax.ShapeDtypeStruct(a.shape, a.dtype) for a in arrs], _dma_sems(n, n), start, finish)


def _gather8(v, reduce, name):
    rows, w = v.shape

    def body(v_ref, out_ref, buf, send_sems, recv_sems):
        x, y, c = _place()
        me, sibling = (x, y, c), (x, y, 1 - c)
        chips = [_flip(x, y, r) for r in (1, 2, 3)]
        dst = out_ref if not reduce else buf

        def slot(px, py, pc):
            return dst.at[4 * px + 2 * py + pc]

        def copy(k, block, to, src=None):
            return pltpu.make_async_remote_copy(
                src_ref=slot(*block) if src is None else src, dst_ref=slot(*block), send_sem=send_sems.at[k],
                recv_sem=recv_sems.at[k], device_id=to, device_id_type=MESH)

        dst[4 * x + 2 * y + c] = v_ref[...]
        first = [copy(0, me, sibling, src=v_ref)]
        first += [copy(1 + j, me, (*chip, c), src=v_ref) for j, chip in enumerate(chips)]
        for cp in first:
            cp.start()
        passed = [copy(4 + j, (*chip, c), sibling) for j, chip in enumerate(chips)]
        for j, chip in enumerate(chips):
            copy(1 + j, (*chip, c), me).wait_recv()
            passed[j].start()
        copy(0, sibling, me).wait_recv()
        for j, chip in enumerate(chips):
            copy(4 + j, (*chip, 1 - c), me).wait_recv()
        for cp in first + passed:
            cp.wait_send()
        if reduce:
            acc = buf[0]
            for d in range(1, 8):
                acc = acc + buf[d]
            out_ref[...] = acc

    vm = pl.BlockSpec(memory_space=pltpu.VMEM)
    scratch = [pltpu.VMEM((8, rows, w) if reduce else (8, 8, 128), F32), pltpu.SemaphoreType.DMA((7,)),
               pltpu.SemaphoreType.DMA((7,))]
    out_shape = jax.ShapeDtypeStruct((rows, w) if reduce else (8, rows, w), F32)
    return _pcall(body, in_specs=[vm], out_specs=vm, out_shape=out_shape, scratch_shapes=scratch,
                  compiler_params=pltpu.CompilerParams(vmem_limit_bytes=VMEM_LIMIT), name=name)(v)


SMALL = [("norm_mix", 1024), ("b_gate", 2048), ("conv_b", 4096), ("dt_bias", 32), ("a_log", 32), ("d_skip", 32),
         ("ssm_norm", 2048), ("norm_xa", 1024), ("norm_mem", 1024), ("norm_mlp", 1024)]


def _rows_of(width):
    return max(1, width // 1024)


def _pack_rows(pieces):
    out = []
    for p in pieces:
        p = p.astype(F32)
        if p.shape[-1] < 1024:
            p = jnp.pad(p, ((0, 0), (0, 1024 - p.shape[-1])))
        out.append(p.reshape(-1, 1024))
    cat = jnp.concatenate(out, axis=0)
    pad = (-cat.shape[0]) % 8
    return jnp.pad(cat, ((0, pad), (0, 0))) if pad else cat


def _unpack_rows(packed, widths_rows):
    out, at = [], 0
    for r, w in widths_rows:
        k = r * _rows_of(w)
        p = packed[at:at + k]
        at += k
        out.append(p[:, :w] if w < 1024 else p.reshape(r, w))
    return out


def _to_cat(w):
    pieces = [w[:, O_XBC:O_XBC + 4096], w[:, O_Q:O_Q + 512], w[:, O_K:O_K + 512], w[:, O_DT:O_DT + 32],
              jnp.zeros((w.shape[0], DT_PAD - 32), w.dtype), w[:, O_Z:O_Z + 2048], w[:, O_GATES:O_GATES + 2048],
              w[:, O_V:O_V + 1024], w[:, O_G:O_G + 1024]]
    return jnp.concatenate(pieces, axis=1)


def _from_cat(g):
    pieces = [g[:, C_Q:C_Q + 512], g[:, C_K:C_K + 512], g[:, C_V:C_V + 1024], g[:, C_G:C_G + 1024],
              g[:, C_Z:C_Z + 2048], g[:, C_XBC:C_XBC + 4096], g[:, C_DT:C_DT + 32], g[:, C_GATES:C_GATES + 2048]]
    return jnp.concatenate(pieces, axis=1)


PACK_ROWS = [("w_br_ret", 256), ("w_br_ssm", 512), ("w_out", 256), ("xa_wq", 256), ("xa_wo", 256), ("mlp_w1", 1024),
             ("mlp_w2", 1024)]
PACK_N = sum(r for _, r in PACK_ROWS)


def _blocks_rows(g, n):
    return g.reshape(4, n, g.shape[-1])


def _blocks_cols(g, n):
    return g.reshape(g.shape[0], 4, n).transpose(1, 0, 2)


def kernel(x, mem, positions, norm_mix, w_in, b_gate, conv_w, conv_b, dt_bias, a_log, d_skip, ssm_norm, w_br_ret, w_br_ssm, w_out, norm_xa, norm_mem, xa_wq, xa_wkv, xa_wo, norm_mlp, mlp_w1, mlp_w2, norm_final, loss_target, m_norm_mix, m_w_in, m_b_gate, m_conv_w, m_conv_b, m_dt_bias, m_a_log, m_d_skip, m_ssm_norm, m_w_br_ret, m_w_br_ssm, m_w_out, m_norm_xa, m_norm_mem, m_xa_wq, m_xa_wkv, m_xa_wo, m_norm_mlp, m_mlp_w1, m_mlp_w2, m_norm_final, v_norm_mix, v_w_in, v_b_gate, v_conv_w, v_conv_b, v_dt_bias, v_a_log, v_d_skip, v_ssm_norm, v_w_br_ret, v_w_br_ssm, v_w_out, v_norm_xa, v_norm_mem, v_xa_wq, v_xa_wkv, v_xa_wo, v_norm_mlp, v_mlp_w1, v_mlp_w2, v_norm_final):
    W = dict(norm_mix=norm_mix, w_in=w_in, b_gate=b_gate, conv_w=conv_w, conv_b=conv_b, dt_bias=dt_bias, a_log=a_log,
             d_skip=d_skip, ssm_norm=ssm_norm, w_br_ret=w_br_ret, w_br_ssm=w_br_ssm, w_out=w_out, norm_xa=norm_xa,
             norm_mem=norm_mem, xa_wq=xa_wq, xa_wkv=xa_wkv, xa_wo=xa_wo, norm_mlp=norm_mlp, mlp_w1=mlp_w1,
             mlp_w2=mlp_w2, norm_final=norm_final)
    M = dict(norm_mix=m_norm_mix, w_in=m_w_in, b_gate=m_b_gate, conv_w=m_conv_w, conv_b=m_conv_b, dt_bias=m_dt_bias,
             a_log=m_a_log, d_skip=m_d_skip, ssm_norm=m_ssm_norm, w_br_ret=m_w_br_ret, w_br_ssm=m_w_br_ssm,
             w_out=m_w_out, norm_xa=m_norm_xa, norm_mem=m_norm_mem, xa_wq=m_xa_wq, xa_wkv=m_xa_wkv, xa_wo=m_xa_wo,
             norm_mlp=m_norm_mlp, mlp_w1=m_mlp_w1, mlp_w2=m_mlp_w2, norm_final=m_norm_final)
    V = dict(norm_mix=v_norm_mix, w_in=v_w_in, b_gate=v_b_gate, conv_w=v_conv_w, conv_b=v_conv_b, dt_bias=v_dt_bias,
             a_log=v_a_log, d_skip=v_d_skip, ssm_norm=v_ssm_norm, w_br_ret=v_w_br_ret, w_br_ssm=v_w_br_ssm,
             w_out=v_w_out, norm_xa=v_norm_xa, norm_mem=v_norm_mem, xa_wq=v_xa_wq, xa_wkv=v_xa_wkv, xa_wo=v_xa_wo,
             norm_mlp=v_norm_mlp, mlp_w1=v_mlp_w1, mlp_w2=v_mlp_w2, norm_final=v_norm_final)
    nl = w_in.shape[0]
    s = x.shape[1]
    x0 = x[0]
    mem2 = mem[0]
    tgt = loss_target[0]
    blk = 2 * lax.axis_index("x") + lax.axis_index("y")

    groups = [["w_in"], ["xa_wkv", "w_br_ret", "w_br_ssm", "w_out", "xa_wq", "xa_wo"], ["mlp_w1", "mlp_w2"]]
    big = groups[0] + groups[1] + groups[2]
    zero = jnp.zeros((), jnp.int32)
    blk = blk.astype(jnp.int32)

    def shards_of(l, ks):
        return [W[k][l].astype(BF16) for k in ks]

    def with_own(l, bufs):
        return {k: lax.dynamic_update_slice(b, W[k][l].astype(BF16)[None], (blk, zero, zero)) for k, b in zip(big, bufs)}

    landed = _run_comm(_gather_ici(shards_of(0, big)), "gather_weights")
    gl = with_own(0, _run_comm(_gather_d2d(landed), "gather_weights_cores"))
    cw_all = _gather8(conv_w.reshape(nl * SSM_CONV, 1024), False, "gather_conv_w")
    cw_full = cw_all.reshape(4, 2, nl, SSM_CONV, 1024)[:, 0].transpose(1, 2, 0, 3).reshape(nl, SSM_CONV, SSM_CONV_DIM)

    offs = {}
    at = 0
    for k, r in PACK_ROWS:
        offs[k] = (at, r)
        at += r


    inv_freq = ROPE_THETA ** (-jnp.arange(0, RET_QK_DIM, 2, dtype=F32) / RET_QK_DIM)
    ang = positions.astype(F32)[0][:, None] * inv_freq
    cos, sin = jnp.cos(ang), jnp.sin(ang)
    cosf = jnp.concatenate([cos, cos], axis=1)
    sinf = jnp.concatenate([-sin, sin], axis=1)
    dm, qd, kd, cd = (jnp.asarray(c) for c in _ret_constants())
    eye, blkm = (jnp.asarray(c) for c in _ssd_constants())
    consts = (dm, qd, kd, cd, eye, blkm)
    e_np = _head_expand()
    eexp = jnp.asarray(e_np, BF16)
    eexp_t = jnp.asarray(e_np.T.copy(), BF16)

    saved = []
    xcur = x0
    for l in range(nl):
        rows_weight = lambda k: gl[k].reshape(-1, D_MODEL)
        wcat = _to_cat(jnp.concatenate([gl["w_in"][j] for j in range(4)], axis=1))
        wr, ws, wo = rows_weight("w_br_ret"), rows_weight("w_br_ssm"), rows_weight("w_out")
        wq, wxo, w2 = rows_weight("xa_wq"), rows_weight("xa_wo"), rows_weight("mlp_w2")
        w1, wkv = gl["mlp_w1"], gl["xa_wkv"]
        cw, cb = cw_full[l], conv_b[l][None]
        dtb = jnp.pad(dt_bias[l], (0, 128 - SSM_HEADS))[None]
        ax = jnp.repeat(-jnp.exp(a_log[l]), 64)[None]
        dsk = jnp.repeat(d_skip[l], 64)[None]
        bg, sn = b_gate[l][None], ssm_norm[l][None]
        more = l + 1 < nl
        ici = [_gather_ici(shards_of(l + 1, ks)) for ks in groups] if more else [None] * 3
        _carry(ici[0])
        proj, u = _nmm(xcur, norm_mix[l][None], wcat, "in_proj", save_u=True)
        _carry(ici[1])
        qr, kr, xc, dtx = _prescan(proj, cosf, sinf, cw, cb, dtb, eexp, "prescan")
        _carry(ici[2])
        yr, ys, sst, hst = _scan_fwd(qr, kr, proj, xc, dtx, ax, consts, "scan_fwd")
        x1 = _postscan_fwd(xcur, yr, ys, xc, proj, bg, dsk, sn, wr, ws, wo, "postscan")
        kv = _bf(_nmm(mem2, norm_mem[l][None], wkv, "mem_kv"))
        x2 = _xattn_fwd(x1, norm_xa[l][None], wq, kv, wxo, "xattn")
        cores = _gather_d2d(ici[0].results + ici[1].results + ici[2].results) if more else None
        _carry(cores)
        x3 = _mlp_fwd(x2, norm_mlp[l][None], w1, w2, "mlp")
        if more:
            gl = with_own(l + 1, cores.results)
        saved.append(dict(x0=xcur, x1=x1, x2=x2, proj=proj, u=u, qr=qr, kr=kr, xc=xc, dtx=dtx, yr=yr, ys=ys, sst=sst,
                          hst=hst, kv=kv, wcat=wcat, wr=wr, ws=ws, wo=wo, wq=wq, wxo=wxo, w1=w1, w2=w2, wkv=wkv, cw=cw,
                          cb=cb, dtb=dtb, ax=ax, dsk=dsk, bg=bg, sn=sn))
        xcur = x3

    dx, loss_acc, dnf = _final(xcur, norm_final[None], tgt, "final")
    loss = lax.psum(loss_acc[0, 0], ("x", "y", "c"))

    small_g = [None] * nl
    c = lax.axis_index("c")
    half_id = c.astype(jnp.int32)[None]
    sel_own = blk[None]
    sel_rem = jnp.stack([blk ^ 1, blk ^ 2, blk ^ 3])
    layer_grads = {k: [None] * nl for k in big}

    def pair_sums(packs, got):
        own = [_pair_sum(p, g_, sel_own, half_id, F32, "chip_sum_own")[0] for p, g_ in zip(packs, got)]
        out_b = [_pair_sum(p, g_, sel_rem, half_id, BF16, "chip_sum_send") for p, g_ in zip(packs, got)]
        return own, out_b

    def totals(own, inc):
        return [_sum_cast([o, i_[0], i_[1], i_[2]], F32, "grads_total") for o, i_ in zip(own, inc)]

    def finish_layer(lr, red_half, sib_half):
        def whole(i):
            mine_, theirs_ = red_half[i], sib_half[i]
            return jnp.concatenate([jnp.where(c == 0, mine_, theirs_), jnp.where(c == 0, theirs_, mine_)], axis=0)

        full1 = whole(0)
        for k, r in PACK_ROWS:
            layer_grads[k][lr] = full1[offs[k][0]:offs[k][0] + r]
        layer_grads["w_in"][lr] = whole(1)
        layer_grads["xa_wkv"][lr] = whole(2)

    riding = None
    for l in reversed(range(nl)):
        sv = saved[l]
        swap = _swap_rows(riding[1]) if riding else None
        _carry(swap)
        dx2, hm, rm, dam, dg_mlp = _mlp_bwd(sv["x2"], dx, norm_mlp[l][None], sv["w1"], sv["w2"], "mlp_bwd")
        if riding:
            own, out_b = pair_sums(riding[1], swap.results)
        dw1 = _mm_tn(hm, dam, "dw_mlp1", col_blocks=4)
        dw2 = _mm_tn(rm, dx, "dw_mlp2")
        dx1, hx, dqx, ox, dkv, dg_xa = _xattn_bwd(sv["x1"], dx2, norm_xa[l][None], sv["wq"], sv["kv"], sv["wxo"],
                                                  "xattn_bwd")
        dwq = _mm_tn(hx, dqx, "dw_xq")
        dwxo = _mm_tn(ox, dx2, "dw_xo")
        memn, dg_mem = _mem_bwd(mem2, norm_mem[l][None], dkv, sv["wkv"], "mem_bwd")
        dwkv = _mm_tn(memn, dkv, "dw_xkv", col_blocks=4)
        chips_a = _exchange(out_b[0:1]) if riding else None
        _carry(chips_a)
        (dyr, dys, dxs_skip, dproj, yrn, ysn, mg, dbr, dbs, dbg, ddsk, dsn) = _postscan_bwd(
            dx1, sv["yr"], sv["ys"], sv["xc"], sv["proj"], sv["bg"], sv["dsk"], sv["sn"], sv["wr"], sv["ws"], sv["wo"],
            "postscan_bwd")
        dwo = _mm_tn(mg, dx1, "dw_out")
        dwr = _mm_tn(yrn, dbr, "dw_br_ret")
        dws = _mm_tn(ysn, dbs, "dw_br_ssm")
        chips_b = _exchange(out_b[1:3]) if riding else None
        _carry(chips_b)
        dqr, dkr, dproj, dxc, gdtx, da_cols = _scan_bwd(sv["qr"], sv["kr"], sv["proj"], sv["xc"], sv["dtx"], sv["ax"],
                                                        consts, sv["sst"], sv["hst"], dyr, dys, dproj, "scan_bwd")
        if riding:
            red_half = totals(own, chips_a.results + chips_b.results)
        cores = _to_sibling(red_half) if riding else None
        _carry(cores)
        dproj, dcw, dcb, ddtb = _prescan_bwd(sv["proj"], dxc, dxs_skip, gdtx, dqr, dkr, cosf, sinf, sv["cw"], sv["cb"],
                                             sv["dtb"], eexp_t, dproj, "prescan_bwd")
        if riding:
            finish_layer(riding[0], red_half, cores.results)
        dwcat = _mm_tn(sv["u"], dproj, "dw_in")
        dx, dg_mix = _in_bwd(dproj, sv["wcat"], sv["x0"], norm_mix[l][None], dx1, "in_bwd")

        da_log = (da_cols.reshape(SSM_HEADS, 64).sum(axis=1)) * (-jnp.exp(a_log[l]))
        dd_skip = ddsk[0].reshape(SSM_HEADS, 64).sum(axis=1)
        small_g[l] = [dg_mix[0:1], dbg[0:1], dcb[0:1], ddtb[0:1, :SSM_HEADS], da_log[None], dd_skip[None], dsn[0:1],
                      dg_xa[0:1], dg_mem[0:1], dg_mlp[0:1], dcw[0::8]]
        pack = jnp.concatenate([_blocks_rows(dwr, 256), _blocks_rows(dws, 512), _blocks_rows(dwo, 256),
                                _blocks_rows(dwq, 256), _blocks_rows(dwxo, 256), dw1,
                                _blocks_rows(dw2, 1024)], axis=1)
        riding = (l, [pack, _blocks_cols(_from_cat(dwcat), IN_DIM // 4), dwkv])

    got = _run_comm(_swap_rows(riding[1]), "grads_core_swap")
    own, out_b = pair_sums(riding[1], got)
    red_half = totals(own, _run_comm(_exchange(out_b), "grads_chip_exchange"))
    finish_layer(riding[0], red_half, _run_comm(_to_sibling(red_half), "grads_core_join"))
    grad_x = dx[None]

    pieces = []
    for l in range(nl):
        pieces += small_g[l]
    pieces.append(dnf[0:1])
    small_sum = _gather8(_pack_rows(pieces), True, "reduce_small")
    layout = []
    for l in range(nl):
        layout += [(1, w) for _, w in SMALL] + [(SSM_CONV, SSM_CONV_DIM)]
    layout.append((1, 1024))
    red = _unpack_rows(small_sum, layout)
    per = len(SMALL) + 1
    g_small = {k: jnp.concatenate([red[l * per + i] for l in range(nl)], axis=0) for i, (k, _) in enumerate(SMALL)}
    g_convw_full = jnp.stack([red[l * per + len(SMALL)] for l in range(nl)])
    g_small["conv_w"] = lax.dynamic_slice_in_dim(g_convw_full, blk * 1024, 1024, axis=2)
    g_small["norm_final"] = red[-1][0]

    grads = dict(g_small)
    for k in big:
        grads[k] = jnp.stack(layer_grads[k])

    delta, new_m, new_v = {}, {}, {}
    for k in ["w_in", "xa_wkv"] + [k for k, _ in PACK_ROWS]:
        delta[k], new_m[k], new_v[k] = _adamw(W[k], grads[k], M[k], V[k], "adamw_" + k)
    small_names = [k for k, _ in SMALL] + ["conv_w", "norm_final"]

    def pack_small(src):
        ps = []
        for k in small_names:
            a = src[k]
            ps.append(a.reshape(-1, a.shape[-1]) if a.ndim > 1 else a[None])
        return _pack_rows(ps)

    ds_, ms_, vs_ = _adamw(pack_small(W), pack_small(grads), pack_small(M), pack_small(V), "adamw_small")
    lay2 = []
    for k in small_names:
        a = W[k]
        lay2.append((int(np.prod(a.shape[:-1])) if a.ndim > 1 else 1, a.shape[-1]))
    for src, dst in ((ds_, delta), (ms_, new_m), (vs_, new_v)):
        for k, piece in zip(small_names, _unpack_rows(src, lay2)):
            dst[k] = piece.reshape(W[k].shape)

    names = ["norm_mix", "w_in", "b_gate", "conv_w", "conv_b", "dt_bias", "a_log", "d_skip", "ssm_norm", "w_br_ret",
             "w_br_ssm", "w_out", "norm_xa", "norm_mem", "xa_wq", "xa_wkv", "xa_wo", "norm_mlp", "mlp_w1", "mlp_w2",
             "norm_final"]
    return (loss, grad_x, *[grads[n] for n in names], *[delta[n] for n in names], *[new_m[n] for n in names],
            *[new_v[n] for n in names])
```

```python
import numpy as np
import jax
import jax.numpy as jnp
from jax import lax
from jax.experimental import pallas as pl
from jax.experimental.pallas import tpu as pltpu

F32 = jnp.float32
BF16 = jnp.bfloat16
MESH = pl.DeviceIdType.MESH

D_MODEL = 1024
CHUNK = 64
EPS = 1e-6
RET_HEADS = 4
RET_QK_DIM = 128
RET_V_DIM = 256
RET_QK = 512
RET_V = 1024
ROPE_THETA = 10000.0
SSM_INNER = 2048
SSM_HEADS = 32
SSM_GROUPS = 8
SSM_STATE = 128
SSM_CONV = 4
SSM_BC = 1024
SSM_CONV_DIM = 4096
XA_HEADS = 4
XA_HEAD_DIM = 256
D_FF = 4096
GROUP_W = 256

DT_PAD = 1024
IN_DIM = 11296
NP = 12288
C_XBC, C_Q, C_K, C_DT, C_Z, C_GATES, C_V, C_G = 0, 4096, 4608, 5120, 6144, 8192, 10240, 11264
O_Q, O_K, O_V, O_G, O_Z, O_XBC, O_DT, O_GATES = 0, 512, 1024, 2048, 3072, 5120, 9216, 9248

ADAM_LR = 0.001
ADAM_B1 = 0.9
ADAM_B2 = 0.999
ADAM_EPS = 1e-08
ADAM_WD = 0.01
ADAM_STEP = 10

VMEM_LIMIT = 56 * 1024 * 1024


def _params(*sem):
    return pltpu.CompilerParams(dimension_semantics=sem, vmem_limit_bytes=VMEM_LIMIT)


_CARRY = []


def _carry(comm):
    if comm is not None:
        _CARRY.append(comm)


def _pcall(body, **kw):
    if _CARRY:
        return _hosted(body, _CARRY.pop(), kw)
    return pl.pallas_call(body, **kw)


class _Comm:
    def __init__(self, ins, out_shapes, sems, start, finish, aliases=None):
        self.ins, self.out_shapes, self.sems = list(ins), list(out_shapes), list(sems)
        self.start, self.finish, self.aliases = start, finish, dict(aliases or {})
        self.results = None


def _hosted(body, comm, kw):
    in_specs = list(kw.pop("in_specs"))
    out_specs, out_shape = kw.pop("out_specs"), kw.pop("out_shape")
    single = not isinstance(out_shape, (list, tuple))
    if single:
        out_specs, out_shape = [out_specs], [out_shape]
    out_specs, out_shape = list(out_specs), list(out_shape)
    scratch = list(kw.pop("scratch_shapes", []))
    grid = tuple(kw.get("grid", ()))
    aliases = dict(kw.pop("input_output_aliases", {}))
    n_in, n_out, n_sc = len(in_specs), len(out_shape), len(scratch)
    c_in, c_out = len(comm.ins), len(comm.out_shapes)
    for i, o in comm.aliases.items():
        aliases[n_in + i] = n_out + o
    kw["compiler_params"] = _params(*(["arbitrary"] * len(grid)))

    def wrapped(*refs):
        at = 0
        parts = []
        for cnt in (n_in, c_in, n_out, c_out, n_sc):
            parts.append(refs[at:at + cnt])
            at += cnt
        a, ci, b, co, s = parts
        cs = refs[at:]
        first, last = None, None
        for d, size in enumerate(grid):
            f, l_ = pl.program_id(d) == 0, pl.program_id(d) == size - 1
            first = f if first is None else first & f
            last = l_ if last is None else last & l_

        @pl.when(first)
        def _():
            comm.start(ci, co, cs)

        body(*a, *b, *s)

        @pl.when(last)
        def _():
            comm.finish(ci, co, cs)

    call = _pcall(wrapped, in_specs=in_specs + [ANY] * c_in, out_specs=out_specs + [ANY] * c_out,
                  out_shape=out_shape + comm.out_shapes, scratch_shapes=scratch + comm.sems,
                  input_output_aliases=aliases, **kw)

    def run(*ops):
        res = call(*ops, *comm.ins)
        comm.results = list(res[n_out:])
        return res[0] if single else list(res[:n_out])

    return run


def _run_comm(comm, name):
    def body(*refs):
        c_in, c_out = len(comm.ins), len(comm.out_shapes)
        ci, co, cs = refs[:c_in], refs[c_in:c_in + c_out], refs[c_in + c_out:]
        comm.start(ci, co, cs)
        comm.finish(ci, co, cs)

    aliases = {i: o for i, o in comm.aliases.items()}
    res = _pcall(body, in_specs=[ANY] * len(comm.ins), out_specs=[ANY] * len(comm.out_shapes),
                 out_shape=comm.out_shapes, scratch_shapes=comm.sems, input_output_aliases=aliases, name=name)(*comm.ins)
    comm.results = list(res)
    return comm.results


def _bf(a):
    return a.astype(BF16)


def _dot(a, b):
    return jnp.dot(_bf(a), _bf(b), preferred_element_type=F32)


def _dot_nt(a, b):
    return lax.dot_general(_bf(a), _bf(b), (((1,), (1,)), ((), ())), preferred_element_type=F32)


def _dot_tn(a, b):
    return lax.dot_general(_bf(a), _bf(b), (((0,), (0,)), ((), ())), preferred_element_type=F32)


def _colsum(a):
    return jnp.sum(a, axis=0, keepdims=True)


def _rstd(x):
    return lax.rsqrt(jnp.mean(x * x, axis=-1, keepdims=True) + EPS)


def _rms_bwd(dy, x, rstd):
    xh = x * rstd
    return rstd * (dy - xh * jnp.mean(dy * xh, axis=-1, keepdims=True))


def _sigmoid(x):
    return 1.0 / (1.0 + jnp.exp(-x))


def _silu_and_grad(x):
    s = _sigmoid(x)
    return x * s, s + x * s * (1.0 - s)


def _softplus(x):
    u = jnp.exp(-jnp.abs(x))
    l1p = jnp.where(u < 1e-4, u * (1.0 - 0.5 * u), jnp.log(1.0 + u))
    return jnp.maximum(x, 0.0) + l1p


def _split3_dot(a, e):
    hi = a.astype(BF16)
    r1 = a - hi.astype(F32)
    mid = r1.astype(BF16)
    lo = (r1 - mid.astype(F32)).astype(BF16)
    return (jnp.dot(hi, e, preferred_element_type=F32) + jnp.dot(mid, e, preferred_element_type=F32)
            + jnp.dot(lo, e, preferred_element_type=F32))


def _cumsum_rows(a):
    rows = lax.broadcasted_iota(jnp.int32, a.shape, 0)
    s = 1
    while s < a.shape[0]:
        a = a + jnp.where(rows >= s, pltpu.roll(a, s, 0), 0.0)
        s *= 2
    return a


def _revcumsum_rows(a):
    n = a.shape[0]
    rows = lax.broadcasted_iota(jnp.int32, a.shape, 0)
    s = 1
    while s < n:
        a = a + jnp.where(rows < n - s, pltpu.roll(a, n - s, 0), 0.0)
        s *= 2
    return a


def _rms_groups(y, width):
    out = []
    for h in range(y.shape[1] // width):
        slab = y[:, h * width:(h + 1) * width]
        out.append((slab, _rstd(slab)))
    return out


def _ret_constants():
    idx = np.arange(CHUNK, dtype=np.float32)
    lg = np.log1p(-(np.float32(2.0) ** (np.float32(-5.0) - np.arange(RET_HEADS, dtype=np.float32)))).astype(np.float32)
    rel = np.abs(idx[:, None] - idx[None, :])
    dm = np.exp(lg[:, None, None] * rel).astype(np.float32)
    qd = np.exp(lg[None, :] * (idx[:, None] + 1.0)).astype(np.float32)
    kd = np.exp(lg[None, :] * (CHUNK - 1.0 - idx[:, None])).astype(np.float32)
    cd = np.exp(lg * CHUNK).astype(np.float32)
    qd = np.repeat(qd, RET_QK_DIM, axis=1)
    kd = np.repeat(kd, RET_QK_DIM, axis=1)
    cd = np.repeat(cd, RET_QK_DIM)[:, None] * np.ones((1, RET_V_DIM), np.float32)
    return dm, qd, kd, cd.astype(np.float32)


def _ssd_constants():
    eye = np.tile(np.eye(CHUNK, dtype=np.float32), (1, GROUP_W // CHUNK))
    blk = np.kron(np.eye(GROUP_W // CHUNK, dtype=np.float32), np.ones((CHUNK, CHUNK), np.float32))
    return eye, blk


def _head_expand():
    e = np.zeros((128, SSM_INNER), np.float32)
    for h in range(SSM_HEADS):
        e[h, h * 64:(h + 1) * 64] = 1.0
    return e


def _ret_chunk_fwd(qh, kh, vh, sh, dmh, qdh, kdh, cdh):
    a = _dot_nt(qh, kh) * dmh
    y = _dot(a, vh) + _dot(qh * qdh, sh)
    s_new = sh * cdh + _dot_tn(kh * kdh, vh)
    return y, s_new


def _ret_chunk_bwd(qh, kh, vh, sh, dmh, qdh, kdh, cdh, dy, ds_new):
    a = _dot_nt(qh, kh) * dmh
    dp = _dot_nt(dy, vh) * dmh
    dq = _dot(dp, kh) + _dot_nt(dy, sh) * qdh
    dk = _dot_tn(dp, qh) + _dot_nt(vh, ds_new) * kdh
    dv = _dot_tn(a, dy) + _dot(kh * kdh, ds_new)
    ds = cdh * ds_new + _dot_tn(qh * qdh, dy)
    return dq, dk, dv, ds


def _ssd_common(xs, dtx, ax, eye):
    cum = _cumsum_rows(dtx * ax)
    last = cum[CHUNK - 1:CHUNK, :]
    r = _colsum(jnp.where(eye > 0.5, cum, 0.0))
    return cum, last, r, xs * dtx


def _tile4(a):
    return jnp.concatenate([a, a, a, a], axis=0)


def _ssd_chunk_fwd(xs, dtx, b, c, ax, hg, eye, blk):
    cum, last, r, x = _ssd_common(xs, dtx, ax, eye)
    lam = jnp.exp(-jnp.abs(cum - r))
    wc = _dot_nt(c, _tile4(b)) * lam
    bd = _tile4(x) * blk
    y = _dot(wc, bd) + _dot(c, hg) * jnp.exp(cum)
    h_new = hg * jnp.exp(last) + _dot_tn(b, x * jnp.exp(last - cum))
    return y, h_new


def _ssd_chunk_bwd(xs, dtx, b, c, ax, hg, eye, blk, dy, dh_new):
    cum, last, r, x = _ssd_common(xs, dtx, ax, eye)
    delta = cum - r
    lam = jnp.exp(-jnp.abs(delta))
    b4 = _tile4(b)
    cb4 = _dot_nt(c, b4)
    wc = cb4 * lam
    bd = _tile4(x) * blk
    ecx = jnp.exp(cum)
    wl = jnp.exp(last - cum)
    ecl = jnp.exp(last)
    z = _dot(c, hg)
    dwc = _dot_nt(dy, bd)
    dbd = _dot_tn(wc, dy) * blk
    dx = dbd[0:64] + dbd[64:128] + dbd[128:192] + dbd[192:256]
    dt_ = _dot(b, dh_new)
    dx = dx + dt_ * wl
    dcb4 = dwc * lam
    dz = dy * ecx
    dc = _dot(dcb4, b4) + _dot_nt(dz, hg)
    db4 = _dot_tn(dcb4, c)
    db = db4[0:64] + db4[64:128] + db4[128:192] + db4[192:256] + _dot_nt(x * wl, dh_new)
    g = dwc * cb4 * lam * (-jnp.sign(delta))
    dr = -_colsum(g)
    dwl = dt_ * x * wl
    u = g + eye * dr + dy * z * ecx - dwl
    lastrow = _colsum(dwl) + _colsum(dh_new * hg) * ecl
    rows = lax.broadcasted_iota(jnp.int32, u.shape, 0)
    u = u + jnp.where(rows == CHUNK - 1, lastrow, 0.0)
    dh = _dot_tn(c, dz) + dh_new * ecl
    rc = _revcumsum_rows(u)
    dxs = dx * dtx
    g_dtx = dx * xs + rc * ax
    da = _colsum(rc * dtx)
    return dxs, g_dtx, db, dc, da, dh


def _row_tile(s, want):
    t = min(s, want)
    assert s % t == 0
    return t


def _nmm(x, gain, w, name, tn=1024, save_u=False):
    s, d = x.shape
    blocked = w.ndim == 3
    if blocked:
        tn = w.shape[2]
        n = w.shape[0] * tn
        w_spec = pl.BlockSpec((1, d, tn), lambda i, j: (j, 0, 0))
    else:
        n = w.shape[1]
        w_spec = pl.BlockSpec((d, tn), lambda i, j: (0, j))
    tm = _row_tile(s, 1024)
    assert n % tn == 0

    def body(x_ref, g_ref, w_ref, *rest):
        o_ref, u_sc = rest[0], rest[-1]

        @pl.when(pl.program_id(1) == 0)
        def _():
            xx = x_ref[...]
            u = _bf((xx * _rstd(xx)) * g_ref[...])
            u_sc[...] = u
            if save_u:
                rest[1][...] = u

        o_ref[...] = jnp.dot(u_sc[...], w_ref[0] if blocked else w_ref[...], preferred_element_type=F32)

    out_shape = [jax.ShapeDtypeStruct((s, n), F32)]
    out_specs = [pl.BlockSpec((tm, tn), lambda i, j: (i, j))]
    if save_u:
        out_shape.append(jax.ShapeDtypeStruct((s, d), BF16))
        out_specs.append(pl.BlockSpec((tm, d), lambda i, j: (i, 0)))
    res = _pcall(
        body, grid=(s // tm, n // tn),
        in_specs=[pl.BlockSpec((tm, d), lambda i, j: (i, 0)), pl.BlockSpec((1, d), lambda i, j: (0, 0)), w_spec],
        out_specs=out_specs, out_shape=out_shape, scratch_shapes=[pltpu.VMEM((tm, d), BF16)],
        compiler_params=_params("parallel", "arbitrary"), name=name)(x, gain, w)
    return res if save_u else res[0]


def _mm_tn(a, b, name, tm=1024, tn=1024, col_blocks=None):
    k, m = a.shape
    n = b.shape[1]
    tk = _row_tile(k, 1024)
    tm, tn = min(tm, m), min(tn, n)
    if col_blocks:
        tn = n // col_blocks
    assert m % tm == 0 and n % tn == 0
    nk = k // tk

    def body(a_ref, b_ref, o_ref, acc):
        kk = pl.program_id(2)

        @pl.when(kk == 0)
        def _():
            acc[...] = jnp.zeros_like(acc)

        acc[...] += _dot_tn(a_ref[...], b_ref[...])

        @pl.when(kk == nk - 1)
        def _():
            if col_blocks:
                o_ref[0] = acc[...]
            else:
                o_ref[...] = acc[...]

    if col_blocks:
        out_spec = pl.BlockSpec((1, tm, tn), lambda i, j, kk: (j, i, 0))
        out_shape = jax.ShapeDtypeStruct((col_blocks, m, tn), F32)
    else:
        out_spec = pl.BlockSpec((tm, tn), lambda i, j, kk: (i, j))
        out_shape = jax.ShapeDtypeStruct((m, n), F32)
    return _pcall(
        body, grid=(m // tm, n // tn, nk),
        in_specs=[pl.BlockSpec((tk, tm), lambda i, j, kk: (kk, i)), pl.BlockSpec((tk, tn), lambda i, j, kk: (kk, j))],
        out_specs=out_spec, out_shape=out_shape,
        scratch_shapes=[pltpu.VMEM((tm, tn), F32)],
        compiler_params=_params("parallel", "parallel", "arbitrary"), name=name)(a, b)


def _mm_tn_into(a, b, name, pack, off, by_cols):
    k, m = a.shape
    n = b.shape[1]
    tk = _row_tile(k, 1024)
    nk = k // tk
    rows = m if by_cols else m // 4
    tm = min(rows, 1024)
    assert rows % tm == 0 and off % tm == 0 and n == (4096 if by_cols else 1024)
    per = rows // tm

    def body(a_ref, b_ref, *rest):
        o_ref, acc = rest[-2], rest[-1]
        kk = pl.program_id(2)

        @pl.when(kk == 0)
        def _():
            acc[...] = jnp.zeros_like(acc)

        acc[...] += _dot_tn(a_ref[...], b_ref[...])

        @pl.when(kk == nk - 1)
        def _():
            o_ref[0] = acc[...]

    if by_cols:
        out_spec = pl.BlockSpec((1, tm, 1024), lambda i, j, kk: (j, off // tm + i, 0))
    else:
        out_spec = pl.BlockSpec((1, tm, 1024), lambda i, j, kk: (i // per, off // tm + i % per, 0))
    in_specs = [pl.BlockSpec((tk, tm), lambda i, j, kk: (kk, i)), pl.BlockSpec((tk, 1024), lambda i, j, kk: (kk, j))]
    ops, alias = [a, b], {}
    if pack is not None:
        in_specs.append(ANY)
        ops.append(pack)
        alias = {2: 0}
    return _pcall(
        body, grid=(m // tm, n // 1024, nk), in_specs=in_specs, out_specs=out_spec,
        out_shape=jax.ShapeDtypeStruct((4, PACK_N, 1024), F32), scratch_shapes=[pltpu.VMEM((tm, 1024), F32)],
        input_output_aliases=alias, compiler_params=_params("parallel", "parallel", "arbitrary"), name=name)(*ops)


def _in_bwd(dproj, wcat, x, gain, dres, name):
    s, n = dproj.shape
    d = wcat.shape[0]
    tm = _row_tile(s, 1024)
    tk = 1024
    nk = n // tk
    ns = s // tm

    def body(dp_ref, w_ref, x_ref, g_ref, dr_ref, dx_ref, dg_ref, acc):
        i, kk = pl.program_id(0), pl.program_id(1)

        @pl.when(kk == 0)
        def _():
            acc[...] = jnp.zeros_like(acc)

        @pl.when((kk == 0) & (i == 0))
        def _():
            dg_ref[...] = jnp.zeros_like(dg_ref)

        acc[...] += _dot_nt(dp_ref[...], w_ref[...])

        @pl.when(kk == nk - 1)
        def _():
            xx = x_ref[...]
            r = _rstd(xx)
            du = acc[...]
            dg_ref[...] += _colsum(du * (xx * r))
            dx_ref[...] = dr_ref[...] + _rms_bwd(du * g_ref[...], xx, r)

    return _pcall(
        body, grid=(ns, nk),
        in_specs=[pl.BlockSpec((tm, tk), lambda i, kk: (i, kk)), pl.BlockSpec((d, tk), lambda i, kk: (0, kk)),
                  pl.BlockSpec((tm, d), lambda i, kk: (i, 0)), pl.BlockSpec((1, d), lambda i, kk: (0, 0)),
                  pl.BlockSpec((tm, d), lambda i, kk: (i, 0))],
        out_specs=[pl.BlockSpec((tm, d), lambda i, kk: (i, 0)), pl.BlockSpec((8, d), lambda i, kk: (0, 0))],
        out_shape=[jax.ShapeDtypeStruct((s, d), F32), jax.ShapeDtypeStruct((8, d), F32)],
        scratch_shapes=[pltpu.VMEM((tm, d), F32)],
        compiler_params=_params("arbitrary", "arbitrary"), name=name)(dproj, wcat, x, gain, dres)


def _prev_rows_spec(ts, width):
    return pl.BlockSpec((8, width), lambda i: (jnp.maximum(i * (ts // 8) - 1, 0), 0))


def _prescan(proj, cosf, sinf, cw, cb, dtb, eexp, name):
    s = proj.shape[0]
    ts = _row_tile(s, 256)

    def body(xbc_ref, prev_ref, q_ref, k_ref, dt_ref, cos_ref, sin_ref, cw_ref, cb_ref, dtb_ref, e_ref,
             qo_ref, ko_ref, xc_ref, dtx_ref):
        i = pl.program_id(0)
        for st in range(SSM_CONV_DIM // 128):
            sl = slice(st * 128, (st + 1) * 128)
            prev = jnp.where(i > 0, prev_ref[:, sl], 0.0)
            xcat = jnp.concatenate([prev, xbc_ref[:, sl]], axis=0)
            pre = cb_ref[:, sl] + cw_ref[3:4, sl] * xcat[8:8 + ts]
            for j in range(3):
                pre = pre + cw_ref[j:j + 1, sl] * pltpu.roll(xcat, 3 - j, 0)[8:8 + ts]
            xc_ref[:, sl] = pre * _sigmoid(pre)
        cs, sn = cos_ref[...], sin_ref[...]
        for h in range(RET_HEADS):
            sl = slice(h * 128, (h + 1) * 128)
            qh, kh = q_ref[:, sl], k_ref[:, sl]
            qo_ref[:, sl] = qh * cs + pltpu.roll(qh, 64, 1) * sn
            ko_ref[:, sl] = (kh * cs + pltpu.roll(kh, 64, 1) * sn) * (RET_QK_DIM ** -0.5)
        dtv = _softplus(dt_ref[:, 0:128] + dtb_ref[...])
        dtx_ref[...] = _split3_dot(dtv, e_ref[...])

    row = lambda w, c: pl.BlockSpec((ts, w), lambda i: (i, c))
    full = lambda a: pl.BlockSpec(a.shape, lambda i: (0,) * a.ndim)
    return _pcall(
        body, grid=(s // ts,),
        in_specs=[row(4096, 0), _prev_rows_spec(ts, 4096), row(512, C_Q // 512), row(512, C_K // 512),
                  row(DT_PAD, C_DT // DT_PAD), row(128, 0), row(128, 0), full(cw), full(cb), full(dtb), full(eexp)],
        out_specs=[row(512, 0), row(512, 0), row(4096, 0), row(2048, 0)],
        out_shape=[jax.ShapeDtypeStruct((s, 512), F32), jax.ShapeDtypeStruct((s, 512), F32),
                   jax.ShapeDtypeStruct((s, 4096), F32), jax.ShapeDtypeStruct((s, 2048), F32)],
        compiler_params=_params("parallel"), name=name)(proj, proj, proj, proj, proj, cosf, sinf, cw, cb, dtb, eexp)


def _scan_fwd(qr, kr, proj, xc, dtx, ax, consts, name):
    s = qr.shape[0]
    nc = s // CHUNK
    dm, qd, kd, cd, eye, blk = consts

    def body(q_ref, k_ref, v_ref, xc_ref, dtx_ref, ax_ref, dm_ref, qd_ref, kd_ref, cd_ref, eye_ref, blk_ref,
             yr_ref, ys_ref, sst_ref, hst_ref, s_sc, h_sc):
        @pl.when(pl.program_id(0) == 0)
        def _():
            s_sc[...] = jnp.zeros_like(s_sc)
            h_sc[...] = jnp.zeros_like(h_sc)

        sst_ref[0] = s_sc[...]
        hst_ref[0] = h_sc[...]
        for h in range(RET_HEADS):
            ql, vl = slice(h * 128, (h + 1) * 128), slice(h * 256, (h + 1) * 256)
            y, s_new = _ret_chunk_fwd(q_ref[:, ql], k_ref[:, ql], v_ref[:, vl], s_sc[ql, :], dm_ref[h],
                                      qd_ref[:, ql], kd_ref[:, ql], cd_ref[ql, :])
            yr_ref[:, vl] = y
            s_sc[ql, :] = s_new
        eye_v, blk_v = eye_ref[...], blk_ref[...]
        for g in range(SSM_GROUPS):
            sl = slice(g * GROUP_W, (g + 1) * GROUP_W)
            bl = slice(SSM_INNER + g * 128, SSM_INNER + (g + 1) * 128)
            cl = slice(SSM_INNER + SSM_BC + g * 128, SSM_INNER + SSM_BC + (g + 1) * 128)
            y, h_new = _ssd_chunk_fwd(xc_ref[:, sl], dtx_ref[:, sl], xc_ref[:, bl], xc_ref[:, cl], ax_ref[:, sl],
                                      h_sc[:, sl], eye_v, blk_v)
            ys_ref[:, sl] = y
            h_sc[:, sl] = h_new

    row = lambda w, c=0: pl.BlockSpec((CHUNK, w), lambda i: (i, c))
    full = lambda a: pl.BlockSpec(a.shape, lambda i: (0,) * a.ndim)
    return _pcall(
        body, grid=(nc,),
        in_specs=[row(512), row(512), row(1024, C_V // 1024), row(4096), row(2048), full(ax), full(dm), full(qd),
                  full(kd), full(cd), full(eye), full(blk)],
        out_specs=[row(1024), row(2048), pl.BlockSpec((1, 512, 256), lambda i: (i, 0, 0)),
                   pl.BlockSpec((1, 128, 2048), lambda i: (i, 0, 0))],
        out_shape=[jax.ShapeDtypeStruct((s, 1024), F32), jax.ShapeDtypeStruct((s, 2048), F32),
                   jax.ShapeDtypeStruct((nc, 512, 256), F32), jax.ShapeDtypeStruct((nc, 128, 2048), F32)],
        scratch_shapes=[pltpu.VMEM((512, 256), F32), pltpu.VMEM((128, 2048), F32)],
        compiler_params=_params("arbitrary"), name=name)(qr, kr, proj, xc, dtx, ax, dm, qd, kd, cd, eye, blk)


def _scan_bwd(qr, kr, proj, xc, dtx, ax, consts, sst, hst, dyr, dys, dproj, name):
    s = qr.shape[0]
    nc = s // CHUNK
    dm, qd, kd, cd, eye, blk = consts

    def body(q_ref, k_ref, v_ref, xc_ref, dtx_ref, ax_ref, dm_ref, qd_ref, kd_ref, cd_ref, eye_ref, blk_ref,
             sst_ref, hst_ref, dyr_ref, dys_ref, dproj_in, dq_ref, dk_ref, dv_ref, dxc_ref, gdt_ref, da_ref, ds_sc,
             dh_sc):
        @pl.when(pl.program_id(0) == 0)
        def _():
            ds_sc[...] = jnp.zeros_like(ds_sc)
            dh_sc[...] = jnp.zeros_like(dh_sc)
            da_ref[...] = jnp.zeros_like(da_ref)

        for h in range(RET_HEADS):
            ql, vl = slice(h * 128, (h + 1) * 128), slice(h * 256, (h + 1) * 256)
            dq, dk, dv, ds = _ret_chunk_bwd(q_ref[:, ql], k_ref[:, ql], v_ref[:, vl], sst_ref[0, ql, :], dm_ref[h],
                                            qd_ref[:, ql], kd_ref[:, ql], cd_ref[ql, :], dyr_ref[:, vl], ds_sc[ql, :])
            dq_ref[:, ql] = dq
            dk_ref[:, ql] = dk
            dv_ref[:, vl] = _bf(dv)
            ds_sc[ql, :] = ds
        eye_v, blk_v = eye_ref[...], blk_ref[...]
        for g in range(SSM_GROUPS):
            sl = slice(g * GROUP_W, (g + 1) * GROUP_W)
            bl = slice(SSM_INNER + g * 128, SSM_INNER + (g + 1) * 128)
            cl = slice(SSM_INNER + SSM_BC + g * 128, SSM_INNER + SSM_BC + (g + 1) * 128)
            dxs, g_dtx, db, dc, da, dh = _ssd_chunk_bwd(
                xc_ref[:, sl], dtx_ref[:, sl], xc_ref[:, bl], xc_ref[:, cl], ax_ref[:, sl], hst_ref[0, :, sl],
                eye_v, blk_v, dys_ref[:, sl], dh_sc[:, sl])
            dxc_ref[:, sl] = dxs
            dxc_ref[:, bl] = db
            dxc_ref[:, cl] = dc
            gdt_ref[:, sl] = g_dtx
            da_ref[:, sl] += da
            dh_sc[:, sl] = dh

    row = lambda w, c=0: pl.BlockSpec((CHUNK, w), lambda i: (nc - 1 - i, c))
    full = lambda a: pl.BlockSpec(a.shape, lambda i: (0,) * a.ndim)
    return _pcall(
        body, grid=(nc,),
        in_specs=[row(512), row(512), row(1024, C_V // 1024), row(4096), row(2048), full(ax), full(dm), full(qd),
                  full(kd), full(cd), full(eye), full(blk),
                  pl.BlockSpec((1, 512, 256), lambda i: (nc - 1 - i, 0, 0)),
                  pl.BlockSpec((1, 128, 2048), lambda i: (nc - 1 - i, 0, 0)), row(1024), row(2048), ANY],
        out_specs=[row(512), row(512), row(1024, C_V // 1024), row(4096), row(2048),
                   pl.BlockSpec((1, 2048), lambda i: (0, 0))],
        out_shape=[jax.ShapeDtypeStruct((s, 512), F32), jax.ShapeDtypeStruct((s, 512), F32),
                   jax.ShapeDtypeStruct(dproj.shape, BF16), jax.ShapeDtypeStruct((s, 4096), F32),
                   jax.ShapeDtypeStruct((s, 2048), F32), jax.ShapeDtypeStruct((1, 2048), F32)],
        scratch_shapes=[pltpu.VMEM((512, 256), F32), pltpu.VMEM((128, 2048), F32)],
        input_output_aliases={16: 2},
        compiler_params=_params("arbitrary"), name=name)(qr, kr, proj, xc, dtx, ax, dm, qd, kd, cd, eye, blk, sst, hst,
                                                          dyr, dys, dproj)


def _mix_values(yr, g, ys, xs, z, gates, bg, dsk, sn):
    sg, dsg = _silu_and_grad(g)
    ret = _rms_groups(yr, RET_V_DIM)
    yrn = jnp.concatenate([slab * r for slab, r in ret], axis=1) * sg
    sz, dsz = _silu_and_grad(z)
    ys0 = ys + xs * dsk
    ys1 = ys0 * sz
    grp = _rms_groups(ys1, GROUP_W)
    ysh = jnp.concatenate([slab * r for slab, r in grp], axis=1)
    ysn = ysh * sn
    gg = _sigmoid(gates + bg)
    return dict(sg=sg, dsg=dsg, ret=ret, yrn=yrn, sz=sz, dsz=dsz, ys0=ys0, ys1=ys1, grp=grp, ysh=ysh, ysn=ysn,
                gr=gg[:, :D_MODEL], gs=gg[:, D_MODEL:])


def _postscan_fwd(x, yr, ys, xc, proj, bg, dsk, sn, wr, ws, wo, name):
    s = x.shape[0]
    ts = _row_tile(s, 256)

    def body(x_ref, yr_ref, ys_ref, xs_ref, g_ref, z_ref, gt_ref, bg_ref, dsk_ref, sn_ref, wr_ref, ws_ref, wo_ref,
             o_ref):
        m = _mix_values(yr_ref[...], g_ref[...], ys_ref[...], xs_ref[...], z_ref[...], gt_ref[...], bg_ref[...],
                        dsk_ref[...], sn_ref[...])
        merged = m["gr"] * _dot(m["yrn"], wr_ref[...]) + m["gs"] * _dot(m["ysn"], ws_ref[...])
        o_ref[...] = x_ref[...] + _dot(merged, wo_ref[...])

    row = lambda w, c=0: pl.BlockSpec((ts, w), lambda i: (i, c))
    full = lambda a: pl.BlockSpec(a.shape, lambda i: (0,) * a.ndim)
    return _pcall(
        body, grid=(s // ts,),
        in_specs=[row(1024), row(1024), row(2048), row(2048), row(1024, C_G // 1024), row(2048, C_Z // 2048),
                  row(2048, C_GATES // 2048), full(bg), full(dsk), full(sn), full(wr), full(ws), full(wo)],
        out_specs=row(1024), out_shape=jax.ShapeDtypeStruct((s, D_MODEL), F32),
        compiler_params=_params("parallel"), name=name)(x, yr, ys, xc, proj, proj, proj, bg, dsk, sn, wr, ws, wo)


def _postscan_bwd(dout, yr, ys, xc, proj, bg, dsk, sn, wr, ws, wo, name):
    s = dout.shape[0]
    ts = _row_tile(s, 128)

    def body(do_ref, yr_ref, ys_ref, xs_ref, g_ref, z_ref, gt_ref, bg_ref, dsk_ref, sn_ref, wr_ref, ws_ref, wo_ref,
             dyr_ref, dys_ref, dxs_ref, dproj_ref, yrn_ref, ysn_ref, mg_ref, dbr_ref, dbs_ref,
             dbg_ref, ddsk_ref, dsn_ref):
        @pl.when(pl.program_id(0) == 0)
        def _():
            dbg_ref[...] = jnp.zeros_like(dbg_ref)
            ddsk_ref[...] = jnp.zeros_like(ddsk_ref)
            dsn_ref[...] = jnp.zeros_like(dsn_ref)

        xs = xs_ref[...]
        m = _mix_values(yr_ref[...], g_ref[...], ys_ref[...], xs, z_ref[...], gt_ref[...], bg_ref[...],
                        dsk_ref[...], sn_ref[...])
        gr, gs = m["gr"], m["gs"]
        br, bs = _dot(m["yrn"], wr_ref[...]), _dot(m["ysn"], ws_ref[...])
        dmerged = _dot_nt(do_ref[...], wo_ref[...])
        dgt = jnp.concatenate([dmerged * br * gr * (1.0 - gr), dmerged * bs * gs * (1.0 - gs)], axis=1)
        dproj_ref[:, C_GATES:C_GATES + 2048] = _bf(dgt)
        dbg_ref[...] += _colsum(dgt)
        dbr, dbs = dmerged * gr, dmerged * gs
        yrn_ref[...] = _bf(m["yrn"])
        ysn_ref[...] = _bf(m["ysn"])
        mg_ref[...] = _bf(gr * br + gs * bs)
        dbr_ref[...] = _bf(dbr)
        dbs_ref[...] = _bf(dbs)
        dyrn = _dot_nt(dbr, wr_ref[...])
        dysn = _dot_nt(dbs, ws_ref[...])
        rn = jnp.concatenate([slab * r for slab, r in m["ret"]], axis=1)
        dproj_ref[:, C_G:C_G + 1024] = _bf(dyrn * rn * m["dsg"])
        drn = dyrn * m["sg"]
        dyr_ref[...] = jnp.concatenate(
            [_rms_bwd(drn[:, h * RET_V_DIM:(h + 1) * RET_V_DIM], slab, r) for h, (slab, r) in enumerate(m["ret"])], axis=1)
        dsn_ref[...] += _colsum(dysn * m["ysh"])
        dysh = dysn * sn_ref[...]
        dys1 = jnp.concatenate(
            [_rms_bwd(dysh[:, h * GROUP_W:(h + 1) * GROUP_W], slab, r) for h, (slab, r) in enumerate(m["grp"])], axis=1)
        dproj_ref[:, C_Z:C_Z + 2048] = _bf(dys1 * m["ys0"] * m["dsz"])
        dys0 = dys1 * m["sz"]
        dys_ref[...] = dys0
        dxs_ref[...] = dys0 * dsk_ref[...]
        ddsk_ref[...] += _colsum(dys0 * xs)

    row = lambda w, c=0: pl.BlockSpec((ts, w), lambda i: (i, c))
    full = lambda a: pl.BlockSpec(a.shape, lambda i: (0,) * a.ndim)
    acc = lambda w: pl.BlockSpec((8, w), lambda i: (0, 0))
    sds = jax.ShapeDtypeStruct
    return _pcall(
        body, grid=(s // ts,),
        in_specs=[row(1024), row(1024), row(2048), row(2048), row(1024, C_G // 1024), row(2048, C_Z // 2048),
                  row(2048, C_GATES // 2048), full(bg), full(dsk), full(sn), full(wr), full(ws), full(wo)],
        out_specs=[row(1024), row(2048), row(2048), row(NP), row(1024), row(2048), row(1024),
                   row(1024), row(1024), acc(2048), acc(2048), acc(2048)],
        out_shape=[sds((s, 1024), F32), sds((s, 2048), F32), sds((s, 2048), F32), sds((s, NP), BF16),
                   sds((s, 1024), BF16), sds((s, 2048), BF16),
                   sds((s, 1024), BF16), sds((s, 1024), BF16), sds((s, 1024), BF16), sds((8, 2048), F32),
                   sds((8, 2048), F32), sds((8, 2048), F32)],
        compiler_params=_params("arbitrary"), name=name)(dout, yr, ys, xc, proj, proj, proj, bg, dsk, sn, wr, ws, wo)


def _prescan_bwd(proj, dxc, dxs_skip, gdtx, dqr, dkr, cosf, sinf, cw, cb, dtb, eexp_t, dproj, name):
    s = proj.shape[0]
    ts = _row_tile(s, 256)
    nt = s // ts
    m = ts + 8
    width = C_DT + DT_PAD

    def body(xbc_ref, prev_ref, nxt_ref, dt_ref, dxc_ref, dxcn_ref, dsk_ref, dskn_ref, gdt_ref, dq_ref, dk_ref,
             cos_ref, sin_ref, cw_ref, cb_ref, dtb_ref, et_ref, dproj_in, dp_ref, dcw_ref, dcb_ref, ddtb_ref):
        i = pl.program_id(0)

        @pl.when(i == 0)
        def _():
            ddtb_ref[...] = jnp.zeros_like(ddtb_ref)
            dcw_ref[...] = jnp.zeros_like(dcw_ref)
            dcb_ref[...] = jnp.zeros_like(dcb_ref)

        rows = lax.broadcasted_iota(jnp.int32, (m, 128), 0)
        live = (rows < ts) | (i < nt - 1)
        for st in range(SSM_CONV_DIM // 128):
            sl = slice(st * 128, (st + 1) * 128)
            prev = jnp.where(i > 0, prev_ref[:, sl], 0.0)
            xcat = jnp.concatenate([prev, xbc_ref[:, sl], nxt_ref[:, sl]], axis=0)
            shifted = [pltpu.roll(xcat, 3 - j, 0) for j in range(3)] + [xcat]
            pre = cb_ref[:, sl]
            for j in range(SSM_CONV):
                pre = pre + cw_ref[j:j + 1, sl] * shifted[j][8:]
            _, dsilu = _silu_and_grad(pre)
            dxc = jnp.concatenate([dxc_ref[:, sl], dxcn_ref[:, sl]], axis=0)
            if st * 128 < SSM_INNER:
                dxc = dxc + jnp.concatenate([dsk_ref[:, sl], dskn_ref[:, sl]], axis=0)
            dpre = jnp.where(live, dxc * dsilu, 0.0)
            dpt = dpre[0:ts]
            dx = cw_ref[3:4, sl] * dpt
            for j in range(3):
                dx = dx + cw_ref[j:j + 1, sl] * pltpu.roll(dpre, m - (3 - j), 0)[0:ts]
            for j in range(SSM_CONV):
                dcw_ref[8 * j:8 * j + 8, sl] += _colsum(dpt * shifted[j][8:8 + ts])
            dcb_ref[:, sl] += _colsum(dpt)
            dp_ref[:, sl] = _bf(dx)
        cs, sn = cos_ref[...], sin_ref[...]
        for h in range(RET_HEADS):
            sl = slice(h * 128, (h + 1) * 128)
            dq = dq_ref[:, sl]
            dk = dk_ref[:, sl] * (RET_QK_DIM ** -0.5)
            dp_ref[:, C_Q + h * 128:C_Q + (h + 1) * 128] = _bf(dq * cs + pltpu.roll(dq * sn, 64, 1))
            dp_ref[:, C_K + h * 128:C_K + (h + 1) * 128] = _bf(dk * cs + pltpu.roll(dk * sn, 64, 1))
        ddt = _split3_dot(gdt_ref[...], et_ref[...])
        ddt = ddt * _sigmoid(dt_ref[:, 0:128] + dtb_ref[...])
        ddtb_ref[...] += _colsum(ddt)
        dp_ref[:, C_DT:C_DT + 128] = _bf(ddt)
        dp_ref[:, C_DT + 128:C_DT + DT_PAD] = jnp.zeros((ts, DT_PAD - 128), BF16)

    row = lambda w, c=0: pl.BlockSpec((ts, w), lambda i: (i, c))
    nxt = lambda w: pl.BlockSpec((8, w), lambda i: (jnp.minimum((i + 1) * (ts // 8), s // 8 - 1), 0))
    full = lambda a: pl.BlockSpec(a.shape, lambda i: (0,) * a.ndim)
    sds = jax.ShapeDtypeStruct
    return _pcall(
        body, grid=(nt,),
        in_specs=[row(4096), _prev_rows_spec(ts, 4096), nxt(4096), row(DT_PAD, C_DT // DT_PAD), row(4096), nxt(4096),
                  row(2048), nxt(2048), row(2048), row(512), row(512), row(128), row(128), full(cw), full(cb),
                  full(dtb), full(eexp_t), ANY],
        out_specs=[row(width), pl.BlockSpec((32, 4096), lambda i: (0, 0)), pl.BlockSpec((8, 4096), lambda i: (0, 0)),
                   pl.BlockSpec((8, 128), lambda i: (0, 0))],
        out_shape=[sds(dproj.shape, BF16), sds((32, 4096), F32), sds((8, 4096), F32), sds((8, 128), F32)],
        input_output_aliases={17: 0},
        compiler_params=_params("arbitrary"), name=name)(proj, proj, proj, proj, dxc, dxc, dxs_skip, dxs_skip, gdtx,
                                                          dqr, dkr, cosf, sinf, cw, cb, dtb, eexp_t, dproj)


def _xattn_values(x, gain, wq, kv):
    r = _rstd(x)
    h = (x * r) * gain
    q = _dot(h, wq)
    ps, os_ = [], []
    for hd in range(XA_HEADS):
        sl = slice(hd * XA_HEAD_DIM, (hd + 1) * XA_HEAD_DIM)
        sc = _dot_nt(q[:, sl], kv[:, sl]) * (XA_HEAD_DIM ** -0.5)
        e = jnp.exp(sc - jnp.max(sc, axis=-1, keepdims=True))
        p = e / jnp.sum(e, axis=-1, keepdims=True)
        ps.append(p)
        os_.append(_dot(p, kv[:, D_MODEL + hd * XA_HEAD_DIM:D_MODEL + (hd + 1) * XA_HEAD_DIM]))
    return r, h, q, ps, jnp.concatenate(os_, axis=1)


def _xattn_fwd(x, gain, wq, kv, wo, name):
    s = x.shape[0]
    ts = _row_tile(s, 256)

    def body(x_ref, g_ref, wq_ref, kv_ref, wo_ref, o_ref):
        x_ = x_ref[...]
        _, _, _, _, o = _xattn_values(x_, g_ref[...], wq_ref[...], kv_ref[...])
        o_ref[...] = x_ + _dot(o, wo_ref[...])

    row = pl.BlockSpec((ts, D_MODEL), lambda i: (i, 0))
    full = lambda a: pl.BlockSpec(a.shape, lambda i: (0,) * a.ndim)
    return _pcall(
        body, grid=(s // ts,), in_specs=[row, full(gain), full(wq), full(kv), full(wo)], out_specs=row,
        out_shape=jax.ShapeDtypeStruct((s, D_MODEL), F32), compiler_params=_params("parallel"), name=name)(
            x, gain, wq, kv, wo)


def _xattn_bwd(x, dout, gain, wq, kv, wo, name):
    s = x.shape[0]
    m = kv.shape[0]
    ts = _row_tile(s, 256)

    def body(x_ref, do_ref, g_ref, wq_ref, kv_ref, wo_ref, dx_ref, h_ref, dq_ref, o_ref, dkv_ref, dg_ref):
        @pl.when(pl.program_id(0) == 0)
        def _():
            dkv_ref[...] = jnp.zeros_like(dkv_ref)
            dg_ref[...] = jnp.zeros_like(dg_ref)

        x_, do, kvv = x_ref[...], do_ref[...], kv_ref[...]
        r, h, q, ps, o = _xattn_values(x_, g_ref[...], wq_ref[...], kvv)
        dov = _dot_nt(do, wo_ref[...])
        dqs = []
        for hd in range(XA_HEADS):
            sl = slice(hd * XA_HEAD_DIM, (hd + 1) * XA_HEAD_DIM)
            vl = slice(D_MODEL + hd * XA_HEAD_DIM, D_MODEL + (hd + 1) * XA_HEAD_DIM)
            p, doh = ps[hd], dov[:, sl]
            dp = _dot_nt(doh, kvv[:, vl])
            dsc = p * (dp - jnp.sum(dp * p, axis=-1, keepdims=True)) * (XA_HEAD_DIM ** -0.5)
            dqs.append(_dot(dsc, kvv[:, sl]))
            dkv_ref[:, sl] += _dot_tn(dsc, q[:, sl])
            dkv_ref[:, vl] += _dot_tn(p, doh)
        dq = jnp.concatenate(dqs, axis=1)
        dh = _dot_nt(dq, wq_ref[...])
        dg_ref[...] += _colsum(dh * (x_ * r))
        dx_ref[...] = do + _rms_bwd(dh * g_ref[...], x_, r)
        h_ref[...] = _bf(h)
        dq_ref[...] = _bf(dq)
        o_ref[...] = _bf(o)

    row = pl.BlockSpec((ts, D_MODEL), lambda i: (i, 0))
    full = lambda a: pl.BlockSpec(a.shape, lambda i: (0,) * a.ndim)
    sds = jax.ShapeDtypeStruct
    return _pcall(
        body, grid=(s // ts,), in_specs=[row, row, full(gain), full(wq), full(kv), full(wo)],
        out_specs=[row, row, row, row, pl.BlockSpec((m, 2 * D_MODEL), lambda i: (0, 0)),
                   pl.BlockSpec((8, D_MODEL), lambda i: (0, 0))],
        out_shape=[sds((s, D_MODEL), F32), sds((s, D_MODEL), BF16), sds((s, D_MODEL), BF16), sds((s, D_MODEL), BF16),
                   sds((m, 2 * D_MODEL), F32), sds((8, D_MODEL), F32)],
        compiler_params=_params("arbitrary"), name=name)(x, dout, gain, wq, kv, wo)


def _mem_bwd(mem, gain, dkv, wkv, name):
    m = mem.shape[0]

    def body(mem_ref, g_ref, dkv_ref, w_ref, mn_ref, dg_ref):
        mm = mem_ref[...]
        r = _rstd(mm)
        xh = mm * r
        mn_ref[...] = _bf(xh * g_ref[...])
        nb, _, wb = w_ref.shape
        dmn = _dot_nt(dkv_ref[:, 0:wb], w_ref[0])
        for j in range(1, nb):
            dmn = dmn + _dot_nt(dkv_ref[:, j * wb:(j + 1) * wb], w_ref[j])
        dg_ref[...] = jnp.zeros_like(dg_ref) + _colsum(dmn * xh)

    full = lambda a: pl.BlockSpec(a.shape, lambda: (0,) * a.ndim)
    return _pcall(
        body, in_specs=[full(mem), full(gain), full(dkv), full(wkv)],
        out_specs=[pl.BlockSpec((m, D_MODEL), lambda: (0, 0)), pl.BlockSpec((8, D_MODEL), lambda: (0, 0))],
        out_shape=[jax.ShapeDtypeStruct((m, D_MODEL), BF16), jax.ShapeDtypeStruct((8, D_MODEL), F32)],
        compiler_params=pltpu.CompilerParams(vmem_limit_bytes=VMEM_LIMIT), name=name)(mem, gain, dkv, wkv)


def _mlp_fwd(x, gain, w1, w2, name):
    s = x.shape[0]
    ts = _row_tile(s, 512)
    tf = 1024
    nf = D_FF // tf

    def body(x_ref, g_ref, w1_ref, w2_ref, o_ref, h_sc, acc):
        j = pl.program_id(1)

        @pl.when(j == 0)
        def _():
            xx = x_ref[...]
            h_sc[...] = _bf((xx * _rstd(xx)) * g_ref[...])
            acc[...] = jnp.zeros_like(acc)

        a = jnp.dot(h_sc[...], w1_ref[0], preferred_element_type=F32)
        r = jnp.square(jnp.maximum(a, 0.0))
        acc[...] += _dot(r, w2_ref[...])

        @pl.when(j == nf - 1)
        def _():
            o_ref[...] = x_ref[...] + acc[...]

    row = pl.BlockSpec((ts, D_MODEL), lambda i, j: (i, 0))
    return _pcall(
        body, grid=(s // ts, nf),
        in_specs=[row, pl.BlockSpec((1, D_MODEL), lambda i, j: (0, 0)),
                  pl.BlockSpec((1, D_MODEL, tf), lambda i, j: (j, 0, 0)), pl.BlockSpec((tf, D_MODEL), lambda i, j: (j, 0))],
        out_specs=row, out_shape=jax.ShapeDtypeStruct((s, D_MODEL), F32),
        scratch_shapes=[pltpu.VMEM((ts, D_MODEL), BF16), pltpu.VMEM((ts, D_MODEL), F32)],
        compiler_params=_params("parallel", "arbitrary"), name=name)(x, gain, w1, w2)


def _mlp_bwd(x, dout, gain, w1, w2, name):
    s = x.shape[0]
    ts = _row_tile(s, 512)
    tf = 1024
    nf = D_FF // tf

    def body(x_ref, do_ref, g_ref, w1_ref, w2_ref, dx_ref, h_ref, r_ref, da_ref, dg_ref, h_sc, do_sc, acc):
        i, j = pl.program_id(0), pl.program_id(1)

        @pl.when(j == 0)
        def _():
            xx = x_ref[...]
            h_sc[...] = _bf((xx * _rstd(xx)) * g_ref[...])
            do_sc[...] = _bf(do_ref[...])
            acc[...] = jnp.zeros_like(acc)
            h_ref[...] = h_sc[...]

        @pl.when((j == 0) & (i == 0))
        def _():
            dg_ref[...] = jnp.zeros_like(dg_ref)

        a = jnp.dot(h_sc[...], w1_ref[0], preferred_element_type=F32)
        ra = jnp.maximum(a, 0.0)
        r_ref[...] = _bf(ra * ra)
        dr = lax.dot_general(do_sc[...], w2_ref[...], (((1,), (1,)), ((), ())), preferred_element_type=F32)
        da = _bf(dr * 2.0 * ra)
        da_ref[...] = da
        acc[...] += lax.dot_general(da, w1_ref[0], (((1,), (1,)), ((), ())), preferred_element_type=F32)

        @pl.when(j == nf - 1)
        def _():
            xx = x_ref[...]
            r = _rstd(xx)
            dh = acc[...]
            dg_ref[...] += _colsum(dh * (xx * r))
            dx_ref[...] = do_ref[...] + _rms_bwd(dh * g_ref[...], xx, r)

    row = pl.BlockSpec((ts, D_MODEL), lambda i, j: (i, 0))
    ff = pl.BlockSpec((ts, tf), lambda i, j: (i, j))
    sds = jax.ShapeDtypeStruct
    return _pcall(
        body, grid=(s // ts, nf),
        in_specs=[row, row, pl.BlockSpec((1, D_MODEL), lambda i, j: (0, 0)),
                  pl.BlockSpec((1, D_MODEL, tf), lambda i, j: (j, 0, 0)), pl.BlockSpec((tf, D_MODEL), lambda i, j: (j, 0))],
        out_specs=[row, row, ff, ff, pl.BlockSpec((8, D_MODEL), lambda i, j: (0, 0))],
        out_shape=[sds((s, D_MODEL), F32), sds((s, D_MODEL), BF16), sds((s, D_FF), BF16), sds((s, D_FF), BF16),
                   sds((8, D_MODEL), F32)],
        scratch_shapes=[pltpu.VMEM((ts, D_MODEL), BF16), pltpu.VMEM((ts, D_MODEL), BF16), pltpu.VMEM((ts, D_MODEL), F32)],
        compiler_params=_params("arbitrary", "arbitrary"), name=name)(x, dout, gain, w1, w2)


def _final(x, gain, tgt, name):
    s = x.shape[0]
    ts = _row_tile(s, 512)

    def body(x_ref, g_ref, t_ref, dx_ref, loss_ref, dg_ref):
        @pl.when(pl.program_id(0) == 0)
        def _():
            loss_ref[...] = jnp.zeros_like(loss_ref)
            dg_ref[...] = jnp.zeros_like(dg_ref)

        xx = x_ref[...]
        r = _rstd(xx)
        xh = xx * r
        err = xh * g_ref[...] - t_ref[...]
        loss_ref[...] += 0.5 * jnp.sum(jnp.sum(err * err, axis=1, keepdims=True), axis=0, keepdims=True) / D_MODEL
        dy = err * (1.0 / D_MODEL)
        dg_ref[...] += _colsum(dy * xh)
        dx_ref[...] = _rms_bwd(dy * g_ref[...], xx, r)

    row = pl.BlockSpec((ts, D_MODEL), lambda i: (i, 0))
    return _pcall(
        body, grid=(s // ts,), in_specs=[row, pl.BlockSpec((1, D_MODEL), lambda i: (0, 0)), row],
        out_specs=[row, pl.BlockSpec((8, 128), lambda i: (0, 0)), pl.BlockSpec((8, D_MODEL), lambda i: (0, 0))],
        out_shape=[jax.ShapeDtypeStruct((s, D_MODEL), F32), jax.ShapeDtypeStruct((8, 128), F32),
                   jax.ShapeDtypeStruct((8, D_MODEL), F32)],
        compiler_params=_params("arbitrary"), name=name)(x, gain, tgt)


def _as3d(a):
    return a.reshape((-1,) + a.shape[-2:])


def _sum_cast(terms, out_dtype, name, row_want=256):
    shape = terms[0].shape
    t3 = [_as3d(t) for t in terms]
    b, r, c = t3[0].shape
    tr = _row_tile(r, row_want)

    def body(*refs):
        acc = refs[0][...].astype(F32)
        for t in refs[1:-1]:
            acc = acc + t[...].astype(F32)
        refs[-1][...] = acc.astype(out_dtype)

    spec = pl.BlockSpec((1, tr, c), lambda i, j: (i, j, 0))
    out = _pcall(body, grid=(b, r // tr), in_specs=[spec] * len(t3), out_specs=spec,
                 out_shape=jax.ShapeDtypeStruct((b, r, c), out_dtype), compiler_params=_params("parallel", "parallel"),
                 name=name)(*t3)
    return out.reshape(shape)


def _pair_sum(a, b, sel, half_id, out_dtype, name):
    _, h, c = b.shape
    k = sel.shape[0]
    tr = _row_tile(h, 256)
    nt = h // tr

    def body(sel_ref, hid_ref, a_ref, b_ref, o_ref):
        o_ref[...] = (a_ref[...] + b_ref[...]).astype(out_dtype)

    blkshape = (1, tr, c)
    grid_spec = pltpu.PrefetchScalarGridSpec(
        num_scalar_prefetch=2, grid=(k, nt),
        in_specs=[pl.BlockSpec(blkshape, lambda q, j, sel_ref, hid_ref: (sel_ref[q], hid_ref[0] * nt + j, 0)),
                  pl.BlockSpec(blkshape, lambda q, j, sel_ref, hid_ref: (sel_ref[q], j, 0))],
        out_specs=pl.BlockSpec(blkshape, lambda q, j, sel_ref, hid_ref: (q, j, 0)))
    return _pcall(body, grid_spec=grid_spec, out_shape=jax.ShapeDtypeStruct((k, h, c), out_dtype),
                  compiler_params=_params("parallel", "parallel"), name=name)(sel, half_id, a, b)


def _adamw(w, g, m, v, name):
    shape = w.shape
    w3, g3, m3, v3 = _as3d(w), _as3d(g), _as3d(m), _as3d(v)
    b, r, c = w3.shape
    if r % 256 == 0 or r <= 256:
        tr, tc = _row_tile(r, 256), c
    else:
        tr, tc = r, 128

    def body(w_ref, g_ref, m_ref, v_ref, d_ref, mo_ref, vo_ref):
        gg = g_ref[...]
        mn = ADAM_B1 * m_ref[...] + (1.0 - ADAM_B1) * gg
        vn = ADAM_B2 * v_ref[...] + (1.0 - ADAM_B2) * jnp.square(gg)
        m_hat = mn / (1.0 - ADAM_B1 ** ADAM_STEP)
        v_hat = vn / (1.0 - ADAM_B2 ** ADAM_STEP)
        d_ref[...] = -ADAM_LR * (m_hat / (jnp.sqrt(v_hat) + ADAM_EPS) + ADAM_WD * w_ref[...])
        mo_ref[...] = mn
        vo_ref[...] = vn

    spec = pl.BlockSpec((1, tr, tc), lambda i, j: (i, j // (c // tc), j % (c // tc)))
    sd = jax.ShapeDtypeStruct((b, r, c), F32)
    d, mo, vo = _pcall(body, grid=(b, (r // tr) * (c // tc)), in_specs=[spec] * 4, out_specs=[spec] * 3,
                       out_shape=[sd] * 3, compiler_params=_params("parallel", "parallel"), name=name)(w3, g3, m3, v3)
    return d.reshape(shape), mo.reshape(shape), vo.reshape(shape)


ANY = pl.BlockSpec(memory_space=pl.ANY)


def _place():
    return lax.axis_index("x"), lax.axis_index("y"), lax.axis_index("c")


def _flip(x, y, r):
    return (1 - x if r & 2 else x), (1 - y if r & 1 else y)


def _dma_sems(*counts):
    return [pltpu.SemaphoreType.DMA((k,)) for k in counts]


def _gather_ici(shards):
    n = len(shards)
    hs = [a.shape[0] // 2 for a in shards]

    def copies(ins, outs, sems, incoming):
        send, recv = sems
        x, y, c = _place()
        out = []
        for r in (1, 2, 3):
            cx, cy = _flip(x, y, r)
            for a in range(n):
                rows = pl.ds(c * hs[a], hs[a])
                block = 2 * cx + cy if incoming else 2 * x + y
                out.append(pltpu.make_async_remote_copy(
                    src_ref=ins[a].at[rows], dst_ref=outs[a].at[block, rows], send_sem=send.at[(r - 1) * n + a],
                    recv_sem=recv.at[(r - 1) * n + a], device_id=(cx, cy, c), device_id_type=MESH))
        return out

    def start(ins, outs, sems):
        for cp in copies(ins, outs, sems, False):
            cp.start()

    def finish(ins, outs, sems):
        for cp in copies(ins, outs, sems, True):
            cp.wait_recv()
        for cp in copies(ins, outs, sems, False):
            cp.wait_send()

    return _Comm(shards, [jax.ShapeDtypeStruct((4,) + a.shape, a.dtype) for a in shards], _dma_sems(3 * n, 3 * n),
                 start, finish)


def _gather_d2d(bufs, shards):
    n = len(bufs)
    hs = [a.shape[1] // 2 for a in bufs]

    def copies(ins, outs, sems, incoming):
        send, recv = sems
        x, y, c = _place()
        out = []
        for r in (1, 2, 3):
            cx, cy = _flip(x, y, r)
            for a in range(n):
                ref = outs[a].at[2 * cx + cy, pl.ds(((1 - c) if incoming else c) * hs[a], hs[a])]
                out.append(pltpu.make_async_remote_copy(
                    src_ref=ref, dst_ref=ref, send_sem=send.at[(r - 1) * n + a], recv_sem=recv.at[(r - 1) * n + a],
                    device_id=(x, y, 1 - c), device_id_type=MESH))
        for a in range(n):
            out.append(pltpu.make_async_remote_copy(
                src_ref=ins[n + a], dst_ref=outs[a].at[2 * x + y], send_sem=send.at[3 * n + a],
                recv_sem=recv.at[3 * n + a], device_id=(x, y, 1 - c), device_id_type=MESH))
        return out

    def start(ins, outs, sems):
        for cp in copies(ins, outs, sems, False):
            cp.start()

    def finish(ins, outs, sems):
        for cp in copies(ins, outs, sems, True):
            cp.wait_recv()
        for cp in copies(ins, outs, sems, False):
            cp.wait_send()

    return _Comm(list(bufs) + list(shards), [jax.ShapeDtypeStruct(a.shape, a.dtype) for a in bufs],
                 _dma_sems(4 * n, 4 * n), start, finish, aliases={a: a for a in range(n)})


def _swap_rows(packs):
    n = len(packs)
    hs = [a.shape[1] // 2 for a in packs]

    def copies(ins, outs, sems):
        send, recv = sems
        x, y, c = _place()
        return [pltpu.make_async_remote_copy(
            src_ref=ins[a].at[:, pl.ds((1 - c) * hs[a], hs[a])], dst_ref=outs[a], send_sem=send.at[a],
            recv_sem=recv.at[a], device_id=(x, y, 1 - c), device_id_type=MESH) for a in range(n)]

    def start(ins, outs, sems):
        for cp in copies(ins, outs, sems):
            cp.start()

    def finish(ins, outs, sems):
        for cp in copies(ins, outs, sems):
            cp.wait()

    return _Comm(packs, [jax.ShapeDtypeStruct((4, h, a.shape[2]), a.dtype) for a, h in zip(packs, hs)],
                 _dma_sems(n, n), start, finish)


def _exchange(arrs):
    n = len(arrs)

    def copies(ins, outs, sems):
        send, recv = sems
        x, y, c = _place()
        out = []
        for r in (1, 2, 3):
            cx, cy = _flip(x, y, r)
            for a in range(n):
                out.append(pltpu.make_async_remote_copy(
                    src_ref=ins[a].at[r - 1], dst_ref=outs[a].at[r - 1], send_sem=send.at[(r - 1) * n + a],
                    recv_sem=recv.at[(r - 1) * n + a], device_id=(cx, cy, c), device_id_type=MESH))
        return out

    def start(ins, outs, sems):
        for cp in copies(ins, outs, sems):
            cp.start()

    def finish(ins, outs, sems):
        for cp in copies(ins, outs, sems):
            cp.wait()

    return _Comm(arrs, [jax.ShapeDtypeStruct(a.shape, a.dtype) for a in arrs], _dma_sems(3 * n, 3 * n), start, finish)


def _to_sibling(arrs):
    n = len(arrs)

    def copies(ins, outs, sems):
        send, recv = sems
        x, y, c = _place()
        return [pltpu.make_async_remote_copy(
            src_ref=ins[a], dst_ref=outs[a], send_sem=send.at[a], recv_sem=recv.at[a],
            device_id=(x, y, 1 - c), device_id_type=MESH) for a in range(n)]

    def start(ins, outs, sems):
        for cp in copies(ins, outs, sems):
            cp.start()

    def finish(ins, outs, sems):
        for cp in copies(ins, outs, sems):
            cp.wait()

    return _Comm(arrs, [jax.ShapeDtypeStruct(a.shape, a.dtype) for a in arrs], _dma_sems(n, n), start, finish)


def _gather8(v, reduce, name):
    rows, w = v.shape

    def body(v_ref, out_ref, buf, send_sems, recv_sems):
        x, y, c = _place()
        me, sibling = (x, y, c), (x, y, 1 - c)
        chips = [_flip(x, y, r) for r in (1, 2, 3)]
        dst = out_ref if not reduce else buf

        def slot(px, py, pc):
            return dst.at[4 * px + 2 * py + pc]

        def copy(k, block, to, src=None):
            return pltpu.make_async_remote_copy(
                src_ref=slot(*block) if src is None else src, dst_ref=slot(*block), send_sem=send_sems.at[k],
                recv_sem=recv_sems.at[k], device_id=to, device_id_type=MESH)

        dst[4 * x + 2 * y + c] = v_ref[...]
        first = [copy(0, me, sibling, src=v_ref)]
        first += [copy(1 + j, me, (*chip, c), src=v_ref) for j, chip in enumerate(chips)]
        for cp in first:
            cp.start()
        passed = [copy(4 + j, (*chip, c), sibling) for j, chip in enumerate(chips)]
        for j, chip in enumerate(chips):
            copy(1 + j, (*chip, c), me).wait_recv()
            passed[j].start()
        copy(0, sibling, me).wait_recv()
        for j, chip in enumerate(chips):
            copy(4 + j, (*chip, 1 - c), me).wait_recv()
        for cp in first + passed:
            cp.wait_send()
        if reduce:
            acc = buf[0]
            for d in range(1, 8):
                acc = acc + buf[d]
            out_ref[...] = acc

    vm = pl.BlockSpec(memory_space=pltpu.VMEM)
    scratch = [pltpu.VMEM((8, rows, w) if reduce else (8, 8, 128), F32), pltpu.SemaphoreType.DMA((7,)),
               pltpu.SemaphoreType.DMA((7,))]
    out_shape = jax.ShapeDtypeStruct((rows, w) if reduce else (8, rows, w), F32)
    return _pcall(body, in_specs=[vm], out_specs=vm, out_shape=out_shape, scratch_shapes=scratch,
                  compiler_params=pltpu.CompilerParams(vmem_limit_bytes=VMEM_LIMIT), name=name)(v)


SMALL = [("norm_mix", 1024), ("b_gate", 2048), ("conv_b", 4096), ("dt_bias", 32), ("a_log", 32), ("d_skip", 32),
         ("ssm_norm", 2048), ("norm_xa", 1024), ("norm_mem", 1024), ("norm_mlp", 1024)]


def _rows_of(width):
    return max(1, width // 1024)


def _pack_rows(pieces):
    out = []
    for p in pieces:
        p = p.astype(F32)
        if p.shape[-1] < 1024:
            p = jnp.pad(p, ((0, 0), (0, 1024 - p.shape[-1])))
        out.append(p.reshape(-1, 1024))
    cat = jnp.concatenate(out, axis=0)
    pad = (-cat.shape[0]) % 8
    return jnp.pad(cat, ((0, pad), (0, 0))) if pad else cat


def _unpack_rows(packed, widths_rows):
    out, at = [], 0
    for r, w in widths_rows:
        k = r * _rows_of(w)
        p = packed[at:at + k]
        at += k
        out.append(p[:, :w] if w < 1024 else p.reshape(r, w))
    return out


def _to_cat(w):
    pieces = [w[:, O_XBC:O_XBC + 4096], w[:, O_Q:O_Q + 512], w[:, O_K:O_K + 512], w[:, O_DT:O_DT + 32],
              jnp.zeros((w.shape[0], DT_PAD - 32), w.dtype), w[:, O_Z:O_Z + 2048], w[:, O_GATES:O_GATES + 2048],
              w[:, O_V:O_V + 1024], w[:, O_G:O_G + 1024]]
    return jnp.concatenate(pieces, axis=1)


def _from_cat(g):
    pieces = [g[:, C_Q:C_Q + 512], g[:, C_K:C_K + 512], g[:, C_V:C_V + 1024], g[:, C_G:C_G + 1024],
              g[:, C_Z:C_Z + 2048], g[:, C_XBC:C_XBC + 4096], g[:, C_DT:C_DT + 32], g[:, C_GATES:C_GATES + 2048]]
    return jnp.concatenate(pieces, axis=1)


PACK_ROWS = [("mlp_w1", 1024), ("mlp_w2", 1024), ("w_br_ssm", 512), ("w_br_ret", 256), ("w_out", 256), ("xa_wq", 256),
             ("xa_wo", 256)]
PACK_N = sum(r for _, r in PACK_ROWS)


def _blocks_rows(g, n):
    return g.reshape(4, n, g.shape[-1])


def _blocks_cols(g, n):
    return g.reshape(g.shape[0], 4, n).transpose(1, 0, 2)


def kernel(x, mem, positions, norm_mix, w_in, b_gate, conv_w, conv_b, dt_bias, a_log, d_skip, ssm_norm, w_br_ret, w_br_ssm, w_out, norm_xa, norm_mem, xa_wq, xa_wkv, xa_wo, norm_mlp, mlp_w1, mlp_w2, norm_final, loss_target, m_norm_mix, m_w_in, m_b_gate, m_conv_w, m_conv_b, m_dt_bias, m_a_log, m_d_skip, m_ssm_norm, m_w_br_ret, m_w_br_ssm, m_w_out, m_norm_xa, m_norm_mem, m_xa_wq, m_xa_wkv, m_xa_wo, m_norm_mlp, m_mlp_w1, m_mlp_w2, m_norm_final, v_norm_mix, v_w_in, v_b_gate, v_conv_w, v_conv_b, v_dt_bias, v_a_log, v_d_skip, v_ssm_norm, v_w_br_ret, v_w_br_ssm, v_w_out, v_norm_xa, v_norm_mem, v_xa_wq, v_xa_wkv, v_xa_wo, v_norm_mlp, v_mlp_w1, v_mlp_w2, v_norm_final):
    W = dict(norm_mix=norm_mix, w_in=w_in, b_gate=b_gate, conv_w=conv_w, conv_b=conv_b, dt_bias=dt_bias, a_log=a_log,
             d_skip=d_skip, ssm_norm=ssm_norm, w_br_ret=w_br_ret, w_br_ssm=w_br_ssm, w_out=w_out, norm_xa=norm_xa,
             norm_mem=norm_mem, xa_wq=xa_wq, xa_wkv=xa_wkv, xa_wo=xa_wo, norm_mlp=norm_mlp, mlp_w1=mlp_w1,
             mlp_w2=mlp_w2, norm_final=norm_final)
    M = dict(norm_mix=m_norm_mix, w_in=m_w_in, b_gate=m_b_gate, conv_w=m_conv_w, conv_b=m_conv_b, dt_bias=m_dt_bias,
             a_log=m_a_log, d_skip=m_d_skip, ssm_norm=m_ssm_norm, w_br_ret=m_w_br_ret, w_br_ssm=m_w_br_ssm,
             w_out=m_w_out, norm_xa=m_norm_xa, norm_mem=m_norm_mem, xa_wq=m_xa_wq, xa_wkv=m_xa_wkv, xa_wo=m_xa_wo,
             norm_mlp=m_norm_mlp, mlp_w1=m_mlp_w1, mlp_w2=m_mlp_w2, norm_final=m_norm_final)
    V = dict(norm_mix=v_norm_mix, w_in=v_w_in, b_gate=v_b_gate, conv_w=v_conv_w, conv_b=v_conv_b, dt_bias=v_dt_bias,
             a_log=v_a_log, d_skip=v_d_skip, ssm_norm=v_ssm_norm, w_br_ret=v_w_br_ret, w_br_ssm=v_w_br_ssm,
             w_out=v_w_out, norm_xa=v_norm_xa, norm_mem=v_norm_mem, xa_wq=v_xa_wq, xa_wkv=v_xa_wkv, xa_wo=v_xa_wo,
             norm_mlp=v_norm_mlp, mlp_w1=v_mlp_w1, mlp_w2=v_mlp_w2, norm_final=v_norm_final)
    nl = w_in.shape[0]
    s = x.shape[1]
    x0 = x[0]
    mem2 = mem[0]
    tgt = loss_target[0]
    blk = 2 * lax.axis_index("x") + lax.axis_index("y")

    groups = [["w_in"], ["xa_wkv", "w_br_ret", "w_br_ssm", "w_out", "xa_wq", "xa_wo"], ["mlp_w1", "mlp_w2"]]
    big = groups[0] + groups[1] + groups[2]
    blk = blk.astype(jnp.int32)
    wb = {k: W[k].astype(BF16) for k in big}

    def shards_of(l, ks):
        return [wb[k][l] for k in ks]

    landed = _run_comm(_gather_ici(shards_of(0, big)), "gather_weights")
    gl = dict(zip(big, _run_comm(_gather_d2d(landed, shards_of(0, big)), "gather_weights_cores")))
    cw_all = _gather8(conv_w.reshape(nl * SSM_CONV, 1024), False, "gather_conv_w")
    cw_full = cw_all.reshape(4, 2, nl, SSM_CONV, 1024)[:, 0].transpose(1, 2, 0, 3).reshape(nl, SSM_CONV, SSM_CONV_DIM)

    offs = {}
    at = 0
    for k, r in PACK_ROWS:
        offs[k] = (at, r)
        at += r


    inv_freq = ROPE_THETA ** (-jnp.arange(0, RET_QK_DIM, 2, dtype=F32) / RET_QK_DIM)
    ang = positions.astype(F32)[0][:, None] * inv_freq
    cos, sin = jnp.cos(ang), jnp.sin(ang)
    cosf = jnp.concatenate([cos, cos], axis=1)
    sinf = jnp.concatenate([-sin, sin], axis=1)
    dm, qd, kd, cd = (jnp.asarray(c) for c in _ret_constants())
    eye, blkm = (jnp.asarray(c) for c in _ssd_constants())
    consts = (dm, qd, kd, cd, eye, blkm)
    e_np = _head_expand()
    eexp = jnp.asarray(e_np, BF16)
    eexp_t = jnp.asarray(e_np.T.copy(), BF16)

    saved = []
    xcur = x0
    for l in range(nl):
        rows_weight = lambda k: gl[k].reshape(-1, D_MODEL)
        wcat = _to_cat(jnp.concatenate([gl["w_in"][j] for j in range(4)], axis=1))
        wr, ws, wo = rows_weight("w_br_ret"), rows_weight("w_br_ssm"), rows_weight("w_out")
        wq, wxo, w2 = rows_weight("xa_wq"), rows_weight("xa_wo"), rows_weight("mlp_w2")
        w1, wkv = gl["mlp_w1"], gl["xa_wkv"]
        cw, cb = cw_full[l], conv_b[l][None]
        dtb = jnp.pad(dt_bias[l], (0, 128 - SSM_HEADS))[None]
        ax = jnp.repeat(-jnp.exp(a_log[l]), 64)[None]
        dsk = jnp.repeat(d_skip[l], 64)[None]
        bg, sn = b_gate[l][None], ssm_norm[l][None]
        more = l + 1 < nl
        ici = [_gather_ici(shards_of(l + 1, ks)) for ks in groups] if more else [None] * 3
        _carry(ici[0])
        proj, u = _nmm(xcur, norm_mix[l][None], wcat, "in_proj", save_u=True)
        _carry(ici[1])
        qr, kr, xc, dtx = _prescan(proj, cosf, sinf, cw, cb, dtb, eexp, "prescan")
        _carry(ici[2])
        yr, ys, sst, hst = _scan_fwd(qr, kr, proj, xc, dtx, ax, consts, "scan_fwd")
        x1 = _postscan_fwd(xcur, yr, ys, xc, proj, bg, dsk, sn, wr, ws, wo, "postscan")
        kv = _bf(_nmm(mem2, norm_mem[l][None], wkv, "mem_kv"))
        x2 = _xattn_fwd(x1, norm_xa[l][None], wq, kv, wxo, "xattn")
        cores = (_gather_d2d(ici[0].results + ici[1].results + ici[2].results, shards_of(l + 1, big))
                 if more else None)
        _carry(cores)
        x3 = _mlp_fwd(x2, norm_mlp[l][None], w1, w2, "mlp")
        if more:
            gl = dict(zip(big, cores.results))
        saved.append(dict(x0=xcur, x1=x1, x2=x2, proj=proj, u=u, qr=qr, kr=kr, xc=xc, dtx=dtx, yr=yr, ys=ys, sst=sst,
                          hst=hst, kv=kv, wcat=wcat, wr=wr, ws=ws, wo=wo, wq=wq, wxo=wxo, w1=w1, w2=w2, wkv=wkv, cw=cw,
                          cb=cb, dtb=dtb, ax=ax, dsk=dsk, bg=bg, sn=sn))
        xcur = x3

    dx, loss_acc, dnf = _final(xcur, norm_final[None], tgt, "final")
    loss = lax.psum(loss_acc[0, 0], ("x", "y", "c"))

    small_g = [None] * nl
    c = lax.axis_index("c")
    half_id = c.astype(jnp.int32)[None]
    sel_own = blk[None]
    sel_rem = jnp.stack([blk ^ 1, blk ^ 2, blk ^ 3])
    layer_grads = {k: [None] * nl for k in big}

    def pair_sums(packs, got):
        own = [_pair_sum(p, g_, sel_own, half_id, F32, "chip_sum_own")[0] for p, g_ in zip(packs, got)]
        out_b = [_pair_sum(p, g_, sel_rem, half_id, BF16, "chip_sum_send") for p, g_ in zip(packs, got)]
        return own, out_b

    def totals(own, inc):
        return [_sum_cast([o, i_[0], i_[1], i_[2]], F32, "grads_total") for o, i_ in zip(own, inc)]

    def finish_layer(lr, red_half, sib_half):
        def whole(i):
            mine_, theirs_ = red_half[i], sib_half[i]
            return jnp.concatenate([jnp.where(c == 0, mine_, theirs_), jnp.where(c == 0, theirs_, mine_)], axis=0)

        full1 = whole(0)
        for k, r in PACK_ROWS:
            layer_grads[k][lr] = full1[offs[k][0]:offs[k][0] + r]
        layer_grads["w_in"][lr] = whole(1)
        layer_grads["xa_wkv"][lr] = whole(2)

    riding = None
    for l in reversed(range(nl)):
        sv = saved[l]
        swap = _swap_rows(riding[1]) if riding else None
        _carry(swap)
        dx2, hm, rm, dam, dg_mlp = _mlp_bwd(sv["x2"], dx, norm_mlp[l][None], sv["w1"], sv["w2"], "mlp_bwd")
        if riding:
            own, out_b = pair_sums(riding[1], swap.results)
        pack = _mm_tn_into(hm, dam, "dw_mlp1", None, offs["mlp_w1"][0], True)
        pack = _mm_tn_into(rm, dx, "dw_mlp2", pack, offs["mlp_w2"][0], False)
        dx1, hx, dqx, ox, dkv, dg_xa = _xattn_bwd(sv["x1"], dx2, norm_xa[l][None], sv["wq"], sv["kv"], sv["wxo"],
                                                  "xattn_bwd")
        pack = _mm_tn_into(hx, dqx, "dw_xq", pack, offs["xa_wq"][0], False)
        pack = _mm_tn_into(ox, dx2, "dw_xo", pack, offs["xa_wo"][0], False)
        memn, dg_mem = _mem_bwd(mem2, norm_mem[l][None], dkv, sv["wkv"], "mem_bwd")
        dwkv = _mm_tn(memn, dkv, "dw_xkv", col_blocks=4)
        chips_a = _exchange(out_b[0:1]) if riding else None
        _carry(chips_a)
        (dyr, dys, dxs_skip, dproj, yrn, ysn, mg, dbr, dbs, dbg, ddsk, dsn) = _postscan_bwd(
            dx1, sv["yr"], sv["ys"], sv["xc"], sv["proj"], sv["bg"], sv["dsk"], sv["sn"], sv["wr"], sv["ws"], sv["wo"],
            "postscan_bwd")
        pack = _mm_tn_into(mg, dx1, "dw_out", pack, offs["w_out"][0], False)
        pack = _mm_tn_into(yrn, dbr, "dw_br_ret", pack, offs["w_br_ret"][0], False)
        pack = _mm_tn_into(ysn, dbs, "dw_br_ssm", pack, offs["w_br_ssm"][0], False)
        chips_b = _exchange(out_b[1:3]) if riding else None
        _carry(chips_b)
        dqr, dkr, dproj, dxc, gdtx, da_cols = _scan_bwd(sv["qr"], sv["kr"], sv["proj"], sv["xc"], sv["dtx"], sv["ax"],
                                                        consts, sv["sst"], sv["hst"], dyr, dys, dproj, "scan_bwd")
        if riding:
            red_half = totals(own, chips_a.results + chips_b.results)
        cores = _to_sibling(red_half) if riding else None
        _carry(cores)
        dproj, dcw, dcb, ddtb = _prescan_bwd(sv["proj"], dxc, dxs_skip, gdtx, dqr, dkr, cosf, sinf, sv["cw"], sv["cb"],
                                             sv["dtb"], eexp_t, dproj, "prescan_bwd")
        if riding:
            finish_layer(riding[0], red_half, cores.results)
        dwcat = _mm_tn(sv["u"], dproj, "dw_in")
        dx, dg_mix = _in_bwd(dproj, sv["wcat"], sv["x0"], norm_mix[l][None], dx1, "in_bwd")

        da_log = (da_cols.reshape(SSM_HEADS, 64).sum(axis=1)) * (-jnp.exp(a_log[l]))
        dd_skip = ddsk[0].reshape(SSM_HEADS, 64).sum(axis=1)
        small_g[l] = [dg_mix[0:1], dbg[0:1], dcb[0:1], ddtb[0:1, :SSM_HEADS], da_log[None], dd_skip[None], dsn[0:1],
                      dg_xa[0:1], dg_mem[0:1], dg_mlp[0:1], dcw[0::8]]
        riding = (l, [pack, _blocks_cols(_from_cat(dwcat), IN_DIM // 4), dwkv])

    got = _run_comm(_swap_rows(riding[1]), "grads_core_swap")
    own, out_b = pair_sums(riding[1], got)
    red_half = totals(own, _run_comm(_exchange(out_b), "grads_chip_exchange"))
    finish_layer(riding[0], red_half, _run_comm(_to_sibling(red_half), "grads_core_join"))
    grad_x = dx[None]

    pieces = []
    for l in range(nl):
        pieces += small_g[l]
    pieces.append(dnf[0:1])
    small_sum = _gather8(_pack_rows(pieces), True, "reduce_small")
    layout = []
    for l in range(nl):
        layout += [(1, w) for _, w in SMALL] + [(SSM_CONV, SSM_CONV_DIM)]
    layout.append((1, 1024))
    red = _unpack_rows(small_sum, layout)
    per = len(SMALL) + 1
    g_small = {k: jnp.concatenate([red[l * per + i] for l in range(nl)], axis=0) for i, (k, _) in enumerate(SMALL)}
    g_convw_full = jnp.stack([red[l * per + len(SMALL)] for l in range(nl)])
    g_small["conv_w"] = lax.dynamic_slice_in_dim(g_convw_full, blk * 1024, 1024, axis=2)
    g_small["norm_final"] = red[-1][0]

    grads = dict(g_small)
    for k in big:
        grads[k] = jnp.stack(layer_grads[k])

    delta, new_m, new_v = {}, {}, {}
    for k in ["xa_wkv"] + [k for k, _ in PACK_ROWS]:
        delta[k], new_m[k], new_v[k] = _adamw(W[k], grads[k], M[k], V[k], "adamw_" + k)
    tr_ = lambda a: jnp.swapaxes(a, 1, 2)
    g_in_t = tr_(grads["w_in"])
    grads["w_in"] = tr_(g_in_t)
    d_t, m_t, v_t = _adamw(tr_(w_in), g_in_t, tr_(m_w_in), tr_(v_w_in), "adamw_w_in")
    delta["w_in"], new_m["w_in"], new_v["w_in"] = tr_(d_t), tr_(m_t), tr_(v_t)
    small_names = [k for k, _ in SMALL] + ["conv_w", "norm_final"]

    def pack_small(src):
        ps = []
        for k in small_names:
            a = src[k]
            ps.append(a.reshape(-1, a.shape[-1]) if a.ndim > 1 else a[None])
        return _pack_rows(ps)

    ds_, ms_, vs_ = _adamw(pack_small(W), pack_small(grads), pack_small(M), pack_small(V), "adamw_small")
    lay2 = []
    for k in small_names:
        a = W[k]
        lay2.append((int(np.prod(a.shape[:-1])) if a.ndim > 1 else 1, a.shape[-1]))
    for src, dst in ((ds_, delta), (ms_, new_m), (vs_, new_v)):
        for k, piece in zip(small_names, _unpack_rows(src, lay2)):
            dst[k] = piece.reshape(W[k].shape)

    names = ["norm_mix", "w_in", "b_gate", "conv_w", "conv_b", "dt_bias", "a_log", "d_skip", "ssm_norm", "w_br_ret",
             "w_br_ssm", "w_out", "norm_xa", "norm_mem", "xa_wq", "xa_wkv", "xa_wo", "norm_mlp", "mlp_w1", "mlp_w2",
             "norm_final"]
    return (loss, grad_x, *[grads[n] for n in names], *[delta[n] for n in names], *[new_m[n] for n in names],
            *[new_v[n] for n in names])
```

```python
import numpy as np
import jax
import jax.numpy as jnp
from jax import lax
from jax.experimental import pallas as pl
from jax.experimental.pallas import tpu as pltpu

F32 = jnp.float32
BF16 = jnp.bfloat16
MESH = pl.DeviceIdType.MESH

D_MODEL = 1024
CHUNK = 64
EPS = 1e-6
RET_HEADS = 4
RET_QK_DIM = 128
RET_V_DIM = 256
RET_QK = 512
RET_V = 1024
ROPE_THETA = 10000.0
SSM_INNER = 2048
SSM_HEADS = 32
SSM_GROUPS = 8
SSM_STATE = 128
SSM_CONV = 4
SSM_BC = 1024
SSM_CONV_DIM = 4096
XA_HEADS = 4
XA_HEAD_DIM = 256
D_FF = 4096
GROUP_W = 256

DT_PAD = 1024
IN_DIM = 11296
NP = 12288
C_XBC, C_Q, C_K, C_DT, C_Z, C_GATES, C_V, C_G = 0, 4096, 4608, 5120, 6144, 8192, 10240, 11264
O_Q, O_K, O_V, O_G, O_Z, O_XBC, O_DT, O_GATES = 0, 512, 1024, 2048, 3072, 5120, 9216, 9248

ADAM_LR = 0.001
ADAM_B1 = 0.9
ADAM_B2 = 0.999
ADAM_EPS = 1e-08
ADAM_WD = 0.01
ADAM_STEP = 10

VMEM_LIMIT = 56 * 1024 * 1024


def _params(*sem):
    return pltpu.CompilerParams(dimension_semantics=sem, vmem_limit_bytes=VMEM_LIMIT)


_CARRY = []


def _carry(comm):
    if comm is not None:
        _CARRY.append(comm)


def _pcall(body, **kw):
    if _CARRY:
        return _hosted(body, _CARRY.pop(), kw)
    return pl.pallas_call(body, **kw)


class _Comm:
    def __init__(self, ins, out_shapes, sems, start, finish, aliases=None):
        self.ins, self.out_shapes, self.sems = list(ins), list(out_shapes), list(sems)
        self.start, self.finish, self.aliases = start, finish, dict(aliases or {})
        self.results = None


def _hosted(body, comm, kw):
    in_specs = list(kw.pop("in_specs"))
    out_specs, out_shape = kw.pop("out_specs"), kw.pop("out_shape")
    single = not isinstance(out_shape, (list, tuple))
    if single:
        out_specs, out_shape = [out_specs], [out_shape]
    out_specs, out_shape = list(out_specs), list(out_shape)
    scratch = list(kw.pop("scratch_shapes", []))
    grid = tuple(kw.get("grid", ()))
    aliases = dict(kw.pop("input_output_aliases", {}))
    n_in, n_out, n_sc = len(in_specs), len(out_shape), len(scratch)
    c_in, c_out = len(comm.ins), len(comm.out_shapes)
    for i, o in comm.aliases.items():
        aliases[n_in + i] = n_out + o
    kw["compiler_params"] = _params(*(["arbitrary"] * len(grid)))

    def wrapped(*refs):
        at = 0
        parts = []
        for cnt in (n_in, c_in, n_out, c_out, n_sc):
            parts.append(refs[at:at + cnt])
            at += cnt
        a, ci, b, co, s = parts
        cs = refs[at:]
        first, last = None, None
        for d, size in enumerate(grid):
            f, l_ = pl.program_id(d) == 0, pl.program_id(d) == size - 1
            first = f if first is None else first & f
            last = l_ if last is None else last & l_

        @pl.when(first)
        def _():
            comm.start(ci, co, cs)

        body(*a, *b, *s)

        @pl.when(last)
        def _():
            comm.finish(ci, co, cs)

    call = _pcall(wrapped, in_specs=in_specs + [ANY] * c_in, out_specs=out_specs + [ANY] * c_out,
                  out_shape=out_shape + comm.out_shapes, scratch_shapes=scratch + comm.sems,
                  input_output_aliases=aliases, **kw)

    def run(*ops):
        res = call(*ops, *comm.ins)
        comm.results = list(res[n_out:])
        return res[0] if single else list(res[:n_out])

    return run


def _run_comm(comm, name):
    def body(*refs):
        c_in, c_out = len(comm.ins), len(comm.out_shapes)
        ci, co, cs = refs[:c_in], refs[c_in:c_in + c_out], refs[c_in + c_out:]
        comm.start(ci, co, cs)
        comm.finish(ci, co, cs)

    aliases = {i: o for i, o in comm.aliases.items()}
    res = _pcall(body, in_specs=[ANY] * len(comm.ins), out_specs=[ANY] * len(comm.out_shapes),
                 out_shape=comm.out_shapes, scratch_shapes=comm.sems, input_output_aliases=aliases, name=name)(*comm.ins)
    comm.results = list(res)
    return comm.results


def _bf(a):
    return a.astype(BF16)


def _dot(a, b):
    return jnp.dot(_bf(a), _bf(b), preferred_element_type=F32)


def _dot_nt(a, b):
    return lax.dot_general(_bf(a), _bf(b), (((1,), (1,)), ((), ())), preferred_element_type=F32)


def _dot_tn(a, b):
    return lax.dot_general(_bf(a), _bf(b), (((0,), (0,)), ((), ())), preferred_element_type=F32)


def _colsum(a):
    return jnp.sum(a, axis=0, keepdims=True)


def _rstd(x):
    return lax.rsqrt(jnp.mean(x * x, axis=-1, keepdims=True) + EPS)


def _rms_bwd(dy, x, rstd):
    xh = x * rstd
    return rstd * (dy - xh * jnp.mean(dy * xh, axis=-1, keepdims=True))


def _sigmoid(x):
    return 1.0 / (1.0 + jnp.exp(-x))


def _silu_and_grad(x):
    s = _sigmoid(x)
    return x * s, s + x * s * (1.0 - s)


def _softplus(x):
    u = jnp.exp(-jnp.abs(x))
    l1p = jnp.where(u < 1e-4, u * (1.0 - 0.5 * u), jnp.log(1.0 + u))
    return jnp.maximum(x, 0.0) + l1p


def _split3_dot(a, e):
    hi = a.astype(BF16)
    r1 = a - hi.astype(F32)
    mid = r1.astype(BF16)
    lo = (r1 - mid.astype(F32)).astype(BF16)
    return (jnp.dot(hi, e, preferred_element_type=F32) + jnp.dot(mid, e, preferred_element_type=F32)
            + jnp.dot(lo, e, preferred_element_type=F32))


def _cumsum_rows(a):
    rows = lax.broadcasted_iota(jnp.int32, a.shape, 0)
    s = 1
    while s < a.shape[0]:
        a = a + jnp.where(rows >= s, pltpu.roll(a, s, 0), 0.0)
        s *= 2
    return a


def _revcumsum_rows(a):
    n = a.shape[0]
    rows = lax.broadcasted_iota(jnp.int32, a.shape, 0)
    s = 1
    while s < n:
        a = a + jnp.where(rows < n - s, pltpu.roll(a, n - s, 0), 0.0)
        s *= 2
    return a


def _rms_groups(y, width):
    out = []
    for h in range(y.shape[1] // width):
        slab = y[:, h * width:(h + 1) * width]
        out.append((slab, _rstd(slab)))
    return out


def _ret_constants():
    idx = np.arange(CHUNK, dtype=np.float32)
    lg = np.log1p(-(np.float32(2.0) ** (np.float32(-5.0) - np.arange(RET_HEADS, dtype=np.float32)))).astype(np.float32)
    rel = np.abs(idx[:, None] - idx[None, :])
    dm = np.exp(lg[:, None, None] * rel).astype(np.float32)
    qd = np.exp(lg[None, :] * (idx[:, None] + 1.0)).astype(np.float32)
    kd = np.exp(lg[None, :] * (CHUNK - 1.0 - idx[:, None])).astype(np.float32)
    cd = np.exp(lg * CHUNK).astype(np.float32)
    qd = np.repeat(qd, RET_QK_DIM, axis=1)
    kd = np.repeat(kd, RET_QK_DIM, axis=1)
    cd = np.repeat(cd, RET_QK_DIM)[:, None] * np.ones((1, RET_V_DIM), np.float32)
    return dm, qd, kd, cd.astype(np.float32)


def _ssd_constants():
    eye = np.tile(np.eye(CHUNK, dtype=np.float32), (1, GROUP_W // CHUNK))
    blk = np.kron(np.eye(GROUP_W // CHUNK, dtype=np.float32), np.ones((CHUNK, CHUNK), np.float32))
    return eye, blk


def _head_expand():
    e = np.zeros((128, SSM_INNER), np.float32)
    for h in range(SSM_HEADS):
        e[h, h * 64:(h + 1) * 64] = 1.0
    return e


def _ret_chunk_fwd(qh, kh, vh, sh, dmh, qdh, kdh, cdh):
    a = _dot_nt(qh, kh) * dmh
    y = _dot(a, vh) + _dot(qh * qdh, sh)
    s_new = sh * cdh + _dot_tn(kh * kdh, vh)
    return y, s_new


def _ret_chunk_bwd(qh, kh, vh, sh, dmh, qdh, kdh, cdh, dy, ds_new):
    a = _dot_nt(qh, kh) * dmh
    dp = _dot_nt(dy, vh) * dmh
    dq = _dot(dp, kh) + _dot_nt(dy, sh) * qdh
    dk = _dot_tn(dp, qh) + _dot_nt(vh, ds_new) * kdh
    dv = _dot_tn(a, dy) + _dot(kh * kdh, ds_new)
    ds = cdh * ds_new + _dot_tn(qh * qdh, dy)
    return dq, dk, dv, ds


def _ssd_common(xs, dtx, ax, eye):
    cum = _cumsum_rows(dtx * ax)
    last = cum[CHUNK - 1:CHUNK, :]
    r = _colsum(jnp.where(eye > 0.5, cum, 0.0))
    return cum, last, r, xs * dtx


def _tile4(a):
    return jnp.concatenate([a, a, a, a], axis=0)


def _ssd_chunk_fwd(xs, dtx, b, c, ax, hg, eye, blk):
    cum, last, r, x = _ssd_common(xs, dtx, ax, eye)
    lam = jnp.exp(-jnp.abs(cum - r))
    wc = _dot_nt(c, _tile4(b)) * lam
    bd = _tile4(x) * blk
    y = _dot(wc, bd) + _dot(c, hg) * jnp.exp(cum)
    h_new = hg * jnp.exp(last) + _dot_tn(b, x * jnp.exp(last - cum))
    return y, h_new


def _ssd_chunk_bwd(xs, dtx, b, c, ax, hg, eye, blk, dy, dh_new):
    cum, last, r, x = _ssd_common(xs, dtx, ax, eye)
    delta = cum - r
    lam = jnp.exp(-jnp.abs(delta))
    b4 = _tile4(b)
    cb4 = _dot_nt(c, b4)
    wc = cb4 * lam
    bd = _tile4(x) * blk
    ecx = jnp.exp(cum)
    wl = jnp.exp(last - cum)
    ecl = jnp.exp(last)
    z = _dot(c, hg)
    dwc = _dot_nt(dy, bd)
    dbd = _dot_tn(wc, dy) * blk
    dx = dbd[0:64] + dbd[64:128] + dbd[128:192] + dbd[192:256]
    dt_ = _dot(b, dh_new)
    dx = dx + dt_ * wl
    dcb4 = dwc * lam
    dz = dy * ecx
    dc = _dot(dcb4, b4) + _dot_nt(dz, hg)
    db4 = _dot_tn(dcb4, c)
    db = db4[0:64] + db4[64:128] + db4[128:192] + db4[192:256] + _dot_nt(x * wl, dh_new)
    g = dwc * cb4 * lam * (-jnp.sign(delta))
    dr = -_colsum(g)
    dwl = dt_ * x * wl
    u = g + eye * dr + dy * z * ecx - dwl
    lastrow = _colsum(dwl) + _colsum(dh_new * hg) * ecl
    rows = lax.broadcasted_iota(jnp.int32, u.shape, 0)
    u = u + jnp.where(rows == CHUNK - 1, lastrow, 0.0)
    dh = _dot_tn(c, dz) + dh_new * ecl
    rc = _revcumsum_rows(u)
    dxs = dx * dtx
    g_dtx = dx * xs + rc * ax
    da = _colsum(rc * dtx)
    return dxs, g_dtx, db, dc, da, dh


def _row_tile(s, want):
    t = min(s, want)
    assert s % t == 0
    return t


def _nmm(x, gain, w, name, tn=1024, save_u=False, w_rows=False):
    s, d = x.shape
    blocked = w.ndim == 3
    if blocked:
        tn = w.shape[2]
        n = w.shape[0] * tn
        w_spec = pl.BlockSpec((1, d, tn), lambda i, j: (j, 0, 0))
    elif w_rows:
        n = w.shape[0]
        w_spec = pl.BlockSpec((tn, d), lambda i, j: (j, 0))
    else:
        n = w.shape[1]
        w_spec = pl.BlockSpec((d, tn), lambda i, j: (0, j))
    tm = _row_tile(s, 1024)
    assert n % tn == 0

    def body(x_ref, g_ref, w_ref, *rest):
        o_ref, u_sc = rest[0], rest[-1]

        @pl.when(pl.program_id(1) == 0)
        def _():
            xx = x_ref[...]
            u = _bf((xx * _rstd(xx)) * g_ref[...])
            u_sc[...] = u
            if save_u:
                rest[1][...] = u

        if w_rows:
            o_ref[...] = _dot_nt(u_sc[...], w_ref[...])
        else:
            o_ref[...] = jnp.dot(u_sc[...], w_ref[0] if blocked else w_ref[...], preferred_element_type=F32)

    out_shape = [jax.ShapeDtypeStruct((s, n), F32)]
    out_specs = [pl.BlockSpec((tm, tn), lambda i, j: (i, j))]
    if save_u:
        out_shape.append(jax.ShapeDtypeStruct((s, d), BF16))
        out_specs.append(pl.BlockSpec((tm, d), lambda i, j: (i, 0)))
    res = _pcall(
        body, grid=(s // tm, n // tn),
        in_specs=[pl.BlockSpec((tm, d), lambda i, j: (i, 0)), pl.BlockSpec((1, d), lambda i, j: (0, 0)), w_spec],
        out_specs=out_specs, out_shape=out_shape, scratch_shapes=[pltpu.VMEM((tm, d), BF16)],
        compiler_params=_params("parallel", "arbitrary"), name=name)(x, gain, w)
    return res if save_u else res[0]


def _mm_tn(a, b, name, tm=1024, tn=1024, col_blocks=None):
    k, m = a.shape
    n = b.shape[1]
    tk = _row_tile(k, 1024)
    tm, tn = min(tm, m), min(tn, n)
    if col_blocks:
        tn = n // col_blocks
    assert m % tm == 0 and n % tn == 0
    nk = k // tk

    def body(a_ref, b_ref, o_ref, acc):
        kk = pl.program_id(2)

        @pl.when(kk == 0)
        def _():
            acc[...] = jnp.zeros_like(acc)

        acc[...] += _dot_tn(a_ref[...], b_ref[...])

        @pl.when(kk == nk - 1)
        def _():
            if col_blocks:
                o_ref[0] = acc[...]
            else:
                o_ref[...] = acc[...]

    if col_blocks:
        out_spec = pl.BlockSpec((1, tm, tn), lambda i, j, kk: (j, i, 0))
        out_shape = jax.ShapeDtypeStruct((col_blocks, m, tn), F32)
    else:
        out_spec = pl.BlockSpec((tm, tn), lambda i, j, kk: (i, j))
        out_shape = jax.ShapeDtypeStruct((m, n), F32)
    return _pcall(
        body, grid=(m // tm, n // tn, nk),
        in_specs=[pl.BlockSpec((tk, tm), lambda i, j, kk: (kk, i)), pl.BlockSpec((tk, tn), lambda i, j, kk: (kk, j))],
        out_specs=out_spec, out_shape=out_shape,
        scratch_shapes=[pltpu.VMEM((tm, tn), F32)],
        compiler_params=_params("parallel", "parallel", "arbitrary"), name=name)(a, b)


def _mm_tn_into(a, b, name, pack, off, by_cols):
    k, m = a.shape
    n = b.shape[1]
    tk = _row_tile(k, 1024)
    nk = k // tk
    rows = m if by_cols else m // 4
    tm = min(m, 1024)
    nb = 1 if by_cols else tm // rows
    assert tm == nb * rows and off % rows == 0 and n == (4096 if by_cols else 1024)

    def body(a_ref, b_ref, *rest):
        o_ref, acc = rest[-2], rest[-1]
        kk = pl.program_id(2)

        @pl.when(kk == 0)
        def _():
            acc[...] = jnp.zeros_like(acc)

        acc[...] += _dot_tn(a_ref[...], b_ref[...])

        @pl.when(kk == nk - 1)
        def _():
            o_ref[...] = acc[...].reshape(nb, rows, 1024)

    if by_cols:
        out_spec = pl.BlockSpec((1, rows, 1024), lambda i, j, kk: (j, off // rows, 0))
    else:
        out_spec = pl.BlockSpec((nb, rows, 1024), lambda i, j, kk: (i, off // rows, 0))
    in_specs = [pl.BlockSpec((tk, tm), lambda i, j, kk: (kk, i)), pl.BlockSpec((tk, 1024), lambda i, j, kk: (kk, j))]
    ops, alias = [a, b], {}
    if pack is not None:
        in_specs.append(ANY)
        ops.append(pack)
        alias = {2: 0}
    return _pcall(
        body, grid=(m // tm, n // 1024, nk), in_specs=in_specs, out_specs=out_spec,
        out_shape=jax.ShapeDtypeStruct((4, PACK_N, 1024), F32), scratch_shapes=[pltpu.VMEM((tm, 1024), F32)],
        input_output_aliases=alias, compiler_params=_params("parallel", "parallel", "arbitrary"), name=name)(*ops)


def _in_bwd(dproj, wcat_t, x, gain, dres, name):
    s, n = dproj.shape
    d = wcat_t.shape[1]
    tm = _row_tile(s, 1024)
    tk = 1024
    nk = n // tk
    ns = s // tm

    def body(dp_ref, w_ref, x_ref, g_ref, dr_ref, dx_ref, dg_ref, acc):
        i, kk = pl.program_id(0), pl.program_id(1)

        @pl.when(kk == 0)
        def _():
            acc[...] = jnp.zeros_like(acc)

        @pl.when((kk == 0) & (i == 0))
        def _():
            dg_ref[...] = jnp.zeros_like(dg_ref)

        acc[...] += _dot(dp_ref[...], w_ref[...])

        @pl.when(kk == nk - 1)
        def _():
            xx = x_ref[...]
            r = _rstd(xx)
            du = acc[...]
            dg_ref[...] += _colsum(du * (xx * r))
            dx_ref[...] = dr_ref[...] + _rms_bwd(du * g_ref[...], xx, r)

    return _pcall(
        body, grid=(ns, nk),
        in_specs=[pl.BlockSpec((tm, tk), lambda i, kk: (i, kk)), pl.BlockSpec((tk, d), lambda i, kk: (kk, 0)),
                  pl.BlockSpec((tm, d), lambda i, kk: (i, 0)), pl.BlockSpec((1, d), lambda i, kk: (0, 0)),
                  pl.BlockSpec((tm, d), lambda i, kk: (i, 0))],
        out_specs=[pl.BlockSpec((tm, d), lambda i, kk: (i, 0)), pl.BlockSpec((8, d), lambda i, kk: (0, 0))],
        out_shape=[jax.ShapeDtypeStruct((s, d), F32), jax.ShapeDtypeStruct((8, d), F32)],
        scratch_shapes=[pltpu.VMEM((tm, d), F32)],
        compiler_params=_params("arbitrary", "arbitrary"), name=name)(dproj, wcat_t, x, gain, dres)


def _prev_rows_spec(ts, width):
    return pl.BlockSpec((8, width), lambda i: (jnp.maximum(i * (ts // 8) - 1, 0), 0))


def _prescan(proj, cosf, sinf, cw, cb, dtb, eexp, name):
    s = proj.shape[0]
    ts = _row_tile(s, 256)

    def body(xbc_ref, prev_ref, q_ref, k_ref, dt_ref, cos_ref, sin_ref, cw_ref, cb_ref, dtb_ref, e_ref,
             qo_ref, ko_ref, xc_ref, dtx_ref):
        i = pl.program_id(0)
        for st in range(SSM_CONV_DIM // 128):
            sl = slice(st * 128, (st + 1) * 128)
            prev = jnp.where(i > 0, prev_ref[:, sl], 0.0)
            xcat = jnp.concatenate([prev, xbc_ref[:, sl]], axis=0)
            pre = cb_ref[:, sl] + cw_ref[3:4, sl] * xcat[8:8 + ts]
            for j in range(3):
                pre = pre + cw_ref[j:j + 1, sl] * pltpu.roll(xcat, 3 - j, 0)[8:8 + ts]
            xc_ref[:, sl] = pre * _sigmoid(pre)
        cs, sn = cos_ref[...], sin_ref[...]
        for h in range(RET_HEADS):
            sl = slice(h * 128, (h + 1) * 128)
            qh, kh = q_ref[:, sl], k_ref[:, sl]
            qo_ref[:, sl] = qh * cs + pltpu.roll(qh, 64, 1) * sn
            ko_ref[:, sl] = (kh * cs + pltpu.roll(kh, 64, 1) * sn) * (RET_QK_DIM ** -0.5)
        dtv = _softplus(dt_ref[:, 0:128] + dtb_ref[...])
        dtx_ref[...] = _split3_dot(dtv, e_ref[...])

    row = lambda w, c: pl.BlockSpec((ts, w), lambda i: (i, c))
    full = lambda a: pl.BlockSpec(a.shape, lambda i: (0,) * a.ndim)
    return _pcall(
        body, grid=(s // ts,),
        in_specs=[row(4096, 0), _prev_rows_spec(ts, 4096), row(512, C_Q // 512), row(512, C_K // 512),
                  row(DT_PAD, C_DT // DT_PAD), row(128, 0), row(128, 0), full(cw), full(cb), full(dtb), full(eexp)],
        out_specs=[row(512, 0), row(512, 0), row(4096, 0), row(2048, 0)],
        out_shape=[jax.ShapeDtypeStruct((s, 512), F32), jax.ShapeDtypeStruct((s, 512), F32),
                   jax.ShapeDtypeStruct((s, 4096), F32), jax.ShapeDtypeStruct((s, 2048), F32)],
        compiler_params=_params("parallel"), name=name)(proj, proj, proj, proj, proj, cosf, sinf, cw, cb, dtb, eexp)


def _scan_fwd(qr, kr, proj, xc, dtx, ax, consts, name):
    s = qr.shape[0]
    nc = s // CHUNK
    dm, qd, kd, cd, eye, blk = consts

    def body(q_ref, k_ref, v_ref, xc_ref, dtx_ref, ax_ref, dm_ref, qd_ref, kd_ref, cd_ref, eye_ref, blk_ref,
             yr_ref, ys_ref, sst_ref, hst_ref, s_sc, h_sc):
        @pl.when(pl.program_id(0) == 0)
        def _():
            s_sc[...] = jnp.zeros_like(s_sc)
            h_sc[...] = jnp.zeros_like(h_sc)

        sst_ref[0] = s_sc[...]
        hst_ref[0] = h_sc[...]
        for h in range(RET_HEADS):
            ql, vl = slice(h * 128, (h + 1) * 128), slice(h * 256, (h + 1) * 256)
            y, s_new = _ret_chunk_fwd(q_ref[:, ql], k_ref[:, ql], v_ref[:, vl], s_sc[ql, :], dm_ref[h],
                                      qd_ref[:, ql], kd_ref[:, ql], cd_ref[ql, :])
            yr_ref[:, vl] = y
            s_sc[ql, :] = s_new
        eye_v, blk_v = eye_ref[...], blk_ref[...]
        for g in range(SSM_GROUPS):
            sl = slice(g * GROUP_W, (g + 1) * GROUP_W)
            bl = slice(SSM_INNER + g * 128, SSM_INNER + (g + 1) * 128)
            cl = slice(SSM_INNER + SSM_BC + g * 128, SSM_INNER + SSM_BC + (g + 1) * 128)
            y, h_new = _ssd_chunk_fwd(xc_ref[:, sl], dtx_ref[:, sl], xc_ref[:, bl], xc_ref[:, cl], ax_ref[:, sl],
                                      h_sc[:, sl], eye_v, blk_v)
            ys_ref[:, sl] = y
            h_sc[:, sl] = h_new

    row = lambda w, c=0: pl.BlockSpec((CHUNK, w), lambda i: (i, c))
    full = lambda a: pl.BlockSpec(a.shape, lambda i: (0,) * a.ndim)
    return _pcall(
        body, grid=(nc,),
        in_specs=[row(512), row(512), row(1024, C_V // 1024), row(4096), row(2048), full(ax), full(dm), full(qd),
                  full(kd), full(cd), full(eye), full(blk)],
        out_specs=[row(1024), row(2048), pl.BlockSpec((1, 512, 256), lambda i: (i, 0, 0)),
                   pl.BlockSpec((1, 128, 2048), lambda i: (i, 0, 0))],
        out_shape=[jax.ShapeDtypeStruct((s, 1024), F32), jax.ShapeDtypeStruct((s, 2048), F32),
                   jax.ShapeDtypeStruct((nc, 512, 256), F32), jax.ShapeDtypeStruct((nc, 128, 2048), F32)],
        scratch_shapes=[pltpu.VMEM((512, 256), F32), pltpu.VMEM((128, 2048), F32)],
        compiler_params=_params("arbitrary"), name=name)(qr, kr, proj, xc, dtx, ax, dm, qd, kd, cd, eye, blk)


def _scan_bwd(qr, kr, proj, xc, dtx, ax, consts, sst, hst, dyr, dys, dproj, name):
    s = qr.shape[0]
    nc = s // CHUNK
    dm, qd, kd, cd, eye, blk = consts

    def body(q_ref, k_ref, v_ref, xc_ref, dtx_ref, ax_ref, dm_ref, qd_ref, kd_ref, cd_ref, eye_ref, blk_ref,
             sst_ref, hst_ref, dyr_ref, dys_ref, dproj_in, dq_ref, dk_ref, dv_ref, dxc_ref, gdt_ref, da_ref, ds_sc,
             dh_sc):
        @pl.when(pl.program_id(0) == 0)
        def _():
            ds_sc[...] = jnp.zeros_like(ds_sc)
            dh_sc[...] = jnp.zeros_like(dh_sc)
            da_ref[...] = jnp.zeros_like(da_ref)

        for h in range(RET_HEADS):
            ql, vl = slice(h * 128, (h + 1) * 128), slice(h * 256, (h + 1) * 256)
            dq, dk, dv, ds = _ret_chunk_bwd(q_ref[:, ql], k_ref[:, ql], v_ref[:, vl], sst_ref[0, ql, :], dm_ref[h],
                                            qd_ref[:, ql], kd_ref[:, ql], cd_ref[ql, :], dyr_ref[:, vl], ds_sc[ql, :])
            dq_ref[:, ql] = dq
            dk_ref[:, ql] = dk
            dv_ref[:, vl] = _bf(dv)
            ds_sc[ql, :] = ds
        eye_v, blk_v = eye_ref[...], blk_ref[...]
        for g in range(SSM_GROUPS):
            sl = slice(g * GROUP_W, (g + 1) * GROUP_W)
            bl = slice(SSM_INNER + g * 128, SSM_INNER + (g + 1) * 128)
            cl = slice(SSM_INNER + SSM_BC + g * 128, SSM_INNER + SSM_BC + (g + 1) * 128)
            dxs, g_dtx, db, dc, da, dh = _ssd_chunk_bwd(
                xc_ref[:, sl], dtx_ref[:, sl], xc_ref[:, bl], xc_ref[:, cl], ax_ref[:, sl], hst_ref[0, :, sl],
                eye_v, blk_v, dys_ref[:, sl], dh_sc[:, sl])
            dxc_ref[:, sl] = dxs
            dxc_ref[:, bl] = db
            dxc_ref[:, cl] = dc
            gdt_ref[:, sl] = g_dtx
            da_ref[:, sl] += da
            dh_sc[:, sl] = dh

    row = lambda w, c=0: pl.BlockSpec((CHUNK, w), lambda i: (nc - 1 - i, c))
    full = lambda a: pl.BlockSpec(a.shape, lambda i: (0,) * a.ndim)
    return _pcall(
        body, grid=(nc,),
        in_specs=[row(512), row(512), row(1024, C_V // 1024), row(4096), row(2048), full(ax), full(dm), full(qd),
                  full(kd), full(cd), full(eye), full(blk),
                  pl.BlockSpec((1, 512, 256), lambda i: (nc - 1 - i, 0, 0)),
                  pl.BlockSpec((1, 128, 2048), lambda i: (nc - 1 - i, 0, 0)), row(1024), row(2048), ANY],
        out_specs=[row(512), row(512), row(1024, C_V // 1024), row(4096), row(2048),
                   pl.BlockSpec((1, 2048), lambda i: (0, 0))],
        out_shape=[jax.ShapeDtypeStruct((s, 512), F32), jax.ShapeDtypeStruct((s, 512), F32),
                   jax.ShapeDtypeStruct(dproj.shape, BF16), jax.ShapeDtypeStruct((s, 4096), F32),
                   jax.ShapeDtypeStruct((s, 2048), F32), jax.ShapeDtypeStruct((1, 2048), F32)],
        scratch_shapes=[pltpu.VMEM((512, 256), F32), pltpu.VMEM((128, 2048), F32)],
        input_output_aliases={16: 2},
        compiler_params=_params("arbitrary"), name=name)(qr, kr, proj, xc, dtx, ax, dm, qd, kd, cd, eye, blk, sst, hst,
                                                          dyr, dys, dproj)


def _mix_values(yr, g, ys, xs, z, gates, bg, dsk, sn):
    sg, dsg = _silu_and_grad(g)
    ret = _rms_groups(yr, RET_V_DIM)
    yrn = jnp.concatenate([slab * r for slab, r in ret], axis=1) * sg
    sz, dsz = _silu_and_grad(z)
    ys0 = ys + xs * dsk
    ys1 = ys0 * sz
    grp = _rms_groups(ys1, GROUP_W)
    ysh = jnp.concatenate([slab * r for slab, r in grp], axis=1)
    ysn = ysh * sn
    gg = _sigmoid(gates + bg)
    return dict(sg=sg, dsg=dsg, ret=ret, yrn=yrn, sz=sz, dsz=dsz, ys0=ys0, ys1=ys1, grp=grp, ysh=ysh, ysn=ysn,
                gr=gg[:, :D_MODEL], gs=gg[:, D_MODEL:])


def _postscan_fwd(x, yr, ys, xc, proj, bg, dsk, sn, wr, ws, wo, name):
    s = x.shape[0]
    ts = _row_tile(s, 256)

    def body(x_ref, yr_ref, ys_ref, xs_ref, g_ref, z_ref, gt_ref, bg_ref, dsk_ref, sn_ref, wr_ref, ws_ref, wo_ref,
             o_ref):
        m = _mix_values(yr_ref[...], g_ref[...], ys_ref[...], xs_ref[...], z_ref[...], gt_ref[...], bg_ref[...],
                        dsk_ref[...], sn_ref[...])
        merged = m["gr"] * _dot(m["yrn"], wr_ref[...]) + m["gs"] * _dot(m["ysn"], ws_ref[...])
        o_ref[...] = x_ref[...] + _dot(merged, wo_ref[...])

    row = lambda w, c=0: pl.BlockSpec((ts, w), lambda i: (i, c))
    full = lambda a: pl.BlockSpec(a.shape, lambda i: (0,) * a.ndim)
    return _pcall(
        body, grid=(s // ts,),
        in_specs=[row(1024), row(1024), row(2048), row(2048), row(1024, C_G // 1024), row(2048, C_Z // 2048),
                  row(2048, C_GATES // 2048), full(bg), full(dsk), full(sn), full(wr), full(ws), full(wo)],
        out_specs=row(1024), out_shape=jax.ShapeDtypeStruct((s, D_MODEL), F32),
        compiler_params=_params("parallel"), name=name)(x, yr, ys, xc, proj, proj, proj, bg, dsk, sn, wr, ws, wo)


def _postscan_bwd(dout, yr, ys, xc, proj, bg, dsk, sn, wr, ws, wo, name):
    s = dout.shape[0]
    ts = _row_tile(s, 128)

    def body(do_ref, yr_ref, ys_ref, xs_ref, g_ref, z_ref, gt_ref, bg_ref, dsk_ref, sn_ref, wr_ref, ws_ref, wo_ref,
             dyr_ref, dys_ref, dxs_ref, dproj_ref, yrn_ref, ysn_ref, mg_ref, dbr_ref, dbs_ref,
             dbg_ref, ddsk_ref, dsn_ref):
        @pl.when(pl.program_id(0) == 0)
        def _():
            dbg_ref[...] = jnp.zeros_like(dbg_ref)
            ddsk_ref[...] = jnp.zeros_like(ddsk_ref)
            dsn_ref[...] = jnp.zeros_like(dsn_ref)

        xs = xs_ref[...]
        m = _mix_values(yr_ref[...], g_ref[...], ys_ref[...], xs, z_ref[...], gt_ref[...], bg_ref[...],
                        dsk_ref[...], sn_ref[...])
        gr, gs = m["gr"], m["gs"]
        br, bs = _dot(m["yrn"], wr_ref[...]), _dot(m["ysn"], ws_ref[...])
        dmerged = _dot_nt(do_ref[...], wo_ref[...])
        dgt = jnp.concatenate([dmerged * br * gr * (1.0 - gr), dmerged * bs * gs * (1.0 - gs)], axis=1)
        dproj_ref[:, C_GATES:C_GATES + 2048] = _bf(dgt)
        dbg_ref[...] += _colsum(dgt)
        dbr, dbs = dmerged * gr, dmerged * gs
        yrn_ref[...] = _bf(m["yrn"])
        ysn_ref[...] = _bf(m["ysn"])
        mg_ref[...] = _bf(gr * br + gs * bs)
        dbr_ref[...] = _bf(dbr)
        dbs_ref[...] = _bf(dbs)
        dyrn = _dot_nt(dbr, wr_ref[...])
        dysn = _dot_nt(dbs, ws_ref[...])
        rn = jnp.concatenate([slab * r for slab, r in m["ret"]], axis=1)
        dproj_ref[:, C_G:C_G + 1024] = _bf(dyrn * rn * m["dsg"])
        drn = dyrn * m["sg"]
        dyr_ref[...] = jnp.concatenate(
            [_rms_bwd(drn[:, h * RET_V_DIM:(h + 1) * RET_V_DIM], slab, r) for h, (slab, r) in enumerate(m["ret"])], axis=1)
        dsn_ref[...] += _colsum(dysn * m["ysh"])
        dysh = dysn * sn_ref[...]
        dys1 = jnp.concatenate(
            [_rms_bwd(dysh[:, h * GROUP_W:(h + 1) * GROUP_W], slab, r) for h, (slab, r) in enumerate(m["grp"])], axis=1)
        dproj_ref[:, C_Z:C_Z + 2048] = _bf(dys1 * m["ys0"] * m["dsz"])
        dys0 = dys1 * m["sz"]
        dys_ref[...] = dys0
        dxs_ref[...] = dys0 * dsk_ref[...]
        ddsk_ref[...] += _colsum(dys0 * xs)

    row = lambda w, c=0: pl.BlockSpec((ts, w), lambda i: (i, c))
    full = lambda a: pl.BlockSpec(a.shape, lambda i: (0,) * a.ndim)
    acc = lambda w: pl.BlockSpec((8, w), lambda i: (0, 0))
    sds = jax.ShapeDtypeStruct
    return _pcall(
        body, grid=(s // ts,),
        in_specs=[row(1024), row(1024), row(2048), row(2048), row(1024, C_G // 1024), row(2048, C_Z // 2048),
                  row(2048, C_GATES // 2048), full(bg), full(dsk), full(sn), full(wr), full(ws), full(wo)],
        out_specs=[row(1024), row(2048), row(2048), row(NP), row(1024), row(2048), row(1024),
                   row(1024), row(1024), acc(2048), acc(2048), acc(2048)],
        out_shape=[sds((s, 1024), F32), sds((s, 2048), F32), sds((s, 2048), F32), sds((s, NP), BF16),
                   sds((s, 1024), BF16), sds((s, 2048), BF16),
                   sds((s, 1024), BF16), sds((s, 1024), BF16), sds((s, 1024), BF16), sds((8, 2048), F32),
                   sds((8, 2048), F32), sds((8, 2048), F32)],
        compiler_params=_params("arbitrary"), name=name)(dout, yr, ys, xc, proj, proj, proj, bg, dsk, sn, wr, ws, wo)


def _prescan_bwd(proj, dxc, dxs_skip, gdtx, dqr, dkr, cosf, sinf, cw, cb, dtb, eexp_t, dproj, name):
    s = proj.shape[0]
    ts = _row_tile(s, 256)
    nt = s // ts
    m = ts + 8
    width = C_DT + DT_PAD

    def body(xbc_ref, prev_ref, nxt_ref, dt_ref, dxc_ref, dxcn_ref, dsk_ref, dskn_ref, gdt_ref, dq_ref, dk_ref,
             cos_ref, sin_ref, cw_ref, cb_ref, dtb_ref, et_ref, dproj_in, dp_ref, dcw_ref, dcb_ref, ddtb_ref):
        i = pl.program_id(0)

        @pl.when(i == 0)
        def _():
            ddtb_ref[...] = jnp.zeros_like(ddtb_ref)
            dcw_ref[...] = jnp.zeros_like(dcw_ref)
            dcb_ref[...] = jnp.zeros_like(dcb_ref)

        rows = lax.broadcasted_iota(jnp.int32, (m, 128), 0)
        live = (rows < ts) | (i < nt - 1)
        for st in range(SSM_CONV_DIM // 128):
            sl = slice(st * 128, (st + 1) * 128)
            prev = jnp.where(i > 0, prev_ref[:, sl], 0.0)
            xcat = jnp.concatenate([prev, xbc_ref[:, sl], nxt_ref[:, sl]], axis=0)
            shifted = [pltpu.roll(xcat, 3 - j, 0) for j in range(3)] + [xcat]
            pre = cb_ref[:, sl]
            for j in range(SSM_CONV):
                pre = pre + cw_ref[j:j + 1, sl] * shifted[j][8:]
            _, dsilu = _silu_and_grad(pre)
            dxc = jnp.concatenate([dxc_ref[:, sl], dxcn_ref[:, sl]], axis=0)
            if st * 128 < SSM_INNER:
                dxc = dxc + jnp.concatenate([dsk_ref[:, sl], dskn_ref[:, sl]], axis=0)
            dpre = jnp.where(live, dxc * dsilu, 0.0)
            dpt = dpre[0:ts]
            dx = cw_ref[3:4, sl] * dpt
            for j in range(3):
                dx = dx + cw_ref[j:j + 1, sl] * pltpu.roll(dpre, m - (3 - j), 0)[0:ts]
            for j in range(SSM_CONV):
                dcw_ref[8 * j:8 * j + 8, sl] += _colsum(dpt * shifted[j][8:8 + ts])
            dcb_ref[:, sl] += _colsum(dpt)
            dp_ref[:, sl] = _bf(dx)
        cs, sn = cos_ref[...], sin_ref[...]
        for h in range(RET_HEADS):
            sl = slice(h * 128, (h + 1) * 128)
            dq = dq_ref[:, sl]
            dk = dk_ref[:, sl] * (RET_QK_DIM ** -0.5)
            dp_ref[:, C_Q + h * 128:C_Q + (h + 1) * 128] = _bf(dq * cs + pltpu.roll(dq * sn, 64, 1))
            dp_ref[:, C_K + h * 128:C_K + (h + 1) * 128] = _bf(dk * cs + pltpu.roll(dk * sn, 64, 1))
        ddt = _split3_dot(gdt_ref[...], et_ref[...])
        ddt = ddt * _sigmoid(dt_ref[:, 0:128] + dtb_ref[...])
        ddtb_ref[...] += _colsum(ddt)
        dp_ref[:, C_DT:C_DT + 128] = _bf(ddt)
        dp_ref[:, C_DT + 128:C_DT + DT_PAD] = jnp.zeros((ts, DT_PAD - 128), BF16)

    row = lambda w, c=0: pl.BlockSpec((ts, w), lambda i: (i, c))
    nxt = lambda w: pl.BlockSpec((8, w), lambda i: (jnp.minimum((i + 1) * (ts // 8), s // 8 - 1), 0))
    full = lambda a: pl.BlockSpec(a.shape, lambda i: (0,) * a.ndim)
    sds = jax.ShapeDtypeStruct
    return _pcall(
        body, grid=(nt,),
        in_specs=[row(4096), _prev_rows_spec(ts, 4096), nxt(4096), row(DT_PAD, C_DT // DT_PAD), row(4096), nxt(4096),
                  row(2048), nxt(2048), row(2048), row(512), row(512), row(128), row(128), full(cw), full(cb),
                  full(dtb), full(eexp_t), ANY],
        out_specs=[row(width), pl.BlockSpec((32, 4096), lambda i: (0, 0)), pl.BlockSpec((8, 4096), lambda i: (0, 0)),
                   pl.BlockSpec((8, 128), lambda i: (0, 0))],
        out_shape=[sds(dproj.shape, BF16), sds((32, 4096), F32), sds((8, 4096), F32), sds((8, 128), F32)],
        input_output_aliases={17: 0},
        compiler_params=_params("arbitrary"), name=name)(proj, proj, proj, proj, dxc, dxc, dxs_skip, dxs_skip, gdtx,
                                                          dqr, dkr, cosf, sinf, cw, cb, dtb, eexp_t, dproj)


def _xattn_values(x, gain, wq, kv):
    r = _rstd(x)
    h = (x * r) * gain
    q = _dot(h, wq)
    ps, os_ = [], []
    for hd in range(XA_HEADS):
        sl = slice(hd * XA_HEAD_DIM, (hd + 1) * XA_HEAD_DIM)
        sc = _dot_nt(q[:, sl], kv[:, sl]) * (XA_HEAD_DIM ** -0.5)
        e = jnp.exp(sc - jnp.max(sc, axis=-1, keepdims=True))
        p = e / jnp.sum(e, axis=-1, keepdims=True)
        ps.append(p)
        os_.append(_dot(p, kv[:, D_MODEL + hd * XA_HEAD_DIM:D_MODEL + (hd + 1) * XA_HEAD_DIM]))
    return r, h, q, ps, jnp.concatenate(os_, axis=1)


def _xattn_fwd(x, gain, wq, kv, wo, name):
    s = x.shape[0]
    ts = _row_tile(s, 256)

    def body(x_ref, g_ref, wq_ref, kv_ref, wo_ref, o_ref):
        x_ = x_ref[...]
        _, _, _, _, o = _xattn_values(x_, g_ref[...], wq_ref[...], kv_ref[...])
        o_ref[...] = x_ + _dot(o, wo_ref[...])

    row = pl.BlockSpec((ts, D_MODEL), lambda i: (i, 0))
    full = lambda a: pl.BlockSpec(a.shape, lambda i: (0,) * a.ndim)
    return _pcall(
        body, grid=(s // ts,), in_specs=[row, full(gain), full(wq), full(kv), full(wo)], out_specs=row,
        out_shape=jax.ShapeDtypeStruct((s, D_MODEL), F32), compiler_params=_params("parallel"), name=name)(
            x, gain, wq, kv, wo)


def _xattn_bwd(x, dout, gain, wq, kv, wo, name):
    s = x.shape[0]
    m = kv.shape[0]
    ts = _row_tile(s, 256)

    def body(x_ref, do_ref, g_ref, wq_ref, kv_ref, wo_ref, dx_ref, h_ref, dq_ref, o_ref, dkv_ref, dg_ref):
        @pl.when(pl.program_id(0) == 0)
        def _():
            dkv_ref[...] = jnp.zeros_like(dkv_ref)
            dg_ref[...] = jnp.zeros_like(dg_ref)

        x_, do, kvv = x_ref[...], do_ref[...], kv_ref[...]
        r, h, q, ps, o = _xattn_values(x_, g_ref[...], wq_ref[...], kvv)
        dov = _dot_nt(do, wo_ref[...])
        dqs = []
        for hd in range(XA_HEADS):
            sl = slice(hd * XA_HEAD_DIM, (hd + 1) * XA_HEAD_DIM)
            vl = slice(D_MODEL + hd * XA_HEAD_DIM, D_MODEL + (hd + 1) * XA_HEAD_DIM)
            p, doh = ps[hd], dov[:, sl]
            dp = _dot_nt(doh, kvv[:, vl])
            dsc = p * (dp - jnp.sum(dp * p, axis=-1, keepdims=True)) * (XA_HEAD_DIM ** -0.5)
            dqs.append(_dot(dsc, kvv[:, sl]))
            dkv_ref[:, sl] += _dot_tn(dsc, q[:, sl])
            dkv_ref[:, vl] += _dot_tn(p, doh)
        dq = jnp.concatenate(dqs, axis=1)
        dh = _dot_nt(dq, wq_ref[...])
        dg_ref[...] += _colsum(dh * (x_ * r))
        dx_ref[...] = do + _rms_bwd(dh * g_ref[...], x_, r)
        h_ref[...] = _bf(h)
        dq_ref[...] = _bf(dq)
        o_ref[...] = _bf(o)

    row = pl.BlockSpec((ts, D_MODEL), lambda i: (i, 0))
    full = lambda a: pl.BlockSpec(a.shape, lambda i: (0,) * a.ndim)
    sds = jax.ShapeDtypeStruct
    return _pcall(
        body, grid=(s // ts,), in_specs=[row, row, full(gain), full(wq), full(kv), full(wo)],
        out_specs=[row, row, row, row, pl.BlockSpec((m, 2 * D_MODEL), lambda i: (0, 0)),
                   pl.BlockSpec((8, D_MODEL), lambda i: (0, 0))],
        out_shape=[sds((s, D_MODEL), F32), sds((s, D_MODEL), BF16), sds((s, D_MODEL), BF16), sds((s, D_MODEL), BF16),
                   sds((m, 2 * D_MODEL), F32), sds((8, D_MODEL), F32)],
        compiler_params=_params("arbitrary"), name=name)(x, dout, gain, wq, kv, wo)


def _mem_bwd(mem, gain, dkv, wkv, name):
    m = mem.shape[0]

    def body(mem_ref, g_ref, dkv_ref, w_ref, mn_ref, dg_ref):
        mm = mem_ref[...]
        r = _rstd(mm)
        xh = mm * r
        mn_ref[...] = _bf(xh * g_ref[...])
        nb, _, wb = w_ref.shape
        dmn = _dot_nt(dkv_ref[:, 0:wb], w_ref[0])
        for j in range(1, nb):
            dmn = dmn + _dot_nt(dkv_ref[:, j * wb:(j + 1) * wb], w_ref[j])
        dg_ref[...] = jnp.zeros_like(dg_ref) + _colsum(dmn * xh)

    full = lambda a: pl.BlockSpec(a.shape, lambda: (0,) * a.ndim)
    return _pcall(
        body, in_specs=[full(mem), full(gain), full(dkv), full(wkv)],
        out_specs=[pl.BlockSpec((m, D_MODEL), lambda: (0, 0)), pl.BlockSpec((8, D_MODEL), lambda: (0, 0))],
        out_shape=[jax.ShapeDtypeStruct((m, D_MODEL), BF16), jax.ShapeDtypeStruct((8, D_MODEL), F32)],
        compiler_params=pltpu.CompilerParams(vmem_limit_bytes=VMEM_LIMIT), name=name)(mem, gain, dkv, wkv)


def _mlp_fwd(x, gain, w1, w2, name):
    s = x.shape[0]
    ts = _row_tile(s, 512)
    tf = 1024
    nf = D_FF // tf

    def body(x_ref, g_ref, w1_ref, w2_ref, o_ref, h_sc, acc):
        j = pl.program_id(1)

        @pl.when(j == 0)
        def _():
            xx = x_ref[...]
            h_sc[...] = _bf((xx * _rstd(xx)) * g_ref[...])
            acc[...] = jnp.zeros_like(acc)

        a = jnp.dot(h_sc[...], w1_ref[0], preferred_element_type=F32)
        r = jnp.square(jnp.maximum(a, 0.0))
        acc[...] += _dot(r, w2_ref[...])

        @pl.when(j == nf - 1)
        def _():
            o_ref[...] = x_ref[...] + acc[...]

    row = pl.BlockSpec((ts, D_MODEL), lambda i, j: (i, 0))
    return _pcall(
        body, grid=(s // ts, nf),
        in_specs=[row, pl.BlockSpec((1, D_MODEL), lambda i, j: (0, 0)),
                  pl.BlockSpec((1, D_MODEL, tf), lambda i, j: (j, 0, 0)), pl.BlockSpec((tf, D_MODEL), lambda i, j: (j, 0))],
        out_specs=row, out_shape=jax.ShapeDtypeStruct((s, D_MODEL), F32),
        scratch_shapes=[pltpu.VMEM((ts, D_MODEL), BF16), pltpu.VMEM((ts, D_MODEL), F32)],
        compiler_params=_params("parallel", "arbitrary"), name=name)(x, gain, w1, w2)


def _mlp_bwd(x, dout, gain, w1, w2, name):
    s = x.shape[0]
    ts = _row_tile(s, 512)
    tf = 1024
    nf = D_FF // tf

    def body(x_ref, do_ref, g_ref, w1_ref, w2_ref, dx_ref, h_ref, r_ref, da_ref, dg_ref, h_sc, do_sc, acc):
        i, j = pl.program_id(0), pl.program_id(1)

        @pl.when(j == 0)
        def _():
            xx = x_ref[...]
            h_sc[...] = _bf((xx * _rstd(xx)) * g_ref[...])
            do_sc[...] = _bf(do_ref[...])
            acc[...] = jnp.zeros_like(acc)
            h_ref[...] = h_sc[...]

        @pl.when((j == 0) & (i == 0))
        def _():
            dg_ref[...] = jnp.zeros_like(dg_ref)

        a = jnp.dot(h_sc[...], w1_ref[0], preferred_element_type=F32)
        ra = jnp.maximum(a, 0.0)
        r_ref[...] = _bf(ra * ra)
        dr = lax.dot_general(do_sc[...], w2_ref[...], (((1,), (1,)), ((), ())), preferred_element_type=F32)
        da = _bf(dr * 2.0 * ra)
        da_ref[...] = da
        acc[...] += lax.dot_general(da, w1_ref[0], (((1,), (1,)), ((), ())), preferred_element_type=F32)

        @pl.when(j == nf - 1)
        def _():
            xx = x_ref[...]
            r = _rstd(xx)
            dh = acc[...]
            dg_ref[...] += _colsum(dh * (xx * r))
            dx_ref[...] = do_ref[...] + _rms_bwd(dh * g_ref[...], xx, r)

    row = pl.BlockSpec((ts, D_MODEL), lambda i, j: (i, 0))
    ff = pl.BlockSpec((ts, tf), lambda i, j: (i, j))
    sds = jax.ShapeDtypeStruct
    return _pcall(
        body, grid=(s // ts, nf),
        in_specs=[row, row, pl.BlockSpec((1, D_MODEL), lambda i, j: (0, 0)),
                  pl.BlockSpec((1, D_MODEL, tf), lambda i, j: (j, 0, 0)), pl.BlockSpec((tf, D_MODEL), lambda i, j: (j, 0))],
        out_specs=[row, row, ff, ff, pl.BlockSpec((8, D_MODEL), lambda i, j: (0, 0))],
        out_shape=[sds((s, D_MODEL), F32), sds((s, D_MODEL), BF16), sds((s, D_FF), BF16), sds((s, D_FF), BF16),
                   sds((8, D_MODEL), F32)],
        scratch_shapes=[pltpu.VMEM((ts, D_MODEL), BF16), pltpu.VMEM((ts, D_MODEL), BF16), pltpu.VMEM((ts, D_MODEL), F32)],
        compiler_params=_params("arbitrary", "arbitrary"), name=name)(x, dout, gain, w1, w2)


def _final(x, gain, tgt, name):
    s = x.shape[0]
    ts = _row_tile(s, 512)

    def body(x_ref, g_ref, t_ref, dx_ref, loss_ref, dg_ref):
        @pl.when(pl.program_id(0) == 0)
        def _():
            loss_ref[...] = jnp.zeros_like(loss_ref)
            dg_ref[...] = jnp.zeros_like(dg_ref)

        xx = x_ref[...]
        r = _rstd(xx)
        xh = xx * r
        err = xh * g_ref[...] - t_ref[...]
        loss_ref[...] += 0.5 * jnp.sum(jnp.sum(err * err, axis=1, keepdims=True), axis=0, keepdims=True) / D_MODEL
        dy = err * (1.0 / D_MODEL)
        dg_ref[...] += _colsum(dy * xh)
        dx_ref[...] = _rms_bwd(dy * g_ref[...], xx, r)

    row = pl.BlockSpec((ts, D_MODEL), lambda i: (i, 0))
    return _pcall(
        body, grid=(s // ts,), in_specs=[row, pl.BlockSpec((1, D_MODEL), lambda i: (0, 0)), row],
        out_specs=[row, pl.BlockSpec((8, 128), lambda i: (0, 0)), pl.BlockSpec((8, D_MODEL), lambda i: (0, 0))],
        out_shape=[jax.ShapeDtypeStruct((s, D_MODEL), F32), jax.ShapeDtypeStruct((8, 128), F32),
                   jax.ShapeDtypeStruct((8, D_MODEL), F32)],
        compiler_params=_params("arbitrary"), name=name)(x, gain, tgt)


def _as3d(a):
    return a.reshape((-1,) + a.shape[-2:])


def _ew_tile(r, c):
    if r % 256 == 0 or r <= 256:
        return _row_tile(r, 256), c
    return r, 128


def _sum_cast(terms, out_dtype, name):
    shape = terms[0].shape
    t3 = [_as3d(t) for t in terms]
    b, r, c = t3[0].shape
    tr, tc = _ew_tile(r, c)
    nc = c // tc

    def body(*refs):
        acc = refs[0][...].astype(F32)
        for t in refs[1:-1]:
            acc = acc + t[...].astype(F32)
        refs[-1][...] = acc.astype(out_dtype)

    spec = pl.BlockSpec((1, tr, tc), lambda i, j: (i, j // nc, j % nc))
    out = _pcall(body, grid=(b, (r // tr) * nc), in_specs=[spec] * len(t3), out_specs=spec,
                 out_shape=jax.ShapeDtypeStruct((b, r, c), out_dtype), compiler_params=_params("parallel", "parallel"),
                 name=name)(*t3)
    return out.reshape(shape)


def _split_axis(rows):
    return 0 if rows % 64 == 0 else 1


def _half_of(ref, which, lead=()):
    rows, cols = ref.shape[-2], ref.shape[-1]
    if _split_axis(rows) == 0:
        return ref.at[(*lead, pl.ds(which * (rows // 2), rows // 2))]
    return ref.at[(*lead, slice(None), pl.ds(which * (cols // 2), cols // 2))]


def _half_shape(shape):
    rows, cols = shape[-2], shape[-1]
    return (*shape[:-2], rows // 2, cols) if _split_axis(rows) == 0 else (*shape[:-2], rows, cols // 2)


def _pair_sum(a, b, sel, half_id, out_dtype, name):
    _, h, c = b.shape
    k = sel.shape[0]
    by_rows = _split_axis(a.shape[1]) == 0
    tr, tc = _ew_tile(h, c)
    nr, nc = h // tr, c // tc

    def body(sel_ref, hid_ref, a_ref, b_ref, o_ref):
        o_ref[...] = (a_ref[...] + b_ref[...]).astype(out_dtype)

    def a_map(q, j, sel_ref, hid_ref):
        if by_rows:
            return sel_ref[q], hid_ref[0] * nr + j // nc, j % nc
        return sel_ref[q], j // nc, hid_ref[0] * nc + j % nc

    blkshape = (1, tr, tc)
    grid_spec = pltpu.PrefetchScalarGridSpec(
        num_scalar_prefetch=2, grid=(k, nr * nc),
        in_specs=[pl.BlockSpec(blkshape, a_map),
                  pl.BlockSpec(blkshape, lambda q, j, sel_ref, hid_ref: (sel_ref[q], j // nc, j % nc))],
        out_specs=pl.BlockSpec(blkshape, lambda q, j, sel_ref, hid_ref: (q, j // nc, j % nc)))
    return _pcall(body, grid_spec=grid_spec, out_shape=jax.ShapeDtypeStruct((k, h, c), out_dtype),
                  compiler_params=_params("parallel", "parallel"), name=name)(sel, half_id, a, b)


def _adamw(w, g, m, v, name):
    shape = w.shape
    w3, g3, m3, v3 = _as3d(w), _as3d(g), _as3d(m), _as3d(v)
    b, r, c = w3.shape
    tr, tc = _ew_tile(r, c)

    def body(w_ref, g_ref, m_ref, v_ref, d_ref, mo_ref, vo_ref):
        gg = g_ref[...]
        mn = ADAM_B1 * m_ref[...] + (1.0 - ADAM_B1) * gg
        vn = ADAM_B2 * v_ref[...] + (1.0 - ADAM_B2) * jnp.square(gg)
        m_hat = mn / (1.0 - ADAM_B1 ** ADAM_STEP)
        v_hat = vn / (1.0 - ADAM_B2 ** ADAM_STEP)
        d_ref[...] = -ADAM_LR * (m_hat / (jnp.sqrt(v_hat) + ADAM_EPS) + ADAM_WD * w_ref[...])
        mo_ref[...] = mn
        vo_ref[...] = vn

    spec = pl.BlockSpec((1, tr, tc), lambda i, j: (i, j // (c // tc), j % (c // tc)))
    sd = jax.ShapeDtypeStruct((b, r, c), F32)
    d, mo, vo = _pcall(body, grid=(b, (r // tr) * (c // tc)), in_specs=[spec] * 4, out_specs=[spec] * 3,
                       out_shape=[sd] * 3, compiler_params=_params("parallel", "parallel"), name=name)(w3, g3, m3, v3)
    return d.reshape(shape), mo.reshape(shape), vo.reshape(shape)


ANY = pl.BlockSpec(memory_space=pl.ANY)


def _place():
    return lax.axis_index("x"), lax.axis_index("y"), lax.axis_index("c")


def _flip(x, y, r):
    return (1 - x if r & 2 else x), (1 - y if r & 1 else y)


def _dma_sems(*counts):
    return [pltpu.SemaphoreType.DMA((k,)) for k in counts]


def _gather_ici(shards):
    n = len(shards)

    def copies(ins, outs, sems, incoming):
        send, recv = sems
        x, y, c = _place()
        out = []
        for r in (1, 2, 3):
            cx, cy = _flip(x, y, r)
            for a in range(n):
                block = 2 * cx + cy if incoming else 2 * x + y
                out.append(pltpu.make_async_remote_copy(
                    src_ref=_half_of(ins[a], c), dst_ref=_half_of(outs[a], c, (block,)),
                    send_sem=send.at[(r - 1) * n + a], recv_sem=recv.at[(r - 1) * n + a], device_id=(cx, cy, c),
                    device_id_type=MESH))
        return out

    def start(ins, outs, sems):
        for cp in copies(ins, outs, sems, False):
            cp.start()

    def finish(ins, outs, sems):
        for cp in copies(ins, outs, sems, True):
            cp.wait_recv()
        for cp in copies(ins, outs, sems, False):
            cp.wait_send()

    return _Comm(shards, [jax.ShapeDtypeStruct((4,) + a.shape, a.dtype) for a in shards], _dma_sems(3 * n, 3 * n),
                 start, finish)


def _gather_d2d(bufs, shards):
    n = len(bufs)

    def copies(ins, outs, sems, incoming):
        send, recv = sems
        x, y, c = _place()
        out = []
        for r in (1, 2, 3):
            cx, cy = _flip(x, y, r)
            for a in range(n):
                ref = _half_of(outs[a], (1 - c) if incoming else c, (2 * cx + cy,))
                out.append(pltpu.make_async_remote_copy(
                    src_ref=ref, dst_ref=ref, send_sem=send.at[(r - 1) * n + a], recv_sem=recv.at[(r - 1) * n + a],
                    device_id=(x, y, 1 - c), device_id_type=MESH))
        for a in range(n):
            out.append(pltpu.make_async_remote_copy(
                src_ref=ins[n + a], dst_ref=outs[a].at[2 * x + y], send_sem=send.at[3 * n + a],
                recv_sem=recv.at[3 * n + a], device_id=(x, y, 1 - c), device_id_type=MESH))
        return out

    def start(ins, outs, sems):
        for cp in copies(ins, outs, sems, False):
            cp.start()

    def finish(ins, outs, sems):
        for cp in copies(ins, outs, sems, True):
            cp.wait_recv()
        for cp in copies(ins, outs, sems, False):
            cp.wait_send()

    return _Comm(list(bufs) + list(shards), [jax.ShapeDtypeStruct(a.shape, a.dtype) for a in bufs],
                 _dma_sems(4 * n, 4 * n), start, finish, aliases={a: a for a in range(n)})


def _swap_rows(packs):
    n = len(packs)

    def copies(ins, outs, sems):
        send, recv = sems
        x, y, c = _place()
        return [pltpu.make_async_remote_copy(
            src_ref=_half_of(ins[a], 1 - c, (slice(None),)), dst_ref=outs[a], send_sem=send.at[a],
            recv_sem=recv.at[a], device_id=(x, y, 1 - c), device_id_type=MESH) for a in range(n)]

    def start(ins, outs, sems):
        for cp in copies(ins, outs, sems):
            cp.start()

    def finish(ins, outs, sems):
        for cp in copies(ins, outs, sems):
            cp.wait()

    return _Comm(packs, [jax.ShapeDtypeStruct(_half_shape(a.shape), a.dtype) for a in packs], _dma_sems(n, n),
                 start, finish)


def _exchange(arrs):
    n = len(arrs)

    def copies(ins, outs, sems):
        send, recv = sems
        x, y, c = _place()
        out = []
        for r in (1, 2, 3):
            cx, cy = _flip(x, y, r)
            for a in range(n):
                out.append(pltpu.make_async_remote_copy(
                    src_ref=ins[a].at[r - 1], dst_ref=outs[a].at[r - 1], send_sem=send.at[(r - 1) * n + a],
                    recv_sem=recv.at[(r - 1) * n + a], device_id=(cx, cy, c), device_id_type=MESH))
        return out

    def start(ins, outs, sems):
        for cp in copies(ins, outs, sems):
            cp.start()

    def finish(ins, outs, sems):
        for cp in copies(ins, outs, sems):
            cp.wait()

    return _Comm(arrs, [jax.ShapeDtypeStruct(a.shape, a.dtype) for a in arrs], _dma_sems(3 * n, 3 * n), start, finish)


def _to_sibling(arrs):
    n = len(arrs)

    def copies(ins, outs, sems):
        send, recv = sems
        x, y, c = _place()
        return [pltpu.make_async_remote_copy(
            src_ref=ins[a], dst_ref=outs[a], send_sem=send.at[a], recv_sem=recv.at[a],
            device_id=(x, y, 1 - c), device_id_type=MESH) for a in range(n)]

    def start(ins, outs, sems):
        for cp in copies(ins, outs, sems):
            cp.start()

    def finish(ins, outs, sems):
        for cp in copies(ins, outs, sems):
            cp.wait()

    return _Comm(arrs, [jax.ShapeDtypeStruct(a.shape, a.dtype) for a in arrs], _dma_sems(n, n), start, finish)


def _gather8(v, reduce, name):
    rows, w = v.shape

    def body(v_ref, out_ref, buf, send_sems, recv_sems):
        x, y, c = _place()
        me, sibling = (x, y, c), (x, y, 1 - c)
        chips = [_flip(x, y, r) for r in (1, 2, 3)]
        dst = out_ref if not reduce else buf

        def slot(px, py, pc):
            return dst.at[4 * px + 2 * py + pc]

        def copy(k, block, to, src=None):
            return pltpu.make_async_remote_copy(
                src_ref=slot(*block) if src is None else src, dst_ref=slot(*block), send_sem=send_sems.at[k],
                recv_sem=recv_sems.at[k], device_id=to, device_id_type=MESH)

        dst[4 * x + 2 * y + c] = v_ref[...]
        first = [copy(0, me, sibling, src=v_ref)]
        first += [copy(1 + j, me, (*chip, c), src=v_ref) for j, chip in enumerate(chips)]
        for cp in first:
            cp.start()
        passed = [copy(4 + j, (*chip, c), sibling) for j, chip in enumerate(chips)]
        for j, chip in enumerate(chips):
            copy(1 + j, (*chip, c), me).wait_recv()
            passed[j].start()
        copy(0, sibling, me).wait_recv()
        for j, chip in enumerate(chips):
            copy(4 + j, (*chip, 1 - c), me).wait_recv()
        for cp in first + passed:
            cp.wait_send()
        if reduce:
            acc = buf[0]
            for d in range(1, 8):
                acc = acc + buf[d]
            out_ref[...] = acc

    vm = pl.BlockSpec(memory_space=pltpu.VMEM)
    scratch = [pltpu.VMEM((8, rows, w) if reduce else (8, 8, 128), F32), pltpu.SemaphoreType.DMA((7,)),
               pltpu.SemaphoreType.DMA((7,))]
    out_shape = jax.ShapeDtypeStruct((rows, w) if reduce else (8, rows, w), F32)
    return _pcall(body, in_specs=[vm], out_specs=vm, out_shape=out_shape, scratch_shapes=scratch,
                  compiler_params=pltpu.CompilerParams(vmem_limit_bytes=VMEM_LIMIT), name=name)(v)


SMALL = [("norm_mix", 1024), ("b_gate", 2048), ("conv_b", 4096), ("dt_bias", 32), ("a_log", 32), ("d_skip", 32),
         ("ssm_norm", 2048), ("norm_xa", 1024), ("norm_mem", 1024), ("norm_mlp", 1024)]


def _rows_of(width):
    return max(1, width // 1024)


def _pack_rows(pieces):
    out = []
    for p in pieces:
        p = p.astype(F32)
        if p.shape[-1] < 1024:
            p = jnp.pad(p, ((0, 0), (0, 1024 - p.shape[-1])))
        out.append(p.reshape(-1, 1024))
    cat = jnp.concatenate(out, axis=0)
    pad = (-cat.shape[0]) % 8
    return jnp.pad(cat, ((0, pad), (0, 0))) if pad else cat


def _unpack_rows(packed, widths_rows):
    out, at = [], 0
    for r, w in widths_rows:
        k = r * _rows_of(w)
        p = packed[at:at + k]
        at += k
        out.append(p[:, :w] if w < 1024 else p.reshape(r, w))
    return out


def _to_cat(wt):
    pieces = [wt[O_XBC:O_XBC + 4096], wt[O_Q:O_Q + 512], wt[O_K:O_K + 512], wt[O_DT:O_DT + 32],
              jnp.zeros((DT_PAD - 32, wt.shape[1]), wt.dtype), wt[O_Z:O_Z + 2048], wt[O_GATES:O_GATES + 2048],
              wt[O_V:O_V + 1024], wt[O_G:O_G + 1024]]
    return jnp.concatenate(pieces, axis=0)


def _from_cat(gt):
    pieces = [gt[C_Q:C_Q + 512], gt[C_K:C_K + 512], gt[C_V:C_V + 1024], gt[C_G:C_G + 1024],
              gt[C_Z:C_Z + 2048], gt[C_XBC:C_XBC + 4096], gt[C_DT:C_DT + 32], gt[C_GATES:C_GATES + 2048]]
    return jnp.concatenate(pieces, axis=0)


PACK_ROWS = [("mlp_w1", 1024), ("mlp_w2", 1024), ("w_br_ssm", 512), ("w_br_ret", 256), ("w_out", 256), ("xa_wq", 256),
             ("xa_wo", 256)]
PACK_N = sum(r for _, r in PACK_ROWS)


def kernel(x, mem, positions, norm_mix, w_in, b_gate, conv_w, conv_b, dt_bias, a_log, d_skip, ssm_norm, w_br_ret, w_br_ssm, w_out, norm_xa, norm_mem, xa_wq, xa_wkv, xa_wo, norm_mlp, mlp_w1, mlp_w2, norm_final, loss_target, m_norm_mix, m_w_in, m_b_gate, m_conv_w, m_conv_b, m_dt_bias, m_a_log, m_d_skip, m_ssm_norm, m_w_br_ret, m_w_br_ssm, m_w_out, m_norm_xa, m_norm_mem, m_xa_wq, m_xa_wkv, m_xa_wo, m_norm_mlp, m_mlp_w1, m_mlp_w2, m_norm_final, v_norm_mix, v_w_in, v_b_gate, v_conv_w, v_conv_b, v_dt_bias, v_a_log, v_d_skip, v_ssm_norm, v_w_br_ret, v_w_br_ssm, v_w_out, v_norm_xa, v_norm_mem, v_xa_wq, v_xa_wkv, v_xa_wo, v_norm_mlp, v_mlp_w1, v_mlp_w2, v_norm_final):
    W = dict(norm_mix=norm_mix, w_in=w_in, b_gate=b_gate, conv_w=conv_w, conv_b=conv_b, dt_bias=dt_bias, a_log=a_log,
             d_skip=d_skip, ssm_norm=ssm_norm, w_br_ret=w_br_ret, w_br_ssm=w_br_ssm, w_out=w_out, norm_xa=norm_xa,
             norm_mem=norm_mem, xa_wq=xa_wq, xa_wkv=xa_wkv, xa_wo=xa_wo, norm_mlp=norm_mlp, mlp_w1=mlp_w1,
             mlp_w2=mlp_w2, norm_final=norm_final)
    M = dict(norm_mix=m_norm_mix, w_in=m_w_in, b_gate=m_b_gate, conv_w=m_conv_w, conv_b=m_conv_b, dt_bias=m_dt_bias,
             a_log=m_a_log, d_skip=m_d_skip, ssm_norm=m_ssm_norm, w_br_ret=m_w_br_ret, w_br_ssm=m_w_br_ssm,
             w_out=m_w_out, norm_xa=m_norm_xa, norm_mem=m_norm_mem, xa_wq=m_xa_wq, xa_wkv=m_xa_wkv, xa_wo=m_xa_wo,
             norm_mlp=m_norm_mlp, mlp_w1=m_mlp_w1, mlp_w2=m_mlp_w2, norm_final=m_norm_final)
    V = dict(norm_mix=v_norm_mix, w_in=v_w_in, b_gate=v_b_gate, conv_w=v_conv_w, conv_b=v_conv_b, dt_bias=v_dt_bias,
             a_log=v_a_log, d_skip=v_d_skip, ssm_norm=v_ssm_norm, w_br_ret=v_w_br_ret, w_br_ssm=v_w_br_ssm,
             w_out=v_w_out, norm_xa=v_norm_xa, norm_mem=v_norm_mem, xa_wq=v_xa_wq, xa_wkv=v_xa_wkv, xa_wo=v_xa_wo,
             norm_mlp=v_norm_mlp, mlp_w1=v_mlp_w1, mlp_w2=v_mlp_w2, norm_final=v_norm_final)
    nl = w_in.shape[0]
    s = x.shape[1]
    x0 = x[0]
    mem2 = mem[0]
    tgt = loss_target[0]
    blk = 2 * lax.axis_index("x") + lax.axis_index("y")

    groups = [["w_in"], ["xa_wkv", "w_br_ret", "w_br_ssm", "w_out", "xa_wq", "xa_wo"], ["mlp_w1", "mlp_w2"]]
    big = groups[0] + groups[1] + groups[2]
    blk = blk.astype(jnp.int32)
    tr_ = lambda a: jnp.swapaxes(a, 1, 2)
    wb = {k: (tr_(W[k]) if k == "w_in" else W[k]).astype(BF16) for k in big}

    def shards_of(l, ks):
        return [wb[k][l] for k in ks]

    landed = _run_comm(_gather_ici(shards_of(0, big)), "gather_weights")
    gl = dict(zip(big, _run_comm(_gather_d2d(landed, shards_of(0, big)), "gather_weights_cores")))
    cw_all = _gather8(conv_w.reshape(nl * SSM_CONV, 1024), False, "gather_conv_w")
    cw_full = cw_all.reshape(4, 2, nl, SSM_CONV, 1024)[:, 0].transpose(1, 2, 0, 3).reshape(nl, SSM_CONV, SSM_CONV_DIM)

    offs = {}
    at = 0
    for k, r in PACK_ROWS:
        offs[k] = (at, r)
        at += r


    inv_freq = ROPE_THETA ** (-jnp.arange(0, RET_QK_DIM, 2, dtype=F32) / RET_QK_DIM)
    ang = positions.astype(F32)[0][:, None] * inv_freq
    cos, sin = jnp.cos(ang), jnp.sin(ang)
    cosf = jnp.concatenate([cos, cos], axis=1)
    sinf = jnp.concatenate([-sin, sin], axis=1)
    dm, qd, kd, cd = (jnp.asarray(c) for c in _ret_constants())
    eye, blkm = (jnp.asarray(c) for c in _ssd_constants())
    consts = (dm, qd, kd, cd, eye, blkm)
    e_np = _head_expand()
    eexp = jnp.asarray(e_np, BF16)
    eexp_t = jnp.asarray(e_np.T.copy(), BF16)

    saved = []
    xcur = x0
    for l in range(nl):
        rows_weight = lambda k: gl[k].reshape(-1, D_MODEL)
        wcat = _to_cat(gl["w_in"].reshape(IN_DIM, D_MODEL))
        wr, ws, wo = rows_weight("w_br_ret"), rows_weight("w_br_ssm"), rows_weight("w_out")
        wq, wxo, w2 = rows_weight("xa_wq"), rows_weight("xa_wo"), rows_weight("mlp_w2")
        w1, wkv = gl["mlp_w1"], gl["xa_wkv"]
        cw, cb = cw_full[l], conv_b[l][None]
        dtb = jnp.pad(dt_bias[l], (0, 128 - SSM_HEADS))[None]
        ax = jnp.repeat(-jnp.exp(a_log[l]), 64)[None]
        dsk = jnp.repeat(d_skip[l], 64)[None]
        bg, sn = b_gate[l][None], ssm_norm[l][None]
        more = l + 1 < nl
        ici = [_gather_ici(shards_of(l + 1, ks)) for ks in groups] if more else [None] * 3
        _carry(ici[0])
        proj, u = _nmm(xcur, norm_mix[l][None], wcat, "in_proj", save_u=True, w_rows=True)
        _carry(ici[1])
        qr, kr, xc, dtx = _prescan(proj, cosf, sinf, cw, cb, dtb, eexp, "prescan")
        _carry(ici[2])
        yr, ys, sst, hst = _scan_fwd(qr, kr, proj, xc, dtx, ax, consts, "scan_fwd")
        x1 = _postscan_fwd(xcur, yr, ys, xc, proj, bg, dsk, sn, wr, ws, wo, "postscan")
        kv = _bf(_nmm(mem2, norm_mem[l][None], wkv, "mem_kv"))
        x2 = _xattn_fwd(x1, norm_xa[l][None], wq, kv, wxo, "xattn")
        cores = (_gather_d2d(ici[0].results + ici[1].results + ici[2].results, shards_of(l + 1, big))
                 if more else None)
        _carry(cores)
        x3 = _mlp_fwd(x2, norm_mlp[l][None], w1, w2, "mlp")
        if more:
            gl = dict(zip(big, cores.results))
        saved.append(dict(x0=xcur, x1=x1, x2=x2, proj=proj, u=u, qr=qr, kr=kr, xc=xc, dtx=dtx, yr=yr, ys=ys, sst=sst,
                          hst=hst, kv=kv, wcat=wcat, wr=wr, ws=ws, wo=wo, wq=wq, wxo=wxo, w1=w1, w2=w2, wkv=wkv, cw=cw,
                          cb=cb, dtb=dtb, ax=ax, dsk=dsk, bg=bg, sn=sn))
        xcur = x3

    dx, loss_acc, dnf = _final(xcur, norm_final[None], tgt, "final")
    loss = lax.psum(loss_acc[0, 0], ("x", "y", "c"))

    small_g = [None] * nl
    c = lax.axis_index("c")
    half_id = c.astype(jnp.int32)[None]
    sel_own = blk[None]
    sel_rem = jnp.stack([blk ^ 1, blk ^ 2, blk ^ 3])
    layer_grads = {k: [None] * nl for k in big}

    def pair_sums(packs, got):
        own = [_pair_sum(p, g_, sel_own, half_id, F32, "chip_sum_own")[0] for p, g_ in zip(packs, got)]
        out_b = [_pair_sum(p, g_, sel_rem, half_id, BF16, "chip_sum_send") for p, g_ in zip(packs, got)]
        return own, out_b

    def totals(own, inc):
        return [_sum_cast([o, i_[0], i_[1], i_[2]], F32, "grads_total") for o, i_ in zip(own, inc)]

    def finish_layer(lr, red_half, sib_half):
        def whole(i, axis):
            mine_, theirs_ = red_half[i], sib_half[i]
            return jnp.concatenate([jnp.where(c == 0, mine_, theirs_), jnp.where(c == 0, theirs_, mine_)], axis=axis)

        full1 = whole(0, _split_axis(PACK_N))
        for k, r in PACK_ROWS:
            layer_grads[k][lr] = full1[offs[k][0]:offs[k][0] + r]
        layer_grads["w_in"][lr] = whole(1, _split_axis(IN_DIM // 4))
        layer_grads["xa_wkv"][lr] = whole(2, _split_axis(D_MODEL))

    riding = None
    for l in reversed(range(nl)):
        sv = saved[l]
        swap = _swap_rows(riding[1]) if riding else None
        _carry(swap)
        dx2, hm, rm, dam, dg_mlp = _mlp_bwd(sv["x2"], dx, norm_mlp[l][None], sv["w1"], sv["w2"], "mlp_bwd")
        if riding:
            own, out_b = pair_sums(riding[1], swap.results)
        pack = _mm_tn_into(hm, dam, "dw_mlp1", None, offs["mlp_w1"][0], True)
        pack = _mm_tn_into(rm, dx, "dw_mlp2", pack, offs["mlp_w2"][0], False)
        dx1, hx, dqx, ox, dkv, dg_xa = _xattn_bwd(sv["x1"], dx2, norm_xa[l][None], sv["wq"], sv["kv"], sv["wxo"],
                                                  "xattn_bwd")
        pack = _mm_tn_into(hx, dqx, "dw_xq", pack, offs["xa_wq"][0], False)
        pack = _mm_tn_into(ox, dx2, "dw_xo", pack, offs["xa_wo"][0], False)
        memn, dg_mem = _mem_bwd(mem2, norm_mem[l][None], dkv, sv["wkv"], "mem_bwd")
        dwkv = _mm_tn(memn, dkv, "dw_xkv", col_blocks=4)
        chips_a = _exchange(out_b[0:1]) if riding else None
        _carry(chips_a)
        (dyr, dys, dxs_skip, dproj, yrn, ysn, mg, dbr, dbs, dbg, ddsk, dsn) = _postscan_bwd(
            dx1, sv["yr"], sv["ys"], sv["xc"], sv["proj"], sv["bg"], sv["dsk"], sv["sn"], sv["wr"], sv["ws"], sv["wo"],
            "postscan_bwd")
        pack = _mm_tn_into(mg, dx1, "dw_out", pack, offs["w_out"][0], False)
        pack = _mm_tn_into(yrn, dbr, "dw_br_ret", pack, offs["w_br_ret"][0], False)
        pack = _mm_tn_into(ysn, dbs, "dw_br_ssm", pack, offs["w_br_ssm"][0], False)
        chips_b = _exchange(out_b[1:3]) if riding else None
        _carry(chips_b)
        dqr, dkr, dproj, dxc, gdtx, da_cols = _scan_bwd(sv["qr"], sv["kr"], sv["proj"], sv["xc"], sv["dtx"], sv["ax"],
                                                        consts, sv["sst"], sv["hst"], dyr, dys, dproj, "scan_bwd")
        if riding:
            red_half = totals(own, chips_a.results + chips_b.results)
        cores = _to_sibling(red_half) if riding else None
        _carry(cores)
        dproj, dcw, dcb, ddtb = _prescan_bwd(sv["proj"], dxc, dxs_skip, gdtx, dqr, dkr, cosf, sinf, sv["cw"], sv["cb"],
                                             sv["dtb"], eexp_t, dproj, "prescan_bwd")
        if riding:
            finish_layer(riding[0], red_half, cores.results)
        dwcat = _mm_tn(dproj, sv["u"], "dw_in")
        dx, dg_mix = _in_bwd(dproj, sv["wcat"], sv["x0"], norm_mix[l][None], dx1, "in_bwd")

        da_log = (da_cols.reshape(SSM_HEADS, 64).sum(axis=1)) * (-jnp.exp(a_log[l]))
        dd_skip = ddsk[0].reshape(SSM_HEADS, 64).sum(axis=1)
        small_g[l] = [dg_mix[0:1], dbg[0:1], dcb[0:1], ddtb[0:1, :SSM_HEADS], da_log[None], dd_skip[None], dsn[0:1],
                      dg_xa[0:1], dg_mem[0:1], dg_mlp[0:1], dcw[0::8]]
        riding = (l, [pack, _from_cat(dwcat).reshape(4, IN_DIM // 4, D_MODEL), dwkv])

    got = _run_comm(_swap_rows(riding[1]), "grads_core_swap")
    own, out_b = pair_sums(riding[1], got)
    red_half = totals(own, _run_comm(_exchange(out_b), "grads_chip_exchange"))
    finish_layer(riding[0], red_half, _run_comm(_to_sibling(red_half), "grads_core_join"))
    grad_x = dx[None]

    pieces = []
    for l in range(nl):
        pieces += small_g[l]
    pieces.append(dnf[0:1])
    small_sum = _gather8(_pack_rows(pieces), True, "reduce_small")
    layout = []
    for l in range(nl):
        layout += [(1, w) for _, w in SMALL] + [(SSM_CONV, SSM_CONV_DIM)]
    layout.append((1, 1024))
    red = _unpack_rows(small_sum, layout)
    per = len(SMALL) + 1
    g_small = {k: jnp.concatenate([red[l * per + i] for l in range(nl)], axis=0) for i, (k, _) in enumerate(SMALL)}
    g_convw_full = jnp.stack([red[l * per + len(SMALL)] for l in range(nl)])
    g_small["conv_w"] = lax.dynamic_slice_in_dim(g_convw_full, blk * 1024, 1024, axis=2)
    g_small["norm_final"] = red[-1][0]

    grads = dict(g_small)
    for k in big:
        grads[k] = jnp.stack(layer_grads[k])

    delta, new_m, new_v = {}, {}, {}
    for k in ["xa_wkv"] + [k for k, _ in PACK_ROWS]:
        delta[k], new_m[k], new_v[k] = _adamw(W[k], grads[k], M[k], V[k], "adamw_" + k)
    g_in_t = grads["w_in"]
    grads["w_in"] = tr_(g_in_t)
    d_t, m_t, v_t = _adamw(tr_(w_in), g_in_t, tr_(m_w_in), tr_(v_w_in), "adamw_w_in")
    delta["w_in"], new_m["w_in"], new_v["w_in"] = tr_(d_t), tr_(m_t), tr_(v_t)
    small_names = [k for k, _ in SMALL] + ["conv_w", "norm_final"]

    def pack_small(src):
        ps = []
        for k in small_names:
            a = src[k]
            ps.append(a.reshape(-1, a.shape[-1]) if a.ndim > 1 else a[None])
        return _pack_rows(ps)

    ds_, ms_, vs_ = _adamw(pack_small(W), pack_small(grads), pack_small(M), pack_small(V), "adamw_small")
    lay2 = []
    for k in small_names:
        a = W[k]
        lay2.append((int(np.prod(a.shape[:-1])) if a.ndim > 1 else 1, a.shape[-1]))
    for src, dst in ((ds_, delta), (ms_, new_m), (vs_, new_v)):
        for k, piece in zip(small_names, _unpack_rows(src, lay2)):
            dst[k] = piece.reshape(W[k].shape)

    names = ["norm_mix", "w_in", "b_gate", "conv_w", "conv_b", "dt_bias", "a_log", "d_skip", "ssm_norm", "w_br_ret",
             "w_br_ssm", "w_out", "norm_xa", "norm_mem", "xa_wq", "xa_wkv", "xa_wo", "norm_mlp", "mlp_w1", "mlp_w2",
             "norm_final"]
    return (loss, grad_x, *[grads[n] for n in names], *[delta[n] for n in names], *[new_m[n] for n in names],
            *[new_v[n] for n in names])
```

```python
import numpy as np
import jax
import jax.numpy as jnp
from jax import lax
from jax.experimental import pallas as pl
from jax.experimental.pallas import tpu as pltpu

F32 = jnp.float32
BF16 = jnp.bfloat16
MESH = pl.DeviceIdType.MESH

D_MODEL = 1024
CHUNK = 64
EPS = 1e-6
RET_HEADS = 4
RET_QK_DIM = 128
RET_V_DIM = 256
RET_QK = 512
RET_V = 1024
ROPE_THETA = 10000.0
SSM_INNER = 2048
SSM_HEADS = 32
SSM_GROUPS = 8
SSM_STATE = 128
SSM_CONV = 4
SSM_BC = 1024
SSM_CONV_DIM = 4096
XA_HEADS = 4
XA_HEAD_DIM = 256
D_FF = 4096
GROUP_W = 256

DT_PAD = 1024
IN_DIM = 11296
WIN_SPLIT = 1696
NP = 12288
C_XBC, C_Q, C_K, C_DT, C_Z, C_GATES, C_V, C_G = 0, 4096, 4608, 5120, 6144, 8192, 10240, 11264
O_Q, O_K, O_V, O_G, O_Z, O_XBC, O_DT, O_GATES = 0, 512, 1024, 2048, 3072, 5120, 9216, 9248

ADAM_LR = 0.001
ADAM_B1 = 0.9
ADAM_B2 = 0.999
ADAM_EPS = 1e-08
ADAM_WD = 0.01
ADAM_STEP = 10

VMEM_LIMIT = 56 * 1024 * 1024


def _params(*sem):
    return pltpu.CompilerParams(dimension_semantics=sem, vmem_limit_bytes=VMEM_LIMIT)


_CARRY = []


def _carry(comm):
    if comm is not None:
        _CARRY.append(comm)


def _pcall(body, **kw):
    if _CARRY:
        return _hosted(body, _CARRY.pop(), kw)
    return pl.pallas_call(body, **kw)


class _Comm:
    def __init__(self, ins, out_shapes, sems, start, finish, aliases=None):
        self.ins, self.out_shapes, self.sems = list(ins), list(out_shapes), list(sems)
        self.start, self.finish, self.aliases = start, finish, dict(aliases or {})
        self.results = None


def _hosted(body, comm, kw):
    in_specs = list(kw.pop("in_specs"))
    out_specs, out_shape = kw.pop("out_specs"), kw.pop("out_shape")
    single = not isinstance(out_shape, (list, tuple))
    if single:
        out_specs, out_shape = [out_specs], [out_shape]
    out_specs, out_shape = list(out_specs), list(out_shape)
    scratch = list(kw.pop("scratch_shapes", []))
    grid = tuple(kw.get("grid", ()))
    aliases = dict(kw.pop("input_output_aliases", {}))
    n_in, n_out, n_sc = len(in_specs), len(out_shape), len(scratch)
    c_in, c_out = len(comm.ins), len(comm.out_shapes)
    for i, o in comm.aliases.items():
        aliases[n_in + i] = n_out + o
    kw["compiler_params"] = _params(*(["arbitrary"] * len(grid)))

    def wrapped(*refs):
        at = 0
        parts = []
        for cnt in (n_in, c_in, n_out, c_out, n_sc):
            parts.append(refs[at:at + cnt])
            at += cnt
        a, ci, b, co, s = parts
        cs = refs[at:]
        first, last = None, None
        for d, size in enumerate(grid):
            f, l_ = pl.program_id(d) == 0, pl.program_id(d) == size - 1
            first = f if first is None else first & f
            last = l_ if last is None else last & l_

        @pl.when(first)
        def _():
            comm.start(ci, co, cs)

        body(*a, *b, *s)

        @pl.when(last)
        def _():
            comm.finish(ci, co, cs)

    call = _pcall(wrapped, in_specs=in_specs + [ANY] * c_in, out_specs=out_specs + [ANY] * c_out,
                  out_shape=out_shape + comm.out_shapes, scratch_shapes=scratch + comm.sems,
                  input_output_aliases=aliases, **kw)

    def run(*ops):
        res = call(*ops, *comm.ins)
        comm.results = list(res[n_out:])
        return res[0] if single else list(res[:n_out])

    return run


def _run_comm(comm, name):
    def body(*refs):
        c_in, c_out = len(comm.ins), len(comm.out_shapes)
        ci, co, cs = refs[:c_in], refs[c_in:c_in + c_out], refs[c_in + c_out:]
        comm.start(ci, co, cs)
        comm.finish(ci, co, cs)

    aliases = {i: o for i, o in comm.aliases.items()}
    res = _pcall(body, in_specs=[ANY] * len(comm.ins), out_specs=[ANY] * len(comm.out_shapes),
                 out_shape=comm.out_shapes, scratch_shapes=comm.sems, input_output_aliases=aliases, name=name)(*comm.ins)
    comm.results = list(res)
    return comm.results


def _bf(a):
    return a.astype(BF16)


def _dot(a, b):
    return jnp.dot(_bf(a), _bf(b), preferred_element_type=F32)


def _dot_nt(a, b):
    return lax.dot_general(_bf(a), _bf(b), (((1,), (1,)), ((), ())), preferred_element_type=F32)


def _dot_tn(a, b):
    return lax.dot_general(_bf(a), _bf(b), (((0,), (0,)), ((), ())), preferred_element_type=F32)


def _colsum(a):
    return jnp.sum(a, axis=0, keepdims=True)


def _rstd(x):
    return lax.rsqrt(jnp.mean(x * x, axis=-1, keepdims=True) + EPS)


def _rms_bwd(dy, x, rstd):
    xh = x * rstd
    return rstd * (dy - xh * jnp.mean(dy * xh, axis=-1, keepdims=True))


def _sigmoid(x):
    return 1.0 / (1.0 + jnp.exp(-x))


def _silu_and_grad(x):
    s = _sigmoid(x)
    return x * s, s + x * s * (1.0 - s)


def _softplus(x):
    u = jnp.exp(-jnp.abs(x))
    l1p = jnp.where(u < 1e-4, u * (1.0 - 0.5 * u), jnp.log(1.0 + u))
    return jnp.maximum(x, 0.0) + l1p


def _split3_dot(a, e):
    hi = a.astype(BF16)
    r1 = a - hi.astype(F32)
    mid = r1.astype(BF16)
    lo = (r1 - mid.astype(F32)).astype(BF16)
    return (jnp.dot(hi, e, preferred_element_type=F32) + jnp.dot(mid, e, preferred_element_type=F32)
            + jnp.dot(lo, e, preferred_element_type=F32))


def _cumsum_rows(a):
    rows = lax.broadcasted_iota(jnp.int32, a.shape, 0)
    s = 1
    while s < a.shape[0]:
        a = a + jnp.where(rows >= s, pltpu.roll(a, s, 0), 0.0)
        s *= 2
    return a


def _revcumsum_rows(a):
    n = a.shape[0]
    rows = lax.broadcasted_iota(jnp.int32, a.shape, 0)
    s = 1
    while s < n:
        a = a + jnp.where(rows < n - s, pltpu.roll(a, n - s, 0), 0.0)
        s *= 2
    return a


def _rms_groups(y, width):
    out = []
    for h in range(y.shape[1] // width):
        slab = y[:, h * width:(h + 1) * width]
        out.append((slab, _rstd(slab)))
    return out


def _ret_constants():
    idx = np.arange(CHUNK, dtype=np.float32)
    lg = np.log1p(-(np.float32(2.0) ** (np.float32(-5.0) - np.arange(RET_HEADS, dtype=np.float32)))).astype(np.float32)
    rel = np.abs(idx[:, None] - idx[None, :])
    dm = np.exp(lg[:, None, None] * rel).astype(np.float32)
    qd = np.exp(lg[None, :] * (idx[:, None] + 1.0)).astype(np.float32)
    kd = np.exp(lg[None, :] * (CHUNK - 1.0 - idx[:, None])).astype(np.float32)
    cd = np.exp(lg * CHUNK).astype(np.float32)
    qd = np.repeat(qd, RET_QK_DIM, axis=1)
    kd = np.repeat(kd, RET_QK_DIM, axis=1)
    cd = np.repeat(cd, RET_QK_DIM)[:, None] * np.ones((1, RET_V_DIM), np.float32)
    return dm, qd, kd, cd.astype(np.float32)


def _ssd_constants():
    eye = np.tile(np.eye(CHUNK, dtype=np.float32), (1, GROUP_W // CHUNK))
    blk = np.kron(np.eye(GROUP_W // CHUNK, dtype=np.float32), np.ones((CHUNK, CHUNK), np.float32))
    return eye, blk


def _head_expand():
    e = np.zeros((128, SSM_INNER), np.float32)
    for h in range(SSM_HEADS):
        e[h, h * 64:(h + 1) * 64] = 1.0
    return e


def _ret_chunk_fwd(qh, kh, vh, sh, dmh, qdh, kdh, cdh):
    a = _dot_nt(qh, kh) * dmh
    y = _dot(a, vh) + _dot(qh * qdh, sh)
    s_new = sh * cdh + _dot_tn(kh * kdh, vh)
    return y, s_new


def _ret_chunk_bwd(qh, kh, vh, sh, dmh, qdh, kdh, cdh, dy, ds_new):
    a = _dot_nt(qh, kh) * dmh
    dp = _dot_nt(dy, vh) * dmh
    dq = _dot(dp, kh) + _dot_nt(dy, sh) * qdh
    dk = _dot_tn(dp, qh) + _dot_nt(vh, ds_new) * kdh
    dv = _dot_tn(a, dy) + _dot(kh * kdh, ds_new)
    ds = cdh * ds_new + _dot_tn(qh * qdh, dy)
    return dq, dk, dv, ds


def _ssd_common(xs, dtx, ax, eye):
    cum = _cumsum_rows(dtx * ax)
    last = cum[CHUNK - 1:CHUNK, :]
    r = _colsum(jnp.where(eye > 0.5, cum, 0.0))
    return cum, last, r, xs * dtx


def _tile4(a):
    return jnp.concatenate([a, a, a, a], axis=0)


def _ssd_chunk_fwd(xs, dtx, b, c, ax, hg, eye, blk):
    cum, last, r, x = _ssd_common(xs, dtx, ax, eye)
    lam = jnp.exp(-jnp.abs(cum - r))
    wc = _dot_nt(c, _tile4(b)) * lam
    bd = _tile4(x) * blk
    y = _dot(wc, bd) + _dot(c, hg) * jnp.exp(cum)
    h_new = hg * jnp.exp(last) + _dot_tn(b, x * jnp.exp(last - cum))
    return y, h_new


def _ssd_chunk_bwd(xs, dtx, b, c, ax, hg, eye, blk, dy, dh_new):
    cum, last, r, x = _ssd_common(xs, dtx, ax, eye)
    delta = cum - r
    lam = jnp.exp(-jnp.abs(delta))
    b4 = _tile4(b)
    cb4 = _dot_nt(c, b4)
    wc = cb4 * lam
    bd = _tile4(x) * blk
    ecx = jnp.exp(cum)
    wl = jnp.exp(last - cum)
    ecl = jnp.exp(last)
    z = _dot(c, hg)
    dwc = _dot_nt(dy, bd)
    dbd = _dot_tn(wc, dy) * blk
    dx = dbd[0:64] + dbd[64:128] + dbd[128:192] + dbd[192:256]
    dt_ = _dot(b, dh_new)
    dx = dx + dt_ * wl
    dcb4 = dwc * lam
    dz = dy * ecx
    dc = _dot(dcb4, b4) + _dot_nt(dz, hg)
    db4 = _dot_tn(dcb4, c)
    db = db4[0:64] + db4[64:128] + db4[128:192] + db4[192:256] + _dot_nt(x * wl, dh_new)
    g = dwc * cb4 * lam * (-jnp.sign(delta))
    dr = -_colsum(g)
    dwl = dt_ * x * wl
    u = g + eye * dr + dy * z * ecx - dwl
    lastrow = _colsum(dwl) + _colsum(dh_new * hg) * ecl
    rows = lax.broadcasted_iota(jnp.int32, u.shape, 0)
    u = u + jnp.where(rows == CHUNK - 1, lastrow, 0.0)
    dh = _dot_tn(c, dz) + dh_new * ecl
    rc = _revcumsum_rows(u)
    dxs = dx * dtx
    g_dtx = dx * xs + rc * ax
    da = _colsum(rc * dtx)
    return dxs, g_dtx, db, dc, da, dh


def _row_tile(s, want):
    t = min(s, want)
    assert s % t == 0
    return t


def _nmm(x, gain, w, name, tn=1024, save_u=False, w_rows=False):
    s, d = x.shape
    blocked = w.ndim == 3
    if blocked:
        tn = w.shape[2]
        n = w.shape[0] * tn
        w_spec = pl.BlockSpec((1, d, tn), lambda i, j: (j, 0, 0))
    elif w_rows:
        n = w.shape[0]
        w_spec = pl.BlockSpec((tn, d), lambda i, j: (j, 0))
    else:
        n = w.shape[1]
        w_spec = pl.BlockSpec((d, tn), lambda i, j: (0, j))
    tm = _row_tile(s, 1024)
    assert n % tn == 0

    def body(x_ref, g_ref, w_ref, *rest):
        o_ref, u_sc = rest[0], rest[-1]

        @pl.when(pl.program_id(1) == 0)
        def _():
            xx = x_ref[...]
            u = _bf((xx * _rstd(xx)) * g_ref[...])
            u_sc[...] = u
            if save_u:
                rest[1][...] = u

        if w_rows:
            o_ref[...] = _dot_nt(u_sc[...], w_ref[...])
        else:
            o_ref[...] = jnp.dot(u_sc[...], w_ref[0] if blocked else w_ref[...], preferred_element_type=F32)

    out_shape = [jax.ShapeDtypeStruct((s, n), F32)]
    out_specs = [pl.BlockSpec((tm, tn), lambda i, j: (i, j))]
    if save_u:
        out_shape.append(jax.ShapeDtypeStruct((s, d), BF16))
        out_specs.append(pl.BlockSpec((tm, d), lambda i, j: (i, 0)))
    res = _pcall(
        body, grid=(s // tm, n // tn),
        in_specs=[pl.BlockSpec((tm, d), lambda i, j: (i, 0)), pl.BlockSpec((1, d), lambda i, j: (0, 0)), w_spec],
        out_specs=out_specs, out_shape=out_shape, scratch_shapes=[pltpu.VMEM((tm, d), BF16)],
        compiler_params=_params("parallel", "arbitrary"), name=name)(x, gain, w)
    return res if save_u else res[0]


def _mm_tn(a, b, name, tm=1024, tn=1024, col_blocks=None):
    k, m = a.shape
    n = b.shape[1]
    tk = _row_tile(k, 1024)
    tm, tn = min(tm, m), min(tn, n)
    if col_blocks:
        tn = n // col_blocks
    assert m % tm == 0 and n % tn == 0
    nk = k // tk

    def body(a_ref, b_ref, o_ref, acc):
        kk = pl.program_id(2)

        @pl.when(kk == 0)
        def _():
            acc[...] = jnp.zeros_like(acc)

        acc[...] += _dot_tn(a_ref[...], b_ref[...])

        @pl.when(kk == nk - 1)
        def _():
            if col_blocks:
                o_ref[0] = acc[...]
            else:
                o_ref[...] = acc[...]

    if col_blocks:
        out_spec = pl.BlockSpec((1, tm, tn), lambda i, j, kk: (j, i, 0))
        out_shape = jax.ShapeDtypeStruct((col_blocks, m, tn), F32)
    else:
        out_spec = pl.BlockSpec((tm, tn), lambda i, j, kk: (i, j))
        out_shape = jax.ShapeDtypeStruct((m, n), F32)
    return _pcall(
        body, grid=(m // tm, n // tn, nk),
        in_specs=[pl.BlockSpec((tk, tm), lambda i, j, kk: (kk, i)), pl.BlockSpec((tk, tn), lambda i, j, kk: (kk, j))],
        out_specs=out_spec, out_shape=out_shape,
        scratch_shapes=[pltpu.VMEM((tm, tn), F32)],
        compiler_params=_params("parallel", "parallel", "arbitrary"), name=name)(a, b)


def _mm_tn_into(a, b, name, pack, off, by_cols):
    k, m = a.shape
    n = b.shape[1]
    tk = _row_tile(k, 1024)
    nk = k // tk
    rows = m if by_cols else m // 4
    tm = min(m, 1024)
    nb = 1 if by_cols else tm // rows
    assert tm == nb * rows and off % rows == 0 and n == (4096 if by_cols else 1024)

    def body(a_ref, b_ref, *rest):
        o_ref, acc = rest[-2], rest[-1]
        kk = pl.program_id(2)

        @pl.when(kk == 0)
        def _():
            acc[...] = jnp.zeros_like(acc)

        acc[...] += _dot_tn(a_ref[...], b_ref[...])

        @pl.when(kk == nk - 1)
        def _():
            o_ref[...] = acc[...].reshape(nb, rows, 1024)

    if by_cols:
        out_spec = pl.BlockSpec((1, rows, 1024), lambda i, j, kk: (j, off // rows, 0))
    else:
        out_spec = pl.BlockSpec((nb, rows, 1024), lambda i, j, kk: (i, off // rows, 0))
    in_specs = [pl.BlockSpec((tk, tm), lambda i, j, kk: (kk, i)), pl.BlockSpec((tk, 1024), lambda i, j, kk: (kk, j))]
    ops, alias = [a, b], {}
    if pack is not None:
        in_specs.append(ANY)
        ops.append(pack)
        alias = {2: 0}
    return _pcall(
        body, grid=(m // tm, n // 1024, nk), in_specs=in_specs, out_specs=out_spec,
        out_shape=jax.ShapeDtypeStruct((4, PACK_N, 1024), F32), scratch_shapes=[pltpu.VMEM((tm, 1024), F32)],
        input_output_aliases=alias, compiler_params=_params("parallel", "parallel", "arbitrary"), name=name)(*ops)


def _in_bwd(dproj, wcat_t, x, gain, dres, name):
    s, n = dproj.shape
    d = wcat_t.shape[1]
    tm = _row_tile(s, 1024)
    tk = 1024
    nk = n // tk
    ns = s // tm

    def body(dp_ref, w_ref, x_ref, g_ref, dr_ref, dx_ref, dg_ref, acc):
        i, kk = pl.program_id(0), pl.program_id(1)

        @pl.when(kk == 0)
        def _():
            acc[...] = jnp.zeros_like(acc)

        @pl.when((kk == 0) & (i == 0))
        def _():
            dg_ref[...] = jnp.zeros_like(dg_ref)

        acc[...] += _dot(dp_ref[...], w_ref[...])

        @pl.when(kk == nk - 1)
        def _():
            xx = x_ref[...]
            r = _rstd(xx)
            du = acc[...]
            dg_ref[...] += _colsum(du * (xx * r))
            dx_ref[...] = dr_ref[...] + _rms_bwd(du * g_ref[...], xx, r)

    return _pcall(
        body, grid=(ns, nk),
        in_specs=[pl.BlockSpec((tm, tk), lambda i, kk: (i, kk)), pl.BlockSpec((tk, d), lambda i, kk: (kk, 0)),
                  pl.BlockSpec((tm, d), lambda i, kk: (i, 0)), pl.BlockSpec((1, d), lambda i, kk: (0, 0)),
                  pl.BlockSpec((tm, d), lambda i, kk: (i, 0))],
        out_specs=[pl.BlockSpec((tm, d), lambda i, kk: (i, 0)), pl.BlockSpec((8, d), lambda i, kk: (0, 0))],
        out_shape=[jax.ShapeDtypeStruct((s, d), F32), jax.ShapeDtypeStruct((8, d), F32)],
        scratch_shapes=[pltpu.VMEM((tm, d), F32)],
        compiler_params=_params("arbitrary", "arbitrary"), name=name)(dproj, wcat_t, x, gain, dres)


def _prev_rows_spec(ts, width):
    return pl.BlockSpec((8, width), lambda i: (jnp.maximum(i * (ts // 8) - 1, 0), 0))


def _prescan(proj, cosf, sinf, cw, cb, dtb, eexp, name):
    s = proj.shape[0]
    ts = _row_tile(s, 256)

    def body(xbc_ref, prev_ref, q_ref, k_ref, dt_ref, cos_ref, sin_ref, cw_ref, cb_ref, dtb_ref, e_ref,
             qo_ref, ko_ref, xc_ref, dtx_ref):
        i = pl.program_id(0)
        for st in range(SSM_CONV_DIM // 128):
            sl = slice(st * 128, (st + 1) * 128)
            prev = jnp.where(i > 0, prev_ref[:, sl], 0.0)
            xcat = jnp.concatenate([prev, xbc_ref[:, sl]], axis=0)
            pre = cb_ref[:, sl] + cw_ref[3:4, sl] * xcat[8:8 + ts]
            for j in range(3):
                pre = pre + cw_ref[j:j + 1, sl] * pltpu.roll(xcat, 3 - j, 0)[8:8 + ts]
            xc_ref[:, sl] = pre * _sigmoid(pre)
        cs, sn = cos_ref[...], sin_ref[...]
        for h in range(RET_HEADS):
            sl = slice(h * 128, (h + 1) * 128)
            qh, kh = q_ref[:, sl], k_ref[:, sl]
            qo_ref[:, sl] = qh * cs + pltpu.roll(qh, 64, 1) * sn
            ko_ref[:, sl] = (kh * cs + pltpu.roll(kh, 64, 1) * sn) * (RET_QK_DIM ** -0.5)
        dtv = _softplus(dt_ref[:, 0:128] + dtb_ref[...])
        dtx_ref[...] = _split3_dot(dtv, e_ref[...])

    row = lambda w, c: pl.BlockSpec((ts, w), lambda i: (i, c))
    full = lambda a: pl.BlockSpec(a.shape, lambda i: (0,) * a.ndim)
    return _pcall(
        body, grid=(s // ts,),
        in_specs=[row(4096, 0), _prev_rows_spec(ts, 4096), row(512, C_Q // 512), row(512, C_K // 512),
                  row(DT_PAD, C_DT // DT_PAD), row(128, 0), row(128, 0), full(cw), full(cb), full(dtb), full(eexp)],
        out_specs=[row(512, 0), row(512, 0), row(4096, 0), row(2048, 0)],
        out_shape=[jax.ShapeDtypeStruct((s, 512), F32), jax.ShapeDtypeStruct((s, 512), F32),
                   jax.ShapeDtypeStruct((s, 4096), F32), jax.ShapeDtypeStruct((s, 2048), F32)],
        compiler_params=_params("parallel"), name=name)(proj, proj, proj, proj, proj, cosf, sinf, cw, cb, dtb, eexp)


def _scan_fwd(qr, kr, proj, xc, dtx, ax, consts, name):
    s = qr.shape[0]
    nc = s // CHUNK
    dm, qd, kd, cd, eye, blk = consts

    def body(q_ref, k_ref, v_ref, xc_ref, dtx_ref, ax_ref, dm_ref, qd_ref, kd_ref, cd_ref, eye_ref, blk_ref,
             yr_ref, ys_ref, sst_ref, hst_ref, s_sc, h_sc):
        @pl.when(pl.program_id(0) == 0)
        def _():
            s_sc[...] = jnp.zeros_like(s_sc)
            h_sc[...] = jnp.zeros_like(h_sc)

        sst_ref[0] = s_sc[...]
        hst_ref[0] = h_sc[...]
        for h in range(RET_HEADS):
            ql, vl = slice(h * 128, (h + 1) * 128), slice(h * 256, (h + 1) * 256)
            y, s_new = _ret_chunk_fwd(q_ref[:, ql], k_ref[:, ql], v_ref[:, vl], s_sc[ql, :], dm_ref[h],
                                      qd_ref[:, ql], kd_ref[:, ql], cd_ref[ql, :])
            yr_ref[:, vl] = y
            s_sc[ql, :] = s_new
        eye_v, blk_v = eye_ref[...], blk_ref[...]
        for g in range(SSM_GROUPS):
            sl = slice(g * GROUP_W, (g + 1) * GROUP_W)
            bl = slice(SSM_INNER + g * 128, SSM_INNER + (g + 1) * 128)
            cl = slice(SSM_INNER + SSM_BC + g * 128, SSM_INNER + SSM_BC + (g + 1) * 128)
            y, h_new = _ssd_chunk_fwd(xc_ref[:, sl], dtx_ref[:, sl], xc_ref[:, bl], xc_ref[:, cl], ax_ref[:, sl],
                                      h_sc[:, sl], eye_v, blk_v)
            ys_ref[:, sl] = y
            h_sc[:, sl] = h_new

    row = lambda w, c=0: pl.BlockSpec((CHUNK, w), lambda i: (i, c))
    full = lambda a: pl.BlockSpec(a.shape, lambda i: (0,) * a.ndim)
    return _pcall(
        body, grid=(nc,),
        in_specs=[row(512), row(512), row(1024, C_V // 1024), row(4096), row(2048), full(ax), full(dm), full(qd),
                  full(kd), full(cd), full(eye), full(blk)],
        out_specs=[row(1024), row(2048), pl.BlockSpec((1, 512, 256), lambda i: (i, 0, 0)),
                   pl.BlockSpec((1, 128, 2048), lambda i: (i, 0, 0))],
        out_shape=[jax.ShapeDtypeStruct((s, 1024), F32), jax.ShapeDtypeStruct((s, 2048), F32),
                   jax.ShapeDtypeStruct((nc, 512, 256), F32), jax.ShapeDtypeStruct((nc, 128, 2048), F32)],
        scratch_shapes=[pltpu.VMEM((512, 256), F32), pltpu.VMEM((128, 2048), F32)],
        compiler_params=_params("arbitrary"), name=name)(qr, kr, proj, xc, dtx, ax, dm, qd, kd, cd, eye, blk)


def _scan_bwd(qr, kr, proj, xc, dtx, ax, consts, sst, hst, dyr, dys, dproj, name):
    s = qr.shape[0]
    nc = s // CHUNK
    dm, qd, kd, cd, eye, blk = consts

    def body(q_ref, k_ref, v_ref, xc_ref, dtx_ref, ax_ref, dm_ref, qd_ref, kd_ref, cd_ref, eye_ref, blk_ref,
             sst_ref, hst_ref, dyr_ref, dys_ref, dproj_in, dq_ref, dk_ref, dv_ref, dxc_ref, gdt_ref, da_ref, ds_sc,
             dh_sc):
        @pl.when(pl.program_id(0) == 0)
        def _():
            ds_sc[...] = jnp.zeros_like(ds_sc)
            dh_sc[...] = jnp.zeros_like(dh_sc)
            da_ref[...] = jnp.zeros_like(da_ref)

        for h in range(RET_HEADS):
            ql, vl = slice(h * 128, (h + 1) * 128), slice(h * 256, (h + 1) * 256)
            dq, dk, dv, ds = _ret_chunk_bwd(q_ref[:, ql], k_ref[:, ql], v_ref[:, vl], sst_ref[0, ql, :], dm_ref[h],
                                            qd_ref[:, ql], kd_ref[:, ql], cd_ref[ql, :], dyr_ref[:, vl], ds_sc[ql, :])
            dq_ref[:, ql] = dq
            dk_ref[:, ql] = dk
            dv_ref[:, vl] = _bf(dv)
            ds_sc[ql, :] = ds
        eye_v, blk_v = eye_ref[...], blk_ref[...]
        for g in range(SSM_GROUPS):
            sl = slice(g * GROUP_W, (g + 1) * GROUP_W)
            bl = slice(SSM_INNER + g * 128, SSM_INNER + (g + 1) * 128)
            cl = slice(SSM_INNER + SSM_BC + g * 128, SSM_INNER + SSM_BC + (g + 1) * 128)
            dxs, g_dtx, db, dc, da, dh = _ssd_chunk_bwd(
                xc_ref[:, sl], dtx_ref[:, sl], xc_ref[:, bl], xc_ref[:, cl], ax_ref[:, sl], hst_ref[0, :, sl],
                eye_v, blk_v, dys_ref[:, sl], dh_sc[:, sl])
            dxc_ref[:, sl] = dxs
            dxc_ref[:, bl] = db
            dxc_ref[:, cl] = dc
            gdt_ref[:, sl] = g_dtx
            da_ref[:, sl] += da
            dh_sc[:, sl] = dh

    row = lambda w, c=0: pl.BlockSpec((CHUNK, w), lambda i: (nc - 1 - i, c))
    full = lambda a: pl.BlockSpec(a.shape, lambda i: (0,) * a.ndim)
    return _pcall(
        body, grid=(nc,),
        in_specs=[row(512), row(512), row(1024, C_V // 1024), row(4096), row(2048), full(ax), full(dm), full(qd),
                  full(kd), full(cd), full(eye), full(blk),
                  pl.BlockSpec((1, 512, 256), lambda i: (nc - 1 - i, 0, 0)),
                  pl.BlockSpec((1, 128, 2048), lambda i: (nc - 1 - i, 0, 0)), row(1024), row(2048), ANY],
        out_specs=[row(512), row(512), row(1024, C_V // 1024), row(4096), row(2048),
                   pl.BlockSpec((1, 2048), lambda i: (0, 0))],
        out_shape=[jax.ShapeDtypeStruct((s, 512), F32), jax.ShapeDtypeStruct((s, 512), F32),
                   jax.ShapeDtypeStruct(dproj.shape, BF16), jax.ShapeDtypeStruct((s, 4096), F32),
                   jax.ShapeDtypeStruct((s, 2048), F32), jax.ShapeDtypeStruct((1, 2048), F32)],
        scratch_shapes=[pltpu.VMEM((512, 256), F32), pltpu.VMEM((128, 2048), F32)],
        input_output_aliases={16: 2},
        compiler_params=_params("arbitrary"), name=name)(qr, kr, proj, xc, dtx, ax, dm, qd, kd, cd, eye, blk, sst, hst,
                                                          dyr, dys, dproj)


def _mix_values(yr, g, ys, xs, z, gates, bg, dsk, sn):
    sg, dsg = _silu_and_grad(g)
    ret = _rms_groups(yr, RET_V_DIM)
    yrn = jnp.concatenate([slab * r for slab, r in ret], axis=1) * sg
    sz, dsz = _silu_and_grad(z)
    ys0 = ys + xs * dsk
    ys1 = ys0 * sz
    grp = _rms_groups(ys1, GROUP_W)
    ysh = jnp.concatenate([slab * r for slab, r in grp], axis=1)
    ysn = ysh * sn
    gg = _sigmoid(gates + bg)
    return dict(sg=sg, dsg=dsg, ret=ret, yrn=yrn, sz=sz, dsz=dsz, ys0=ys0, ys1=ys1, grp=grp, ysh=ysh, ysn=ysn,
                gr=gg[:, :D_MODEL], gs=gg[:, D_MODEL:])


def _postscan_fwd(x, yr, ys, xc, proj, bg, dsk, sn, wr, ws, wo, name):
    s = x.shape[0]
    ts = _row_tile(s, 256)

    def body(x_ref, yr_ref, ys_ref, xs_ref, g_ref, z_ref, gt_ref, bg_ref, dsk_ref, sn_ref, wr_ref, ws_ref, wo_ref,
             o_ref):
        m = _mix_values(yr_ref[...], g_ref[...], ys_ref[...], xs_ref[...], z_ref[...], gt_ref[...], bg_ref[...],
                        dsk_ref[...], sn_ref[...])
        merged = m["gr"] * _dot(m["yrn"], wr_ref[...]) + m["gs"] * _dot(m["ysn"], ws_ref[...])
        o_ref[...] = x_ref[...] + _dot(merged, wo_ref[...])

    row = lambda w, c=0: pl.BlockSpec((ts, w), lambda i: (i, c))
    full = lambda a: pl.BlockSpec(a.shape, lambda i: (0,) * a.ndim)
    return _pcall(
        body, grid=(s // ts,),
        in_specs=[row(1024), row(1024), row(2048), row(2048), row(1024, C_G // 1024), row(2048, C_Z // 2048),
                  row(2048, C_GATES // 2048), full(bg), full(dsk), full(sn), full(wr), full(ws), full(wo)],
        out_specs=row(1024), out_shape=jax.ShapeDtypeStruct((s, D_MODEL), F32),
        compiler_params=_params("parallel"), name=name)(x, yr, ys, xc, proj, proj, proj, bg, dsk, sn, wr, ws, wo)


def _postscan_bwd(dout, yr, ys, xc, proj, bg, dsk, sn, wr, ws, wo, name):
    s = dout.shape[0]
    ts = _row_tile(s, 128)

    def body(do_ref, yr_ref, ys_ref, xs_ref, g_ref, z_ref, gt_ref, bg_ref, dsk_ref, sn_ref, wr_ref, ws_ref, wo_ref,
             dyr_ref, dys_ref, dxs_ref, dproj_ref, yrn_ref, ysn_ref, mg_ref, dbr_ref, dbs_ref,
             dbg_ref, ddsk_ref, dsn_ref):
        @pl.when(pl.program_id(0) == 0)
        def _():
            dbg_ref[...] = jnp.zeros_like(dbg_ref)
            ddsk_ref[...] = jnp.zeros_like(ddsk_ref)
            dsn_ref[...] = jnp.zeros_like(dsn_ref)

        xs = xs_ref[...]
        m = _mix_values(yr_ref[...], g_ref[...], ys_ref[...], xs, z_ref[...], gt_ref[...], bg_ref[...],
                        dsk_ref[...], sn_ref[...])
        gr, gs = m["gr"], m["gs"]
        br, bs = _dot(m["yrn"], wr_ref[...]), _dot(m["ysn"], ws_ref[...])
        dmerged = _dot_nt(do_ref[...], wo_ref[...])
        dgt = jnp.concatenate([dmerged * br * gr * (1.0 - gr), dmerged * bs * gs * (1.0 - gs)], axis=1)
        dproj_ref[:, C_GATES:C_GATES + 2048] = _bf(dgt)
        dbg_ref[...] += _colsum(dgt)
        dbr, dbs = dmerged * gr, dmerged * gs
        yrn_ref[...] = _bf(m["yrn"])
        ysn_ref[...] = _bf(m["ysn"])
        mg_ref[...] = _bf(gr * br + gs * bs)
        dbr_ref[...] = _bf(dbr)
        dbs_ref[...] = _bf(dbs)
        dyrn = _dot_nt(dbr, wr_ref[...])
        dysn = _dot_nt(dbs, ws_ref[...])
        rn = jnp.concatenate([slab * r for slab, r in m["ret"]], axis=1)
        dproj_ref[:, C_G:C_G + 1024] = _bf(dyrn * rn * m["dsg"])
        drn = dyrn * m["sg"]
        dyr_ref[...] = jnp.concatenate(
            [_rms_bwd(drn[:, h * RET_V_DIM:(h + 1) * RET_V_DIM], slab, r) for h, (slab, r) in enumerate(m["ret"])], axis=1)
        dsn_ref[...] += _colsum(dysn * m["ysh"])
        dysh = dysn * sn_ref[...]
        dys1 = jnp.concatenate(
            [_rms_bwd(dysh[:, h * GROUP_W:(h + 1) * GROUP_W], slab, r) for h, (slab, r) in enumerate(m["grp"])], axis=1)
        dproj_ref[:, C_Z:C_Z + 2048] = _bf(dys1 * m["ys0"] * m["dsz"])
        dys0 = dys1 * m["sz"]
        dys_ref[...] = dys0
        dxs_ref[...] = dys0 * dsk_ref[...]
        ddsk_ref[...] += _colsum(dys0 * xs)

    row = lambda w, c=0: pl.BlockSpec((ts, w), lambda i: (i, c))
    full = lambda a: pl.BlockSpec(a.shape, lambda i: (0,) * a.ndim)
    acc = lambda w: pl.BlockSpec((8, w), lambda i: (0, 0))
    sds = jax.ShapeDtypeStruct
    return _pcall(
        body, grid=(s // ts,),
        in_specs=[row(1024), row(1024), row(2048), row(2048), row(1024, C_G // 1024), row(2048, C_Z // 2048),
                  row(2048, C_GATES // 2048), full(bg), full(dsk), full(sn), full(wr), full(ws), full(wo)],
        out_specs=[row(1024), row(2048), row(2048), row(NP), row(1024), row(2048), row(1024),
                   row(1024), row(1024), acc(2048), acc(2048), acc(2048)],
        out_shape=[sds((s, 1024), F32), sds((s, 2048), F32), sds((s, 2048), F32), sds((s, NP), BF16),
                   sds((s, 1024), BF16), sds((s, 2048), BF16),
                   sds((s, 1024), BF16), sds((s, 1024), BF16), sds((s, 1024), BF16), sds((8, 2048), F32),
                   sds((8, 2048), F32), sds((8, 2048), F32)],
        compiler_params=_params("arbitrary"), name=name)(dout, yr, ys, xc, proj, proj, proj, bg, dsk, sn, wr, ws, wo)


def _prescan_bwd(proj, dxc, dxs_skip, gdtx, dqr, dkr, cosf, sinf, cw, cb, dtb, eexp_t, dproj, name):
    s = proj.shape[0]
    ts = _row_tile(s, 256)
    nt = s // ts
    m = ts + 8
    width = C_DT + DT_PAD

    def body(xbc_ref, prev_ref, nxt_ref, dt_ref, dxc_ref, dxcn_ref, dsk_ref, dskn_ref, gdt_ref, dq_ref, dk_ref,
             cos_ref, sin_ref, cw_ref, cb_ref, dtb_ref, et_ref, dproj_in, dp_ref, dcw_ref, dcb_ref, ddtb_ref):
        i = pl.program_id(0)

        @pl.when(i == 0)
        def _():
            ddtb_ref[...] = jnp.zeros_like(ddtb_ref)
            dcw_ref[...] = jnp.zeros_like(dcw_ref)
            dcb_ref[...] = jnp.zeros_like(dcb_ref)

        rows = lax.broadcasted_iota(jnp.int32, (m, 128), 0)
        live = (rows < ts) | (i < nt - 1)
        for st in range(SSM_CONV_DIM // 128):
            sl = slice(st * 128, (st + 1) * 128)
            prev = jnp.where(i > 0, prev_ref[:, sl], 0.0)
            xcat = jnp.concatenate([prev, xbc_ref[:, sl], nxt_ref[:, sl]], axis=0)
            shifted = [pltpu.roll(xcat, 3 - j, 0) for j in range(3)] + [xcat]
            pre = cb_ref[:, sl]
            for j in range(SSM_CONV):
                pre = pre + cw_ref[j:j + 1, sl] * shifted[j][8:]
            _, dsilu = _silu_and_grad(pre)
            dxc = jnp.concatenate([dxc_ref[:, sl], dxcn_ref[:, sl]], axis=0)
            if st * 128 < SSM_INNER:
                dxc = dxc + jnp.concatenate([dsk_ref[:, sl], dskn_ref[:, sl]], axis=0)
            dpre = jnp.where(live, dxc * dsilu, 0.0)
            dpt = dpre[0:ts]
            dx = cw_ref[3:4, sl] * dpt
            for j in range(3):
                dx = dx + cw_ref[j:j + 1, sl] * pltpu.roll(dpre, m - (3 - j), 0)[0:ts]
            for j in range(SSM_CONV):
                dcw_ref[8 * j:8 * j + 8, sl] += _colsum(dpt * shifted[j][8:8 + ts])
            dcb_ref[:, sl] += _colsum(dpt)
            dp_ref[:, sl] = _bf(dx)
        cs, sn = cos_ref[...], sin_ref[...]
        for h in range(RET_HEADS):
            sl = slice(h * 128, (h + 1) * 128)
            dq = dq_ref[:, sl]
            dk = dk_ref[:, sl] * (RET_QK_DIM ** -0.5)
            dp_ref[:, C_Q + h * 128:C_Q + (h + 1) * 128] = _bf(dq * cs + pltpu.roll(dq * sn, 64, 1))
            dp_ref[:, C_K + h * 128:C_K + (h + 1) * 128] = _bf(dk * cs + pltpu.roll(dk * sn, 64, 1))
        ddt = _split3_dot(gdt_ref[...], et_ref[...])
        ddt = ddt * _sigmoid(dt_ref[:, 0:128] + dtb_ref[...])
        ddtb_ref[...] += _colsum(ddt)
        dp_ref[:, C_DT:C_DT + 128] = _bf(ddt)
        dp_ref[:, C_DT + 128:C_DT + DT_PAD] = jnp.zeros((ts, DT_PAD - 128), BF16)

    row = lambda w, c=0: pl.BlockSpec((ts, w), lambda i: (i, c))
    nxt = lambda w: pl.BlockSpec((8, w), lambda i: (jnp.minimum((i + 1) * (ts // 8), s // 8 - 1), 0))
    full = lambda a: pl.BlockSpec(a.shape, lambda i: (0,) * a.ndim)
    sds = jax.ShapeDtypeStruct
    return _pcall(
        body, grid=(nt,),
        in_specs=[row(4096), _prev_rows_spec(ts, 4096), nxt(4096), row(DT_PAD, C_DT // DT_PAD), row(4096), nxt(4096),
                  row(2048), nxt(2048), row(2048), row(512), row(512), row(128), row(128), full(cw), full(cb),
                  full(dtb), full(eexp_t), ANY],
        out_specs=[row(width), pl.BlockSpec((32, 4096), lambda i: (0, 0)), pl.BlockSpec((8, 4096), lambda i: (0, 0)),
                   pl.BlockSpec((8, 128), lambda i: (0, 0))],
        out_shape=[sds(dproj.shape, BF16), sds((32, 4096), F32), sds((8, 4096), F32), sds((8, 128), F32)],
        input_output_aliases={17: 0},
        compiler_params=_params("arbitrary"), name=name)(proj, proj, proj, proj, dxc, dxc, dxs_skip, dxs_skip, gdtx,
                                                          dqr, dkr, cosf, sinf, cw, cb, dtb, eexp_t, dproj)


def _xattn_values(x, gain, wq, kv):
    r = _rstd(x)
    h = (x * r) * gain
    q = _dot(h, wq)
    ps, os_ = [], []
    for hd in range(XA_HEADS):
        sl = slice(hd * XA_HEAD_DIM, (hd + 1) * XA_HEAD_DIM)
        sc = _dot_nt(q[:, sl], kv[:, sl]) * (XA_HEAD_DIM ** -0.5)
        e = jnp.exp(sc - jnp.max(sc, axis=-1, keepdims=True))
        p = e / jnp.sum(e, axis=-1, keepdims=True)
        ps.append(p)
        os_.append(_dot(p, kv[:, D_MODEL + hd * XA_HEAD_DIM:D_MODEL + (hd + 1) * XA_HEAD_DIM]))
    return r, h, q, ps, jnp.concatenate(os_, axis=1)


def _xattn_fwd(x, gain, wq, kv, wo, name):
    s = x.shape[0]
    ts = _row_tile(s, 256)

    def body(x_ref, g_ref, wq_ref, kv_ref, wo_ref, o_ref):
        x_ = x_ref[...]
        _, _, _, _, o = _xattn_values(x_, g_ref[...], wq_ref[...], kv_ref[...])
        o_ref[...] = x_ + _dot(o, wo_ref[...])

    row = pl.BlockSpec((ts, D_MODEL), lambda i: (i, 0))
    full = lambda a: pl.BlockSpec(a.shape, lambda i: (0,) * a.ndim)
    return _pcall(
        body, grid=(s // ts,), in_specs=[row, full(gain), full(wq), full(kv), full(wo)], out_specs=row,
        out_shape=jax.ShapeDtypeStruct((s, D_MODEL), F32), compiler_params=_params("parallel"), name=name)(
            x, gain, wq, kv, wo)


def _xattn_bwd(x, dout, gain, wq, kv, wo, name):
    s = x.shape[0]
    m = kv.shape[0]
    ts = _row_tile(s, 256)

    def body(x_ref, do_ref, g_ref, wq_ref, kv_ref, wo_ref, dx_ref, h_ref, dq_ref, o_ref, dkv_ref, dg_ref):
        @pl.when(pl.program_id(0) == 0)
        def _():
            dkv_ref[...] = jnp.zeros_like(dkv_ref)
            dg_ref[...] = jnp.zeros_like(dg_ref)

        x_, do, kvv = x_ref[...], do_ref[...], kv_ref[...]
        r, h, q, ps, o = _xattn_values(x_, g_ref[...], wq_ref[...], kvv)
        dov = _dot_nt(do, wo_ref[...])
        dqs = []
        for hd in range(XA_HEADS):
            sl = slice(hd * XA_HEAD_DIM, (hd + 1) * XA_HEAD_DIM)
            vl = slice(D_MODEL + hd * XA_HEAD_DIM, D_MODEL + (hd + 1) * XA_HEAD_DIM)
            p, doh = ps[hd], dov[:, sl]
            dp = _dot_nt(doh, kvv[:, vl])
            dsc = p * (dp - jnp.sum(dp * p, axis=-1, keepdims=True)) * (XA_HEAD_DIM ** -0.5)
            dqs.append(_dot(dsc, kvv[:, sl]))
            dkv_ref[:, sl] += _dot_tn(dsc, q[:, sl])
            dkv_ref[:, vl] += _dot_tn(p, doh)
        dq = jnp.concatenate(dqs, axis=1)
        dh = _dot_nt(dq, wq_ref[...])
        dg_ref[...] += _colsum(dh * (x_ * r))
        dx_ref[...] = do + _rms_bwd(dh * g_ref[...], x_, r)
        h_ref[...] = _bf(h)
        dq_ref[...] = _bf(dq)
        o_ref[...] = _bf(o)

    row = pl.BlockSpec((ts, D_MODEL), lambda i: (i, 0))
    full = lambda a: pl.BlockSpec(a.shape, lambda i: (0,) * a.ndim)
    sds = jax.ShapeDtypeStruct
    return _pcall(
        body, grid=(s // ts,), in_specs=[row, row, full(gain), full(wq), full(kv), full(wo)],
        out_specs=[row, row, row, row, pl.BlockSpec((m, 2 * D_MODEL), lambda i: (0, 0)),
                   pl.BlockSpec((8, D_MODEL), lambda i: (0, 0))],
        out_shape=[sds((s, D_MODEL), F32), sds((s, D_MODEL), BF16), sds((s, D_MODEL), BF16), sds((s, D_MODEL), BF16),
                   sds((m, 2 * D_MODEL), F32), sds((8, D_MODEL), F32)],
        compiler_params=_params("arbitrary"), name=name)(x, dout, gain, wq, kv, wo)


def _mem_bwd(mem, gain, dkv, wkv, name):
    m = mem.shape[0]

    def body(mem_ref, g_ref, dkv_ref, w_ref, mn_ref, dg_ref):
        mm = mem_ref[...]
        r = _rstd(mm)
        xh = mm * r
        mn_ref[...] = _bf(xh * g_ref[...])
        nb, _, wb = w_ref.shape
        dmn = _dot_nt(dkv_ref[:, 0:wb], w_ref[0])
        for j in range(1, nb):
            dmn = dmn + _dot_nt(dkv_ref[:, j * wb:(j + 1) * wb], w_ref[j])
        dg_ref[...] = jnp.zeros_like(dg_ref) + _colsum(dmn * xh)

    full = lambda a: pl.BlockSpec(a.shape, lambda: (0,) * a.ndim)
    return _pcall(
        body, in_specs=[full(mem), full(gain), full(dkv), full(wkv)],
        out_specs=[pl.BlockSpec((m, D_MODEL), lambda: (0, 0)), pl.BlockSpec((8, D_MODEL), lambda: (0, 0))],
        out_shape=[jax.ShapeDtypeStruct((m, D_MODEL), BF16), jax.ShapeDtypeStruct((8, D_MODEL), F32)],
        compiler_params=pltpu.CompilerParams(vmem_limit_bytes=VMEM_LIMIT), name=name)(mem, gain, dkv, wkv)


def _mlp_fwd(x, gain, w1, w2, name):
    s = x.shape[0]
    ts = _row_tile(s, 512)
    tf = 1024
    nf = D_FF // tf

    def body(x_ref, g_ref, w1_ref, w2_ref, o_ref, h_sc, acc):
        j = pl.program_id(1)

        @pl.when(j == 0)
        def _():
            xx = x_ref[...]
            h_sc[...] = _bf((xx * _rstd(xx)) * g_ref[...])
            acc[...] = jnp.zeros_like(acc)

        a = jnp.dot(h_sc[...], w1_ref[0], preferred_element_type=F32)
        r = jnp.square(jnp.maximum(a, 0.0))
        acc[...] += _dot(r, w2_ref[...])

        @pl.when(j == nf - 1)
        def _():
            o_ref[...] = x_ref[...] + acc[...]

    row = pl.BlockSpec((ts, D_MODEL), lambda i, j: (i, 0))
    return _pcall(
        body, grid=(s // ts, nf),
        in_specs=[row, pl.BlockSpec((1, D_MODEL), lambda i, j: (0, 0)),
                  pl.BlockSpec((1, D_MODEL, tf), lambda i, j: (j, 0, 0)), pl.BlockSpec((tf, D_MODEL), lambda i, j: (j, 0))],
        out_specs=row, out_shape=jax.ShapeDtypeStruct((s, D_MODEL), F32),
        scratch_shapes=[pltpu.VMEM((ts, D_MODEL), BF16), pltpu.VMEM((ts, D_MODEL), F32)],
        compiler_params=_params("parallel", "arbitrary"), name=name)(x, gain, w1, w2)


def _mlp_bwd(x, dout, gain, w1, w2, name):
    s = x.shape[0]
    ts = _row_tile(s, 512)
    tf = 1024
    nf = D_FF // tf

    def body(x_ref, do_ref, g_ref, w1_ref, w2_ref, dx_ref, h_ref, r_ref, da_ref, dg_ref, h_sc, do_sc, acc):
        i, j = pl.program_id(0), pl.program_id(1)

        @pl.when(j == 0)
        def _():
            xx = x_ref[...]
            h_sc[...] = _bf((xx * _rstd(xx)) * g_ref[...])
            do_sc[...] = _bf(do_ref[...])
            acc[...] = jnp.zeros_like(acc)
            h_ref[...] = h_sc[...]

        @pl.when((j == 0) & (i == 0))
        def _():
            dg_ref[...] = jnp.zeros_like(dg_ref)

        a = jnp.dot(h_sc[...], w1_ref[0], preferred_element_type=F32)
        ra = jnp.maximum(a, 0.0)
        r_ref[...] = _bf(ra * ra)
        dr = lax.dot_general(do_sc[...], w2_ref[...], (((1,), (1,)), ((), ())), preferred_element_type=F32)
        da = _bf(dr * 2.0 * ra)
        da_ref[...] = da
        acc[...] += lax.dot_general(da, w1_ref[0], (((1,), (1,)), ((), ())), preferred_element_type=F32)

        @pl.when(j == nf - 1)
        def _():
            xx = x_ref[...]
            r = _rstd(xx)
            dh = acc[...]
            dg_ref[...] += _colsum(dh * (xx * r))
            dx_ref[...] = do_ref[...] + _rms_bwd(dh * g_ref[...], xx, r)

    row = pl.BlockSpec((ts, D_MODEL), lambda i, j: (i, 0))
    ff = pl.BlockSpec((ts, tf), lambda i, j: (i, j))
    sds = jax.ShapeDtypeStruct
    return _pcall(
        body, grid=(s // ts, nf),
        in_specs=[row, row, pl.BlockSpec((1, D_MODEL), lambda i, j: (0, 0)),
                  pl.BlockSpec((1, D_MODEL, tf), lambda i, j: (j, 0, 0)), pl.BlockSpec((tf, D_MODEL), lambda i, j: (j, 0))],
        out_specs=[row, row, ff, ff, pl.BlockSpec((8, D_MODEL), lambda i, j: (0, 0))],
        out_shape=[sds((s, D_MODEL), F32), sds((s, D_MODEL), BF16), sds((s, D_FF), BF16), sds((s, D_FF), BF16),
                   sds((8, D_MODEL), F32)],
        scratch_shapes=[pltpu.VMEM((ts, D_MODEL), BF16), pltpu.VMEM((ts, D_MODEL), BF16), pltpu.VMEM((ts, D_MODEL), F32)],
        compiler_params=_params("arbitrary", "arbitrary"), name=name)(x, dout, gain, w1, w2)


def _final(x, gain, tgt, name):
    s = x.shape[0]
    ts = _row_tile(s, 512)

    def body(x_ref, g_ref, t_ref, dx_ref, loss_ref, dg_ref):
        @pl.when(pl.program_id(0) == 0)
        def _():
            loss_ref[...] = jnp.zeros_like(loss_ref)
            dg_ref[...] = jnp.zeros_like(dg_ref)

        xx = x_ref[...]
        r = _rstd(xx)
        xh = xx * r
        err = xh * g_ref[...] - t_ref[...]
        loss_ref[...] += 0.5 * jnp.sum(jnp.sum(err * err, axis=1, keepdims=True), axis=0, keepdims=True) / D_MODEL
        dy = err * (1.0 / D_MODEL)
        dg_ref[...] += _colsum(dy * xh)
        dx_ref[...] = _rms_bwd(dy * g_ref[...], xx, r)

    row = pl.BlockSpec((ts, D_MODEL), lambda i: (i, 0))
    return _pcall(
        body, grid=(s // ts,), in_specs=[row, pl.BlockSpec((1, D_MODEL), lambda i: (0, 0)), row],
        out_specs=[row, pl.BlockSpec((8, 128), lambda i: (0, 0)), pl.BlockSpec((8, D_MODEL), lambda i: (0, 0))],
        out_shape=[jax.ShapeDtypeStruct((s, D_MODEL), F32), jax.ShapeDtypeStruct((8, 128), F32),
                   jax.ShapeDtypeStruct((8, D_MODEL), F32)],
        compiler_params=_params("arbitrary"), name=name)(x, gain, tgt)


def _as3d(a):
    return a.reshape((-1,) + a.shape[-2:])


def _ew_tile(r, c):
    if r % 256 == 0 or r <= 256:
        return _row_tile(r, 256), c
    return r, 128


def _sum_cast(terms, out_dtype, name):
    shape = terms[0].shape
    t3 = [_as3d(t) for t in terms]
    b, r, c = t3[0].shape
    tr, tc = _ew_tile(r, c)
    nc = c // tc

    def body(*refs):
        acc = refs[0][...].astype(F32)
        for t in refs[1:-1]:
            acc = acc + t[...].astype(F32)
        refs[-1][...] = acc.astype(out_dtype)

    spec = pl.BlockSpec((1, tr, tc), lambda i, j: (i, j // nc, j % nc))
    out = _pcall(body, grid=(b, (r // tr) * nc), in_specs=[spec] * len(t3), out_specs=spec,
                 out_shape=jax.ShapeDtypeStruct((b, r, c), out_dtype), compiler_params=_params("parallel", "parallel"),
                 name=name)(*t3)
    return out.reshape(shape)


def _split_axis(rows):
    return 0 if rows % 64 == 0 else 1


def _half_of(ref, which, lead=()):
    rows, cols = ref.shape[-2], ref.shape[-1]
    if _split_axis(rows) == 0:
        return ref.at[(*lead, pl.ds(which * (rows // 2), rows // 2))]
    return ref.at[(*lead, slice(None), pl.ds(which * (cols // 2), cols // 2))]


def _half_shape(shape):
    rows, cols = shape[-2], shape[-1]
    return (*shape[:-2], rows // 2, cols) if _split_axis(rows) == 0 else (*shape[:-2], rows, cols // 2)


def _pair_sum(a, b, sel, half_id, out_dtype, name):
    _, h, c = b.shape
    k = sel.shape[0]
    by_rows = _split_axis(a.shape[1]) == 0
    tr, tc = _ew_tile(h, c)
    nr, nc = h // tr, c // tc

    def body(sel_ref, hid_ref, a_ref, b_ref, o_ref):
        o_ref[...] = (a_ref[...] + b_ref[...]).astype(out_dtype)

    def a_map(q, j, sel_ref, hid_ref):
        if by_rows:
            return sel_ref[q], hid_ref[0] * nr + j // nc, j % nc
        return sel_ref[q], j // nc, hid_ref[0] * nc + j % nc

    blkshape = (1, tr, tc)
    grid_spec = pltpu.PrefetchScalarGridSpec(
        num_scalar_prefetch=2, grid=(k, nr * nc),
        in_specs=[pl.BlockSpec(blkshape, a_map),
                  pl.BlockSpec(blkshape, lambda q, j, sel_ref, hid_ref: (sel_ref[q], j // nc, j % nc))],
        out_specs=pl.BlockSpec(blkshape, lambda q, j, sel_ref, hid_ref: (q, j // nc, j % nc)))
    return _pcall(body, grid_spec=grid_spec, out_shape=jax.ShapeDtypeStruct((k, h, c), out_dtype),
                  compiler_params=_params("parallel", "parallel"), name=name)(sel, half_id, a, b)


def _adamw(w, g, m, v, name):
    shape = w.shape
    w3, g3, m3, v3 = _as3d(w), _as3d(g), _as3d(m), _as3d(v)
    b, r, c = w3.shape
    tr, tc = _ew_tile(r, c)

    def body(w_ref, g_ref, m_ref, v_ref, d_ref, mo_ref, vo_ref):
        gg = g_ref[...]
        mn = ADAM_B1 * m_ref[...] + (1.0 - ADAM_B1) * gg
        vn = ADAM_B2 * v_ref[...] + (1.0 - ADAM_B2) * jnp.square(gg)
        m_hat = mn / (1.0 - ADAM_B1 ** ADAM_STEP)
        v_hat = vn / (1.0 - ADAM_B2 ** ADAM_STEP)
        d_ref[...] = -ADAM_LR * (m_hat / (jnp.sqrt(v_hat) + ADAM_EPS) + ADAM_WD * w_ref[...])
        mo_ref[...] = mn
        vo_ref[...] = vn

    spec = pl.BlockSpec((1, tr, tc), lambda i, j: (i, j // (c // tc), j % (c // tc)))
    sd = jax.ShapeDtypeStruct((b, r, c), F32)
    d, mo, vo = _pcall(body, grid=(b, (r // tr) * (c // tc)), in_specs=[spec] * 4, out_specs=[spec] * 3,
                       out_shape=[sd] * 3, compiler_params=_params("parallel", "parallel"), name=name)(w3, g3, m3, v3)
    return d.reshape(shape), mo.reshape(shape), vo.reshape(shape)


ANY = pl.BlockSpec(memory_space=pl.ANY)


def _place():
    return lax.axis_index("x"), lax.axis_index("y"), lax.axis_index("c")


def _flip(x, y, r):
    return (1 - x if r & 2 else x), (1 - y if r & 1 else y)


def _dma_sems(*counts):
    return [pltpu.SemaphoreType.DMA((k,)) for k in counts]


def _gather_ici(shards):
    n = len(shards)

    def copies(ins, outs, sems, incoming):
        send, recv = sems
        x, y, c = _place()
        out = []
        for r in (1, 2, 3):
            cx, cy = _flip(x, y, r)
            for a in range(n):
                block = 2 * cx + cy if incoming else 2 * x + y
                out.append(pltpu.make_async_remote_copy(
                    src_ref=_half_of(ins[a], c), dst_ref=_half_of(outs[a], c, (block,)),
                    send_sem=send.at[(r - 1) * n + a], recv_sem=recv.at[(r - 1) * n + a], device_id=(cx, cy, c),
                    device_id_type=MESH))
        return out

    def start(ins, outs, sems):
        for cp in copies(ins, outs, sems, False):
            cp.start()

    def finish(ins, outs, sems):
        for cp in copies(ins, outs, sems, True):
            cp.wait_recv()
        for cp in copies(ins, outs, sems, False):
            cp.wait_send()

    return _Comm(shards, [jax.ShapeDtypeStruct((4,) + a.shape, a.dtype) for a in shards], _dma_sems(3 * n, 3 * n),
                 start, finish)


def _gather_d2d(bufs, shards):
    n = len(bufs)

    def copies(ins, outs, sems, incoming):
        send, recv = sems
        x, y, c = _place()
        out = []
        for r in (1, 2, 3):
            cx, cy = _flip(x, y, r)
            for a in range(n):
                ref = _half_of(outs[a], (1 - c) if incoming else c, (2 * cx + cy,))
                out.append(pltpu.make_async_remote_copy(
                    src_ref=ref, dst_ref=ref, send_sem=send.at[(r - 1) * n + a], recv_sem=recv.at[(r - 1) * n + a],
                    device_id=(x, y, 1 - c), device_id_type=MESH))
        for a in range(n):
            out.append(pltpu.make_async_remote_copy(
                src_ref=ins[n + a], dst_ref=outs[a].at[2 * x + y], send_sem=send.at[3 * n + a],
                recv_sem=recv.at[3 * n + a], device_id=(x, y, 1 - c), device_id_type=MESH))
        return out

    def start(ins, outs, sems):
        for cp in copies(ins, outs, sems, False):
            cp.start()

    def finish(ins, outs, sems):
        for cp in copies(ins, outs, sems, True):
            cp.wait_recv()
        for cp in copies(ins, outs, sems, False):
            cp.wait_send()

    return _Comm(list(bufs) + list(shards), [jax.ShapeDtypeStruct(a.shape, a.dtype) for a in bufs],
                 _dma_sems(4 * n, 4 * n), start, finish, aliases={a: a for a in range(n)})


def _swap_rows(packs):
    n = len(packs)

    def copies(ins, outs, sems):
        send, recv = sems
        x, y, c = _place()
        return [pltpu.make_async_remote_copy(
            src_ref=_half_of(ins[a], 1 - c, (slice(None),)), dst_ref=outs[a], send_sem=send.at[a],
            recv_sem=recv.at[a], device_id=(x, y, 1 - c), device_id_type=MESH) for a in range(n)]

    def start(ins, outs, sems):
        for cp in copies(ins, outs, sems):
            cp.start()

    def finish(ins, outs, sems):
        for cp in copies(ins, outs, sems):
            cp.wait()

    return _Comm(packs, [jax.ShapeDtypeStruct(_half_shape(a.shape), a.dtype) for a in packs], _dma_sems(n, n),
                 start, finish)


def _exchange(arrs):
    n = len(arrs)

    def copies(ins, outs, sems):
        send, recv = sems
        x, y, c = _place()
        out = []
        for r in (1, 2, 3):
            cx, cy = _flip(x, y, r)
            for a in range(n):
                out.append(pltpu.make_async_remote_copy(
                    src_ref=ins[a].at[r - 1], dst_ref=outs[a].at[r - 1], send_sem=send.at[(r - 1) * n + a],
                    recv_sem=recv.at[(r - 1) * n + a], device_id=(cx, cy, c), device_id_type=MESH))
        return out

    def start(ins, outs, sems):
        for cp in copies(ins, outs, sems):
            cp.start()

    def finish(ins, outs, sems):
        for cp in copies(ins, outs, sems):
            cp.wait()

    return _Comm(arrs, [jax.ShapeDtypeStruct(a.shape, a.dtype) for a in arrs], _dma_sems(3 * n, 3 * n), start, finish)


def _to_sibling(arrs):
    n = len(arrs)

    def copies(ins, outs, sems):
        send, recv = sems
        x, y, c = _place()
        return [pltpu.make_async_remote_copy(
            src_ref=ins[a], dst_ref=outs[a], send_sem=send.at[a], recv_sem=recv.at[a],
            device_id=(x, y, 1 - c), device_id_type=MESH) for a in range(n)]

    def start(ins, outs, sems):
        for cp in copies(ins, outs, sems):
            cp.start()

    def finish(ins, outs, sems):
        for cp in copies(ins, outs, sems):
            cp.wait()

    return _Comm(arrs, [jax.ShapeDtypeStruct(a.shape, a.dtype) for a in arrs], _dma_sems(n, n), start, finish)


def _gather8(v, reduce, name):
    rows, w = v.shape

    def body(v_ref, out_ref, buf, send_sems, recv_sems):
        x, y, c = _place()
        me, sibling = (x, y, c), (x, y, 1 - c)
        chips = [_flip(x, y, r) for r in (1, 2, 3)]
        dst = out_ref if not reduce else buf

        def slot(px, py, pc):
            return dst.at[4 * px + 2 * py + pc]

        def copy(k, block, to, src=None):
            return pltpu.make_async_remote_copy(
                src_ref=slot(*block) if src is None else src, dst_ref=slot(*block), send_sem=send_sems.at[k],
                recv_sem=recv_sems.at[k], device_id=to, device_id_type=MESH)

        dst[4 * x + 2 * y + c] = v_ref[...]
        first = [copy(0, me, sibling, src=v_ref)]
        first += [copy(1 + j, me, (*chip, c), src=v_ref) for j, chip in enumerate(chips)]
        for cp in first:
            cp.start()
        passed = [copy(4 + j, (*chip, c), sibling) for j, chip in enumerate(chips)]
        for j, chip in enumerate(chips):
            copy(1 + j, (*chip, c), me).wait_recv()
            passed[j].start()
        copy(0, sibling, me).wait_recv()
        for j, chip in enumerate(chips):
            copy(4 + j, (*chip, 1 - c), me).wait_recv()
        for cp in first + passed:
            cp.wait_send()
        if reduce:
            acc = buf[0]
            for d in range(1, 8):
                acc = acc + buf[d]
            out_ref[...] = acc

    vm = pl.BlockSpec(memory_space=pltpu.VMEM)
    scratch = [pltpu.VMEM((8, rows, w) if reduce else (8, 8, 128), F32), pltpu.SemaphoreType.DMA((7,)),
               pltpu.SemaphoreType.DMA((7,))]
    out_shape = jax.ShapeDtypeStruct((rows, w) if reduce else (8, rows, w), F32)
    return _pcall(body, in_specs=[vm], out_specs=vm, out_shape=out_shape, scratch_shapes=scratch,
                  compiler_params=pltpu.CompilerParams(vmem_limit_bytes=VMEM_LIMIT), name=name)(v)


SMALL = [("norm_mix", 1024), ("b_gate", 2048), ("conv_b", 4096), ("dt_bias", 32), ("a_log", 32), ("d_skip", 32),
         ("ssm_norm", 2048), ("norm_xa", 1024), ("norm_mem", 1024), ("norm_mlp", 1024)]


def _rows_of(width):
    return max(1, width // 1024)


def _pack_rows(pieces):
    out = []
    for p in pieces:
        p = p.astype(F32)
        if p.shape[-1] < 1024:
            p = jnp.pad(p, ((0, 0), (0, 1024 - p.shape[-1])))
        out.append(p.reshape(-1, 1024))
    cat = jnp.concatenate(out, axis=0)
    pad = (-cat.shape[0]) % 8
    return jnp.pad(cat, ((0, pad), (0, 0))) if pad else cat


def _unpack_rows(packed, widths_rows):
    out, at = [], 0
    for r, w in widths_rows:
        k = r * _rows_of(w)
        p = packed[at:at + k]
        at += k
        out.append(p[:, :w] if w < 1024 else p.reshape(r, w))
    return out


def _to_cat(wt):
    pieces = [wt[O_XBC:O_XBC + 4096], wt[O_Q:O_Q + 512], wt[O_K:O_K + 512], wt[O_DT:O_DT + 32],
              jnp.zeros((DT_PAD - 32, wt.shape[1]), wt.dtype), wt[O_Z:O_Z + 2048], wt[O_GATES:O_GATES + 2048],
              wt[O_V:O_V + 1024], wt[O_G:O_G + 1024]]
    return jnp.concatenate(pieces, axis=0)


def _from_cat(gt):
    pieces = [gt[C_Q:C_Q + 512], gt[C_K:C_K + 512], gt[C_V:C_V + 1024], gt[C_G:C_G + 1024],
              gt[C_Z:C_Z + 2048], gt[C_XBC:C_XBC + 4096], gt[C_DT:C_DT + 32], gt[C_GATES:C_GATES + 2048]]
    return jnp.concatenate(pieces, axis=0)


PACK_ROWS = [("mlp_w1", 1024), ("mlp_w2", 1024), ("w_br_ssm", 512), ("w_br_ret", 256), ("w_out", 256), ("xa_wq", 256),
             ("xa_wo", 256)]
PACK_N = sum(r for _, r in PACK_ROWS)


def kernel(x, mem, positions, norm_mix, w_in, b_gate, conv_w, conv_b, dt_bias, a_log, d_skip, ssm_norm, w_br_ret, w_br_ssm, w_out, norm_xa, norm_mem, xa_wq, xa_wkv, xa_wo, norm_mlp, mlp_w1, mlp_w2, norm_final, loss_target, m_norm_mix, m_w_in, m_b_gate, m_conv_w, m_conv_b, m_dt_bias, m_a_log, m_d_skip, m_ssm_norm, m_w_br_ret, m_w_br_ssm, m_w_out, m_norm_xa, m_norm_mem, m_xa_wq, m_xa_wkv, m_xa_wo, m_norm_mlp, m_mlp_w1, m_mlp_w2, m_norm_final, v_norm_mix, v_w_in, v_b_gate, v_conv_w, v_conv_b, v_dt_bias, v_a_log, v_d_skip, v_ssm_norm, v_w_br_ret, v_w_br_ssm, v_w_out, v_norm_xa, v_norm_mem, v_xa_wq, v_xa_wkv, v_xa_wo, v_norm_mlp, v_mlp_w1, v_mlp_w2, v_norm_final):
    W = dict(norm_mix=norm_mix, w_in=w_in, b_gate=b_gate, conv_w=conv_w, conv_b=conv_b, dt_bias=dt_bias, a_log=a_log,
             d_skip=d_skip, ssm_norm=ssm_norm, w_br_ret=w_br_ret, w_br_ssm=w_br_ssm, w_out=w_out, norm_xa=norm_xa,
             norm_mem=norm_mem, xa_wq=xa_wq, xa_wkv=xa_wkv, xa_wo=xa_wo, norm_mlp=norm_mlp, mlp_w1=mlp_w1,
             mlp_w2=mlp_w2, norm_final=norm_final)
    M = dict(norm_mix=m_norm_mix, w_in=m_w_in, b_gate=m_b_gate, conv_w=m_conv_w, conv_b=m_conv_b, dt_bias=m_dt_bias,
             a_log=m_a_log, d_skip=m_d_skip, ssm_norm=m_ssm_norm, w_br_ret=m_w_br_ret, w_br_ssm=m_w_br_ssm,
             w_out=m_w_out, norm_xa=m_norm_xa, norm_mem=m_norm_mem, xa_wq=m_xa_wq, xa_wkv=m_xa_wkv, xa_wo=m_xa_wo,
             norm_mlp=m_norm_mlp, mlp_w1=m_mlp_w1, mlp_w2=m_mlp_w2, norm_final=m_norm_final)
    V = dict(norm_mix=v_norm_mix, w_in=v_w_in, b_gate=v_b_gate, conv_w=v_conv_w, conv_b=v_conv_b, dt_bias=v_dt_bias,
             a_log=v_a_log, d_skip=v_d_skip, ssm_norm=v_ssm_norm, w_br_ret=v_w_br_ret, w_br_ssm=v_w_br_ssm,
             w_out=v_w_out, norm_xa=v_norm_xa, norm_mem=v_norm_mem, xa_wq=v_xa_wq, xa_wkv=v_xa_wkv, xa_wo=v_xa_wo,
             norm_mlp=v_norm_mlp, mlp_w1=v_mlp_w1, mlp_w2=v_mlp_w2, norm_final=v_norm_final)
    nl = w_in.shape[0]
    s = x.shape[1]
    x0 = x[0]
    mem2 = mem[0]
    tgt = loss_target[0]
    blk = 2 * lax.axis_index("x") + lax.axis_index("y")

    grp_b = ["xa_wkv", "w_br_ret", "w_br_ssm", "w_out", "xa_wq", "xa_wo"]
    grp_c = ["mlp_w1", "mlp_w2"]
    win = ["w_in_a", "w_in_b"]
    big = ["w_in"] + grp_b + grp_c
    blk = blk.astype(jnp.int32)
    tr_ = lambda a: jnp.swapaxes(a, 1, 2)
    wb = {k: W[k].astype(BF16) for k in grp_b + grp_c}
    w_in_t = tr_(w_in).astype(BF16)
    wb["w_in_a"], wb["w_in_b"] = w_in_t[:, :WIN_SPLIT], w_in_t[:, WIN_SPLIT:]

    def shards_of(l, ks):
        return [wb[k][l] for k in ks]

    landed = _run_comm(_gather_ici(shards_of(0, win)), "gather_w_in")
    g_win = _run_comm(_gather_d2d(landed, shards_of(0, win)), "gather_w_in_cores")
    cw_all = _gather8(conv_w.reshape(nl * SSM_CONV, 1024), False, "gather_conv_w")
    cw_full = cw_all.reshape(4, 2, nl, SSM_CONV, 1024)[:, 0].transpose(1, 2, 0, 3).reshape(nl, SSM_CONV, SSM_CONV_DIM)

    offs = {}
    at = 0
    for k, r in PACK_ROWS:
        offs[k] = (at, r)
        at += r


    inv_freq = ROPE_THETA ** (-jnp.arange(0, RET_QK_DIM, 2, dtype=F32) / RET_QK_DIM)
    ang = positions.astype(F32)[0][:, None] * inv_freq
    cos, sin = jnp.cos(ang), jnp.sin(ang)
    cosf = jnp.concatenate([cos, cos], axis=1)
    sinf = jnp.concatenate([-sin, sin], axis=1)
    dm, qd, kd, cd = (jnp.asarray(c) for c in _ret_constants())
    eye, blkm = (jnp.asarray(c) for c in _ssd_constants())
    consts = (dm, qd, kd, cd, eye, blkm)
    e_np = _head_expand()
    eexp = jnp.asarray(e_np, BF16)
    eexp_t = jnp.asarray(e_np.T.copy(), BF16)

    saved = []
    xcur = x0
    for l in range(nl):
        wcat = _to_cat(jnp.concatenate(g_win, axis=1).reshape(IN_DIM, D_MODEL))
        cw, cb = cw_full[l], conv_b[l][None]
        dtb = jnp.pad(dt_bias[l], (0, 128 - SSM_HEADS))[None]
        ax = jnp.repeat(-jnp.exp(a_log[l]), 64)[None]
        dsk = jnp.repeat(d_skip[l], 64)[None]
        bg, sn = b_gate[l][None], ssm_norm[l][None]
        more = l + 1 < nl
        ici_b, ici_c = _gather_ici(shards_of(l, grp_b)), _gather_ici(shards_of(l, grp_c))
        _carry(ici_b)
        proj, u = _nmm(xcur, norm_mix[l][None], wcat, "in_proj", save_u=True, w_rows=True)
        _carry(ici_c)
        qr, kr, xc, dtx = _prescan(proj, cosf, sinf, cw, cb, dtb, eexp, "prescan")
        cores = _gather_d2d(ici_b.results + ici_c.results, shards_of(l, grp_b + grp_c))
        _carry(cores)
        yr, ys, sst, hst = _scan_fwd(qr, kr, proj, xc, dtx, ax, consts, "scan_fwd")
        gl = dict(zip(grp_b + grp_c, cores.results))
        rows_weight = lambda k: gl[k].reshape(-1, D_MODEL)
        wr, ws, wo = rows_weight("w_br_ret"), rows_weight("w_br_ssm"), rows_weight("w_out")
        wq, wxo, w2 = rows_weight("xa_wq"), rows_weight("xa_wo"), rows_weight("mlp_w2")
        w1, wkv = gl["mlp_w1"], gl["xa_wkv"]
        ici_a = [_gather_ici(shards_of(l + 1, [k])) for k in win] if more else [None, None]
        _carry(ici_a[0])
        x1 = _postscan_fwd(xcur, yr, ys, xc, proj, bg, dsk, sn, wr, ws, wo, "postscan")
        kv = _bf(_nmm(mem2, norm_mem[l][None], wkv, "mem_kv"))
        _carry(ici_a[1])
        x2 = _xattn_fwd(x1, norm_xa[l][None], wq, kv, wxo, "xattn")
        cores = _gather_d2d(ici_a[0].results + ici_a[1].results, shards_of(l + 1, win)) if more else None
        _carry(cores)
        x3 = _mlp_fwd(x2, norm_mlp[l][None], w1, w2, "mlp")
        if more:
            g_win = cores.results
        saved.append(dict(x0=xcur, x1=x1, x2=x2, proj=proj, u=u, qr=qr, kr=kr, xc=xc, dtx=dtx, yr=yr, ys=ys, sst=sst,
                          hst=hst, kv=kv, wcat=wcat, wr=wr, ws=ws, wo=wo, wq=wq, wxo=wxo, w1=w1, w2=w2, wkv=wkv, cw=cw,
                          cb=cb, dtb=dtb, ax=ax, dsk=dsk, bg=bg, sn=sn))
        xcur = x3

    dx, loss_acc, dnf = _final(xcur, norm_final[None], tgt, "final")
    loss = lax.psum(loss_acc[0, 0], ("x", "y", "c"))

    small_g = [None] * nl
    c = lax.axis_index("c")
    half_id = c.astype(jnp.int32)[None]
    sel_own = blk[None]
    sel_rem = jnp.stack([blk ^ 1, blk ^ 2, blk ^ 3])
    layer_grads = {k: [None] * nl for k in big}

    def pair_sums(packs, got):
        own = [_pair_sum(p, g_, sel_own, half_id, F32, "chip_sum_own")[0] for p, g_ in zip(packs, got)]
        out_b = [_pair_sum(p, g_, sel_rem, half_id, BF16, "chip_sum_send") for p, g_ in zip(packs, got)]
        return own, out_b

    def totals(own, inc):
        return [_sum_cast([o, i_[0], i_[1], i_[2]], F32, "grads_total") for o, i_ in zip(own, inc)]

    def finish_layer(lr, red_half, sib_half):
        def whole(i, axis):
            mine_, theirs_ = red_half[i], sib_half[i]
            return jnp.concatenate([jnp.where(c == 0, mine_, theirs_), jnp.where(c == 0, theirs_, mine_)], axis=axis)

        full1 = whole(0, _split_axis(PACK_N))
        for k, r in PACK_ROWS:
            layer_grads[k][lr] = full1[offs[k][0]:offs[k][0] + r]
        layer_grads["w_in"][lr] = whole(1, _split_axis(IN_DIM // 4))
        layer_grads["xa_wkv"][lr] = whole(2, _split_axis(D_MODEL))

    riding = None
    for l in reversed(range(nl)):
        sv = saved[l]
        swap = _swap_rows(riding[1]) if riding else None
        _carry(swap)
        dx2, hm, rm, dam, dg_mlp = _mlp_bwd(sv["x2"], dx, norm_mlp[l][None], sv["w1"], sv["w2"], "mlp_bwd")
        if riding:
            own, out_b = pair_sums(riding[1], swap.results)
        pack = _mm_tn_into(hm, dam, "dw_mlp1", None, offs["mlp_w1"][0], True)
        pack = _mm_tn_into(rm, dx, "dw_mlp2", pack, offs["mlp_w2"][0], False)
        dx1, hx, dqx, ox, dkv, dg_xa = _xattn_bwd(sv["x1"], dx2, norm_xa[l][None], sv["wq"], sv["kv"], sv["wxo"],
                                                  "xattn_bwd")
        pack = _mm_tn_into(hx, dqx, "dw_xq", pack, offs["xa_wq"][0], False)
        pack = _mm_tn_into(ox, dx2, "dw_xo", pack, offs["xa_wo"][0], False)
        memn, dg_mem = _mem_bwd(mem2, norm_mem[l][None], dkv, sv["wkv"], "mem_bwd")
        dwkv = _mm_tn(memn, dkv, "dw_xkv", col_blocks=4)
        chips_a = _exchange(out_b[0:1]) if riding else None
        _carry(chips_a)
        (dyr, dys, dxs_skip, dproj, yrn, ysn, mg, dbr, dbs, dbg, ddsk, dsn) = _postscan_bwd(
            dx1, sv["yr"], sv["ys"], sv["xc"], sv["proj"], sv["bg"], sv["dsk"], sv["sn"], sv["wr"], sv["ws"], sv["wo"],
            "postscan_bwd")
        pack = _mm_tn_into(mg, dx1, "dw_out", pack, offs["w_out"][0], False)
        pack = _mm_tn_into(yrn, dbr, "dw_br_ret", pack, offs["w_br_ret"][0], False)
        pack = _mm_tn_into(ysn, dbs, "dw_br_ssm", pack, offs["w_br_ssm"][0], False)
        chips_b = _exchange(out_b[1:3]) if riding else None
        _carry(chips_b)
        dqr, dkr, dproj, dxc, gdtx, da_cols = _scan_bwd(sv["qr"], sv["kr"], sv["proj"], sv["xc"], sv["dtx"], sv["ax"],
                                                        consts, sv["sst"], sv["hst"], dyr, dys, dproj, "scan_bwd")
        if riding:
            red_half = totals(own, chips_a.results + chips_b.results)
        cores = _to_sibling(red_half) if riding else None
        _carry(cores)
        dproj, dcw, dcb, ddtb = _prescan_bwd(sv["proj"], dxc, dxs_skip, gdtx, dqr, dkr, cosf, sinf, sv["cw"], sv["cb"],
                                             sv["dtb"], eexp_t, dproj, "prescan_bwd")
        if riding:
            finish_layer(riding[0], red_half, cores.results)
        dwcat = _mm_tn(dproj, sv["u"], "dw_in")
        dx, dg_mix = _in_bwd(dproj, sv["wcat"], sv["x0"], norm_mix[l][None], dx1, "in_bwd")

        da_log = (da_cols.reshape(SSM_HEADS, 64).sum(axis=1)) * (-jnp.exp(a_log[l]))
        dd_skip = ddsk[0].reshape(SSM_HEADS, 64).sum(axis=1)
        small_g[l] = [dg_mix[0:1], dbg[0:1], dcb[0:1], ddtb[0:1, :SSM_HEADS], da_log[None], dd_skip[None], dsn[0:1],
                      dg_xa[0:1], dg_mem[0:1], dg_mlp[0:1], dcw[0::8]]
        riding = (l, [pack, _from_cat(dwcat).reshape(4, IN_DIM // 4, D_MODEL), dwkv])

    got = _run_comm(_swap_rows(riding[1]), "grads_core_swap")
    own, out_b = pair_sums(riding[1], got)
    red_half = totals(own, _run_comm(_exchange(out_b), "grads_chip_exchange"))
    finish_layer(riding[0], red_half, _run_comm(_to_sibling(red_half), "grads_core_join"))
    grad_x = dx[None]

    pieces = []
    for l in range(nl):
        pieces += small_g[l]
    pieces.append(dnf[0:1])
    small_sum = _gather8(_pack_rows(pieces), True, "reduce_small")
    layout = []
    for l in range(nl):
        layout += [(1, w) for _, w in SMALL] + [(SSM_CONV, SSM_CONV_DIM)]
    layout.append((1, 1024))
    red = _unpack_rows(small_sum, layout)
    per = len(SMALL) + 1
    g_small = {k: jnp.concatenate([red[l * per + i] for l in range(nl)], axis=0) for i, (k, _) in enumerate(SMALL)}
    g_convw_full = jnp.stack([red[l * per + len(SMALL)] for l in range(nl)])
    g_small["conv_w"] = lax.dynamic_slice_in_dim(g_convw_full, blk * 1024, 1024, axis=2)
    g_small["norm_final"] = red[-1][0]

    grads = dict(g_small)
    for k in big:
        grads[k] = jnp.stack(layer_grads[k])

    delta, new_m, new_v = {}, {}, {}
    for k in ["xa_wkv"] + [k for k, _ in PACK_ROWS]:
        delta[k], new_m[k], new_v[k] = _adamw(W[k], grads[k], M[k], V[k], "adamw_" + k)
    g_in_t = grads["w_in"]
    grads["w_in"] = tr_(g_in_t)
    d_t, m_t, v_t = _adamw(tr_(w_in), g_in_t, tr_(m_w_in), tr_(v_w_in), "adamw_w_in")
    delta["w_in"], new_m["w_in"], new_v["w_in"] = tr_(d_t), tr_(m_t), tr_(v_t)
    small_names = [k for k, _ in SMALL] + ["conv_w", "norm_final"]

    def pack_small(src):
        ps = []
        for k in small_names:
            a = src[k]
            ps.append(a.reshape(-1, a.shape[-1]) if a.ndim > 1 else a[None])
        return _pack_rows(ps)

    ds_, ms_, vs_ = _adamw(pack_small(W), pack_small(grads), pack_small(M), pack_small(V), "adamw_small")
    lay2 = []
    for k in small_names:
        a = W[k]
        lay2.append((int(np.prod(a.shape[:-1])) if a.ndim > 1 else 1, a.shape[-1]))
    for src, dst in ((ds_, delta), (ms_, new_m), (vs_, new_v)):
        for k, piece in zip(small_names, _unpack_rows(src, lay2)):
            dst[k] = piece.reshape(W[k].shape)

    names = ["norm_mix", "w_in", "b_gate", "conv_w", "conv_b", "dt_bias", "a_log", "d_skip", "ssm_norm", "w_br_ret",
             "w_br_ssm", "w_out", "norm_xa", "norm_mem", "xa_wq", "xa_wkv", "xa_wo", "norm_mlp", "mlp_w1", "mlp_w2",
             "norm_final"]
    return (loss, grad_x, *[grads[n] for n in names], *[delta[n] for n in names], *[new_m[n] for n in names],
            *[new_v[n] for n in names])
```

```python
import numpy as np
import jax
import jax.numpy as jnp
from jax import lax
from jax.experimental import pallas as pl
from jax.experimental.pallas import tpu as pltpu

F32 = jnp.float32
BF16 = jnp.bfloat16
MESH = pl.DeviceIdType.MESH

D_MODEL = 1024
CHUNK = 64
EPS = 1e-6
RET_HEADS = 4
RET_QK_DIM = 128
RET_V_DIM = 256
RET_QK = 512
RET_V = 1024
ROPE_THETA = 10000.0
SSM_INNER = 2048
SSM_HEADS = 32
SSM_GROUPS = 8
SSM_STATE = 128
SSM_CONV = 4
SSM_BC = 1024
SSM_CONV_DIM = 4096
XA_HEADS = 4
XA_HEAD_DIM = 256
D_FF = 4096
GROUP_W = 256

DT_PAD = 1024
IN_DIM = 11296
WIN_SPLIT = 1824
NP = 12288
C_XBC, C_Q, C_K, C_DT, C_Z, C_GATES, C_V, C_G = 0, 4096, 4608, 5120, 6144, 8192, 10240, 11264
O_Q, O_K, O_V, O_G, O_Z, O_XBC, O_DT, O_GATES = 0, 512, 1024, 2048, 3072, 5120, 9216, 9248

ADAM_LR = 0.001
ADAM_B1 = 0.9
ADAM_B2 = 0.999
ADAM_EPS = 1e-08
ADAM_WD = 0.01
ADAM_STEP = 10

VMEM_LIMIT = 56 * 1024 * 1024


def _params(*sem):
    return pltpu.CompilerParams(dimension_semantics=sem, vmem_limit_bytes=VMEM_LIMIT)


_CARRY = []


def _carry(comm):
    if comm is not None:
        _CARRY.append(comm)


def _pcall(body, **kw):
    if _CARRY:
        return _hosted(body, _CARRY.pop(), kw)
    return pl.pallas_call(body, **kw)


class _Comm:
    def __init__(self, ins, out_shapes, sems, start, finish, aliases=None):
        self.ins, self.out_shapes, self.sems = list(ins), list(out_shapes), list(sems)
        self.start, self.finish, self.aliases = start, finish, dict(aliases or {})
        self.results, self.parts = None, None

    def deliver(self, results):
        self.results = results
        if self.parts:
            a, b, k = self.parts
            a.deliver(results[:k])
            b.deliver(results[k:])


def _hosted(body, comm, kw):
    in_specs = list(kw.pop("in_specs"))
    out_specs, out_shape = kw.pop("out_specs"), kw.pop("out_shape")
    single = not isinstance(out_shape, (list, tuple))
    if single:
        out_specs, out_shape = [out_specs], [out_shape]
    out_specs, out_shape = list(out_specs), list(out_shape)
    scratch = list(kw.pop("scratch_shapes", []))
    grid = tuple(kw.get("grid", ()))
    aliases = dict(kw.pop("input_output_aliases", {}))
    n_in, n_out, n_sc = len(in_specs), len(out_shape), len(scratch)
    c_in, c_out = len(comm.ins), len(comm.out_shapes)
    for i, o in comm.aliases.items():
        aliases[n_in + i] = n_out + o
    kw["compiler_params"] = _params(*(["arbitrary"] * len(grid)))

    def wrapped(*refs):
        at = 0
        parts = []
        for cnt in (n_in, c_in, n_out, c_out, n_sc):
            parts.append(refs[at:at + cnt])
            at += cnt
        a, ci, b, co, s = parts
        cs = refs[at:]
        first, last = None, None
        for d, size in enumerate(grid):
            f, l_ = pl.program_id(d) == 0, pl.program_id(d) == size - 1
            first = f if first is None else first & f
            last = l_ if last is None else last & l_

        @pl.when(first)
        def _():
            comm.start(ci, co, cs)

        body(*a, *b, *s)

        @pl.when(last)
        def _():
            comm.finish(ci, co, cs)

    call = _pcall(wrapped, in_specs=in_specs + [ANY] * c_in, out_specs=out_specs + [ANY] * c_out,
                  out_shape=out_shape + comm.out_shapes, scratch_shapes=scratch + comm.sems,
                  input_output_aliases=aliases, **kw)

    def run(*ops):
        res = call(*ops, *comm.ins)
        comm.deliver(list(res[n_out:]))
        return res[0] if single else list(res[:n_out])

    return run


def _both(a, b):
    if a is None or b is None:
        return a if b is None else b
    ni, no, ns = len(a.ins), len(a.out_shapes), len(a.sems)

    def start(ins, outs, sems):
        a.start(ins[:ni], outs[:no], sems[:ns])
        b.start(ins[ni:], outs[no:], sems[ns:])

    def finish(ins, outs, sems):
        a.finish(ins[:ni], outs[:no], sems[:ns])
        b.finish(ins[ni:], outs[no:], sems[ns:])

    both = _Comm(a.ins + b.ins, a.out_shapes + b.out_shapes, a.sems + b.sems, start, finish,
                 {**a.aliases, **{ni + i: no + o for i, o in b.aliases.items()}})
    both.parts = (a, b, no)
    return both


def _run_comm(comm, name):
    def body(*refs):
        c_in, c_out = len(comm.ins), len(comm.out_shapes)
        ci, co, cs = refs[:c_in], refs[c_in:c_in + c_out], refs[c_in + c_out:]
        comm.start(ci, co, cs)
        comm.finish(ci, co, cs)

    aliases = {i: o for i, o in comm.aliases.items()}
    res = _pcall(body, in_specs=[ANY] * len(comm.ins), out_specs=[ANY] * len(comm.out_shapes),
                 out_shape=comm.out_shapes, scratch_shapes=comm.sems, input_output_aliases=aliases, name=name)(*comm.ins)
    comm.deliver(list(res))
    return comm.results


def _bf(a):
    return a.astype(BF16)


def _dot(a, b):
    return jnp.dot(_bf(a), _bf(b), preferred_element_type=F32)


def _dot_nt(a, b):
    return lax.dot_general(_bf(a), _bf(b), (((1,), (1,)), ((), ())), preferred_element_type=F32)


def _dot_tn(a, b):
    return lax.dot_general(_bf(a), _bf(b), (((0,), (0,)), ((), ())), preferred_element_type=F32)


def _colsum(a):
    return jnp.sum(a, axis=0, keepdims=True)


def _rstd(x):
    return lax.rsqrt(jnp.mean(x * x, axis=-1, keepdims=True) + EPS)


def _rms_bwd(dy, x, rstd):
    xh = x * rstd
    return rstd * (dy - xh * jnp.mean(dy * xh, axis=-1, keepdims=True))


def _sigmoid(x):
    return 1.0 / (1.0 + jnp.exp(-x))


def _silu_and_grad(x):
    s = _sigmoid(x)
    return x * s, s + x * s * (1.0 - s)


def _softplus(x):
    u = jnp.exp(-jnp.abs(x))
    l1p = jnp.where(u < 1e-4, u * (1.0 - 0.5 * u), jnp.log(1.0 + u))
    return jnp.maximum(x, 0.0) + l1p


def _split3_dot(a, e):
    hi = a.astype(BF16)
    r1 = a - hi.astype(F32)
    mid = r1.astype(BF16)
    lo = (r1 - mid.astype(F32)).astype(BF16)
    return (jnp.dot(hi, e, preferred_element_type=F32) + jnp.dot(mid, e, preferred_element_type=F32)
            + jnp.dot(lo, e, preferred_element_type=F32))


def _cumsum_rows(a):
    rows = lax.broadcasted_iota(jnp.int32, a.shape, 0)
    s = 1
    while s < a.shape[0]:
        a = a + jnp.where(rows >= s, pltpu.roll(a, s, 0), 0.0)
        s *= 2
    return a


def _revcumsum_rows(a):
    n = a.shape[0]
    rows = lax.broadcasted_iota(jnp.int32, a.shape, 0)
    s = 1
    while s < n:
        a = a + jnp.where(rows < n - s, pltpu.roll(a, n - s, 0), 0.0)
        s *= 2
    return a


def _rms_groups(y, width):
    out = []
    for h in range(y.shape[1] // width):
        slab = y[:, h * width:(h + 1) * width]
        out.append((slab, _rstd(slab)))
    return out


def _ret_constants():
    idx = np.arange(CHUNK, dtype=np.float32)
    lg = np.log1p(-(np.float32(2.0) ** (np.float32(-5.0) - np.arange(RET_HEADS, dtype=np.float32)))).astype(np.float32)
    rel = np.abs(idx[:, None] - idx[None, :])
    dm = np.exp(lg[:, None, None] * rel).astype(np.float32)
    qd = np.exp(lg[None, :] * (idx[:, None] + 1.0)).astype(np.float32)
    kd = np.exp(lg[None, :] * (CHUNK - 1.0 - idx[:, None])).astype(np.float32)
    cd = np.exp(lg * CHUNK).astype(np.float32)
    qd = np.repeat(qd, RET_QK_DIM, axis=1)
    kd = np.repeat(kd, RET_QK_DIM, axis=1)
    cd = np.repeat(cd, RET_QK_DIM)[:, None] * np.ones((1, RET_V_DIM), np.float32)
    return dm, qd, kd, cd.astype(np.float32)


def _ssd_constants():
    eye = np.tile(np.eye(CHUNK, dtype=np.float32), (1, GROUP_W // CHUNK))
    blk = np.kron(np.eye(GROUP_W // CHUNK, dtype=np.float32), np.ones((CHUNK, CHUNK), np.float32))
    return eye, blk


def _head_expand():
    e = np.zeros((128, SSM_INNER), np.float32)
    for h in range(SSM_HEADS):
        e[h, h * 64:(h + 1) * 64] = 1.0
    return e


def _ret_chunk_fwd(qh, kh, vh, sh, dmh, qdh, kdh, cdh):
    a = _dot_nt(qh, kh) * dmh
    y = _dot(a, vh) + _dot(qh * qdh, sh)
    s_new = sh * cdh + _dot_tn(kh * kdh, vh)
    return y, s_new


def _ret_chunk_bwd(qh, kh, vh, sh, dmh, qdh, kdh, cdh, dy, ds_new):
    a = _dot_nt(qh, kh) * dmh
    dp = _dot_nt(dy, vh) * dmh
    dq = _dot(dp, kh) + _dot_nt(dy, sh) * qdh
    dk = _dot_tn(dp, qh) + _dot_nt(vh, ds_new) * kdh
    dv = _dot_tn(a, dy) + _dot(kh * kdh, ds_new)
    ds = cdh * ds_new + _dot_tn(qh * qdh, dy)
    return dq, dk, dv, ds


def _ssd_common(xs, dtx, ax, eye):
    cum = _cumsum_rows(dtx * ax)
    last = cum[CHUNK - 1:CHUNK, :]
    r = _colsum(jnp.where(eye > 0.5, cum, 0.0))
    return cum, last, r, xs * dtx


def _tile4(a):
    return jnp.concatenate([a, a, a, a], axis=0)


def _ssd_chunk_fwd(xs, dtx, b, c, ax, hg, eye, blk):
    cum, last, r, x = _ssd_common(xs, dtx, ax, eye)
    lam = jnp.exp(-jnp.abs(cum - r))
    wc = _dot_nt(c, _tile4(b)) * lam
    bd = _tile4(x) * blk
    y = _dot(wc, bd) + _dot(c, hg) * jnp.exp(cum)
    h_new = hg * jnp.exp(last) + _dot_tn(b, x * jnp.exp(last - cum))
    return y, h_new


def _ssd_chunk_bwd(xs, dtx, b, c, ax, hg, eye, blk, dy, dh_new):
    cum, last, r, x = _ssd_common(xs, dtx, ax, eye)
    delta = cum - r
    lam = jnp.exp(-jnp.abs(delta))
    b4 = _tile4(b)
    cb4 = _dot_nt(c, b4)
    wc = cb4 * lam
    bd = _tile4(x) * blk
    ecx = jnp.exp(cum)
    wl = jnp.exp(last - cum)
    ecl = jnp.exp(last)
    z = _dot(c, hg)
    dwc = _dot_nt(dy, bd)
    dbd = _dot_tn(wc, dy) * blk
    dx = dbd[0:64] + dbd[64:128] + dbd[128:192] + dbd[192:256]
    dt_ = _dot(b, dh_new)
    dx = dx + dt_ * wl
    dcb4 = dwc * lam
    dz = dy * ecx
    dc = _dot(dcb4, b4) + _dot_nt(dz, hg)
    db4 = _dot_tn(dcb4, c)
    db = db4[0:64] + db4[64:128] + db4[128:192] + db4[192:256] + _dot_nt(x * wl, dh_new)
    g = dwc * cb4 * lam * (-jnp.sign(delta))
    dr = -_colsum(g)
    dwl = dt_ * x * wl
    u = g + eye * dr + dy * z * ecx - dwl
    lastrow = _colsum(dwl) + _colsum(dh_new * hg) * ecl
    rows = lax.broadcasted_iota(jnp.int32, u.shape, 0)
    u = u + jnp.where(rows == CHUNK - 1, lastrow, 0.0)
    dh = _dot_tn(c, dz) + dh_new * ecl
    rc = _revcumsum_rows(u)
    dxs = dx * dtx
    g_dtx = dx * xs + rc * ax
    da = _colsum(rc * dtx)
    return dxs, g_dtx, db, dc, da, dh


def _row_tile(s, want):
    t = min(s, want)
    assert s % t == 0
    return t


def _nmm(x, gain, w, name, tn=1024, save_u=False, w_rows=False):
    s, d = x.shape
    blocked = w.ndim == 3
    if blocked:
        tn = w.shape[2]
        n = w.shape[0] * tn
        w_spec = pl.BlockSpec((1, d, tn), lambda i, j: (j, 0, 0))
    elif w_rows:
        n = w.shape[0]
        w_spec = pl.BlockSpec((tn, d), lambda i, j: (j, 0))
    else:
        n = w.shape[1]
        w_spec = pl.BlockSpec((d, tn), lambda i, j: (0, j))
    tm = _row_tile(s, 1024)
    assert n % tn == 0

    def body(x_ref, g_ref, w_ref, *rest):
        o_ref, u_sc = rest[0], rest[-1]

        @pl.when(pl.program_id(1) == 0)
        def _():
            xx = x_ref[...]
            u = _bf((xx * _rstd(xx)) * g_ref[...])
            u_sc[...] = u
            if save_u:
                rest[1][...] = u

        if w_rows:
            o_ref[...] = _dot_nt(u_sc[...], w_ref[...])
        else:
            o_ref[...] = jnp.dot(u_sc[...], w_ref[0] if blocked else w_ref[...], preferred_element_type=F32)

    out_shape = [jax.ShapeDtypeStruct((s, n), F32)]
    out_specs = [pl.BlockSpec((tm, tn), lambda i, j: (i, j))]
    if save_u:
        out_shape.append(jax.ShapeDtypeStruct((s, d), BF16))
        out_specs.append(pl.BlockSpec((tm, d), lambda i, j: (i, 0)))
    res = _pcall(
        body, grid=(s // tm, n // tn),
        in_specs=[pl.BlockSpec((tm, d), lambda i, j: (i, 0)), pl.BlockSpec((1, d), lambda i, j: (0, 0)), w_spec],
        out_specs=out_specs, out_shape=out_shape, scratch_shapes=[pltpu.VMEM((tm, d), BF16)],
        compiler_params=_params("parallel", "arbitrary"), name=name)(x, gain, w)
    return res if save_u else res[0]


def _mm_tn(a, b, name, tm=1024, tn=1024, col_blocks=None):
    k, m = a.shape
    n = b.shape[1]
    tk = _row_tile(k, 1024)
    tm, tn = min(tm, m), min(tn, n)
    if col_blocks:
        tn = n // col_blocks
    assert m % tm == 0 and n % tn == 0
    nk = k // tk

    def body(a_ref, b_ref, o_ref, acc):
        kk = pl.program_id(2)

        @pl.when(kk == 0)
        def _():
            acc[...] = jnp.zeros_like(acc)

        acc[...] += _dot_tn(a_ref[...], b_ref[...])

        @pl.when(kk == nk - 1)
        def _():
            if col_blocks:
                o_ref[0] = acc[...]
            else:
                o_ref[...] = acc[...]

    if col_blocks:
        out_spec = pl.BlockSpec((1, tm, tn), lambda i, j, kk: (j, i, 0))
        out_shape = jax.ShapeDtypeStruct((col_blocks, m, tn), F32)
    else:
        out_spec = pl.BlockSpec((tm, tn), lambda i, j, kk: (i, j))
        out_shape = jax.ShapeDtypeStruct((m, n), F32)
    return _pcall(
        body, grid=(m // tm, n // tn, nk),
        in_specs=[pl.BlockSpec((tk, tm), lambda i, j, kk: (kk, i)), pl.BlockSpec((tk, tn), lambda i, j, kk: (kk, j))],
        out_specs=out_spec, out_shape=out_shape,
        scratch_shapes=[pltpu.VMEM((tm, tn), F32)],
        compiler_params=_params("parallel", "parallel", "arbitrary"), name=name)(a, b)


def _mm_tn_into(a, b, name, pack, off, by_cols):
    k, m = a.shape
    n = b.shape[1]
    tk = _row_tile(k, 1024)
    nk = k // tk
    rows = m if by_cols else m // 4
    tm = min(m, 1024)
    nb = 1 if by_cols else tm // rows
    assert tm == nb * rows and off % rows == 0 and n == (4096 if by_cols else 1024)

    def body(a_ref, b_ref, *rest):
        o_ref, acc = rest[-2], rest[-1]
        kk = pl.program_id(2)

        @pl.when(kk == 0)
        def _():
            acc[...] = jnp.zeros_like(acc)

        acc[...] += _dot_tn(a_ref[...], b_ref[...])

        @pl.when(kk == nk - 1)
        def _():
            o_ref[...] = acc[...].reshape(nb, rows, 1024)

    if by_cols:
        out_spec = pl.BlockSpec((1, rows, 1024), lambda i, j, kk: (j, off // rows, 0))
    else:
        out_spec = pl.BlockSpec((nb, rows, 1024), lambda i, j, kk: (i, off // rows, 0))
    in_specs = [pl.BlockSpec((tk, tm), lambda i, j, kk: (kk, i)), pl.BlockSpec((tk, 1024), lambda i, j, kk: (kk, j))]
    ops, alias = [a, b], {}
    if pack is not None:
        in_specs.append(ANY)
        ops.append(pack)
        alias = {2: 0}
    return _pcall(
        body, grid=(m // tm, n // 1024, nk), in_specs=in_specs, out_specs=out_spec,
        out_shape=jax.ShapeDtypeStruct((4, PACK_N, 1024), F32), scratch_shapes=[pltpu.VMEM((tm, 1024), F32)],
        input_output_aliases=alias, compiler_params=_params("parallel", "parallel", "arbitrary"), name=name)(*ops)


def _in_bwd(dproj, wcat_t, x, gain, dres, name):
    s, n = dproj.shape
    d = wcat_t.shape[1]
    tm = _row_tile(s, 1024)
    tk = 1024
    nk = n // tk
    ns = s // tm

    def body(dp_ref, w_ref, x_ref, g_ref, dr_ref, dx_ref, dg_ref, acc):
        i, kk = pl.program_id(0), pl.program_id(1)

        @pl.when(kk == 0)
        def _():
            acc[...] = jnp.zeros_like(acc)

        @pl.when((kk == 0) & (i == 0))
        def _():
            dg_ref[...] = jnp.zeros_like(dg_ref)

        acc[...] += _dot(dp_ref[...], w_ref[...])

        @pl.when(kk == nk - 1)
        def _():
            xx = x_ref[...]
            r = _rstd(xx)
            du = acc[...]
            dg_ref[...] += _colsum(du * (xx * r))
            dx_ref[...] = dr_ref[...] + _rms_bwd(du * g_ref[...], xx, r)

    return _pcall(
        body, grid=(ns, nk),
        in_specs=[pl.BlockSpec((tm, tk), lambda i, kk: (i, kk)), pl.BlockSpec((tk, d), lambda i, kk: (kk, 0)),
                  pl.BlockSpec((tm, d), lambda i, kk: (i, 0)), pl.BlockSpec((1, d), lambda i, kk: (0, 0)),
                  pl.BlockSpec((tm, d), lambda i, kk: (i, 0))],
        out_specs=[pl.BlockSpec((tm, d), lambda i, kk: (i, 0)), pl.BlockSpec((8, d), lambda i, kk: (0, 0))],
        out_shape=[jax.ShapeDtypeStruct((s, d), F32), jax.ShapeDtypeStruct((8, d), F32)],
        scratch_shapes=[pltpu.VMEM((tm, d), F32)],
        compiler_params=_params("arbitrary", "arbitrary"), name=name)(dproj, wcat_t, x, gain, dres)


def _prev_rows_spec(ts, width):
    return pl.BlockSpec((8, width), lambda i: (jnp.maximum(i * (ts // 8) - 1, 0), 0))


def _prescan(proj, cosf, sinf, cw, cb, dtb, eexp, name):
    s = proj.shape[0]
    ts = _row_tile(s, 256)

    def body(xbc_ref, prev_ref, q_ref, k_ref, dt_ref, cos_ref, sin_ref, cw_ref, cb_ref, dtb_ref, e_ref,
             qo_ref, ko_ref, xc_ref, dtx_ref):
        i = pl.program_id(0)
        for st in range(SSM_CONV_DIM // 128):
            sl = slice(st * 128, (st + 1) * 128)
            prev = jnp.where(i > 0, prev_ref[:, sl], 0.0)
            xcat = jnp.concatenate([prev, xbc_ref[:, sl]], axis=0)
            pre = cb_ref[:, sl] + cw_ref[3:4, sl] * xcat[8:8 + ts]
            for j in range(3):
                pre = pre + cw_ref[j:j + 1, sl] * pltpu.roll(xcat, 3 - j, 0)[8:8 + ts]
            xc_ref[:, sl] = pre * _sigmoid(pre)
        cs, sn = cos_ref[...], sin_ref[...]
        for h in range(RET_HEADS):
            sl = slice(h * 128, (h + 1) * 128)
            qh, kh = q_ref[:, sl], k_ref[:, sl]
            qo_ref[:, sl] = qh * cs + pltpu.roll(qh, 64, 1) * sn
            ko_ref[:, sl] = (kh * cs + pltpu.roll(kh, 64, 1) * sn) * (RET_QK_DIM ** -0.5)
        dtv = _softplus(dt_ref[:, 0:128] + dtb_ref[...])
        dtx_ref[...] = _split3_dot(dtv, e_ref[...])

    row = lambda w, c: pl.BlockSpec((ts, w), lambda i: (i, c))
    full = lambda a: pl.BlockSpec(a.shape, lambda i: (0,) * a.ndim)
    return _pcall(
        body, grid=(s // ts,),
        in_specs=[row(4096, 0), _prev_rows_spec(ts, 4096), row(512, C_Q // 512), row(512, C_K // 512),
                  row(DT_PAD, C_DT // DT_PAD), row(128, 0), row(128, 0), full(cw), full(cb), full(dtb), full(eexp)],
        out_specs=[row(512, 0), row(512, 0), row(4096, 0), row(2048, 0)],
        out_shape=[jax.ShapeDtypeStruct((s, 512), F32), jax.ShapeDtypeStruct((s, 512), F32),
                   jax.ShapeDtypeStruct((s, 4096), F32), jax.ShapeDtypeStruct((s, 2048), F32)],
        compiler_params=_params("parallel"), name=name)(proj, proj, proj, proj, proj, cosf, sinf, cw, cb, dtb, eexp)


def _scan_fwd(qr, kr, proj, xc, dtx, ax, consts, name):
    s = qr.shape[0]
    nc = s // CHUNK
    dm, qd, kd, cd, eye, blk = consts

    def body(q_ref, k_ref, v_ref, xc_ref, dtx_ref, ax_ref, dm_ref, qd_ref, kd_ref, cd_ref, eye_ref, blk_ref,
             yr_ref, ys_ref, sst_ref, hst_ref, s_sc, h_sc):
        @pl.when(pl.program_id(0) == 0)
        def _():
            s_sc[...] = jnp.zeros_like(s_sc)
            h_sc[...] = jnp.zeros_like(h_sc)

        sst_ref[0] = s_sc[...]
        hst_ref[0] = h_sc[...]
        for h in range(RET_HEADS):
            ql, vl = slice(h * 128, (h + 1) * 128), slice(h * 256, (h + 1) * 256)
            y, s_new = _ret_chunk_fwd(q_ref[:, ql], k_ref[:, ql], v_ref[:, vl], s_sc[ql, :], dm_ref[h],
                                      qd_ref[:, ql], kd_ref[:, ql], cd_ref[ql, :])
            yr_ref[:, vl] = y
            s_sc[ql, :] = s_new
        eye_v, blk_v = eye_ref[...], blk_ref[...]
        for g in range(SSM_GROUPS):
            sl = slice(g * GROUP_W, (g + 1) * GROUP_W)
            bl = slice(SSM_INNER + g * 128, SSM_INNER + (g + 1) * 128)
            cl = slice(SSM_INNER + SSM_BC + g * 128, SSM_INNER + SSM_BC + (g + 1) * 128)
            y, h_new = _ssd_chunk_fwd(xc_ref[:, sl], dtx_ref[:, sl], xc_ref[:, bl], xc_ref[:, cl], ax_ref[:, sl],
                                      h_sc[:, sl], eye_v, blk_v)
            ys_ref[:, sl] = y
            h_sc[:, sl] = h_new

    row = lambda w, c=0: pl.BlockSpec((CHUNK, w), lambda i: (i, c))
    full = lambda a: pl.BlockSpec(a.shape, lambda i: (0,) * a.ndim)
    return _pcall(
        body, grid=(nc,),
        in_specs=[row(512), row(512), row(1024, C_V // 1024), row(4096), row(2048), full(ax), full(dm), full(qd),
                  full(kd), full(cd), full(eye), full(blk)],
        out_specs=[row(1024), row(2048), pl.BlockSpec((1, 512, 256), lambda i: (i, 0, 0)),
                   pl.BlockSpec((1, 128, 2048), lambda i: (i, 0, 0))],
        out_shape=[jax.ShapeDtypeStruct((s, 1024), F32), jax.ShapeDtypeStruct((s, 2048), F32),
                   jax.ShapeDtypeStruct((nc, 512, 256), F32), jax.ShapeDtypeStruct((nc, 128, 2048), F32)],
        scratch_shapes=[pltpu.VMEM((512, 256), F32), pltpu.VMEM((128, 2048), F32)],
        compiler_params=_params("arbitrary"), name=name)(qr, kr, proj, xc, dtx, ax, dm, qd, kd, cd, eye, blk)


def _scan_bwd(qr, kr, proj, xc, dtx, ax, consts, sst, hst, dyr, dys, dproj, name):
    s = qr.shape[0]
    nc = s // CHUNK
    dm, qd, kd, cd, eye, blk = consts

    def body(q_ref, k_ref, v_ref, xc_ref, dtx_ref, ax_ref, dm_ref, qd_ref, kd_ref, cd_ref, eye_ref, blk_ref,
             sst_ref, hst_ref, dyr_ref, dys_ref, dproj_in, dq_ref, dk_ref, dv_ref, dxc_ref, gdt_ref, da_ref, ds_sc,
             dh_sc):
        @pl.when(pl.program_id(0) == 0)
        def _():
            ds_sc[...] = jnp.zeros_like(ds_sc)
            dh_sc[...] = jnp.zeros_like(dh_sc)
            da_ref[...] = jnp.zeros_like(da_ref)

        for h in range(RET_HEADS):
            ql, vl = slice(h * 128, (h + 1) * 128), slice(h * 256, (h + 1) * 256)
            dq, dk, dv, ds = _ret_chunk_bwd(q_ref[:, ql], k_ref[:, ql], v_ref[:, vl], sst_ref[0, ql, :], dm_ref[h],
                                            qd_ref[:, ql], kd_ref[:, ql], cd_ref[ql, :], dyr_ref[:, vl], ds_sc[ql, :])
            dq_ref[:, ql] = dq
            dk_ref[:, ql] = dk
            dv_ref[:, vl] = _bf(dv)
            ds_sc[ql, :] = ds
        eye_v, blk_v = eye_ref[...], blk_ref[...]
        for g in range(SSM_GROUPS):
            sl = slice(g * GROUP_W, (g + 1) * GROUP_W)
            bl = slice(SSM_INNER + g * 128, SSM_INNER + (g + 1) * 128)
            cl = slice(SSM_INNER + SSM_BC + g * 128, SSM_INNER + SSM_BC + (g + 1) * 128)
            dxs, g_dtx, db, dc, da, dh = _ssd_chunk_bwd(
                xc_ref[:, sl], dtx_ref[:, sl], xc_ref[:, bl], xc_ref[:, cl], ax_ref[:, sl], hst_ref[0, :, sl],
                eye_v, blk_v, dys_ref[:, sl], dh_sc[:, sl])
            dxc_ref[:, sl] = dxs
            dxc_ref[:, bl] = db
            dxc_ref[:, cl] = dc
            gdt_ref[:, sl] = g_dtx
            da_ref[:, sl] += da
            dh_sc[:, sl] = dh

    row = lambda w, c=0: pl.BlockSpec((CHUNK, w), lambda i: (nc - 1 - i, c))
    full = lambda a: pl.BlockSpec(a.shape, lambda i: (0,) * a.ndim)
    return _pcall(
        body, grid=(nc,),
        in_specs=[row(512), row(512), row(1024, C_V // 1024), row(4096), row(2048), full(ax), full(dm), full(qd),
                  full(kd), full(cd), full(eye), full(blk),
                  pl.BlockSpec((1, 512, 256), lambda i: (nc - 1 - i, 0, 0)),
                  pl.BlockSpec((1, 128, 2048), lambda i: (nc - 1 - i, 0, 0)), row(1024), row(2048), ANY],
        out_specs=[row(512), row(512), row(1024, C_V // 1024), row(4096), row(2048),
                   pl.BlockSpec((1, 2048), lambda i: (0, 0))],
        out_shape=[jax.ShapeDtypeStruct((s, 512), F32), jax.ShapeDtypeStruct((s, 512), F32),
                   jax.ShapeDtypeStruct(dproj.shape, BF16), jax.ShapeDtypeStruct((s, 4096), F32),
                   jax.ShapeDtypeStruct((s, 2048), F32), jax.ShapeDtypeStruct((1, 2048), F32)],
        scratch_shapes=[pltpu.VMEM((512, 256), F32), pltpu.VMEM((128, 2048), F32)],
        input_output_aliases={16: 2},
        compiler_params=_params("arbitrary"), name=name)(qr, kr, proj, xc, dtx, ax, dm, qd, kd, cd, eye, blk, sst, hst,
                                                          dyr, dys, dproj)


def _mix_values(yr, g, ys, xs, z, gates, bg, dsk, sn):
    sg, dsg = _silu_and_grad(g)
    ret = _rms_groups(yr, RET_V_DIM)
    yrn = jnp.concatenate([slab * r for slab, r in ret], axis=1) * sg
    sz, dsz = _silu_and_grad(z)
    ys0 = ys + xs * dsk
    ys1 = ys0 * sz
    grp = _rms_groups(ys1, GROUP_W)
    ysh = jnp.concatenate([slab * r for slab, r in grp], axis=1)
    ysn = ysh * sn
    gg = _sigmoid(gates + bg)
    return dict(sg=sg, dsg=dsg, ret=ret, yrn=yrn, sz=sz, dsz=dsz, ys0=ys0, ys1=ys1, grp=grp, ysh=ysh, ysn=ysn,
                gr=gg[:, :D_MODEL], gs=gg[:, D_MODEL:])


def _postscan_fwd(x, yr, ys, xc, proj, bg, dsk, sn, wr, ws, wo, name):
    s = x.shape[0]
    ts = _row_tile(s, 256)

    def body(x_ref, yr_ref, ys_ref, xs_ref, g_ref, z_ref, gt_ref, bg_ref, dsk_ref, sn_ref, wr_ref, ws_ref, wo_ref,
             o_ref):
        m = _mix_values(yr_ref[...], g_ref[...], ys_ref[...], xs_ref[...], z_ref[...], gt_ref[...], bg_ref[...],
                        dsk_ref[...], sn_ref[...])
        merged = m["gr"] * _dot(m["yrn"], wr_ref[...]) + m["gs"] * _dot(m["ysn"], ws_ref[...])
        o_ref[...] = x_ref[...] + _dot(merged, wo_ref[...])

    row = lambda w, c=0: pl.BlockSpec((ts, w), lambda i: (i, c))
    full = lambda a: pl.BlockSpec(a.shape, lambda i: (0,) * a.ndim)
    return _pcall(
        body, grid=(s // ts,),
        in_specs=[row(1024), row(1024), row(2048), row(2048), row(1024, C_G // 1024), row(2048, C_Z // 2048),
                  row(2048, C_GATES // 2048), full(bg), full(dsk), full(sn), full(wr), full(ws), full(wo)],
        out_specs=row(1024), out_shape=jax.ShapeDtypeStruct((s, D_MODEL), F32),
        compiler_params=_params("parallel"), name=name)(x, yr, ys, xc, proj, proj, proj, bg, dsk, sn, wr, ws, wo)


def _postscan_bwd(dout, yr, ys, xc, proj, bg, dsk, sn, wr, ws, wo, name):
    s = dout.shape[0]
    ts = _row_tile(s, 128)

    def body(do_ref, yr_ref, ys_ref, xs_ref, g_ref, z_ref, gt_ref, bg_ref, dsk_ref, sn_ref, wr_ref, ws_ref, wo_ref,
             dyr_ref, dys_ref, dxs_ref, dproj_ref, yrn_ref, ysn_ref, mg_ref, dbr_ref, dbs_ref,
             dbg_ref, ddsk_ref, dsn_ref):
        @pl.when(pl.program_id(0) == 0)
        def _():
            dbg_ref[...] = jnp.zeros_like(dbg_ref)
            ddsk_ref[...] = jnp.zeros_like(ddsk_ref)
            dsn_ref[...] = jnp.zeros_like(dsn_ref)

        xs = xs_ref[...]
        m = _mix_values(yr_ref[...], g_ref[...], ys_ref[...], xs, z_ref[...], gt_ref[...], bg_ref[...],
                        dsk_ref[...], sn_ref[...])
        gr, gs = m["gr"], m["gs"]
        br, bs = _dot(m["yrn"], wr_ref[...]), _dot(m["ysn"], ws_ref[...])
        dmerged = _dot_nt(do_ref[...], wo_ref[...])
        dgt = jnp.concatenate([dmerged * br * gr * (1.0 - gr), dmerged * bs * gs * (1.0 - gs)], axis=1)
        dproj_ref[:, C_GATES:C_GATES + 2048] = _bf(dgt)
        dbg_ref[...] += _colsum(dgt)
        dbr, dbs = dmerged * gr, dmerged * gs
        yrn_ref[...] = _bf(m["yrn"])
        ysn_ref[...] = _bf(m["ysn"])
        mg_ref[...] = _bf(gr * br + gs * bs)
        dbr_ref[...] = _bf(dbr)
        dbs_ref[...] = _bf(dbs)
        dyrn = _dot_nt(dbr, wr_ref[...])
        dysn = _dot_nt(dbs, ws_ref[...])
        rn = jnp.concatenate([slab * r for slab, r in m["ret"]], axis=1)
        dproj_ref[:, C_G:C_G + 1024] = _bf(dyrn * rn * m["dsg"])
        drn = dyrn * m["sg"]
        dyr_ref[...] = jnp.concatenate(
            [_rms_bwd(drn[:, h * RET_V_DIM:(h + 1) * RET_V_DIM], slab, r) for h, (slab, r) in enumerate(m["ret"])], axis=1)
        dsn_ref[...] += _colsum(dysn * m["ysh"])
        dysh = dysn * sn_ref[...]
        dys1 = jnp.concatenate(
            [_rms_bwd(dysh[:, h * GROUP_W:(h + 1) * GROUP_W], slab, r) for h, (slab, r) in enumerate(m["grp"])], axis=1)
        dproj_ref[:, C_Z:C_Z + 2048] = _bf(dys1 * m["ys0"] * m["dsz"])
        dys0 = dys1 * m["sz"]
        dys_ref[...] = dys0
        dxs_ref[...] = dys0 * dsk_ref[...]
        ddsk_ref[...] += _colsum(dys0 * xs)

    row = lambda w, c=0: pl.BlockSpec((ts, w), lambda i: (i, c))
    full = lambda a: pl.BlockSpec(a.shape, lambda i: (0,) * a.ndim)
    acc = lambda w: pl.BlockSpec((8, w), lambda i: (0, 0))
    sds = jax.ShapeDtypeStruct
    return _pcall(
        body, grid=(s // ts,),
        in_specs=[row(1024), row(1024), row(2048), row(2048), row(1024, C_G // 1024), row(2048, C_Z // 2048),
                  row(2048, C_GATES // 2048), full(bg), full(dsk), full(sn), full(wr), full(ws), full(wo)],
        out_specs=[row(1024), row(2048), row(2048), row(NP), row(1024), row(2048), row(1024),
                   row(1024), row(1024), acc(2048), acc(2048), acc(2048)],
        out_shape=[sds((s, 1024), F32), sds((s, 2048), F32), sds((s, 2048), F32), sds((s, NP), BF16),
                   sds((s, 1024), BF16), sds((s, 2048), BF16),
                   sds((s, 1024), BF16), sds((s, 1024), BF16), sds((s, 1024), BF16), sds((8, 2048), F32),
                   sds((8, 2048), F32), sds((8, 2048), F32)],
        compiler_params=_params("arbitrary"), name=name)(dout, yr, ys, xc, proj, proj, proj, bg, dsk, sn, wr, ws, wo)


def _prescan_bwd(proj, dxc, dxs_skip, gdtx, dqr, dkr, cosf, sinf, cw, cb, dtb, eexp_t, dproj, name):
    s = proj.shape[0]
    ts = _row_tile(s, 256)
    nt = s // ts
    m = ts + 8
    width = C_DT + DT_PAD

    def body(xbc_ref, prev_ref, nxt_ref, dt_ref, dxc_ref, dxcn_ref, dsk_ref, dskn_ref, gdt_ref, dq_ref, dk_ref,
             cos_ref, sin_ref, cw_ref, cb_ref, dtb_ref, et_ref, dproj_in, dp_ref, dcw_ref, dcb_ref, ddtb_ref):
        i = pl.program_id(0)

        @pl.when(i == 0)
        def _():
            ddtb_ref[...] = jnp.zeros_like(ddtb_ref)
            dcw_ref[...] = jnp.zeros_like(dcw_ref)
            dcb_ref[...] = jnp.zeros_like(dcb_ref)

        rows = lax.broadcasted_iota(jnp.int32, (m, 128), 0)
        live = (rows < ts) | (i < nt - 1)
        for st in range(SSM_CONV_DIM // 128):
            sl = slice(st * 128, (st + 1) * 128)
            prev = jnp.where(i > 0, prev_ref[:, sl], 0.0)
            xcat = jnp.concatenate([prev, xbc_ref[:, sl], nxt_ref[:, sl]], axis=0)
            shifted = [pltpu.roll(xcat, 3 - j, 0) for j in range(3)] + [xcat]
            pre = cb_ref[:, sl]
            for j in range(SSM_CONV):
                pre = pre + cw_ref[j:j + 1, sl] * shifted[j][8:]
            _, dsilu = _silu_and_grad(pre)
            dxc = jnp.concatenate([dxc_ref[:, sl], dxcn_ref[:, sl]], axis=0)
            if st * 128 < SSM_INNER:
                dxc = dxc + jnp.concatenate([dsk_ref[:, sl], dskn_ref[:, sl]], axis=0)
            dpre = jnp.where(live, dxc * dsilu, 0.0)
            dpt = dpre[0:ts]
            dx = cw_ref[3:4, sl] * dpt
            for j in range(3):
                dx = dx + cw_ref[j:j + 1, sl] * pltpu.roll(dpre, m - (3 - j), 0)[0:ts]
            for j in range(SSM_CONV):
                dcw_ref[8 * j:8 * j + 8, sl] += _colsum(dpt * shifted[j][8:8 + ts])
            dcb_ref[:, sl] += _colsum(dpt)
            dp_ref[:, sl] = _bf(dx)
        cs, sn = cos_ref[...], sin_ref[...]
        for h in range(RET_HEADS):
            sl = slice(h * 128, (h + 1) * 128)
            dq = dq_ref[:, sl]
            dk = dk_ref[:, sl] * (RET_QK_DIM ** -0.5)
            dp_ref[:, C_Q + h * 128:C_Q + (h + 1) * 128] = _bf(dq * cs + pltpu.roll(dq * sn, 64, 1))
            dp_ref[:, C_K + h * 128:C_K + (h + 1) * 128] = _bf(dk * cs + pltpu.roll(dk * sn, 64, 1))
        ddt = _split3_dot(gdt_ref[...], et_ref[...])
        ddt = ddt * _sigmoid(dt_ref[:, 0:128] + dtb_ref[...])
        ddtb_ref[...] += _colsum(ddt)
        dp_ref[:, C_DT:C_DT + 128] = _bf(ddt)
        dp_ref[:, C_DT + 128:C_DT + DT_PAD] = jnp.zeros((ts, DT_PAD - 128), BF16)

    row = lambda w, c=0: pl.BlockSpec((ts, w), lambda i: (i, c))
    nxt = lambda w: pl.BlockSpec((8, w), lambda i: (jnp.minimum((i + 1) * (ts // 8), s // 8 - 1), 0))
    full = lambda a: pl.BlockSpec(a.shape, lambda i: (0,) * a.ndim)
    sds = jax.ShapeDtypeStruct
    return _pcall(
        body, grid=(nt,),
        in_specs=[row(4096), _prev_rows_spec(ts, 4096), nxt(4096), row(DT_PAD, C_DT // DT_PAD), row(4096), nxt(4096),
                  row(2048), nxt(2048), row(2048), row(512), row(512), row(128), row(128), full(cw), full(cb),
                  full(dtb), full(eexp_t), ANY],
        out_specs=[row(width), pl.BlockSpec((32, 4096), lambda i: (0, 0)), pl.BlockSpec((8, 4096), lambda i: (0, 0)),
                   pl.BlockSpec((8, 128), lambda i: (0, 0))],
        out_shape=[sds(dproj.shape, BF16), sds((32, 4096), F32), sds((8, 4096), F32), sds((8, 128), F32)],
        input_output_aliases={17: 0},
        compiler_params=_params("arbitrary"), name=name)(proj, proj, proj, proj, dxc, dxc, dxs_skip, dxs_skip, gdtx,
                                                          dqr, dkr, cosf, sinf, cw, cb, dtb, eexp_t, dproj)


def _xattn_values(x, gain, wq, kv):
    r = _rstd(x)
    h = (x * r) * gain
    q = _dot(h, wq)
    ps, os_ = [], []
    for hd in range(XA_HEADS):
        sl = slice(hd * XA_HEAD_DIM, (hd + 1) * XA_HEAD_DIM)
        sc = _dot_nt(q[:, sl], kv[:, sl]) * (XA_HEAD_DIM ** -0.5)
        e = jnp.exp(sc - jnp.max(sc, axis=-1, keepdims=True))
        p = e / jnp.sum(e, axis=-1, keepdims=True)
        ps.append(p)
        os_.append(_dot(p, kv[:, D_MODEL + hd * XA_HEAD_DIM:D_MODEL + (hd + 1) * XA_HEAD_DIM]))
    return r, h, q, ps, jnp.concatenate(os_, axis=1)


def _xattn_fwd(x, gain, wq, kv, wo, name):
    s = x.shape[0]
    ts = _row_tile(s, 256)

    def body(x_ref, g_ref, wq_ref, kv_ref, wo_ref, o_ref):
        x_ = x_ref[...]
        _, _, _, _, o = _xattn_values(x_, g_ref[...], wq_ref[...], kv_ref[...])
        o_ref[...] = x_ + _dot(o, wo_ref[...])

    row = pl.BlockSpec((ts, D_MODEL), lambda i: (i, 0))
    full = lambda a: pl.BlockSpec(a.shape, lambda i: (0,) * a.ndim)
    return _pcall(
        body, grid=(s // ts,), in_specs=[row, full(gain), full(wq), full(kv), full(wo)], out_specs=row,
        out_shape=jax.ShapeDtypeStruct((s, D_MODEL), F32), compiler_params=_params("parallel"), name=name)(
            x, gain, wq, kv, wo)


def _xattn_bwd(x, dout, gain, wq, kv, wo, name):
    s = x.shape[0]
    m = kv.shape[0]
    ts = _row_tile(s, 256)

    def body(x_ref, do_ref, g_ref, wq_ref, kv_ref, wo_ref, dx_ref, h_ref, dq_ref, o_ref, dkv_ref, dg_ref):
        @pl.when(pl.program_id(0) == 0)
        def _():
            dkv_ref[...] = jnp.zeros_like(dkv_ref)
            dg_ref[...] = jnp.zeros_like(dg_ref)

        x_, do, kvv = x_ref[...], do_ref[...], kv_ref[...]
        r, h, q, ps, o = _xattn_values(x_, g_ref[...], wq_ref[...], kvv)
        dov = _dot_nt(do, wo_ref[...])
        dqs = []
        for hd in range(XA_HEADS):
            sl = slice(hd * XA_HEAD_DIM, (hd + 1) * XA_HEAD_DIM)
            vl = slice(D_MODEL + hd * XA_HEAD_DIM, D_MODEL + (hd + 1) * XA_HEAD_DIM)
            p, doh = ps[hd], dov[:, sl]
            dp = _dot_nt(doh, kvv[:, vl])
            dsc = p * (dp - jnp.sum(dp * p, axis=-1, keepdims=True)) * (XA_HEAD_DIM ** -0.5)
            dqs.append(_dot(dsc, kvv[:, sl]))
            dkv_ref[:, sl] += _dot_tn(dsc, q[:, sl])
            dkv_ref[:, vl] += _dot_tn(p, doh)
        dq = jnp.concatenate(dqs, axis=1)
        dh = _dot_nt(dq, wq_ref[...])
        dg_ref[...] += _colsum(dh * (x_ * r))
        dx_ref[...] = do + _rms_bwd(dh * g_ref[...], x_, r)
        h_ref[...] = _bf(h)
        dq_ref[...] = _bf(dq)
        o_ref[...] = _bf(o)

    row = pl.BlockSpec((ts, D_MODEL), lambda i: (i, 0))
    full = lambda a: pl.BlockSpec(a.shape, lambda i: (0,) * a.ndim)
    sds = jax.ShapeDtypeStruct
    return _pcall(
        body, grid=(s // ts,), in_specs=[row, row, full(gain), full(wq), full(kv), full(wo)],
        out_specs=[row, row, row, row, pl.BlockSpec((m, 2 * D_MODEL), lambda i: (0, 0)),
                   pl.BlockSpec((8, D_MODEL), lambda i: (0, 0))],
        out_shape=[sds((s, D_MODEL), F32), sds((s, D_MODEL), BF16), sds((s, D_MODEL), BF16), sds((s, D_MODEL), BF16),
                   sds((m, 2 * D_MODEL), F32), sds((8, D_MODEL), F32)],
        compiler_params=_params("arbitrary"), name=name)(x, dout, gain, wq, kv, wo)


def _mem_bwd(mem, gain, dkv, wkv, name):
    m = mem.shape[0]

    def body(mem_ref, g_ref, dkv_ref, w_ref, mn_ref, dg_ref):
        mm = mem_ref[...]
        r = _rstd(mm)
        xh = mm * r
        mn_ref[...] = _bf(xh * g_ref[...])
        nb, _, wb = w_ref.shape
        dmn = _dot_nt(dkv_ref[:, 0:wb], w_ref[0])
        for j in range(1, nb):
            dmn = dmn + _dot_nt(dkv_ref[:, j * wb:(j + 1) * wb], w_ref[j])
        dg_ref[...] = jnp.zeros_like(dg_ref) + _colsum(dmn * xh)

    full = lambda a: pl.BlockSpec(a.shape, lambda: (0,) * a.ndim)
    return _pcall(
        body, in_specs=[full(mem), full(gain), full(dkv), full(wkv)],
        out_specs=[pl.BlockSpec((m, D_MODEL), lambda: (0, 0)), pl.BlockSpec((8, D_MODEL), lambda: (0, 0))],
        out_shape=[jax.ShapeDtypeStruct((m, D_MODEL), BF16), jax.ShapeDtypeStruct((8, D_MODEL), F32)],
        compiler_params=pltpu.CompilerParams(vmem_limit_bytes=VMEM_LIMIT), name=name)(mem, gain, dkv, wkv)


def _mlp_fwd(x, gain, w1, w2, name):
    s = x.shape[0]
    ts = _row_tile(s, 512)
    tf = 1024
    nf = D_FF // tf

    def body(x_ref, g_ref, w1_ref, w2_ref, o_ref, h_sc, acc):
        j = pl.program_id(1)

        @pl.when(j == 0)
        def _():
            xx = x_ref[...]
            h_sc[...] = _bf((xx * _rstd(xx)) * g_ref[...])
            acc[...] = jnp.zeros_like(acc)

        a = jnp.dot(h_sc[...], w1_ref[0], preferred_element_type=F32)
        r = jnp.square(jnp.maximum(a, 0.0))
        acc[...] += _dot(r, w2_ref[...])

        @pl.when(j == nf - 1)
        def _():
            o_ref[...] = x_ref[...] + acc[...]

    row = pl.BlockSpec((ts, D_MODEL), lambda i, j: (i, 0))
    return _pcall(
        body, grid=(s // ts, nf),
        in_specs=[row, pl.BlockSpec((1, D_MODEL), lambda i, j: (0, 0)),
                  pl.BlockSpec((1, D_MODEL, tf), lambda i, j: (j, 0, 0)), pl.BlockSpec((tf, D_MODEL), lambda i, j: (j, 0))],
        out_specs=row, out_shape=jax.ShapeDtypeStruct((s, D_MODEL), F32),
        scratch_shapes=[pltpu.VMEM((ts, D_MODEL), BF16), pltpu.VMEM((ts, D_MODEL), F32)],
        compiler_params=_params("parallel", "arbitrary"), name=name)(x, gain, w1, w2)


def _mlp_bwd(x, dout, gain, w1, w2, name):
    s = x.shape[0]
    ts = _row_tile(s, 512)
    tf = 1024
    nf = D_FF // tf

    def body(x_ref, do_ref, g_ref, w1_ref, w2_ref, dx_ref, h_ref, r_ref, da_ref, dg_ref, h_sc, do_sc, acc):
        i, j = pl.program_id(0), pl.program_id(1)

        @pl.when(j == 0)
        def _():
            xx = x_ref[...]
            h_sc[...] = _bf((xx * _rstd(xx)) * g_ref[...])
            do_sc[...] = _bf(do_ref[...])
            acc[...] = jnp.zeros_like(acc)
            h_ref[...] = h_sc[...]

        @pl.when((j == 0) & (i == 0))
        def _():
            dg_ref[...] = jnp.zeros_like(dg_ref)

        a = jnp.dot(h_sc[...], w1_ref[0], preferred_element_type=F32)
        ra = jnp.maximum(a, 0.0)
        r_ref[...] = _bf(ra * ra)
        dr = lax.dot_general(do_sc[...], w2_ref[...], (((1,), (1,)), ((), ())), preferred_element_type=F32)
        da = _bf(dr * 2.0 * ra)
        da_ref[...] = da
        acc[...] += lax.dot_general(da, w1_ref[0], (((1,), (1,)), ((), ())), preferred_element_type=F32)

        @pl.when(j == nf - 1)
        def _():
            xx = x_ref[...]
            r = _rstd(xx)
            dh = acc[...]
            dg_ref[...] += _colsum(dh * (xx * r))
            dx_ref[...] = do_ref[...] + _rms_bwd(dh * g_ref[...], xx, r)

    row = pl.BlockSpec((ts, D_MODEL), lambda i, j: (i, 0))
    ff = pl.BlockSpec((ts, tf), lambda i, j: (i, j))
    sds = jax.ShapeDtypeStruct
    return _pcall(
        body, grid=(s // ts, nf),
        in_specs=[row, row, pl.BlockSpec((1, D_MODEL), lambda i, j: (0, 0)),
                  pl.BlockSpec((1, D_MODEL, tf), lambda i, j: (j, 0, 0)), pl.BlockSpec((tf, D_MODEL), lambda i, j: (j, 0))],
        out_specs=[row, row, ff, ff, pl.BlockSpec((8, D_MODEL), lambda i, j: (0, 0))],
        out_shape=[sds((s, D_MODEL), F32), sds((s, D_MODEL), BF16), sds((s, D_FF), BF16), sds((s, D_FF), BF16),
                   sds((8, D_MODEL), F32)],
        scratch_shapes=[pltpu.VMEM((ts, D_MODEL), BF16), pltpu.VMEM((ts, D_MODEL), BF16), pltpu.VMEM((ts, D_MODEL), F32)],
        compiler_params=_params("arbitrary", "arbitrary"), name=name)(x, dout, gain, w1, w2)


def _final(x, gain, tgt, name):
    s = x.shape[0]
    ts = _row_tile(s, 512)

    def body(x_ref, g_ref, t_ref, dx_ref, loss_ref, dg_ref):
        @pl.when(pl.program_id(0) == 0)
        def _():
            loss_ref[...] = jnp.zeros_like(loss_ref)
            dg_ref[...] = jnp.zeros_like(dg_ref)

        xx = x_ref[...]
        r = _rstd(xx)
        xh = xx * r
        err = xh * g_ref[...] - t_ref[...]
        loss_ref[...] += 0.5 * jnp.sum(jnp.sum(err * err, axis=1, keepdims=True), axis=0, keepdims=True) / D_MODEL
        dy = err * (1.0 / D_MODEL)
        dg_ref[...] += _colsum(dy * xh)
        dx_ref[...] = _rms_bwd(dy * g_ref[...], xx, r)

    row = pl.BlockSpec((ts, D_MODEL), lambda i: (i, 0))
    return _pcall(
        body, grid=(s // ts,), in_specs=[row, pl.BlockSpec((1, D_MODEL), lambda i: (0, 0)), row],
        out_specs=[row, pl.BlockSpec((8, 128), lambda i: (0, 0)), pl.BlockSpec((8, D_MODEL), lambda i: (0, 0))],
        out_shape=[jax.ShapeDtypeStruct((s, D_MODEL), F32), jax.ShapeDtypeStruct((8, 128), F32),
                   jax.ShapeDtypeStruct((8, D_MODEL), F32)],
        compiler_params=_params("arbitrary"), name=name)(x, gain, tgt)


def _as3d(a):
    return a.reshape((-1,) + a.shape[-2:])


def _ew_tile(r, c):
    if r % 256 == 0 or r <= 256:
        return _row_tile(r, 256), c
    return r, 128


def _sum_cast(terms, out_dtype, name):
    shape = terms[0].shape
    t3 = [_as3d(t) for t in terms]
    b, r, c = t3[0].shape
    tr, tc = _ew_tile(r, c)
    nc = c // tc

    def body(*refs):
        acc = refs[0][...].astype(F32)
        for t in refs[1:-1]:
            acc = acc + t[...].astype(F32)
        refs[-1][...] = acc.astype(out_dtype)

    spec = pl.BlockSpec((1, tr, tc), lambda i, j: (i, j // nc, j % nc))
    out = _pcall(body, grid=(b, (r // tr) * nc), in_specs=[spec] * len(t3), out_specs=spec,
                 out_shape=jax.ShapeDtypeStruct((b, r, c), out_dtype), compiler_params=_params("parallel", "parallel"),
                 name=name)(*t3)
    return out.reshape(shape)


def _split_axis(rows):
    return 0 if rows % 64 == 0 else 1


def _half_of(ref, which, lead=()):
    rows, cols = ref.shape[-2], ref.shape[-1]
    if _split_axis(rows) == 0:
        return ref.at[(*lead, pl.ds(which * (rows // 2), rows // 2))]
    return ref.at[(*lead, slice(None), pl.ds(which * (cols // 2), cols // 2))]


def _half_shape(shape):
    rows, cols = shape[-2], shape[-1]
    return (*shape[:-2], rows // 2, cols) if _split_axis(rows) == 0 else (*shape[:-2], rows, cols // 2)


def _pair_sum(a, b, sel, half_id, out_dtype, name):
    _, h, c = b.shape
    k = sel.shape[0]
    by_rows = _split_axis(a.shape[1]) == 0
    tr, tc = _ew_tile(h, c)
    nr, nc = h // tr, c // tc

    def body(sel_ref, hid_ref, a_ref, b_ref, o_ref):
        o_ref[...] = (a_ref[...] + b_ref[...]).astype(out_dtype)

    def a_map(q, j, sel_ref, hid_ref):
        if by_rows:
            return sel_ref[q], hid_ref[0] * nr + j // nc, j % nc
        return sel_ref[q], j // nc, hid_ref[0] * nc + j % nc

    blkshape = (1, tr, tc)
    grid_spec = pltpu.PrefetchScalarGridSpec(
        num_scalar_prefetch=2, grid=(k, nr * nc),
        in_specs=[pl.BlockSpec(blkshape, a_map),
                  pl.BlockSpec(blkshape, lambda q, j, sel_ref, hid_ref: (sel_ref[q], j // nc, j % nc))],
        out_specs=pl.BlockSpec(blkshape, lambda q, j, sel_ref, hid_ref: (q, j // nc, j % nc)))
    return _pcall(body, grid_spec=grid_spec, out_shape=jax.ShapeDtypeStruct((k, h, c), out_dtype),
                  compiler_params=_params("parallel", "parallel"), name=name)(sel, half_id, a, b)


def _adamw(w, g, m, v, name):
    shape = w.shape
    w3, g3, m3, v3 = _as3d(w), _as3d(g), _as3d(m), _as3d(v)
    b, r, c = w3.shape
    tr, tc = _ew_tile(r, c)

    def body(w_ref, g_ref, m_ref, v_ref, d_ref, mo_ref, vo_ref):
        gg = g_ref[...]
        mn = ADAM_B1 * m_ref[...] + (1.0 - ADAM_B1) * gg
        vn = ADAM_B2 * v_ref[...] + (1.0 - ADAM_B2) * jnp.square(gg)
        m_hat = mn / (1.0 - ADAM_B1 ** ADAM_STEP)
        v_hat = vn / (1.0 - ADAM_B2 ** ADAM_STEP)
        d_ref[...] = -ADAM_LR * (m_hat / (jnp.sqrt(v_hat) + ADAM_EPS) + ADAM_WD * w_ref[...])
        mo_ref[...] = mn
        vo_ref[...] = vn

    spec = pl.BlockSpec((1, tr, tc), lambda i, j: (i, j // (c // tc), j % (c // tc)))
    sd = jax.ShapeDtypeStruct((b, r, c), F32)
    d, mo, vo = _pcall(body, grid=(b, (r // tr) * (c // tc)), in_specs=[spec] * 4, out_specs=[spec] * 3,
                       out_shape=[sd] * 3, compiler_params=_params("parallel", "parallel"), name=name)(w3, g3, m3, v3)
    return d.reshape(shape), mo.reshape(shape), vo.reshape(shape)


ANY = pl.BlockSpec(memory_space=pl.ANY)


def _place():
    return lax.axis_index("x"), lax.axis_index("y"), lax.axis_index("c")


def _flip(x, y, r):
    return (1 - x if r & 2 else x), (1 - y if r & 1 else y)


def _dma_sems(*counts):
    return [pltpu.SemaphoreType.DMA((k,)) for k in counts]


def _gather_ici(shards):
    n = len(shards)

    def copies(ins, outs, sems, incoming):
        send, recv = sems
        x, y, c = _place()
        out = []
        for r in (1, 2, 3):
            cx, cy = _flip(x, y, r)
            for a in range(n):
                block = 2 * cx + cy if incoming else 2 * x + y
                out.append(pltpu.make_async_remote_copy(
                    src_ref=_half_of(ins[a], c), dst_ref=_half_of(outs[a], c, (block,)),
                    send_sem=send.at[(r - 1) * n + a], recv_sem=recv.at[(r - 1) * n + a], device_id=(cx, cy, c),
                    device_id_type=MESH))
        return out

    def start(ins, outs, sems):
        for cp in copies(ins, outs, sems, False):
            cp.start()

    def finish(ins, outs, sems):
        for cp in copies(ins, outs, sems, True):
            cp.wait_recv()
        for cp in copies(ins, outs, sems, False):
            cp.wait_send()

    return _Comm(shards, [jax.ShapeDtypeStruct((4,) + a.shape, a.dtype) for a in shards], _dma_sems(3 * n, 3 * n),
                 start, finish)


def _gather_d2d(bufs, shards):
    n = len(bufs)

    def copies(ins, outs, sems, incoming):
        send, recv = sems
        x, y, c = _place()
        out = []
        for r in (1, 2, 3):
            cx, cy = _flip(x, y, r)
            for a in range(n):
                ref = _half_of(outs[a], (1 - c) if incoming else c, (2 * cx + cy,))
                out.append(pltpu.make_async_remote_copy(
                    src_ref=ref, dst_ref=ref, send_sem=send.at[(r - 1) * n + a], recv_sem=recv.at[(r - 1) * n + a],
                    device_id=(x, y, 1 - c), device_id_type=MESH))
        for a in range(n):
            out.append(pltpu.make_async_remote_copy(
                src_ref=ins[n + a], dst_ref=outs[a].at[2 * x + y], send_sem=send.at[3 * n + a],
                recv_sem=recv.at[3 * n + a], device_id=(x, y, 1 - c), device_id_type=MESH))
        return out

    def start(ins, outs, sems):
        for cp in copies(ins, outs, sems, False):
            cp.start()

    def finish(ins, outs, sems):
        for cp in copies(ins, outs, sems, True):
            cp.wait_recv()
        for cp in copies(ins, outs, sems, False):
            cp.wait_send()

    return _Comm(list(bufs) + list(shards), [jax.ShapeDtypeStruct(a.shape, a.dtype) for a in bufs],
                 _dma_sems(4 * n, 4 * n), start, finish, aliases={a: a for a in range(n)})


def _swap_rows(packs):
    n = len(packs)

    def copies(ins, outs, sems):
        send, recv = sems
        x, y, c = _place()
        return [pltpu.make_async_remote_copy(
            src_ref=_half_of(ins[a], 1 - c, (slice(None),)), dst_ref=outs[a], send_sem=send.at[a],
            recv_sem=recv.at[a], device_id=(x, y, 1 - c), device_id_type=MESH) for a in range(n)]

    def start(ins, outs, sems):
        for cp in copies(ins, outs, sems):
            cp.start()

    def finish(ins, outs, sems):
        for cp in copies(ins, outs, sems):
            cp.wait()

    return _Comm(packs, [jax.ShapeDtypeStruct(_half_shape(a.shape), a.dtype) for a in packs], _dma_sems(n, n),
                 start, finish)


def _exchange(arrs):
    n = len(arrs)

    def copies(ins, outs, sems):
        send, recv = sems
        x, y, c = _place()
        out = []
        for r in (1, 2, 3):
            cx, cy = _flip(x, y, r)
            for a in range(n):
                out.append(pltpu.make_async_remote_copy(
                    src_ref=ins[a].at[r - 1], dst_ref=outs[a].at[r - 1], send_sem=send.at[(r - 1) * n + a],
                    recv_sem=recv.at[(r - 1) * n + a], device_id=(cx, cy, c), device_id_type=MESH))
        return out

    def start(ins, outs, sems):
        for cp in copies(ins, outs, sems):
            cp.start()

    def finish(ins, outs, sems):
        for cp in copies(ins, outs, sems):
            cp.wait()

    return _Comm(arrs, [jax.ShapeDtypeStruct(a.shape, a.dtype) for a in arrs], _dma_sems(3 * n, 3 * n), start, finish)


def _to_sibling(arrs):
    n = len(arrs)

    def copies(ins, outs, sems):
        send, recv = sems
        x, y, c = _place()
        return [pltpu.make_async_remote_copy(
            src_ref=ins[a], dst_ref=outs[a], send_sem=send.at[a], recv_sem=recv.at[a],
            device_id=(x, y, 1 - c), device_id_type=MESH) for a in range(n)]

    def start(ins, outs, sems):
        for cp in copies(ins, outs, sems):
            cp.start()

    def finish(ins, outs, sems):
        for cp in copies(ins, outs, sems):
            cp.wait()

    return _Comm(arrs, [jax.ShapeDtypeStruct(a.shape, a.dtype) for a in arrs], _dma_sems(n, n), start, finish)


def _gather8(v, reduce, name):
    rows, w = v.shape

    def body(v_ref, out_ref, buf, send_sems, recv_sems):
        x, y, c = _place()
        me, sibling = (x, y, c), (x, y, 1 - c)
        chips = [_flip(x, y, r) for r in (1, 2, 3)]
        dst = out_ref if not reduce else buf

        def slot(px, py, pc):
            return dst.at[4 * px + 2 * py + pc]

        def copy(k, block, to, src=None):
            return pltpu.make_async_remote_copy(
                src_ref=slot(*block) if src is None else src, dst_ref=slot(*block), send_sem=send_sems.at[k],
                recv_sem=recv_sems.at[k], device_id=to, device_id_type=MESH)

        dst[4 * x + 2 * y + c] = v_ref[...]
        first = [copy(0, me, sibling, src=v_ref)]
        first += [copy(1 + j, me, (*chip, c), src=v_ref) for j, chip in enumerate(chips)]
        for cp in first:
            cp.start()
        passed = [copy(4 + j, (*chip, c), sibling) for j, chip in enumerate(chips)]
        for j, chip in enumerate(chips):
            copy(1 + j, (*chip, c), me).wait_recv()
            passed[j].start()
        copy(0, sibling, me).wait_recv()
        for j, chip in enumerate(chips):
            copy(4 + j, (*chip, 1 - c), me).wait_recv()
        for cp in first + passed:
            cp.wait_send()
        if reduce:
            acc = buf[0]
            for d in range(1, 8):
                acc = acc + buf[d]
            out_ref[...] = acc

    vm = pl.BlockSpec(memory_space=pltpu.VMEM)
    scratch = [pltpu.VMEM((8, rows, w) if reduce else (8, 8, 128), F32), pltpu.SemaphoreType.DMA((7,)),
               pltpu.SemaphoreType.DMA((7,))]
    out_shape = jax.ShapeDtypeStruct((rows, w) if reduce else (8, rows, w), F32)
    return _pcall(body, in_specs=[vm], out_specs=vm, out_shape=out_shape, scratch_shapes=scratch,
                  compiler_params=pltpu.CompilerParams(vmem_limit_bytes=VMEM_LIMIT), name=name)(v)


SMALL = [("norm_mix", 1024), ("b_gate", 2048), ("conv_b", 4096), ("dt_bias", 32), ("a_log", 32), ("d_skip", 32),
         ("ssm_norm", 2048), ("norm_xa", 1024), ("norm_mem", 1024), ("norm_mlp", 1024)]


def _rows_of(width):
    return max(1, width // 1024)


def _pack_rows(pieces):
    out = []
    for p in pieces:
        p = p.astype(F32)
        if p.shape[-1] < 1024:
            p = jnp.pad(p, ((0, 0), (0, 1024 - p.shape[-1])))
        out.append(p.reshape(-1, 1024))
    cat = jnp.concatenate(out, axis=0)
    pad = (-cat.shape[0]) % 8
    return jnp.pad(cat, ((0, pad), (0, 0))) if pad else cat


def _unpack_rows(packed, widths_rows):
    out, at = [], 0
    for r, w in widths_rows:
        k = r * _rows_of(w)
        p = packed[at:at + k]
        at += k
        out.append(p[:, :w] if w < 1024 else p.reshape(r, w))
    return out


def _to_cat(wt):
    pieces = [wt[O_XBC:O_XBC + 4096], wt[O_Q:O_Q + 512], wt[O_K:O_K + 512], wt[O_DT:O_DT + 32],
              jnp.zeros((DT_PAD - 32, wt.shape[1]), wt.dtype), wt[O_Z:O_Z + 2048], wt[O_GATES:O_GATES + 2048],
              wt[O_V:O_V + 1024], wt[O_G:O_G + 1024]]
    return jnp.concatenate(pieces, axis=0)


def _from_cat(gt):
    pieces = [gt[C_Q:C_Q + 512], gt[C_K:C_K + 512], gt[C_V:C_V + 1024], gt[C_G:C_G + 1024],
              gt[C_Z:C_Z + 2048], gt[C_XBC:C_XBC + 4096], gt[C_DT:C_DT + 32], gt[C_GATES:C_GATES + 2048]]
    return jnp.concatenate(pieces, axis=0)


PACK_ROWS = [("mlp_w1", 1024), ("mlp_w2", 1024), ("w_br_ssm", 512), ("w_br_ret", 256), ("w_out", 256), ("xa_wq", 256),
             ("xa_wo", 256)]
PACK_N = sum(r for _, r in PACK_ROWS)


def kernel(x, mem, positions, norm_mix, w_in, b_gate, conv_w, conv_b, dt_bias, a_log, d_skip, ssm_norm, w_br_ret, w_br_ssm, w_out, norm_xa, norm_mem, xa_wq, xa_wkv, xa_wo, norm_mlp, mlp_w1, mlp_w2, norm_final, loss_target, m_norm_mix, m_w_in, m_b_gate, m_conv_w, m_conv_b, m_dt_bias, m_a_log, m_d_skip, m_ssm_norm, m_w_br_ret, m_w_br_ssm, m_w_out, m_norm_xa, m_norm_mem, m_xa_wq, m_xa_wkv, m_xa_wo, m_norm_mlp, m_mlp_w1, m_mlp_w2, m_norm_final, v_norm_mix, v_w_in, v_b_gate, v_conv_w, v_conv_b, v_dt_bias, v_a_log, v_d_skip, v_ssm_norm, v_w_br_ret, v_w_br_ssm, v_w_out, v_norm_xa, v_norm_mem, v_xa_wq, v_xa_wkv, v_xa_wo, v_norm_mlp, v_mlp_w1, v_mlp_w2, v_norm_final):
    W = dict(norm_mix=norm_mix, w_in=w_in, b_gate=b_gate, conv_w=conv_w, conv_b=conv_b, dt_bias=dt_bias, a_log=a_log,
             d_skip=d_skip, ssm_norm=ssm_norm, w_br_ret=w_br_ret, w_br_ssm=w_br_ssm, w_out=w_out, norm_xa=norm_xa,
             norm_mem=norm_mem, xa_wq=xa_wq, xa_wkv=xa_wkv, xa_wo=xa_wo, norm_mlp=norm_mlp, mlp_w1=mlp_w1,
             mlp_w2=mlp_w2, norm_final=norm_final)
    M = dict(norm_mix=m_norm_mix, w_in=m_w_in, b_gate=m_b_gate, conv_w=m_conv_w, conv_b=m_conv_b, dt_bias=m_dt_bias,
             a_log=m_a_log, d_skip=m_d_skip, ssm_norm=m_ssm_norm, w_br_ret=m_w_br_ret, w_br_ssm=m_w_br_ssm,
             w_out=m_w_out, norm_xa=m_norm_xa, norm_mem=m_norm_mem, xa_wq=m_xa_wq, xa_wkv=m_xa_wkv, xa_wo=m_xa_wo,
             norm_mlp=m_norm_mlp, mlp_w1=m_mlp_w1, mlp_w2=m_mlp_w2, norm_final=m_norm_final)
    V = dict(norm_mix=v_norm_mix, w_in=v_w_in, b_gate=v_b_gate, conv_w=v_conv_w, conv_b=v_conv_b, dt_bias=v_dt_bias,
             a_log=v_a_log, d_skip=v_d_skip, ssm_norm=v_ssm_norm, w_br_ret=v_w_br_ret, w_br_ssm=v_w_br_ssm,
             w_out=v_w_out, norm_xa=v_norm_xa, norm_mem=v_norm_mem, xa_wq=v_xa_wq, xa_wkv=v_xa_wkv, xa_wo=v_xa_wo,
             norm_mlp=v_norm_mlp, mlp_w1=v_mlp_w1, mlp_w2=v_mlp_w2, norm_final=v_norm_final)
    nl = w_in.shape[0]
    s = x.shape[1]
    x0 = x[0]
    mem2 = mem[0]
    tgt = loss_target[0]
    blk = 2 * lax.axis_index("x") + lax.axis_index("y")

    grp_b = ["xa_wkv", "w_br_ret", "w_br_ssm", "w_out", "xa_wq", "xa_wo"]
    grp_c = ["mlp_w1", "mlp_w2"]
    win = ["w_in_a", "w_in_b"]
    big = ["w_in"] + grp_b + grp_c
    blk = blk.astype(jnp.int32)
    tr_ = lambda a: jnp.swapaxes(a, 1, 2)
    wb = {k: W[k].astype(BF16) for k in grp_b + grp_c}
    w_in_t = tr_(w_in).astype(BF16)
    wb["w_in_a"], wb["w_in_b"] = w_in_t[:, :WIN_SPLIT], w_in_t[:, WIN_SPLIT:]

    def shards_of(l, ks):
        return [wb[k][l] for k in ks]

    landed = _run_comm(_gather_ici(shards_of(0, win)), "gather_w_in")
    g_win = _run_comm(_gather_d2d(landed, shards_of(0, win)), "gather_w_in_cores")
    cw_all = _gather8(conv_w.reshape(nl * SSM_CONV, 1024), False, "gather_conv_w")
    cw_full = cw_all.reshape(4, 2, nl, SSM_CONV, 1024)[:, 0].transpose(1, 2, 0, 3).reshape(nl, SSM_CONV, SSM_CONV_DIM)

    offs = {}
    at = 0
    for k, r in PACK_ROWS:
        offs[k] = (at, r)
        at += r


    inv_freq = ROPE_THETA ** (-jnp.arange(0, RET_QK_DIM, 2, dtype=F32) / RET_QK_DIM)
    ang = positions.astype(F32)[0][:, None] * inv_freq
    cos, sin = jnp.cos(ang), jnp.sin(ang)
    cosf = jnp.concatenate([cos, cos], axis=1)
    sinf = jnp.concatenate([-sin, sin], axis=1)
    dm, qd, kd, cd = (jnp.asarray(c) for c in _ret_constants())
    eye, blkm = (jnp.asarray(c) for c in _ssd_constants())
    consts = (dm, qd, kd, cd, eye, blkm)
    e_np = _head_expand()
    eexp = jnp.asarray(e_np, BF16)
    eexp_t = jnp.asarray(e_np.T.copy(), BF16)

    saved = []
    xcur = x0
    for l in range(nl):
        wcat = _to_cat(jnp.concatenate(g_win, axis=1).reshape(IN_DIM, D_MODEL))
        cw, cb = cw_full[l], conv_b[l][None]
        dtb = jnp.pad(dt_bias[l], (0, 128 - SSM_HEADS))[None]
        ax = jnp.repeat(-jnp.exp(a_log[l]), 64)[None]
        dsk = jnp.repeat(d_skip[l], 64)[None]
        bg, sn = b_gate[l][None], ssm_norm[l][None]
        more = l + 1 < nl
        ici_b, ici_c = _gather_ici(shards_of(l, grp_b)), _gather_ici(shards_of(l, grp_c))
        _carry(ici_b)
        proj, u = _nmm(xcur, norm_mix[l][None], wcat, "in_proj", save_u=True, w_rows=True)
        _carry(ici_c)
        qr, kr, xc, dtx = _prescan(proj, cosf, sinf, cw, cb, dtb, eexp, "prescan")
        cores = _gather_d2d(ici_b.results + ici_c.results, shards_of(l, grp_b + grp_c))
        _carry(cores)
        yr, ys, sst, hst = _scan_fwd(qr, kr, proj, xc, dtx, ax, consts, "scan_fwd")
        gl = dict(zip(grp_b + grp_c, cores.results))
        rows_weight = lambda k: gl[k].reshape(-1, D_MODEL)
        wr, ws, wo = rows_weight("w_br_ret"), rows_weight("w_br_ssm"), rows_weight("w_out")
        wq, wxo, w2 = rows_weight("xa_wq"), rows_weight("xa_wo"), rows_weight("mlp_w2")
        w1, wkv = gl["mlp_w1"], gl["xa_wkv"]
        ici_a = [_gather_ici(shards_of(l + 1, [k])) for k in win] if more else [None, None]
        _carry(ici_a[0])
        x1 = _postscan_fwd(xcur, yr, ys, xc, proj, bg, dsk, sn, wr, ws, wo, "postscan")
        kv = _bf(_nmm(mem2, norm_mem[l][None], wkv, "mem_kv"))
        _carry(ici_a[1])
        x2 = _xattn_fwd(x1, norm_xa[l][None], wq, kv, wxo, "xattn")
        cores = _gather_d2d(ici_a[0].results + ici_a[1].results, shards_of(l + 1, win)) if more else None
        _carry(cores)
        x3 = _mlp_fwd(x2, norm_mlp[l][None], w1, w2, "mlp")
        if more:
            g_win = cores.results
        saved.append(dict(x0=xcur, x1=x1, x2=x2, proj=proj, u=u, qr=qr, kr=kr, xc=xc, dtx=dtx, yr=yr, ys=ys, sst=sst,
                          hst=hst, kv=kv, wcat=wcat, wr=wr, ws=ws, wo=wo, wq=wq, wxo=wxo, w1=w1, w2=w2, wkv=wkv, cw=cw,
                          cb=cb, dtb=dtb, ax=ax, dsk=dsk, bg=bg, sn=sn))
        xcur = x3

    dx, loss_acc, dnf = _final(xcur, norm_final[None], tgt, "final")
    loss = lax.psum(loss_acc[0, 0], ("x", "y", "c"))

    small_g = [None] * nl
    c = lax.axis_index("c")
    half_id = c.astype(jnp.int32)[None]
    sel_own = blk[None]
    sel_rem = jnp.stack([blk ^ 1, blk ^ 2, blk ^ 3])
    layer_grads = {k: [None] * nl for k in ("pack", "w_in", "xa_wkv")}

    def pair_sums(packs, got):
        own = [_pair_sum(p, g_, sel_own, half_id, F32, "chip_sum_own")[0] for p, g_ in zip(packs, got)]
        out_b = [_pair_sum(p, g_, sel_rem, half_id, BF16, "chip_sum_send") for p, g_ in zip(packs, got)]
        return own, out_b

    def totals(own, inc):
        return [_sum_cast([o, i_[0], i_[1], i_[2]], F32, "grads_total") for o, i_ in zip(own, inc)]

    def finish_layer(lr, red_half, sib_half):
        def whole(i, axis):
            mine_, theirs_ = red_half[i], sib_half[i]
            return jnp.concatenate([jnp.where(c == 0, mine_, theirs_), jnp.where(c == 0, theirs_, mine_)], axis=axis)

        layer_grads["pack"][lr] = whole(0, _split_axis(PACK_N))
        layer_grads["w_in"][lr] = whole(1, _split_axis(IN_DIM // 4))
        layer_grads["xa_wkv"][lr] = whole(2, _split_axis(D_MODEL))

    riding = None
    for l in reversed(range(nl)):
        sv = saved[l]
        swap = _swap_rows(riding[1]) if riding else None
        _carry(swap)
        dx2, hm, rm, dam, dg_mlp = _mlp_bwd(sv["x2"], dx, norm_mlp[l][None], sv["w1"], sv["w2"], "mlp_bwd")
        if riding:
            own, out_b = pair_sums(riding[1], swap.results)
        pack = _mm_tn_into(hm, dam, "dw_mlp1", None, offs["mlp_w1"][0], True)
        pack = _mm_tn_into(rm, dx, "dw_mlp2", pack, offs["mlp_w2"][0], False)
        dx1, hx, dqx, ox, dkv, dg_xa = _xattn_bwd(sv["x1"], dx2, norm_xa[l][None], sv["wq"], sv["kv"], sv["wxo"],
                                                  "xattn_bwd")
        pack = _mm_tn_into(hx, dqx, "dw_xq", pack, offs["xa_wq"][0], False)
        pack = _mm_tn_into(ox, dx2, "dw_xo", pack, offs["xa_wo"][0], False)
        memn, dg_mem = _mem_bwd(mem2, norm_mem[l][None], dkv, sv["wkv"], "mem_bwd")
        dwkv = _mm_tn(memn, dkv, "dw_xkv", col_blocks=4)
        chips_a = _exchange(out_b[0:1]) if riding else None
        _carry(chips_a)
        (dyr, dys, dxs_skip, dproj, yrn, ysn, mg, dbr, dbs, dbg, ddsk, dsn) = _postscan_bwd(
            dx1, sv["yr"], sv["ys"], sv["xc"], sv["proj"], sv["bg"], sv["dsk"], sv["sn"], sv["wr"], sv["ws"], sv["wo"],
            "postscan_bwd")
        pack = _mm_tn_into(mg, dx1, "dw_out", pack, offs["w_out"][0], False)
        pack = _mm_tn_into(yrn, dbr, "dw_br_ret", pack, offs["w_br_ret"][0], False)
        pack = _mm_tn_into(ysn, dbs, "dw_br_ssm", pack, offs["w_br_ssm"][0], False)
        early = [pack, dwkv] if l == 0 else None
        swap_e = _swap_rows(early) if early else None
        chips_b = _exchange(out_b[1:3]) if riding else None
        _carry(_both(chips_b, swap_e))
        dqr, dkr, dproj, dxc, gdtx, da_cols = _scan_bwd(sv["qr"], sv["kr"], sv["proj"], sv["xc"], sv["dtx"], sv["ax"],
                                                        consts, sv["sst"], sv["hst"], dyr, dys, dproj, "scan_bwd")
        if riding:
            red_half = totals(own, chips_a.results + chips_b.results)
        if early:
            own_e, out_e = pair_sums(early, swap_e.results)
        cores = _to_sibling(red_half) if riding else None
        chips_e = _exchange(out_e[0:1]) if early else None
        _carry(_both(cores, chips_e))
        dproj, dcw, dcb, ddtb = _prescan_bwd(sv["proj"], dxc, dxs_skip, gdtx, dqr, dkr, cosf, sinf, sv["cw"], sv["cb"],
                                             sv["dtb"], eexp_t, dproj, "prescan_bwd")
        if riding:
            finish_layer(riding[0], red_half, cores.results)
        chips_kv = _exchange(out_e[1:2]) if early else None
        _carry(chips_kv)
        dwcat = _mm_tn(dproj, sv["u"], "dw_in")
        dx, dg_mix = _in_bwd(dproj, sv["wcat"], sv["x0"], norm_mix[l][None], dx1, "in_bwd")

        da_log = (da_cols.reshape(SSM_HEADS, 64).sum(axis=1)) * (-jnp.exp(a_log[l]))
        dd_skip = ddsk[0].reshape(SSM_HEADS, 64).sum(axis=1)
        small_g[l] = [dg_mix[0:1], dbg[0:1], dcb[0:1], ddtb[0:1, :SSM_HEADS], da_log[None], dd_skip[None], dsn[0:1],
                      dg_xa[0:1], dg_mem[0:1], dg_mlp[0:1], dcw[0::8]]
        riding = (l, [pack, _from_cat(dwcat).reshape(4, IN_DIM // 4, D_MODEL), dwkv])

    w_in_grad = riding[1][1:2]
    own_l, out_l = pair_sums(w_in_grad, _run_comm(_swap_rows(w_in_grad), "grads_core_swap"))
    inc_l = _run_comm(_exchange(out_l), "grads_chip_exchange")
    red_half = totals([own_e[0], own_l[0], own_e[1]], [chips_e.results[0], inc_l[0], chips_kv.results[0]])
    finish_layer(0, red_half, _run_comm(_to_sibling(red_half), "grads_core_join"))
    grad_x = dx[None]

    pieces = []
    for l in range(nl):
        pieces += small_g[l]
    pieces.append(dnf[0:1])
    small_sum = _gather8(_pack_rows(pieces), True, "reduce_small")
    layout = []
    for l in range(nl):
        layout += [(1, w) for _, w in SMALL] + [(SSM_CONV, SSM_CONV_DIM)]
    layout.append((1, 1024))
    red = _unpack_rows(small_sum, layout)
    per = len(SMALL) + 1
    g_small = {k: jnp.concatenate([red[l * per + i] for l in range(nl)], axis=0) for i, (k, _) in enumerate(SMALL)}
    g_convw_full = jnp.stack([red[l * per + len(SMALL)] for l in range(nl)])
    g_small["conv_w"] = lax.dynamic_slice_in_dim(g_convw_full, blk * 1024, 1024, axis=2)
    g_small["norm_final"] = red[-1][0]

    grads = dict(g_small)
    pack_all = jnp.stack(layer_grads["pack"])
    for k, r in PACK_ROWS:
        grads[k] = pack_all[:, offs[k][0]:offs[k][0] + r]
    grads["w_in"] = jnp.stack(layer_grads["w_in"])
    grads["xa_wkv"] = jnp.stack(layer_grads["xa_wkv"])

    delta, new_m, new_v = {}, {}, {}
    for k in ["xa_wkv"] + [k for k, _ in PACK_ROWS]:
        delta[k], new_m[k], new_v[k] = _adamw(W[k], grads[k], M[k], V[k], "adamw_" + k)
    g_in_t = grads["w_in"]
    grads["w_in"] = tr_(g_in_t)
    d_t, m_t, v_t = _adamw(tr_(w_in), g_in_t, tr_(m_w_in), tr_(v_w_in), "adamw_w_in")
    delta["w_in"], new_m["w_in"], new_v["w_in"] = tr_(d_t), tr_(m_t), tr_(v_t)
    small_names = [k for k, _ in SMALL] + ["conv_w", "norm_final"]

    def pack_small(src):
        ps = []
        for k in small_names:
            a = src[k]
            ps.append(a.reshape(-1, a.shape[-1]) if a.ndim > 1 else a[None])
        return _pack_rows(ps)

    ds_, ms_, vs_ = _adamw(pack_small(W), pack_small(grads), pack_small(M), pack_small(V), "adamw_small")
    lay2 = []
    for k in small_names:
        a = W[k]
        lay2.append((int(np.prod(a.shape[:-1])) if a.ndim > 1 else 1, a.shape[-1]))
    for src, dst in ((ds_, delta), (ms_, new_m), (vs_, new_v)):
        for k, piece in zip(small_names, _unpack_rows(src, lay2)):
            dst[k] = piece.reshape(W[k].shape)

    names = ["norm_mix", "w_in", "b_gate", "conv_w", "conv_b", "dt_bias", "a_log", "d_skip", "ssm_norm", "w_br_ret",
             "w_br_ssm", "w_out", "norm_xa", "norm_mem", "xa_wq", "xa_wkv", "xa_wo", "norm_mlp", "mlp_w1", "mlp_w2",
             "norm_final"]
    return (loss, grad_x, *[grads[n] for n in names], *[delta[n] for n in names], *[new_m[n] for n in names],
            *[new_v[n] for n in names])
```

```python
import numpy as np
import jax
import jax.numpy as jnp
from jax import lax
from jax.experimental import pallas as pl
from jax.experimental.pallas import tpu as pltpu

F32 = jnp.float32
BF16 = jnp.bfloat16
MESH = pl.DeviceIdType.MESH

D_MODEL = 1024
CHUNK = 64
EPS = 1e-6
RET_HEADS = 4
RET_QK_DIM = 128
RET_V_DIM = 256
RET_QK = 512
RET_V = 1024
ROPE_THETA = 10000.0
SSM_INNER = 2048
SSM_HEADS = 32
SSM_GROUPS = 8
SSM_STATE = 128
SSM_CONV = 4
SSM_BC = 1024
SSM_CONV_DIM = 4096
XA_HEADS = 4
XA_HEAD_DIM = 256
D_FF = 4096
GROUP_W = 256

DT_PAD = 1024
IN_DIM = 11296
WIN_SPLIT = 1824
NP = 12288
C_XBC, C_Q, C_K, C_DT, C_Z, C_GATES, C_V, C_G = 0, 4096, 4608, 5120, 6144, 8192, 10240, 11264
O_Q, O_K, O_V, O_G, O_Z, O_XBC, O_DT, O_GATES = 0, 512, 1024, 2048, 3072, 5120, 9216, 9248

ADAM_LR = 0.001
ADAM_B1 = 0.9
ADAM_B2 = 0.999
ADAM_EPS = 1e-08
ADAM_WD = 0.01
ADAM_STEP = 10

VMEM_LIMIT = 56 * 1024 * 1024


def _params(*sem):
    return pltpu.CompilerParams(dimension_semantics=sem, vmem_limit_bytes=VMEM_LIMIT)


_CARRY = []


def _carry(comm):
    if comm is not None:
        _CARRY.append(comm)


def _pcall(body, **kw):
    if _CARRY:
        return _hosted(body, _CARRY.pop(), kw)
    return pl.pallas_call(body, **kw)


class _Comm:
    def __init__(self, ins, out_shapes, sems, start, finish, aliases=None):
        self.ins, self.out_shapes, self.sems = list(ins), list(out_shapes), list(sems)
        self.start, self.finish, self.aliases = start, finish, dict(aliases or {})
        self.results, self.parts = None, None

    def deliver(self, results):
        self.results = results
        if self.parts:
            a, b, k = self.parts
            a.deliver(results[:k])
            b.deliver(results[k:])


def _hosted(body, comm, kw):
    in_specs = list(kw.pop("in_specs"))
    out_specs, out_shape = kw.pop("out_specs"), kw.pop("out_shape")
    single = not isinstance(out_shape, (list, tuple))
    if single:
        out_specs, out_shape = [out_specs], [out_shape]
    out_specs, out_shape = list(out_specs), list(out_shape)
    scratch = list(kw.pop("scratch_shapes", []))
    grid = tuple(kw.get("grid", ()))
    aliases = dict(kw.pop("input_output_aliases", {}))
    n_in, n_out, n_sc = len(in_specs), len(out_shape), len(scratch)
    c_in, c_out = len(comm.ins), len(comm.out_shapes)
    for i, o in comm.aliases.items():
        aliases[n_in + i] = n_out + o
    kw["compiler_params"] = _params(*(["arbitrary"] * len(grid)))

    def wrapped(*refs):
        at = 0
        parts = []
        for cnt in (n_in, c_in, n_out, c_out, n_sc):
            parts.append(refs[at:at + cnt])
            at += cnt
        a, ci, b, co, s = parts
        cs = refs[at:]
        first, last = None, None
        for d, size in enumerate(grid):
            f, l_ = pl.program_id(d) == 0, pl.program_id(d) == size - 1
            first = f if first is None else first & f
            last = l_ if last is None else last & l_

        @pl.when(first)
        def _():
            comm.start(ci, co, cs)

        body(*a, *b, *s)

        @pl.when(last)
        def _():
            comm.finish(ci, co, cs)

    call = _pcall(wrapped, in_specs=in_specs + [ANY] * c_in, out_specs=out_specs + [ANY] * c_out,
                  out_shape=out_shape + comm.out_shapes, scratch_shapes=scratch + comm.sems,
                  input_output_aliases=aliases, **kw)

    def run(*ops):
        res = call(*ops, *comm.ins)
        comm.deliver(list(res[n_out:]))
        return res[0] if single else list(res[:n_out])

    return run


def _both(a, b):
    if a is None or b is None:
        return a if b is None else b
    ni, no, ns = len(a.ins), len(a.out_shapes), len(a.sems)

    def start(ins, outs, sems):
        a.start(ins[:ni], outs[:no], sems[:ns])
        b.start(ins[ni:], outs[no:], sems[ns:])

    def finish(ins, outs, sems):
        a.finish(ins[:ni], outs[:no], sems[:ns])
        b.finish(ins[ni:], outs[no:], sems[ns:])

    both = _Comm(a.ins + b.ins, a.out_shapes + b.out_shapes, a.sems + b.sems, start, finish,
                 {**a.aliases, **{ni + i: no + o for i, o in b.aliases.items()}})
    both.parts = (a, b, no)
    return both


def _run_comm(comm, name):
    def body(*refs):
        c_in, c_out = len(comm.ins), len(comm.out_shapes)
        ci, co, cs = refs[:c_in], refs[c_in:c_in + c_out], refs[c_in + c_out:]
        comm.start(ci, co, cs)
        comm.finish(ci, co, cs)

    aliases = {i: o for i, o in comm.aliases.items()}
    res = _pcall(body, in_specs=[ANY] * len(comm.ins), out_specs=[ANY] * len(comm.out_shapes),
                 out_shape=comm.out_shapes, scratch_shapes=comm.sems, input_output_aliases=aliases, name=name)(*comm.ins)
    comm.deliver(list(res))
    return comm.results


def _bf(a):
    return a.astype(BF16)


def _dot(a, b):
    return jnp.dot(_bf(a), _bf(b), preferred_element_type=F32)


def _dot_nt(a, b):
    return lax.dot_general(_bf(a), _bf(b), (((1,), (1,)), ((), ())), preferred_element_type=F32)


def _dot_tn(a, b):
    return lax.dot_general(_bf(a), _bf(b), (((0,), (0,)), ((), ())), preferred_element_type=F32)


def _colsum(a):
    return jnp.sum(a, axis=0, keepdims=True)


def _rstd(x):
    return lax.rsqrt(jnp.mean(x * x, axis=-1, keepdims=True) + EPS)


def _rms_bwd(dy, x, rstd):
    xh = x * rstd
    return rstd * (dy - xh * jnp.mean(dy * xh, axis=-1, keepdims=True))


def _sigmoid(x):
    return 1.0 / (1.0 + jnp.exp(-x))


def _silu_and_grad(x):
    s = _sigmoid(x)
    return x * s, s + x * s * (1.0 - s)


def _softplus(x):
    u = jnp.exp(-jnp.abs(x))
    l1p = jnp.where(u < 1e-4, u * (1.0 - 0.5 * u), jnp.log(1.0 + u))
    return jnp.maximum(x, 0.0) + l1p


def _split3_dot(a, e):
    hi = a.astype(BF16)
    r1 = a - hi.astype(F32)
    mid = r1.astype(BF16)
    lo = (r1 - mid.astype(F32)).astype(BF16)
    return (jnp.dot(hi, e, preferred_element_type=F32) + jnp.dot(mid, e, preferred_element_type=F32)
            + jnp.dot(lo, e, preferred_element_type=F32))


def _cumsum_rows(a):
    rows = lax.broadcasted_iota(jnp.int32, a.shape, 0)
    s = 1
    while s < a.shape[0]:
        a = a + jnp.where(rows >= s, pltpu.roll(a, s, 0), 0.0)
        s *= 2
    return a


def _revcumsum_rows(a):
    n = a.shape[0]
    rows = lax.broadcasted_iota(jnp.int32, a.shape, 0)
    s = 1
    while s < n:
        a = a + jnp.where(rows < n - s, pltpu.roll(a, n - s, 0), 0.0)
        s *= 2
    return a


def _rms_groups(y, width):
    out = []
    for h in range(y.shape[1] // width):
        slab = y[:, h * width:(h + 1) * width]
        out.append((slab, _rstd(slab)))
    return out


def _ret_constants():
    idx = np.arange(CHUNK, dtype=np.float32)
    lg = np.log1p(-(np.float32(2.0) ** (np.float32(-5.0) - np.arange(RET_HEADS, dtype=np.float32)))).astype(np.float32)
    rel = np.abs(idx[:, None] - idx[None, :])
    dm = np.exp(lg[:, None, None] * rel).astype(np.float32)
    qd = np.exp(lg[None, :] * (idx[:, None] + 1.0)).astype(np.float32)
    kd = np.exp(lg[None, :] * (CHUNK - 1.0 - idx[:, None])).astype(np.float32)
    cd = np.exp(lg * CHUNK).astype(np.float32)
    qd = np.repeat(qd, RET_QK_DIM, axis=1)
    kd = np.repeat(kd, RET_QK_DIM, axis=1)
    cd = np.repeat(cd, RET_QK_DIM)[:, None] * np.ones((1, RET_V_DIM), np.float32)
    return dm, qd, kd, cd.astype(np.float32)


def _ssd_constants():
    eye = np.tile(np.eye(CHUNK, dtype=np.float32), (1, GROUP_W // CHUNK))
    blk = np.kron(np.eye(GROUP_W // CHUNK, dtype=np.float32), np.ones((CHUNK, CHUNK), np.float32))
    return eye, blk


def _head_expand():
    e = np.zeros((128, SSM_INNER), np.float32)
    for h in range(SSM_HEADS):
        e[h, h * 64:(h + 1) * 64] = 1.0
    return e


def _ret_chunk_fwd(qh, kh, vh, sh, dmh, qdh, kdh, cdh):
    a = _dot_nt(qh, kh) * dmh
    y = _dot(a, vh) + _dot(qh * qdh, sh)
    s_new = sh * cdh + _dot_tn(kh * kdh, vh)
    return y, s_new


def _ret_chunk_bwd(qh, kh, vh, sh, dmh, qdh, kdh, cdh, dy, ds_new):
    a = _dot_nt(qh, kh) * dmh
    dp = _dot_nt(dy, vh) * dmh
    dq = _dot(dp, kh) + _dot_nt(dy, sh) * qdh
    dk = _dot_tn(dp, qh) + _dot_nt(vh, ds_new) * kdh
    dv = _dot_tn(a, dy) + _dot(kh * kdh, ds_new)
    ds = cdh * ds_new + _dot_tn(qh * qdh, dy)
    return dq, dk, dv, ds


def _ssd_common(xs, dtx, ax, eye):
    cum = _cumsum_rows(dtx * ax)
    last = cum[CHUNK - 1:CHUNK, :]
    r = _colsum(jnp.where(eye > 0.5, cum, 0.0))
    return cum, last, r, xs * dtx


def _tile4(a):
    return jnp.concatenate([a, a, a, a], axis=0)


def _ssd_chunk_fwd(xs, dtx, b, c, ax, hg, eye, blk):
    cum, last, r, x = _ssd_common(xs, dtx, ax, eye)
    lam = jnp.exp(-jnp.abs(cum - r))
    wc = _dot_nt(c, _tile4(b)) * lam
    bd = _tile4(x) * blk
    y = _dot(wc, bd) + _dot(c, hg) * jnp.exp(cum)
    h_new = hg * jnp.exp(last) + _dot_tn(b, x * jnp.exp(last - cum))
    return y, h_new


def _ssd_chunk_bwd(xs, dtx, b, c, ax, hg, eye, blk, dy, dh_new):
    cum, last, r, x = _ssd_common(xs, dtx, ax, eye)
    delta = cum - r
    lam = jnp.exp(-jnp.abs(delta))
    b4 = _tile4(b)
    cb4 = _dot_nt(c, b4)
    wc = cb4 * lam
    bd = _tile4(x) * blk
    ecx = jnp.exp(cum)
    wl = jnp.exp(last - cum)
    ecl = jnp.exp(last)
    z = _dot(c, hg)
    dwc = _dot_nt(dy, bd)
    dbd = _dot_tn(wc, dy) * blk
    dx = dbd[0:64] + dbd[64:128] + dbd[128:192] + dbd[192:256]
    dt_ = _dot(b, dh_new)
    dx = dx + dt_ * wl
    dcb4 = dwc * lam
    dz = dy * ecx
    dc = _dot(dcb4, b4) + _dot_nt(dz, hg)
    db4 = _dot_tn(dcb4, c)
    db = db4[0:64] + db4[64:128] + db4[128:192] + db4[192:256] + _dot_nt(x * wl, dh_new)
    g = dwc * cb4 * lam * (-jnp.sign(delta))
    dr = -_colsum(g)
    dwl = dt_ * x * wl
    u = g + eye * dr + dy * z * ecx - dwl
    lastrow = _colsum(dwl) + _colsum(dh_new * hg) * ecl
    rows = lax.broadcasted_iota(jnp.int32, u.shape, 0)
    u = u + jnp.where(rows == CHUNK - 1, lastrow, 0.0)
    dh = _dot_tn(c, dz) + dh_new * ecl
    rc = _revcumsum_rows(u)
    dxs = dx * dtx
    g_dtx = dx * xs + rc * ax
    da = _colsum(rc * dtx)
    return dxs, g_dtx, db, dc, da, dh


def _row_tile(s, want):
    t = min(s, want)
    assert s % t == 0
    return t


def _nmm(x, gain, w, name, tn=1024, save_u=False, w_rows=False):
    s, d = x.shape
    blocked = w.ndim == 3
    if blocked:
        tn = w.shape[2]
        n = w.shape[0] * tn
        w_spec = pl.BlockSpec((1, d, tn), lambda i, j: (j, 0, 0))
    elif w_rows:
        n = w.shape[0]
        w_spec = pl.BlockSpec((tn, d), lambda i, j: (j, 0))
    else:
        n = w.shape[1]
        w_spec = pl.BlockSpec((d, tn), lambda i, j: (0, j))
    tm = _row_tile(s, 1024)
    assert n % tn == 0

    def body(x_ref, g_ref, w_ref, *rest):
        o_ref, u_sc = rest[0], rest[-1]

        @pl.when(pl.program_id(1) == 0)
        def _():
            xx = x_ref[...]
            u = _bf((xx * _rstd(xx)) * g_ref[...])
            u_sc[...] = u
            if save_u:
                rest[1][...] = u

        if w_rows:
            o_ref[...] = _dot_nt(u_sc[...], w_ref[...])
        else:
            o_ref[...] = jnp.dot(u_sc[...], w_ref[0] if blocked else w_ref[...], preferred_element_type=F32)

    out_shape = [jax.ShapeDtypeStruct((s, n), F32)]
    out_specs = [pl.BlockSpec((tm, tn), lambda i, j: (i, j))]
    if save_u:
        out_shape.append(jax.ShapeDtypeStruct((s, d), BF16))
        out_specs.append(pl.BlockSpec((tm, d), lambda i, j: (i, 0)))
    res = _pcall(
        body, grid=(s // tm, n // tn),
        in_specs=[pl.BlockSpec((tm, d), lambda i, j: (i, 0)), pl.BlockSpec((1, d), lambda i, j: (0, 0)), w_spec],
        out_specs=out_specs, out_shape=out_shape, scratch_shapes=[pltpu.VMEM((tm, d), BF16)],
        compiler_params=_params("parallel", "arbitrary"), name=name)(x, gain, w)
    return res if save_u else res[0]


def _mm_tn(a, b, name, tm=1024, tn=1024, col_blocks=None):
    k, m = a.shape
    n = b.shape[1]
    tk = _row_tile(k, 1024)
    tm, tn = min(tm, m), min(tn, n)
    if col_blocks:
        tn = n // col_blocks
    assert m % tm == 0 and n % tn == 0
    nk = k // tk

    def body(a_ref, b_ref, o_ref, acc):
        kk = pl.program_id(2)

        @pl.when(kk == 0)
        def _():
            acc[...] = jnp.zeros_like(acc)

        acc[...] += _dot_tn(a_ref[...], b_ref[...])

        @pl.when(kk == nk - 1)
        def _():
            if col_blocks:
                o_ref[0] = acc[...]
            else:
                o_ref[...] = acc[...]

    if col_blocks:
        out_spec = pl.BlockSpec((1, tm, tn), lambda i, j, kk: (j, i, 0))
        out_shape = jax.ShapeDtypeStruct((col_blocks, m, tn), F32)
    else:
        out_spec = pl.BlockSpec((tm, tn), lambda i, j, kk: (i, j))
        out_shape = jax.ShapeDtypeStruct((m, n), F32)
    return _pcall(
        body, grid=(m // tm, n // tn, nk),
        in_specs=[pl.BlockSpec((tk, tm), lambda i, j, kk: (kk, i)), pl.BlockSpec((tk, tn), lambda i, j, kk: (kk, j))],
        out_specs=out_spec, out_shape=out_shape,
        scratch_shapes=[pltpu.VMEM((tm, tn), F32)],
        compiler_params=_params("parallel", "parallel", "arbitrary"), name=name)(a, b)


def _mm_tn_into(a, b, name, pack, off, by_cols):
    k, m = a.shape
    n = b.shape[1]
    tk = _row_tile(k, 1024)
    nk = k // tk
    rows = m if by_cols else m // 4
    tm = min(m, 1024)
    nb = 1 if by_cols else tm // rows
    assert tm == nb * rows and off % rows == 0 and n == (4096 if by_cols else 1024)

    def body(a_ref, b_ref, *rest):
        o_ref, acc = rest[-2], rest[-1]
        kk = pl.program_id(2)

        @pl.when(kk == 0)
        def _():
            acc[...] = jnp.zeros_like(acc)

        acc[...] += _dot_tn(a_ref[...], b_ref[...])

        @pl.when(kk == nk - 1)
        def _():
            o_ref[...] = acc[...].reshape(nb, rows, 1024)

    if by_cols:
        out_spec = pl.BlockSpec((1, rows, 1024), lambda i, j, kk: (j, off // rows, 0))
    else:
        out_spec = pl.BlockSpec((nb, rows, 1024), lambda i, j, kk: (i, off // rows, 0))
    in_specs = [pl.BlockSpec((tk, tm), lambda i, j, kk: (kk, i)), pl.BlockSpec((tk, 1024), lambda i, j, kk: (kk, j))]
    ops, alias = [a, b], {}
    if pack is not None:
        in_specs.append(ANY)
        ops.append(pack)
        alias = {2: 0}
    return _pcall(
        body, grid=(m // tm, n // 1024, nk), in_specs=in_specs, out_specs=out_spec,
        out_shape=jax.ShapeDtypeStruct((4, PACK_N, 1024), F32), scratch_shapes=[pltpu.VMEM((tm, 1024), F32)],
        input_output_aliases=alias, compiler_params=_params("parallel", "parallel", "arbitrary"), name=name)(*ops)


def _in_bwd(dproj, wcat_t, x, gain, dres, name):
    s, n = dproj.shape
    d = wcat_t.shape[1]
    tm = _row_tile(s, 1024)
    tk = 1024
    nk = n // tk
    ns = s // tm

    def body(dp_ref, w_ref, x_ref, g_ref, dr_ref, dx_ref, dg_ref, acc):
        i, kk = pl.program_id(0), pl.program_id(1)

        @pl.when(kk == 0)
        def _():
            acc[...] = jnp.zeros_like(acc)

        @pl.when((kk == 0) & (i == 0))
        def _():
            dg_ref[...] = jnp.zeros_like(dg_ref)

        acc[...] += _dot(dp_ref[...], w_ref[...])

        @pl.when(kk == nk - 1)
        def _():
            xx = x_ref[...]
            r = _rstd(xx)
            du = acc[...]
            dg_ref[...] += _colsum(du * (xx * r))
            dx_ref[...] = dr_ref[...] + _rms_bwd(du * g_ref[...], xx, r)

    return _pcall(
        body, grid=(ns, nk),
        in_specs=[pl.BlockSpec((tm, tk), lambda i, kk: (i, kk)), pl.BlockSpec((tk, d), lambda i, kk: (kk, 0)),
                  pl.BlockSpec((tm, d), lambda i, kk: (i, 0)), pl.BlockSpec((1, d), lambda i, kk: (0, 0)),
                  pl.BlockSpec((tm, d), lambda i, kk: (i, 0))],
        out_specs=[pl.BlockSpec((tm, d), lambda i, kk: (i, 0)), pl.BlockSpec((8, d), lambda i, kk: (0, 0))],
        out_shape=[jax.ShapeDtypeStruct((s, d), F32), jax.ShapeDtypeStruct((8, d), F32)],
        scratch_shapes=[pltpu.VMEM((tm, d), F32)],
        compiler_params=_params("arbitrary", "arbitrary"), name=name)(dproj, wcat_t, x, gain, dres)


def _prev_rows_spec(ts, width):
    return pl.BlockSpec((8, width), lambda i: (jnp.maximum(i * (ts // 8) - 1, 0), 0))


def _prescan(proj, cosf, sinf, cw, cb, dtb, eexp, name):
    s = proj.shape[0]
    ts = _row_tile(s, 256)

    def body(xbc_ref, prev_ref, q_ref, k_ref, dt_ref, cos_ref, sin_ref, cw_ref, cb_ref, dtb_ref, e_ref,
             qo_ref, ko_ref, xc_ref, dtx_ref):
        i = pl.program_id(0)
        for st in range(SSM_CONV_DIM // 128):
            sl = slice(st * 128, (st + 1) * 128)
            prev = jnp.where(i > 0, prev_ref[:, sl], 0.0)
            xcat = jnp.concatenate([prev, xbc_ref[:, sl]], axis=0)
            pre = cb_ref[:, sl] + cw_ref[3:4, sl] * xcat[8:8 + ts]
            for j in range(3):
                pre = pre + cw_ref[j:j + 1, sl] * pltpu.roll(xcat, 3 - j, 0)[8:8 + ts]
            xc_ref[:, sl] = pre * _sigmoid(pre)
        cs, sn = cos_ref[...], sin_ref[...]
        for h in range(RET_HEADS):
            sl = slice(h * 128, (h + 1) * 128)
            qh, kh = q_ref[:, sl], k_ref[:, sl]
            qo_ref[:, sl] = qh * cs + pltpu.roll(qh, 64, 1) * sn
            ko_ref[:, sl] = (kh * cs + pltpu.roll(kh, 64, 1) * sn) * (RET_QK_DIM ** -0.5)
        dtv = _softplus(dt_ref[:, 0:128] + dtb_ref[...])
        dtx_ref[...] = _split3_dot(dtv, e_ref[...])

    row = lambda w, c: pl.BlockSpec((ts, w), lambda i: (i, c))
    full = lambda a: pl.BlockSpec(a.shape, lambda i: (0,) * a.ndim)
    return _pcall(
        body, grid=(s // ts,),
        in_specs=[row(4096, 0), _prev_rows_spec(ts, 4096), row(512, C_Q // 512), row(512, C_K // 512),
                  row(DT_PAD, C_DT // DT_PAD), row(128, 0), row(128, 0), full(cw), full(cb), full(dtb), full(eexp)],
        out_specs=[row(512, 0), row(512, 0), row(4096, 0), row(2048, 0)],
        out_shape=[jax.ShapeDtypeStruct((s, 512), F32), jax.ShapeDtypeStruct((s, 512), F32),
                   jax.ShapeDtypeStruct((s, 4096), F32), jax.ShapeDtypeStruct((s, 2048), F32)],
        compiler_params=_params("parallel"), name=name)(proj, proj, proj, proj, proj, cosf, sinf, cw, cb, dtb, eexp)


def _scan_fwd(qr, kr, proj, xc, dtx, ax, consts, name):
    s = qr.shape[0]
    nc = s // CHUNK
    dm, qd, kd, cd, eye, blk = consts

    def body(q_ref, k_ref, v_ref, xc_ref, dtx_ref, ax_ref, dm_ref, qd_ref, kd_ref, cd_ref, eye_ref, blk_ref,
             yr_ref, ys_ref, sst_ref, hst_ref, s_sc, h_sc):
        @pl.when(pl.program_id(0) == 0)
        def _():
            s_sc[...] = jnp.zeros_like(s_sc)
            h_sc[...] = jnp.zeros_like(h_sc)

        sst_ref[0] = s_sc[...]
        hst_ref[0] = h_sc[...]
        for h in range(RET_HEADS):
            ql, vl = slice(h * 128, (h + 1) * 128), slice(h * 256, (h + 1) * 256)
            y, s_new = _ret_chunk_fwd(q_ref[:, ql], k_ref[:, ql], v_ref[:, vl], s_sc[ql, :], dm_ref[h],
                                      qd_ref[:, ql], kd_ref[:, ql], cd_ref[ql, :])
            yr_ref[:, vl] = y
            s_sc[ql, :] = s_new
        eye_v, blk_v = eye_ref[...], blk_ref[...]
        for g in range(SSM_GROUPS):
            sl = slice(g * GROUP_W, (g + 1) * GROUP_W)
            bl = slice(SSM_INNER + g * 128, SSM_INNER + (g + 1) * 128)
            cl = slice(SSM_INNER + SSM_BC + g * 128, SSM_INNER + SSM_BC + (g + 1) * 128)
            y, h_new = _ssd_chunk_fwd(xc_ref[:, sl], dtx_ref[:, sl], xc_ref[:, bl], xc_ref[:, cl], ax_ref[:, sl],
                                      h_sc[:, sl], eye_v, blk_v)
            ys_ref[:, sl] = y
            h_sc[:, sl] = h_new

    row = lambda w, c=0: pl.BlockSpec((CHUNK, w), lambda i: (i, c))
    full = lambda a: pl.BlockSpec(a.shape, lambda i: (0,) * a.ndim)
    return _pcall(
        body, grid=(nc,),
        in_specs=[row(512), row(512), row(1024, C_V // 1024), row(4096), row(2048), full(ax), full(dm), full(qd),
                  full(kd), full(cd), full(eye), full(blk)],
        out_specs=[row(1024), row(2048), pl.BlockSpec((1, 512, 256), lambda i: (i, 0, 0)),
                   pl.BlockSpec((1, 128, 2048), lambda i: (i, 0, 0))],
        out_shape=[jax.ShapeDtypeStruct((s, 1024), F32), jax.ShapeDtypeStruct((s, 2048), F32),
                   jax.ShapeDtypeStruct((nc, 512, 256), F32), jax.ShapeDtypeStruct((nc, 128, 2048), F32)],
        scratch_shapes=[pltpu.VMEM((512, 256), F32), pltpu.VMEM((128, 2048), F32)],
        compiler_params=_params("arbitrary"), name=name)(qr, kr, proj, xc, dtx, ax, dm, qd, kd, cd, eye, blk)


def _scan_bwd(qr, kr, proj, xc, dtx, ax, consts, sst, hst, dyr, dys, dproj, name):
    s = qr.shape[0]
    nc = s // CHUNK
    dm, qd, kd, cd, eye, blk = consts

    def body(q_ref, k_ref, v_ref, xc_ref, dtx_ref, ax_ref, dm_ref, qd_ref, kd_ref, cd_ref, eye_ref, blk_ref,
             sst_ref, hst_ref, dyr_ref, dys_ref, dproj_in, dq_ref, dk_ref, dv_ref, dxc_ref, gdt_ref, da_ref, ds_sc,
             dh_sc):
        @pl.when(pl.program_id(0) == 0)
        def _():
            ds_sc[...] = jnp.zeros_like(ds_sc)
            dh_sc[...] = jnp.zeros_like(dh_sc)
            da_ref[...] = jnp.zeros_like(da_ref)

        for h in range(RET_HEADS):
            ql, vl = slice(h * 128, (h + 1) * 128), slice(h * 256, (h + 1) * 256)
            dq, dk, dv, ds = _ret_chunk_bwd(q_ref[:, ql], k_ref[:, ql], v_ref[:, vl], sst_ref[0, ql, :], dm_ref[h],
                                            qd_ref[:, ql], kd_ref[:, ql], cd_ref[ql, :], dyr_ref[:, vl], ds_sc[ql, :])
            dq_ref[:, ql] = dq
            dk_ref[:, ql] = dk
            dv_ref[:, vl] = _bf(dv)
            ds_sc[ql, :] = ds
        eye_v, blk_v = eye_ref[...], blk_ref[...]
        for g in range(SSM_GROUPS):
            sl = slice(g * GROUP_W, (g + 1) * GROUP_W)
            bl = slice(SSM_INNER + g * 128, SSM_INNER + (g + 1) * 128)
            cl = slice(SSM_INNER + SSM_BC + g * 128, SSM_INNER + SSM_BC + (g + 1) * 128)
            dxs, g_dtx, db, dc, da, dh = _ssd_chunk_bwd(
                xc_ref[:, sl], dtx_ref[:, sl], xc_ref[:, bl], xc_ref[:, cl], ax_ref[:, sl], hst_ref[0, :, sl],
                eye_v, blk_v, dys_ref[:, sl], dh_sc[:, sl])
            dxc_ref[:, sl] = dxs
            dxc_ref[:, bl] = db
            dxc_ref[:, cl] = dc
            gdt_ref[:, sl] = g_dtx
            da_ref[:, sl] += da
            dh_sc[:, sl] = dh

    row = lambda w, c=0: pl.BlockSpec((CHUNK, w), lambda i: (nc - 1 - i, c))
    full = lambda a: pl.BlockSpec(a.shape, lambda i: (0,) * a.ndim)
    return _pcall(
        body, grid=(nc,),
        in_specs=[row(512), row(512), row(1024, C_V // 1024), row(4096), row(2048), full(ax), full(dm), full(qd),
                  full(kd), full(cd), full(eye), full(blk),
                  pl.BlockSpec((1, 512, 256), lambda i: (nc - 1 - i, 0, 0)),
                  pl.BlockSpec((1, 128, 2048), lambda i: (nc - 1 - i, 0, 0)), row(1024), row(2048), ANY],
        out_specs=[row(512), row(512), row(1024, C_V // 1024), row(4096), row(2048),
                   pl.BlockSpec((1, 2048), lambda i: (0, 0))],
        out_shape=[jax.ShapeDtypeStruct((s, 512), F32), jax.ShapeDtypeStruct((s, 512), F32),
                   jax.ShapeDtypeStruct(dproj.shape, BF16), jax.ShapeDtypeStruct((s, 4096), F32),
                   jax.ShapeDtypeStruct((s, 2048), F32), jax.ShapeDtypeStruct((1, 2048), F32)],
        scratch_shapes=[pltpu.VMEM((512, 256), F32), pltpu.VMEM((128, 2048), F32)],
        input_output_aliases={16: 2},
        compiler_params=_params("arbitrary"), name=name)(qr, kr, proj, xc, dtx, ax, dm, qd, kd, cd, eye, blk, sst, hst,
                                                          dyr, dys, dproj)


def _mix_values(yr, g, ys, xs, z, gates, bg, dsk, sn):
    sg, dsg = _silu_and_grad(g)
    ret = _rms_groups(yr, RET_V_DIM)
    yrn = jnp.concatenate([slab * r for slab, r in ret], axis=1) * sg
    sz, dsz = _silu_and_grad(z)
    ys0 = ys + xs * dsk
    ys1 = ys0 * sz
    grp = _rms_groups(ys1, GROUP_W)
    ysh = jnp.concatenate([slab * r for slab, r in grp], axis=1)
    ysn = ysh * sn
    gg = _sigmoid(gates + bg)
    return dict(sg=sg, dsg=dsg, ret=ret, yrn=yrn, sz=sz, dsz=dsz, ys0=ys0, ys1=ys1, grp=grp, ysh=ysh, ysn=ysn,
                gr=gg[:, :D_MODEL], gs=gg[:, D_MODEL:])


def _postscan_fwd(x, yr, ys, xc, proj, bg, dsk, sn, wr, ws, wo, name):
    s = x.shape[0]
    ts = _row_tile(s, 256)

    def body(x_ref, yr_ref, ys_ref, xs_ref, g_ref, z_ref, gt_ref, bg_ref, dsk_ref, sn_ref, wr_ref, ws_ref, wo_ref,
             o_ref):
        m = _mix_values(yr_ref[...], g_ref[...], ys_ref[...], xs_ref[...], z_ref[...], gt_ref[...], bg_ref[...],
                        dsk_ref[...], sn_ref[...])
        merged = m["gr"] * _dot(m["yrn"], wr_ref[...]) + m["gs"] * _dot(m["ysn"], ws_ref[...])
        o_ref[...] = x_ref[...] + _dot(merged, wo_ref[...])

    row = lambda w, c=0: pl.BlockSpec((ts, w), lambda i: (i, c))
    full = lambda a: pl.BlockSpec(a.shape, lambda i: (0,) * a.ndim)
    return _pcall(
        body, grid=(s // ts,),
        in_specs=[row(1024), row(1024), row(2048), row(2048), row(1024, C_G // 1024), row(2048, C_Z // 2048),
                  row(2048, C_GATES // 2048), full(bg), full(dsk), full(sn), full(wr), full(ws), full(wo)],
        out_specs=row(1024), out_shape=jax.ShapeDtypeStruct((s, D_MODEL), F32),
        compiler_params=_params("parallel"), name=name)(x, yr, ys, xc, proj, proj, proj, bg, dsk, sn, wr, ws, wo)


def _postscan_bwd(dout, yr, ys, xc, proj, bg, dsk, sn, wr, ws, wo, name):
    s = dout.shape[0]
    ts = _row_tile(s, 128)

    def body(do_ref, yr_ref, ys_ref, xs_ref, g_ref, z_ref, gt_ref, bg_ref, dsk_ref, sn_ref, wr_ref, ws_ref, wo_ref,
             dyr_ref, dys_ref, dxs_ref, dproj_ref, yrn_ref, ysn_ref, mg_ref, dbr_ref, dbs_ref,
             dbg_ref, ddsk_ref, dsn_ref):
        @pl.when(pl.program_id(0) == 0)
        def _():
            dbg_ref[...] = jnp.zeros_like(dbg_ref)
            ddsk_ref[...] = jnp.zeros_like(ddsk_ref)
            dsn_ref[...] = jnp.zeros_like(dsn_ref)

        xs = xs_ref[...]
        m = _mix_values(yr_ref[...], g_ref[...], ys_ref[...], xs, z_ref[...], gt_ref[...], bg_ref[...],
                        dsk_ref[...], sn_ref[...])
        gr, gs = m["gr"], m["gs"]
        br, bs = _dot(m["yrn"], wr_ref[...]), _dot(m["ysn"], ws_ref[...])
        dmerged = _dot_nt(do_ref[...], wo_ref[...])
        dgt = jnp.concatenate([dmerged * br * gr * (1.0 - gr), dmerged * bs * gs * (1.0 - gs)], axis=1)
        dproj_ref[:, C_GATES:C_GATES + 2048] = _bf(dgt)
        dbg_ref[...] += _colsum(dgt)
        dbr, dbs = dmerged * gr, dmerged * gs
        yrn_ref[...] = _bf(m["yrn"])
        ysn_ref[...] = _bf(m["ysn"])
        mg_ref[...] = _bf(gr * br + gs * bs)
        dbr_ref[...] = _bf(dbr)
        dbs_ref[...] = _bf(dbs)
        dyrn = _dot_nt(dbr, wr_ref[...])
        dysn = _dot_nt(dbs, ws_ref[...])
        rn = jnp.concatenate([slab * r for slab, r in m["ret"]], axis=1)
        dproj_ref[:, C_G:C_G + 1024] = _bf(dyrn * rn * m["dsg"])
        drn = dyrn * m["sg"]
        dyr_ref[...] = jnp.concatenate(
            [_rms_bwd(drn[:, h * RET_V_DIM:(h + 1) * RET_V_DIM], slab, r) for h, (slab, r) in enumerate(m["ret"])], axis=1)
        dsn_ref[...] += _colsum(dysn * m["ysh"])
        dysh = dysn * sn_ref[...]
        dys1 = jnp.concatenate(
            [_rms_bwd(dysh[:, h * GROUP_W:(h + 1) * GROUP_W], slab, r) for h, (slab, r) in enumerate(m["grp"])], axis=1)
        dproj_ref[:, C_Z:C_Z + 2048] = _bf(dys1 * m["ys0"] * m["dsz"])
        dys0 = dys1 * m["sz"]
        dys_ref[...] = dys0
        dxs_ref[...] = dys0 * dsk_ref[...]
        ddsk_ref[...] += _colsum(dys0 * xs)

    row = lambda w, c=0: pl.BlockSpec((ts, w), lambda i: (i, c))
    full = lambda a: pl.BlockSpec(a.shape, lambda i: (0,) * a.ndim)
    acc = lambda w: pl.BlockSpec((8, w), lambda i: (0, 0))
    sds = jax.ShapeDtypeStruct
    return _pcall(
        body, grid=(s // ts,),
        in_specs=[row(1024), row(1024), row(2048), row(2048), row(1024, C_G // 1024), row(2048, C_Z // 2048),
                  row(2048, C_GATES // 2048), full(bg), full(dsk), full(sn), full(wr), full(ws), full(wo)],
        out_specs=[row(1024), row(2048), row(2048), row(NP), row(1024), row(2048), row(1024),
                   row(1024), row(1024), acc(2048), acc(2048), acc(2048)],
        out_shape=[sds((s, 1024), F32), sds((s, 2048), F32), sds((s, 2048), F32), sds((s, NP), BF16),
                   sds((s, 1024), BF16), sds((s, 2048), BF16),
                   sds((s, 1024), BF16), sds((s, 1024), BF16), sds((s, 1024), BF16), sds((8, 2048), F32),
                   sds((8, 2048), F32), sds((8, 2048), F32)],
        compiler_params=_params("arbitrary"), name=name)(dout, yr, ys, xc, proj, proj, proj, bg, dsk, sn, wr, ws, wo)


def _prescan_bwd(proj, dxc, dxs_skip, gdtx, dqr, dkr, cosf, sinf, cw, cb, dtb, eexp_t, dproj, name):
    s = proj.shape[0]
    ts = _row_tile(s, 256)
    nt = s // ts
    m = ts + 8
    width = C_DT + DT_PAD

    def body(xbc_ref, prev_ref, nxt_ref, dt_ref, dxc_ref, dxcn_ref, dsk_ref, dskn_ref, gdt_ref, dq_ref, dk_ref,
             cos_ref, sin_ref, cw_ref, cb_ref, dtb_ref, et_ref, dproj_in, dp_ref, dcw_ref, dcb_ref, ddtb_ref):
        i = pl.program_id(0)

        @pl.when(i == 0)
        def _():
            ddtb_ref[...] = jnp.zeros_like(ddtb_ref)
            dcw_ref[...] = jnp.zeros_like(dcw_ref)
            dcb_ref[...] = jnp.zeros_like(dcb_ref)

        rows = lax.broadcasted_iota(jnp.int32, (m, 128), 0)
        live = (rows < ts) | (i < nt - 1)
        for st in range(SSM_CONV_DIM // 128):
            sl = slice(st * 128, (st + 1) * 128)
            prev = jnp.where(i > 0, prev_ref[:, sl], 0.0)
            xcat = jnp.concatenate([prev, xbc_ref[:, sl], nxt_ref[:, sl]], axis=0)
            shifted = [pltpu.roll(xcat, 3 - j, 0) for j in range(3)] + [xcat]
            pre = cb_ref[:, sl]
            for j in range(SSM_CONV):
                pre = pre + cw_ref[j:j + 1, sl] * shifted[j][8:]
            _, dsilu = _silu_and_grad(pre)
            dxc = jnp.concatenate([dxc_ref[:, sl], dxcn_ref[:, sl]], axis=0)
            if st * 128 < SSM_INNER:
                dxc = dxc + jnp.concatenate([dsk_ref[:, sl], dskn_ref[:, sl]], axis=0)
            dpre = jnp.where(live, dxc * dsilu, 0.0)
            dpt = dpre[0:ts]
            dx = cw_ref[3:4, sl] * dpt
            for j in range(3):
                dx = dx + cw_ref[j:j + 1, sl] * pltpu.roll(dpre, m - (3 - j), 0)[0:ts]
            for j in range(SSM_CONV):
                dcw_ref[8 * j:8 * j + 8, sl] += _colsum(dpt * shifted[j][8:8 + ts])
            dcb_ref[:, sl] += _colsum(dpt)
            dp_ref[:, sl] = _bf(dx)
        cs, sn = cos_ref[...], sin_ref[...]
        for h in range(RET_HEADS):
            sl = slice(h * 128, (h + 1) * 128)
            dq = dq_ref[:, sl]
            dk = dk_ref[:, sl] * (RET_QK_DIM ** -0.5)
            dp_ref[:, C_Q + h * 128:C_Q + (h + 1) * 128] = _bf(dq * cs + pltpu.roll(dq * sn, 64, 1))
            dp_ref[:, C_K + h * 128:C_K + (h + 1) * 128] = _bf(dk * cs + pltpu.roll(dk * sn, 64, 1))
        ddt = _split3_dot(gdt_ref[...], et_ref[...])
        ddt = ddt * _sigmoid(dt_ref[:, 0:128] + dtb_ref[...])
        ddtb_ref[...] += _colsum(ddt)
        dp_ref[:, C_DT:C_DT + 128] = _bf(ddt)
        dp_ref[:, C_DT + 128:C_DT + DT_PAD] = jnp.zeros((ts, DT_PAD - 128), BF16)

    row = lambda w, c=0: pl.BlockSpec((ts, w), lambda i: (i, c))
    nxt = lambda w: pl.BlockSpec((8, w), lambda i: (jnp.minimum((i + 1) * (ts // 8), s // 8 - 1), 0))
    full = lambda a: pl.BlockSpec(a.shape, lambda i: (0,) * a.ndim)
    sds = jax.ShapeDtypeStruct
    return _pcall(
        body, grid=(nt,),
        in_specs=[row(4096), _prev_rows_spec(ts, 4096), nxt(4096), row(DT_PAD, C_DT // DT_PAD), row(4096), nxt(4096),
                  row(2048), nxt(2048), row(2048), row(512), row(512), row(128), row(128), full(cw), full(cb),
                  full(dtb), full(eexp_t), ANY],
        out_specs=[row(width), pl.BlockSpec((32, 4096), lambda i: (0, 0)), pl.BlockSpec((8, 4096), lambda i: (0, 0)),
                   pl.BlockSpec((8, 128), lambda i: (0, 0))],
        out_shape=[sds(dproj.shape, BF16), sds((32, 4096), F32), sds((8, 4096), F32), sds((8, 128), F32)],
        input_output_aliases={17: 0},
        compiler_params=_params("arbitrary"), name=name)(proj, proj, proj, proj, dxc, dxc, dxs_skip, dxs_skip, gdtx,
                                                          dqr, dkr, cosf, sinf, cw, cb, dtb, eexp_t, dproj)


def _xattn_values(x, gain, wq, kv):
    r = _rstd(x)
    h = (x * r) * gain
    q = _dot(h, wq)
    ps, os_ = [], []
    for hd in range(XA_HEADS):
        sl = slice(hd * XA_HEAD_DIM, (hd + 1) * XA_HEAD_DIM)
        sc = _dot_nt(q[:, sl], kv[:, sl]) * (XA_HEAD_DIM ** -0.5)
        e = jnp.exp(sc - jnp.max(sc, axis=-1, keepdims=True))
        p = e / jnp.sum(e, axis=-1, keepdims=True)
        ps.append(p)
        os_.append(_dot(p, kv[:, D_MODEL + hd * XA_HEAD_DIM:D_MODEL + (hd + 1) * XA_HEAD_DIM]))
    return r, h, q, ps, jnp.concatenate(os_, axis=1)


def _xattn_fwd(x, gain, wq, kv, wo, name):
    s = x.shape[0]
    ts = _row_tile(s, 256)

    def body(x_ref, g_ref, wq_ref, kv_ref, wo_ref, o_ref):
        x_ = x_ref[...]
        _, _, _, _, o = _xattn_values(x_, g_ref[...], wq_ref[...], kv_ref[...])
        o_ref[...] = x_ + _dot(o, wo_ref[...])

    row = pl.BlockSpec((ts, D_MODEL), lambda i: (i, 0))
    full = lambda a: pl.BlockSpec(a.shape, lambda i: (0,) * a.ndim)
    return _pcall(
        body, grid=(s // ts,), in_specs=[row, full(gain), full(wq), full(kv), full(wo)], out_specs=row,
        out_shape=jax.ShapeDtypeStruct((s, D_MODEL), F32), compiler_params=_params("parallel"), name=name)(
            x, gain, wq, kv, wo)


def _xattn_bwd(x, dout, gain, wq, kv, wo, name):
    s = x.shape[0]
    m = kv.shape[0]
    ts = _row_tile(s, 256)

    def body(x_ref, do_ref, g_ref, wq_ref, kv_ref, wo_ref, dx_ref, h_ref, dq_ref, o_ref, dkv_ref, dg_ref):
        @pl.when(pl.program_id(0) == 0)
        def _():
            dkv_ref[...] = jnp.zeros_like(dkv_ref)
            dg_ref[...] = jnp.zeros_like(dg_ref)

        x_, do, kvv = x_ref[...], do_ref[...], kv_ref[...]
        r, h, q, ps, o = _xattn_values(x_, g_ref[...], wq_ref[...], kvv)
        dov = _dot_nt(do, wo_ref[...])
        dqs = []
        for hd in range(XA_HEADS):
            sl = slice(hd * XA_HEAD_DIM, (hd + 1) * XA_HEAD_DIM)
            vl = slice(D_MODEL + hd * XA_HEAD_DIM, D_MODEL + (hd + 1) * XA_HEAD_DIM)
            p, doh = ps[hd], dov[:, sl]
            dp = _dot_nt(doh, kvv[:, vl])
            dsc = p * (dp - jnp.sum(dp * p, axis=-1, keepdims=True)) * (XA_HEAD_DIM ** -0.5)
            dqs.append(_dot(dsc, kvv[:, sl]))
            dkv_ref[:, sl] += _dot_tn(dsc, q[:, sl])
            dkv_ref[:, vl] += _dot_tn(p, doh)
        dq = jnp.concatenate(dqs, axis=1)
        dh = _dot_nt(dq, wq_ref[...])
        dg_ref[...] += _colsum(dh * (x_ * r))
        dx_ref[...] = do + _rms_bwd(dh * g_ref[...], x_, r)
        h_ref[...] = _bf(h)
        dq_ref[...] = _bf(dq)
        o_ref[...] = _bf(o)

    row = pl.BlockSpec((ts, D_MODEL), lambda i: (i, 0))
    full = lambda a: pl.BlockSpec(a.shape, lambda i: (0,) * a.ndim)
    sds = jax.ShapeDtypeStruct
    return _pcall(
        body, grid=(s // ts,), in_specs=[row, row, full(gain), full(wq), full(kv), full(wo)],
        out_specs=[row, row, row, row, pl.BlockSpec((m, 2 * D_MODEL), lambda i: (0, 0)),
                   pl.BlockSpec((8, D_MODEL), lambda i: (0, 0))],
        out_shape=[sds((s, D_MODEL), F32), sds((s, D_MODEL), BF16), sds((s, D_MODEL), BF16), sds((s, D_MODEL), BF16),
                   sds((m, 2 * D_MODEL), F32), sds((8, D_MODEL), F32)],
        compiler_params=_params("arbitrary"), name=name)(x, dout, gain, wq, kv, wo)


def _mem_bwd(mem, gain, dkv, wkv, name):
    m = mem.shape[0]

    def body(mem_ref, g_ref, dkv_ref, w_ref, mn_ref, dg_ref):
        mm = mem_ref[...]
        r = _rstd(mm)
        xh = mm * r
        mn_ref[...] = _bf(xh * g_ref[...])
        nb, _, wb = w_ref.shape
        dmn = _dot_nt(dkv_ref[:, 0:wb], w_ref[0])
        for j in range(1, nb):
            dmn = dmn + _dot_nt(dkv_ref[:, j * wb:(j + 1) * wb], w_ref[j])
        dg_ref[...] = jnp.zeros_like(dg_ref) + _colsum(dmn * xh)

    full = lambda a: pl.BlockSpec(a.shape, lambda: (0,) * a.ndim)
    return _pcall(
        body, in_specs=[full(mem), full(gain), full(dkv), full(wkv)],
        out_specs=[pl.BlockSpec((m, D_MODEL), lambda: (0, 0)), pl.BlockSpec((8, D_MODEL), lambda: (0, 0))],
        out_shape=[jax.ShapeDtypeStruct((m, D_MODEL), BF16), jax.ShapeDtypeStruct((8, D_MODEL), F32)],
        compiler_params=pltpu.CompilerParams(vmem_limit_bytes=VMEM_LIMIT), name=name)(mem, gain, dkv, wkv)


def _mlp_fwd(x, gain, w1, w2, name):
    s = x.shape[0]
    ts = _row_tile(s, 512)
    tf = 1024
    nf = D_FF // tf

    def body(x_ref, g_ref, w1_ref, w2_ref, o_ref, h_sc, acc):
        j = pl.program_id(1)

        @pl.when(j == 0)
        def _():
            xx = x_ref[...]
            h_sc[...] = _bf((xx * _rstd(xx)) * g_ref[...])
            acc[...] = jnp.zeros_like(acc)

        a = jnp.dot(h_sc[...], w1_ref[0], preferred_element_type=F32)
        r = jnp.square(jnp.maximum(a, 0.0))
        acc[...] += _dot(r, w2_ref[...])

        @pl.when(j == nf - 1)
        def _():
            o_ref[...] = x_ref[...] + acc[...]

    row = pl.BlockSpec((ts, D_MODEL), lambda i, j: (i, 0))
    return _pcall(
        body, grid=(s // ts, nf),
        in_specs=[row, pl.BlockSpec((1, D_MODEL), lambda i, j: (0, 0)),
                  pl.BlockSpec((1, D_MODEL, tf), lambda i, j: (j, 0, 0)), pl.BlockSpec((tf, D_MODEL), lambda i, j: (j, 0))],
        out_specs=row, out_shape=jax.ShapeDtypeStruct((s, D_MODEL), F32),
        scratch_shapes=[pltpu.VMEM((ts, D_MODEL), BF16), pltpu.VMEM((ts, D_MODEL), F32)],
        compiler_params=_params("parallel", "arbitrary"), name=name)(x, gain, w1, w2)


def _mlp_bwd(x, dout, gain, w1, w2, name):
    s = x.shape[0]
    ts = _row_tile(s, 512)
    tf = 1024
    nf = D_FF // tf

    def body(x_ref, do_ref, g_ref, w1_ref, w2_ref, dx_ref, h_ref, r_ref, da_ref, dg_ref, h_sc, do_sc, acc):
        i, j = pl.program_id(0), pl.program_id(1)

        @pl.when(j == 0)
        def _():
            xx = x_ref[...]
            h_sc[...] = _bf((xx * _rstd(xx)) * g_ref[...])
            do_sc[...] = _bf(do_ref[...])
            acc[...] = jnp.zeros_like(acc)
            h_ref[...] = h_sc[...]

        @pl.when((j == 0) & (i == 0))
        def _():
            dg_ref[...] = jnp.zeros_like(dg_ref)

        a = jnp.dot(h_sc[...], w1_ref[0], preferred_element_type=F32)
        ra = jnp.maximum(a, 0.0)
        r_ref[...] = _bf(ra * ra)
        dr = lax.dot_general(do_sc[...], w2_ref[...], (((1,), (1,)), ((), ())), preferred_element_type=F32)
        da = _bf(dr * 2.0 * ra)
        da_ref[...] = da
        acc[...] += lax.dot_general(da, w1_ref[0], (((1,), (1,)), ((), ())), preferred_element_type=F32)

        @pl.when(j == nf - 1)
        def _():
            xx = x_ref[...]
            r = _rstd(xx)
            dh = acc[...]
            dg_ref[...] += _colsum(dh * (xx * r))
            dx_ref[...] = do_ref[...] + _rms_bwd(dh * g_ref[...], xx, r)

    row = pl.BlockSpec((ts, D_MODEL), lambda i, j: (i, 0))
    ff = pl.BlockSpec((ts, tf), lambda i, j: (i, j))
    sds = jax.ShapeDtypeStruct
    return _pcall(
        body, grid=(s // ts, nf),
        in_specs=[row, row, pl.BlockSpec((1, D_MODEL), lambda i, j: (0, 0)),
                  pl.BlockSpec((1, D_MODEL, tf), lambda i, j: (j, 0, 0)), pl.BlockSpec((tf, D_MODEL), lambda i, j: (j, 0))],
        out_specs=[row, row, ff, ff, pl.BlockSpec((8, D_MODEL), lambda i, j: (0, 0))],
        out_shape=[sds((s, D_MODEL), F32), sds((s, D_MODEL), BF16), sds((s, D_FF), BF16), sds((s, D_FF), BF16),
                   sds((8, D_MODEL), F32)],
        scratch_shapes=[pltpu.VMEM((ts, D_MODEL), BF16), pltpu.VMEM((ts, D_MODEL), BF16), pltpu.VMEM((ts, D_MODEL), F32)],
        compiler_params=_params("arbitrary", "arbitrary"), name=name)(x, dout, gain, w1, w2)


def _final(x, gain, tgt, name):
    s = x.shape[0]
    ts = _row_tile(s, 512)

    def body(x_ref, g_ref, t_ref, dx_ref, loss_ref, dg_ref):
        @pl.when(pl.program_id(0) == 0)
        def _():
            loss_ref[...] = jnp.zeros_like(loss_ref)
            dg_ref[...] = jnp.zeros_like(dg_ref)

        xx = x_ref[...]
        r = _rstd(xx)
        xh = xx * r
        err = xh * g_ref[...] - t_ref[...]
        loss_ref[...] += 0.5 * jnp.sum(jnp.sum(err * err, axis=1, keepdims=True), axis=0, keepdims=True) / D_MODEL
        dy = err * (1.0 / D_MODEL)
        dg_ref[...] += _colsum(dy * xh)
        dx_ref[...] = _rms_bwd(dy * g_ref[...], xx, r)

    row = pl.BlockSpec((ts, D_MODEL), lambda i: (i, 0))
    return _pcall(
        body, grid=(s // ts,), in_specs=[row, pl.BlockSpec((1, D_MODEL), lambda i: (0, 0)), row],
        out_specs=[row, pl.BlockSpec((8, 128), lambda i: (0, 0)), pl.BlockSpec((8, D_MODEL), lambda i: (0, 0))],
        out_shape=[jax.ShapeDtypeStruct((s, D_MODEL), F32), jax.ShapeDtypeStruct((8, 128), F32),
                   jax.ShapeDtypeStruct((8, D_MODEL), F32)],
        compiler_params=_params("arbitrary"), name=name)(x, gain, tgt)


def _as3d(a):
    return a.reshape((-1,) + a.shape[-2:])


def _ew_tile(r, c):
    if r % 256 == 0 or r <= 256:
        return _row_tile(r, 256), c
    return r, 128


def _sum_cast(terms, out_dtype, name):
    shape = terms[0].shape
    t3 = [_as3d(t) for t in terms]
    b, r, c = t3[0].shape
    tr, tc = _ew_tile(r, c)
    nc = c // tc

    def body(*refs):
        acc = refs[0][...].astype(F32)
        for t in refs[1:-1]:
            acc = acc + t[...].astype(F32)
        refs[-1][...] = acc.astype(out_dtype)

    spec = pl.BlockSpec((1, tr, tc), lambda i, j: (i, j // nc, j % nc))
    out = _pcall(body, grid=(b, (r // tr) * nc), in_specs=[spec] * len(t3), out_specs=spec,
                 out_shape=jax.ShapeDtypeStruct((b, r, c), out_dtype), compiler_params=_params("parallel", "parallel"),
                 name=name)(*t3)
    return out.reshape(shape)


def _split_axis(rows):
    return 0 if rows % 64 == 0 else 1


def _half_of(ref, which, lead=()):
    rows, cols = ref.shape[-2], ref.shape[-1]
    if _split_axis(rows) == 0:
        return ref.at[(*lead, pl.ds(which * (rows // 2), rows // 2))]
    return ref.at[(*lead, slice(None), pl.ds(which * (cols // 2), cols // 2))]


def _half_shape(shape):
    rows, cols = shape[-2], shape[-1]
    return (*shape[:-2], rows // 2, cols) if _split_axis(rows) == 0 else (*shape[:-2], rows, cols // 2)


def _pair_sum(a, b, sel, half_id, out_dtype, name):
    _, h, c = b.shape
    k = sel.shape[0]
    by_rows = _split_axis(a.shape[1]) == 0
    tr, tc = _ew_tile(h, c)
    nr, nc = h // tr, c // tc

    def body(sel_ref, hid_ref, a_ref, b_ref, o_ref):
        o_ref[...] = (a_ref[...] + b_ref[...]).astype(out_dtype)

    def a_map(q, j, sel_ref, hid_ref):
        if by_rows:
            return sel_ref[q], hid_ref[0] * nr + j // nc, j % nc
        return sel_ref[q], j // nc, hid_ref[0] * nc + j % nc

    blkshape = (1, tr, tc)
    grid_spec = pltpu.PrefetchScalarGridSpec(
        num_scalar_prefetch=2, grid=(k, nr * nc),
        in_specs=[pl.BlockSpec(blkshape, a_map),
                  pl.BlockSpec(blkshape, lambda q, j, sel_ref, hid_ref: (sel_ref[q], j // nc, j % nc))],
        out_specs=pl.BlockSpec(blkshape, lambda q, j, sel_ref, hid_ref: (q, j // nc, j % nc)))
    return _pcall(body, grid_spec=grid_spec, out_shape=jax.ShapeDtypeStruct((k, h, c), out_dtype),
                  compiler_params=_params("parallel", "parallel"), name=name)(sel, half_id, a, b)


def _adamw(w, g, m, v, name):
    shape = w.shape
    w3, g3, m3, v3 = _as3d(w), _as3d(g), _as3d(m), _as3d(v)
    b, r, c = w3.shape
    tr, tc = _ew_tile(r, c)

    def body(w_ref, g_ref, m_ref, v_ref, d_ref, mo_ref, vo_ref):
        gg = g_ref[...]
        mn = ADAM_B1 * m_ref[...] + (1.0 - ADAM_B1) * gg
        vn = ADAM_B2 * v_ref[...] + (1.0 - ADAM_B2) * jnp.square(gg)
        m_hat = mn / (1.0 - ADAM_B1 ** ADAM_STEP)
        v_hat = vn / (1.0 - ADAM_B2 ** ADAM_STEP)
        d_ref[...] = -ADAM_LR * (m_hat / (jnp.sqrt(v_hat) + ADAM_EPS) + ADAM_WD * w_ref[...])
        mo_ref[...] = mn
        vo_ref[...] = vn

    spec = pl.BlockSpec((1, tr, tc), lambda i, j: (i, j // (c // tc), j % (c // tc)))
    sd = jax.ShapeDtypeStruct((b, r, c), F32)
    d, mo, vo = _pcall(body, grid=(b, (r // tr) * (c // tc)), in_specs=[spec] * 4, out_specs=[spec] * 3,
                       out_shape=[sd] * 3, compiler_params=_params("parallel", "parallel"), name=name)(w3, g3, m3, v3)
    return d.reshape(shape), mo.reshape(shape), vo.reshape(shape)


ANY = pl.BlockSpec(memory_space=pl.ANY)


def _place():
    return lax.axis_index("x"), lax.axis_index("y"), lax.axis_index("c")


def _flip(x, y, r):
    return (1 - x if r & 2 else x), (1 - y if r & 1 else y)


def _dma_sems(*counts):
    return [pltpu.SemaphoreType.DMA((k,)) for k in counts]


def _gather_ici(shards, rows=None, into=None):
    n = len(shards)
    assert rows is None or all(_split_axis(a.shape[0]) == 1 for a in shards)

    def piece(ref):
        return ref if rows is None else ref.at[pl.ds(rows[0], rows[1])]

    def copies(ins, outs, sems, incoming):
        send, recv = sems
        x, y, c = _place()
        out = []
        for r in (1, 2, 3):
            cx, cy = _flip(x, y, r)
            for a in range(n):
                block = 2 * cx + cy if incoming else 2 * x + y
                if rows is None:
                    src, dst = _half_of(ins[a], c), _half_of(outs[a], c, (block,))
                else:
                    hc = ins[a].shape[1] // 2
                    src = piece(ins[a]).at[:, pl.ds(c * hc, hc)]
                    dst = piece(outs[a].at[block]).at[:, pl.ds(c * hc, hc)]
                out.append(pltpu.make_async_remote_copy(
                    src_ref=src, dst_ref=dst, send_sem=send.at[(r - 1) * n + a], recv_sem=recv.at[(r - 1) * n + a],
                    device_id=(cx, cy, c), device_id_type=MESH))
        return out

    def start(ins, outs, sems):
        for cp in copies(ins, outs, sems, False):
            cp.start()

    def finish(ins, outs, sems):
        for cp in copies(ins, outs, sems, True):
            cp.wait_recv()
        for cp in copies(ins, outs, sems, False):
            cp.wait_send()

    return _Comm(list(shards) + list(into or []), [jax.ShapeDtypeStruct((4,) + a.shape, a.dtype) for a in shards],
                 _dma_sems(3 * n, 3 * n), start, finish, aliases={n + a: a for a in range(n)} if into else None)


def _gather_d2d(bufs, shards):
    n = len(bufs)

    def copies(ins, outs, sems, incoming):
        send, recv = sems
        x, y, c = _place()
        out = []
        for r in (1, 2, 3):
            cx, cy = _flip(x, y, r)
            for a in range(n):
                ref = _half_of(outs[a], (1 - c) if incoming else c, (2 * cx + cy,))
                out.append(pltpu.make_async_remote_copy(
                    src_ref=ref, dst_ref=ref, send_sem=send.at[(r - 1) * n + a], recv_sem=recv.at[(r - 1) * n + a],
                    device_id=(x, y, 1 - c), device_id_type=MESH))
        for a in range(n):
            out.append(pltpu.make_async_remote_copy(
                src_ref=ins[n + a], dst_ref=outs[a].at[2 * x + y], send_sem=send.at[3 * n + a],
                recv_sem=recv.at[3 * n + a], device_id=(x, y, 1 - c), device_id_type=MESH))
        return out

    def start(ins, outs, sems):
        for cp in copies(ins, outs, sems, False):
            cp.start()

    def finish(ins, outs, sems):
        for cp in copies(ins, outs, sems, True):
            cp.wait_recv()
        for cp in copies(ins, outs, sems, False):
            cp.wait_send()

    return _Comm(list(bufs) + list(shards), [jax.ShapeDtypeStruct(a.shape, a.dtype) for a in bufs],
                 _dma_sems(4 * n, 4 * n), start, finish, aliases={a: a for a in range(n)})


def _swap_rows(packs):
    n = len(packs)

    def copies(ins, outs, sems):
        send, recv = sems
        x, y, c = _place()
        return [pltpu.make_async_remote_copy(
            src_ref=_half_of(ins[a], 1 - c, (slice(None),)), dst_ref=outs[a], send_sem=send.at[a],
            recv_sem=recv.at[a], device_id=(x, y, 1 - c), device_id_type=MESH) for a in range(n)]

    def start(ins, outs, sems):
        for cp in copies(ins, outs, sems):
            cp.start()

    def finish(ins, outs, sems):
        for cp in copies(ins, outs, sems):
            cp.wait()

    return _Comm(packs, [jax.ShapeDtypeStruct(_half_shape(a.shape), a.dtype) for a in packs], _dma_sems(n, n),
                 start, finish)


def _exchange(arrs):
    n = len(arrs)

    def copies(ins, outs, sems):
        send, recv = sems
        x, y, c = _place()
        out = []
        for r in (1, 2, 3):
            cx, cy = _flip(x, y, r)
            for a in range(n):
                out.append(pltpu.make_async_remote_copy(
                    src_ref=ins[a].at[r - 1], dst_ref=outs[a].at[r - 1], send_sem=send.at[(r - 1) * n + a],
                    recv_sem=recv.at[(r - 1) * n + a], device_id=(cx, cy, c), device_id_type=MESH))
        return out

    def start(ins, outs, sems):
        for cp in copies(ins, outs, sems):
            cp.start()

    def finish(ins, outs, sems):
        for cp in copies(ins, outs, sems):
            cp.wait()

    return _Comm(arrs, [jax.ShapeDtypeStruct(a.shape, a.dtype) for a in arrs], _dma_sems(3 * n, 3 * n), start, finish)


def _to_sibling(arrs):
    n = len(arrs)

    def copies(ins, outs, sems):
        send, recv = sems
        x, y, c = _place()
        return [pltpu.make_async_remote_copy(
            src_ref=ins[a], dst_ref=outs[a], send_sem=send.at[a], recv_sem=recv.at[a],
            device_id=(x, y, 1 - c), device_id_type=MESH) for a in range(n)]

    def start(ins, outs, sems):
        for cp in copies(ins, outs, sems):
            cp.start()

    def finish(ins, outs, sems):
        for cp in copies(ins, outs, sems):
            cp.wait()

    return _Comm(arrs, [jax.ShapeDtypeStruct(a.shape, a.dtype) for a in arrs], _dma_sems(n, n), start, finish)


def _gather8(v, reduce, name):
    rows, w = v.shape

    def body(v_ref, out_ref, buf, send_sems, recv_sems):
        x, y, c = _place()
        me, sibling = (x, y, c), (x, y, 1 - c)
        chips = [_flip(x, y, r) for r in (1, 2, 3)]
        dst = out_ref if not reduce else buf

        def slot(px, py, pc):
            return dst.at[4 * px + 2 * py + pc]

        def copy(k, block, to, src=None):
            return pltpu.make_async_remote_copy(
                src_ref=slot(*block) if src is None else src, dst_ref=slot(*block), send_sem=send_sems.at[k],
                recv_sem=recv_sems.at[k], device_id=to, device_id_type=MESH)

        dst[4 * x + 2 * y + c] = v_ref[...]
        first = [copy(0, me, sibling, src=v_ref)]
        first += [copy(1 + j, me, (*chip, c), src=v_ref) for j, chip in enumerate(chips)]
        for cp in first:
            cp.start()
        passed = [copy(4 + j, (*chip, c), sibling) for j, chip in enumerate(chips)]
        for j, chip in enumerate(chips):
            copy(1 + j, (*chip, c), me).wait_recv()
            passed[j].start()
        copy(0, sibling, me).wait_recv()
        for j, chip in enumerate(chips):
            copy(4 + j, (*chip, 1 - c), me).wait_recv()
        for cp in first + passed:
            cp.wait_send()
        if reduce:
            acc = buf[0]
            for d in range(1, 8):
                acc = acc + buf[d]
            out_ref[...] = acc

    vm = pl.BlockSpec(memory_space=pltpu.VMEM)
    scratch = [pltpu.VMEM((8, rows, w) if reduce else (8, 8, 128), F32), pltpu.SemaphoreType.DMA((7,)),
               pltpu.SemaphoreType.DMA((7,))]
    out_shape = jax.ShapeDtypeStruct((rows, w) if reduce else (8, rows, w), F32)
    return _pcall(body, in_specs=[vm], out_specs=vm, out_shape=out_shape, scratch_shapes=scratch,
                  compiler_params=pltpu.CompilerParams(vmem_limit_bytes=VMEM_LIMIT), name=name)(v)


SMALL = [("norm_mix", 1024), ("b_gate", 2048), ("conv_b", 4096), ("dt_bias", 32), ("a_log", 32), ("d_skip", 32),
         ("ssm_norm", 2048), ("norm_xa", 1024), ("norm_mem", 1024), ("norm_mlp", 1024)]


def _rows_of(width):
    return max(1, width // 1024)


def _pack_rows(pieces):
    out = []
    for p in pieces:
        p = p.astype(F32)
        if p.shape[-1] < 1024:
            p = jnp.pad(p, ((0, 0), (0, 1024 - p.shape[-1])))
        out.append(p.reshape(-1, 1024))
    cat = jnp.concatenate(out, axis=0)
    pad = (-cat.shape[0]) % 8
    return jnp.pad(cat, ((0, pad), (0, 0))) if pad else cat


def _unpack_rows(packed, widths_rows):
    out, at = [], 0
    for r, w in widths_rows:
        k = r * _rows_of(w)
        p = packed[at:at + k]
        at += k
        out.append(p[:, :w] if w < 1024 else p.reshape(r, w))
    return out


def _to_cat(wt):
    pieces = [wt[O_XBC:O_XBC + 4096], wt[O_Q:O_Q + 512], wt[O_K:O_K + 512], wt[O_DT:O_DT + 32],
              jnp.zeros((DT_PAD - 32, wt.shape[1]), wt.dtype), wt[O_Z:O_Z + 2048], wt[O_GATES:O_GATES + 2048],
              wt[O_V:O_V + 1024], wt[O_G:O_G + 1024]]
    return jnp.concatenate(pieces, axis=0)


def _from_cat(gt):
    pieces = [gt[C_Q:C_Q + 512], gt[C_K:C_K + 512], gt[C_V:C_V + 1024], gt[C_G:C_G + 1024],
              gt[C_Z:C_Z + 2048], gt[C_XBC:C_XBC + 4096], gt[C_DT:C_DT + 32], gt[C_GATES:C_GATES + 2048]]
    return jnp.concatenate(pieces, axis=0)


PACK_ROWS = [("mlp_w1", 1024), ("mlp_w2", 1024), ("w_br_ssm", 512), ("w_br_ret", 256), ("w_out", 256), ("xa_wq", 256),
             ("xa_wo", 256)]
PACK_N = sum(r for _, r in PACK_ROWS)


def kernel(x, mem, positions, norm_mix, w_in, b_gate, conv_w, conv_b, dt_bias, a_log, d_skip, ssm_norm, w_br_ret, w_br_ssm, w_out, norm_xa, norm_mem, xa_wq, xa_wkv, xa_wo, norm_mlp, mlp_w1, mlp_w2, norm_final, loss_target, m_norm_mix, m_w_in, m_b_gate, m_conv_w, m_conv_b, m_dt_bias, m_a_log, m_d_skip, m_ssm_norm, m_w_br_ret, m_w_br_ssm, m_w_out, m_norm_xa, m_norm_mem, m_xa_wq, m_xa_wkv, m_xa_wo, m_norm_mlp, m_mlp_w1, m_mlp_w2, m_norm_final, v_norm_mix, v_w_in, v_b_gate, v_conv_w, v_conv_b, v_dt_bias, v_a_log, v_d_skip, v_ssm_norm, v_w_br_ret, v_w_br_ssm, v_w_out, v_norm_xa, v_norm_mem, v_xa_wq, v_xa_wkv, v_xa_wo, v_norm_mlp, v_mlp_w1, v_mlp_w2, v_norm_final):
    W = dict(norm_mix=norm_mix, w_in=w_in, b_gate=b_gate, conv_w=conv_w, conv_b=conv_b, dt_bias=dt_bias, a_log=a_log,
             d_skip=d_skip, ssm_norm=ssm_norm, w_br_ret=w_br_ret, w_br_ssm=w_br_ssm, w_out=w_out, norm_xa=norm_xa,
             norm_mem=norm_mem, xa_wq=xa_wq, xa_wkv=xa_wkv, xa_wo=xa_wo, norm_mlp=norm_mlp, mlp_w1=mlp_w1,
             mlp_w2=mlp_w2, norm_final=norm_final)
    M = dict(norm_mix=m_norm_mix, w_in=m_w_in, b_gate=m_b_gate, conv_w=m_conv_w, conv_b=m_conv_b, dt_bias=m_dt_bias,
             a_log=m_a_log, d_skip=m_d_skip, ssm_norm=m_ssm_norm, w_br_ret=m_w_br_ret, w_br_ssm=m_w_br_ssm,
             w_out=m_w_out, norm_xa=m_norm_xa, norm_mem=m_norm_mem, xa_wq=m_xa_wq, xa_wkv=m_xa_wkv, xa_wo=m_xa_wo,
             norm_mlp=m_norm_mlp, mlp_w1=m_mlp_w1, mlp_w2=m_mlp_w2, norm_final=m_norm_final)
    V = dict(norm_mix=v_norm_mix, w_in=v_w_in, b_gate=v_b_gate, conv_w=v_conv_w, conv_b=v_conv_b, dt_bias=v_dt_bias,
             a_log=v_a_log, d_skip=v_d_skip, ssm_norm=v_ssm_norm, w_br_ret=v_w_br_ret, w_br_ssm=v_w_br_ssm,
             w_out=v_w_out, norm_xa=v_norm_xa, norm_mem=v_norm_mem, xa_wq=v_xa_wq, xa_wkv=v_xa_wkv, xa_wo=v_xa_wo,
             norm_mlp=v_norm_mlp, mlp_w1=v_mlp_w1, mlp_w2=v_mlp_w2, norm_final=v_norm_final)
    nl = w_in.shape[0]
    s = x.shape[1]
    x0 = x[0]
    mem2 = mem[0]
    tgt = loss_target[0]
    blk = 2 * lax.axis_index("x") + lax.axis_index("y")

    grp_b = ["xa_wkv", "w_br_ret", "w_br_ssm", "w_out", "xa_wq", "xa_wo"]
    grp_c = ["mlp_w1", "mlp_w2"]
    win = ["w_in"]
    blk = blk.astype(jnp.int32)
    tr_ = lambda a: jnp.swapaxes(a, 1, 2)
    wb = {k: W[k].astype(BF16) for k in grp_b + grp_c}
    wb["w_in"] = tr_(w_in).astype(BF16)

    def shards_of(l, ks):
        return [wb[k][l] for k in ks]

    landed = _run_comm(_gather_ici(shards_of(0, win)), "gather_w_in")
    g_win = _run_comm(_gather_d2d(landed, shards_of(0, win)), "gather_w_in_cores")
    cw_all = _gather8(conv_w.reshape(nl * SSM_CONV, 1024), False, "gather_conv_w")
    cw_full = cw_all.reshape(4, 2, nl, SSM_CONV, 1024)[:, 0].transpose(1, 2, 0, 3).reshape(nl, SSM_CONV, SSM_CONV_DIM)

    offs = {}
    at = 0
    for k, r in PACK_ROWS:
        offs[k] = (at, r)
        at += r


    inv_freq = ROPE_THETA ** (-jnp.arange(0, RET_QK_DIM, 2, dtype=F32) / RET_QK_DIM)
    ang = positions.astype(F32)[0][:, None] * inv_freq
    cos, sin = jnp.cos(ang), jnp.sin(ang)
    cosf = jnp.concatenate([cos, cos], axis=1)
    sinf = jnp.concatenate([-sin, sin], axis=1)
    dm, qd, kd, cd = (jnp.asarray(c) for c in _ret_constants())
    eye, blkm = (jnp.asarray(c) for c in _ssd_constants())
    consts = (dm, qd, kd, cd, eye, blkm)
    e_np = _head_expand()
    eexp = jnp.asarray(e_np, BF16)
    eexp_t = jnp.asarray(e_np.T.copy(), BF16)

    saved = []
    xcur = x0
    for l in range(nl):
        wcat = _to_cat(g_win[0].reshape(IN_DIM, D_MODEL))
        cw, cb = cw_full[l], conv_b[l][None]
        dtb = jnp.pad(dt_bias[l], (0, 128 - SSM_HEADS))[None]
        ax = jnp.repeat(-jnp.exp(a_log[l]), 64)[None]
        dsk = jnp.repeat(d_skip[l], 64)[None]
        bg, sn = b_gate[l][None], ssm_norm[l][None]
        more = l + 1 < nl
        ici_b, ici_c = _gather_ici(shards_of(l, grp_b)), _gather_ici(shards_of(l, grp_c))
        _carry(ici_b)
        proj, u = _nmm(xcur, norm_mix[l][None], wcat, "in_proj", save_u=True, w_rows=True)
        _carry(ici_c)
        qr, kr, xc, dtx = _prescan(proj, cosf, sinf, cw, cb, dtb, eexp, "prescan")
        cores = _gather_d2d(ici_b.results + ici_c.results, shards_of(l, grp_b + grp_c))
        _carry(cores)
        yr, ys, sst, hst = _scan_fwd(qr, kr, proj, xc, dtx, ax, consts, "scan_fwd")
        gl = dict(zip(grp_b + grp_c, cores.results))
        rows_weight = lambda k: gl[k].reshape(-1, D_MODEL)
        wr, ws, wo = rows_weight("w_br_ret"), rows_weight("w_br_ssm"), rows_weight("w_out")
        wq, wxo, w2 = rows_weight("xa_wq"), rows_weight("xa_wo"), rows_weight("mlp_w2")
        w1, wkv = gl["mlp_w1"], gl["xa_wkv"]
        first = _gather_ici(shards_of(l + 1, win), rows=(0, WIN_SPLIT)) if more else None
        _carry(first)
        x1 = _postscan_fwd(xcur, yr, ys, xc, proj, bg, dsk, sn, wr, ws, wo, "postscan")
        kv = _bf(_nmm(mem2, norm_mem[l][None], wkv, "mem_kv"))
        rest = (_gather_ici(shards_of(l + 1, win), rows=(WIN_SPLIT, IN_DIM // 4 - WIN_SPLIT), into=first.results)
                if more else None)
        _carry(rest)
        x2 = _xattn_fwd(x1, norm_xa[l][None], wq, kv, wxo, "xattn")
        cores = _gather_d2d(rest.results, shards_of(l + 1, win)) if more else None
        _carry(cores)
        x3 = _mlp_fwd(x2, norm_mlp[l][None], w1, w2, "mlp")
        if more:
            g_win = cores.results
        saved.append(dict(x0=xcur, x1=x1, x2=x2, proj=proj, u=u, qr=qr, kr=kr, xc=xc, dtx=dtx, yr=yr, ys=ys, sst=sst,
                          hst=hst, kv=kv, wcat=wcat, wr=wr, ws=ws, wo=wo, wq=wq, wxo=wxo, w1=w1, w2=w2, wkv=wkv, cw=cw,
                          cb=cb, dtb=dtb, ax=ax, dsk=dsk, bg=bg, sn=sn))
        xcur = x3

    dx, loss_acc, dnf = _final(xcur, norm_final[None], tgt, "final")
    loss = lax.psum(loss_acc[0, 0], ("x", "y", "c"))

    small_g = [None] * nl
    c = lax.axis_index("c")
    half_id = c.astype(jnp.int32)[None]
    sel_own = blk[None]
    sel_rem = jnp.stack([blk ^ 1, blk ^ 2, blk ^ 3])
    layer_grads = {k: [None] * nl for k in ("pack", "w_in", "xa_wkv")}

    def pair_sums(packs, got):
        own = [_pair_sum(p, g_, sel_own, half_id, F32, "chip_sum_own")[0] for p, g_ in zip(packs, got)]
        out_b = [_pair_sum(p, g_, sel_rem, half_id, BF16, "chip_sum_send") for p, g_ in zip(packs, got)]
        return own, out_b

    def totals(own, inc):
        return [_sum_cast([o, i_[0], i_[1], i_[2]], F32, "grads_total") for o, i_ in zip(own, inc)]

    def finish_layer(lr, red_half, sib_half):
        def whole(i, axis):
            mine_, theirs_ = red_half[i], sib_half[i]
            return jnp.concatenate([jnp.where(c == 0, mine_, theirs_), jnp.where(c == 0, theirs_, mine_)], axis=axis)

        layer_grads["pack"][lr] = whole(0, _split_axis(PACK_N))
        layer_grads["w_in"][lr] = whole(1, _split_axis(IN_DIM // 4))
        layer_grads["xa_wkv"][lr] = whole(2, _split_axis(D_MODEL))

    riding = None
    for l in reversed(range(nl)):
        sv = saved[l]
        swap = _swap_rows(riding[1]) if riding else None
        _carry(swap)
        dx2, hm, rm, dam, dg_mlp = _mlp_bwd(sv["x2"], dx, norm_mlp[l][None], sv["w1"], sv["w2"], "mlp_bwd")
        if riding:
            own, out_b = pair_sums(riding[1], swap.results)
        pack = _mm_tn_into(hm, dam, "dw_mlp1", None, offs["mlp_w1"][0], True)
        pack = _mm_tn_into(rm, dx, "dw_mlp2", pack, offs["mlp_w2"][0], False)
        dx1, hx, dqx, ox, dkv, dg_xa = _xattn_bwd(sv["x1"], dx2, norm_xa[l][None], sv["wq"], sv["kv"], sv["wxo"],
                                                  "xattn_bwd")
        pack = _mm_tn_into(hx, dqx, "dw_xq", pack, offs["xa_wq"][0], False)
        pack = _mm_tn_into(ox, dx2, "dw_xo", pack, offs["xa_wo"][0], False)
        memn, dg_mem = _mem_bwd(mem2, norm_mem[l][None], dkv, sv["wkv"], "mem_bwd")
        dwkv = _mm_tn(memn, dkv, "dw_xkv", col_blocks=4)
        chips_a = _exchange(out_b[0:1]) if riding else None
        _carry(chips_a)
        (dyr, dys, dxs_skip, dproj, yrn, ysn, mg, dbr, dbs, dbg, ddsk, dsn) = _postscan_bwd(
            dx1, sv["yr"], sv["ys"], sv["xc"], sv["proj"], sv["bg"], sv["dsk"], sv["sn"], sv["wr"], sv["ws"], sv["wo"],
            "postscan_bwd")
        pack = _mm_tn_into(mg, dx1, "dw_out", pack, offs["w_out"][0], False)
        pack = _mm_tn_into(yrn, dbr, "dw_br_ret", pack, offs["w_br_ret"][0], False)
        pack = _mm_tn_into(ysn, dbs, "dw_br_ssm", pack, offs["w_br_ssm"][0], False)
        early = [pack, dwkv] if l == 0 else None
        swap_e = _swap_rows(early) if early else None
        chips_b = _exchange(out_b[1:3]) if riding else None
        _carry(_both(chips_b, swap_e))
        dqr, dkr, dproj, dxc, gdtx, da_cols = _scan_bwd(sv["qr"], sv["kr"], sv["proj"], sv["xc"], sv["dtx"], sv["ax"],
                                                        consts, sv["sst"], sv["hst"], dyr, dys, dproj, "scan_bwd")
        if riding:
            red_half = totals(own, chips_a.results + chips_b.results)
        if early:
            own_e, out_e = pair_sums(early, swap_e.results)
        cores = _to_sibling(red_half) if riding else None
        chips_e = _exchange(out_e[0:1]) if early else None
        _carry(_both(cores, chips_e))
        dproj, dcw, dcb, ddtb = _prescan_bwd(sv["proj"], dxc, dxs_skip, gdtx, dqr, dkr, cosf, sinf, sv["cw"], sv["cb"],
                                             sv["dtb"], eexp_t, dproj, "prescan_bwd")
        if riding:
            finish_layer(riding[0], red_half, cores.results)
        chips_kv = _exchange(out_e[1:2]) if early else None
        _carry(chips_kv)
        dwcat = _mm_tn(dproj, sv["u"], "dw_in")
        dx, dg_mix = _in_bwd(dproj, sv["wcat"], sv["x0"], norm_mix[l][None], dx1, "in_bwd")

        da_log = (da_cols.reshape(SSM_HEADS, 64).sum(axis=1)) * (-jnp.exp(a_log[l]))
        dd_skip = ddsk[0].reshape(SSM_HEADS, 64).sum(axis=1)
        small_g[l] = [dg_mix[0:1], dbg[0:1], dcb[0:1], ddtb[0:1, :SSM_HEADS], da_log[None], dd_skip[None], dsn[0:1],
                      dg_xa[0:1], dg_mem[0:1], dg_mlp[0:1], dcw[0::8]]
        riding = (l, [pack, _from_cat(dwcat).reshape(4, IN_DIM // 4, D_MODEL), dwkv])

    w_in_grad = riding[1][1:2]
    own_l, out_l = pair_sums(w_in_grad, _run_comm(_swap_rows(w_in_grad), "grads_core_swap"))
    inc_l = _run_comm(_exchange(out_l), "grads_chip_exchange")
    red_half = totals([own_e[0], own_l[0], own_e[1]], [chips_e.results[0], inc_l[0], chips_kv.results[0]])
    finish_layer(0, red_half, _run_comm(_to_sibling(red_half), "grads_core_join"))
    grad_x = dx[None]

    pieces = []
    for l in range(nl):
        pieces += small_g[l]
    pieces.append(dnf[0:1])
    small_sum = _gather8(_pack_rows(pieces), True, "reduce_small")
    layout = []
    for l in range(nl):
        layout += [(1, w) for _, w in SMALL] + [(SSM_CONV, SSM_CONV_DIM)]
    layout.append((1, 1024))
    red = _unpack_rows(small_sum, layout)
    per = len(SMALL) + 1
    g_small = {k: jnp.concatenate([red[l * per + i] for l in range(nl)], axis=0) for i, (k, _) in enumerate(SMALL)}
    g_convw_full = jnp.stack([red[l * per + len(SMALL)] for l in range(nl)])
    g_small["conv_w"] = lax.dynamic_slice_in_dim(g_convw_full, blk * 1024, 1024, axis=2)
    g_small["norm_final"] = red[-1][0]

    grads = dict(g_small)
    pack_all = jnp.stack(layer_grads["pack"])
    for k, r in PACK_ROWS:
        grads[k] = pack_all[:, offs[k][0]:offs[k][0] + r]
    grads["w_in"] = jnp.stack(layer_grads["w_in"])
    grads["xa_wkv"] = jnp.stack(layer_grads["xa_wkv"])

    delta, new_m, new_v = {}, {}, {}
    for k in ["xa_wkv"] + [k for k, _ in PACK_ROWS]:
        delta[k], new_m[k], new_v[k] = _adamw(W[k], grads[k], M[k], V[k], "adamw_" + k)
    g_in_t = grads["w_in"]
    grads["w_in"] = tr_(g_in_t)
    d_t, m_t, v_t = _adamw(tr_(w_in), g_in_t, tr_(m_w_in), tr_(v_w_in), "adamw_w_in")
    delta["w_in"], new_m["w_in"], new_v["w_in"] = tr_(d_t), tr_(m_t), tr_(v_t)
    small_names = [k for k, _ in SMALL] + ["conv_w", "norm_final"]

    def pack_small(src):
        ps = []
        for k in small_names:
            a = src[k]
            ps.append(a.reshape(-1, a.shape[-1]) if a.ndim > 1 else a[None])
        return _pack_rows(ps)

    ds_, ms_, vs_ = _adamw(pack_small(W), pack_small(grads), pack_small(M), pack_small(V), "adamw_small")
    lay2 = []
    for k in small_names:
        a = W[k]
        lay2.append((int(np.prod(a.shape[:-1])) if a.ndim > 1 else 1, a.shape[-1]))
    for src, dst in ((ds_, delta), (ms_, new_m), (vs_, new_v)):
        for k, piece in zip(small_names, _unpack_rows(src, lay2)):
            dst[k] = piece.reshape(W[k].shape)

    names = ["norm_mix", "w_in", "b_gate", "conv_w", "conv_b", "dt_bias", "a_log", "d_skip", "ssm_norm", "w_br_ret",
             "w_br_ssm", "w_out", "norm_xa", "norm_mem", "xa_wq", "xa_wkv", "xa_wo", "norm_mlp", "mlp_w1", "mlp_w2",
             "norm_final"]
    return (loss, grad_x, *[grads[n] for n in names], *[delta[n] for n in names], *[new_m[n] for n in names],
            *[new_v[n] for n in names])
```

```python
import numpy as np
import jax
import jax.numpy as jnp
from jax import lax
from jax.experimental import pallas as pl
from jax.experimental.pallas import tpu as pltpu

F32 = jnp.float32
BF16 = jnp.bfloat16
MESH = pl.DeviceIdType.MESH

D_MODEL = 1024
CHUNK = 64
EPS = 1e-6
RET_HEADS = 4
RET_QK_DIM = 128
RET_V_DIM = 256
RET_QK = 512
RET_V = 1024
ROPE_THETA = 10000.0
SSM_INNER = 2048
SSM_HEADS = 32
SSM_GROUPS = 8
SSM_STATE = 128
SSM_CONV = 4
SSM_BC = 1024
SSM_CONV_DIM = 4096
XA_HEADS = 4
XA_HEAD_DIM = 256
D_FF = 4096
GROUP_W = 256

DT_PAD = 1024
IN_DIM = 11296
WIN_SPLIT = 1824
NP = 12288
C_XBC, C_Q, C_K, C_DT, C_Z, C_GATES, C_V, C_G = 0, 4096, 4608, 5120, 6144, 8192, 10240, 11264
O_Q, O_K, O_V, O_G, O_Z, O_XBC, O_DT, O_GATES = 0, 512, 1024, 2048, 3072, 5120, 9216, 9248

ADAM_LR = 0.001
ADAM_B1 = 0.9
ADAM_B2 = 0.999
ADAM_EPS = 1e-08
ADAM_WD = 0.01
ADAM_STEP = 10

VMEM_LIMIT = 56 * 1024 * 1024


def _params(*sem):
    return pltpu.CompilerParams(dimension_semantics=sem, vmem_limit_bytes=VMEM_LIMIT)


_CARRY = []


def _carry(comm):
    if comm is not None:
        _CARRY.append(comm)


def _pcall(body, **kw):
    if _CARRY:
        return _hosted(body, _CARRY.pop(), kw)
    return pl.pallas_call(body, **kw)


class _Comm:
    def __init__(self, ins, out_shapes, sems, start, finish, aliases=None):
        self.ins, self.out_shapes, self.sems = list(ins), list(out_shapes), list(sems)
        self.start, self.finish, self.aliases = start, finish, dict(aliases or {})
        self.results, self.parts = None, None

    def deliver(self, results):
        self.results = results
        if self.parts:
            a, b, k = self.parts
            a.deliver(results[:k])
            b.deliver(results[k:])


def _hosted(body, comm, kw):
    in_specs = list(kw.pop("in_specs"))
    out_specs, out_shape = kw.pop("out_specs"), kw.pop("out_shape")
    single = not isinstance(out_shape, (list, tuple))
    if single:
        out_specs, out_shape = [out_specs], [out_shape]
    out_specs, out_shape = list(out_specs), list(out_shape)
    scratch = list(kw.pop("scratch_shapes", []))
    grid = tuple(kw.get("grid", ()))
    aliases = dict(kw.pop("input_output_aliases", {}))
    n_in, n_out, n_sc = len(in_specs), len(out_shape), len(scratch)
    c_in, c_out = len(comm.ins), len(comm.out_shapes)
    for i, o in comm.aliases.items():
        aliases[n_in + i] = n_out + o
    kw["compiler_params"] = _params(*(["arbitrary"] * len(grid)))

    def wrapped(*refs):
        at = 0
        parts = []
        for cnt in (n_in, c_in, n_out, c_out, n_sc):
            parts.append(refs[at:at + cnt])
            at += cnt
        a, ci, b, co, s = parts
        cs = refs[at:]
        first, last = None, None
        for d, size in enumerate(grid):
            f, l_ = pl.program_id(d) == 0, pl.program_id(d) == size - 1
            first = f if first is None else first & f
            last = l_ if last is None else last & l_

        @pl.when(first)
        def _():
            comm.start(ci, co, cs)

        body(*a, *b, *s)

        @pl.when(last)
        def _():
            comm.finish(ci, co, cs)

    call = _pcall(wrapped, in_specs=in_specs + [ANY] * c_in, out_specs=out_specs + [ANY] * c_out,
                  out_shape=out_shape + comm.out_shapes, scratch_shapes=scratch + comm.sems,
                  input_output_aliases=aliases, **kw)

    def run(*ops):
        res = call(*ops, *comm.ins)
        comm.deliver(list(res[n_out:]))
        return res[0] if single else list(res[:n_out])

    return run


def _both(a, b):
    if a is None or b is None:
        return a if b is None else b
    ni, no, ns = len(a.ins), len(a.out_shapes), len(a.sems)

    def start(ins, outs, sems):
        a.start(ins[:ni], outs[:no], sems[:ns])
        b.start(ins[ni:], outs[no:], sems[ns:])

    def finish(ins, outs, sems):
        a.finish(ins[:ni], outs[:no], sems[:ns])
        b.finish(ins[ni:], outs[no:], sems[ns:])

    both = _Comm(a.ins + b.ins, a.out_shapes + b.out_shapes, a.sems + b.sems, start, finish,
                 {**a.aliases, **{ni + i: no + o for i, o in b.aliases.items()}})
    both.parts = (a, b, no)
    return both


def _run_comm(comm, name):
    def body(*refs):
        c_in, c_out = len(comm.ins), len(comm.out_shapes)
        ci, co, cs = refs[:c_in], refs[c_in:c_in + c_out], refs[c_in + c_out:]
        comm.start(ci, co, cs)
        comm.finish(ci, co, cs)

    aliases = {i: o for i, o in comm.aliases.items()}
    res = _pcall(body, in_specs=[ANY] * len(comm.ins), out_specs=[ANY] * len(comm.out_shapes),
                 out_shape=comm.out_shapes, scratch_shapes=comm.sems, input_output_aliases=aliases, name=name)(*comm.ins)
    comm.deliver(list(res))
    return comm.results


def _bf(a):
    return a.astype(BF16)


def _dot(a, b):
    return jnp.dot(_bf(a), _bf(b), preferred_element_type=F32)


def _dot_nt(a, b):
    return lax.dot_general(_bf(a), _bf(b), (((1,), (1,)), ((), ())), preferred_element_type=F32)


def _dot_tn(a, b):
    return lax.dot_general(_bf(a), _bf(b), (((0,), (0,)), ((), ())), preferred_element_type=F32)


def _colsum(a):
    return jnp.sum(a, axis=0, keepdims=True)


def _rstd(x):
    return lax.rsqrt(jnp.mean(x * x, axis=-1, keepdims=True) + EPS)


def _rms_bwd(dy, x, rstd):
    xh = x * rstd
    return rstd * (dy - xh * jnp.mean(dy * xh, axis=-1, keepdims=True))


def _sigmoid(x):
    return 1.0 / (1.0 + jnp.exp(-x))


def _silu_and_grad(x):
    s = _sigmoid(x)
    return x * s, s + x * s * (1.0 - s)


def _softplus(x):
    u = jnp.exp(-jnp.abs(x))
    l1p = jnp.where(u < 1e-4, u * (1.0 - 0.5 * u), jnp.log(1.0 + u))
    return jnp.maximum(x, 0.0) + l1p


def _split3_dot(a, e):
    hi = a.astype(BF16)
    r1 = a - hi.astype(F32)
    mid = r1.astype(BF16)
    lo = (r1 - mid.astype(F32)).astype(BF16)
    return (jnp.dot(hi, e, preferred_element_type=F32) + jnp.dot(mid, e, preferred_element_type=F32)
            + jnp.dot(lo, e, preferred_element_type=F32))


def _cumsum_rows(a):
    rows = lax.broadcasted_iota(jnp.int32, a.shape, 0)
    s = 1
    while s < a.shape[0]:
        a = a + jnp.where(rows >= s, pltpu.roll(a, s, 0), 0.0)
        s *= 2
    return a


def _revcumsum_rows(a):
    n = a.shape[0]
    rows = lax.broadcasted_iota(jnp.int32, a.shape, 0)
    s = 1
    while s < n:
        a = a + jnp.where(rows < n - s, pltpu.roll(a, n - s, 0), 0.0)
        s *= 2
    return a


def _rms_groups(y, width):
    out = []
    for h in range(y.shape[1] // width):
        slab = y[:, h * width:(h + 1) * width]
        out.append((slab, _rstd(slab)))
    return out


def _ret_constants():
    idx = np.arange(CHUNK, dtype=np.float32)
    lg = np.log1p(-(np.float32(2.0) ** (np.float32(-5.0) - np.arange(RET_HEADS, dtype=np.float32)))).astype(np.float32)
    rel = np.abs(idx[:, None] - idx[None, :])
    dm = np.exp(lg[:, None, None] * rel).astype(np.float32)
    qd = np.exp(lg[None, :] * (idx[:, None] + 1.0)).astype(np.float32)
    kd = np.exp(lg[None, :] * (CHUNK - 1.0 - idx[:, None])).astype(np.float32)
    cd = np.exp(lg * CHUNK).astype(np.float32)
    qd = np.repeat(qd, RET_QK_DIM, axis=1)
    kd = np.repeat(kd, RET_QK_DIM, axis=1)
    cd = np.repeat(cd, RET_QK_DIM)[:, None] * np.ones((1, RET_V_DIM), np.float32)
    return dm, qd, kd, cd.astype(np.float32)


def _ssd_constants():
    eye = np.tile(np.eye(CHUNK, dtype=np.float32), (1, GROUP_W // CHUNK))
    blk = np.kron(np.eye(GROUP_W // CHUNK, dtype=np.float32), np.ones((CHUNK, CHUNK), np.float32))
    return eye, blk


def _head_expand():
    e = np.zeros((128, SSM_INNER), np.float32)
    for h in range(SSM_HEADS):
        e[h, h * 64:(h + 1) * 64] = 1.0
    return e


def _ret_chunk_fwd(qh, kh, vh, sh, dmh, qdh, kdh, cdh):
    a = _dot_nt(qh, kh) * dmh
    y = _dot(a, vh) + _dot(qh * qdh, sh)
    s_new = sh * cdh + _dot_tn(kh * kdh, vh)
    return y, s_new


def _ret_chunk_bwd(qh, kh, vh, sh, dmh, qdh, kdh, cdh, dy, ds_new):
    a = _dot_nt(qh, kh) * dmh
    dp = _dot_nt(dy, vh) * dmh
    dq = _dot(dp, kh) + _dot_nt(dy, sh) * qdh
    dk = _dot_tn(dp, qh) + _dot_nt(vh, ds_new) * kdh
    dv = _dot_tn(a, dy) + _dot(kh * kdh, ds_new)
    ds = cdh * ds_new + _dot_tn(qh * qdh, dy)
    return dq, dk, dv, ds


def _ssd_common(xs, dtx, ax, eye):
    cum = _cumsum_rows(dtx * ax)
    last = cum[CHUNK - 1:CHUNK, :]
    r = _colsum(jnp.where(eye > 0.5, cum, 0.0))
    return cum, last, r, xs * dtx


def _tile4(a):
    return jnp.concatenate([a, a, a, a], axis=0)


def _ssd_chunk_fwd(xs, dtx, b, c, ax, hg, eye, blk):
    cum, last, r, x = _ssd_common(xs, dtx, ax, eye)
    lam = jnp.exp(-jnp.abs(cum - r))
    wc = _dot_nt(c, _tile4(b)) * lam
    bd = _tile4(x) * blk
    y = _dot(wc, bd) + _dot(c, hg) * jnp.exp(cum)
    h_new = hg * jnp.exp(last) + _dot_tn(b, x * jnp.exp(last - cum))
    return y, h_new


def _ssd_chunk_bwd(xs, dtx, b, c, ax, hg, eye, blk, dy, dh_new):
    cum, last, r, x = _ssd_common(xs, dtx, ax, eye)
    delta = cum - r
    lam = jnp.exp(-jnp.abs(delta))
    b4 = _tile4(b)
    cb4 = _dot_nt(c, b4)
    wc = cb4 * lam
    bd = _tile4(x) * blk
    ecx = jnp.exp(cum)
    wl = jnp.exp(last - cum)
    ecl = jnp.exp(last)
    z = _dot(c, hg)
    dwc = _dot_nt(dy, bd)
    dbd = _dot_tn(wc, dy) * blk
    dx = dbd[0:64] + dbd[64:128] + dbd[128:192] + dbd[192:256]
    dt_ = _dot(b, dh_new)
    dx = dx + dt_ * wl
    dcb4 = dwc * lam
    dz = dy * ecx
    dc = _dot(dcb4, b4) + _dot_nt(dz, hg)
    db4 = _dot_tn(dcb4, c)
    db = db4[0:64] + db4[64:128] + db4[128:192] + db4[192:256] + _dot_nt(x * wl, dh_new)
    g = dwc * cb4 * lam * (-jnp.sign(delta))
    dr = -_colsum(g)
    dwl = dt_ * x * wl
    u = g + eye * dr + dy * z * ecx - dwl
    lastrow = _colsum(dwl) + _colsum(dh_new * hg) * ecl
    rows = lax.broadcasted_iota(jnp.int32, u.shape, 0)
    u = u + jnp.where(rows == CHUNK - 1, lastrow, 0.0)
    dh = _dot_tn(c, dz) + dh_new * ecl
    rc = _revcumsum_rows(u)
    dxs = dx * dtx
    g_dtx = dx * xs + rc * ax
    da = _colsum(rc * dtx)
    return dxs, g_dtx, db, dc, da, dh


def _row_tile(s, want):
    t = min(s, want)
    assert s % t == 0
    return t


def _nmm(x, gain, w, name, tn=1024, save_u=False, w_rows=False):
    s, d = x.shape
    blocked = w.ndim == 3
    if blocked:
        tn = w.shape[2]
        n = w.shape[0] * tn
        w_spec = pl.BlockSpec((1, d, tn), lambda i, j: (j, 0, 0))
    elif w_rows:
        n = w.shape[0]
        w_spec = pl.BlockSpec((tn, d), lambda i, j: (j, 0))
    else:
        n = w.shape[1]
        w_spec = pl.BlockSpec((d, tn), lambda i, j: (0, j))
    tm = _row_tile(s, 1024)
    assert n % tn == 0

    def body(x_ref, g_ref, w_ref, *rest):
        o_ref, u_sc = rest[0], rest[-1]

        @pl.when(pl.program_id(1) == 0)
        def _():
            xx = x_ref[...]
            u = _bf((xx * _rstd(xx)) * g_ref[...])
            u_sc[...] = u
            if save_u:
                rest[1][...] = u

        if w_rows:
            o_ref[...] = _dot_nt(u_sc[...], w_ref[...])
        else:
            o_ref[...] = jnp.dot(u_sc[...], w_ref[0] if blocked else w_ref[...], preferred_element_type=F32)

    out_shape = [jax.ShapeDtypeStruct((s, n), F32)]
    out_specs = [pl.BlockSpec((tm, tn), lambda i, j: (i, j))]
    if save_u:
        out_shape.append(jax.ShapeDtypeStruct((s, d), BF16))
        out_specs.append(pl.BlockSpec((tm, d), lambda i, j: (i, 0)))
    res = _pcall(
        body, grid=(s // tm, n // tn),
        in_specs=[pl.BlockSpec((tm, d), lambda i, j: (i, 0)), pl.BlockSpec((1, d), lambda i, j: (0, 0)), w_spec],
        out_specs=out_specs, out_shape=out_shape, scratch_shapes=[pltpu.VMEM((tm, d), BF16)],
        compiler_params=_params("parallel", "arbitrary"), name=name)(x, gain, w)
    return res if save_u else res[0]


def _mm_tn(a, b, name, tm=1024, tn=1024, col_blocks=None):
    k, m = a.shape
    n = b.shape[1]
    tk = _row_tile(k, 1024)
    tm, tn = min(tm, m), min(tn, n)
    if col_blocks:
        tn = n // col_blocks
    assert m % tm == 0 and n % tn == 0
    nk = k // tk

    def body(a_ref, b_ref, o_ref, acc):
        kk = pl.program_id(2)

        @pl.when(kk == 0)
        def _():
            acc[...] = jnp.zeros_like(acc)

        acc[...] += _dot_tn(a_ref[...], b_ref[...])

        @pl.when(kk == nk - 1)
        def _():
            if col_blocks:
                o_ref[0] = acc[...]
            else:
                o_ref[...] = acc[...]

    if col_blocks:
        out_spec = pl.BlockSpec((1, tm, tn), lambda i, j, kk: (j, i, 0))
        out_shape = jax.ShapeDtypeStruct((col_blocks, m, tn), F32)
    else:
        out_spec = pl.BlockSpec((tm, tn), lambda i, j, kk: (i, j))
        out_shape = jax.ShapeDtypeStruct((m, n), F32)
    return _pcall(
        body, grid=(m // tm, n // tn, nk),
        in_specs=[pl.BlockSpec((tk, tm), lambda i, j, kk: (kk, i)), pl.BlockSpec((tk, tn), lambda i, j, kk: (kk, j))],
        out_specs=out_spec, out_shape=out_shape,
        scratch_shapes=[pltpu.VMEM((tm, tn), F32)],
        compiler_params=_params("parallel", "parallel", "arbitrary"), name=name)(a, b)


def _mm_tn_into(a, b, name, pack, off, by_cols):
    k, m = a.shape
    n = b.shape[1]
    tk = _row_tile(k, 1024)
    nk = k // tk
    rows = m if by_cols else m // 4
    tm = min(m, 1024)
    nb = 1 if by_cols else tm // rows
    assert tm == nb * rows and off % rows == 0 and n == (4096 if by_cols else 1024)

    def body(a_ref, b_ref, *rest):
        o_ref, acc = rest[-2], rest[-1]
        kk = pl.program_id(2)

        @pl.when(kk == 0)
        def _():
            acc[...] = jnp.zeros_like(acc)

        acc[...] += _dot_tn(a_ref[...], b_ref[...])

        @pl.when(kk == nk - 1)
        def _():
            o_ref[...] = acc[...].reshape(nb, rows, 1024)

    if by_cols:
        out_spec = pl.BlockSpec((1, rows, 1024), lambda i, j, kk: (j, off // rows, 0))
    else:
        out_spec = pl.BlockSpec((nb, rows, 1024), lambda i, j, kk: (i, off // rows, 0))
    in_specs = [pl.BlockSpec((tk, tm), lambda i, j, kk: (kk, i)), pl.BlockSpec((tk, 1024), lambda i, j, kk: (kk, j))]
    ops, alias = [a, b], {}
    if pack is not None:
        in_specs.append(ANY)
        ops.append(pack)
        alias = {2: 0}
    return _pcall(
        body, grid=(m // tm, n // 1024, nk), in_specs=in_specs, out_specs=out_spec,
        out_shape=jax.ShapeDtypeStruct((4, PACK_N, 1024), F32), scratch_shapes=[pltpu.VMEM((tm, 1024), F32)],
        input_output_aliases=alias, compiler_params=_params("parallel", "parallel", "arbitrary"), name=name)(*ops)


def _in_bwd(dproj, wcat_t, x, gain, dres, name):
    s, n = dproj.shape
    d = wcat_t.shape[1]
    tm = _row_tile(s, 1024)
    tk = 1024
    nk = n // tk
    ns = s // tm

    def body(dp_ref, w_ref, x_ref, g_ref, dr_ref, dx_ref, dg_ref, acc):
        i, kk = pl.program_id(0), pl.program_id(1)

        @pl.when(kk == 0)
        def _():
            acc[...] = jnp.zeros_like(acc)

        @pl.when((kk == 0) & (i == 0))
        def _():
            dg_ref[...] = jnp.zeros_like(dg_ref)

        acc[...] += _dot(dp_ref[...], w_ref[...])

        @pl.when(kk == nk - 1)
        def _():
            xx = x_ref[...]
            r = _rstd(xx)
            du = acc[...]
            dg_ref[...] += _colsum(du * (xx * r))
            dx_ref[...] = dr_ref[...] + _rms_bwd(du * g_ref[...], xx, r)

    return _pcall(
        body, grid=(ns, nk),
        in_specs=[pl.BlockSpec((tm, tk), lambda i, kk: (i, kk)), pl.BlockSpec((tk, d), lambda i, kk: (kk, 0)),
                  pl.BlockSpec((tm, d), lambda i, kk: (i, 0)), pl.BlockSpec((1, d), lambda i, kk: (0, 0)),
                  pl.BlockSpec((tm, d), lambda i, kk: (i, 0))],
        out_specs=[pl.BlockSpec((tm, d), lambda i, kk: (i, 0)), pl.BlockSpec((8, d), lambda i, kk: (0, 0))],
        out_shape=[jax.ShapeDtypeStruct((s, d), F32), jax.ShapeDtypeStruct((8, d), F32)],
        scratch_shapes=[pltpu.VMEM((tm, d), F32)],
        compiler_params=_params("arbitrary", "arbitrary"), name=name)(dproj, wcat_t, x, gain, dres)


def _prev_rows_spec(ts, width):
    return pl.BlockSpec((8, width), lambda i: (jnp.maximum(i * (ts // 8) - 1, 0), 0))


def _prescan(proj, cosf, sinf, cw, cb, dtb, eexp, name):
    s = proj.shape[0]
    ts = _row_tile(s, 256)

    def body(xbc_ref, prev_ref, q_ref, k_ref, dt_ref, cos_ref, sin_ref, cw_ref, cb_ref, dtb_ref, e_ref,
             qo_ref, ko_ref, xc_ref, dtx_ref):
        i = pl.program_id(0)
        for st in range(SSM_CONV_DIM // 128):
            sl = slice(st * 128, (st + 1) * 128)
            prev = jnp.where(i > 0, prev_ref[:, sl], 0.0)
            xcat = jnp.concatenate([prev, xbc_ref[:, sl]], axis=0)
            pre = cb_ref[:, sl] + cw_ref[3:4, sl] * xcat[8:8 + ts]
            for j in range(3):
                pre = pre + cw_ref[j:j + 1, sl] * pltpu.roll(xcat, 3 - j, 0)[8:8 + ts]
            xc_ref[:, sl] = pre * _sigmoid(pre)
        cs, sn = cos_ref[...], sin_ref[...]
        for h in range(RET_HEADS):
            sl = slice(h * 128, (h + 1) * 128)
            qh, kh = q_ref[:, sl], k_ref[:, sl]
            qo_ref[:, sl] = qh * cs + pltpu.roll(qh, 64, 1) * sn
            ko_ref[:, sl] = (kh * cs + pltpu.roll(kh, 64, 1) * sn) * (RET_QK_DIM ** -0.5)
        dtv = _softplus(dt_ref[:, 0:128] + dtb_ref[...])
        dtx_ref[...] = _split3_dot(dtv, e_ref[...])

    row = lambda w, c: pl.BlockSpec((ts, w), lambda i: (i, c))
    full = lambda a: pl.BlockSpec(a.shape, lambda i: (0,) * a.ndim)
    return _pcall(
        body, grid=(s // ts,),
        in_specs=[row(4096, 0), _prev_rows_spec(ts, 4096), row(512, C_Q // 512), row(512, C_K // 512),
                  row(DT_PAD, C_DT // DT_PAD), row(128, 0), row(128, 0), full(cw), full(cb), full(dtb), full(eexp)],
        out_specs=[row(512, 0), row(512, 0), row(4096, 0), row(2048, 0)],
        out_shape=[jax.ShapeDtypeStruct((s, 512), F32), jax.ShapeDtypeStruct((s, 512), F32),
                   jax.ShapeDtypeStruct((s, 4096), F32), jax.ShapeDtypeStruct((s, 2048), F32)],
        compiler_params=_params("parallel"), name=name)(proj, proj, proj, proj, proj, cosf, sinf, cw, cb, dtb, eexp)


def _scan_fwd(qr, kr, proj, xc, dtx, ax, consts, name):
    s = qr.shape[0]
    nc = s // CHUNK
    dm, qd, kd, cd, eye, blk = consts

    def body(q_ref, k_ref, v_ref, xc_ref, dtx_ref, ax_ref, dm_ref, qd_ref, kd_ref, cd_ref, eye_ref, blk_ref,
             yr_ref, ys_ref, sst_ref, hst_ref, s_sc, h_sc):
        @pl.when(pl.program_id(0) == 0)
        def _():
            s_sc[...] = jnp.zeros_like(s_sc)
            h_sc[...] = jnp.zeros_like(h_sc)

        sst_ref[0] = s_sc[...]
        hst_ref[0] = h_sc[...]
        for h in range(RET_HEADS):
            ql, vl = slice(h * 128, (h + 1) * 128), slice(h * 256, (h + 1) * 256)
            y, s_new = _ret_chunk_fwd(q_ref[:, ql], k_ref[:, ql], v_ref[:, vl], s_sc[ql, :], dm_ref[h],
                                      qd_ref[:, ql], kd_ref[:, ql], cd_ref[ql, :])
            yr_ref[:, vl] = y
            s_sc[ql, :] = s_new
        eye_v, blk_v = eye_ref[...], blk_ref[...]
        for g in range(SSM_GROUPS):
            sl = slice(g * GROUP_W, (g + 1) * GROUP_W)
            bl = slice(SSM_INNER + g * 128, SSM_INNER + (g + 1) * 128)
            cl = slice(SSM_INNER + SSM_BC + g * 128, SSM_INNER + SSM_BC + (g + 1) * 128)
            y, h_new = _ssd_chunk_fwd(xc_ref[:, sl], dtx_ref[:, sl], xc_ref[:, bl], xc_ref[:, cl], ax_ref[:, sl],
                                      h_sc[:, sl], eye_v, blk_v)
            ys_ref[:, sl] = y
            h_sc[:, sl] = h_new

    row = lambda w, c=0: pl.BlockSpec((CHUNK, w), lambda i: (i, c))
    full = lambda a: pl.BlockSpec(a.shape, lambda i: (0,) * a.ndim)
    return _pcall(
        body, grid=(nc,),
        in_specs=[row(512), row(512), row(1024, C_V // 1024), row(4096), row(2048), full(ax), full(dm), full(qd),
                  full(kd), full(cd), full(eye), full(blk)],
        out_specs=[row(1024), row(2048), pl.BlockSpec((1, 512, 256), lambda i: (i, 0, 0)),
                   pl.BlockSpec((1, 128, 2048), lambda i: (i, 0, 0))],
        out_shape=[jax.ShapeDtypeStruct((s, 1024), F32), jax.ShapeDtypeStruct((s, 2048), F32),
                   jax.ShapeDtypeStruct((nc, 512, 256), F32), jax.ShapeDtypeStruct((nc, 128, 2048), F32)],
        scratch_shapes=[pltpu.VMEM((512, 256), F32), pltpu.VMEM((128, 2048), F32)],
        compiler_params=_params("arbitrary"), name=name)(qr, kr, proj, xc, dtx, ax, dm, qd, kd, cd, eye, blk)


def _scan_bwd(qr, kr, proj, xc, dtx, ax, consts, sst, hst, dyr, dys, dproj, name):
    s = qr.shape[0]
    nc = s // CHUNK
    dm, qd, kd, cd, eye, blk = consts

    def body(q_ref, k_ref, v_ref, xc_ref, dtx_ref, ax_ref, dm_ref, qd_ref, kd_ref, cd_ref, eye_ref, blk_ref,
             sst_ref, hst_ref, dyr_ref, dys_ref, dproj_in, dq_ref, dk_ref, dv_ref, dxc_ref, gdt_ref, da_ref, ds_sc,
             dh_sc):
        @pl.when(pl.program_id(0) == 0)
        def _():
            ds_sc[...] = jnp.zeros_like(ds_sc)
            dh_sc[...] = jnp.zeros_like(dh_sc)
            da_ref[...] = jnp.zeros_like(da_ref)

        for h in range(RET_HEADS):
            ql, vl = slice(h * 128, (h + 1) * 128), slice(h * 256, (h + 1) * 256)
            dq, dk, dv, ds = _ret_chunk_bwd(q_ref[:, ql], k_ref[:, ql], v_ref[:, vl], sst_ref[0, ql, :], dm_ref[h],
                                            qd_ref[:, ql], kd_ref[:, ql], cd_ref[ql, :], dyr_ref[:, vl], ds_sc[ql, :])
            dq_ref[:, ql] = dq
            dk_ref[:, ql] = dk
            dv_ref[:, vl] = _bf(dv)
            ds_sc[ql, :] = ds
        eye_v, blk_v = eye_ref[...], blk_ref[...]
        for g in range(SSM_GROUPS):
            sl = slice(g * GROUP_W, (g + 1) * GROUP_W)
            bl = slice(SSM_INNER + g * 128, SSM_INNER + (g + 1) * 128)
            cl = slice(SSM_INNER + SSM_BC + g * 128, SSM_INNER + SSM_BC + (g + 1) * 128)
            dxs, g_dtx, db, dc, da, dh = _ssd_chunk_bwd(
                xc_ref[:, sl], dtx_ref[:, sl], xc_ref[:, bl], xc_ref[:, cl], ax_ref[:, sl], hst_ref[0, :, sl],
                eye_v, blk_v, dys_ref[:, sl], dh_sc[:, sl])
            dxc_ref[:, sl] = dxs
            dxc_ref[:, bl] = db
            dxc_ref[:, cl] = dc
            gdt_ref[:, sl] = g_dtx
            da_ref[:, sl] += da
            dh_sc[:, sl] = dh

    row = lambda w, c=0: pl.BlockSpec((CHUNK, w), lambda i: (nc - 1 - i, c))
    full = lambda a: pl.BlockSpec(a.shape, lambda i: (0,) * a.ndim)
    return _pcall(
        body, grid=(nc,),
        in_specs=[row(512), row(512), row(1024, C_V // 1024), row(4096), row(2048), full(ax), full(dm), full(qd),
                  full(kd), full(cd), full(eye), full(blk),
                  pl.BlockSpec((1, 512, 256), lambda i: (nc - 1 - i, 0, 0)),
                  pl.BlockSpec((1, 128, 2048), lambda i: (nc - 1 - i, 0, 0)), row(1024), row(2048), ANY],
        out_specs=[row(512), row(512), row(1024, C_V // 1024), row(4096), row(2048),
                   pl.BlockSpec((1, 2048), lambda i: (0, 0))],
        out_shape=[jax.ShapeDtypeStruct((s, 512), F32), jax.ShapeDtypeStruct((s, 512), F32),
                   jax.ShapeDtypeStruct(dproj.shape, BF16), jax.ShapeDtypeStruct((s, 4096), F32),
                   jax.ShapeDtypeStruct((s, 2048), F32), jax.ShapeDtypeStruct((1, 2048), F32)],
        scratch_shapes=[pltpu.VMEM((512, 256), F32), pltpu.VMEM((128, 2048), F32)],
        input_output_aliases={16: 2},
        compiler_params=_params("arbitrary"), name=name)(qr, kr, proj, xc, dtx, ax, dm, qd, kd, cd, eye, blk, sst, hst,
                                                          dyr, dys, dproj)


def _mix_values(yr, g, ys, xs, z, gates, bg, dsk, sn):
    sg, dsg = _silu_and_grad(g)
    ret = _rms_groups(yr, RET_V_DIM)
    yrn = jnp.concatenate([slab * r for slab, r in ret], axis=1) * sg
    sz, dsz = _silu_and_grad(z)
    ys0 = ys + xs * dsk
    ys1 = ys0 * sz
    grp = _rms_groups(ys1, GROUP_W)
    ysh = jnp.concatenate([slab * r for slab, r in grp], axis=1)
    ysn = ysh * sn
    gg = _sigmoid(gates + bg)
    return dict(sg=sg, dsg=dsg, ret=ret, yrn=yrn, sz=sz, dsz=dsz, ys0=ys0, ys1=ys1, grp=grp, ysh=ysh, ysn=ysn,
                gr=gg[:, :D_MODEL], gs=gg[:, D_MODEL:])


def _postscan_fwd(x, yr, ys, xc, proj, bg, dsk, sn, wr, ws, wo, name):
    s = x.shape[0]
    ts = _row_tile(s, 256)

    def body(x_ref, yr_ref, ys_ref, xs_ref, g_ref, z_ref, gt_ref, bg_ref, dsk_ref, sn_ref, wr_ref, ws_ref, wo_ref,
             o_ref):
        m = _mix_values(yr_ref[...], g_ref[...], ys_ref[...], xs_ref[...], z_ref[...], gt_ref[...], bg_ref[...],
                        dsk_ref[...], sn_ref[...])
        merged = m["gr"] * _dot(m["yrn"], wr_ref[...]) + m["gs"] * _dot(m["ysn"], ws_ref[...])
        o_ref[...] = x_ref[...] + _dot(merged, wo_ref[...])

    row = lambda w, c=0: pl.BlockSpec((ts, w), lambda i: (i, c))
    full = lambda a: pl.BlockSpec(a.shape, lambda i: (0,) * a.ndim)
    return _pcall(
        body, grid=(s // ts,),
        in_specs=[row(1024), row(1024), row(2048), row(2048), row(1024, C_G // 1024), row(2048, C_Z // 2048),
                  row(2048, C_GATES // 2048), full(bg), full(dsk), full(sn), full(wr), full(ws), full(wo)],
        out_specs=row(1024), out_shape=jax.ShapeDtypeStruct((s, D_MODEL), F32),
        compiler_params=_params("parallel"), name=name)(x, yr, ys, xc, proj, proj, proj, bg, dsk, sn, wr, ws, wo)


def _postscan_bwd(dout, yr, ys, xc, proj, bg, dsk, sn, wr, ws, wo, name):
    s = dout.shape[0]
    ts = _row_tile(s, 128)

    def body(do_ref, yr_ref, ys_ref, xs_ref, g_ref, z_ref, gt_ref, bg_ref, dsk_ref, sn_ref, wr_ref, ws_ref, wo_ref,
             dyr_ref, dys_ref, dxs_ref, dproj_ref, yrn_ref, ysn_ref, mg_ref, dbr_ref, dbs_ref,
             dbg_ref, ddsk_ref, dsn_ref):
        @pl.when(pl.program_id(0) == 0)
        def _():
            dbg_ref[...] = jnp.zeros_like(dbg_ref)
            ddsk_ref[...] = jnp.zeros_like(ddsk_ref)
            dsn_ref[...] = jnp.zeros_like(dsn_ref)

        xs = xs_ref[...]
        m = _mix_values(yr_ref[...], g_ref[...], ys_ref[...], xs, z_ref[...], gt_ref[...], bg_ref[...],
                        dsk_ref[...], sn_ref[...])
        gr, gs = m["gr"], m["gs"]
        br, bs = _dot(m["yrn"], wr_ref[...]), _dot(m["ysn"], ws_ref[...])
        dmerged = _dot_nt(do_ref[...], wo_ref[...])
        dgt = jnp.concatenate([dmerged * br * gr * (1.0 - gr), dmerged * bs * gs * (1.0 - gs)], axis=1)
        dproj_ref[:, C_GATES:C_GATES + 2048] = _bf(dgt)
        dbg_ref[...] += _colsum(dgt)
        dbr, dbs = dmerged * gr, dmerged * gs
        yrn_ref[...] = _bf(m["yrn"])
        ysn_ref[...] = _bf(m["ysn"])
        mg_ref[...] = _bf(gr * br + gs * bs)
        dbr_ref[...] = _bf(dbr)
        dbs_ref[...] = _bf(dbs)
        dyrn = _dot_nt(dbr, wr_ref[...])
        dysn = _dot_nt(dbs, ws_ref[...])
        rn = jnp.concatenate([slab * r for slab, r in m["ret"]], axis=1)
        dproj_ref[:, C_G:C_G + 1024] = _bf(dyrn * rn * m["dsg"])
        drn = dyrn * m["sg"]
        dyr_ref[...] = jnp.concatenate(
            [_rms_bwd(drn[:, h * RET_V_DIM:(h + 1) * RET_V_DIM], slab, r) for h, (slab, r) in enumerate(m["ret"])], axis=1)
        dsn_ref[...] += _colsum(dysn * m["ysh"])
        dysh = dysn * sn_ref[...]
        dys1 = jnp.concatenate(
            [_rms_bwd(dysh[:, h * GROUP_W:(h + 1) * GROUP_W], slab, r) for h, (slab, r) in enumerate(m["grp"])], axis=1)
        dproj_ref[:, C_Z:C_Z + 2048] = _bf(dys1 * m["ys0"] * m["dsz"])
        dys0 = dys1 * m["sz"]
        dys_ref[...] = dys0
        dxs_ref[...] = dys0 * dsk_ref[...]
        ddsk_ref[...] += _colsum(dys0 * xs)

    row = lambda w, c=0: pl.BlockSpec((ts, w), lambda i: (i, c))
    full = lambda a: pl.BlockSpec(a.shape, lambda i: (0,) * a.ndim)
    acc = lambda w: pl.BlockSpec((8, w), lambda i: (0, 0))
    sds = jax.ShapeDtypeStruct
    return _pcall(
        body, grid=(s // ts,),
        in_specs=[row(1024), row(1024), row(2048), row(2048), row(1024, C_G // 1024), row(2048, C_Z // 2048),
                  row(2048, C_GATES // 2048), full(bg), full(dsk), full(sn), full(wr), full(ws), full(wo)],
        out_specs=[row(1024), row(2048), row(2048), row(NP), row(1024), row(2048), row(1024),
                   row(1024), row(1024), acc(2048), acc(2048), acc(2048)],
        out_shape=[sds((s, 1024), F32), sds((s, 2048), F32), sds((s, 2048), F32), sds((s, NP), BF16),
                   sds((s, 1024), BF16), sds((s, 2048), BF16),
                   sds((s, 1024), BF16), sds((s, 1024), BF16), sds((s, 1024), BF16), sds((8, 2048), F32),
                   sds((8, 2048), F32), sds((8, 2048), F32)],
        compiler_params=_params("arbitrary"), name=name)(dout, yr, ys, xc, proj, proj, proj, bg, dsk, sn, wr, ws, wo)


def _prescan_bwd(proj, dxc, dxs_skip, gdtx, dqr, dkr, cosf, sinf, cw, cb, dtb, eexp_t, dproj, name):
    s = proj.shape[0]
    ts = _row_tile(s, 256)
    nt = s // ts
    m = ts + 8
    width = C_DT + DT_PAD

    def body(xbc_ref, prev_ref, nxt_ref, dt_ref, dxc_ref, dxcn_ref, dsk_ref, dskn_ref, gdt_ref, dq_ref, dk_ref,
             cos_ref, sin_ref, cw_ref, cb_ref, dtb_ref, et_ref, dproj_in, dp_ref, dcw_ref, dcb_ref, ddtb_ref):
        i = pl.program_id(0)

        @pl.when(i == 0)
        def _():
            ddtb_ref[...] = jnp.zeros_like(ddtb_ref)
            dcw_ref[...] = jnp.zeros_like(dcw_ref)
            dcb_ref[...] = jnp.zeros_like(dcb_ref)

        rows = lax.broadcasted_iota(jnp.int32, (m, 128), 0)
        live = (rows < ts) | (i < nt - 1)
        for st in range(SSM_CONV_DIM // 128):
            sl = slice(st * 128, (st + 1) * 128)
            prev = jnp.where(i > 0, prev_ref[:, sl], 0.0)
            xcat = jnp.concatenate([prev, xbc_ref[:, sl], nxt_ref[:, sl]], axis=0)
            shifted = [pltpu.roll(xcat, 3 - j, 0) for j in range(3)] + [xcat]
            pre = cb_ref[:, sl]
            for j in range(SSM_CONV):
                pre = pre + cw_ref[j:j + 1, sl] * shifted[j][8:]
            _, dsilu = _silu_and_grad(pre)
            dxc = jnp.concatenate([dxc_ref[:, sl], dxcn_ref[:, sl]], axis=0)
            if st * 128 < SSM_INNER:
                dxc = dxc + jnp.concatenate([dsk_ref[:, sl], dskn_ref[:, sl]], axis=0)
            dpre = jnp.where(live, dxc * dsilu, 0.0)
            dpt = dpre[0:ts]
            dx = cw_ref[3:4, sl] * dpt
            for j in range(3):
                dx = dx + cw_ref[j:j + 1, sl] * pltpu.roll(dpre, m - (3 - j), 0)[0:ts]
            for j in range(SSM_CONV):
                dcw_ref[8 * j:8 * j + 8, sl] += _colsum(dpt * shifted[j][8:8 + ts])
            dcb_ref[:, sl] += _colsum(dpt)
            dp_ref[:, sl] = _bf(dx)
        cs, sn = cos_ref[...], sin_ref[...]
        for h in range(RET_HEADS):
            sl = slice(h * 128, (h + 1) * 128)
            dq = dq_ref[:, sl]
            dk = dk_ref[:, sl] * (RET_QK_DIM ** -0.5)
            dp_ref[:, C_Q + h * 128:C_Q + (h + 1) * 128] = _bf(dq * cs + pltpu.roll(dq * sn, 64, 1))
            dp_ref[:, C_K + h * 128:C_K + (h + 1) * 128] = _bf(dk * cs + pltpu.roll(dk * sn, 64, 1))
        ddt = _split3_dot(gdt_ref[...], et_ref[...])
        ddt = ddt * _sigmoid(dt_ref[:, 0:128] + dtb_ref[...])
        ddtb_ref[...] += _colsum(ddt)
        dp_ref[:, C_DT:C_DT + 128] = _bf(ddt)
        dp_ref[:, C_DT + 128:C_DT + DT_PAD] = jnp.zeros((ts, DT_PAD - 128), BF16)

    row = lambda w, c=0: pl.BlockSpec((ts, w), lambda i: (i, c))
    nxt = lambda w: pl.BlockSpec((8, w), lambda i: (jnp.minimum((i + 1) * (ts // 8), s // 8 - 1), 0))
    full = lambda a: pl.BlockSpec(a.shape, lambda i: (0,) * a.ndim)
    sds = jax.ShapeDtypeStruct
    return _pcall(
        body, grid=(nt,),
        in_specs=[row(4096), _prev_rows_spec(ts, 4096), nxt(4096), row(DT_PAD, C_DT // DT_PAD), row(4096), nxt(4096),
                  row(2048), nxt(2048), row(2048), row(512), row(512), row(128), row(128), full(cw), full(cb),
                  full(dtb), full(eexp_t), ANY],
        out_specs=[row(width), pl.BlockSpec((32, 4096), lambda i: (0, 0)), pl.BlockSpec((8, 4096), lambda i: (0, 0)),
                   pl.BlockSpec((8, 128), lambda i: (0, 0))],
        out_shape=[sds(dproj.shape, BF16), sds((32, 4096), F32), sds((8, 4096), F32), sds((8, 128), F32)],
        input_output_aliases={17: 0},
        compiler_params=_params("arbitrary"), name=name)(proj, proj, proj, proj, dxc, dxc, dxs_skip, dxs_skip, gdtx,
                                                          dqr, dkr, cosf, sinf, cw, cb, dtb, eexp_t, dproj)


def _xattn_values(x, gain, wq, kv):
    r = _rstd(x)
    h = (x * r) * gain
    q = _dot(h, wq)
    ps, os_ = [], []
    for hd in range(XA_HEADS):
        sl = slice(hd * XA_HEAD_DIM, (hd + 1) * XA_HEAD_DIM)
        sc = _dot_nt(q[:, sl], kv[:, sl]) * (XA_HEAD_DIM ** -0.5)
        e = jnp.exp(sc - jnp.max(sc, axis=-1, keepdims=True))
        p = e / jnp.sum(e, axis=-1, keepdims=True)
        ps.append(p)
        os_.append(_dot(p, kv[:, D_MODEL + hd * XA_HEAD_DIM:D_MODEL + (hd + 1) * XA_HEAD_DIM]))
    return r, h, q, ps, jnp.concatenate(os_, axis=1)


def _xattn_fwd(x, gain, wq, kv, wo, name):
    s = x.shape[0]
    ts = _row_tile(s, 256)

    def body(x_ref, g_ref, wq_ref, kv_ref, wo_ref, o_ref):
        x_ = x_ref[...]
        _, _, _, _, o = _xattn_values(x_, g_ref[...], wq_ref[...], kv_ref[...])
        o_ref[...] = x_ + _dot(o, wo_ref[...])

    row = pl.BlockSpec((ts, D_MODEL), lambda i: (i, 0))
    full = lambda a: pl.BlockSpec(a.shape, lambda i: (0,) * a.ndim)
    return _pcall(
        body, grid=(s // ts,), in_specs=[row, full(gain), full(wq), full(kv), full(wo)], out_specs=row,
        out_shape=jax.ShapeDtypeStruct((s, D_MODEL), F32), compiler_params=_params("parallel"), name=name)(
            x, gain, wq, kv, wo)


def _xattn_bwd(x, dout, gain, wq, kv, wo, name):
    s = x.shape[0]
    m = kv.shape[0]
    ts = _row_tile(s, 256)

    def body(x_ref, do_ref, g_ref, wq_ref, kv_ref, wo_ref, dx_ref, h_ref, dq_ref, o_ref, dkv_ref, dg_ref):
        @pl.when(pl.program_id(0) == 0)
        def _():
            dkv_ref[...] = jnp.zeros_like(dkv_ref)
            dg_ref[...] = jnp.zeros_like(dg_ref)

        x_, do, kvv = x_ref[...], do_ref[...], kv_ref[...]
        r, h, q, ps, o = _xattn_values(x_, g_ref[...], wq_ref[...], kvv)
        dov = _dot_nt(do, wo_ref[...])
        dqs = []
        for hd in range(XA_HEADS):
            sl = slice(hd * XA_HEAD_DIM, (hd + 1) * XA_HEAD_DIM)
            vl = slice(D_MODEL + hd * XA_HEAD_DIM, D_MODEL + (hd + 1) * XA_HEAD_DIM)
            p, doh = ps[hd], dov[:, sl]
            dp = _dot_nt(doh, kvv[:, vl])
            dsc = p * (dp - jnp.sum(dp * p, axis=-1, keepdims=True)) * (XA_HEAD_DIM ** -0.5)
            dqs.append(_dot(dsc, kvv[:, sl]))
            dkv_ref[:, sl] += _dot_tn(dsc, q[:, sl])
            dkv_ref[:, vl] += _dot_tn(p, doh)
        dq = jnp.concatenate(dqs, axis=1)
        dh = _dot_nt(dq, wq_ref[...])
        dg_ref[...] += _colsum(dh * (x_ * r))
        dx_ref[...] = do + _rms_bwd(dh * g_ref[...], x_, r)
        h_ref[...] = _bf(h)
        dq_ref[...] = _bf(dq)
        o_ref[...] = _bf(o)

    row = pl.BlockSpec((ts, D_MODEL), lambda i: (i, 0))
    full = lambda a: pl.BlockSpec(a.shape, lambda i: (0,) * a.ndim)
    sds = jax.ShapeDtypeStruct
    return _pcall(
        body, grid=(s // ts,), in_specs=[row, row, full(gain), full(wq), full(kv), full(wo)],
        out_specs=[row, row, row, row, pl.BlockSpec((m, 2 * D_MODEL), lambda i: (0, 0)),
                   pl.BlockSpec((8, D_MODEL), lambda i: (0, 0))],
        out_shape=[sds((s, D_MODEL), F32), sds((s, D_MODEL), BF16), sds((s, D_MODEL), BF16), sds((s, D_MODEL), BF16),
                   sds((m, 2 * D_MODEL), F32), sds((8, D_MODEL), F32)],
        compiler_params=_params("arbitrary"), name=name)(x, dout, gain, wq, kv, wo)


def _mem_bwd(mem, gain, dkv, wkv, name):
    m = mem.shape[0]

    def body(mem_ref, g_ref, dkv_ref, w_ref, mn_ref, dg_ref):
        mm = mem_ref[...]
        r = _rstd(mm)
        xh = mm * r
        mn_ref[...] = _bf(xh * g_ref[...])
        nb, _, wb = w_ref.shape
        dmn = _dot_nt(dkv_ref[:, 0:wb], w_ref[0])
        for j in range(1, nb):
            dmn = dmn + _dot_nt(dkv_ref[:, j * wb:(j + 1) * wb], w_ref[j])
        dg_ref[...] = jnp.zeros_like(dg_ref) + _colsum(dmn * xh)

    full = lambda a: pl.BlockSpec(a.shape, lambda: (0,) * a.ndim)
    return _pcall(
        body, in_specs=[full(mem), full(gain), full(dkv), full(wkv)],
        out_specs=[pl.BlockSpec((m, D_MODEL), lambda: (0, 0)), pl.BlockSpec((8, D_MODEL), lambda: (0, 0))],
        out_shape=[jax.ShapeDtypeStruct((m, D_MODEL), BF16), jax.ShapeDtypeStruct((8, D_MODEL), F32)],
        compiler_params=pltpu.CompilerParams(vmem_limit_bytes=VMEM_LIMIT), name=name)(mem, gain, dkv, wkv)


def _mlp_fwd(x, gain, w1, w2, name):
    s = x.shape[0]
    ts = _row_tile(s, 512)
    tf = 1024
    nf = D_FF // tf

    def body(x_ref, g_ref, w1_ref, w2_ref, o_ref, h_sc, acc):
        j = pl.program_id(1)

        @pl.when(j == 0)
        def _():
            xx = x_ref[...]
            h_sc[...] = _bf((xx * _rstd(xx)) * g_ref[...])
            acc[...] = jnp.zeros_like(acc)

        a = jnp.dot(h_sc[...], w1_ref[0], preferred_element_type=F32)
        r = jnp.square(jnp.maximum(a, 0.0))
        acc[...] += _dot(r, w2_ref[...])

        @pl.when(j == nf - 1)
        def _():
            o_ref[...] = x_ref[...] + acc[...]

    row = pl.BlockSpec((ts, D_MODEL), lambda i, j: (i, 0))
    return _pcall(
        body, grid=(s // ts, nf),
        in_specs=[row, pl.BlockSpec((1, D_MODEL), lambda i, j: (0, 0)),
                  pl.BlockSpec((1, D_MODEL, tf), lambda i, j: (j, 0, 0)), pl.BlockSpec((tf, D_MODEL), lambda i, j: (j, 0))],
        out_specs=row, out_shape=jax.ShapeDtypeStruct((s, D_MODEL), F32),
        scratch_shapes=[pltpu.VMEM((ts, D_MODEL), BF16), pltpu.VMEM((ts, D_MODEL), F32)],
        compiler_params=_params("parallel", "arbitrary"), name=name)(x, gain, w1, w2)


def _mlp_bwd(x, dout, gain, w1, w2, name):
    s = x.shape[0]
    ts = _row_tile(s, 512)
    tf = 1024
    nf = D_FF // tf

    def body(x_ref, do_ref, g_ref, w1_ref, w2_ref, dx_ref, h_ref, r_ref, da_ref, dg_ref, h_sc, do_sc, acc):
        i, j = pl.program_id(0), pl.program_id(1)

        @pl.when(j == 0)
        def _():
            xx = x_ref[...]
            h_sc[...] = _bf((xx * _rstd(xx)) * g_ref[...])
            do_sc[...] = _bf(do_ref[...])
            acc[...] = jnp.zeros_like(acc)
            h_ref[...] = h_sc[...]

        @pl.when((j == 0) & (i == 0))
        def _():
            dg_ref[...] = jnp.zeros_like(dg_ref)

        a = jnp.dot(h_sc[...], w1_ref[0], preferred_element_type=F32)
        ra = jnp.maximum(a, 0.0)
        r_ref[...] = _bf(ra * ra)
        dr = lax.dot_general(do_sc[...], w2_ref[...], (((1,), (1,)), ((), ())), preferred_element_type=F32)
        da = _bf(dr * 2.0 * ra)
        da_ref[...] = da
        acc[...] += lax.dot_general(da, w1_ref[0], (((1,), (1,)), ((), ())), preferred_element_type=F32)

        @pl.when(j == nf - 1)
        def _():
            xx = x_ref[...]
            r = _rstd(xx)
            dh = acc[...]
            dg_ref[...] += _colsum(dh * (xx * r))
            dx_ref[...] = do_ref[...] + _rms_bwd(dh * g_ref[...], xx, r)

    row = pl.BlockSpec((ts, D_MODEL), lambda i, j: (i, 0))
    ff = pl.BlockSpec((ts, tf), lambda i, j: (i, j))
    sds = jax.ShapeDtypeStruct
    return _pcall(
        body, grid=(s // ts, nf),
        in_specs=[row, row, pl.BlockSpec((1, D_MODEL), lambda i, j: (0, 0)),
                  pl.BlockSpec((1, D_MODEL, tf), lambda i, j: (j, 0, 0)), pl.BlockSpec((tf, D_MODEL), lambda i, j: (j, 0))],
        out_specs=[row, row, ff, ff, pl.BlockSpec((8, D_MODEL), lambda i, j: (0, 0))],
        out_shape=[sds((s, D_MODEL), F32), sds((s, D_MODEL), BF16), sds((s, D_FF), BF16), sds((s, D_FF), BF16),
                   sds((8, D_MODEL), F32)],
        scratch_shapes=[pltpu.VMEM((ts, D_MODEL), BF16), pltpu.VMEM((ts, D_MODEL), BF16), pltpu.VMEM((ts, D_MODEL), F32)],
        compiler_params=_params("arbitrary", "arbitrary"), name=name)(x, dout, gain, w1, w2)


def _final(x, gain, tgt, name):
    s = x.shape[0]
    ts = _row_tile(s, 512)

    def body(x_ref, g_ref, t_ref, dx_ref, loss_ref, dg_ref):
        @pl.when(pl.program_id(0) == 0)
        def _():
            loss_ref[...] = jnp.zeros_like(loss_ref)
            dg_ref[...] = jnp.zeros_like(dg_ref)

        xx = x_ref[...]
        r = _rstd(xx)
        xh = xx * r
        err = xh * g_ref[...] - t_ref[...]
        loss_ref[...] += 0.5 * jnp.sum(jnp.sum(err * err, axis=1, keepdims=True), axis=0, keepdims=True) / D_MODEL
        dy = err * (1.0 / D_MODEL)
        dg_ref[...] += _colsum(dy * xh)
        dx_ref[...] = _rms_bwd(dy * g_ref[...], xx, r)

    row = pl.BlockSpec((ts, D_MODEL), lambda i: (i, 0))
    return _pcall(
        body, grid=(s // ts,), in_specs=[row, pl.BlockSpec((1, D_MODEL), lambda i: (0, 0)), row],
        out_specs=[row, pl.BlockSpec((8, 128), lambda i: (0, 0)), pl.BlockSpec((8, D_MODEL), lambda i: (0, 0))],
        out_shape=[jax.ShapeDtypeStruct((s, D_MODEL), F32), jax.ShapeDtypeStruct((8, 128), F32),
                   jax.ShapeDtypeStruct((8, D_MODEL), F32)],
        compiler_params=_params("arbitrary"), name=name)(x, gain, tgt)


def _as3d(a):
    return a.reshape((-1,) + a.shape[-2:])


def _ew_tile(r, c):
    if r % 256 == 0 or r <= 256:
        return _row_tile(r, 256), c
    return r, 128


def _sum_cast(terms, out_dtype, name):
    shape = terms[0].shape
    t3 = [_as3d(t) for t in terms]
    b, r, c = t3[0].shape
    tr, tc = _ew_tile(r, c)
    nc = c // tc

    def body(*refs):
        acc = refs[0][...].astype(F32)
        for t in refs[1:-1]:
            acc = acc + t[...].astype(F32)
        refs[-1][...] = acc.astype(out_dtype)

    spec = pl.BlockSpec((1, tr, tc), lambda i, j: (i, j // nc, j % nc))
    out = _pcall(body, grid=(b, (r // tr) * nc), in_specs=[spec] * len(t3), out_specs=spec,
                 out_shape=jax.ShapeDtypeStruct((b, r, c), out_dtype), compiler_params=_params("parallel", "parallel"),
                 name=name)(*t3)
    return out.reshape(shape)


def _split_axis(rows):
    return 0 if rows % 64 == 0 else 1


def _half_of(ref, which, lead=()):
    rows, cols = ref.shape[-2], ref.shape[-1]
    if _split_axis(rows) == 0:
        return ref.at[(*lead, pl.ds(which * (rows // 2), rows // 2))]
    return ref.at[(*lead, slice(None), pl.ds(which * (cols // 2), cols // 2))]


def _half_shape(shape):
    rows, cols = shape[-2], shape[-1]
    return (*shape[:-2], rows // 2, cols) if _split_axis(rows) == 0 else (*shape[:-2], rows, cols // 2)


def _pair_sum(a, b, sel, half_id, out_dtype, name):
    _, h, c = b.shape
    k = sel.shape[0]
    by_rows = _split_axis(a.shape[1]) == 0
    tr, tc = _ew_tile(h, c)
    nr, nc = h // tr, c // tc

    def body(sel_ref, hid_ref, a_ref, b_ref, o_ref):
        o_ref[...] = (a_ref[...] + b_ref[...]).astype(out_dtype)

    def a_map(q, j, sel_ref, hid_ref):
        if by_rows:
            return sel_ref[q], hid_ref[0] * nr + j // nc, j % nc
        return sel_ref[q], j // nc, hid_ref[0] * nc + j % nc

    blkshape = (1, tr, tc)
    grid_spec = pltpu.PrefetchScalarGridSpec(
        num_scalar_prefetch=2, grid=(k, nr * nc),
        in_specs=[pl.BlockSpec(blkshape, a_map),
                  pl.BlockSpec(blkshape, lambda q, j, sel_ref, hid_ref: (sel_ref[q], j // nc, j % nc))],
        out_specs=pl.BlockSpec(blkshape, lambda q, j, sel_ref, hid_ref: (q, j // nc, j % nc)))
    return _pcall(body, grid_spec=grid_spec, out_shape=jax.ShapeDtypeStruct((k, h, c), out_dtype),
                  compiler_params=_params("parallel", "parallel"), name=name)(sel, half_id, a, b)


def _total_into_half(terms, full_shape, half_id, name):
    rows, cols = full_shape
    h, c = terms[0].shape
    by_rows = _split_axis(rows) == 0
    tr, tc = _ew_tile(h, c)
    nr, nc = h // tr, c // tc

    def body(hid_ref, *refs):
        acc = refs[0][...].astype(F32)
        for t in refs[1:-1]:
            acc = acc + t[...].astype(F32)
        refs[-1][...] = acc

    def out_map(j, hid_ref):
        if by_rows:
            return hid_ref[0] * nr + j // nc, j % nc
        return j // nc, hid_ref[0] * nc + j % nc

    spec = pl.BlockSpec((tr, tc), lambda j, hid_ref: (j // nc, j % nc))
    grid_spec = pltpu.PrefetchScalarGridSpec(num_scalar_prefetch=1, grid=(nr * nc,), in_specs=[spec] * len(terms),
                                             out_specs=pl.BlockSpec((tr, tc), out_map))
    return _pcall(body, grid_spec=grid_spec, out_shape=jax.ShapeDtypeStruct((rows, cols), F32),
                  compiler_params=_params("parallel"), name=name)(half_id, *terms)


def _adamw(w, g, m, v, name):
    shape = w.shape
    w3, g3, m3, v3 = _as3d(w), _as3d(g), _as3d(m), _as3d(v)
    b, r, c = w3.shape
    tr, tc = _ew_tile(r, c)

    def body(w_ref, g_ref, m_ref, v_ref, d_ref, mo_ref, vo_ref):
        gg = g_ref[...]
        mn = ADAM_B1 * m_ref[...] + (1.0 - ADAM_B1) * gg
        vn = ADAM_B2 * v_ref[...] + (1.0 - ADAM_B2) * jnp.square(gg)
        m_hat = mn / (1.0 - ADAM_B1 ** ADAM_STEP)
        v_hat = vn / (1.0 - ADAM_B2 ** ADAM_STEP)
        d_ref[...] = -ADAM_LR * (m_hat / (jnp.sqrt(v_hat) + ADAM_EPS) + ADAM_WD * w_ref[...])
        mo_ref[...] = mn
        vo_ref[...] = vn

    spec = pl.BlockSpec((1, tr, tc), lambda i, j: (i, j // (c // tc), j % (c // tc)))
    sd = jax.ShapeDtypeStruct((b, r, c), F32)
    d, mo, vo = _pcall(body, grid=(b, (r // tr) * (c // tc)), in_specs=[spec] * 4, out_specs=[spec] * 3,
                       out_shape=[sd] * 3, compiler_params=_params("parallel", "parallel"), name=name)(w3, g3, m3, v3)
    return d.reshape(shape), mo.reshape(shape), vo.reshape(shape)


ANY = pl.BlockSpec(memory_space=pl.ANY)


def _place():
    return lax.axis_index("x"), lax.axis_index("y"), lax.axis_index("c")


def _flip(x, y, r):
    return (1 - x if r & 2 else x), (1 - y if r & 1 else y)


def _dma_sems(*counts):
    return [pltpu.SemaphoreType.DMA((k,)) for k in counts]


def _gather_ici(shards, rows=None, into=None):
    n = len(shards)
    assert rows is None or all(_split_axis(a.shape[0]) == 1 for a in shards)

    def piece(ref):
        return ref if rows is None else ref.at[pl.ds(rows[0], rows[1])]

    def copies(ins, outs, sems, incoming):
        send, recv = sems
        x, y, c = _place()
        out = []
        for r in (1, 2, 3):
            cx, cy = _flip(x, y, r)
            for a in range(n):
                block = 2 * cx + cy if incoming else 2 * x + y
                if rows is None:
                    src, dst = _half_of(ins[a], c), _half_of(outs[a], c, (block,))
                else:
                    hc = ins[a].shape[1] // 2
                    src = piece(ins[a]).at[:, pl.ds(c * hc, hc)]
                    dst = piece(outs[a].at[block]).at[:, pl.ds(c * hc, hc)]
                out.append(pltpu.make_async_remote_copy(
                    src_ref=src, dst_ref=dst, send_sem=send.at[(r - 1) * n + a], recv_sem=recv.at[(r - 1) * n + a],
                    device_id=(cx, cy, c), device_id_type=MESH))
        return out

    def start(ins, outs, sems):
        for cp in copies(ins, outs, sems, False):
            cp.start()

    def finish(ins, outs, sems):
        for cp in copies(ins, outs, sems, True):
            cp.wait_recv()
        for cp in copies(ins, outs, sems, False):
            cp.wait_send()

    return _Comm(list(shards) + list(into or []), [jax.ShapeDtypeStruct((4,) + a.shape, a.dtype) for a in shards],
                 _dma_sems(3 * n, 3 * n), start, finish, aliases={n + a: a for a in range(n)} if into else None)


def _gather_d2d(bufs, shards):
    n = len(bufs)

    def copies(ins, outs, sems, incoming):
        send, recv = sems
        x, y, c = _place()
        out = []
        for r in (1, 2, 3):
            cx, cy = _flip(x, y, r)
            for a in range(n):
                ref = _half_of(outs[a], (1 - c) if incoming else c, (2 * cx + cy,))
                out.append(pltpu.make_async_remote_copy(
                    src_ref=ref, dst_ref=ref, send_sem=send.at[(r - 1) * n + a], recv_sem=recv.at[(r - 1) * n + a],
                    device_id=(x, y, 1 - c), device_id_type=MESH))
        for a in range(n):
            out.append(pltpu.make_async_remote_copy(
                src_ref=ins[n + a], dst_ref=outs[a].at[2 * x + y], send_sem=send.at[3 * n + a],
                recv_sem=recv.at[3 * n + a], device_id=(x, y, 1 - c), device_id_type=MESH))
        return out

    def start(ins, outs, sems):
        for cp in copies(ins, outs, sems, False):
            cp.start()

    def finish(ins, outs, sems):
        for cp in copies(ins, outs, sems, True):
            cp.wait_recv()
        for cp in copies(ins, outs, sems, False):
            cp.wait_send()

    return _Comm(list(bufs) + list(shards), [jax.ShapeDtypeStruct(a.shape, a.dtype) for a in bufs],
                 _dma_sems(4 * n, 4 * n), start, finish, aliases={a: a for a in range(n)})


def _swap_rows(packs):
    n = len(packs)

    def copies(ins, outs, sems):
        send, recv = sems
        x, y, c = _place()
        return [pltpu.make_async_remote_copy(
            src_ref=_half_of(ins[a], 1 - c, (slice(None),)), dst_ref=outs[a], send_sem=send.at[a],
            recv_sem=recv.at[a], device_id=(x, y, 1 - c), device_id_type=MESH) for a in range(n)]

    def start(ins, outs, sems):
        for cp in copies(ins, outs, sems):
            cp.start()

    def finish(ins, outs, sems):
        for cp in copies(ins, outs, sems):
            cp.wait()

    return _Comm(packs, [jax.ShapeDtypeStruct(_half_shape(a.shape), a.dtype) for a in packs], _dma_sems(n, n),
                 start, finish)


def _exchange(arrs):
    n = len(arrs)

    def copies(ins, outs, sems):
        send, recv = sems
        x, y, c = _place()
        out = []
        for r in (1, 2, 3):
            cx, cy = _flip(x, y, r)
            for a in range(n):
                out.append(pltpu.make_async_remote_copy(
                    src_ref=ins[a].at[r - 1], dst_ref=outs[a].at[r - 1], send_sem=send.at[(r - 1) * n + a],
                    recv_sem=recv.at[(r - 1) * n + a], device_id=(cx, cy, c), device_id_type=MESH))
        return out

    def start(ins, outs, sems):
        for cp in copies(ins, outs, sems):
            cp.start()

    def finish(ins, outs, sems):
        for cp in copies(ins, outs, sems):
            cp.wait()

    return _Comm(arrs, [jax.ShapeDtypeStruct(a.shape, a.dtype) for a in arrs], _dma_sems(3 * n, 3 * n), start, finish)


def _halves_to_sibling(arrs):
    n = len(arrs)

    def copies(outs, sems, incoming):
        send, recv = sems
        x, y, c = _place()
        out = []
        for a in range(n):
            ref = _half_of(outs[a], (1 - c) if incoming else c)
            out.append(pltpu.make_async_remote_copy(
                src_ref=ref, dst_ref=ref, send_sem=send.at[a], recv_sem=recv.at[a], device_id=(x, y, 1 - c),
                device_id_type=MESH))
        return out

    def start(ins, outs, sems):
        for cp in copies(outs, sems, False):
            cp.start()

    def finish(ins, outs, sems):
        for cp in copies(outs, sems, True):
            cp.wait_recv()
        for cp in copies(outs, sems, False):
            cp.wait_send()

    return _Comm(arrs, [jax.ShapeDtypeStruct(a.shape, a.dtype) for a in arrs], _dma_sems(n, n), start, finish,
                 aliases={a: a for a in range(n)})


def _gather8(v, reduce, name):
    rows, w = v.shape

    def body(v_ref, out_ref, buf, send_sems, recv_sems):
        x, y, c = _place()
        me, sibling = (x, y, c), (x, y, 1 - c)
        chips = [_flip(x, y, r) for r in (1, 2, 3)]
        dst = out_ref if not reduce else buf

        def slot(px, py, pc):
            return dst.at[4 * px + 2 * py + pc]

        def copy(k, block, to, src=None):
            return pltpu.make_async_remote_copy(
                src_ref=slot(*block) if src is None else src, dst_ref=slot(*block), send_sem=send_sems.at[k],
                recv_sem=recv_sems.at[k], device_id=to, device_id_type=MESH)

        dst[4 * x + 2 * y + c] = v_ref[...]
        first = [copy(0, me, sibling, src=v_ref)]
        first += [copy(1 + j, me, (*chip, c), src=v_ref) for j, chip in enumerate(chips)]
        for cp in first:
            cp.start()
        passed = [copy(4 + j, (*chip, c), sibling) for j, chip in enumerate(chips)]
        for j, chip in enumerate(chips):
            copy(1 + j, (*chip, c), me).wait_recv()
            passed[j].start()
        copy(0, sibling, me).wait_recv()
        for j, chip in enumerate(chips):
            copy(4 + j, (*chip, 1 - c), me).wait_recv()
        for cp in first + passed:
            cp.wait_send()
        if reduce:
            acc = buf[0]
            for d in range(1, 8):
                acc = acc + buf[d]
            out_ref[...] = acc

    vm = pl.BlockSpec(memory_space=pltpu.VMEM)
    scratch = [pltpu.VMEM((8, rows, w) if reduce else (8, 8, 128), F32), pltpu.SemaphoreType.DMA((7,)),
               pltpu.SemaphoreType.DMA((7,))]
    out_shape = jax.ShapeDtypeStruct((rows, w) if reduce else (8, rows, w), F32)
    return _pcall(body, in_specs=[vm], out_specs=vm, out_shape=out_shape, scratch_shapes=scratch,
                  compiler_params=pltpu.CompilerParams(vmem_limit_bytes=VMEM_LIMIT), name=name)(v)


SMALL = [("norm_mix", 1024), ("b_gate", 2048), ("conv_b", 4096), ("dt_bias", 32), ("a_log", 32), ("d_skip", 32),
         ("ssm_norm", 2048), ("norm_xa", 1024), ("norm_mem", 1024), ("norm_mlp", 1024)]


def _rows_of(width):
    return max(1, width // 1024)


def _pack_rows(pieces):
    out = []
    for p in pieces:
        p = p.astype(F32)
        if p.shape[-1] < 1024:
            p = jnp.pad(p, ((0, 0), (0, 1024 - p.shape[-1])))
        out.append(p.reshape(-1, 1024))
    cat = jnp.concatenate(out, axis=0)
    pad = (-cat.shape[0]) % 8
    return jnp.pad(cat, ((0, pad), (0, 0))) if pad else cat


def _unpack_rows(packed, widths_rows):
    out, at = [], 0
    for r, w in widths_rows:
        k = r * _rows_of(w)
        p = packed[at:at + k]
        at += k
        out.append(p[:, :w] if w < 1024 else p.reshape(r, w))
    return out


def _to_cat(wt):
    pieces = [wt[O_XBC:O_XBC + 4096], wt[O_Q:O_Q + 512], wt[O_K:O_K + 512], wt[O_DT:O_DT + 32],
              jnp.zeros((DT_PAD - 32, wt.shape[1]), wt.dtype), wt[O_Z:O_Z + 2048], wt[O_GATES:O_GATES + 2048],
              wt[O_V:O_V + 1024], wt[O_G:O_G + 1024]]
    return jnp.concatenate(pieces, axis=0)


def _from_cat(gt):
    pieces = [gt[C_Q:C_Q + 512], gt[C_K:C_K + 512], gt[C_V:C_V + 1024], gt[C_G:C_G + 1024],
              gt[C_Z:C_Z + 2048], gt[C_XBC:C_XBC + 4096], gt[C_DT:C_DT + 32], gt[C_GATES:C_GATES + 2048]]
    return jnp.concatenate(pieces, axis=0)


PACK_ROWS = [("mlp_w1", 1024), ("mlp_w2", 1024), ("w_br_ssm", 512), ("w_br_ret", 256), ("w_out", 256), ("xa_wq", 256),
             ("xa_wo", 256)]
PACK_N = sum(r for _, r in PACK_ROWS)


def kernel(x, mem, positions, norm_mix, w_in, b_gate, conv_w, conv_b, dt_bias, a_log, d_skip, ssm_norm, w_br_ret, w_br_ssm, w_out, norm_xa, norm_mem, xa_wq, xa_wkv, xa_wo, norm_mlp, mlp_w1, mlp_w2, norm_final, loss_target, m_norm_mix, m_w_in, m_b_gate, m_conv_w, m_conv_b, m_dt_bias, m_a_log, m_d_skip, m_ssm_norm, m_w_br_ret, m_w_br_ssm, m_w_out, m_norm_xa, m_norm_mem, m_xa_wq, m_xa_wkv, m_xa_wo, m_norm_mlp, m_mlp_w1, m_mlp_w2, m_norm_final, v_norm_mix, v_w_in, v_b_gate, v_conv_w, v_conv_b, v_dt_bias, v_a_log, v_d_skip, v_ssm_norm, v_w_br_ret, v_w_br_ssm, v_w_out, v_norm_xa, v_norm_mem, v_xa_wq, v_xa_wkv, v_xa_wo, v_norm_mlp, v_mlp_w1, v_mlp_w2, v_norm_final):
    W = dict(norm_mix=norm_mix, w_in=w_in, b_gate=b_gate, conv_w=conv_w, conv_b=conv_b, dt_bias=dt_bias, a_log=a_log,
             d_skip=d_skip, ssm_norm=ssm_norm, w_br_ret=w_br_ret, w_br_ssm=w_br_ssm, w_out=w_out, norm_xa=norm_xa,
             norm_mem=norm_mem, xa_wq=xa_wq, xa_wkv=xa_wkv, xa_wo=xa_wo, norm_mlp=norm_mlp, mlp_w1=mlp_w1,
             mlp_w2=mlp_w2, norm_final=norm_final)
    M = dict(norm_mix=m_norm_mix, w_in=m_w_in, b_gate=m_b_gate, conv_w=m_conv_w, conv_b=m_conv_b, dt_bias=m_dt_bias,
             a_log=m_a_log, d_skip=m_d_skip, ssm_norm=m_ssm_norm, w_br_ret=m_w_br_ret, w_br_ssm=m_w_br_ssm,
             w_out=m_w_out, norm_xa=m_norm_xa, norm_mem=m_norm_mem, xa_wq=m_xa_wq, xa_wkv=m_xa_wkv, xa_wo=m_xa_wo,
             norm_mlp=m_norm_mlp, mlp_w1=m_mlp_w1, mlp_w2=m_mlp_w2, norm_final=m_norm_final)
    V = dict(norm_mix=v_norm_mix, w_in=v_w_in, b_gate=v_b_gate, conv_w=v_conv_w, conv_b=v_conv_b, dt_bias=v_dt_bias,
             a_log=v_a_log, d_skip=v_d_skip, ssm_norm=v_ssm_norm, w_br_ret=v_w_br_ret, w_br_ssm=v_w_br_ssm,
             w_out=v_w_out, norm_xa=v_norm_xa, norm_mem=v_norm_mem, xa_wq=v_xa_wq, xa_wkv=v_xa_wkv, xa_wo=v_xa_wo,
             norm_mlp=v_norm_mlp, mlp_w1=v_mlp_w1, mlp_w2=v_mlp_w2, norm_final=v_norm_final)
    nl = w_in.shape[0]
    s = x.shape[1]
    x0 = x[0]
    mem2 = mem[0]
    tgt = loss_target[0]
    blk = 2 * lax.axis_index("x") + lax.axis_index("y")

    grp_b = ["xa_wkv", "w_br_ret", "w_br_ssm", "w_out", "xa_wq", "xa_wo"]
    grp_c = ["mlp_w1", "mlp_w2"]
    win = ["w_in"]
    blk = blk.astype(jnp.int32)
    tr_ = lambda a: jnp.swapaxes(a, 1, 2)
    wb = {k: W[k].astype(BF16) for k in grp_b + grp_c}
    wb["w_in"] = tr_(w_in).astype(BF16)

    def shards_of(l, ks):
        return [wb[k][l] for k in ks]

    landed = _run_comm(_gather_ici(shards_of(0, win)), "gather_w_in")
    g_win = _run_comm(_gather_d2d(landed, shards_of(0, win)), "gather_w_in_cores")
    cw_all = _gather8(conv_w.reshape(nl * SSM_CONV, 1024), False, "gather_conv_w")
    cw_full = cw_all.reshape(4, 2, nl, SSM_CONV, 1024)[:, 0].transpose(1, 2, 0, 3).reshape(nl, SSM_CONV, SSM_CONV_DIM)

    offs = {}
    at = 0
    for k, r in PACK_ROWS:
        offs[k] = (at, r)
        at += r


    inv_freq = ROPE_THETA ** (-jnp.arange(0, RET_QK_DIM, 2, dtype=F32) / RET_QK_DIM)
    ang = positions.astype(F32)[0][:, None] * inv_freq
    cos, sin = jnp.cos(ang), jnp.sin(ang)
    cosf = jnp.concatenate([cos, cos], axis=1)
    sinf = jnp.concatenate([-sin, sin], axis=1)
    dm, qd, kd, cd = (jnp.asarray(c) for c in _ret_constants())
    eye, blkm = (jnp.asarray(c) for c in _ssd_constants())
    consts = (dm, qd, kd, cd, eye, blkm)
    e_np = _head_expand()
    eexp = jnp.asarray(e_np, BF16)
    eexp_t = jnp.asarray(e_np.T.copy(), BF16)

    saved = []
    xcur = x0
    for l in range(nl):
        wcat = _to_cat(g_win[0].reshape(IN_DIM, D_MODEL))
        cw, cb = cw_full[l], conv_b[l][None]
        dtb = jnp.pad(dt_bias[l], (0, 128 - SSM_HEADS))[None]
        ax = jnp.repeat(-jnp.exp(a_log[l]), 64)[None]
        dsk = jnp.repeat(d_skip[l], 64)[None]
        bg, sn = b_gate[l][None], ssm_norm[l][None]
        more = l + 1 < nl
        ici_b, ici_c = _gather_ici(shards_of(l, grp_b)), _gather_ici(shards_of(l, grp_c))
        _carry(ici_b)
        proj, u = _nmm(xcur, norm_mix[l][None], wcat, "in_proj", save_u=True, w_rows=True)
        _carry(ici_c)
        qr, kr, xc, dtx = _prescan(proj, cosf, sinf, cw, cb, dtb, eexp, "prescan")
        cores = _gather_d2d(ici_b.results + ici_c.results, shards_of(l, grp_b + grp_c))
        _carry(cores)
        yr, ys, sst, hst = _scan_fwd(qr, kr, proj, xc, dtx, ax, consts, "scan_fwd")
        gl = dict(zip(grp_b + grp_c, cores.results))
        rows_weight = lambda k: gl[k].reshape(-1, D_MODEL)
        wr, ws, wo = rows_weight("w_br_ret"), rows_weight("w_br_ssm"), rows_weight("w_out")
        wq, wxo, w2 = rows_weight("xa_wq"), rows_weight("xa_wo"), rows_weight("mlp_w2")
        w1, wkv = gl["mlp_w1"], gl["xa_wkv"]
        first = _gather_ici(shards_of(l + 1, win), rows=(0, WIN_SPLIT)) if more else None
        _carry(first)
        x1 = _postscan_fwd(xcur, yr, ys, xc, proj, bg, dsk, sn, wr, ws, wo, "postscan")
        kv = _bf(_nmm(mem2, norm_mem[l][None], wkv, "mem_kv"))
        rest = (_gather_ici(shards_of(l + 1, win), rows=(WIN_SPLIT, IN_DIM // 4 - WIN_SPLIT), into=first.results)
                if more else None)
        _carry(rest)
        x2 = _xattn_fwd(x1, norm_xa[l][None], wq, kv, wxo, "xattn")
        cores = _gather_d2d(rest.results, shards_of(l + 1, win)) if more else None
        _carry(cores)
        x3 = _mlp_fwd(x2, norm_mlp[l][None], w1, w2, "mlp")
        if more:
            g_win = cores.results
        saved.append(dict(x0=xcur, x1=x1, x2=x2, proj=proj, u=u, qr=qr, kr=kr, xc=xc, dtx=dtx, yr=yr, ys=ys, sst=sst,
                          hst=hst, kv=kv, wcat=wcat, wr=wr, ws=ws, wo=wo, wq=wq, wxo=wxo, w1=w1, w2=w2, wkv=wkv, cw=cw,
                          cb=cb, dtb=dtb, ax=ax, dsk=dsk, bg=bg, sn=sn))
        xcur = x3

    dx, loss_acc, dnf = _final(xcur, norm_final[None], tgt, "final")
    loss = lax.psum(loss_acc[0, 0], ("x", "y", "c"))

    small_g = [None] * nl
    c = lax.axis_index("c")
    half_id = c.astype(jnp.int32)[None]
    sel_own = blk[None]
    sel_rem = jnp.stack([blk ^ 1, blk ^ 2, blk ^ 3])
    layer_grads = {k: [None] * nl for k in ("pack", "w_in", "xa_wkv")}

    def pair_sums(packs, got):
        own = [_pair_sum(p, g_, sel_own, half_id, F32, "chip_sum_own")[0] for p, g_ in zip(packs, got)]
        out_b = [_pair_sum(p, g_, sel_rem, half_id, BF16, "chip_sum_send") for p, g_ in zip(packs, got)]
        return own, out_b

    full_shapes = [(PACK_N, D_MODEL), (IN_DIM // 4, D_MODEL), (D_MODEL, 2 * D_MODEL // 4)]

    def totals(own, inc):
        return [_total_into_half([o, i_[0], i_[1], i_[2]], fs, half_id, "grads_total")
                for o, i_, fs in zip(own, inc, full_shapes)]

    def finish_layer(lr, whole):
        layer_grads["pack"][lr], layer_grads["w_in"][lr], layer_grads["xa_wkv"][lr] = whole

    riding = None
    for l in reversed(range(nl)):
        sv = saved[l]
        swap = _swap_rows(riding[1]) if riding else None
        _carry(swap)
        dx2, hm, rm, dam, dg_mlp = _mlp_bwd(sv["x2"], dx, norm_mlp[l][None], sv["w1"], sv["w2"], "mlp_bwd")
        if riding:
            own, out_b = pair_sums(riding[1], swap.results)
        pack = _mm_tn_into(hm, dam, "dw_mlp1", None, offs["mlp_w1"][0], True)
        pack = _mm_tn_into(rm, dx, "dw_mlp2", pack, offs["mlp_w2"][0], False)
        dx1, hx, dqx, ox, dkv, dg_xa = _xattn_bwd(sv["x1"], dx2, norm_xa[l][None], sv["wq"], sv["kv"], sv["wxo"],
                                                  "xattn_bwd")
        pack = _mm_tn_into(hx, dqx, "dw_xq", pack, offs["xa_wq"][0], False)
        pack = _mm_tn_into(ox, dx2, "dw_xo", pack, offs["xa_wo"][0], False)
        memn, dg_mem = _mem_bwd(mem2, norm_mem[l][None], dkv, sv["wkv"], "mem_bwd")
        dwkv = _mm_tn(memn, dkv, "dw_xkv", col_blocks=4)
        chips_a = _exchange(out_b[0:1]) if riding else None
        _carry(chips_a)
        (dyr, dys, dxs_skip, dproj, yrn, ysn, mg, dbr, dbs, dbg, ddsk, dsn) = _postscan_bwd(
            dx1, sv["yr"], sv["ys"], sv["xc"], sv["proj"], sv["bg"], sv["dsk"], sv["sn"], sv["wr"], sv["ws"], sv["wo"],
            "postscan_bwd")
        pack = _mm_tn_into(mg, dx1, "dw_out", pack, offs["w_out"][0], False)
        pack = _mm_tn_into(yrn, dbr, "dw_br_ret", pack, offs["w_br_ret"][0], False)
        pack = _mm_tn_into(ysn, dbs, "dw_br_ssm", pack, offs["w_br_ssm"][0], False)
        early = [pack, dwkv] if l == 0 else None
        swap_e = _swap_rows(early) if early else None
        chips_b = _exchange(out_b[1:3]) if riding else None
        _carry(_both(chips_b, swap_e))
        dqr, dkr, dproj, dxc, gdtx, da_cols = _scan_bwd(sv["qr"], sv["kr"], sv["proj"], sv["xc"], sv["dtx"], sv["ax"],
                                                        consts, sv["sst"], sv["hst"], dyr, dys, dproj, "scan_bwd")
        if riding:
            red_half = totals(own, chips_a.results + chips_b.results)
        if early:
            own_e, out_e = pair_sums(early, swap_e.results)
        cores = _halves_to_sibling(red_half) if riding else None
        chips_e = _exchange(out_e[0:1]) if early else None
        _carry(_both(cores, chips_e))
        dproj, dcw, dcb, ddtb = _prescan_bwd(sv["proj"], dxc, dxs_skip, gdtx, dqr, dkr, cosf, sinf, sv["cw"], sv["cb"],
                                             sv["dtb"], eexp_t, dproj, "prescan_bwd")
        if riding:
            finish_layer(riding[0], cores.results)
        chips_kv = _exchange(out_e[1:2]) if early else None
        _carry(chips_kv)
        dwcat = _mm_tn(dproj, sv["u"], "dw_in")
        dx, dg_mix = _in_bwd(dproj, sv["wcat"], sv["x0"], norm_mix[l][None], dx1, "in_bwd")

        da_log = (da_cols.reshape(SSM_HEADS, 64).sum(axis=1)) * (-jnp.exp(a_log[l]))
        dd_skip = ddsk[0].reshape(SSM_HEADS, 64).sum(axis=1)
        small_g[l] = [dg_mix[0:1], dbg[0:1], dcb[0:1], ddtb[0:1, :SSM_HEADS], da_log[None], dd_skip[None], dsn[0:1],
                      dg_xa[0:1], dg_mem[0:1], dg_mlp[0:1], dcw[0::8]]
        riding = (l, [pack, _from_cat(dwcat).reshape(4, IN_DIM // 4, D_MODEL), dwkv])

    w_in_grad = riding[1][1:2]
    own_l, out_l = pair_sums(w_in_grad, _run_comm(_swap_rows(w_in_grad), "grads_core_swap"))
    inc_l = _run_comm(_exchange(out_l), "grads_chip_exchange")
    red_half = totals([own_e[0], own_l[0], own_e[1]], [chips_e.results[0], inc_l[0], chips_kv.results[0]])
    finish_layer(0, _run_comm(_halves_to_sibling(red_half), "grads_core_join"))
    grad_x = dx[None]

    pieces = []
    for l in range(nl):
        pieces += small_g[l]
    pieces.append(dnf[0:1])
    small_sum = _gather8(_pack_rows(pieces), True, "reduce_small")
    layout = []
    for l in range(nl):
        layout += [(1, w) for _, w in SMALL] + [(SSM_CONV, SSM_CONV_DIM)]
    layout.append((1, 1024))
    red = _unpack_rows(small_sum, layout)
    per = len(SMALL) + 1
    g_small = {k: jnp.concatenate([red[l * per + i] for l in range(nl)], axis=0) for i, (k, _) in enumerate(SMALL)}
    g_convw_full = jnp.stack([red[l * per + len(SMALL)] for l in range(nl)])
    g_small["conv_w"] = lax.dynamic_slice_in_dim(g_convw_full, blk * 1024, 1024, axis=2)
    g_small["norm_final"] = red[-1][0]

    grads = dict(g_small)
    pack_all = jnp.stack(layer_grads["pack"])
    for k, r in PACK_ROWS:
        grads[k] = pack_all[:, offs[k][0]:offs[k][0] + r]
    grads["w_in"] = jnp.stack(layer_grads["w_in"])
    grads["xa_wkv"] = jnp.stack(layer_grads["xa_wkv"])

    delta, new_m, new_v = {}, {}, {}
    for k in ["xa_wkv"] + [k for k, _ in PACK_ROWS]:
        delta[k], new_m[k], new_v[k] = _adamw(W[k], grads[k], M[k], V[k], "adamw_" + k)
    g_in_t = grads["w_in"]
    grads["w_in"] = tr_(g_in_t)
    d_t, m_t, v_t = _adamw(tr_(w_in), g_in_t, tr_(m_w_in), tr_(v_w_in), "adamw_w_in")
    delta["w_in"], new_m["w_in"], new_v["w_in"] = tr_(d_t), tr_(m_t), tr_(v_t)
    small_names = [k for k, _ in SMALL] + ["conv_w", "norm_final"]

    def pack_small(src):
        ps = []
        for k in small_names:
            a = src[k]
            ps.append(a.reshape(-1, a.shape[-1]) if a.ndim > 1 else a[None])
        return _pack_rows(ps)

    ds_, ms_, vs_ = _adamw(pack_small(W), pack_small(grads), pack_small(M), pack_small(V), "adamw_small")
    lay2 = []
    for k in small_names:
        a = W[k]
        lay2.append((int(np.prod(a.shape[:-1])) if a.ndim > 1 else 1, a.shape[-1]))
    for src, dst in ((ds_, delta), (ms_, new_m), (vs_, new_v)):
        for k, piece in zip(small_names, _unpack_rows(src, lay2)):
            dst[k] = piece.reshape(W[k].shape)

    names = ["norm_mix", "w_in", "b_gate", "conv_w", "conv_b", "dt_bias", "a_log", "d_skip", "ssm_norm", "w_br_ret",
             "w_br_ssm", "w_out", "norm_xa", "norm_mem", "xa_wq", "xa_wkv", "xa_wo", "norm_mlp", "mlp_w1", "mlp_w2",
             "norm_final"]
    return (loss, grad_x, *[grads[n] for n in names], *[delta[n] for n in names], *[new_m[n] for n in names],
            *[new_v[n] for n in names])
```

```python
import numpy as np
import jax
import jax.numpy as jnp
from jax import lax
from jax.experimental import pallas as pl
from jax.experimental.pallas import tpu as pltpu

F32 = jnp.float32
BF16 = jnp.bfloat16
MESH = pl.DeviceIdType.MESH

D_MODEL = 1024
CHUNK = 64
EPS = 1e-6
RET_HEADS = 4
RET_QK_DIM = 128
RET_V_DIM = 256
RET_QK = 512
RET_V = 1024
ROPE_THETA = 10000.0
SSM_INNER = 2048
SSM_HEADS = 32
SSM_GROUPS = 8
SSM_STATE = 128
SSM_CONV = 4
SSM_BC = 1024
SSM_CONV_DIM = 4096
XA_HEADS = 4
XA_HEAD_DIM = 256
D_FF = 4096
GROUP_W = 256

DT_PAD = 1024
IN_DIM = 11296
WIN_SPLIT = 1824
NP = 12288
C_XBC, C_Q, C_K, C_DT, C_Z, C_GATES, C_V, C_G = 0, 4096, 4608, 5120, 6144, 8192, 10240, 11264
O_Q, O_K, O_V, O_G, O_Z, O_XBC, O_DT, O_GATES = 0, 512, 1024, 2048, 3072, 5120, 9216, 9248

ADAM_LR = 0.001
ADAM_B1 = 0.9
ADAM_B2 = 0.999
ADAM_EPS = 1e-08
ADAM_WD = 0.01
ADAM_STEP = 10

VMEM_LIMIT = 56 * 1024 * 1024


def _params(*sem):
    return pltpu.CompilerParams(dimension_semantics=sem, vmem_limit_bytes=VMEM_LIMIT)


_CARRY = []


def _carry(comm):
    if comm is not None:
        _CARRY.append(comm)


def _pcall(body, **kw):
    if _CARRY:
        return _hosted(body, _CARRY.pop(), kw)
    return pl.pallas_call(body, **kw)


class _Comm:
    def __init__(self, ins, out_shapes, sems, start, finish, aliases=None):
        self.ins, self.out_shapes, self.sems = list(ins), list(out_shapes), list(sems)
        self.start, self.finish, self.aliases = start, finish, dict(aliases or {})
        self.results, self.parts = None, None

    def deliver(self, results):
        self.results = results
        if self.parts:
            a, b, k = self.parts
            a.deliver(results[:k])
            b.deliver(results[k:])


def _hosted(body, comm, kw):
    in_specs = list(kw.pop("in_specs"))
    out_specs, out_shape = kw.pop("out_specs"), kw.pop("out_shape")
    single = not isinstance(out_shape, (list, tuple))
    if single:
        out_specs, out_shape = [out_specs], [out_shape]
    out_specs, out_shape = list(out_specs), list(out_shape)
    scratch = list(kw.pop("scratch_shapes", []))
    grid = tuple(kw.get("grid", ()))
    aliases = dict(kw.pop("input_output_aliases", {}))
    n_in, n_out, n_sc = len(in_specs), len(out_shape), len(scratch)
    c_in, c_out = len(comm.ins), len(comm.out_shapes)
    for i, o in comm.aliases.items():
        aliases[n_in + i] = n_out + o
    kw["compiler_params"] = _params(*(["arbitrary"] * len(grid)))

    def wrapped(*refs):
        at = 0
        parts = []
        for cnt in (n_in, c_in, n_out, c_out, n_sc):
            parts.append(refs[at:at + cnt])
            at += cnt
        a, ci, b, co, s = parts
        cs = refs[at:]
        first, last = None, None
        for d, size in enumerate(grid):
            f, l_ = pl.program_id(d) == 0, pl.program_id(d) == size - 1
            first = f if first is None else first & f
            last = l_ if last is None else last & l_

        @pl.when(first)
        def _():
            comm.start(ci, co, cs)

        body(*a, *b, *s)

        @pl.when(last)
        def _():
            comm.finish(ci, co, cs)

    call = _pcall(wrapped, in_specs=in_specs + [ANY] * c_in, out_specs=out_specs + [ANY] * c_out,
                  out_shape=out_shape + comm.out_shapes, scratch_shapes=scratch + comm.sems,
                  input_output_aliases=aliases, **kw)

    def run(*ops):
        res = call(*ops, *comm.ins)
        comm.deliver(list(res[n_out:]))
        return res[0] if single else list(res[:n_out])

    return run


def _both(a, b):
    if a is None or b is None:
        return a if b is None else b
    ni, no, ns = len(a.ins), len(a.out_shapes), len(a.sems)

    def start(ins, outs, sems):
        a.start(ins[:ni], outs[:no], sems[:ns])
        b.start(ins[ni:], outs[no:], sems[ns:])

    def finish(ins, outs, sems):
        a.finish(ins[:ni], outs[:no], sems[:ns])
        b.finish(ins[ni:], outs[no:], sems[ns:])

    both = _Comm(a.ins + b.ins, a.out_shapes + b.out_shapes, a.sems + b.sems, start, finish,
                 {**a.aliases, **{ni + i: no + o for i, o in b.aliases.items()}})
    both.parts = (a, b, no)
    return both


def _run_comm(comm, name):
    def body(*refs):
        c_in, c_out = len(comm.ins), len(comm.out_shapes)
        ci, co, cs = refs[:c_in], refs[c_in:c_in + c_out], refs[c_in + c_out:]
        comm.start(ci, co, cs)
        comm.finish(ci, co, cs)

    aliases = {i: o for i, o in comm.aliases.items()}
    res = _pcall(body, in_specs=[ANY] * len(comm.ins), out_specs=[ANY] * len(comm.out_shapes),
                 out_shape=comm.out_shapes, scratch_shapes=comm.sems, input_output_aliases=aliases, name=name)(*comm.ins)
    comm.deliver(list(res))
    return comm.results


def _bf(a):
    return a.astype(BF16)


def _dot(a, b):
    return jnp.dot(_bf(a), _bf(b), preferred_element_type=F32)


def _dot_nt(a, b):
    return lax.dot_general(_bf(a), _bf(b), (((1,), (1,)), ((), ())), preferred_element_type=F32)


def _dot_tn(a, b):
    return lax.dot_general(_bf(a), _bf(b), (((0,), (0,)), ((), ())), preferred_element_type=F32)


def _colsum(a):
    return jnp.sum(a, axis=0, keepdims=True)


def _rstd(x):
    return lax.rsqrt(jnp.mean(x * x, axis=-1, keepdims=True) + EPS)


def _rms_bwd(dy, x, rstd):
    xh = x * rstd
    return rstd * (dy - xh * jnp.mean(dy * xh, axis=-1, keepdims=True))


def _sigmoid(x):
    return 1.0 / (1.0 + jnp.exp(-x))


def _silu_and_grad(x):
    s = _sigmoid(x)
    return x * s, s + x * s * (1.0 - s)


def _softplus(x):
    u = jnp.exp(-jnp.abs(x))
    l1p = jnp.where(u < 1e-4, u * (1.0 - 0.5 * u), jnp.log(1.0 + u))
    return jnp.maximum(x, 0.0) + l1p


def _split3_dot(a, e):
    hi = a.astype(BF16)
    r1 = a - hi.astype(F32)
    mid = r1.astype(BF16)
    lo = (r1 - mid.astype(F32)).astype(BF16)
    return (jnp.dot(hi, e, preferred_element_type=F32) + jnp.dot(mid, e, preferred_element_type=F32)
            + jnp.dot(lo, e, preferred_element_type=F32))


def _cumsum_rows(a):
    rows = lax.broadcasted_iota(jnp.int32, a.shape, 0)
    s = 1
    while s < a.shape[0]:
        a = a + jnp.where(rows >= s, pltpu.roll(a, s, 0), 0.0)
        s *= 2
    return a


def _revcumsum_rows(a):
    n = a.shape[0]
    rows = lax.broadcasted_iota(jnp.int32, a.shape, 0)
    s = 1
    while s < n:
        a = a + jnp.where(rows < n - s, pltpu.roll(a, n - s, 0), 0.0)
        s *= 2
    return a


def _rms_groups(y, width):
    out = []
    for h in range(y.shape[1] // width):
        slab = y[:, h * width:(h + 1) * width]
        out.append((slab, _rstd(slab)))
    return out


def _ret_constants():
    idx = np.arange(CHUNK, dtype=np.float32)
    lg = np.log1p(-(np.float32(2.0) ** (np.float32(-5.0) - np.arange(RET_HEADS, dtype=np.float32)))).astype(np.float32)
    rel = np.abs(idx[:, None] - idx[None, :])
    dm = np.exp(lg[:, None, None] * rel).astype(np.float32)
    qd = np.exp(lg[None, :] * (idx[:, None] + 1.0)).astype(np.float32)
    kd = np.exp(lg[None, :] * (CHUNK - 1.0 - idx[:, None])).astype(np.float32)
    cd = np.exp(lg * CHUNK).astype(np.float32)
    qd = np.repeat(qd, RET_QK_DIM, axis=1)
    kd = np.repeat(kd, RET_QK_DIM, axis=1)
    cd = np.repeat(cd, RET_QK_DIM)[:, None] * np.ones((1, RET_V_DIM), np.float32)
    return dm, qd, kd, cd.astype(np.float32)


def _ssd_constants():
    eye = np.tile(np.eye(CHUNK, dtype=np.float32), (1, GROUP_W // CHUNK))
    blk = np.kron(np.eye(GROUP_W // CHUNK, dtype=np.float32), np.ones((CHUNK, CHUNK), np.float32))
    return eye, blk


def _head_expand():
    e = np.zeros((128, SSM_INNER), np.float32)
    for h in range(SSM_HEADS):
        e[h, h * 64:(h + 1) * 64] = 1.0
    return e


def _ret_chunk_fwd(qh, kh, vh, sh, dmh, qdh, kdh, cdh):
    a = _dot_nt(qh, kh) * dmh
    y = _dot(a, vh) + _dot(qh * qdh, sh)
    s_new = sh * cdh + _dot_tn(kh * kdh, vh)
    return y, s_new


def _ret_chunk_bwd(qh, kh, vh, sh, dmh, qdh, kdh, cdh, dy, ds_new):
    a = _dot_nt(qh, kh) * dmh
    dp = _dot_nt(dy, vh) * dmh
    dq = _dot(dp, kh) + _dot_nt(dy, sh) * qdh
    dk = _dot_tn(dp, qh) + _dot_nt(vh, ds_new) * kdh
    dv = _dot_tn(a, dy) + _dot(kh * kdh, ds_new)
    ds = cdh * ds_new + _dot_tn(qh * qdh, dy)
    return dq, dk, dv, ds


def _ssd_common(xs, dtx, ax, eye):
    cum = _cumsum_rows(dtx * ax)
    last = cum[CHUNK - 1:CHUNK, :]
    r = _colsum(jnp.where(eye > 0.5, cum, 0.0))
    return cum, last, r, xs * dtx


def _tile4(a):
    return jnp.concatenate([a, a, a, a], axis=0)


def _ssd_chunk_fwd(xs, dtx, b, c, ax, hg, eye, blk):
    cum, last, r, x = _ssd_common(xs, dtx, ax, eye)
    lam = jnp.exp(-jnp.abs(cum - r))
    wc = _dot_nt(c, _tile4(b)) * lam
    bd = _tile4(x) * blk
    y = _dot(wc, bd) + _dot(c, hg) * jnp.exp(cum)
    h_new = hg * jnp.exp(last) + _dot_tn(b, x * jnp.exp(last - cum))
    return y, h_new


def _ssd_chunk_bwd(xs, dtx, b, c, ax, hg, eye, blk, dy, dh_new):
    cum, last, r, x = _ssd_common(xs, dtx, ax, eye)
    delta = cum - r
    lam = jnp.exp(-jnp.abs(delta))
    b4 = _tile4(b)
    cb4 = _dot_nt(c, b4)
    wc = cb4 * lam
    bd = _tile4(x) * blk
    ecx = jnp.exp(cum)
    wl = jnp.exp(last - cum)
    ecl = jnp.exp(last)
    z = _dot(c, hg)
    dwc = _dot_nt(dy, bd)
    dbd = _dot_tn(wc, dy) * blk
    dx = dbd[0:64] + dbd[64:128] + dbd[128:192] + dbd[192:256]
    dt_ = _dot(b, dh_new)
    dx = dx + dt_ * wl
    dcb4 = dwc * lam
    dz = dy * ecx
    dc = _dot(dcb4, b4) + _dot_nt(dz, hg)
    db4 = _dot_tn(dcb4, c)
    db = db4[0:64] + db4[64:128] + db4[128:192] + db4[192:256] + _dot_nt(x * wl, dh_new)
    g = dwc * cb4 * lam * (-jnp.sign(delta))
    dr = -_colsum(g)
    dwl = dt_ * x * wl
    u = g + eye * dr + dy * z * ecx - dwl
    lastrow = _colsum(dwl) + _colsum(dh_new * hg) * ecl
    rows = lax.broadcasted_iota(jnp.int32, u.shape, 0)
    u = u + jnp.where(rows == CHUNK - 1, lastrow, 0.0)
    dh = _dot_tn(c, dz) + dh_new * ecl
    rc = _revcumsum_rows(u)
    dxs = dx * dtx
    g_dtx = dx * xs + rc * ax
    da = _colsum(rc * dtx)
    return dxs, g_dtx, db, dc, da, dh


def _row_tile(s, want):
    t = min(s, want)
    assert s % t == 0
    return t


def _nmm(x, gain, w, name, tn=1024, save_u=False, w_rows=False):
    s, d = x.shape
    blocked = w.ndim == 3
    if blocked:
        tn = w.shape[2]
        n = w.shape[0] * tn
        w_spec = pl.BlockSpec((1, d, tn), lambda i, j: (j, 0, 0))
    elif w_rows:
        n = w.shape[0]
        w_spec = pl.BlockSpec((tn, d), lambda i, j: (j, 0))
    else:
        n = w.shape[1]
        w_spec = pl.BlockSpec((d, tn), lambda i, j: (0, j))
    tm = _row_tile(s, 1024)
    assert n % tn == 0

    def body(x_ref, g_ref, w_ref, *rest):
        o_ref, u_sc = rest[0], rest[-1]

        @pl.when(pl.program_id(1) == 0)
        def _():
            xx = x_ref[...]
            u = _bf((xx * _rstd(xx)) * g_ref[...])
            u_sc[...] = u
            if save_u:
                rest[1][...] = u

        if w_rows:
            o_ref[...] = _dot_nt(u_sc[...], w_ref[...])
        else:
            o_ref[...] = jnp.dot(u_sc[...], w_ref[0] if blocked else w_ref[...], preferred_element_type=F32)

    out_shape = [jax.ShapeDtypeStruct((s, n), F32)]
    out_specs = [pl.BlockSpec((tm, tn), lambda i, j: (i, j))]
    if save_u:
        out_shape.append(jax.ShapeDtypeStruct((s, d), BF16))
        out_specs.append(pl.BlockSpec((tm, d), lambda i, j: (i, 0)))
    res = _pcall(
        body, grid=(s // tm, n // tn),
        in_specs=[pl.BlockSpec((tm, d), lambda i, j: (i, 0)), pl.BlockSpec((1, d), lambda i, j: (0, 0)), w_spec],
        out_specs=out_specs, out_shape=out_shape, scratch_shapes=[pltpu.VMEM((tm, d), BF16)],
        compiler_params=_params("parallel", "arbitrary"), name=name)(x, gain, w)
    return res if save_u else res[0]


def _mm_tn(a, b, name, tm=1024, tn=1024, col_blocks=None):
    k, m = a.shape
    n = b.shape[1]
    tk = _row_tile(k, 2048)
    tm, tn = min(tm, m), min(tn, n)
    if col_blocks:
        tn = n // col_blocks
    assert m % tm == 0 and n % tn == 0
    nk = k // tk

    def body(a_ref, b_ref, o_ref, acc):
        kk = pl.program_id(2)

        @pl.when(kk == 0)
        def _():
            acc[...] = jnp.zeros_like(acc)

        acc[...] += _dot_tn(a_ref[...], b_ref[...])

        @pl.when(kk == nk - 1)
        def _():
            if col_blocks:
                o_ref[0] = acc[...]
            else:
                o_ref[...] = acc[...]

    if col_blocks:
        out_spec = pl.BlockSpec((1, tm, tn), lambda i, j, kk: (j, i, 0))
        out_shape = jax.ShapeDtypeStruct((col_blocks, m, tn), F32)
    else:
        out_spec = pl.BlockSpec((tm, tn), lambda i, j, kk: (i, j))
        out_shape = jax.ShapeDtypeStruct((m, n), F32)
    return _pcall(
        body, grid=(m // tm, n // tn, nk),
        in_specs=[pl.BlockSpec((tk, tm), lambda i, j, kk: (kk, i)), pl.BlockSpec((tk, tn), lambda i, j, kk: (kk, j))],
        out_specs=out_spec, out_shape=out_shape,
        scratch_shapes=[pltpu.VMEM((tm, tn), F32)],
        compiler_params=_params("parallel", "parallel", "arbitrary"), name=name)(a, b)


def _mm_tn_into(a, b, name, pack, off, by_cols):
    k, m = a.shape
    n = b.shape[1]
    tk = _row_tile(k, 2048)
    nk = k // tk
    rows = m if by_cols else m // 4
    tm = min(m, 1024)
    nb = 1 if by_cols else tm // rows
    assert tm == nb * rows and off % rows == 0 and n == (4096 if by_cols else 1024)

    def body(a_ref, b_ref, *rest):
        o_ref, acc = rest[-2], rest[-1]
        kk = pl.program_id(2)

        @pl.when(kk == 0)
        def _():
            acc[...] = jnp.zeros_like(acc)

        acc[...] += _dot_tn(a_ref[...], b_ref[...])

        @pl.when(kk == nk - 1)
        def _():
            o_ref[...] = acc[...].reshape(nb, rows, 1024)

    if by_cols:
        out_spec = pl.BlockSpec((1, rows, 1024), lambda i, j, kk: (j, off // rows, 0))
    else:
        out_spec = pl.BlockSpec((nb, rows, 1024), lambda i, j, kk: (i, off // rows, 0))
    in_specs = [pl.BlockSpec((tk, tm), lambda i, j, kk: (kk, i)), pl.BlockSpec((tk, 1024), lambda i, j, kk: (kk, j))]
    ops, alias = [a, b], {}
    if pack is not None:
        in_specs.append(ANY)
        ops.append(pack)
        alias = {2: 0}
    return _pcall(
        body, grid=(m // tm, n // 1024, nk), in_specs=in_specs, out_specs=out_spec,
        out_shape=jax.ShapeDtypeStruct((4, PACK_N, 1024), F32), scratch_shapes=[pltpu.VMEM((tm, 1024), F32)],
        input_output_aliases=alias, compiler_params=_params("parallel", "parallel", "arbitrary"), name=name)(*ops)


def _in_bwd(dproj, wcat_t, x, gain, dres, name):
    s, n = dproj.shape
    d = wcat_t.shape[1]
    tm = _row_tile(s, 1024)
    tk = 1024
    nk = n // tk
    ns = s // tm

    def body(dp_ref, w_ref, x_ref, g_ref, dr_ref, dx_ref, dg_ref, acc):
        i, kk = pl.program_id(0), pl.program_id(1)

        @pl.when(kk == 0)
        def _():
            acc[...] = jnp.zeros_like(acc)

        @pl.when((kk == 0) & (i == 0))
        def _():
            dg_ref[...] = jnp.zeros_like(dg_ref)

        acc[...] += _dot(dp_ref[...], w_ref[...])

        @pl.when(kk == nk - 1)
        def _():
            xx = x_ref[...]
            r = _rstd(xx)
            du = acc[...]
            dg_ref[...] += _colsum(du * (xx * r))
            dx_ref[...] = dr_ref[...] + _rms_bwd(du * g_ref[...], xx, r)

    return _pcall(
        body, grid=(ns, nk),
        in_specs=[pl.BlockSpec((tm, tk), lambda i, kk: (i, kk)), pl.BlockSpec((tk, d), lambda i, kk: (kk, 0)),
                  pl.BlockSpec((tm, d), lambda i, kk: (i, 0)), pl.BlockSpec((1, d), lambda i, kk: (0, 0)),
                  pl.BlockSpec((tm, d), lambda i, kk: (i, 0))],
        out_specs=[pl.BlockSpec((tm, d), lambda i, kk: (i, 0)), pl.BlockSpec((8, d), lambda i, kk: (0, 0))],
        out_shape=[jax.ShapeDtypeStruct((s, d), F32), jax.ShapeDtypeStruct((8, d), F32)],
        scratch_shapes=[pltpu.VMEM((tm, d), F32)],
        compiler_params=_params("arbitrary", "arbitrary"), name=name)(dproj, wcat_t, x, gain, dres)


def _prev_rows_spec(ts, width):
    return pl.BlockSpec((8, width), lambda i: (jnp.maximum(i * (ts // 8) - 1, 0), 0))


def _prescan(proj, cosf, sinf, cw, cb, dtb, eexp, name):
    s = proj.shape[0]
    ts = _row_tile(s, 256)

    def body(xbc_ref, prev_ref, q_ref, k_ref, dt_ref, cos_ref, sin_ref, cw_ref, cb_ref, dtb_ref, e_ref,
             qo_ref, ko_ref, xc_ref, dtx_ref):
        i = pl.program_id(0)
        for st in range(SSM_CONV_DIM // 128):
            sl = slice(st * 128, (st + 1) * 128)
            prev = jnp.where(i > 0, prev_ref[:, sl], 0.0)
            xcat = jnp.concatenate([prev, xbc_ref[:, sl]], axis=0)
            pre = cb_ref[:, sl] + cw_ref[3:4, sl] * xcat[8:8 + ts]
            for j in range(3):
                pre = pre + cw_ref[j:j + 1, sl] * pltpu.roll(xcat, 3 - j, 0)[8:8 + ts]
            xc_ref[:, sl] = pre * _sigmoid(pre)
        cs, sn = cos_ref[...], sin_ref[...]
        for h in range(RET_HEADS):
            sl = slice(h * 128, (h + 1) * 128)
            qh, kh = q_ref[:, sl], k_ref[:, sl]
            qo_ref[:, sl] = qh * cs + pltpu.roll(qh, 64, 1) * sn
            ko_ref[:, sl] = (kh * cs + pltpu.roll(kh, 64, 1) * sn) * (RET_QK_DIM ** -0.5)
        dtv = _softplus(dt_ref[:, 0:128] + dtb_ref[...])
        dtx_ref[...] = _split3_dot(dtv, e_ref[...])

    row = lambda w, c: pl.BlockSpec((ts, w), lambda i: (i, c))
    full = lambda a: pl.BlockSpec(a.shape, lambda i: (0,) * a.ndim)
    return _pcall(
        body, grid=(s // ts,),
        in_specs=[row(4096, 0), _prev_rows_spec(ts, 4096), row(512, C_Q // 512), row(512, C_K // 512),
                  row(DT_PAD, C_DT // DT_PAD), row(128, 0), row(128, 0), full(cw), full(cb), full(dtb), full(eexp)],
        out_specs=[row(512, 0), row(512, 0), row(4096, 0), row(2048, 0)],
        out_shape=[jax.ShapeDtypeStruct((s, 512), F32), jax.ShapeDtypeStruct((s, 512), F32),
                   jax.ShapeDtypeStruct((s, 4096), F32), jax.ShapeDtypeStruct((s, 2048), F32)],
        compiler_params=_params("parallel"), name=name)(proj, proj, proj, proj, proj, cosf, sinf, cw, cb, dtb, eexp)


def _scan_fwd(qr, kr, proj, xc, dtx, ax, consts, name):
    s = qr.shape[0]
    nc = s // CHUNK
    dm, qd, kd, cd, eye, blk = consts

    def body(q_ref, k_ref, v_ref, xc_ref, dtx_ref, ax_ref, dm_ref, qd_ref, kd_ref, cd_ref, eye_ref, blk_ref,
             yr_ref, ys_ref, sst_ref, hst_ref, s_sc, h_sc):
        @pl.when(pl.program_id(0) == 0)
        def _():
            s_sc[...] = jnp.zeros_like(s_sc)
            h_sc[...] = jnp.zeros_like(h_sc)

        sst_ref[0] = s_sc[...]
        hst_ref[0] = h_sc[...]
        for h in range(RET_HEADS):
            ql, vl = slice(h * 128, (h + 1) * 128), slice(h * 256, (h + 1) * 256)
            y, s_new = _ret_chunk_fwd(q_ref[:, ql], k_ref[:, ql], v_ref[:, vl], s_sc[ql, :], dm_ref[h],
                                      qd_ref[:, ql], kd_ref[:, ql], cd_ref[ql, :])
            yr_ref[:, vl] = y
            s_sc[ql, :] = s_new
        eye_v, blk_v = eye_ref[...], blk_ref[...]
        for g in range(SSM_GROUPS):
            sl = slice(g * GROUP_W, (g + 1) * GROUP_W)
            bl = slice(SSM_INNER + g * 128, SSM_INNER + (g + 1) * 128)
            cl = slice(SSM_INNER + SSM_BC + g * 128, SSM_INNER + SSM_BC + (g + 1) * 128)
            y, h_new = _ssd_chunk_fwd(xc_ref[:, sl], dtx_ref[:, sl], xc_ref[:, bl], xc_ref[:, cl], ax_ref[:, sl],
                                      h_sc[:, sl], eye_v, blk_v)
            ys_ref[:, sl] = y
            h_sc[:, sl] = h_new

    row = lambda w, c=0: pl.BlockSpec((CHUNK, w), lambda i: (i, c))
    full = lambda a: pl.BlockSpec(a.shape, lambda i: (0,) * a.ndim)
    return _pcall(
        body, grid=(nc,),
        in_specs=[row(512), row(512), row(1024, C_V // 1024), row(4096), row(2048), full(ax), full(dm), full(qd),
                  full(kd), full(cd), full(eye), full(blk)],
        out_specs=[row(1024), row(2048), pl.BlockSpec((1, 512, 256), lambda i: (i, 0, 0)),
                   pl.BlockSpec((1, 128, 2048), lambda i: (i, 0, 0))],
        out_shape=[jax.ShapeDtypeStruct((s, 1024), F32), jax.ShapeDtypeStruct((s, 2048), F32),
                   jax.ShapeDtypeStruct((nc, 512, 256), F32), jax.ShapeDtypeStruct((nc, 128, 2048), F32)],
        scratch_shapes=[pltpu.VMEM((512, 256), F32), pltpu.VMEM((128, 2048), F32)],
        compiler_params=_params("arbitrary"), name=name)(qr, kr, proj, xc, dtx, ax, dm, qd, kd, cd, eye, blk)


def _scan_bwd(qr, kr, proj, xc, dtx, ax, consts, sst, hst, dyr, dys, dproj, name):
    s = qr.shape[0]
    nc = s // CHUNK
    dm, qd, kd, cd, eye, blk = consts

    def body(q_ref, k_ref, v_ref, xc_ref, dtx_ref, ax_ref, dm_ref, qd_ref, kd_ref, cd_ref, eye_ref, blk_ref,
             sst_ref, hst_ref, dyr_ref, dys_ref, dproj_in, dq_ref, dk_ref, dv_ref, dxc_ref, gdt_ref, da_ref, ds_sc,
             dh_sc):
        @pl.when(pl.program_id(0) == 0)
        def _():
            ds_sc[...] = jnp.zeros_like(ds_sc)
            dh_sc[...] = jnp.zeros_like(dh_sc)
            da_ref[...] = jnp.zeros_like(da_ref)

        for h in range(RET_HEADS):
            ql, vl = slice(h * 128, (h + 1) * 128), slice(h * 256, (h + 1) * 256)
            dq, dk, dv, ds = _ret_chunk_bwd(q_ref[:, ql], k_ref[:, ql], v_ref[:, vl], sst_ref[0, ql, :], dm_ref[h],
                                            qd_ref[:, ql], kd_ref[:, ql], cd_ref[ql, :], dyr_ref[:, vl], ds_sc[ql, :])
            dq_ref[:, ql] = dq
            dk_ref[:, ql] = dk
            dv_ref[:, vl] = _bf(dv)
            ds_sc[ql, :] = ds
        eye_v, blk_v = eye_ref[...], blk_ref[...]
        for g in range(SSM_GROUPS):
            sl = slice(g * GROUP_W, (g + 1) * GROUP_W)
            bl = slice(SSM_INNER + g * 128, SSM_INNER + (g + 1) * 128)
            cl = slice(SSM_INNER + SSM_BC + g * 128, SSM_INNER + SSM_BC + (g + 1) * 128)
            dxs, g_dtx, db, dc, da, dh = _ssd_chunk_bwd(
                xc_ref[:, sl], dtx_ref[:, sl], xc_ref[:, bl], xc_ref[:, cl], ax_ref[:, sl], hst_ref[0, :, sl],
                eye_v, blk_v, dys_ref[:, sl], dh_sc[:, sl])
            dxc_ref[:, sl] = dxs
            dxc_ref[:, bl] = db
            dxc_ref[:, cl] = dc
            gdt_ref[:, sl] = g_dtx
            da_ref[:, sl] += da
            dh_sc[:, sl] = dh

    row = lambda w, c=0: pl.BlockSpec((CHUNK, w), lambda i: (nc - 1 - i, c))
    full = lambda a: pl.BlockSpec(a.shape, lambda i: (0,) * a.ndim)
    return _pcall(
        body, grid=(nc,),
        in_specs=[row(512), row(512), row(1024, C_V // 1024), row(4096), row(2048), full(ax), full(dm), full(qd),
                  full(kd), full(cd), full(eye), full(blk),
                  pl.BlockSpec((1, 512, 256), lambda i: (nc - 1 - i, 0, 0)),
                  pl.BlockSpec((1, 128, 2048), lambda i: (nc - 1 - i, 0, 0)), row(1024), row(2048), ANY],
        out_specs=[row(512), row(512), row(1024, C_V // 1024), row(4096), row(2048),
                   pl.BlockSpec((1, 2048), lambda i: (0, 0))],
        out_shape=[jax.ShapeDtypeStruct((s, 512), F32), jax.ShapeDtypeStruct((s, 512), F32),
                   jax.ShapeDtypeStruct(dproj.shape, BF16), jax.ShapeDtypeStruct((s, 4096), F32),
                   jax.ShapeDtypeStruct((s, 2048), F32), jax.ShapeDtypeStruct((1, 2048), F32)],
        scratch_shapes=[pltpu.VMEM((512, 256), F32), pltpu.VMEM((128, 2048), F32)],
        input_output_aliases={16: 2},
        compiler_params=_params("arbitrary"), name=name)(qr, kr, proj, xc, dtx, ax, dm, qd, kd, cd, eye, blk, sst, hst,
                                                          dyr, dys, dproj)


def _mix_values(yr, g, ys, xs, z, gates, bg, dsk, sn):
    sg, dsg = _silu_and_grad(g)
    ret = _rms_groups(yr, RET_V_DIM)
    yrn = jnp.concatenate([slab * r for slab, r in ret], axis=1) * sg
    sz, dsz = _silu_and_grad(z)
    ys0 = ys + xs * dsk
    ys1 = ys0 * sz
    grp = _rms_groups(ys1, GROUP_W)
    ysh = jnp.concatenate([slab * r for slab, r in grp], axis=1)
    ysn = ysh * sn
    gg = _sigmoid(gates + bg)
    return dict(sg=sg, dsg=dsg, ret=ret, yrn=yrn, sz=sz, dsz=dsz, ys0=ys0, ys1=ys1, grp=grp, ysh=ysh, ysn=ysn,
                gr=gg[:, :D_MODEL], gs=gg[:, D_MODEL:])


def _postscan_fwd(x, yr, ys, xc, proj, bg, dsk, sn, wr, ws, wo, name):
    s = x.shape[0]
    ts = _row_tile(s, 256)

    def body(x_ref, yr_ref, ys_ref, xs_ref, g_ref, z_ref, gt_ref, bg_ref, dsk_ref, sn_ref, wr_ref, ws_ref, wo_ref,
             o_ref):
        m = _mix_values(yr_ref[...], g_ref[...], ys_ref[...], xs_ref[...], z_ref[...], gt_ref[...], bg_ref[...],
                        dsk_ref[...], sn_ref[...])
        merged = m["gr"] * _dot(m["yrn"], wr_ref[...]) + m["gs"] * _dot(m["ysn"], ws_ref[...])
        o_ref[...] = x_ref[...] + _dot(merged, wo_ref[...])

    row = lambda w, c=0: pl.BlockSpec((ts, w), lambda i: (i, c))
    full = lambda a: pl.BlockSpec(a.shape, lambda i: (0,) * a.ndim)
    return _pcall(
        body, grid=(s // ts,),
        in_specs=[row(1024), row(1024), row(2048), row(2048), row(1024, C_G // 1024), row(2048, C_Z // 2048),
                  row(2048, C_GATES // 2048), full(bg), full(dsk), full(sn), full(wr), full(ws), full(wo)],
        out_specs=row(1024), out_shape=jax.ShapeDtypeStruct((s, D_MODEL), F32),
        compiler_params=_params("parallel"), name=name)(x, yr, ys, xc, proj, proj, proj, bg, dsk, sn, wr, ws, wo)


def _postscan_bwd(dout, yr, ys, xc, proj, bg, dsk, sn, wr, ws, wo, name):
    s = dout.shape[0]
    ts = _row_tile(s, 128)

    def body(do_ref, yr_ref, ys_ref, xs_ref, g_ref, z_ref, gt_ref, bg_ref, dsk_ref, sn_ref, wr_ref, ws_ref, wo_ref,
             dyr_ref, dys_ref, dxs_ref, dproj_ref, yrn_ref, ysn_ref, mg_ref, dbr_ref, dbs_ref,
             dbg_ref, ddsk_ref, dsn_ref):
        @pl.when(pl.program_id(0) == 0)
        def _():
            dbg_ref[...] = jnp.zeros_like(dbg_ref)
            ddsk_ref[...] = jnp.zeros_like(ddsk_ref)
            dsn_ref[...] = jnp.zeros_like(dsn_ref)

        xs = xs_ref[...]
        m = _mix_values(yr_ref[...], g_ref[...], ys_ref[...], xs, z_ref[...], gt_ref[...], bg_ref[...],
                        dsk_ref[...], sn_ref[...])
        gr, gs = m["gr"], m["gs"]
        br, bs = _dot(m["yrn"], wr_ref[...]), _dot(m["ysn"], ws_ref[...])
        dmerged = _dot_nt(do_ref[...], wo_ref[...])
        dgt = jnp.concatenate([dmerged * br * gr * (1.0 - gr), dmerged * bs * gs * (1.0 - gs)], axis=1)
        dproj_ref[:, C_GATES:C_GATES + 2048] = _bf(dgt)
        dbg_ref[...] += _colsum(dgt)
        dbr, dbs = dmerged * gr, dmerged * gs
        yrn_ref[...] = _bf(m["yrn"])
        ysn_ref[...] = _bf(m["ysn"])
        mg_ref[...] = _bf(gr * br + gs * bs)
        dbr_ref[...] = _bf(dbr)
        dbs_ref[...] = _bf(dbs)
        dyrn = _dot_nt(dbr, wr_ref[...])
        dysn = _dot_nt(dbs, ws_ref[...])
        rn = jnp.concatenate([slab * r for slab, r in m["ret"]], axis=1)
        dproj_ref[:, C_G:C_G + 1024] = _bf(dyrn * rn * m["dsg"])
        drn = dyrn * m["sg"]
        dyr_ref[...] = jnp.concatenate(
            [_rms_bwd(drn[:, h * RET_V_DIM:(h + 1) * RET_V_DIM], slab, r) for h, (slab, r) in enumerate(m["ret"])], axis=1)
        dsn_ref[...] += _colsum(dysn * m["ysh"])
        dysh = dysn * sn_ref[...]
        dys1 = jnp.concatenate(
            [_rms_bwd(dysh[:, h * GROUP_W:(h + 1) * GROUP_W], slab, r) for h, (slab, r) in enumerate(m["grp"])], axis=1)
        dproj_ref[:, C_Z:C_Z + 2048] = _bf(dys1 * m["ys0"] * m["dsz"])
        dys0 = dys1 * m["sz"]
        dys_ref[...] = dys0
        dxs_ref[...] = dys0 * dsk_ref[...]
        ddsk_ref[...] += _colsum(dys0 * xs)

    row = lambda w, c=0: pl.BlockSpec((ts, w), lambda i: (i, c))
    full = lambda a: pl.BlockSpec(a.shape, lambda i: (0,) * a.ndim)
    acc = lambda w: pl.BlockSpec((8, w), lambda i: (0, 0))
    sds = jax.ShapeDtypeStruct
    return _pcall(
        body, grid=(s // ts,),
        in_specs=[row(1024), row(1024), row(2048), row(2048), row(1024, C_G // 1024), row(2048, C_Z // 2048),
                  row(2048, C_GATES // 2048), full(bg), full(dsk), full(sn), full(wr), full(ws), full(wo)],
        out_specs=[row(1024), row(2048), row(2048), row(NP), row(1024), row(2048), row(1024),
                   row(1024), row(1024), acc(2048), acc(2048), acc(2048)],
        out_shape=[sds((s, 1024), F32), sds((s, 2048), F32), sds((s, 2048), F32), sds((s, NP), BF16),
                   sds((s, 1024), BF16), sds((s, 2048), BF16),
                   sds((s, 1024), BF16), sds((s, 1024), BF16), sds((s, 1024), BF16), sds((8, 2048), F32),
                   sds((8, 2048), F32), sds((8, 2048), F32)],
        compiler_params=_params("arbitrary"), name=name)(dout, yr, ys, xc, proj, proj, proj, bg, dsk, sn, wr, ws, wo)


def _prescan_bwd(proj, dxc, dxs_skip, gdtx, dqr, dkr, cosf, sinf, cw, cb, dtb, eexp_t, dproj, name):
    s = proj.shape[0]
    ts = _row_tile(s, 256)
    nt = s // ts
    m = ts + 8
    width = C_DT + DT_PAD

    def body(xbc_ref, prev_ref, nxt_ref, dt_ref, dxc_ref, dxcn_ref, dsk_ref, dskn_ref, gdt_ref, dq_ref, dk_ref,
             cos_ref, sin_ref, cw_ref, cb_ref, dtb_ref, et_ref, dproj_in, dp_ref, dcw_ref, dcb_ref, ddtb_ref):
        i = pl.program_id(0)

        @pl.when(i == 0)
        def _():
            ddtb_ref[...] = jnp.zeros_like(ddtb_ref)
            dcw_ref[...] = jnp.zeros_like(dcw_ref)
            dcb_ref[...] = jnp.zeros_like(dcb_ref)

        rows = lax.broadcasted_iota(jnp.int32, (m, 128), 0)
        live = (rows < ts) | (i < nt - 1)
        for st in range(SSM_CONV_DIM // 128):
            sl = slice(st * 128, (st + 1) * 128)
            prev = jnp.where(i > 0, prev_ref[:, sl], 0.0)
            xcat = jnp.concatenate([prev, xbc_ref[:, sl], nxt_ref[:, sl]], axis=0)
            shifted = [pltpu.roll(xcat, 3 - j, 0) for j in range(3)] + [xcat]
            pre = cb_ref[:, sl]
            for j in range(SSM_CONV):
                pre = pre + cw_ref[j:j + 1, sl] * shifted[j][8:]
            _, dsilu = _silu_and_grad(pre)
            dxc = jnp.concatenate([dxc_ref[:, sl], dxcn_ref[:, sl]], axis=0)
            if st * 128 < SSM_INNER:
                dxc = dxc + jnp.concatenate([dsk_ref[:, sl], dskn_ref[:, sl]], axis=0)
            dpre = jnp.where(live, dxc * dsilu, 0.0)
            dpt = dpre[0:ts]
            dx = cw_ref[3:4, sl] * dpt
            for j in range(3):
                dx = dx + cw_ref[j:j + 1, sl] * pltpu.roll(dpre, m - (3 - j), 0)[0:ts]
            for j in range(SSM_CONV):
                dcw_ref[8 * j:8 * j + 8, sl] += _colsum(dpt * shifted[j][8:8 + ts])
            dcb_ref[:, sl] += _colsum(dpt)
            dp_ref[:, sl] = _bf(dx)
        cs, sn = cos_ref[...], sin_ref[...]
        for h in range(RET_HEADS):
            sl = slice(h * 128, (h + 1) * 128)
            dq = dq_ref[:, sl]
            dk = dk_ref[:, sl] * (RET_QK_DIM ** -0.5)
            dp_ref[:, C_Q + h * 128:C_Q + (h + 1) * 128] = _bf(dq * cs + pltpu.roll(dq * sn, 64, 1))
            dp_ref[:, C_K + h * 128:C_K + (h + 1) * 128] = _bf(dk * cs + pltpu.roll(dk * sn, 64, 1))
        ddt = _split3_dot(gdt_ref[...], et_ref[...])
        ddt = ddt * _sigmoid(dt_ref[:, 0:128] + dtb_ref[...])
        ddtb_ref[...] += _colsum(ddt)
        dp_ref[:, C_DT:C_DT + 128] = _bf(ddt)
        dp_ref[:, C_DT + 128:C_DT + DT_PAD] = jnp.zeros((ts, DT_PAD - 128), BF16)

    row = lambda w, c=0: pl.BlockSpec((ts, w), lambda i: (i, c))
    nxt = lambda w: pl.BlockSpec((8, w), lambda i: (jnp.minimum((i + 1) * (ts // 8), s // 8 - 1), 0))
    full = lambda a: pl.BlockSpec(a.shape, lambda i: (0,) * a.ndim)
    sds = jax.ShapeDtypeStruct
    return _pcall(
        body, grid=(nt,),
        in_specs=[row(4096), _prev_rows_spec(ts, 4096), nxt(4096), row(DT_PAD, C_DT // DT_PAD), row(4096), nxt(4096),
                  row(2048), nxt(2048), row(2048), row(512), row(512), row(128), row(128), full(cw), full(cb),
                  full(dtb), full(eexp_t), ANY],
        out_specs=[row(width), pl.BlockSpec((32, 4096), lambda i: (0, 0)), pl.BlockSpec((8, 4096), lambda i: (0, 0)),
                   pl.BlockSpec((8, 128), lambda i: (0, 0))],
        out_shape=[sds(dproj.shape, BF16), sds((32, 4096), F32), sds((8, 4096), F32), sds((8, 128), F32)],
        input_output_aliases={17: 0},
        compiler_params=_params("arbitrary"), name=name)(proj, proj, proj, proj, dxc, dxc, dxs_skip, dxs_skip, gdtx,
                                                          dqr, dkr, cosf, sinf, cw, cb, dtb, eexp_t, dproj)


def _xattn_values(x, gain, wq, kv):
    r = _rstd(x)
    h = (x * r) * gain
    q = _dot(h, wq)
    ps, os_ = [], []
    for hd in range(XA_HEADS):
        sl = slice(hd * XA_HEAD_DIM, (hd + 1) * XA_HEAD_DIM)
        sc = _dot_nt(q[:, sl], kv[:, sl]) * (XA_HEAD_DIM ** -0.5)
        e = jnp.exp(sc - jnp.max(sc, axis=-1, keepdims=True))
        p = e / jnp.sum(e, axis=-1, keepdims=True)
        ps.append(p)
        os_.append(_dot(p, kv[:, D_MODEL + hd * XA_HEAD_DIM:D_MODEL + (hd + 1) * XA_HEAD_DIM]))
    return r, h, q, ps, jnp.concatenate(os_, axis=1)


def _xattn_fwd(x, gain, wq, kv, wo, name):
    s = x.shape[0]
    ts = _row_tile(s, 256)

    def body(x_ref, g_ref, wq_ref, kv_ref, wo_ref, o_ref):
        x_ = x_ref[...]
        _, _, _, _, o = _xattn_values(x_, g_ref[...], wq_ref[...], kv_ref[...])
        o_ref[...] = x_ + _dot(o, wo_ref[...])

    row = pl.BlockSpec((ts, D_MODEL), lambda i: (i, 0))
    full = lambda a: pl.BlockSpec(a.shape, lambda i: (0,) * a.ndim)
    return _pcall(
        body, grid=(s // ts,), in_specs=[row, full(gain), full(wq), full(kv), full(wo)], out_specs=row,
        out_shape=jax.ShapeDtypeStruct((s, D_MODEL), F32), compiler_params=_params("parallel"), name=name)(
            x, gain, wq, kv, wo)


def _xattn_bwd(x, dout, gain, wq, kv, wo, name):
    s = x.shape[0]
    m = kv.shape[0]
    ts = _row_tile(s, 256)

    def body(x_ref, do_ref, g_ref, wq_ref, kv_ref, wo_ref, dx_ref, h_ref, dq_ref, o_ref, dkv_ref, dg_ref):
        @pl.when(pl.program_id(0) == 0)
        def _():
            dkv_ref[...] = jnp.zeros_like(dkv_ref)
            dg_ref[...] = jnp.zeros_like(dg_ref)

        x_, do, kvv = x_ref[...], do_ref[...], kv_ref[...]
        r, h, q, ps, o = _xattn_values(x_, g_ref[...], wq_ref[...], kvv)
        dov = _dot_nt(do, wo_ref[...])
        dqs = []
        for hd in range(XA_HEADS):
            sl = slice(hd * XA_HEAD_DIM, (hd + 1) * XA_HEAD_DIM)
            vl = slice(D_MODEL + hd * XA_HEAD_DIM, D_MODEL + (hd + 1) * XA_HEAD_DIM)
            p, doh = ps[hd], dov[:, sl]
            dp = _dot_nt(doh, kvv[:, vl])
            dsc = p * (dp - jnp.sum(dp * p, axis=-1, keepdims=True)) * (XA_HEAD_DIM ** -0.5)
            dqs.append(_dot(dsc, kvv[:, sl]))
            dkv_ref[:, sl] += _dot_tn(dsc, q[:, sl])
            dkv_ref[:, vl] += _dot_tn(p, doh)
        dq = jnp.concatenate(dqs, axis=1)
        dh = _dot_nt(dq, wq_ref[...])
        dg_ref[...] += _colsum(dh * (x_ * r))
        dx_ref[...] = do + _rms_bwd(dh * g_ref[...], x_, r)
        h_ref[...] = _bf(h)
        dq_ref[...] = _bf(dq)
        o_ref[...] = _bf(o)

    row = pl.BlockSpec((ts, D_MODEL), lambda i: (i, 0))
    full = lambda a: pl.BlockSpec(a.shape, lambda i: (0,) * a.ndim)
    sds = jax.ShapeDtypeStruct
    return _pcall(
        body, grid=(s // ts,), in_specs=[row, row, full(gain), full(wq), full(kv), full(wo)],
        out_specs=[row, row, row, row, pl.BlockSpec((m, 2 * D_MODEL), lambda i: (0, 0)),
                   pl.BlockSpec((8, D_MODEL), lambda i: (0, 0))],
        out_shape=[sds((s, D_MODEL), F32), sds((s, D_MODEL), BF16), sds((s, D_MODEL), BF16), sds((s, D_MODEL), BF16),
                   sds((m, 2 * D_MODEL), F32), sds((8, D_MODEL), F32)],
        compiler_params=_params("arbitrary"), name=name)(x, dout, gain, wq, kv, wo)


def _mem_bwd(mem, gain, dkv, wkv, name):
    m = mem.shape[0]

    def body(mem_ref, g_ref, dkv_ref, w_ref, mn_ref, dg_ref):
        mm = mem_ref[...]
        r = _rstd(mm)
        xh = mm * r
        mn_ref[...] = _bf(xh * g_ref[...])
        nb, _, wb = w_ref.shape
        dmn = _dot_nt(dkv_ref[:, 0:wb], w_ref[0])
        for j in range(1, nb):
            dmn = dmn + _dot_nt(dkv_ref[:, j * wb:(j + 1) * wb], w_ref[j])
        dg_ref[...] = jnp.zeros_like(dg_ref) + _colsum(dmn * xh)

    full = lambda a: pl.BlockSpec(a.shape, lambda: (0,) * a.ndim)
    return _pcall(
        body, in_specs=[full(mem), full(gain), full(dkv), full(wkv)],
        out_specs=[pl.BlockSpec((m, D_MODEL), lambda: (0, 0)), pl.BlockSpec((8, D_MODEL), lambda: (0, 0))],
        out_shape=[jax.ShapeDtypeStruct((m, D_MODEL), BF16), jax.ShapeDtypeStruct((8, D_MODEL), F32)],
        compiler_params=pltpu.CompilerParams(vmem_limit_bytes=VMEM_LIMIT), name=name)(mem, gain, dkv, wkv)


def _mlp_fwd(x, gain, w1, w2, name):
    s = x.shape[0]
    ts = _row_tile(s, 512)
    tf = 1024
    nf = D_FF // tf

    def body(x_ref, g_ref, w1_ref, w2_ref, o_ref, h_sc, acc):
        j = pl.program_id(1)

        @pl.when(j == 0)
        def _():
            xx = x_ref[...]
            h_sc[...] = _bf((xx * _rstd(xx)) * g_ref[...])
            acc[...] = jnp.zeros_like(acc)

        a = jnp.dot(h_sc[...], w1_ref[0], preferred_element_type=F32)
        r = jnp.square(jnp.maximum(a, 0.0))
        acc[...] += _dot(r, w2_ref[...])

        @pl.when(j == nf - 1)
        def _():
            o_ref[...] = x_ref[...] + acc[...]

    row = pl.BlockSpec((ts, D_MODEL), lambda i, j: (i, 0))
    return _pcall(
        body, grid=(s // ts, nf),
        in_specs=[row, pl.BlockSpec((1, D_MODEL), lambda i, j: (0, 0)),
                  pl.BlockSpec((1, D_MODEL, tf), lambda i, j: (j, 0, 0)), pl.BlockSpec((tf, D_MODEL), lambda i, j: (j, 0))],
        out_specs=row, out_shape=jax.ShapeDtypeStruct((s, D_MODEL), F32),
        scratch_shapes=[pltpu.VMEM((ts, D_MODEL), BF16), pltpu.VMEM((ts, D_MODEL), F32)],
        compiler_params=_params("parallel", "arbitrary"), name=name)(x, gain, w1, w2)


def _mlp_bwd(x, dout, gain, w1, w2, name):
    s = x.shape[0]
    ts = _row_tile(s, 512)
    tf = 1024
    nf = D_FF // tf

    def body(x_ref, do_ref, g_ref, w1_ref, w2_ref, dx_ref, h_ref, r_ref, da_ref, dg_ref, h_sc, do_sc, acc):
        i, j = pl.program_id(0), pl.program_id(1)

        @pl.when(j == 0)
        def _():
            xx = x_ref[...]
            h_sc[...] = _bf((xx * _rstd(xx)) * g_ref[...])
            do_sc[...] = _bf(do_ref[...])
            acc[...] = jnp.zeros_like(acc)
            h_ref[...] = h_sc[...]

        @pl.when((j == 0) & (i == 0))
        def _():
            dg_ref[...] = jnp.zeros_like(dg_ref)

        a = jnp.dot(h_sc[...], w1_ref[0], preferred_element_type=F32)
        ra = jnp.maximum(a, 0.0)
        r_ref[...] = _bf(ra * ra)
        dr = lax.dot_general(do_sc[...], w2_ref[...], (((1,), (1,)), ((), ())), preferred_element_type=F32)
        da = _bf(dr * 2.0 * ra)
        da_ref[...] = da
        acc[...] += lax.dot_general(da, w1_ref[0], (((1,), (1,)), ((), ())), preferred_element_type=F32)

        @pl.when(j == nf - 1)
        def _():
            xx = x_ref[...]
            r = _rstd(xx)
            dh = acc[...]
            dg_ref[...] += _colsum(dh * (xx * r))
            dx_ref[...] = do_ref[...] + _rms_bwd(dh * g_ref[...], xx, r)

    row = pl.BlockSpec((ts, D_MODEL), lambda i, j: (i, 0))
    ff = pl.BlockSpec((ts, tf), lambda i, j: (i, j))
    sds = jax.ShapeDtypeStruct
    return _pcall(
        body, grid=(s // ts, nf),
        in_specs=[row, row, pl.BlockSpec((1, D_MODEL), lambda i, j: (0, 0)),
                  pl.BlockSpec((1, D_MODEL, tf), lambda i, j: (j, 0, 0)), pl.BlockSpec((tf, D_MODEL), lambda i, j: (j, 0))],
        out_specs=[row, row, ff, ff, pl.BlockSpec((8, D_MODEL), lambda i, j: (0, 0))],
        out_shape=[sds((s, D_MODEL), F32), sds((s, D_MODEL), BF16), sds((s, D_FF), BF16), sds((s, D_FF), BF16),
                   sds((8, D_MODEL), F32)],
        scratch_shapes=[pltpu.VMEM((ts, D_MODEL), BF16), pltpu.VMEM((ts, D_MODEL), BF16), pltpu.VMEM((ts, D_MODEL), F32)],
        compiler_params=_params("arbitrary", "arbitrary"), name=name)(x, dout, gain, w1, w2)


def _final(x, gain, tgt, name):
    s = x.shape[0]
    ts = _row_tile(s, 512)

    def body(x_ref, g_ref, t_ref, dx_ref, loss_ref, dg_ref):
        @pl.when(pl.program_id(0) == 0)
        def _():
            loss_ref[...] = jnp.zeros_like(loss_ref)
            dg_ref[...] = jnp.zeros_like(dg_ref)

        xx = x_ref[...]
        r = _rstd(xx)
        xh = xx * r
        err = xh * g_ref[...] - t_ref[...]
        loss_ref[...] += 0.5 * jnp.sum(jnp.sum(err * err, axis=1, keepdims=True), axis=0, keepdims=True) / D_MODEL
        dy = err * (1.0 / D_MODEL)
        dg_ref[...] += _colsum(dy * xh)
        dx_ref[...] = _rms_bwd(dy * g_ref[...], xx, r)

    row = pl.BlockSpec((ts, D_MODEL), lambda i: (i, 0))
    return _pcall(
        body, grid=(s // ts,), in_specs=[row, pl.BlockSpec((1, D_MODEL), lambda i: (0, 0)), row],
        out_specs=[row, pl.BlockSpec((8, 128), lambda i: (0, 0)), pl.BlockSpec((8, D_MODEL), lambda i: (0, 0))],
        out_shape=[jax.ShapeDtypeStruct((s, D_MODEL), F32), jax.ShapeDtypeStruct((8, 128), F32),
                   jax.ShapeDtypeStruct((8, D_MODEL), F32)],
        compiler_params=_params("arbitrary"), name=name)(x, gain, tgt)


def _as3d(a):
    return a.reshape((-1,) + a.shape[-2:])


def _ew_tile(r, c):
    if r % 256 == 0 or r <= 256:
        return _row_tile(r, 256), c
    return r, 128


def _sum_cast(terms, out_dtype, name):
    shape = terms[0].shape
    t3 = [_as3d(t) for t in terms]
    b, r, c = t3[0].shape
    tr, tc = _ew_tile(r, c)
    nc = c // tc

    def body(*refs):
        acc = refs[0][...].astype(F32)
        for t in refs[1:-1]:
            acc = acc + t[...].astype(F32)
        refs[-1][...] = acc.astype(out_dtype)

    spec = pl.BlockSpec((1, tr, tc), lambda i, j: (i, j // nc, j % nc))
    out = _pcall(body, grid=(b, (r // tr) * nc), in_specs=[spec] * len(t3), out_specs=spec,
                 out_shape=jax.ShapeDtypeStruct((b, r, c), out_dtype), compiler_params=_params("parallel", "parallel"),
                 name=name)(*t3)
    return out.reshape(shape)


def _split_axis(rows):
    return 0 if rows % 64 == 0 else 1


def _half_of(ref, which, lead=()):
    rows, cols = ref.shape[-2], ref.shape[-1]
    if _split_axis(rows) == 0:
        return ref.at[(*lead, pl.ds(which * (rows // 2), rows // 2))]
    return ref.at[(*lead, slice(None), pl.ds(which * (cols // 2), cols // 2))]


def _half_shape(shape):
    rows, cols = shape[-2], shape[-1]
    return (*shape[:-2], rows // 2, cols) if _split_axis(rows) == 0 else (*shape[:-2], rows, cols // 2)


def _pair_sum(a, b, sel, half_id, out_dtype, name):
    _, h, c = b.shape
    k = sel.shape[0]
    by_rows = _split_axis(a.shape[1]) == 0
    tr, tc = _ew_tile(h, c)
    nr, nc = h // tr, c // tc

    def body(sel_ref, hid_ref, a_ref, b_ref, o_ref):
        o_ref[...] = (a_ref[...] + b_ref[...]).astype(out_dtype)

    def a_map(q, j, sel_ref, hid_ref):
        if by_rows:
            return sel_ref[q], hid_ref[0] * nr + j // nc, j % nc
        return sel_ref[q], j // nc, hid_ref[0] * nc + j % nc

    blkshape = (1, tr, tc)
    grid_spec = pltpu.PrefetchScalarGridSpec(
        num_scalar_prefetch=2, grid=(k, nr * nc),
        in_specs=[pl.BlockSpec(blkshape, a_map),
                  pl.BlockSpec(blkshape, lambda q, j, sel_ref, hid_ref: (sel_ref[q], j // nc, j % nc))],
        out_specs=pl.BlockSpec(blkshape, lambda q, j, sel_ref, hid_ref: (q, j // nc, j % nc)))
    return _pcall(body, grid_spec=grid_spec, out_shape=jax.ShapeDtypeStruct((k, h, c), out_dtype),
                  compiler_params=_params("parallel", "parallel"), name=name)(sel, half_id, a, b)


def _total_into_half(terms, full_shape, half_id, name):
    rows, cols = full_shape
    h, c = terms[0].shape
    by_rows = _split_axis(rows) == 0
    tr, tc = _ew_tile(h, c)
    nr, nc = h // tr, c // tc

    def body(hid_ref, *refs):
        acc = refs[0][...].astype(F32)
        for t in refs[1:-1]:
            acc = acc + t[...].astype(F32)
        refs[-1][...] = acc

    def out_map(j, hid_ref):
        if by_rows:
            return hid_ref[0] * nr + j // nc, j % nc
        return j // nc, hid_ref[0] * nc + j % nc

    spec = pl.BlockSpec((tr, tc), lambda j, hid_ref: (j // nc, j % nc))
    grid_spec = pltpu.PrefetchScalarGridSpec(num_scalar_prefetch=1, grid=(nr * nc,), in_specs=[spec] * len(terms),
                                             out_specs=pl.BlockSpec((tr, tc), out_map))
    return _pcall(body, grid_spec=grid_spec, out_shape=jax.ShapeDtypeStruct((rows, cols), F32),
                  compiler_params=_params("parallel"), name=name)(half_id, *terms)


def _adamw(w, g, m, v, name):
    shape = w.shape
    w3, g3, m3, v3 = _as3d(w), _as3d(g), _as3d(m), _as3d(v)
    b, r, c = w3.shape
    tr, tc = _ew_tile(r, c)

    def body(w_ref, g_ref, m_ref, v_ref, d_ref, mo_ref, vo_ref):
        gg = g_ref[...]
        mn = ADAM_B1 * m_ref[...] + (1.0 - ADAM_B1) * gg
        vn = ADAM_B2 * v_ref[...] + (1.0 - ADAM_B2) * jnp.square(gg)
        m_hat = mn / (1.0 - ADAM_B1 ** ADAM_STEP)
        v_hat = vn / (1.0 - ADAM_B2 ** ADAM_STEP)
        d_ref[...] = -ADAM_LR * (m_hat / (jnp.sqrt(v_hat) + ADAM_EPS) + ADAM_WD * w_ref[...])
        mo_ref[...] = mn
        vo_ref[...] = vn

    spec = pl.BlockSpec((1, tr, tc), lambda i, j: (i, j // (c // tc), j % (c // tc)))
    sd = jax.ShapeDtypeStruct((b, r, c), F32)
    d, mo, vo = _pcall(body, grid=(b, (r // tr) * (c // tc)), in_specs=[spec] * 4, out_specs=[spec] * 3,
                       out_shape=[sd] * 3, compiler_params=_params("parallel", "parallel"), name=name)(w3, g3, m3, v3)
    return d.reshape(shape), mo.reshape(shape), vo.reshape(shape)


ANY = pl.BlockSpec(memory_space=pl.ANY)


def _place():
    return lax.axis_index("x"), lax.axis_index("y"), lax.axis_index("c")


def _flip(x, y, r):
    return (1 - x if r & 2 else x), (1 - y if r & 1 else y)


def _dma_sems(*counts):
    return [pltpu.SemaphoreType.DMA((k,)) for k in counts]


def _gather_ici(shards, rows=None, into=None):
    n = len(shards)
    assert rows is None or all(_split_axis(a.shape[0]) == 1 for a in shards)

    def piece(ref):
        return ref if rows is None else ref.at[pl.ds(rows[0], rows[1])]

    def copies(ins, outs, sems, incoming):
        send, recv = sems
        x, y, c = _place()
        out = []
        for r in (1, 2, 3):
            cx, cy = _flip(x, y, r)
            for a in range(n):
                block = 2 * cx + cy if incoming else 2 * x + y
                if rows is None:
                    src, dst = _half_of(ins[a], c), _half_of(outs[a], c, (block,))
                else:
                    hc = ins[a].shape[1] // 2
                    src = piece(ins[a]).at[:, pl.ds(c * hc, hc)]
                    dst = piece(outs[a].at[block]).at[:, pl.ds(c * hc, hc)]
                out.append(pltpu.make_async_remote_copy(
                    src_ref=src, dst_ref=dst, send_sem=send.at[(r - 1) * n + a], recv_sem=recv.at[(r - 1) * n + a],
                    device_id=(cx, cy, c), device_id_type=MESH))
        return out

    def start(ins, outs, sems):
        for cp in copies(ins, outs, sems, False):
            cp.start()

    def finish(ins, outs, sems):
        for cp in copies(ins, outs, sems, True):
            cp.wait_recv()
        for cp in copies(ins, outs, sems, False):
            cp.wait_send()

    return _Comm(list(shards) + list(into or []), [jax.ShapeDtypeStruct((4,) + a.shape, a.dtype) for a in shards],
                 _dma_sems(3 * n, 3 * n), start, finish, aliases={n + a: a for a in range(n)} if into else None)


def _gather_d2d(bufs, shards):
    n = len(bufs)

    def copies(ins, outs, sems, incoming):
        send, recv = sems
        x, y, c = _place()
        out = []
        for r in (1, 2, 3):
            cx, cy = _flip(x, y, r)
            for a in range(n):
                ref = _half_of(outs[a], (1 - c) if incoming else c, (2 * cx + cy,))
                out.append(pltpu.make_async_remote_copy(
                    src_ref=ref, dst_ref=ref, send_sem=send.at[(r - 1) * n + a], recv_sem=recv.at[(r - 1) * n + a],
                    device_id=(x, y, 1 - c), device_id_type=MESH))
        for a in range(n):
            out.append(pltpu.make_async_remote_copy(
                src_ref=ins[n + a], dst_ref=outs[a].at[2 * x + y], send_sem=send.at[3 * n + a],
                recv_sem=recv.at[3 * n + a], device_id=(x, y, 1 - c), device_id_type=MESH))
        return out

    def start(ins, outs, sems):
        for cp in copies(ins, outs, sems, False):
            cp.start()

    def finish(ins, outs, sems):
        for cp in copies(ins, outs, sems, True):
            cp.wait_recv()
        for cp in copies(ins, outs, sems, False):
            cp.wait_send()

    return _Comm(list(bufs) + list(shards), [jax.ShapeDtypeStruct(a.shape, a.dtype) for a in bufs],
                 _dma_sems(4 * n, 4 * n), start, finish, aliases={a: a for a in range(n)})


def _swap_rows(packs):
    n = len(packs)

    def copies(ins, outs, sems):
        send, recv = sems
        x, y, c = _place()
        return [pltpu.make_async_remote_copy(
            src_ref=_half_of(ins[a], 1 - c, (slice(None),)), dst_ref=outs[a], send_sem=send.at[a],
            recv_sem=recv.at[a], device_id=(x, y, 1 - c), device_id_type=MESH) for a in range(n)]

    def start(ins, outs, sems):
        for cp in copies(ins, outs, sems):
            cp.start()

    def finish(ins, outs, sems):
        for cp in copies(ins, outs, sems):
            cp.wait()

    return _Comm(packs, [jax.ShapeDtypeStruct(_half_shape(a.shape), a.dtype) for a in packs], _dma_sems(n, n),
                 start, finish)


def _exchange(arrs):
    n = len(arrs)

    def copies(ins, outs, sems):
        send, recv = sems
        x, y, c = _place()
        out = []
        for r in (1, 2, 3):
            cx, cy = _flip(x, y, r)
            for a in range(n):
                out.append(pltpu.make_async_remote_copy(
                    src_ref=ins[a].at[r - 1], dst_ref=outs[a].at[r - 1], send_sem=send.at[(r - 1) * n + a],
                    recv_sem=recv.at[(r - 1) * n + a], device_id=(cx, cy, c), device_id_type=MESH))
        return out

    def start(ins, outs, sems):
        for cp in copies(ins, outs, sems):
            cp.start()

    def finish(ins, outs, sems):
        for cp in copies(ins, outs, sems):
            cp.wait()

    return _Comm(arrs, [jax.ShapeDtypeStruct(a.shape, a.dtype) for a in arrs], _dma_sems(3 * n, 3 * n), start, finish)


def _halves_to_sibling(arrs):
    n = len(arrs)

    def copies(outs, sems, incoming):
        send, recv = sems
        x, y, c = _place()
        out = []
        for a in range(n):
            ref = _half_of(outs[a], (1 - c) if incoming else c)
            out.append(pltpu.make_async_remote_copy(
                src_ref=ref, dst_ref=ref, send_sem=send.at[a], recv_sem=recv.at[a], device_id=(x, y, 1 - c),
                device_id_type=MESH))
        return out

    def start(ins, outs, sems):
        for cp in copies(outs, sems, False):
            cp.start()

    def finish(ins, outs, sems):
        for cp in copies(outs, sems, True):
            cp.wait_recv()
        for cp in copies(outs, sems, False):
            cp.wait_send()

    return _Comm(arrs, [jax.ShapeDtypeStruct(a.shape, a.dtype) for a in arrs], _dma_sems(n, n), start, finish,
                 aliases={a: a for a in range(n)})


def _gather8(v, reduce, name):
    rows, w = v.shape

    def body(v_ref, out_ref, buf, send_sems, recv_sems):
        x, y, c = _place()
        me, sibling = (x, y, c), (x, y, 1 - c)
        chips = [_flip(x, y, r) for r in (1, 2, 3)]
        dst = out_ref if not reduce else buf

        def slot(px, py, pc):
            return dst.at[4 * px + 2 * py + pc]

        def copy(k, block, to, src=None):
            return pltpu.make_async_remote_copy(
                src_ref=slot(*block) if src is None else src, dst_ref=slot(*block), send_sem=send_sems.at[k],
                recv_sem=recv_sems.at[k], device_id=to, device_id_type=MESH)

        dst[4 * x + 2 * y + c] = v_ref[...]
        first = [copy(0, me, sibling, src=v_ref)]
        first += [copy(1 + j, me, (*chip, c), src=v_ref) for j, chip in enumerate(chips)]
        for cp in first:
            cp.start()
        passed = [copy(4 + j, (*chip, c), sibling) for j, chip in enumerate(chips)]
        for j, chip in enumerate(chips):
            copy(1 + j, (*chip, c), me).wait_recv()
            passed[j].start()
        copy(0, sibling, me).wait_recv()
        for j, chip in enumerate(chips):
            copy(4 + j, (*chip, 1 - c), me).wait_recv()
        for cp in first + passed:
            cp.wait_send()
        if reduce:
            acc = buf[0]
            for d in range(1, 8):
                acc = acc + buf[d]
            out_ref[...] = acc

    vm = pl.BlockSpec(memory_space=pltpu.VMEM)
    scratch = [pltpu.VMEM((8, rows, w) if reduce else (8, 8, 128), F32), pltpu.SemaphoreType.DMA((7,)),
               pltpu.SemaphoreType.DMA((7,))]
    out_shape = jax.ShapeDtypeStruct((rows, w) if reduce else (8, rows, w), F32)
    return _pcall(body, in_specs=[vm], out_specs=vm, out_shape=out_shape, scratch_shapes=scratch,
                  compiler_params=pltpu.CompilerParams(vmem_limit_bytes=VMEM_LIMIT), name=name)(v)


SMALL = [("norm_mix", 1024), ("b_gate", 2048), ("conv_b", 4096), ("dt_bias", 32), ("a_log", 32), ("d_skip", 32),
         ("ssm_norm", 2048), ("norm_xa", 1024), ("norm_mem", 1024), ("norm_mlp", 1024)]


def _rows_of(width):
    return max(1, width // 1024)


def _pack_rows(pieces):
    out = []
    for p in pieces:
        p = p.astype(F32)
        if p.shape[-1] < 1024:
            p = jnp.pad(p, ((0, 0), (0, 1024 - p.shape[-1])))
        out.append(p.reshape(-1, 1024))
    cat = jnp.concatenate(out, axis=0)
    pad = (-cat.shape[0]) % 8
    return jnp.pad(cat, ((0, pad), (0, 0))) if pad else cat


def _unpack_rows(packed, widths_rows):
    out, at = [], 0
    for r, w in widths_rows:
        k = r * _rows_of(w)
        p = packed[at:at + k]
        at += k
        out.append(p[:, :w] if w < 1024 else p.reshape(r, w))
    return out


def _to_cat(wt):
    pieces = [wt[O_XBC:O_XBC + 4096], wt[O_Q:O_Q + 512], wt[O_K:O_K + 512], wt[O_DT:O_DT + 32],
              jnp.zeros((DT_PAD - 32, wt.shape[1]), wt.dtype), wt[O_Z:O_Z + 2048], wt[O_GATES:O_GATES + 2048],
              wt[O_V:O_V + 1024], wt[O_G:O_G + 1024]]
    return jnp.concatenate(pieces, axis=0)


def _from_cat(gt):
    pieces = [gt[C_Q:C_Q + 512], gt[C_K:C_K + 512], gt[C_V:C_V + 1024], gt[C_G:C_G + 1024],
              gt[C_Z:C_Z + 2048], gt[C_XBC:C_XBC + 4096], gt[C_DT:C_DT + 32], gt[C_GATES:C_GATES + 2048]]
    return jnp.concatenate(pieces, axis=0)


PACK_ROWS = [("mlp_w1", 1024), ("mlp_w2", 1024), ("w_br_ssm", 512), ("w_br_ret", 256), ("w_out", 256), ("xa_wq", 256),
             ("xa_wo", 256)]
PACK_N = sum(r for _, r in PACK_ROWS)


def kernel(x, mem, positions, norm_mix, w_in, b_gate, conv_w, conv_b, dt_bias, a_log, d_skip, ssm_norm, w_br_ret, w_br_ssm, w_out, norm_xa, norm_mem, xa_wq, xa_wkv, xa_wo, norm_mlp, mlp_w1, mlp_w2, norm_final, loss_target, m_norm_mix, m_w_in, m_b_gate, m_conv_w, m_conv_b, m_dt_bias, m_a_log, m_d_skip, m_ssm_norm, m_w_br_ret, m_w_br_ssm, m_w_out, m_norm_xa, m_norm_mem, m_xa_wq, m_xa_wkv, m_xa_wo, m_norm_mlp, m_mlp_w1, m_mlp_w2, m_norm_final, v_norm_mix, v_w_in, v_b_gate, v_conv_w, v_conv_b, v_dt_bias, v_a_log, v_d_skip, v_ssm_norm, v_w_br_ret, v_w_br_ssm, v_w_out, v_norm_xa, v_norm_mem, v_xa_wq, v_xa_wkv, v_xa_wo, v_norm_mlp, v_mlp_w1, v_mlp_w2, v_norm_final):
    W = dict(norm_mix=norm_mix, w_in=w_in, b_gate=b_gate, conv_w=conv_w, conv_b=conv_b, dt_bias=dt_bias, a_log=a_log,
             d_skip=d_skip, ssm_norm=ssm_norm, w_br_ret=w_br_ret, w_br_ssm=w_br_ssm, w_out=w_out, norm_xa=norm_xa,
             norm_mem=norm_mem, xa_wq=xa_wq, xa_wkv=xa_wkv, xa_wo=xa_wo, norm_mlp=norm_mlp, mlp_w1=mlp_w1,
             mlp_w2=mlp_w2, norm_final=norm_final)
    M = dict(norm_mix=m_norm_mix, w_in=m_w_in, b_gate=m_b_gate, conv_w=m_conv_w, conv_b=m_conv_b, dt_bias=m_dt_bias,
             a_log=m_a_log, d_skip=m_d_skip, ssm_norm=m_ssm_norm, w_br_ret=m_w_br_ret, w_br_ssm=m_w_br_ssm,
             w_out=m_w_out, norm_xa=m_norm_xa, norm_mem=m_norm_mem, xa_wq=m_xa_wq, xa_wkv=m_xa_wkv, xa_wo=m_xa_wo,
             norm_mlp=m_norm_mlp, mlp_w1=m_mlp_w1, mlp_w2=m_mlp_w2, norm_final=m_norm_final)
    V = dict(norm_mix=v_norm_mix, w_in=v_w_in, b_gate=v_b_gate, conv_w=v_conv_w, conv_b=v_conv_b, dt_bias=v_dt_bias,
             a_log=v_a_log, d_skip=v_d_skip, ssm_norm=v_ssm_norm, w_br_ret=v_w_br_ret, w_br_ssm=v_w_br_ssm,
             w_out=v_w_out, norm_xa=v_norm_xa, norm_mem=v_norm_mem, xa_wq=v_xa_wq, xa_wkv=v_xa_wkv, xa_wo=v_xa_wo,
             norm_mlp=v_norm_mlp, mlp_w1=v_mlp_w1, mlp_w2=v_mlp_w2, norm_final=v_norm_final)
    nl = w_in.shape[0]
    s = x.shape[1]
    x0 = x[0]
    mem2 = mem[0]
    tgt = loss_target[0]
    blk = 2 * lax.axis_index("x") + lax.axis_index("y")

    grp_b = ["xa_wkv", "w_br_ret", "w_br_ssm", "w_out", "xa_wq", "xa_wo"]
    grp_c = ["mlp_w1", "mlp_w2"]
    win = ["w_in"]
    blk = blk.astype(jnp.int32)
    tr_ = lambda a: jnp.swapaxes(a, 1, 2)
    wb = {k: W[k].astype(BF16) for k in grp_b + grp_c}
    wb["w_in"] = tr_(w_in).astype(BF16)

    def shards_of(l, ks):
        return [wb[k][l] for k in ks]

    landed = _run_comm(_gather_ici(shards_of(0, win)), "gather_w_in")
    g_win = _run_comm(_gather_d2d(landed, shards_of(0, win)), "gather_w_in_cores")
    cw_all = _gather8(conv_w.reshape(nl * SSM_CONV, 1024), False, "gather_conv_w")
    cw_full = cw_all.reshape(4, 2, nl, SSM_CONV, 1024)[:, 0].transpose(1, 2, 0, 3).reshape(nl, SSM_CONV, SSM_CONV_DIM)

    offs = {}
    at = 0
    for k, r in PACK_ROWS:
        offs[k] = (at, r)
        at += r


    inv_freq = ROPE_THETA ** (-jnp.arange(0, RET_QK_DIM, 2, dtype=F32) / RET_QK_DIM)
    ang = positions.astype(F32)[0][:, None] * inv_freq
    cos, sin = jnp.cos(ang), jnp.sin(ang)
    cosf = jnp.concatenate([cos, cos], axis=1)
    sinf = jnp.concatenate([-sin, sin], axis=1)
    dm, qd, kd, cd = (jnp.asarray(c) for c in _ret_constants())
    eye, blkm = (jnp.asarray(c) for c in _ssd_constants())
    consts = (dm, qd, kd, cd, eye, blkm)
    e_np = _head_expand()
    eexp = jnp.asarray(e_np, BF16)
    eexp_t = jnp.asarray(e_np.T.copy(), BF16)

    saved = []
    xcur = x0
    for l in range(nl):
        wcat = _to_cat(g_win[0].reshape(IN_DIM, D_MODEL))
        cw, cb = cw_full[l], conv_b[l][None]
        dtb = jnp.pad(dt_bias[l], (0, 128 - SSM_HEADS))[None]
        ax = jnp.repeat(-jnp.exp(a_log[l]), 64)[None]
        dsk = jnp.repeat(d_skip[l], 64)[None]
        bg, sn = b_gate[l][None], ssm_norm[l][None]
        more = l + 1 < nl
        ici_b, ici_c = _gather_ici(shards_of(l, grp_b)), _gather_ici(shards_of(l, grp_c))
        _carry(ici_b)
        proj, u = _nmm(xcur, norm_mix[l][None], wcat, "in_proj", save_u=True, w_rows=True)
        _carry(ici_c)
        qr, kr, xc, dtx = _prescan(proj, cosf, sinf, cw, cb, dtb, eexp, "prescan")
        cores = _gather_d2d(ici_b.results + ici_c.results, shards_of(l, grp_b + grp_c))
        _carry(cores)
        yr, ys, sst, hst = _scan_fwd(qr, kr, proj, xc, dtx, ax, consts, "scan_fwd")
        gl = dict(zip(grp_b + grp_c, cores.results))
        rows_weight = lambda k: gl[k].reshape(-1, D_MODEL)
        wr, ws, wo = rows_weight("w_br_ret"), rows_weight("w_br_ssm"), rows_weight("w_out")
        wq, wxo, w2 = rows_weight("xa_wq"), rows_weight("xa_wo"), rows_weight("mlp_w2")
        w1, wkv = gl["mlp_w1"], gl["xa_wkv"]
        first = _gather_ici(shards_of(l + 1, win), rows=(0, WIN_SPLIT)) if more else None
        _carry(first)
        x1 = _postscan_fwd(xcur, yr, ys, xc, proj, bg, dsk, sn, wr, ws, wo, "postscan")
        kv = _bf(_nmm(mem2, norm_mem[l][None], wkv, "mem_kv"))
        rest = (_gather_ici(shards_of(l + 1, win), rows=(WIN_SPLIT, IN_DIM // 4 - WIN_SPLIT), into=first.results)
                if more else None)
        _carry(rest)
        x2 = _xattn_fwd(x1, norm_xa[l][None], wq, kv, wxo, "xattn")
        cores = _gather_d2d(rest.results, shards_of(l + 1, win)) if more else None
        _carry(cores)
        x3 = _mlp_fwd(x2, norm_mlp[l][None], w1, w2, "mlp")
        if more:
            g_win = cores.results
        saved.append(dict(x0=xcur, x1=x1, x2=x2, proj=proj, u=u, qr=qr, kr=kr, xc=xc, dtx=dtx, yr=yr, ys=ys, sst=sst,
                          hst=hst, kv=kv, wcat=wcat, wr=wr, ws=ws, wo=wo, wq=wq, wxo=wxo, w1=w1, w2=w2, wkv=wkv, cw=cw,
                          cb=cb, dtb=dtb, ax=ax, dsk=dsk, bg=bg, sn=sn))
        xcur = x3

    dx, loss_acc, dnf = _final(xcur, norm_final[None], tgt, "final")
    loss = lax.psum(loss_acc[0, 0], ("x", "y", "c"))

    small_g = [None] * nl
    c = lax.axis_index("c")
    half_id = c.astype(jnp.int32)[None]
    sel_own = blk[None]
    sel_rem = jnp.stack([blk ^ 1, blk ^ 2, blk ^ 3])
    layer_grads = {k: [None] * nl for k in ("pack", "w_in", "xa_wkv")}

    def pair_sums(packs, got):
        own = [_pair_sum(p, g_, sel_own, half_id, F32, "chip_sum_own")[0] for p, g_ in zip(packs, got)]
        out_b = [_pair_sum(p, g_, sel_rem, half_id, BF16, "chip_sum_send") for p, g_ in zip(packs, got)]
        return own, out_b

    full_shapes = [(PACK_N, D_MODEL), (IN_DIM // 4, D_MODEL), (D_MODEL, 2 * D_MODEL // 4)]

    def totals(own, inc):
        return [_total_into_half([o, i_[0], i_[1], i_[2]], fs, half_id, "grads_total")
                for o, i_, fs in zip(own, inc, full_shapes)]

    def finish_layer(lr, whole):
        layer_grads["pack"][lr], layer_grads["w_in"][lr], layer_grads["xa_wkv"][lr] = whole

    riding = None
    for l in reversed(range(nl)):
        sv = saved[l]
        swap = _swap_rows(riding[1]) if riding else None
        _carry(swap)
        dx2, hm, rm, dam, dg_mlp = _mlp_bwd(sv["x2"], dx, norm_mlp[l][None], sv["w1"], sv["w2"], "mlp_bwd")
        if riding:
            own, out_b = pair_sums(riding[1], swap.results)
        pack = _mm_tn_into(hm, dam, "dw_mlp1", None, offs["mlp_w1"][0], True)
        pack = _mm_tn_into(rm, dx, "dw_mlp2", pack, offs["mlp_w2"][0], False)
        dx1, hx, dqx, ox, dkv, dg_xa = _xattn_bwd(sv["x1"], dx2, norm_xa[l][None], sv["wq"], sv["kv"], sv["wxo"],
                                                  "xattn_bwd")
        pack = _mm_tn_into(hx, dqx, "dw_xq", pack, offs["xa_wq"][0], False)
        pack = _mm_tn_into(ox, dx2, "dw_xo", pack, offs["xa_wo"][0], False)
        memn, dg_mem = _mem_bwd(mem2, norm_mem[l][None], dkv, sv["wkv"], "mem_bwd")
        dwkv = _mm_tn(memn, dkv, "dw_xkv", col_blocks=4)
        chips_a = _exchange(out_b[0:1]) if riding else None
        _carry(chips_a)
        (dyr, dys, dxs_skip, dproj, yrn, ysn, mg, dbr, dbs, dbg, ddsk, dsn) = _postscan_bwd(
            dx1, sv["yr"], sv["ys"], sv["xc"], sv["proj"], sv["bg"], sv["dsk"], sv["sn"], sv["wr"], sv["ws"], sv["wo"],
            "postscan_bwd")
        pack = _mm_tn_into(mg, dx1, "dw_out", pack, offs["w_out"][0], False)
        pack = _mm_tn_into(yrn, dbr, "dw_br_ret", pack, offs["w_br_ret"][0], False)
        pack = _mm_tn_into(ysn, dbs, "dw_br_ssm", pack, offs["w_br_ssm"][0], False)
        early = [pack, dwkv] if l == 0 else None
        swap_e = _swap_rows(early) if early else None
        chips_b = _exchange(out_b[1:3]) if riding else None
        _carry(_both(chips_b, swap_e))
        dqr, dkr, dproj, dxc, gdtx, da_cols = _scan_bwd(sv["qr"], sv["kr"], sv["proj"], sv["xc"], sv["dtx"], sv["ax"],
                                                        consts, sv["sst"], sv["hst"], dyr, dys, dproj, "scan_bwd")
        if riding:
            red_half = totals(own, chips_a.results + chips_b.results)
        if early:
            own_e, out_e = pair_sums(early, swap_e.results)
        cores = _halves_to_sibling(red_half) if riding else None
        chips_e = _exchange(out_e[0:1]) if early else None
        _carry(_both(cores, chips_e))
        dproj, dcw, dcb, ddtb = _prescan_bwd(sv["proj"], dxc, dxs_skip, gdtx, dqr, dkr, cosf, sinf, sv["cw"], sv["cb"],
                                             sv["dtb"], eexp_t, dproj, "prescan_bwd")
        if riding:
            finish_layer(riding[0], cores.results)
        chips_kv = _exchange(out_e[1:2]) if early else None
        _carry(chips_kv)
        dwcat = _mm_tn(dproj, sv["u"], "dw_in")
        dx, dg_mix = _in_bwd(dproj, sv["wcat"], sv["x0"], norm_mix[l][None], dx1, "in_bwd")

        da_log = (da_cols.reshape(SSM_HEADS, 64).sum(axis=1)) * (-jnp.exp(a_log[l]))
        dd_skip = ddsk[0].reshape(SSM_HEADS, 64).sum(axis=1)
        small_g[l] = [dg_mix[0:1], dbg[0:1], dcb[0:1], ddtb[0:1, :SSM_HEADS], da_log[None], dd_skip[None], dsn[0:1],
                      dg_xa[0:1], dg_mem[0:1], dg_mlp[0:1], dcw[0::8]]
        riding = (l, [pack, _from_cat(dwcat).reshape(4, IN_DIM // 4, D_MODEL), dwkv])

    w_in_grad = riding[1][1:2]
    own_l, out_l = pair_sums(w_in_grad, _run_comm(_swap_rows(w_in_grad), "grads_core_swap"))
    inc_l = _run_comm(_exchange(out_l), "grads_chip_exchange")
    red_half = totals([own_e[0], own_l[0], own_e[1]], [chips_e.results[0], inc_l[0], chips_kv.results[0]])
    finish_layer(0, _run_comm(_halves_to_sibling(red_half), "grads_core_join"))
    grad_x = dx[None]

    pieces = []
    for l in range(nl):
        pieces += small_g[l]
    pieces.append(dnf[0:1])
    small_sum = _gather8(_pack_rows(pieces), True, "reduce_small")
    layout = []
    for l in range(nl):
        layout += [(1, w) for _, w in SMALL] + [(SSM_CONV, SSM_CONV_DIM)]
    layout.append((1, 1024))
    red = _unpack_rows(small_sum, layout)
    per = len(SMALL) + 1
    g_small = {k: jnp.concatenate([red[l * per + i] for l in range(nl)], axis=0) for i, (k, _) in enumerate(SMALL)}
    g_convw_full = jnp.stack([red[l * per + len(SMALL)] for l in range(nl)])
    g_small["conv_w"] = lax.dynamic_slice_in_dim(g_convw_full, blk * 1024, 1024, axis=2)
    g_small["norm_final"] = red[-1][0]

    grads = dict(g_small)
    pack_all = jnp.stack(layer_grads["pack"])
    for k, r in PACK_ROWS:
        grads[k] = pack_all[:, offs[k][0]:offs[k][0] + r]
    grads["w_in"] = jnp.stack(layer_grads["w_in"])
    grads["xa_wkv"] = jnp.stack(layer_grads["xa_wkv"])

    delta, new_m, new_v = {}, {}, {}
    for k in ["xa_wkv"] + [k for k, _ in PACK_ROWS]:
        delta[k], new_m[k], new_v[k] = _adamw(W[k], grads[k], M[k], V[k], "adamw_" + k)
    g_in_t = grads["w_in"]
    grads["w_in"] = tr_(g_in_t)
    d_t, m_t, v_t = _adamw(tr_(w_in), g_in_t, tr_(m_w_in), tr_(v_w_in), "adamw_w_in")
    delta["w_in"], new_m["w_in"], new_v["w_in"] = tr_(d_t), tr_(m_t), tr_(v_t)
    small_names = [k for k, _ in SMALL] + ["conv_w", "norm_final"]

    def pack_small(src):
        ps = []
        for k in small_names:
            a = src[k]
            ps.append(a.reshape(-1, a.shape[-1]) if a.ndim > 1 else a[None])
        return _pack_rows(ps)

    ds_, ms_, vs_ = _adamw(pack_small(W), pack_small(grads), pack_small(M), pack_small(V), "adamw_small")
    lay2 = []
    for k in small_names:
        a = W[k]
        lay2.append((int(np.prod(a.shape[:-1])) if a.ndim > 1 else 1, a.shape[-1]))
    for src, dst in ((ds_, delta), (ms_, new_m), (vs_, new_v)):
        for k, piece in zip(small_names, _unpack_rows(src, lay2)):
            dst[k] = piece.reshape(W[k].shape)

    names = ["norm_mix", "w_in", "b_gate", "conv_w", "conv_b", "dt_bias", "a_log", "d_skip", "ssm_norm", "w_br_ret",
             "w_br_ssm", "w_out", "norm_xa", "norm_mem", "xa_wq", "xa_wkv", "xa_wo", "norm_mlp", "mlp_w1", "mlp_w2",
             "norm_final"]
    return (loss, grad_x, *[grads[n] for n in names], *[delta[n] for n in names], *[new_m[n] for n in names],
            *[new_v[n] for n in names])
```
